```python
import jax, jax.numpy as jnp
from jax import lax
import numpy as np

D_MODEL = 2048
BATCH = 8
SEQ = 4096
DEPTH = 2

N_META = 16
BLOCK = 128
PAD = BLOCK - N_META
EPS = 1e-6
MASK_VALUE = -1e30

FOX_HEADS = 8
FOX_HEAD_DIM = D_MODEL // 16
FOX_WIDTH = FOX_HEADS * FOX_HEAD_DIM

GLA_HEADS = 4
GLA_KEY_WIDTH = D_MODEL // 2
GLA_VAL_WIDTH = D_MODEL
GLA_DK = GLA_KEY_WIDTH // GLA_HEADS
GLA_DV = GLA_VAL_WIDTH // GLA_HEADS
GLA_RANK = 16
GLA_TAU = 16.0
GLA_CHUNK = 64

D_FF = 4 * D_MODEL

SPLITS = (FOX_WIDTH, FOX_WIDTH, FOX_WIDTH, FOX_HEADS,
          GLA_KEY_WIDTH, GLA_KEY_WIDTH, GLA_VAL_WIDTH, GLA_VAL_WIDTH, GLA_RANK,
          D_MODEL, D_MODEL)
D_IN_PROJ = 3 * FOX_WIDTH + FOX_HEADS + 2 * GLA_KEY_WIDTH + 2 * GLA_VAL_WIDTH + GLA_RANK + 2 * D_MODEL

kernel_name = "fox_gla_gated_hybrid_block"


def rmsnorm(x, g):
    xf = x.astype(jnp.float32)
    y = xf * lax.rsqrt(jnp.mean(xf * xf, axis=-1, keepdims=True) + EPS)
    return (y * g.astype(jnp.float32)).astype(x.dtype)


def split_heads(t, h):
    b, l, _ = t.shape
    return t.reshape(b, l, h, -1).transpose(0, 2, 1, 3)


def forgetting_attention(q, k, v, f_logit, b_forget):
    B, L, _ = q.shape
    Lp = L + PAD
    pad4 = ((0, 0), (0, 0), (PAD, 0), (0, 0))
    qh = jnp.pad(split_heads(q, FOX_HEADS), pad4)
    kh = jnp.pad(split_heads(k, FOX_HEADS), pad4)
    vh = jnp.pad(split_heads(v, FOX_HEADS), pad4)
    log_f = jax.nn.log_sigmoid((f_logit + b_forget).astype(jnp.float32))
    log_f = jnp.pad(log_f.transpose(0, 2, 1), ((0, 0), (0, 0), (PAD, 0)))
    c = jnp.cumsum(log_f, axis=-1)
    scale = FOX_HEAD_DIM ** -0.5
    pos = np.arange(Lp)
    outs = []
    for i in range(Lp // BLOCK):
        q0, q1 = i * BLOCK, (i + 1) * BLOCK
        s = jnp.einsum('bhqd,bhkd->bhqk', qh[:, :, q0:q1], kh[:, :, :q1],
                       preferred_element_type=jnp.float32) * scale
        s = s + c[:, :, q0:q1, None] - c[:, :, None, :q1]
        mask = (pos[None, :q1] <= pos[q0:q1, None]) & (pos[None, :q1] >= PAD)
        s = jnp.where(mask, s, MASK_VALUE)
        p = jax.nn.softmax(s, axis=-1)
        outs.append(jnp.einsum('bhqk,bhkd->bhqd', p.astype(vh.dtype), vh[:, :, :q1]))
    o = jnp.concatenate(outs, axis=2)[:, :, PAD:]
    return o.transpose(0, 2, 1, 3).reshape(B, L, FOX_WIDTH)


def gated_linear_attention(q, k, v, g):
    B, L, _ = q.shape
    Lp = L + PAD
    N, C = Lp // GLA_CHUNK, GLA_CHUNK

    def chunks(t):
        t = jnp.pad(t.astype(jnp.float32), ((0, 0), (PAD, 0), (0, 0)))
        return t.reshape(B, N, C, GLA_HEADS, -1).transpose(0, 3, 1, 2, 4)

    qc = chunks(q) * (GLA_DK ** -0.5)
    kc, vc, gc = chunks(k), chunks(v), chunks(g)
    b = jnp.cumsum(gc, axis=3)
    b_last = b[:, :, :, -1:]
    q_dec = qc * jnp.exp(b)
    k_inv = kc * jnp.exp(-b)
    k_end = kc * jnp.exp(b_last - b)
    causal = np.tril(np.ones((C, C), dtype=bool))
    a = jnp.where(causal, jnp.einsum('bhnck,bhnsk->bhncs', q_dec, k_inv), 0.0)
    o_intra = jnp.einsum('bhncs,bhnsv->bhncv', a, vc)

    def step(S, inp):
        qd, ke, vv, dl = inp
        o = jnp.einsum('bhck,bhkv->bhcv', qd, S)
        S = S * dl[..., None] + jnp.einsum('bhck,bhcv->bhkv', ke, vv)
        return S, o

    S0 = jnp.zeros((B, GLA_HEADS, GLA_DK, GLA_DV), jnp.float32)
    xs = (jnp.moveaxis(q_dec, 2, 0), jnp.moveaxis(k_end, 2, 0), jnp.moveaxis(vc, 2, 0),
          jnp.moveaxis(jnp.exp(b_last[:, :, :, 0]), 2, 0))
    _, o_inter = lax.scan(step, S0, xs)
    o = o_intra + jnp.moveaxis(o_inter, 0, 2)
    return o.transpose(0, 2, 3, 1, 4).reshape(B, Lp, GLA_VAL_WIDTH)[:, PAD:]


def head_rmsnorm(o, g):
    B, L, _ = o.shape
    oh = o.reshape(B, L, GLA_HEADS, GLA_DV)
    oh = oh * lax.rsqrt(jnp.mean(oh * oh, axis=-1, keepdims=True) + EPS)
    return oh.reshape(B, L, GLA_VAL_WIDTH) * g.astype(jnp.float32)


def hybrid_mixer(xn, w_in, b_forget, w_alpha2, b_alpha, gla_norm_g, w_o_fox, w_o_gla, w_out):
    proj = xn @ w_in
    split_points = np.cumsum(SPLITS)[:-1].tolist()
    (fq, fk, fv, ff, gq, gk, gv, gr, ga, gate_fox, gate_gla) = jnp.split(proj, split_points, axis=-1)
    o_fox = forgetting_attention(fq, fk, fv, ff, b_forget)
    g_log = jax.nn.log_sigmoid((ga @ w_alpha2 + b_alpha).astype(jnp.float32)) / GLA_TAU
    o_gla = gated_linear_attention(gq, gk, gv, g_log)
    o_gla = (head_rmsnorm(o_gla, gla_norm_g) * jax.nn.silu(gr.astype(jnp.float32))).astype(xn.dtype)
    y = jax.nn.sigmoid(gate_fox) * (o_fox @ w_o_fox) + jax.nn.sigmoid(gate_gla) * (o_gla @ w_o_gla)
    return y @ w_out


def squared_relu_mlp(xn, w1, w2):
    return jnp.square(jax.nn.relu(xn @ w1)) @ w2


def _fwd_setup_inputs(seed: int = 0) -> dict:
    key = jax.random.key(seed)
    ks = jax.random.split(key, 16)
    nrm = lambda k, shape, fan_in: jax.random.normal(k, shape, jnp.float32) * (fan_in ** -0.5)
    gain = lambda k, shape: 1.0 + 0.02 * jax.random.normal(k, shape, jnp.float32)
    return {
        "x": jax.random.normal(ks[0], (BATCH, SEQ, D_MODEL), jnp.float32),
        "meta_tokens": jax.random.normal(ks[1], (N_META, D_MODEL), jnp.float32),
        "norm_mix_g": gain(ks[2], (DEPTH, D_MODEL)),
        "w_in": nrm(ks[3], (DEPTH, D_MODEL, D_IN_PROJ), D_MODEL),
        "b_forget": jax.random.uniform(ks[4], (DEPTH, FOX_HEADS), jnp.float32, 1.0, 5.0),
        "w_alpha2": nrm(ks[5], (DEPTH, GLA_RANK, GLA_KEY_WIDTH), GLA_RANK),
        "b_alpha": 0.01 * jax.random.normal(ks[6], (DEPTH, GLA_KEY_WIDTH), jnp.float32),
        "gla_norm_g": gain(ks[7], (DEPTH, GLA_VAL_WIDTH)),
        "w_o_fox": nrm(ks[8], (DEPTH, FOX_WIDTH, D_MODEL), FOX_WIDTH),
        "w_o_gla": nrm(ks[9], (DEPTH, GLA_VAL_WIDTH, D_MODEL), GLA_VAL_WIDTH),
        "w_out": nrm(ks[10], (DEPTH, D_MODEL, D_MODEL), D_MODEL),
        "norm_mlp_g": gain(ks[11], (DEPTH, D_MODEL)),
        "w_ff1": nrm(ks[12], (DEPTH, D_MODEL, D_FF), D_MODEL),
        "w_ff2": nrm(ks[13], (DEPTH, D_FF, D_MODEL), D_FF),
        "final_norm_g": gain(ks[14], (D_MODEL,)),
    }


def _fwd_reference(x, meta_tokens, norm_mix_g, w_in, b_forget, w_alpha2, b_alpha, gla_norm_g,
              w_o_fox, w_o_gla, w_out, norm_mlp_g, w_ff1, w_ff2, final_norm_g):
    B = x.shape[0]
    meta = jnp.broadcast_to(meta_tokens[None].astype(x.dtype), (B, N_META, D_MODEL))
    h = jnp.concatenate([meta, x], axis=1)
    for l in range(DEPTH):
        h = h + hybrid_mixer(rmsnorm(h, norm_mix_g[l]), w_in[l], b_forget[l], w_alpha2[l],
                             b_alpha[l], gla_norm_g[l], w_o_fox[l], w_o_gla[l], w_out[l])
        h = h + squared_relu_mlp(rmsnorm(h, norm_mlp_g[l]), w_ff1[l], w_ff2[l])
    return rmsnorm(h, final_norm_g)[:, N_META:]


import jax as _jax
import jax.numpy as _jnp

TWIN_FORMAT = 'train_step'
FWD_PARAMS = ['x', 'meta_tokens', 'norm_mix_g', 'w_in', 'b_forget', 'w_alpha2', 'b_alpha', 'gla_norm_g', 'w_o_fox', 'w_o_gla', 'w_out', 'norm_mlp_g', 'w_ff1', 'w_ff2', 'final_norm_g']
TWIN_WEIGHTS = ['meta_tokens', 'norm_mix_g', 'w_in', 'b_forget', 'w_alpha2', 'b_alpha', 'gla_norm_g', 'w_o_fox', 'w_o_gla', 'w_out', 'norm_mlp_g', 'w_ff1', 'w_ff2', 'final_norm_g']
TWIN_DIFF_INPUT = 'x'
TWIN_INPUTS = ['x', 'meta_tokens', 'norm_mix_g', 'w_in', 'b_forget', 'w_alpha2', 'b_alpha', 'gla_norm_g', 'w_o_fox', 'w_o_gla', 'w_out', 'norm_mlp_g', 'w_ff1', 'w_ff2', 'final_norm_g', 'loss_target', 'm_meta_tokens', 'm_norm_mix_g', 'm_w_in', 'm_b_forget', 'm_w_alpha2', 'm_b_alpha', 'm_gla_norm_g', 'm_w_o_fox', 'm_w_o_gla', 'm_w_out', 'm_norm_mlp_g', 'm_w_ff1', 'm_w_ff2', 'm_final_norm_g', 'v_meta_tokens', 'v_norm_mix_g', 'v_w_in', 'v_b_forget', 'v_w_alpha2', 'v_b_alpha', 'v_gla_norm_g', 'v_w_o_fox', 'v_w_o_gla', 'v_w_out', 'v_norm_mlp_g', 'v_w_ff1', 'v_w_ff2', 'v_final_norm_g']
TWIN_OUTPUTS = ['loss', 'grad_x', 'grad_meta_tokens', 'grad_norm_mix_g', 'grad_w_in', 'grad_b_forget', 'grad_w_alpha2', 'grad_b_alpha', 'grad_gla_norm_g', 'grad_w_o_fox', 'grad_w_o_gla', 'grad_w_out', 'grad_norm_mlp_g', 'grad_w_ff1', 'grad_w_ff2', 'grad_final_norm_g', 'delta_meta_tokens', 'delta_norm_mix_g', 'delta_w_in', 'delta_b_forget', 'delta_w_alpha2', 'delta_b_alpha', 'delta_gla_norm_g', 'delta_w_o_fox', 'delta_w_o_gla', 'delta_w_out', 'delta_norm_mlp_g', 'delta_w_ff1', 'delta_w_ff2', 'delta_final_norm_g', 'new_m_meta_tokens', 'new_m_norm_mix_g', 'new_m_w_in', 'new_m_b_forget', 'new_m_w_alpha2', 'new_m_b_alpha', 'new_m_gla_norm_g', 'new_m_w_o_fox', 'new_m_w_o_gla', 'new_m_w_out', 'new_m_norm_mlp_g', 'new_m_w_ff1', 'new_m_w_ff2', 'new_m_final_norm_g', 'new_v_meta_tokens', 'new_v_norm_mix_g', 'new_v_w_in', 'new_v_b_forget', 'new_v_w_alpha2', 'new_v_b_alpha', 'new_v_gla_norm_g', 'new_v_w_o_fox', 'new_v_w_o_gla', 'new_v_w_out', 'new_v_norm_mlp_g', 'new_v_w_ff1', 'new_v_w_ff2', 'new_v_final_norm_g']
TWIN_LEAF_KINDS = {'loss': 'loss', 'grad_x': 'grad_x', 'grad_meta_tokens': 'grad_w', 'grad_norm_mix_g': 'grad_w', 'grad_w_in': 'grad_w', 'grad_b_forget': 'grad_w', 'grad_w_alpha2': 'grad_w', 'grad_b_alpha': 'grad_w', 'grad_gla_norm_g': 'grad_w', 'grad_w_o_fox': 'grad_w', 'grad_w_o_gla': 'grad_w', 'grad_w_out': 'grad_w', 'grad_norm_mlp_g': 'grad_w', 'grad_w_ff1': 'grad_w', 'grad_w_ff2': 'grad_w', 'grad_final_norm_g': 'grad_w', 'delta_meta_tokens': 'delta_w', 'delta_norm_mix_g': 'delta_w', 'delta_w_in': 'delta_w', 'delta_b_forget': 'delta_w', 'delta_w_alpha2': 'delta_w', 'delta_b_alpha': 'delta_w', 'delta_gla_norm_g': 'delta_w', 'delta_w_o_fox': 'delta_w', 'delta_w_o_gla': 'delta_w', 'delta_w_out': 'delta_w', 'delta_norm_mlp_g': 'delta_w', 'delta_w_ff1': 'delta_w', 'delta_w_ff2': 'delta_w', 'delta_final_norm_g': 'delta_w', 'new_m_meta_tokens': 'new_m', 'new_m_norm_mix_g': 'new_m', 'new_m_w_in': 'new_m', 'new_m_b_forget': 'new_m', 'new_m_w_alpha2': 'new_m', 'new_m_b_alpha': 'new_m', 'new_m_gla_norm_g': 'new_m', 'new_m_w_o_fox': 'new_m', 'new_m_w_o_gla': 'new_m', 'new_m_w_out': 'new_m', 'new_m_norm_mlp_g': 'new_m', 'new_m_w_ff1': 'new_m', 'new_m_w_ff2': 'new_m', 'new_m_final_norm_g': 'new_m', 'new_v_meta_tokens': 'new_v', 'new_v_norm_mix_g': 'new_v', 'new_v_w_in': 'new_v', 'new_v_b_forget': 'new_v', 'new_v_w_alpha2': 'new_v', 'new_v_b_alpha': 'new_v', 'new_v_gla_norm_g': 'new_v', 'new_v_w_o_fox': 'new_v', 'new_v_w_o_gla': 'new_v', 'new_v_w_out': 'new_v', 'new_v_norm_mlp_g': 'new_v', 'new_v_w_ff1': 'new_v', 'new_v_w_ff2': 'new_v', 'new_v_final_norm_g': 'new_v'}


def _forward(args):
    return _fwd_reference(*[args[k] for k in FWD_PARAMS])


def _output_shape():
    def fwd():
        inp = _fwd_setup_inputs(0)
        return _fwd_reference(*[inp[k] for k in FWD_PARAMS])
    out = _jax.eval_shape(fwd)
    return out.shape, out.dtype

N_MICROBATCH = 1
ADAM_LR = 0.001
ADAM_B1 = 0.9
ADAM_B2 = 0.999
ADAM_EPS = 1e-08
ADAM_WD = 0.01
ADAM_STEP = 10
PER_EXAMPLE_BATCH_AXIS = {'x': 0, 'loss_target': 0}
SHARED_INPUTS = []
_WEIGHT_DTYPES = {'meta_tokens': _jnp.float32, 'norm_mix_g': _jnp.float32, 'w_in': _jnp.float32, 'b_forget': _jnp.float32, 'w_alpha2': _jnp.float32, 'b_alpha': _jnp.float32, 'gla_norm_g': _jnp.float32, 'w_o_fox': _jnp.float32, 'w_o_gla': _jnp.float32, 'w_out': _jnp.float32, 'norm_mlp_g': _jnp.float32, 'w_ff1': _jnp.float32, 'w_ff2': _jnp.float32, 'final_norm_g': _jnp.float32}
MOMENT_SCALE = {'meta_tokens': 2.932422e-03, 'norm_mix_g': 6.334820e-02, 'w_in': 2.411904e-02, 'b_forget': 1.420952e-01, 'w_alpha2': 5.071308e-03, 'b_alpha': 1.851857e-02, 'gla_norm_g': 2.643390e-02, 'w_o_fox': 1.889266e-02, 'w_o_gla': 2.652684e-02, 'w_out': 3.262909e-02, 'norm_mlp_g': 7.536366e-02, 'w_ff1': 3.799115e-02, 'w_ff2': 8.048709e-02, 'final_norm_g': 1.627876e+01}


def _to_microbatches(a, axis):
    t = _jnp.moveaxis(a, axis, 0)
    t = t.reshape((N_MICROBATCH, t.shape[0] // N_MICROBATCH) + t.shape[1:])
    return _jnp.moveaxis(t, 1, axis + 1)


def setup_inputs(seed: int = 0) -> dict:
    inp = _fwd_setup_inputs(seed)
    key = _jax.random.fold_in(_jax.random.key(seed), 7919)
    shape, _ = _output_shape()
    out = dict(inp)
    out["loss_target"] = _jax.random.normal(_jax.random.fold_in(key, 0), shape, _jnp.float32)
    for i, name in enumerate(TWIN_WEIGHTS):
        w = inp[name].astype(_jnp.float32)
        if MOMENT_SCALE is None:
            s = _jnp.sqrt(_jnp.mean(_jnp.square(w)) + 1e-30)
        else:
            s = MOMENT_SCALE[name]
        km, kv = _jax.random.split(_jax.random.fold_in(key, i + 1))
        out[name] = w
        out["m_" + name] = s * _jax.random.normal(km, w.shape, _jnp.float32)
        out["v_" + name] = (s * s) * _jax.random.uniform(kv, w.shape, _jnp.float32, 0.5, 1.5)
    if N_MICROBATCH > 1:
        for name, axis in PER_EXAMPLE_BATCH_AXIS.items():
            out[name] = _to_microbatches(out[name], axis)
    return {'x': out['x'], 'meta_tokens': out['meta_tokens'], 'norm_mix_g': out['norm_mix_g'], 'w_in': out['w_in'], 'b_forget': out['b_forget'], 'w_alpha2': out['w_alpha2'], 'b_alpha': out['b_alpha'], 'gla_norm_g': out['gla_norm_g'], 'w_o_fox': out['w_o_fox'], 'w_o_gla': out['w_o_gla'], 'w_out': out['w_out'], 'norm_mlp_g': out['norm_mlp_g'], 'w_ff1': out['w_ff1'], 'w_ff2': out['w_ff2'], 'final_norm_g': out['final_norm_g'], 'loss_target': out['loss_target'], 'm_meta_tokens': out['m_meta_tokens'], 'm_norm_mix_g': out['m_norm_mix_g'], 'm_w_in': out['m_w_in'], 'm_b_forget': out['m_b_forget'], 'm_w_alpha2': out['m_w_alpha2'], 'm_b_alpha': out['m_b_alpha'], 'm_gla_norm_g': out['m_gla_norm_g'], 'm_w_o_fox': out['m_w_o_fox'], 'm_w_o_gla': out['m_w_o_gla'], 'm_w_out': out['m_w_out'], 'm_norm_mlp_g': out['m_norm_mlp_g'], 'm_w_ff1': out['m_w_ff1'], 'm_w_ff2': out['m_w_ff2'], 'm_final_norm_g': out['m_final_norm_g'], 'v_meta_tokens': out['v_meta_tokens'], 'v_norm_mix_g': out['v_norm_mix_g'], 'v_w_in': out['v_w_in'], 'v_b_forget': out['v_b_forget'], 'v_w_alpha2': out['v_w_alpha2'], 'v_b_alpha': out['v_b_alpha'], 'v_gla_norm_g': out['v_gla_norm_g'], 'v_w_o_fox': out['v_w_o_fox'], 'v_w_o_gla': out['v_w_o_gla'], 'v_w_out': out['v_w_out'], 'v_norm_mlp_g': out['v_norm_mlp_g'], 'v_w_ff1': out['v_w_ff1'], 'v_w_ff2': out['v_w_ff2'], 'v_final_norm_g': out['v_final_norm_g']}


def _loss(weights, diff, rest, loss_target):
    with _jax.named_scope("forward"):
        args = {**rest, TWIN_DIFF_INPUT: diff, **{k: w.astype(_WEIGHT_DTYPES[k]) for k, w in weights.items()}}
        y = _forward(args)
    with _jax.named_scope("loss_head"):
        err = _jnp.square(y.astype(_jnp.float32) - loss_target)
        return 0.5 * _jnp.sum(_jnp.mean(err, axis=-1)) if err.ndim else 0.5 * err


def _adamw(w, g, m, v):
    m = ADAM_B1 * m + (1.0 - ADAM_B1) * g
    v = ADAM_B2 * v + (1.0 - ADAM_B2) * _jnp.square(g)
    m_hat = m / (1.0 - ADAM_B1 ** ADAM_STEP)
    v_hat = v / (1.0 - ADAM_B2 ** ADAM_STEP)
    delta = -ADAM_LR * (m_hat / (_jnp.sqrt(v_hat) + ADAM_EPS) + ADAM_WD * w)
    return delta, m, v


def reference(x, meta_tokens, norm_mix_g, w_in, b_forget, w_alpha2, b_alpha, gla_norm_g, w_o_fox, w_o_gla, w_out, norm_mlp_g, w_ff1, w_ff2, final_norm_g, loss_target, m_meta_tokens, m_norm_mix_g, m_w_in, m_b_forget, m_w_alpha2, m_b_alpha, m_gla_norm_g, m_w_o_fox, m_w_o_gla, m_w_out, m_norm_mlp_g, m_w_ff1, m_w_ff2, m_final_norm_g, v_meta_tokens, v_norm_mix_g, v_w_in, v_b_forget, v_w_alpha2, v_b_alpha, v_gla_norm_g, v_w_o_fox, v_w_o_gla, v_w_out, v_norm_mlp_g, v_w_ff1, v_w_ff2, v_final_norm_g):
    given = dict(x=x, meta_tokens=meta_tokens, norm_mix_g=norm_mix_g, w_in=w_in, b_forget=b_forget, w_alpha2=w_alpha2, b_alpha=b_alpha, gla_norm_g=gla_norm_g, w_o_fox=w_o_fox, w_o_gla=w_o_gla, w_out=w_out, norm_mlp_g=norm_mlp_g, w_ff1=w_ff1, w_ff2=w_ff2, final_norm_g=final_norm_g, loss_target=loss_target, m_meta_tokens=m_meta_tokens, m_norm_mix_g=m_norm_mix_g, m_w_in=m_w_in, m_b_forget=m_b_forget, m_w_alpha2=m_w_alpha2, m_b_alpha=m_b_alpha, m_gla_norm_g=m_gla_norm_g, m_w_o_fox=m_w_o_fox, m_w_o_gla=m_w_o_gla, m_w_out=m_w_out, m_norm_mlp_g=m_norm_mlp_g, m_w_ff1=m_w_ff1, m_w_ff2=m_w_ff2, m_final_norm_g=m_final_norm_g, v_meta_tokens=v_meta_tokens, v_norm_mix_g=v_norm_mix_g, v_w_in=v_w_in, v_b_forget=v_b_forget, v_w_alpha2=v_w_alpha2, v_b_alpha=v_b_alpha, v_gla_norm_g=v_gla_norm_g, v_w_o_fox=v_w_o_fox, v_w_o_gla=v_w_o_gla, v_w_out=v_w_out, v_norm_mlp_g=v_norm_mlp_g, v_w_ff1=v_w_ff1, v_w_ff2=v_w_ff2, v_final_norm_g=v_final_norm_g)
    weights = {n: given[n] for n in TWIN_WEIGHTS}
    shared = {n: given[n] for n in SHARED_INPUTS}
    per_example = {n: given[n] for n in ['x']}
    grad_fn = _jax.value_and_grad(_loss, argnums=(0, 1))

    def one_microbatch(ex, loss_target):
        ex = dict(ex)
        diff = ex.pop(TWIN_DIFF_INPUT)
        return grad_fn(weights, diff, {**shared, **ex}, loss_target)

    if N_MICROBATCH == 1:
        loss, (grad_w, grad_x) = one_microbatch(per_example, given["loss_target"])
    else:
        def body(carry, xs):
            loss_sum, grad_sum = carry
            l_k, (gw_k, gx_k) = one_microbatch(xs[0], xs[1])
            with _jax.named_scope("update"):
                return (loss_sum + l_k, _jax.tree.map(_jnp.add, grad_sum, gw_k)), gx_k

        init = (_jnp.zeros((), _jnp.float32), _jax.tree.map(_jnp.zeros_like, weights))
        (loss, grad_w), grad_x = _jax.lax.scan(body, init, (per_example, given["loss_target"]))
    with _jax.named_scope("update"):
        delta_w, new_m, new_v = {}, {}, {}
        for n in TWIN_WEIGHTS:
            delta_w[n], new_m[n], new_v[n] = _adamw(weights[n], grad_w[n], given["m_" + n], given["v_" + n])
    return (loss, grad_x, *[grad_w[n] for n in TWIN_WEIGHTS], *[delta_w[n] for n in TWIN_WEIGHTS],
            *[new_m[n] for n in TWIN_WEIGHTS], *[new_v[n] for n in TWIN_WEIGHTS])
```

```python
import functools
import math

import jax
import jax.numpy as jnp
from jax import lax
from jax.experimental import pallas as pl
from jax.experimental.pallas import tpu as pltpu

F32 = jnp.float32
BF16 = jnp.bfloat16

N_META = 16
PAD = 112
ROW0 = PAD + N_META
EPS = 1e-6
MASK_VALUE = -1e30
FOX_HEADS = 8
GLA_HEADS = 4
GLA_RANK = 16
GLA_TAU = 16.0
GLA_CHUNK = 64
DEPTH = 2
N_CHIPS = 4
N_DEV = 8

ADAM_LR = 0.001
ADAM_B1 = 0.9
ADAM_B2 = 0.999
ADAM_EPS = 1e-08
ADAM_WD = 0.01
ADAM_STEP = 10

LANES = 128
VMEM_LIMIT = 56 * 1024 * 1024
MESH = pl.DeviceIdType.MESH


def _pick(n, target, mult):
    best = None
    for d in range(mult, min(n, target) + 1, mult):
        if n % d == 0:
            best = d
    return n if best is None else best


def _params(sem=None):
    return pltpu.CompilerParams(dimension_semantics=sem, vmem_limit_bytes=VMEM_LIMIT)


def _bf(v):
    return v if v.dtype == BF16 else v.astype(BF16)


def _sigmoid(z):
    return 1.0 / (1.0 + jnp.exp(-z))


def _log_sigmoid(z):
    return jnp.minimum(z, 0.0) - jnp.log(1.0 + jnp.exp(-jnp.abs(z)))


def _split3(v):
    a = v.astype(BF16)
    r = v - a.astype(F32)
    b = r.astype(BF16)
    c = (r - b.astype(F32)).astype(BF16)
    return a, b, c


def _dot(a, b, dims):
    return lax.dot_general(a, b, (dims, ((), ())), preferred_element_type=F32)


NN = ((1,), (0,))
NT = ((1,), (1,))
TN = ((0,), (0,))


def _tri_dot(tri, v, dims=NN):
    a, b, c = _split3(v)
    return _dot(tri, a, dims) + _dot(tri, b, dims) + _dot(tri, c, dims)


def _mm(name, a, b, *, mode, m, n, k, a_c0=0, b_c0=0, extras=(), epilogue=None,
        out_dtypes=(F32,), tm=1056, tn=1024, tk=512):
    mult_m = LANES if mode == "tn" else 16
    tm = _pick(m, tm, mult_m)
    tn = _pick(n, tn, LANES)
    tk = _pick(k, tk, LANES if mode != "tn" else 16)
    assert a_c0 % (tm if mode == "tn" else tk) == 0 and b_c0 % (tk if mode == "nt" else tn) == 0
    nk = k // tk
    if mode == "nn":
        a_spec = pl.BlockSpec((tm, tk), lambda i, j, kk: (i, kk + a_c0 // tk))
        b_spec = pl.BlockSpec((tk, tn), lambda i, j, kk: (kk, j + b_c0 // tn))
        dims = NN
    elif mode == "nt":
        a_spec = pl.BlockSpec((tm, tk), lambda i, j, kk: (i, kk + a_c0 // tk))
        b_spec = pl.BlockSpec((tn, tk), lambda i, j, kk: (j, kk + b_c0 // tk))
        dims = NT
    else:
        a_spec = pl.BlockSpec((tk, tm), lambda i, j, kk: (kk, i + a_c0 // tm))
        b_spec = pl.BlockSpec((tk, tn), lambda i, j, kk: (kk, j + b_c0 // tn))
        dims = TN
    ex_specs = []
    for arr, c0 in extras:
        assert c0 % tn == 0
        ex_specs.append(pl.BlockSpec((tm, tn), functools.partial(lambda i, j, kk, o: (i, j + o), o=c0 // tn)))
    n_ex = len(extras)
    n_out = len(out_dtypes)

    def body(a_ref, b_ref, *rest):
        ex_refs = rest[:n_ex]
        out_refs = rest[n_ex:n_ex + n_out]
        acc_ref = rest[n_ex + n_out]
        kk = pl.program_id(2)

        @pl.when(kk == 0)
        def _():
            acc_ref[...] = jnp.zeros_like(acc_ref)

        acc_ref[...] += _dot(_bf(a_ref[...]), _bf(b_ref[...]), dims)

        @pl.when(kk == nk - 1)
        def _():
            acc = acc_ref[...]
            vals = (acc,) if epilogue is None else epilogue(acc, *[r[...] for r in ex_refs])
            for r, v in zip(out_refs, vals):
                r[...] = v.astype(r.dtype)

    outs = pl.pallas_call(
        body,
        name=name,
        grid=(m // tm, n // tn, nk),
        in_specs=[a_spec, b_spec] + ex_specs,
        out_specs=[pl.BlockSpec((tm, tn), lambda i, j, kk: (i, j)) for _ in out_dtypes],
        out_shape=[jax.ShapeDtypeStruct((m, n), dt) for dt in out_dtypes],
        scratch_shapes=[pltpu.VMEM((tm, tn), F32)],
        compiler_params=_params(("parallel", "parallel", "arbitrary")),
    )(a, b, *[arr for arr, _ in extras])
    return outs[0] if n_out == 1 else outs


def _ew(name, fn, ins, outs, rows, tm):
    tm = _pick(rows, tm, 16)
    in_specs, args = [], []
    for spec in ins:
        if spec[0] == "tile":
            _, arr, width, c0 = spec
            assert c0 % width == 0
            in_specs.append(pl.BlockSpec((tm, width), functools.partial(lambda i, o: (i, o), o=c0 // width)))
        else:
            arr = spec[1]
            in_specs.append(pl.BlockSpec(arr.shape, lambda i: (0, 0)))
        args.append(arr)
    out_specs, out_shape = [], []
    for kind, dt, width in outs:
        if kind == "tile":
            out_specs.append(pl.BlockSpec((tm, width), lambda i: (i, 0)))
            out_shape.append(jax.ShapeDtypeStruct((rows, width), dt))
        else:
            out_specs.append(pl.BlockSpec((1, width), lambda i: (0, 0)))
            out_shape.append(jax.ShapeDtypeStruct((1, width), dt))
    n_in = len(ins)
    has_acc = any(o[0] == "acc" for o in outs)

    def body(*refs):
        i = pl.program_id(0)
        vals = fn(i * tm, *[r[...] for r in refs[:n_in]])
        for (kind, _, _), r, v in zip(outs, refs[n_in:], vals):
            if kind == "tile":
                r[...] = v.astype(r.dtype)
            else:
                @pl.when(i == 0)
                def _():
                    r[...] = jnp.zeros_like(r)

                r[...] += v.astype(r.dtype)

    res = pl.pallas_call(
        body,
        name=name,
        grid=(rows // tm,),
        in_specs=in_specs,
        out_specs=out_specs,
        out_shape=out_shape,
        compiler_params=_params(("arbitrary",) if has_acc else ("parallel",)),
    )(*args)
    return res[0] if len(outs) == 1 else res


def _row_ids(row0, tm):
    return row0 + lax.broadcasted_iota(jnp.int32, (tm, 1), 0)


def _colsum(v):
    return jnp.sum(v, axis=0, keepdims=True)


def _rms_fwd(name, h, g, t, d):
    def fn(row0, x, gg):
        r = lax.rsqrt(jnp.mean(x * x, axis=-1, keepdims=True) + EPS)
        return (x * r * gg,)

    return _ew(name, fn, [("tile", h, d, 0), ("full", g)], [("tile", BF16, d)], t, 264)


def _rms_bwd(name, h, g, dy, dres, t, d):
    def fn(row0, x, gg, dyv, dr):
        r = lax.rsqrt(jnp.mean(x * x, axis=-1, keepdims=True) + EPS)
        xh = x * r
        dxh = dyv * gg
        dx = r * (dxh - xh * jnp.mean(dxh * xh, axis=-1, keepdims=True))
        out = jnp.where(_row_ids(row0, x.shape[0]) >= PAD, dr + dx, 0.0)
        return out, _colsum(dyv * xh)

    return _ew(name, fn, [("tile", h, d, 0), ("full", g), ("tile", dy, d, 0), ("tile", dres, d, 0)],
               [("tile", F32, d), ("acc", F32, d)], t, 264)


def _loss_head(h, g, target_p, t, d):
    def fn(row0, x, gg, tgt):
        real = _row_ids(row0, x.shape[0]) >= ROW0
        r = lax.rsqrt(jnp.mean(x * x, axis=-1, keepdims=True) + EPS)
        xh = x * r
        err = jnp.where(real, xh * gg - tgt, 0.0)
        loss_rows = 0.5 * jnp.mean(err * err, axis=-1, keepdims=True)
        dyv = err * (1.0 / d)
        dxh = dyv * gg
        dx = r * (dxh - xh * jnp.mean(dxh * xh, axis=-1, keepdims=True))
        loss_part = jnp.sum(loss_rows, axis=0, keepdims=True) * jnp.ones((1, LANES), F32)
        return jnp.where(real, dx, 0.0), _colsum(dyv * xh), loss_part

    return _ew("loss_head", fn, [("tile", h, d, 0), ("full", g), ("tile", target_p, d, 0)],
               [("tile", F32, d), ("acc", F32, d), ("acc", F32, LANES)], t, 264)


def _merge_fwd(a_fox, a_gla, proj, c_gates, t, d):
    def fn(row0, af, ag, gates):
        gates = gates.astype(F32)
        return (_sigmoid(gates[:, :d]) * af + _sigmoid(gates[:, d:]) * ag,)

    return _ew("merge_fwd", fn, [("tile", a_fox, d, 0), ("tile", a_gla, d, 0), ("tile", proj, 2 * d, c_gates)],
               [("tile", BF16, d)], t, 264)


def _merge_bwd(dy, a_fox, a_gla, proj, c_gates, t, d):
    def fn(row0, dyv, af, ag, gates):
        gates = gates.astype(F32)
        sf = _sigmoid(gates[:, :d])
        sg = _sigmoid(gates[:, d:])
        dgates = jnp.concatenate([dyv * af * sf * (1.0 - sf), dyv * ag * sg * (1.0 - sg)], axis=1)
        return dyv * sf, dyv * sg, dgates

    return _ew("merge_bwd", fn,
               [("tile", dy, d, 0), ("tile", a_fox, d, 0), ("tile", a_gla, d, 0), ("tile", proj, 2 * d, c_gates)],
               [("tile", BF16, d), ("tile", BF16, d), ("tile", BF16, 2 * d)], t, 264)


def _fox_gate_fwd(small, b_forget_p, t):
    tb = _pick(t, 384, LANES)

    def body(s_ref, b_ref, c_ref, carry_ref):
        i = pl.program_id(0)

        @pl.when(i == 0)
        def _():
            carry_ref[...] = jnp.zeros_like(carry_ref)

        logf = _log_sigmoid(s_ref[...] + b_ref[...])
        logf = jnp.where(_row_ids(i * tb, tb) >= PAD, logf, 0.0)
        r = lax.broadcasted_iota(jnp.int32, (tb, tb), 0)
        c = lax.broadcasted_iota(jnp.int32, (tb, tb), 1)
        tri = (c <= r).astype(BF16)
        cs = _tri_dot(tri, logf) + carry_ref[...]
        c_ref[...] = cs
        carry_ref[...] = cs[tb - 1:tb, :]

    return pl.pallas_call(
        body, name="fox_gate_fwd", grid=(t // tb,),
        in_specs=[pl.BlockSpec((tb, LANES), lambda i: (i, 0)), pl.BlockSpec((1, LANES), lambda i: (0, 0))],
        out_specs=pl.BlockSpec((tb, LANES), lambda i: (i, 0)),
        out_shape=jax.ShapeDtypeStruct((t, LANES), F32),
        scratch_shapes=[pltpu.VMEM((1, LANES), F32)],
        compiler_params=_params(("arbitrary",)),
    )(small, b_forget_p)


def _fox_gate_bwd(dc, small, b_forget_p, dga, t):
    tb = _pick(t, 384, LANES)
    nb = t // tb

    def body(dc_ref, s_ref, b_ref, dga_ref, ds_ref, db_ref, carry_ref):
        i = pl.program_id(0)

        @pl.when(i == 0)
        def _():
            carry_ref[...] = jnp.zeros_like(carry_ref)
            db_ref[...] = jnp.zeros_like(db_ref)

        r = lax.broadcasted_iota(jnp.int32, (tb, tb), 0)
        c = lax.broadcasted_iota(jnp.int32, (tb, tb), 1)
        tri = (c >= r).astype(BF16)
        dlogf = _tri_dot(tri, dc_ref[...]) + carry_ref[...]
        carry_ref[...] = dlogf[0:1, :]
        z = s_ref[...] + b_ref[...]
        dff = dlogf * _sigmoid(-z)
        lane = lax.broadcasted_iota(jnp.int32, (tb, LANES), 1)
        keep = (_row_ids((nb - 1 - i) * tb, tb) >= PAD) & (lane < FOX_HEADS)
        dff = jnp.where(keep, dff, 0.0)
        ds_ref[...] = dff + dga_ref[...]
        db_ref[...] += _colsum(dff)

    rev = lambda i: (nb - 1 - i, 0)
    return pl.pallas_call(
        body, name="fox_gate_bwd", grid=(nb,),
        in_specs=[pl.BlockSpec((tb, LANES), rev), pl.BlockSpec((tb, LANES), rev),
                  pl.BlockSpec((1, LANES), lambda i: (0, 0)), pl.BlockSpec((tb, LANES), rev)],
        out_specs=[pl.BlockSpec((tb, LANES), rev), pl.BlockSpec((1, LANES), lambda i: (0, 0))],
        out_shape=[jax.ShapeDtypeStruct((t, LANES), F32), jax.ShapeDtypeStruct((1, LANES), F32)],
        scratch_shapes=[pltpu.VMEM((1, LANES), F32)],
        compiler_params=_params(("arbitrary",)),
    )(dc, small, b_forget_p, dga)


def _fox_scores(q, k, cq, ck, qi, ki, tb, scale):
    s = _dot(q, k, NT) * scale + cq - ck
    row = qi * tb + lax.broadcasted_iota(jnp.int32, (tb, tb), 0)
    col = ki * tb + lax.broadcasted_iota(jnp.int32, (tb, tb), 1)
    return jnp.where((col <= row) & (col >= PAD), s, MASK_VALUE)


def _fox_specs(tb, fd, c_fq, c_fkv):
    fq0, kv0 = c_fq // fd, c_fkv // fd
    return dict(
        q=lambda qmap: pl.BlockSpec((tb, fd), lambda h, a, b: (qmap(a, b), fq0 + h)),
        k=lambda kmap: pl.BlockSpec((tb, fd), lambda h, a, b: (kmap(a, b), kv0 + 2 * h)),
        v=lambda kmap: pl.BlockSpec((tb, fd), lambda h, a, b: (kmap(a, b), kv0 + 2 * h + 1)),
        col=lambda qmap: pl.BlockSpec((None, tb, 1), lambda h, a, b: (h, qmap(a, b), 0)),
        row=lambda kmap: pl.BlockSpec((None, 1, tb), lambda h, a, b: (h, 0, kmap(a, b))),
        head=lambda qmap: pl.BlockSpec((tb, fd), lambda h, a, b: (qmap(a, b), h)),
    )


def _fox_fwd(proj, c_col, c_row, t, fd, c_fq, c_fkv):
    tb = _pick(t, 384, LANES)
    nb = t // tb
    scale = fd ** -0.5
    sp = _fox_specs(tb, fd, c_fq, c_fkv)
    qmap = lambda qi, ki: qi
    kmap = lambda qi, ki: jnp.minimum(ki, qi)

    def body(q_ref, k_ref, v_ref, cq_ref, ck_ref, o_ref, lse_ref, m_ref, l_ref, acc_ref):
        qi, ki = pl.program_id(1), pl.program_id(2)

        @pl.when(ki == 0)
        def _():
            m_ref[...] = jnp.full_like(m_ref, -jnp.inf)
            l_ref[...] = jnp.zeros_like(l_ref)
            acc_ref[...] = jnp.zeros_like(acc_ref)

        @pl.when(ki <= qi)
        def _():
            s = _fox_scores(q_ref[...], k_ref[...], cq_ref[...], ck_ref[...], qi, ki, tb, scale)
            m_new = jnp.maximum(m_ref[...], jnp.max(s, axis=-1, keepdims=True))
            alpha = jnp.exp(m_ref[...] - m_new)
            p = jnp.exp(s - m_new)
            l_ref[...] = alpha * l_ref[...] + jnp.sum(p, axis=-1, keepdims=True)
            acc_ref[...] = alpha * acc_ref[...] + _dot(p.astype(BF16), v_ref[...], NN)
            m_ref[...] = m_new

        @pl.when(ki == nb - 1)
        def _():
            o_ref[...] = jnp.where(_row_ids(qi * tb, tb) >= PAD, acc_ref[...] / l_ref[...], 0.0)
            lse_ref[...] = m_ref[...] + jnp.log(l_ref[...])

    return pl.pallas_call(
        body, name="fox_fwd", grid=(FOX_HEADS, nb, nb),
        in_specs=[sp["q"](qmap), sp["k"](kmap), sp["v"](kmap), sp["col"](qmap), sp["row"](kmap)],
        out_specs=[sp["head"](qmap), sp["col"](qmap)],
        out_shape=[jax.ShapeDtypeStruct((t, FOX_HEADS * fd), F32), jax.ShapeDtypeStruct((FOX_HEADS, t, 1), F32)],
        scratch_shapes=[pltpu.VMEM((tb, 1), F32), pltpu.VMEM((tb, 1), F32), pltpu.VMEM((tb, fd), F32)],
        compiler_params=_params(("parallel", "parallel", "arbitrary")),
    )(proj, proj, proj, c_col, c_row)


def _fox_delta(proj, c_col, c_row, lse, do_fox, t, fd, c_fq, c_fkv):
    tb = _pick(t, 384, LANES)
    nb = t // tb
    scale = fd ** -0.5
    sp = _fox_specs(tb, fd, c_fq, c_fkv)
    qmap = lambda qi, ki: qi
    kmap = lambda qi, ki: jnp.minimum(ki, qi)

    def body(q_ref, k_ref, v_ref, cq_ref, ck_ref, lse_ref, do_ref, dl_ref, acc_ref):
        qi, ki = pl.program_id(1), pl.program_id(2)

        @pl.when(ki == 0)
        def _():
            acc_ref[...] = jnp.zeros_like(acc_ref)

        @pl.when(ki <= qi)
        def _():
            s = _fox_scores(q_ref[...], k_ref[...], cq_ref[...], ck_ref[...], qi, ki, tb, scale)
            p = jnp.exp(s - lse_ref[...])
            dp = _dot(_bf(do_ref[...]), v_ref[...], NT)
            acc_ref[...] += jnp.sum(p * dp, axis=-1, keepdims=True)

        @pl.when(ki == nb - 1)
        def _():
            dl_ref[...] = acc_ref[...]

    return pl.pallas_call(
        body, name="fox_delta", grid=(FOX_HEADS, nb, nb),
        in_specs=[sp["q"](qmap), sp["k"](kmap), sp["v"](kmap), sp["col"](qmap), sp["row"](kmap),
                  sp["col"](qmap), sp["head"](qmap)],
        out_specs=sp["col"](qmap),
        out_shape=jax.ShapeDtypeStruct((FOX_HEADS, t, 1), F32),
        scratch_shapes=[pltpu.VMEM((tb, 1), F32)],
        compiler_params=_params(("parallel", "parallel", "arbitrary")),
    )(proj, proj, proj, c_col, c_row, lse, do_fox)


def _fox_bwd_dq(proj, c_col, c_row, lse, delta, do_fox, t, fd, c_fq, c_fkv):
    tb = _pick(t, 384, LANES)
    nb = t // tb
    scale = fd ** -0.5
    sp = _fox_specs(tb, fd, c_fq, c_fkv)
    qmap = lambda qi, ki: qi
    kmap = lambda qi, ki: jnp.minimum(ki, qi)

    def body(q_ref, k_ref, v_ref, cq_ref, ck_ref, lse_ref, dl_ref, do_ref, dq_ref, acc_ref):
        qi, ki = pl.program_id(1), pl.program_id(2)

        @pl.when(ki == 0)
        def _():
            acc_ref[...] = jnp.zeros_like(acc_ref)

        @pl.when(ki <= qi)
        def _():
            s = _fox_scores(q_ref[...], k_ref[...], cq_ref[...], ck_ref[...], qi, ki, tb, scale)
            p = jnp.exp(s - lse_ref[...])
            dp = _dot(_bf(do_ref[...]), v_ref[...], NT)
            ds = p * (dp - dl_ref[...])
            acc_ref[...] += _dot(ds.astype(BF16), k_ref[...], NN)

        @pl.when(ki == nb - 1)
        def _():
            dq_ref[...] = (acc_ref[...] * scale).astype(dq_ref.dtype)

    return pl.pallas_call(
        body, name="fox_bwd_dq", grid=(FOX_HEADS, nb, nb),
        in_specs=[sp["q"](qmap), sp["k"](kmap), sp["v"](kmap), sp["col"](qmap), sp["row"](kmap),
                  sp["col"](qmap), sp["col"](qmap), sp["head"](qmap)],
        out_specs=sp["head"](qmap),
        out_shape=jax.ShapeDtypeStruct((t, FOX_HEADS * fd), BF16),
        scratch_shapes=[pltpu.VMEM((tb, fd), F32)],
        compiler_params=_params(("parallel", "parallel", "arbitrary")),
    )(proj, proj, proj, c_col, c_row, lse, delta, do_fox)


def _fox_bwd_dkv(proj, c_col, c_row, lse, delta, do_fox, t, fd, c_fq, c_fkv):
    tb = _pick(t, 384, LANES)
    nb = t // tb
    scale = fd ** -0.5
    sp = _fox_specs(tb, fd, c_fq, c_fkv)
    kmap = lambda ki, qi: ki
    qmap = lambda ki, qi: jnp.maximum(qi, ki)

    def body(q_ref, k_ref, v_ref, cq_ref, ck_ref, lse_ref, dl_ref, do_ref, dkv_ref, dc_ref,
             dk_acc, dv_acc, dc_acc):
        ki, qi = pl.program_id(1), pl.program_id(2)

        @pl.when(qi == 0)
        def _():
            dk_acc[...] = jnp.zeros_like(dk_acc)
            dv_acc[...] = jnp.zeros_like(dv_acc)
            dc_acc[...] = jnp.zeros_like(dc_acc)

        @pl.when(qi >= ki)
        def _():
            s = _fox_scores(q_ref[...], k_ref[...], cq_ref[...], ck_ref[...], qi, ki, tb, scale)
            p = jnp.exp(s - lse_ref[...])
            do = _bf(do_ref[...])
            dv_acc[...] += _dot(p.astype(BF16), do, TN)
            dp = _dot(do, v_ref[...], NT)
            ds = p * (dp - dl_ref[...])
            dk_acc[...] += _dot(ds.astype(BF16), q_ref[...], TN)
            dc_acc[...] += _colsum(ds)

        @pl.when(qi == nb - 1)
        def _():
            dkv_ref[:, :fd] = (dk_acc[...] * scale).astype(dkv_ref.dtype)
            dkv_ref[:, fd:] = dv_acc[...].astype(dkv_ref.dtype)
            dc_ref[...] = -dc_acc[...]

    return pl.pallas_call(
        body, name="fox_bwd_dkv", grid=(FOX_HEADS, nb, nb),
        in_specs=[sp["q"](qmap), sp["k"](kmap), sp["v"](kmap), sp["col"](qmap), sp["row"](kmap),
                  sp["col"](qmap), sp["col"](qmap), sp["head"](qmap)],
        out_specs=[pl.BlockSpec((tb, 2 * fd), lambda h, a, b: (a, h)), sp["row"](kmap)],
        out_shape=[jax.ShapeDtypeStruct((t, 2 * FOX_HEADS * fd), BF16), jax.ShapeDtypeStruct((FOX_HEADS, 1, t), F32)],
        scratch_shapes=[pltpu.VMEM((tb, fd), F32), pltpu.VMEM((tb, fd), F32), pltpu.VMEM((1, tb), F32)],
        compiler_params=_params(("parallel", "parallel", "arbitrary")),
    )(proj, proj, proj, c_col, c_row, lse, delta, do_fox)


def _gla_gate_fwd(small, w_alpha_p, b_alpha, t, gk):
    def fn(row0, s, w, b):
        z = _dot(s.astype(BF16), w, NN) + b
        return (jnp.where(_row_ids(row0, s.shape[0]) >= PAD, _log_sigmoid(z) * (1.0 / GLA_TAU), 0.0),)

    return _ew("gla_gate_fwd", fn, [("tile", small, LANES, 0), ("full", w_alpha_p), ("full", b_alpha)],
               [("tile", F32, gk)], t, 264)


def _gla_gate_bwd(dglog, small, w_alpha_p, b_alpha, t, gk):
    def fn(row0, dg, s, w, b):
        z = _dot(s.astype(BF16), w, NN) + b
        dz = jnp.where(_row_ids(row0, s.shape[0]) >= PAD, dg * (1.0 / GLA_TAU) * _sigmoid(-z), 0.0)
        return dz, _colsum(dz)

    return _ew("gla_gate_bwd", fn,
               [("tile", dglog, gk, 0), ("tile", small, LANES, 0), ("full", w_alpha_p), ("full", b_alpha)],
               [("tile", BF16, gk), ("acc", F32, gk)], t, 264)


def _gla_chunk(q, k, g, scale, cs):
    r = lax.broadcasted_iota(jnp.int32, (cs, cs), 0)
    c = lax.broadcasted_iota(jnp.int32, (cs, cs), 1)
    causal = c <= r
    b = _tri_dot(causal.astype(BF16), g)
    bl = b[cs - 1:cs, :]
    eb, einv, eend = jnp.exp(b), jnp.exp(-b), jnp.exp(bl - b)
    qd = q.astype(F32) * scale * eb
    kf = k.astype(F32)
    return causal, (eb, einv, eend), bl, qd, kf * einv, kf * eend


def _gla_fwd(proj, glog, t, dk, dv, c_q, c_k, c_v):
    cs = GLA_CHUNK
    nc = t // cs
    wk, wv = GLA_HEADS * dk, GLA_HEADS * dv
    scale = dk ** -0.5

    def body(q_ref, k_ref, v_ref, g_ref, o_ref, sp_ref, st_ref):
        @pl.when(pl.program_id(0) == 0)
        def _():
            st_ref[...] = jnp.zeros_like(st_ref)

        for h in range(GLA_HEADS):
            ks, vs = slice(h * dk, (h + 1) * dk), slice(h * dv, (h + 1) * dv)
            v = v_ref[:, vs]
            causal, _, bl, qd, ki, ke = _gla_chunk(q_ref[:, ks], k_ref[:, ks], g_ref[:, ks], scale, cs)
            st = st_ref[h]
            sp_ref[h] = st
            a = jnp.where(causal, _dot(qd.astype(BF16), ki.astype(BF16), NT), 0.0)
            o_ref[:, vs] = _dot(a.astype(BF16), v, NN) + _dot(qd.astype(BF16), st.astype(BF16), NT)
            st_ref[h] = st * jnp.exp(bl) + _dot(v, ke.astype(BF16), TN)

    return pl.pallas_call(
        body, name="gla_fwd", grid=(nc,),
        in_specs=[pl.BlockSpec((cs, wk), lambda n: (n, c_q // wk)), pl.BlockSpec((cs, wk), lambda n: (n, c_k // wk)),
                  pl.BlockSpec((cs, wv), lambda n: (n, c_v // wv)), pl.BlockSpec((cs, wk), lambda n: (n, 0))],
        out_specs=[pl.BlockSpec((cs, wv), lambda n: (n, 0)),
                   pl.BlockSpec((None, GLA_HEADS, dv, dk), lambda n: (n, 0, 0, 0))],
        out_shape=[jax.ShapeDtypeStruct((t, wv), F32), jax.ShapeDtypeStruct((nc, GLA_HEADS, dv, dk), F32)],
        scratch_shapes=[pltpu.VMEM((GLA_HEADS, dv, dk), F32)],
        compiler_params=_params(("arbitrary",)),
    )(proj, proj, proj, glog)


def _gla_bwd(proj, glog, s_prev, do_raw, t, dk, dv, c_q, c_k, c_v):
    cs = GLA_CHUNK
    nc = t // cs
    wk, wv = GLA_HEADS * dk, GLA_HEADS * dv
    scale = dk ** -0.5

    def body(q_ref, k_ref, v_ref, g_ref, sp_ref, do_ref, dq_ref, dk_ref, dv_ref, dg_ref, dst_ref):
        @pl.when(pl.program_id(0) == 0)
        def _():
            dst_ref[...] = jnp.zeros_like(dst_ref)

        for h in range(GLA_HEADS):
            ks, vs = slice(h * dk, (h + 1) * dk), slice(h * dv, (h + 1) * dv)
            v = v_ref[:, vs]
            do = do_ref[:, vs].astype(BF16)
            causal, (eb, einv, eend), bl, qd, ki, ke = _gla_chunk(q_ref[:, ks], k_ref[:, ks], g_ref[:, ks], scale, cs)
            qd16, ki16, ke16 = qd.astype(BF16), ki.astype(BF16), ke.astype(BF16)
            st = sp_ref[h]
            dst = dst_ref[h]
            dst16 = dst.astype(BF16)
            a = jnp.where(causal, _dot(qd16, ki16, NT), 0.0).astype(BF16)
            da = jnp.where(causal, _dot(do, v, NT), 0.0).astype(BF16)
            dvv = _dot(a, do, TN) + _dot(ke16, dst16, NT)
            dqd = _dot(da, ki16, NN) + _dot(do, st.astype(BF16), NN)
            dki = _dot(da, qd16, TN)
            dke = _dot(v, dst16, NN)
            dl = jnp.exp(bl)
            ddl = _colsum(dst * st)
            dst_ref[h] = dst * dl + _dot(do, qd16, TN)
            dq_ref[:, ks] = (dqd * eb * scale).astype(dq_ref.dtype)
            dk_ref[:, ks] = (dki * einv + dke * eend).astype(dk_ref.dtype)
            dv_ref[:, vs] = dvv.astype(dv_ref.dtype)
            db = dqd * qd - dki * ki - dke * ke
            db_last = _colsum(dke * ke) + ddl * dl
            r = lax.broadcasted_iota(jnp.int32, (cs, cs), 0)
            c = lax.broadcasted_iota(jnp.int32, (cs, cs), 1)
            dg_ref[:, ks] = _tri_dot((c >= r).astype(BF16), db) + db_last

    rev = lambda f: (lambda n: f(nc - 1 - n))
    return pl.pallas_call(
        body, name="gla_bwd", grid=(nc,),
        in_specs=[pl.BlockSpec((cs, wk), rev(lambda n: (n, c_q // wk))), pl.BlockSpec((cs, wk), rev(lambda n: (n, c_k // wk))),
                  pl.BlockSpec((cs, wv), rev(lambda n: (n, c_v // wv))), pl.BlockSpec((cs, wk), rev(lambda n: (n, 0))),
                  pl.BlockSpec((None, GLA_HEADS, dv, dk), rev(lambda n: (n, 0, 0, 0))),
                  pl.BlockSpec((cs, wv), rev(lambda n: (n, 0)))],
        out_specs=[pl.BlockSpec((cs, wk), rev(lambda n: (n, 0))), pl.BlockSpec((cs, wk), rev(lambda n: (n, 0))),
                   pl.BlockSpec((cs, wv), rev(lambda n: (n, 0))), pl.BlockSpec((cs, wk), rev(lambda n: (n, 0)))],
        out_shape=[jax.ShapeDtypeStruct((t, wk), BF16), jax.ShapeDtypeStruct((t, wk), BF16),
                   jax.ShapeDtypeStruct((t, wv), BF16), jax.ShapeDtypeStruct((t, wk), F32)],
        scratch_shapes=[pltpu.VMEM((GLA_HEADS, dv, dk), F32)],
        compiler_params=_params(("arbitrary",)),
    )(proj, proj, proj, glog, s_prev, do_raw)


def _gla_post_fwd(o_raw, proj, gn, t, dv, c_gr):
    w = GLA_HEADS * dv

    def fn(row0, o, gr, g):
        gr = gr.astype(F32)
        outs = []
        for h in range(GLA_HEADS):
            oh = o[:, h * dv:(h + 1) * dv]
            outs.append(oh * lax.rsqrt(jnp.mean(oh * oh, axis=-1, keepdims=True) + EPS))
        on = jnp.concatenate(outs, axis=1) * g
        return (on * (gr * _sigmoid(gr)),)

    return _ew("gla_post_fwd", fn, [("tile", o_raw, w, 0), ("tile", proj, w, c_gr), ("full", gn)],
               [("tile", BF16, w)], t, 264)


def _gla_post_bwd(o_raw, proj, gn, do_gla, t, dv, c_gr):
    w = GLA_HEADS * dv

    def fn(row0, o, gr, g, do):
        gr = gr.astype(F32)
        sg = _sigmoid(gr)
        don = do * (gr * sg)
        ohs, dos = [], []
        for h in range(GLA_HEADS):
            sl = slice(h * dv, (h + 1) * dv)
            oh = o[:, sl]
            r = lax.rsqrt(jnp.mean(oh * oh, axis=-1, keepdims=True) + EPS)
            xh = oh * r
            dxh = don[:, sl] * g[:, sl]
            ohs.append(xh)
            dos.append(r * (dxh - xh * jnp.mean(dxh * xh, axis=-1, keepdims=True)))
        xh = jnp.concatenate(ohs, axis=1)
        dgr = do * (xh * g) * (sg * (1.0 + gr * (1.0 - sg)))
        return jnp.concatenate(dos, axis=1), dgr, _colsum(don * xh)

    return _ew("gla_post_bwd", fn,
               [("tile", o_raw, w, 0), ("tile", proj, w, c_gr), ("full", gn), ("tile", do_gla, w, 0)],
               [("tile", F32, w), ("tile", BF16, w), ("acc", F32, w)], t, 264)


def _adamw(name, w, g, m, v):
    rows, cols = w.shape
    tm = _pick(rows, max(8, (512 * 1024) // max(cols, 1) // 8 * 8), 8)

    def body(w_ref, g_ref, m_ref, v_ref, d_ref, nm_ref, nv_ref):
        gg = g_ref[...]
        nm = ADAM_B1 * m_ref[...] + (1.0 - ADAM_B1) * gg
        nv = ADAM_B2 * v_ref[...] + (1.0 - ADAM_B2) * (gg * gg)
        m_hat = nm / (1.0 - ADAM_B1 ** ADAM_STEP)
        v_hat = nv / (1.0 - ADAM_B2 ** ADAM_STEP)
        d_ref[...] = -ADAM_LR * (m_hat / (jnp.sqrt(v_hat) + ADAM_EPS) + ADAM_WD * w_ref[...])
        nm_ref[...] = nm
        nv_ref[...] = nv

    spec = pl.BlockSpec((tm, cols), lambda i: (i, 0))
    return pl.pallas_call(
        body, name=name, grid=(rows // tm,), in_specs=[spec] * 4, out_specs=[spec] * 3,
        out_shape=[jax.ShapeDtypeStruct((rows, cols), F32)] * 3,
        compiler_params=_params(("parallel",)),
    )(w, g, m, v)


def _me():
    return lax.axis_index("x"), lax.axis_index("y"), lax.axis_index("c")


def _hbm_specs(n):
    return [pl.BlockSpec(memory_space=pl.ANY)] * n


def _gather_weights(shards):
    n = len(shards)

    def body(*refs):
        src, dst = refs[:n], refs[n:2 * n]
        send_sems, recv_sems, local_sems = refs[2 * n:]
        x, y, c = _me()
        sibling = (x, y, 1 - c)
        chips = [(1 - x, y), (x, 1 - y), (1 - x, 1 - y)]

        def slot(i, layer, px, py):
            return dst[i].at[layer, 2 * px + py]

        locals_, first, passed = [], [], []
        for i in range(n):
            for layer in range(DEPTH):
                cp = pltpu.make_async_copy(src[i].at[layer], slot(i, layer, x, y), local_sems.at[DEPTH * i + layer])
                cp.start()
                locals_.append(cp)
            for j, chip in enumerate(chips):
                cp = pltpu.make_async_remote_copy(
                    src_ref=src[i].at[c], dst_ref=slot(i, c, x, y), send_sem=send_sems.at[6 * i + j],
                    recv_sem=recv_sems.at[6 * i + j], device_id=(*chip, c), device_id_type=MESH)
                cp.start()
                first.append(cp)

        def landed(i, j, layer, chip):
            return pltpu.make_async_remote_copy(
                src_ref=slot(i, layer, *chip), dst_ref=slot(i, layer, *chip), send_sem=send_sems.at[6 * i + j],
                recv_sem=recv_sems.at[6 * i + j], device_id=sibling, device_id_type=MESH)

        for i in range(n):
            for j, chip in enumerate(chips):
                landed(i, j, c, chip).wait_recv()
                cp = landed(i, 3 + j, c, chip)
                cp.start()
                passed.append(cp)
        for i in range(n):
            for j, chip in enumerate(chips):
                landed(i, 3 + j, 1 - c, chip).wait_recv()
        for cp in first + passed:
            cp.wait_send()
        for cp in locals_:
            cp.wait()

    out_shape = [jax.ShapeDtypeStruct((DEPTH, N_CHIPS) + s.shape[1:], s.dtype) for s in shards]
    return pl.pallas_call(
        body, name="gather_weights", in_specs=_hbm_specs(n), out_specs=_hbm_specs(n), out_shape=out_shape,
        scratch_shapes=[pltpu.SemaphoreType.DMA((6 * n,)), pltpu.SemaphoreType.DMA((6 * n,)),
                        pltpu.SemaphoreType.DMA((DEPTH * n,))],
    )(*shards)


def _all_gather_small(v):
    def body(v_ref, out_ref, send_sems, recv_sems, local_sem):
        x, y, c = _me()
        mine = pltpu.make_async_copy(v_ref, out_ref.at[4 * x + 2 * y + c], local_sem)
        mine.start()
        copies = []
        for k in range(1, N_DEV):
            peer = (x ^ ((k >> 2) & 1), y ^ ((k >> 1) & 1), c ^ (k & 1))
            cp = pltpu.make_async_remote_copy(
                src_ref=v_ref, dst_ref=out_ref.at[4 * x + 2 * y + c], send_sem=send_sems.at[k - 1],
                recv_sem=recv_sems.at[k - 1], device_id=peer, device_id_type=MESH)
            cp.start()
            copies.append(cp)
        for cp in copies:
            cp.wait_recv()
        for cp in copies:
            cp.wait_send()
        mine.wait()

    return pl.pallas_call(
        body, name="all_gather_small", in_specs=_hbm_specs(1), out_specs=pl.BlockSpec(memory_space=pl.ANY),
        out_shape=jax.ShapeDtypeStruct((N_DEV,) + v.shape, v.dtype),
        scratch_shapes=[pltpu.SemaphoreType.DMA((N_DEV - 1,)), pltpu.SemaphoreType.DMA((N_DEV - 1,)),
                        pltpu.SemaphoreType.DMA],
    )(v)


def _swap_layers(grads):
    n = len(grads)

    def body(*refs):
        src, dst = refs[:n], refs[n:2 * n]
        send_sems, recv_sems = refs[2 * n:]
        x, y, c = _me()
        copies = []
        for i in range(n):
            cp = pltpu.make_async_remote_copy(
                src_ref=src[i].at[1 - c], dst_ref=dst[i], send_sem=send_sems.at[i], recv_sem=recv_sems.at[i],
                device_id=(x, y, 1 - c), device_id_type=MESH)
            cp.start()
            copies.append(cp)
        for cp in copies:
            cp.wait_recv()
        for cp in copies:
            cp.wait_send()

    out_shape = [jax.ShapeDtypeStruct(g.shape[1:], g.dtype) for g in grads]
    return pl.pallas_call(
        body, name="swap_layers", in_specs=_hbm_specs(n), out_specs=_hbm_specs(n), out_shape=out_shape,
        scratch_shapes=[pltpu.SemaphoreType.DMA((n,)), pltpu.SemaphoreType.DMA((n,))],
    )(*grads)


def _scatter_chips(parts):
    n = len(parts)

    def body(*refs):
        src, dst = refs[:n], refs[n:2 * n]
        send_sems, recv_sems, local_sems = refs[2 * n:]
        x, y, c = _me()
        me = 2 * x + y
        chips = [(1 - x, y), (x, 1 - y), (1 - x, 1 - y)]
        copies, locals_ = [], []
        for i in range(n):
            cp = pltpu.make_async_copy(src[i].at[me], dst[i].at[me], local_sems.at[i])
            cp.start()
            locals_.append(cp)
            for j, (px, py) in enumerate(chips):
                cp = pltpu.make_async_remote_copy(
                    src_ref=src[i].at[2 * px + py], dst_ref=dst[i].at[me], send_sem=send_sems.at[3 * i + j],
                    recv_sem=recv_sems.at[3 * i + j], device_id=(px, py, c), device_id_type=MESH)
                cp.start()
                copies.append(cp)
        for cp in copies:
            cp.wait_recv()
        for cp in copies:
            cp.wait_send()
        for cp in locals_:
            cp.wait()

    out_shape = [jax.ShapeDtypeStruct(p.shape, p.dtype) for p in parts]
    return pl.pallas_call(
        body, name="scatter_chips", in_specs=_hbm_specs(n), out_specs=_hbm_specs(n), out_shape=out_shape,
        scratch_shapes=[pltpu.SemaphoreType.DMA((3 * n,)), pltpu.SemaphoreType.DMA((3 * n,)),
                        pltpu.SemaphoreType.DMA((n,))],
    )(*parts)


def _share_layers(totals):
    n = len(totals)

    def body(*refs):
        src, dst = refs[:n], refs[n:2 * n]
        send_sems, recv_sems, local_sems = refs[2 * n:]
        x, y, c = _me()
        copies, locals_ = [], []
        for i in range(n):
            cp = pltpu.make_async_copy(src[i], dst[i].at[c], local_sems.at[i])
            cp.start()
            locals_.append(cp)
            cp = pltpu.make_async_remote_copy(
                src_ref=src[i], dst_ref=dst[i].at[c], send_sem=send_sems.at[i], recv_sem=recv_sems.at[i],
                device_id=(x, y, 1 - c), device_id_type=MESH)
            cp.start()
            copies.append(cp)
        for cp in copies:
            cp.wait_recv()
        for cp in copies:
            cp.wait_send()
        for cp in locals_:
            cp.wait()

    out_shape = [jax.ShapeDtypeStruct((DEPTH,) + s.shape, s.dtype) for s in totals]
    return pl.pallas_call(
        body, name="share_layers", in_specs=_hbm_specs(n), out_specs=_hbm_specs(n), out_shape=out_shape,
        scratch_shapes=[pltpu.SemaphoreType.DMA((n,)), pltpu.SemaphoreType.DMA((n,)), pltpu.SemaphoreType.DMA((n,))],
    )(*totals)


def _flat2(a, lead):
    return a.reshape(a.shape[:lead] + (-1, a.shape[-1]))


def _pair_sum(name, own, recv, layer):
    _, nch, r, cdim = own.shape
    tm = _pick(r, max(8, (512 * 1024) // cdim // 16 * 16), 16)

    def body(l_ref, a_ref, b_ref, s_ref, s16_ref):
        s = a_ref[...] + b_ref[...]
        s_ref[...] = s
        s16_ref[...] = s.astype(BF16)

    spec = pl.BlockSpec((None, tm, cdim), lambda j, i, l: (j, i, 0))
    return pl.pallas_call(
        body, name=name,
        grid_spec=pltpu.PrefetchScalarGridSpec(
            num_scalar_prefetch=1, grid=(nch, r // tm),
            in_specs=[pl.BlockSpec((None, None, tm, cdim), lambda j, i, l: (l[0], j, i, 0)), spec],
            out_specs=[spec, spec]),
        out_shape=[jax.ShapeDtypeStruct((nch, r, cdim), F32), jax.ShapeDtypeStruct((nch, r, cdim), BF16)],
        compiler_params=_params(("parallel", "parallel")),
    )(layer, own, recv)


def _chip_sum(name, own, recv, chip):
    nch, r, cdim = own.shape
    tm = _pick(r, max(8, (512 * 1024) // cdim // 16 * 16), 16)

    def body(c_ref, own_ref, *rest):
        recv_refs, out_ref = rest[:nch], rest[nch]
        me = c_ref[0]
        acc = None
        for j in range(nch):
            term = jnp.where(me == j, own_ref[...], recv_refs[j][...].astype(F32))
            acc = term if acc is None else acc + term
        out_ref[...] = acc

    recv_specs = [pl.BlockSpec((None, tm, cdim), functools.partial(lambda i, c, j: (j, i, 0), j=j)) for j in range(nch)]
    return pl.pallas_call(
        body, name=name,
        grid_spec=pltpu.PrefetchScalarGridSpec(
            num_scalar_prefetch=1, grid=(r // tm,),
            in_specs=[pl.BlockSpec((None, tm, cdim), lambda i, c: (c[0], i, 0))] + recv_specs,
            out_specs=pl.BlockSpec((tm, cdim), lambda i, c: (i, 0))),
        out_shape=jax.ShapeDtypeStruct((r, cdim), F32),
        compiler_params=_params(("parallel",)),
    )(chip, own, *([recv] * nch))


def _sum_devices(gathered):
    _, r, cdim = gathered.shape

    def body(g_ref, o_ref):
        acc = g_ref[0]
        for k in range(1, N_DEV):
            acc = acc + g_ref[k]
        o_ref[...] = acc

    return pl.pallas_call(
        body, name="sum_devices", out_shape=jax.ShapeDtypeStruct((r, cdim), F32),
        compiler_params=_params(),
    )(gathered)


class _Layout:
    def __init__(self, d):
        self.d = d
        self.fw = d // 2
        self.fd = self.fw // FOX_HEADS
        self.gk = d // 2
        self.gv = d
        self.dk = self.gk // GLA_HEADS
        self.dv = self.gv // GLA_HEADS
        self.c_fq = 0
        self.c_gq = self.fw
        self.c_gv = self.c_gq + self.gk
        self.c_gr = self.c_gv + self.gv
        self.c_fkv = self.c_gr + self.gv
        self.c_gates = self.c_fkv + 2 * self.fw
        self.c_gk = self.c_gates + 2 * d
        self.c_small = self.c_gk + self.gk
        self.n_main = self.c_small
        self.n_p = self.c_small + LANES
        self.o_fk = self.fw
        self.o_fv = 2 * self.fw
        self.o_ff = 3 * self.fw
        self.o_gq = self.o_ff + FOX_HEADS
        self.o_gk = self.o_gq + self.gk
        self.o_gv = self.o_gk + self.gk
        self.o_gr = self.o_gv + self.gv
        self.o_ga = self.o_gr + self.gv
        self.o_gf = self.o_ga + GLA_RANK
        self.o_gg = self.o_gf + d
        self.n_orig = self.o_gg + d

    def to_p(self, w):
        kv = []
        for h in range(FOX_HEADS):
            kv.append(w[:, self.o_fk + h * self.fd:self.o_fk + (h + 1) * self.fd])
            kv.append(w[:, self.o_fv + h * self.fd:self.o_fv + (h + 1) * self.fd])
        small_pad = jnp.zeros((w.shape[0], LANES - FOX_HEADS - GLA_RANK), w.dtype)
        return jnp.concatenate(
            [w[:, :self.fw], w[:, self.o_gq:self.o_gk], w[:, self.o_gv:self.o_gr], w[:, self.o_gr:self.o_ga]]
            + kv + [w[:, self.o_gf:], w[:, self.o_gk:self.o_gv], w[:, self.o_ff:self.o_gq],
                    w[:, self.o_ga:self.o_gf], small_pad], axis=1)

    def from_segments(self, seg):
        kv = seg["fkv"].reshape(seg["fkv"].shape[0], FOX_HEADS, 2, self.fd)
        fk = kv[:, :, 0, :].reshape(-1, self.fw)
        fv = kv[:, :, 1, :].reshape(-1, self.fw)
        sm = seg["small"]
        return jnp.concatenate(
            [seg["fq"], fk, fv, sm[:, :FOX_HEADS], seg["gq"], seg["gk"], seg["gv"], seg["gr"],
             sm[:, FOX_HEADS:FOX_HEADS + GLA_RANK], seg["gates"]], axis=1)


def _layer_fwd(lay, h, p, t):
    d = lay.d
    xn = _rms_fwd("rms_mix_fwd", h, p["norm_mix_g"], t, d)
    proj = _mm("mm_proj", xn, p["w_in"], mode="nn", m=t, n=lay.n_main, k=d, out_dtypes=(BF16,))
    small = _mm("mm_small", xn, p["w_in"], mode="nn", m=t, n=LANES, k=d, b_c0=lay.c_small)
    cs = _fox_gate_fwd(small, p["b_forget_p"], t)
    ct = cs[:, :FOX_HEADS].T
    c_col, c_row = ct[:, :, None], ct[:, None, :]
    o_fox, lse = _fox_fwd(proj, c_col, c_row, t, lay.fd, lay.c_fq, lay.c_fkv)
    glog = _gla_gate_fwd(small, p["w_alpha_p"], p["b_alpha"], t, lay.gk)
    o_raw, s_prev = _gla_fwd(proj, glog, t, lay.dk, lay.dv, lay.c_gq, lay.c_gk, lay.c_gv)
    o_gla = _gla_post_fwd(o_raw, proj, p["gla_norm_g"], t, lay.dv, lay.c_gr)
    a_fox = _mm("mm_o_fox", o_fox, p["w_o_fox"], mode="nn", m=t, n=d, k=lay.fw)
    a_gla = _mm("mm_o_gla", o_gla, p["w_o_gla"], mode="nn", m=t, n=d, k=lay.gv)
    y = _merge_fwd(a_fox, a_gla, proj, lay.c_gates, t, d)
    h1 = _mm("mm_out", y, p["w_out"], mode="nn", m=t, n=d, k=d, extras=[(h, 0)],
             epilogue=lambda acc, res: (res + acc,))
    xn2 = _rms_fwd("rms_mlp_fwd", h1, p["norm_mlp_g"], t, d)
    u, act = _mm("mm_ff1", xn2, p["w_ff1"], mode="nn", m=t, n=4 * d, k=d, out_dtypes=(BF16, BF16),
                 epilogue=lambda acc: (acc, jnp.square(jnp.maximum(acc, 0.0))))
    h2 = _mm("mm_ff2", act, p["w_ff2"], mode="nn", m=t, n=d, k=4 * d, extras=[(h1, 0)],
             epilogue=lambda acc, res: (res + acc,))
    saved = dict(h=h, xn=xn, proj=proj, small=small, c_col=c_col, c_row=c_row, o_fox=o_fox, lse=lse, glog=glog,
                 o_raw=o_raw, s_prev=s_prev, o_gla=o_gla, a_fox=a_fox, a_gla=a_gla, y=y, h1=h1, xn2=xn2, u=u, act=act)
    return h2, saved


def _layer_bwd(lay, dh2, p, s, t):
    d = lay.d
    g = {}
    du = _mm("mm_dact", dh2, p["w_ff2"], mode="nt", m=t, n=4 * d, k=d, extras=[(s["u"], 0)], out_dtypes=(BF16,),
             epilogue=lambda acc, u: (acc * (2.0 * jnp.maximum(u.astype(F32), 0.0)),))
    g["w_ff2"] = _mm("mm_dw_ff2", s["act"], dh2, mode="tn", m=4 * d, n=d, k=t)
    g["w_ff1"] = _mm("mm_dw_ff1", s["xn2"], du, mode="tn", m=d, n=4 * d, k=t)
    dxn2 = _mm("mm_dxn2", du, p["w_ff1"], mode="nt", m=t, n=d, k=4 * d)
    dh1, g["norm_mlp_g"] = _rms_bwd("rms_mlp_bwd", s["h1"], p["norm_mlp_g"], dxn2, dh2, t, d)
    dy = _mm("mm_dy", dh1, p["w_out"], mode="nt", m=t, n=d, k=d)
    g["w_out"] = _mm("mm_dw_out", s["y"], dh1, mode="tn", m=d, n=d, k=t)
    da_fox, da_gla, dgates = _merge_bwd(dy, s["a_fox"], s["a_gla"], s["proj"], lay.c_gates, t, d)
    g["w_o_fox"] = _mm("mm_dw_o_fox", s["o_fox"], da_fox, mode="tn", m=lay.fw, n=d, k=t)
    do_fox = _mm("mm_do_fox", da_fox, p["w_o_fox"], mode="nt", m=t, n=lay.fw, k=d)
    g["w_o_gla"] = _mm("mm_dw_o_gla", s["o_gla"], da_gla, mode="tn", m=lay.gv, n=d, k=t)
    do_gla = _mm("mm_do_gla", da_gla, p["w_o_gla"], mode="nt", m=t, n=lay.gv, k=d)
    do_raw, dgr, g["gla_norm_g"] = _gla_post_bwd(s["o_raw"], s["proj"], p["gla_norm_g"], do_gla, t, lay.dv, lay.c_gr)
    dgq, dgk, dgv, dglog = _gla_bwd(s["proj"], s["glog"], s["s_prev"], do_raw, t, lay.dk, lay.dv,
                                    lay.c_gq, lay.c_gk, lay.c_gv)
    dz, g["b_alpha"] = _gla_gate_bwd(dglog, s["small"], p["w_alpha_p"], p["b_alpha"], t, lay.gk)
    g["w_alpha_p"] = _mm("mm_dw_alpha", s["small"], dz, mode="tn", m=LANES, n=lay.gk, k=t)
    dga = _mm("mm_dga", dz, p["w_alpha_p"], mode="nt", m=t, n=LANES, k=lay.gk)
    delta = _fox_delta(s["proj"], s["c_col"], s["c_row"], s["lse"], do_fox, t, lay.fd, lay.c_fq, lay.c_fkv)
    dfq = _fox_bwd_dq(s["proj"], s["c_col"], s["c_row"], s["lse"], delta, do_fox, t, lay.fd, lay.c_fq, lay.c_fkv)
    dfkv, dc = _fox_bwd_dkv(s["proj"], s["c_col"], s["c_row"], s["lse"], delta, do_fox, t, lay.fd, lay.c_fq, lay.c_fkv)
    dc_p = jnp.pad(dc[:, 0, :].T, ((0, 0), (0, LANES - FOX_HEADS)))
    dsmall, g["b_forget_p"] = _fox_gate_bwd(dc_p, s["small"], p["b_forget_p"], dga, t)
    segs = [("fq", dfq, lay.c_fq), ("gq", dgq, lay.c_gq), ("gv", dgv, lay.c_gv), ("gr", dgr, lay.c_gr),
            ("fkv", dfkv, lay.c_fkv), ("gates", dgates, lay.c_gates), ("gk", dgk, lay.c_gk),
            ("small", dsmall, lay.c_small)]
    dxn = None
    dw_in = {}
    for nm, dseg, c0 in segs:
        width = dseg.shape[1]
        dw_in[nm] = _mm("mm_dw_in_" + nm, s["xn"], dseg, mode="tn", m=d, n=width, k=t)
        if dxn is None:
            dxn = _mm("mm_dxn_" + nm, dseg, p["w_in"], mode="nt", m=t, n=d, k=width, b_c0=c0)
        else:
            dxn = _mm("mm_dxn_" + nm, dseg, p["w_in"], mode="nt", m=t, n=d, k=width, b_c0=c0, extras=[(dxn, 0)],
                      epilogue=lambda acc, prev: (prev + acc,))
    g["w_in"] = lay.from_segments(dw_in)
    dh, g["norm_mix_g"] = _rms_bwd("rms_mix_bwd", s["h"], p["norm_mix_g"], dxn, dh1, t, d)
    return dh, g


def _sequence_step(x, target, meta, layers, final_g):
    seq, d = x.shape
    t = seq + ROW0
    lay = _Layout(d)
    h = jnp.pad(x, ((ROW0, 0), (0, 0))).at[PAD:ROW0].set(meta)
    target_p = jnp.pad(target, ((ROW0, 0), (0, 0)))
    saved = []
    for p in layers:
        h, s = _layer_fwd(lay, h, p, t)
        saved.append(s)
    dh, dg_final, loss_part = _loss_head(h, final_g, target_p, t, d)
    grads = [None] * len(layers)
    for l in reversed(range(len(layers))):
        dh, grads[l] = _layer_bwd(lay, dh, layers[l], saved[l], t)
    return loss_part, dh[ROW0:], dh[PAD:ROW0], grads, dg_final


_SMALL_ROWS = 32


def _pack_small(d, meta, mix, gla, mlp, final, b_alpha, b_forget):
    rows = [meta.reshape(N_META, d), mix.reshape(DEPTH, d), gla.reshape(DEPTH, d), mlp.reshape(DEPTH, d),
            final.reshape(1, d), b_alpha.reshape(1, d),
            jnp.pad(b_forget.reshape(1, DEPTH * FOX_HEADS), ((0, 0), (0, d - DEPTH * FOX_HEADS)))]
    packed = jnp.concatenate(rows, axis=0)
    return jnp.pad(packed, ((0, _SMALL_ROWS - packed.shape[0]), (0, 0)))


def _unpack_small(d, packed):
    return dict(meta=packed[:N_META], norm_mix_g=packed[16:18], gla_norm_g=packed[18:20], norm_mlp_g=packed[20:22],
                final_norm_g=packed[22], b_alpha=packed[23].reshape(DEPTH, d // 2),
                b_forget=packed[24, :DEPTH * FOX_HEADS].reshape(DEPTH, FOX_HEADS))


_BIG = ("w_in", "w_alpha2", "w_o_fox", "w_o_gla", "w_out", "w_ff1", "w_ff2")
_COL_SHARDED = ("w_in", "w_alpha2", "w_o_fox", "w_ff1")


def _full_matrix(name, gathered_l):
    nch, r, c = gathered_l.shape
    if name in _COL_SHARDED:
        return gathered_l.transpose(1, 0, 2).reshape(r, nch * c)
    return gathered_l.reshape(nch * r, c)


def _shard_major(name, full):
    r, c = full.shape
    if name in _COL_SHARDED:
        return full.reshape(r, N_CHIPS, c // N_CHIPS).transpose(1, 0, 2)
    return full.reshape(N_CHIPS, r // N_CHIPS, c)


def kernel(x, meta_tokens, norm_mix_g, w_in, b_forget, w_alpha2, b_alpha, gla_norm_g, w_o_fox, w_o_gla, w_out, norm_mlp_g, w_ff1, w_ff2, final_norm_g, loss_target, m_meta_tokens, m_norm_mix_g, m_w_in, m_b_forget, m_w_alpha2, m_b_alpha, m_gla_norm_g, m_w_o_fox, m_w_o_gla, m_w_out, m_norm_mlp_g, m_w_ff1, m_w_ff2, m_final_norm_g, v_meta_tokens, v_norm_mix_g, v_w_in, v_b_forget, v_w_alpha2, v_b_alpha, v_gla_norm_g, v_w_o_fox, v_w_o_gla, v_w_out, v_norm_mlp_g, v_w_ff1, v_w_ff2, v_final_norm_g):
    seq, d = x.shape[1], x.shape[2]
    lay = _Layout(d)
    xi, yi, ci = lax.axis_index("x"), lax.axis_index("y"), lax.axis_index("c")
    chip = (2 * xi + yi).astype(jnp.int32)
    w = dict(w_in=w_in, w_alpha2=w_alpha2, w_o_fox=w_o_fox, w_o_gla=w_o_gla, w_out=w_out, w_ff1=w_ff1, w_ff2=w_ff2)
    m = dict(w_in=m_w_in, w_alpha2=m_w_alpha2, w_o_fox=m_w_o_fox, w_o_gla=m_w_o_gla, w_out=m_w_out, w_ff1=m_w_ff1,
             w_ff2=m_w_ff2)
    v = dict(w_in=v_w_in, w_alpha2=v_w_alpha2, w_o_fox=v_w_o_fox, w_o_gla=v_w_o_gla, w_out=v_w_out, w_ff1=v_w_ff1,
             w_ff2=v_w_ff2)

    gathered = _gather_weights([w[n].astype(BF16) for n in _BIG])
    meta_all = _all_gather_small(meta_tokens)
    meta_full = meta_all[0::2].transpose(1, 0, 2).reshape(N_META, d)
    layers = []
    for l in range(DEPTH):
        full = {n: _full_matrix(n, g[l]) for n, g in zip(_BIG, gathered)}
        w_alpha_p = jnp.zeros((LANES, lay.gk), BF16).at[FOX_HEADS:FOX_HEADS + GLA_RANK].set(full["w_alpha2"])
        layers.append(dict(
            w_in=lay.to_p(full["w_in"]), w_alpha_p=w_alpha_p, w_o_fox=full["w_o_fox"], w_o_gla=full["w_o_gla"],
            w_out=full["w_out"], w_ff1=full["w_ff1"], w_ff2=full["w_ff2"],
            norm_mix_g=norm_mix_g[l][None], norm_mlp_g=norm_mlp_g[l][None], gla_norm_g=gla_norm_g[l][None],
            b_alpha=b_alpha[l][None],
            b_forget_p=jnp.pad(b_forget[l][None], ((0, 0), (0, LANES - FOX_HEADS)))))

    loss_part, grad_x, d_meta, grads, dg_final = _sequence_step(x[0], loss_target[0], meta_full, layers,
                                                                final_norm_g[None])
    loss = lax.psum(loss_part[0, 0], ("x", "y", "c"))

    stack = lambda key: jnp.concatenate([grads[l][key] for l in range(DEPTH)], axis=0)
    b_forget_g = jnp.concatenate([grads[l]["b_forget_p"][:, :FOX_HEADS] for l in range(DEPTH)], axis=0)
    packed = _pack_small(d, d_meta, stack("norm_mix_g"), stack("gla_norm_g"), stack("norm_mlp_g"), dg_final,
                         stack("b_alpha"), b_forget_g)
    small_g = _unpack_small(d, _sum_devices(_all_gather_small(packed)))
    small_g["meta"] = lax.dynamic_slice_in_dim(small_g["meta"], chip * (d // N_CHIPS), d // N_CHIPS, axis=1)

    part = []
    for n in _BIG:
        per_layer = []
        for l in range(DEPTH):
            full = grads[l]["w_alpha_p"][FOX_HEADS:FOX_HEADS + GLA_RANK] if n == "w_alpha2" else grads[l][n]
            per_layer.append(_shard_major(n, full))
        part.append(jnp.stack(per_layer))
    from_sibling = _swap_layers(part)
    layer_idx = ci.astype(jnp.int32)[None]
    sums = [_pair_sum("pair_sum_" + n, p_, r_, layer_idx) for n, p_, r_ in zip(_BIG, part, from_sibling)]
    landed = _scatter_chips([s16 for _, s16 in sums])
    chip_idx = chip[None]
    totals = [_chip_sum("chip_sum_" + n, s32, r_, chip_idx) for n, (s32, _), r_ in zip(_BIG, sums, landed)]
    big_g = dict(zip(_BIG, _share_layers(totals)))

    out_g, out_d, out_m, out_v = {}, {}, {}, {}
    for n in _BIG:
        shape = w[n].shape
        res = _adamw("adamw_" + n, _flat2(w[n], 0), _flat2(big_g[n], 0), _flat2(m[n], 0), _flat2(v[n], 0))
        out_g[n] = big_g[n].reshape(shape)
        out_d[n], out_m[n], out_v[n] = [r.reshape(shape) for r in res]
    sm_w = dict(meta_tokens=meta_tokens, norm_mix_g=norm_mix_g, b_forget=b_forget, b_alpha=b_alpha,
                gla_norm_g=gla_norm_g, norm_mlp_g=norm_mlp_g, final_norm_g=final_norm_g)
    sm_m = dict(meta_tokens=m_meta_tokens, norm_mix_g=m_norm_mix_g, b_forget=m_b_forget, b_alpha=m_b_alpha,
                gla_norm_g=m_gla_norm_g, norm_mlp_g=m_norm_mlp_g, final_norm_g=m_final_norm_g)
    sm_v = dict(meta_tokens=v_meta_tokens, norm_mix_g=v_norm_mix_g, b_forget=v_b_forget, b_alpha=v_b_alpha,
                gla_norm_g=v_gla_norm_g, norm_mlp_g=v_norm_mlp_g, final_norm_g=v_final_norm_g)
    sm_g = dict(meta_tokens=small_g["meta"], norm_mix_g=small_g["norm_mix_g"], b_forget=small_g["b_forget"],
                b_alpha=small_g["b_alpha"], gla_norm_g=small_g["gla_norm_g"], norm_mlp_g=small_g["norm_mlp_g"],
                final_norm_g=small_g["final_norm_g"])
    names_small = list(sm_w)
    sizes = [sm_w[n].size for n in names_small]
    width = 512
    total = -(-sum(sizes) // (8 * width)) * (8 * width)

    def pack_flat(dct, fill):
        flat = jnp.concatenate([dct[n].reshape(-1) for n in names_small])
        return jnp.pad(flat, (0, total - flat.shape[0]), constant_values=fill).reshape(-1, width)

    res = _adamw("adamw_small", pack_flat(sm_w, 0.0), pack_flat(sm_g, 0.0), pack_flat(sm_m, 0.0), pack_flat(sm_v, 1.0))
    offs = [0]
    for sz in sizes:
        offs.append(offs[-1] + sz)
    for i, n in enumerate(names_small):
        out_g[n] = sm_g[n].reshape(sm_w[n].shape)
        out_d[n], out_m[n], out_v[n] = [r.reshape(-1)[offs[i]:offs[i + 1]].reshape(sm_w[n].shape) for r in res]

    order = ["meta_tokens", "norm_mix_g", "w_in", "b_forget", "w_alpha2", "b_alpha", "gla_norm_g", "w_o_fox",
             "w_o_gla", "w_out", "norm_mlp_g", "w_ff1", "w_ff2", "final_norm_g"]
    return (loss, grad_x[None], *[out_g[n] for n in order], *[out_d[n] for n in order],
            *[out_m[n] for n in order], *[out_v[n] for n in order])
```

```python
import functools

import numpy as np

import jax
import jax.numpy as jnp
from jax import lax
from jax.experimental import pallas as pl
from jax.experimental.pallas import tpu as pltpu

F32 = jnp.float32
BF16 = jnp.bfloat16

N_META = 16
PAD = 112
ROW0 = PAD + N_META
EPS = 1e-6
MASK_VALUE = -1e30
FOX_HEADS = 8
FOX_GROUP = 2
GLA_HEADS = 4
GLA_RANK = 16
GLA_TAU = 16.0
GLA_CHUNK = 64
DEPTH = 2
N_CHIPS = 4
N_DEV = 8

ADAM_LR = 0.001
ADAM_B1 = 0.9
ADAM_B2 = 0.999
ADAM_EPS = 1e-08
ADAM_WD = 0.01
ADAM_STEP = 10

LANES = 128
VMEM_LIMIT = 56 * 1024 * 1024
MESH = pl.DeviceIdType.MESH


def _pick(n, target, mult):
    best = None
    for d in range(mult, min(n, target) + 1, mult):
        if n % d == 0:
            best = d
    return n if best is None else best


def _params(sem=None):
    return pltpu.CompilerParams(dimension_semantics=sem, vmem_limit_bytes=VMEM_LIMIT)


def _bf(v):
    return v if v.dtype == BF16 else v.astype(BF16)


def _sigmoid(z):
    return 1.0 / (1.0 + jnp.exp(-z))


def _log_sigmoid(z):
    return jnp.minimum(z, 0.0) - jnp.log(1.0 + jnp.exp(-jnp.abs(z)))


def _split3(v):
    a = v.astype(BF16)
    r = v - a.astype(F32)
    b = r.astype(BF16)
    c = (r - b.astype(F32)).astype(BF16)
    return a, b, c


def _dot(a, b, dims):
    return lax.dot_general(a, b, (dims, ((), ())), preferred_element_type=F32)


NN = ((1,), (0,))
NT = ((1,), (1,))
TN = ((0,), (0,))


def _tri_dot(tri, v, dims=NN):
    a, b, c = _split3(v)
    return _dot(tri, a, dims) + _dot(tri, b, dims) + _dot(tri, c, dims)


def _mm(name, a, b, *, mode, m, n, k, b_c0=0, extras=(), epilogue=None, out_dtypes=(F32,),
        b_shards=1, out_shards=1, tm=1056, tn=1024, tk=2048):
    tm = _pick(m, tm, LANES if mode == "tn" else 16)
    tn = _pick(n // max(b_shards if mode == "nn" else 1, out_shards), tn, LANES)
    if mode == "tn":
        tk = _pick(k, 2112, 16)
    else:
        tk = _pick(k // (b_shards if mode == "nt" else 1), tk, LANES)
    assert b_c0 % (tk if mode == "nt" else tn) == 0 and (b_shards == 1 or b_c0 == 0)
    nk = k // tk
    if mode == "tn":
        a_spec = pl.BlockSpec((tk, tm), lambda i, j, kk: (kk, i))
    else:
        a_spec = pl.BlockSpec((tm, tk), lambda i, j, kk: (i, kk))
    if mode == "nt":
        dims = NT
        if b_shards > 1:
            per = (k // b_shards) // tk
            b_spec = pl.BlockSpec((None, tn, tk), lambda i, j, kk: (kk // per, j, kk % per))
        else:
            b_spec = pl.BlockSpec((tn, tk), lambda i, j, kk: (j, kk + b_c0 // tk))
    else:
        dims = NN if mode == "nn" else TN
        if b_shards > 1:
            per = (n // b_shards) // tn
            b_spec = pl.BlockSpec((None, tk, tn), lambda i, j, kk: (j // per, kk, j % per))
        else:
            b_spec = pl.BlockSpec((tk, tn), lambda i, j, kk: (kk, j + b_c0 // tn))
    ex_specs = [pl.BlockSpec((tm, tn), lambda i, j, kk: (i, j)) for _ in extras]
    if out_shards > 1:
        oper = (n // out_shards) // tn
        out_specs = [pl.BlockSpec((None, tm, tn), lambda i, j, kk: (j // oper, i, j % oper)) for _ in out_dtypes]
        out_shape = [jax.ShapeDtypeStruct((out_shards, m, n // out_shards), dt) for dt in out_dtypes]
    else:
        out_specs = [pl.BlockSpec((tm, tn), lambda i, j, kk: (i, j)) for _ in out_dtypes]
        out_shape = [jax.ShapeDtypeStruct((m, n), dt) for dt in out_dtypes]
    n_ex = len(extras)
    n_out = len(out_dtypes)

    def finish(acc, ex_refs, out_refs):
        vals = (acc,) if epilogue is None else epilogue(acc, *[r[...] for r in ex_refs])
        for r, v in zip(out_refs, vals):
            r[...] = v.astype(r.dtype)

    def body(a_ref, b_ref, *rest):
        ex_refs = rest[:n_ex]
        out_refs = rest[n_ex:n_ex + n_out]
        prod = _dot(_bf(a_ref[...]), _bf(b_ref[...]), dims)
        if nk == 1:
            finish(prod, ex_refs, out_refs)
            return
        acc_ref = rest[n_ex + n_out]
        kk = pl.program_id(2)

        @pl.when(kk == 0)
        def _():
            acc_ref[...] = prod

        @pl.when((kk > 0) & (kk < nk - 1))
        def _():
            acc_ref[...] += prod

        @pl.when(kk == nk - 1)
        def _():
            finish(acc_ref[...] + prod, ex_refs, out_refs)

    outs = pl.pallas_call(
        body,
        name=name,
        grid=(m // tm, n // tn, nk),
        in_specs=[a_spec, b_spec] + ex_specs,
        out_specs=out_specs,
        out_shape=out_shape,
        scratch_shapes=[pltpu.VMEM((tm, tn), F32)] if nk > 1 else [],
        compiler_params=_params(("parallel", "parallel", "arbitrary")),
    )(a, b, *extras)
    return outs[0] if n_out == 1 else outs


def _ew(name, fn, ins, outs, rows, tm):
    tm = _pick(rows, tm, 16)
    in_specs, args = [], []
    for spec in ins:
        if spec[0] == "tile":
            _, arr, width, c0 = spec
            assert c0 % width == 0
            in_specs.append(pl.BlockSpec((tm, width), functools.partial(lambda i, o: (i, o), o=c0 // width)))
        else:
            arr = spec[1]
            in_specs.append(pl.BlockSpec(arr.shape, lambda i: (0, 0)))
        args.append(arr)
    out_specs, out_shape = [], []
    for kind, dt, width in outs:
        if kind == "tile":
            out_specs.append(pl.BlockSpec((tm, width), lambda i: (i, 0)))
            out_shape.append(jax.ShapeDtypeStruct((rows, width), dt))
        else:
            out_specs.append(pl.BlockSpec((1, width), lambda i: (0, 0)))
            out_shape.append(jax.ShapeDtypeStruct((1, width), dt))
    n_in = len(ins)
    has_acc = any(o[0] == "acc" for o in outs)

    def body(*refs):
        i = pl.program_id(0)
        vals = fn(i * tm, *[r[...] for r in refs[:n_in]])
        for (kind, _, _), r, v in zip(outs, refs[n_in:], vals):
            if kind == "tile":
                r[...] = v.astype(r.dtype)
            else:
                @pl.when(i == 0)
                def _():
                    r[...] = jnp.zeros_like(r)

                r[...] += v.astype(r.dtype)

    res = pl.pallas_call(
        body,
        name=name,
        grid=(rows // tm,),
        in_specs=in_specs,
        out_specs=out_specs,
        out_shape=out_shape,
        compiler_params=_params(("arbitrary",) if has_acc else ("parallel",)),
    )(*args)
    return res[0] if len(outs) == 1 else res


def _row_ids(row0, tm):
    return row0 + lax.broadcasted_iota(jnp.int32, (tm, 1), 0)


def _colsum(v):
    return jnp.sum(v, axis=0, keepdims=True)


def _rms_fwd(name, h, g, t, d):
    def fn(row0, x, gg):
        r = lax.rsqrt(jnp.mean(x * x, axis=-1, keepdims=True) + EPS)
        return (x * r * gg,)

    return _ew(name, fn, [("tile", h, d, 0), ("full", g)], [("tile", BF16, d)], t, 264)


def _rms_bwd(name, h, g, dy, dres, t, d):
    def fn(row0, x, gg, dyv, dr):
        r = lax.rsqrt(jnp.mean(x * x, axis=-1, keepdims=True) + EPS)
        xh = x * r
        dxh = dyv * gg
        dx = r * (dxh - xh * jnp.mean(dxh * xh, axis=-1, keepdims=True))
        out = jnp.where(_row_ids(row0, x.shape[0]) >= PAD, dr + dx, 0.0)
        return out, _colsum(dyv * xh)

    return _ew(name, fn, [("tile", h, d, 0), ("full", g), ("tile", dy, d, 0), ("tile", dres, d, 0)],
               [("tile", F32, d), ("acc", F32, d)], t, 264)


def _loss_head(h, g, target_p, t, d):
    def fn(row0, x, gg, tgt):
        real = _row_ids(row0, x.shape[0]) >= ROW0
        r = lax.rsqrt(jnp.mean(x * x, axis=-1, keepdims=True) + EPS)
        xh = x * r
        err = jnp.where(real, xh * gg - tgt, 0.0)
        loss_rows = 0.5 * jnp.mean(err * err, axis=-1, keepdims=True)
        dyv = err * (1.0 / d)
        dxh = dyv * gg
        dx = r * (dxh - xh * jnp.mean(dxh * xh, axis=-1, keepdims=True))
        loss_part = jnp.sum(loss_rows, axis=0, keepdims=True) * jnp.ones((1, LANES), F32)
        return jnp.where(real, dx, 0.0), _colsum(dyv * xh), loss_part

    return _ew("loss_head", fn, [("tile", h, d, 0), ("full", g), ("tile", target_p, d, 0)],
               [("tile", F32, d), ("acc", F32, d), ("acc", F32, LANES)], t, 264)


def _merge_fwd(a_fox, a_gla, proj, c_gates, t, d):
    def fn(row0, af, ag, gates):
        gates = gates.astype(F32)
        return (_sigmoid(gates[:, :d]) * af + _sigmoid(gates[:, d:]) * ag,)

    return _ew("merge_fwd", fn, [("tile", a_fox, d, 0), ("tile", a_gla, d, 0), ("tile", proj, 2 * d, c_gates)],
               [("tile", BF16, d)], t, 264)


def _merge_bwd(dy, a_fox, a_gla, proj, c_gates, t, d):
    def fn(row0, dyv, af, ag, gates):
        gates = gates.astype(F32)
        sf = _sigmoid(gates[:, :d])
        sg = _sigmoid(gates[:, d:])
        dgates = jnp.concatenate([dyv * af * sf * (1.0 - sf), dyv * ag * sg * (1.0 - sg)], axis=1)
        return dyv * sf, dyv * sg, dgates

    return _ew("merge_bwd", fn,
               [("tile", dy, d, 0), ("tile", a_fox, d, 0), ("tile", a_gla, d, 0), ("tile", proj, 2 * d, c_gates)],
               [("tile", BF16, d), ("tile", BF16, d), ("tile", BF16, 2 * d)], t, 264)


def _fox_gate_fwd(small, b_forget_p, t):
    tb = _pick(t, 384, LANES)

    def body(s_ref, b_ref, c_ref, carry_ref):
        i = pl.program_id(0)

        @pl.when(i == 0)
        def _():
            carry_ref[...] = jnp.zeros_like(carry_ref)

        logf = _log_sigmoid(s_ref[...] + b_ref[...])
        logf = jnp.where(_row_ids(i * tb, tb) >= PAD, logf, 0.0)
        r = lax.broadcasted_iota(jnp.int32, (tb, tb), 0)
        c = lax.broadcasted_iota(jnp.int32, (tb, tb), 1)
        tri = (c <= r).astype(BF16)
        cs = _tri_dot(tri, logf) + carry_ref[...]
        c_ref[...] = cs
        carry_ref[...] = cs[tb - 1:tb, :]

    return pl.pallas_call(
        body, name="fox_gate_fwd", grid=(t // tb,),
        in_specs=[pl.BlockSpec((tb, LANES), lambda i: (i, 0)), pl.BlockSpec((1, LANES), lambda i: (0, 0))],
        out_specs=pl.BlockSpec((tb, LANES), lambda i: (i, 0)),
        out_shape=jax.ShapeDtypeStruct((t, LANES), F32),
        scratch_shapes=[pltpu.VMEM((1, LANES), F32)],
        compiler_params=_params(("arbitrary",)),
    )(small, b_forget_p)


def _fox_gate_bwd(dc, small, b_forget_p, dga, t):
    tb = _pick(t, 384, LANES)
    nb = t // tb

    def body(dc_ref, s_ref, b_ref, dga_ref, ds_ref, db_ref, carry_ref):
        i = pl.program_id(0)

        @pl.when(i == 0)
        def _():
            carry_ref[...] = jnp.zeros_like(carry_ref)
            db_ref[...] = jnp.zeros_like(db_ref)

        r = lax.broadcasted_iota(jnp.int32, (tb, tb), 0)
        c = lax.broadcasted_iota(jnp.int32, (tb, tb), 1)
        tri = (c >= r).astype(BF16)
        dlogf = _tri_dot(tri, dc_ref[...]) + carry_ref[...]
        carry_ref[...] = dlogf[0:1, :]
        z = s_ref[...] + b_ref[...]
        dff = dlogf * _sigmoid(-z)
        lane = lax.broadcasted_iota(jnp.int32, (tb, LANES), 1)
        keep = (_row_ids((nb - 1 - i) * tb, tb) >= PAD) & (lane < FOX_HEADS)
        dff = jnp.where(keep, dff, 0.0)
        ds_ref[...] = dff + dga_ref[...]
        db_ref[...] += _colsum(dff)

    rev = lambda i: (nb - 1 - i, 0)
    return pl.pallas_call(
        body, name="fox_gate_bwd", grid=(nb,),
        in_specs=[pl.BlockSpec((tb, LANES), rev), pl.BlockSpec((tb, LANES), rev),
                  pl.BlockSpec((1, LANES), lambda i: (0, 0)), pl.BlockSpec((tb, LANES), rev)],
        out_specs=[pl.BlockSpec((tb, LANES), rev), pl.BlockSpec((1, LANES), lambda i: (0, 0))],
        out_shape=[jax.ShapeDtypeStruct((t, LANES), F32), jax.ShapeDtypeStruct((1, LANES), F32)],
        scratch_shapes=[pltpu.VMEM((1, LANES), F32)],
        compiler_params=_params(("arbitrary",)),
    )(dc, small, b_forget_p, dga)


def _fox_pairs(nb, by_key):
    if by_key:
        pairs = [(qi, ki) for ki in range(nb) for qi in range(ki, nb)]
    else:
        pairs = [(qi, ki) for qi in range(nb) for ki in range(qi + 1)]
    return (jnp.asarray(np.array([p[0] for p in pairs], np.int32)),
            jnp.asarray(np.array([p[1] for p in pairs], np.int32)), len(pairs))


def _fox_specs(tb, fd, c_fq, c_fkv):
    gw = FOX_GROUP * fd
    q0, kv0 = c_fq // gw, c_fkv // (2 * gw)
    return dict(
        q=pl.BlockSpec((tb, gw), lambda g, p, qt, kt: (qt[p], q0 + g)),
        kv=pl.BlockSpec((tb, 2 * gw), lambda g, p, qt, kt: (kt[p], kv0 + g)),
        col=pl.BlockSpec((FOX_GROUP, tb, 1), lambda g, p, qt, kt: (g, qt[p], 0)),
        row=pl.BlockSpec((FOX_GROUP, 1, tb), lambda g, p, qt, kt: (g, 0, kt[p])),
        head=pl.BlockSpec((tb, gw), lambda g, p, qt, kt: (qt[p], g)),
        key_kv=pl.BlockSpec((tb, 2 * gw), lambda g, p, qt, kt: (kt[p], g)),
    )


def _fox_mask(qi, ki, tb):
    row = qi * tb + lax.broadcasted_iota(jnp.int32, (tb, tb), 0)
    col = ki * tb + lax.broadcasted_iota(jnp.int32, (tb, tb), 1)
    return (col <= row) & (col >= PAD)


def _fox_heads(q_ref, kv_ref, fd):
    return [(q_ref[:, hh * fd:(hh + 1) * fd], kv_ref[:, 2 * hh * fd:(2 * hh + 1) * fd],
             kv_ref[:, (2 * hh + 1) * fd:(2 * hh + 2) * fd]) for hh in range(FOX_GROUP)]


def _fox_fwd(proj, c_col, c_row, t, fd, c_fq, c_fkv):
    tb = _pick(t, 384, LANES)
    nb = t // tb
    scale = fd ** -0.5
    sp = _fox_specs(tb, fd, c_fq, c_fkv)
    qt, kt, npairs = _fox_pairs(nb, by_key=False)

    def body(qt_ref, kt_ref, q_ref, kv_ref, cq_ref, ck_ref, o_ref, lse_ref, m_ref, l_ref, acc_ref):
        p = pl.program_id(1)
        qi, ki = qt_ref[p], kt_ref[p]

        @pl.when(ki == 0)
        def _():
            m_ref[...] = jnp.full_like(m_ref, -jnp.inf)
            l_ref[...] = jnp.zeros_like(l_ref)
            acc_ref[...] = jnp.zeros_like(acc_ref)

        def update(masked):
            mask = _fox_mask(qi, ki, tb) if masked else None
            for hh, (q, k, v) in enumerate(_fox_heads(q_ref, kv_ref, fd)):
                s = _dot(q, k, NT) * scale + cq_ref[hh] - ck_ref[hh]
                if masked:
                    s = jnp.where(mask, s, MASK_VALUE)
                m_prev = m_ref[hh]
                m_new = jnp.maximum(m_prev, jnp.max(s, axis=-1, keepdims=True))
                alpha = jnp.exp(m_prev - m_new)
                pe = jnp.exp(s - m_new)
                l_ref[hh] = alpha * l_ref[hh] + jnp.sum(pe, axis=-1, keepdims=True)
                acc_ref[hh] = alpha * acc_ref[hh] + _dot(pe.astype(BF16), v, NN)
                m_ref[hh] = m_new

        edge = (ki == 0) | (ki == qi)
        pl.when(edge)(functools.partial(update, True))
        pl.when(jnp.logical_not(edge))(functools.partial(update, False))

        @pl.when(ki == qi)
        def _():
            real = _row_ids(qi * tb, tb) >= PAD
            for hh in range(FOX_GROUP):
                o_ref[:, hh * fd:(hh + 1) * fd] = jnp.where(real, acc_ref[hh] / l_ref[hh], 0.0)
                lse_ref[hh] = m_ref[hh] + jnp.log(l_ref[hh])

    return pl.pallas_call(
        body, name="fox_fwd",
        grid_spec=pltpu.PrefetchScalarGridSpec(
            num_scalar_prefetch=2, grid=(FOX_HEADS // FOX_GROUP, npairs),
            in_specs=[sp["q"], sp["kv"], sp["col"], sp["row"]],
            out_specs=[sp["head"], sp["col"]],
            scratch_shapes=[pltpu.VMEM((FOX_GROUP, tb, 1), F32), pltpu.VMEM((FOX_GROUP, tb, 1), F32),
                            pltpu.VMEM((FOX_GROUP, tb, fd), F32)]),
        out_shape=[jax.ShapeDtypeStruct((t, FOX_HEADS * fd), F32), jax.ShapeDtypeStruct((FOX_HEADS, t, 1), F32)],
        compiler_params=_params(("parallel", "arbitrary")),
    )(qt, kt, proj, proj, c_col, c_row)


def _fox_delta(o_fox, do_fox, t, fd):
    tb = _pick(t, 384, LANES)

    def body(o_ref, do_ref, out_ref):
        for h in range(FOX_HEADS):
            sl = slice(h * fd, (h + 1) * fd)
            out_ref[h] = jnp.sum(o_ref[:, sl] * do_ref[:, sl].astype(BF16).astype(F32), axis=-1, keepdims=True)

    w = FOX_HEADS * fd
    return pl.pallas_call(
        body, name="fox_delta", grid=(t // tb,),
        in_specs=[pl.BlockSpec((tb, w), lambda i: (i, 0)), pl.BlockSpec((tb, w), lambda i: (i, 0))],
        out_specs=pl.BlockSpec((FOX_HEADS, tb, 1), lambda i: (0, i, 0)),
        out_shape=jax.ShapeDtypeStruct((FOX_HEADS, t, 1), F32),
        compiler_params=_params(("parallel",)),
    )(o_fox, do_fox)


def _fox_bwd(proj, c_col, c_row, lse, delta, do_fox, t, fd, c_fq, c_fkv):
    tb = _pick(t, 384, LANES)
    nb = t // tb
    scale = fd ** -0.5
    sp = _fox_specs(tb, fd, c_fq, c_fkv)
    qt, kt, npairs = _fox_pairs(nb, by_key=True)
    gw = FOX_GROUP * fd

    def body(qt_ref, kt_ref, q_ref, kv_ref, cq_ref, ck_ref, lse_ref, dl_ref, do_ref, dq_ref, dkv_ref, dc_ref, dr_ref,
             dq_acc, dk_acc, dv_acc, dc_acc, dr_acc):
        p = pl.program_id(1)
        qi, ki = qt_ref[p], kt_ref[p]

        @pl.when(p == 0)
        def _():
            dq_acc[...] = jnp.zeros_like(dq_acc)
            dr_acc[...] = jnp.zeros_like(dr_acc)

        @pl.when(qi == ki)
        def _():
            dk_acc[...] = jnp.zeros_like(dk_acc)
            dv_acc[...] = jnp.zeros_like(dv_acc)
            dc_acc[...] = jnp.zeros_like(dc_acc)

        rows = pl.ds(pl.multiple_of(qi * tb, LANES), tb)

        def update(masked):
            mask = _fox_mask(qi, ki, tb) if masked else None
            for hh, (q, k, v) in enumerate(_fox_heads(q_ref, kv_ref, fd)):
                do = _bf(do_ref[:, hh * fd:(hh + 1) * fd])
                s = _dot(q, k, NT) * scale + cq_ref[hh] - ck_ref[hh]
                if masked:
                    s = jnp.where(mask, s, MASK_VALUE)
                pr = jnp.exp(s - lse_ref[hh])
                dp = _dot(do, v, NT)
                ds = pr * (dp - dl_ref[hh])
                ds16 = ds.astype(BF16)
                dv_acc[hh] += _dot(pr.astype(BF16), do, TN)
                dk_acc[hh] += _dot(ds16, q, TN)
                dc_acc[hh] += _colsum(ds)
                dr_acc[hh, rows, :] += jnp.sum(ds, axis=-1, keepdims=True)
                dq_acc[hh, rows, :] += _dot(ds16, k, NN)

        edge = (ki == 0) | (ki == qi)
        pl.when(edge)(functools.partial(update, True))
        pl.when(jnp.logical_not(edge))(functools.partial(update, False))

        @pl.when(qi == nb - 1)
        def _():
            for hh in range(FOX_GROUP):
                dkv_ref[:, 2 * hh * fd:(2 * hh + 1) * fd] = (dk_acc[hh] * scale).astype(dkv_ref.dtype)
                dkv_ref[:, (2 * hh + 1) * fd:(2 * hh + 2) * fd] = dv_acc[hh].astype(dkv_ref.dtype)
                dc_ref[hh] = -dc_acc[hh]

        @pl.when(p == npairs - 1)
        def _():
            for hh in range(FOX_GROUP):
                dq_ref[:, hh * fd:(hh + 1) * fd] = (dq_acc[hh] * scale).astype(dq_ref.dtype)
            dr_ref[...] = dr_acc[...]

    return pl.pallas_call(
        body, name="fox_bwd",
        grid_spec=pltpu.PrefetchScalarGridSpec(
            num_scalar_prefetch=2, grid=(FOX_HEADS // FOX_GROUP, npairs),
            in_specs=[sp["q"], sp["kv"], sp["col"], sp["row"], sp["col"], sp["col"], sp["head"]],
            out_specs=[pl.BlockSpec((t, gw), lambda g, p, qt, kt: (0, g)), sp["key_kv"], sp["row"],
                       pl.BlockSpec((FOX_GROUP, t, 1), lambda g, p, qt, kt: (g, 0, 0))],
            scratch_shapes=[pltpu.VMEM((FOX_GROUP, t, fd), F32), pltpu.VMEM((FOX_GROUP, tb, fd), F32),
                            pltpu.VMEM((FOX_GROUP, tb, fd), F32), pltpu.VMEM((FOX_GROUP, 1, tb), F32),
                            pltpu.VMEM((FOX_GROUP, t, 1), F32)]),
        out_shape=[jax.ShapeDtypeStruct((t, FOX_HEADS * fd), BF16), jax.ShapeDtypeStruct((t, 2 * FOX_HEADS * fd), BF16),
                   jax.ShapeDtypeStruct((FOX_HEADS, 1, t), F32), jax.ShapeDtypeStruct((FOX_HEADS, t, 1), F32)],
        compiler_params=_params(("parallel", "arbitrary")),
    )(qt, kt, proj, proj, c_col, c_row, lse, delta, do_fox)


def _gla_gate_fwd(small, w_alpha_p, b_alpha, t, gk):
    def fn(row0, s, w, b):
        z = _dot(s.astype(BF16), w, NN) + b
        return (jnp.where(_row_ids(row0, s.shape[0]) >= PAD, _log_sigmoid(z) * (1.0 / GLA_TAU), 0.0),)

    return _ew("gla_gate_fwd", fn, [("tile", small, LANES, 0), ("full", w_alpha_p), ("full", b_alpha)],
               [("tile", F32, gk)], t, 264)


def _gla_gate_bwd(dglog, small, w_alpha_p, b_alpha, t, gk):
    def fn(row0, dg, s, w, b):
        z = _dot(s.astype(BF16), w, NN) + b
        dz = jnp.where(_row_ids(row0, s.shape[0]) >= PAD, dg * (1.0 / GLA_TAU) * _sigmoid(-z), 0.0)
        return dz, _colsum(dz)

    return _ew("gla_gate_bwd", fn,
               [("tile", dglog, gk, 0), ("tile", small, LANES, 0), ("full", w_alpha_p), ("full", b_alpha)],
               [("tile", BF16, gk), ("acc", F32, gk)], t, 264)


def _gla_chunk(q, k, g, scale, cs):
    r = lax.broadcasted_iota(jnp.int32, (cs, cs), 0)
    c = lax.broadcasted_iota(jnp.int32, (cs, cs), 1)
    causal = c <= r
    b = _tri_dot(causal.astype(BF16), g)
    bl = b[cs - 1:cs, :]
    eb, einv, eend = jnp.exp(b), jnp.exp(-b), jnp.exp(bl - b)
    qd = q.astype(F32) * scale * eb
    kf = k.astype(F32)
    return causal, (eb, einv, eend), bl, qd, kf * einv, kf * eend


def _gla_fwd(proj, glog, t, dk, dv, c_q, c_k, c_v):
    cs = GLA_CHUNK
    nc = t // cs
    wk, wv = GLA_HEADS * dk, GLA_HEADS * dv
    scale = dk ** -0.5

    def body(q_ref, k_ref, v_ref, g_ref, o_ref, sp_ref, st_ref):
        @pl.when(pl.program_id(0) == 0)
        def _():
            st_ref[...] = jnp.zeros_like(st_ref)

        for h in range(GLA_HEADS):
            ks, vs = slice(h * dk, (h + 1) * dk), slice(h * dv, (h + 1) * dv)
            v = v_ref[:, vs]
            causal, _, bl, qd, ki, ke = _gla_chunk(q_ref[:, ks], k_ref[:, ks], g_ref[:, ks], scale, cs)
            st = st_ref[h]
            sp_ref[h] = st
            a = jnp.where(causal, _dot(qd.astype(BF16), ki.astype(BF16), NT), 0.0)
            o_ref[:, vs] = _dot(a.astype(BF16), v, NN) + _dot(qd.astype(BF16), st.astype(BF16), NT)
            st_ref[h] = st * jnp.exp(bl) + _dot(v, ke.astype(BF16), TN)

    return pl.pallas_call(
        body, name="gla_fwd", grid=(nc,),
        in_specs=[pl.BlockSpec((cs, wk), lambda n: (n, c_q // wk)), pl.BlockSpec((cs, wk), lambda n: (n, c_k // wk)),
                  pl.BlockSpec((cs, wv), lambda n: (n, c_v // wv)), pl.BlockSpec((cs, wk), lambda n: (n, 0))],
        out_specs=[pl.BlockSpec((cs, wv), lambda n: (n, 0)),
                   pl.BlockSpec((None, GLA_HEADS, dv, dk), lambda n: (n, 0, 0, 0))],
        out_shape=[jax.ShapeDtypeStruct((t, wv), F32), jax.ShapeDtypeStruct((nc, GLA_HEADS, dv, dk), F32)],
        scratch_shapes=[pltpu.VMEM((GLA_HEADS, dv, dk), F32)],
        compiler_params=_params(("arbitrary",)),
    )(proj, proj, proj, glog)


def _gla_bwd(proj, glog, s_prev, do_raw, t, dk, dv, c_q, c_k, c_v):
    cs = GLA_CHUNK
    nc = t // cs
    wk, wv = GLA_HEADS * dk, GLA_HEADS * dv
    scale = dk ** -0.5

    def body(q_ref, k_ref, v_ref, g_ref, sp_ref, do_ref, dq_ref, dk_ref, dv_ref, dg_ref, dst_ref):
        @pl.when(pl.program_id(0) == 0)
        def _():
            dst_ref[...] = jnp.zeros_like(dst_ref)

        for h in range(GLA_HEADS):
            ks, vs = slice(h * dk, (h + 1) * dk), slice(h * dv, (h + 1) * dv)
            v = v_ref[:, vs]
            do = do_ref[:, vs].astype(BF16)
            causal, (eb, einv, eend), bl, qd, ki, ke = _gla_chunk(q_ref[:, ks], k_ref[:, ks], g_ref[:, ks], scale, cs)
            qd16, ki16, ke16 = qd.astype(BF16), ki.astype(BF16), ke.astype(BF16)
            st = sp_ref[h]
            dst = dst_ref[h]
            dst16 = dst.astype(BF16)
            a = jnp.where(causal, _dot(qd16, ki16, NT), 0.0).astype(BF16)
            da = jnp.where(causal, _dot(do, v, NT), 0.0).astype(BF16)
            dvv = _dot(a, do, TN) + _dot(ke16, dst16, NT)
            dqd = _dot(da, ki16, NN) + _dot(do, st.astype(BF16), NN)
            dki = _dot(da, qd16, TN)
            dke = _dot(v, dst16, NN)
            dl = jnp.exp(bl)
            ddl = _colsum(dst * st)
            dst_ref[h] = dst * dl + _dot(do, qd16, TN)
            dq_ref[:, ks] = (dqd * eb * scale).astype(dq_ref.dtype)
            dk_ref[:, ks] = (dki * einv + dke * eend).astype(dk_ref.dtype)
            dv_ref[:, vs] = dvv.astype(dv_ref.dtype)
            db = dqd * qd - dki * ki - dke * ke
            db_last = _colsum(dke * ke) + ddl * dl
            r = lax.broadcasted_iota(jnp.int32, (cs, cs), 0)
            c = lax.broadcasted_iota(jnp.int32, (cs, cs), 1)
            dg_ref[:, ks] = _tri_dot((c >= r).astype(BF16), db) + db_last

    rev = lambda f: (lambda n: f(nc - 1 - n))
    return pl.pallas_call(
        body, name="gla_bwd", grid=(nc,),
        in_specs=[pl.BlockSpec((cs, wk), rev(lambda n: (n, c_q // wk))), pl.BlockSpec((cs, wk), rev(lambda n: (n, c_k // wk))),
                  pl.BlockSpec((cs, wv), rev(lambda n: (n, c_v // wv))), pl.BlockSpec((cs, wk), rev(lambda n: (n, 0))),
                  pl.BlockSpec((None, GLA_HEADS, dv, dk), rev(lambda n: (n, 0, 0, 0))),
                  pl.BlockSpec((cs, wv), rev(lambda n: (n, 0)))],
        out_specs=[pl.BlockSpec((cs, wk), rev(lambda n: (n, 0))), pl.BlockSpec((cs, wk), rev(lambda n: (n, 0))),
                   pl.BlockSpec((cs, wv), rev(lambda n: (n, 0))), pl.BlockSpec((cs, wk), rev(lambda n: (n, 0)))],
        out_shape=[jax.ShapeDtypeStruct((t, wk), BF16), jax.ShapeDtypeStruct((t, wk), BF16),
                   jax.ShapeDtypeStruct((t, wv), BF16), jax.ShapeDtypeStruct((t, wk), F32)],
        scratch_shapes=[pltpu.VMEM((GLA_HEADS, dv, dk), F32)],
        compiler_params=_params(("arbitrary",)),
    )(proj, proj, proj, glog, s_prev, do_raw)


def _gla_post_fwd(o_raw, proj, gn, t, dv, c_gr):
    w = GLA_HEADS * dv

    def fn(row0, o, gr, g):
        gr = gr.astype(F32)
        outs = []
        for h in range(GLA_HEADS):
            oh = o[:, h * dv:(h + 1) * dv]
            outs.append(oh * lax.rsqrt(jnp.mean(oh * oh, axis=-1, keepdims=True) + EPS))
        on = jnp.concatenate(outs, axis=1) * g
        return (on * (gr * _sigmoid(gr)),)

    return _ew("gla_post_fwd", fn, [("tile", o_raw, w, 0), ("tile", proj, w, c_gr), ("full", gn)],
               [("tile", BF16, w)], t, 264)


def _gla_post_bwd(o_raw, proj, gn, do_gla, t, dv, c_gr):
    w = GLA_HEADS * dv

    def fn(row0, o, gr, g, do):
        gr = gr.astype(F32)
        sg = _sigmoid(gr)
        don = do * (gr * sg)
        ohs, dos = [], []
        for h in range(GLA_HEADS):
            sl = slice(h * dv, (h + 1) * dv)
            oh = o[:, sl]
            r = lax.rsqrt(jnp.mean(oh * oh, axis=-1, keepdims=True) + EPS)
            xh = oh * r
            dxh = don[:, sl] * g[:, sl]
            ohs.append(xh)
            dos.append(r * (dxh - xh * jnp.mean(dxh * xh, axis=-1, keepdims=True)))
        xh = jnp.concatenate(ohs, axis=1)
        dgr = do * (xh * g) * (sg * (1.0 + gr * (1.0 - sg)))
        return jnp.concatenate(dos, axis=1), dgr, _colsum(don * xh)

    return _ew("gla_post_bwd", fn,
               [("tile", o_raw, w, 0), ("tile", proj, w, c_gr), ("full", gn), ("tile", do_gla, w, 0)],
               [("tile", F32, w), ("tile", BF16, w), ("acc", F32, w)], t, 264)


def _adamw(name, w, g, m, v):
    rows, cols = w.shape
    tm = _pick(rows, max(8, (512 * 1024) // max(cols, 1) // 8 * 8), 8)

    def body(w_ref, g_ref, m_ref, v_ref, d_ref, nm_ref, nv_ref):
        gg = g_ref[...]
        nm = ADAM_B1 * m_ref[...] + (1.0 - ADAM_B1) * gg
        nv = ADAM_B2 * v_ref[...] + (1.0 - ADAM_B2) * (gg * gg)
        m_hat = nm / (1.0 - ADAM_B1 ** ADAM_STEP)
        v_hat = nv / (1.0 - ADAM_B2 ** ADAM_STEP)
        d_ref[...] = -ADAM_LR * (m_hat / (jnp.sqrt(v_hat) + ADAM_EPS) + ADAM_WD * w_ref[...])
        nm_ref[...] = nm
        nv_ref[...] = nv

    spec = pl.BlockSpec((tm, cols), lambda i: (i, 0))
    return pl.pallas_call(
        body, name=name, grid=(rows // tm,), in_specs=[spec] * 4, out_specs=[spec] * 3,
        out_shape=[jax.ShapeDtypeStruct((rows, cols), F32)] * 3,
        compiler_params=_params(("parallel",)),
    )(w, g, m, v)


def _me():
    return lax.axis_index("x"), lax.axis_index("y"), lax.axis_index("c")


def _hbm_specs(n):
    return [pl.BlockSpec(memory_space=pl.ANY)] * n


def _gather_weights(shards):
    n = len(shards)

    def body(*refs):
        src, dst = refs[:n], refs[n:2 * n]
        send_sems, recv_sems = refs[2 * n:]
        x, y, c = _me()
        sibling = (x, y, 1 - c)
        chips = [(1 - x, y), (x, 1 - y), (1 - x, 1 - y)]

        def slot(i, layer, px, py):
            return dst[i].at[layer, 2 * px + py]

        first, passed = [], []
        for i in range(n):
            for j, chip in enumerate(chips):
                cp = pltpu.make_async_remote_copy(
                    src_ref=src[i].at[c], dst_ref=slot(i, c, x, y), send_sem=send_sems.at[6 * i + j],
                    recv_sem=recv_sems.at[6 * i + j], device_id=(*chip, c), device_id_type=MESH)
                cp.start()
                first.append(cp)

        def landed(i, j, layer, chip):
            return pltpu.make_async_remote_copy(
                src_ref=slot(i, layer, *chip), dst_ref=slot(i, layer, *chip), send_sem=send_sems.at[6 * i + j],
                recv_sem=recv_sems.at[6 * i + j], device_id=sibling, device_id_type=MESH)

        for i in range(n):
            for j, chip in enumerate(chips):
                landed(i, j, c, chip).wait_recv()
                cp = landed(i, 3 + j, c, chip)
                cp.start()
                passed.append(cp)
        for i in range(n):
            for j, chip in enumerate(chips):
                landed(i, 3 + j, 1 - c, chip).wait_recv()
        for cp in first + passed:
            cp.wait_send()

    out_shape = [jax.ShapeDtypeStruct((DEPTH, N_CHIPS) + s.shape[1:], s.dtype) for s in shards]
    return pl.pallas_call(
        body, name="gather_weights", in_specs=_hbm_specs(n), out_specs=_hbm_specs(n), out_shape=out_shape,
        scratch_shapes=[pltpu.SemaphoreType.DMA((6 * n,)), pltpu.SemaphoreType.DMA((6 * n,))],
    )(*shards)


def _all_gather_small(v):
    def body(v_ref, out_ref, send_sems, recv_sems, local_sem):
        x, y, c = _me()
        mine = pltpu.make_async_copy(v_ref, out_ref.at[4 * x + 2 * y + c], local_sem)
        mine.start()
        copies = []
        for k in range(1, N_DEV):
            peer = (x ^ ((k >> 2) & 1), y ^ ((k >> 1) & 1), c ^ (k & 1))
            cp = pltpu.make_async_remote_copy(
                src_ref=v_ref, dst_ref=out_ref.at[4 * x + 2 * y + c], send_sem=send_sems.at[k - 1],
                recv_sem=recv_sems.at[k - 1], device_id=peer, device_id_type=MESH)
            cp.start()
            copies.append(cp)
        for cp in copies:
            cp.wait_recv()
        for cp in copies:
            cp.wait_send()
        mine.wait()

    return pl.pallas_call(
        body, name="all_gather_small", in_specs=_hbm_specs(1), out_specs=pl.BlockSpec(memory_space=pl.ANY),
        out_shape=jax.ShapeDtypeStruct((N_DEV,) + v.shape, v.dtype),
        scratch_shapes=[pltpu.SemaphoreType.DMA((N_DEV - 1,)), pltpu.SemaphoreType.DMA((N_DEV - 1,)),
                        pltpu.SemaphoreType.DMA],
    )(v)


def _swap_layers(layer0, layer1):
    n = len(layer0)

    def body(*refs):
        src0, src1, dst = refs[:n], refs[n:2 * n], refs[2 * n:3 * n]
        send_sems, recv_sems = refs[3 * n:]
        x, y, c = _me()

        def copy(i, src):
            return pltpu.make_async_remote_copy(
                src_ref=src[i], dst_ref=dst[i], send_sem=send_sems.at[i], recv_sem=recv_sems.at[i],
                device_id=(x, y, 1 - c), device_id_type=MESH)

        for i in range(n):
            @pl.when(c == 0)
            def _():
                copy(i, src1).start()

            @pl.when(c == 1)
            def _():
                copy(i, src0).start()

        for i in range(n):
            copy(i, src0).wait_recv()
        for i in range(n):
            copy(i, src0).wait_send()

    out_shape = [jax.ShapeDtypeStruct(g.shape, g.dtype) for g in layer0]
    return pl.pallas_call(
        body, name="swap_layers", in_specs=_hbm_specs(2 * n), out_specs=_hbm_specs(n), out_shape=out_shape,
        scratch_shapes=[pltpu.SemaphoreType.DMA((n,)), pltpu.SemaphoreType.DMA((n,))],
    )(*layer0, *layer1)


def _scatter_chips(parts):
    n = len(parts)

    def body(*refs):
        src, dst = refs[:n], refs[n:2 * n]
        send_sems, recv_sems = refs[2 * n:]
        x, y, c = _me()
        me = 2 * x + y
        chips = [(1 - x, y), (x, 1 - y), (1 - x, 1 - y)]
        copies = []
        for i in range(n):
            for j, (px, py) in enumerate(chips):
                cp = pltpu.make_async_remote_copy(
                    src_ref=src[i].at[2 * px + py], dst_ref=dst[i].at[me], send_sem=send_sems.at[3 * i + j],
                    recv_sem=recv_sems.at[3 * i + j], device_id=(px, py, c), device_id_type=MESH)
                cp.start()
                copies.append(cp)
        for cp in copies:
            cp.wait_recv()
        for cp in copies:
            cp.wait_send()

    out_shape = [jax.ShapeDtypeStruct(p.shape, p.dtype) for p in parts]
    return pl.pallas_call(
        body, name="scatter_chips", in_specs=_hbm_specs(n), out_specs=_hbm_specs(n), out_shape=out_shape,
        scratch_shapes=[pltpu.SemaphoreType.DMA((3 * n,)), pltpu.SemaphoreType.DMA((3 * n,))],
    )(*parts)


def _share_layers(totals):
    n = len(totals)

    def body(*refs):
        buf = refs[n:2 * n]
        send_sems, recv_sems = refs[2 * n:]
        x, y, c = _me()
        copies = []
        for i in range(n):
            cp = pltpu.make_async_remote_copy(
                src_ref=buf[i].at[c], dst_ref=buf[i].at[c], send_sem=send_sems.at[i], recv_sem=recv_sems.at[i],
                device_id=(x, y, 1 - c), device_id_type=MESH)
            cp.start()
            copies.append(cp)
        for cp in copies:
            cp.wait_recv()
        for cp in copies:
            cp.wait_send()

    out_shape = [jax.ShapeDtypeStruct(s.shape, s.dtype) for s in totals]
    return pl.pallas_call(
        body, name="share_layers", in_specs=_hbm_specs(n), out_specs=_hbm_specs(n), out_shape=out_shape,
        input_output_aliases={i: i for i in range(n)},
        scratch_shapes=[pltpu.SemaphoreType.DMA((n,)), pltpu.SemaphoreType.DMA((n,))],
    )(*totals)


def _flat2(a):
    return a.reshape(-1, a.shape[-1])


def _pair_sum(name, own0, own1, recv, layer):
    nch, r, cdim = recv.shape
    tm = _pick(r, max(8, (512 * 1024) // cdim // 16 * 16), 16)

    def body(l_ref, a0_ref, a1_ref, b_ref, s_ref, s16_ref):
        s = jnp.where(l_ref[0] == 0, a0_ref[...], a1_ref[...]) + b_ref[...]
        s_ref[...] = s
        s16_ref[...] = s.astype(BF16)

    spec = pl.BlockSpec((None, tm, cdim), lambda j, i, l: (j, i, 0))

    def own_spec(which):
        return pl.BlockSpec((None, tm, cdim),
                            lambda j, i, l: (jnp.where(l[0] == which, j, 0), jnp.where(l[0] == which, i, 0), 0))

    return pl.pallas_call(
        body, name=name,
        grid_spec=pltpu.PrefetchScalarGridSpec(
            num_scalar_prefetch=1, grid=(nch, r // tm),
            in_specs=[own_spec(0), own_spec(1), spec], out_specs=[spec, spec]),
        out_shape=[jax.ShapeDtypeStruct((nch, r, cdim), F32), jax.ShapeDtypeStruct((nch, r, cdim), BF16)],
        compiler_params=_params(("arbitrary", "arbitrary")),
    )(layer, own0, own1, recv)


def _chip_sum(name, own, recv, chip, layer):
    nch, r, cdim = own.shape
    tm = _pick(r, max(8, (512 * 1024) // cdim // 16 * 16), 16)

    def body(c_ref, l_ref, own_ref, *rest):
        recv_refs, out_ref = rest[:nch], rest[nch]
        me = c_ref[0]
        acc = None
        for j in range(nch):
            term = jnp.where(me == j, own_ref[...], recv_refs[j][...].astype(F32))
            acc = term if acc is None else acc + term
        out_ref[...] = acc

    recv_specs = [pl.BlockSpec((None, tm, cdim), functools.partial(lambda i, c, l, j: (j, i, 0), j=j))
                  for j in range(nch)]
    return pl.pallas_call(
        body, name=name,
        grid_spec=pltpu.PrefetchScalarGridSpec(
            num_scalar_prefetch=2, grid=(r // tm,),
            in_specs=[pl.BlockSpec((None, tm, cdim), lambda i, c, l: (c[0], i, 0))] + recv_specs,
            out_specs=pl.BlockSpec((None, tm, cdim), lambda i, c, l: (l[0], i, 0))),
        out_shape=jax.ShapeDtypeStruct((DEPTH, r, cdim), F32),
        compiler_params=_params(("parallel",)),
    )(chip, layer, own, *([recv] * nch))


def _sum_devices(gathered):
    _, r, cdim = gathered.shape

    def body(g_ref, o_ref):
        acc = g_ref[0]
        for k in range(1, N_DEV):
            acc = acc + g_ref[k]
        o_ref[...] = acc

    return pl.pallas_call(
        body, name="sum_devices", out_shape=jax.ShapeDtypeStruct((r, cdim), F32),
        compiler_params=_params(),
    )(gathered)


class _Layout:
    def __init__(self, d):
        self.d = d
        self.fw = d // 2
        self.fd = self.fw // FOX_HEADS
        self.gk = d // 2
        self.gv = d
        self.dk = self.gk // GLA_HEADS
        self.dv = self.gv // GLA_HEADS
        self.c_fq = 0
        self.c_gq = self.fw
        self.c_gv = self.c_gq + self.gk
        self.c_gr = self.c_gv + self.gv
        self.c_fkv = self.c_gr + self.gv
        self.c_gates = self.c_fkv + 2 * self.fw
        self.c_gk = self.c_gates + 2 * d
        self.c_small = self.c_gk + self.gk
        self.n_main = self.c_small
        self.n_p = self.c_small + LANES
        self.o_fk = self.fw
        self.o_fv = 2 * self.fw
        self.o_ff = 3 * self.fw
        self.o_gq = self.o_ff + FOX_HEADS
        self.o_gk = self.o_gq + self.gk
        self.o_gv = self.o_gk + self.gk
        self.o_gr = self.o_gv + self.gv
        self.o_ga = self.o_gr + self.gv
        self.o_gf = self.o_ga + GLA_RANK
        self.o_gg = self.o_gf + d
        self.n_orig = self.o_gg + d

    def to_p(self, w):
        kv = []
        for h in range(FOX_HEADS):
            kv.append(w[:, self.o_fk + h * self.fd:self.o_fk + (h + 1) * self.fd])
            kv.append(w[:, self.o_fv + h * self.fd:self.o_fv + (h + 1) * self.fd])
        small_pad = jnp.zeros((w.shape[0], LANES - FOX_HEADS - GLA_RANK), w.dtype)
        return jnp.concatenate(
            [w[:, :self.fw], w[:, self.o_gq:self.o_gk], w[:, self.o_gv:self.o_gr], w[:, self.o_gr:self.o_ga]]
            + kv + [w[:, self.o_gf:], w[:, self.o_gk:self.o_gv], w[:, self.o_ff:self.o_gq],
                    w[:, self.o_ga:self.o_gf], small_pad], axis=1)

    def from_segments(self, seg):
        kv = seg["fkv"].reshape(seg["fkv"].shape[0], FOX_HEADS, 2, self.fd)
        fk = kv[:, :, 0, :].reshape(-1, self.fw)
        fv = kv[:, :, 1, :].reshape(-1, self.fw)
        sm = seg["small"]
        return jnp.concatenate(
            [seg["fq"], fk, fv, sm[:, :FOX_HEADS], seg["gq"], seg["gk"], seg["gv"], seg["gr"],
             sm[:, FOX_HEADS:FOX_HEADS + GLA_RANK], seg["gates"]], axis=1)


def _layer_fwd(lay, h, p, t):
    d = lay.d
    xn = _rms_fwd("rms_mix_fwd", h, p["norm_mix_g"], t, d)
    proj = _mm("mm_proj", xn, p["w_in"], mode="nn", m=t, n=lay.n_main, k=d, out_dtypes=(BF16,))
    small = _mm("mm_small", xn, p["w_in"], mode="nn", m=t, n=LANES, k=d, b_c0=lay.c_small)
    cs = _fox_gate_fwd(small, p["b_forget_p"], t)
    ct = cs[:, :FOX_HEADS].T
    c_col, c_row = ct[:, :, None], ct[:, None, :]
    o_fox, lse = _fox_fwd(proj, c_col, c_row, t, lay.fd, lay.c_fq, lay.c_fkv)
    glog = _gla_gate_fwd(small, p["w_alpha_p"], p["b_alpha"], t, lay.gk)
    o_raw, s_prev = _gla_fwd(proj, glog, t, lay.dk, lay.dv, lay.c_gq, lay.c_gk, lay.c_gv)
    o_gla = _gla_post_fwd(o_raw, proj, p["gla_norm_g"], t, lay.dv, lay.c_gr)
    a_fox = _mm("mm_o_fox", o_fox, p["w_o_fox"], mode="nn", m=t, n=d, k=lay.fw, b_shards=N_CHIPS)
    a_gla = _mm("mm_o_gla", o_gla, p["w_o_gla"], mode="nn", m=t, n=d, k=lay.gv)
    y = _merge_fwd(a_fox, a_gla, proj, lay.c_gates, t, d)
    h1 = _mm("mm_out", y, p["w_out"], mode="nn", m=t, n=d, k=d, extras=[h], epilogue=lambda acc, res: (res + acc,))
    xn2 = _rms_fwd("rms_mlp_fwd", h1, p["norm_mlp_g"], t, d)
    u, act = _mm("mm_ff1", xn2, p["w_ff1"], mode="nn", m=t, n=4 * d, k=d, out_dtypes=(BF16, BF16), b_shards=N_CHIPS,
                 epilogue=lambda acc: (acc, jnp.square(jnp.maximum(acc, 0.0))))
    h2 = _mm("mm_ff2", act, p["w_ff2"], mode="nn", m=t, n=d, k=4 * d, extras=[h1],
             epilogue=lambda acc, res: (res + acc,))
    saved = dict(h=h, xn=xn, proj=proj, small=small, c_col=c_col, c_row=c_row, o_fox=o_fox, lse=lse, glog=glog,
                 o_raw=o_raw, s_prev=s_prev, o_gla=o_gla, a_fox=a_fox, a_gla=a_gla, y=y, h1=h1, xn2=xn2, u=u, act=act)
    return h2, saved


def _layer_bwd(lay, dh2, p, s, t):
    d = lay.d
    g = {}
    du = _mm("mm_dact", dh2, p["w_ff2"], mode="nt", m=t, n=4 * d, k=d, extras=[s["u"]], out_dtypes=(BF16,),
             epilogue=lambda acc, u: (acc * (2.0 * jnp.maximum(u.astype(F32), 0.0)),))
    g["w_ff2"] = _mm("mm_dw_ff2", s["act"], dh2, mode="tn", m=4 * d, n=d, k=t)
    g["w_ff1"] = _mm("mm_dw_ff1", s["xn2"], du, mode="tn", m=d, n=4 * d, k=t, out_shards=N_CHIPS)
    dxn2 = _mm("mm_dxn2", du, p["w_ff1"], mode="nt", m=t, n=d, k=4 * d, b_shards=N_CHIPS)
    dh1, g["norm_mlp_g"] = _rms_bwd("rms_mlp_bwd", s["h1"], p["norm_mlp_g"], dxn2, dh2, t, d)
    dy = _mm("mm_dy", dh1, p["w_out"], mode="nt", m=t, n=d, k=d)
    g["w_out"] = _mm("mm_dw_out", s["y"], dh1, mode="tn", m=d, n=d, k=t)
    da_fox, da_gla, dgates = _merge_bwd(dy, s["a_fox"], s["a_gla"], s["proj"], lay.c_gates, t, d)
    g["w_o_fox"] = _mm("mm_dw_o_fox", s["o_fox"], da_fox, mode="tn", m=lay.fw, n=d, k=t, out_shards=N_CHIPS)
    do_fox = _mm("mm_do_fox", da_fox, p["w_o_fox"], mode="nt", m=t, n=lay.fw, k=d, b_shards=N_CHIPS)
    g["w_o_gla"] = _mm("mm_dw_o_gla", s["o_gla"], da_gla, mode="tn", m=lay.gv, n=d, k=t)
    do_gla = _mm("mm_do_gla", da_gla, p["w_o_gla"], mode="nt", m=t, n=lay.gv, k=d)
    do_raw, dgr, g["gla_norm_g"] = _gla_post_bwd(s["o_raw"], s["proj"], p["gla_norm_g"], do_gla, t, lay.dv, lay.c_gr)
    dgq, dgk, dgv, dglog = _gla_bwd(s["proj"], s["glog"], s["s_prev"], do_raw, t, lay.dk, lay.dv,
                                    lay.c_gq, lay.c_gk, lay.c_gv)
    dz, g["b_alpha"] = _gla_gate_bwd(dglog, s["small"], p["w_alpha_p"], p["b_alpha"], t, lay.gk)
    g["w_alpha_p"] = _mm("mm_dw_alpha", s["small"], dz, mode="tn", m=LANES, n=lay.gk, k=t)
    dga = _mm("mm_dga", dz, p["w_alpha_p"], mode="nt", m=t, n=LANES, k=lay.gk)
    delta = _fox_delta(s["o_fox"], do_fox, t, lay.fd)
    dfq, dfkv, dc, dr = _fox_bwd(s["proj"], s["c_col"], s["c_row"], s["lse"], delta, do_fox, t, lay.fd,
                                 lay.c_fq, lay.c_fkv)
    dc_p = jnp.pad((dc[:, 0, :] + dr[:, :, 0]).T, ((0, 0), (0, LANES - FOX_HEADS)))
    dsmall, g["b_forget_p"] = _fox_gate_bwd(dc_p, s["small"], p["b_forget_p"], dga, t)
    segs = [("fq", dfq, lay.c_fq), ("gq", dgq, lay.c_gq), ("gv", dgv, lay.c_gv), ("gr", dgr, lay.c_gr),
            ("fkv", dfkv, lay.c_fkv), ("gates", dgates, lay.c_gates), ("gk", dgk, lay.c_gk),
            ("small", dsmall, lay.c_small)]
    dxn = None
    dw_in = {}
    for nm, dseg, c0 in segs:
        width = dseg.shape[1]
        dw_in[nm] = _mm("mm_dw_in_" + nm, s["xn"], dseg, mode="tn", m=d, n=width, k=t)
        if dxn is None:
            dxn = _mm("mm_dxn_" + nm, dseg, p["w_in"], mode="nt", m=t, n=d, k=width, b_c0=c0)
        else:
            dxn = _mm("mm_dxn_" + nm, dseg, p["w_in"], mode="nt", m=t, n=d, k=width, b_c0=c0, extras=[dxn],
                      epilogue=lambda acc, prev: (prev + acc,))
    g["w_in"] = lay.from_segments(dw_in)
    dh, g["norm_mix_g"] = _rms_bwd("rms_mix_bwd", s["h"], p["norm_mix_g"], dxn, dh1, t, d)
    return dh, g


def _sequence_step(x, target, meta, layers, final_g):
    seq, d = x.shape
    t = seq + ROW0
    lay = _Layout(d)
    h = jnp.pad(x, ((ROW0, 0), (0, 0))).at[PAD:ROW0].set(meta)
    target_p = jnp.pad(target, ((ROW0, 0), (0, 0)))
    saved = []
    for p in layers:
        h, s = _layer_fwd(lay, h, p, t)
        saved.append(s)
    dh, dg_final, loss_part = _loss_head(h, final_g, target_p, t, d)
    grads = [None] * len(layers)
    for l in reversed(range(len(layers))):
        dh, grads[l] = _layer_bwd(lay, dh, layers[l], saved[l], t)
    return loss_part, dh[ROW0:], dh[PAD:ROW0], grads, dg_final


_SMALL_ROWS = 32


def _pack_small(d, meta, mix, gla, mlp, final, b_alpha, b_forget):
    rows = [meta.reshape(N_META, d), mix.reshape(DEPTH, d), gla.reshape(DEPTH, d), mlp.reshape(DEPTH, d),
            final.reshape(1, d), b_alpha.reshape(1, d),
            jnp.pad(b_forget.reshape(1, DEPTH * FOX_HEADS), ((0, 0), (0, d - DEPTH * FOX_HEADS)))]
    packed = jnp.concatenate(rows, axis=0)
    return jnp.pad(packed, ((0, _SMALL_ROWS - packed.shape[0]), (0, 0)))


def _unpack_small(d, packed):
    return dict(meta=packed[:N_META], norm_mix_g=packed[16:18], gla_norm_g=packed[18:20], norm_mlp_g=packed[20:22],
                final_norm_g=packed[22], b_alpha=packed[23].reshape(DEPTH, d // 2),
                b_forget=packed[24, :DEPTH * FOX_HEADS].reshape(DEPTH, FOX_HEADS))


_BIG = ("w_in", "w_alpha2", "w_o_fox", "w_o_gla", "w_out", "w_ff1", "w_ff2")
_COL_SHARDED = ("w_in", "w_alpha2", "w_o_fox", "w_ff1")


def _full_matrix(name, gathered_l):
    nch, r, c = gathered_l.shape
    if name in _COL_SHARDED:
        return gathered_l.transpose(1, 0, 2).reshape(r, nch * c)
    return gathered_l.reshape(nch * r, c)


def _shard_major(name, full):
    r, c = full.shape
    if name in _COL_SHARDED:
        return full.reshape(r, N_CHIPS, c // N_CHIPS).transpose(1, 0, 2)
    return full.reshape(N_CHIPS, r // N_CHIPS, c)


def kernel(x, meta_tokens, norm_mix_g, w_in, b_forget, w_alpha2, b_alpha, gla_norm_g, w_o_fox, w_o_gla, w_out, norm_mlp_g, w_ff1, w_ff2, final_norm_g, loss_target, m_meta_tokens, m_norm_mix_g, m_w_in, m_b_forget, m_w_alpha2, m_b_alpha, m_gla_norm_g, m_w_o_fox, m_w_o_gla, m_w_out, m_norm_mlp_g, m_w_ff1, m_w_ff2, m_final_norm_g, v_meta_tokens, v_norm_mix_g, v_w_in, v_b_forget, v_w_alpha2, v_b_alpha, v_gla_norm_g, v_w_o_fox, v_w_o_gla, v_w_out, v_norm_mlp_g, v_w_ff1, v_w_ff2, v_final_norm_g):
    d = x.shape[2]
    lay = _Layout(d)
    xi, yi, ci = lax.axis_index("x"), lax.axis_index("y"), lax.axis_index("c")
    chip = (2 * xi + yi).astype(jnp.int32)
    w = dict(w_in=w_in, w_alpha2=w_alpha2, w_o_fox=w_o_fox, w_o_gla=w_o_gla, w_out=w_out, w_ff1=w_ff1, w_ff2=w_ff2)
    m = dict(w_in=m_w_in, w_alpha2=m_w_alpha2, w_o_fox=m_w_o_fox, w_o_gla=m_w_o_gla, w_out=m_w_out, w_ff1=m_w_ff1,
             w_ff2=m_w_ff2)
    v = dict(w_in=v_w_in, w_alpha2=v_w_alpha2, w_o_fox=v_w_o_fox, w_o_gla=v_w_o_gla, w_out=v_w_out, w_ff1=v_w_ff1,
             w_ff2=v_w_ff2)

    own16 = [w[n].astype(BF16) for n in _BIG]
    gathered = _gather_weights(own16)
    gathered = [lax.dynamic_update_slice(g, o[:, None], (0, chip, 0, 0)) for g, o in zip(gathered, own16)]
    meta_all = _all_gather_small(meta_tokens)
    meta_full = meta_all[0::2].transpose(1, 0, 2).reshape(N_META, d)
    layers = []
    for l in range(DEPTH):
        gl = {n: g[l] for n, g in zip(_BIG, gathered)}
        w_alpha_p = jnp.zeros((LANES, lay.gk), BF16).at[FOX_HEADS:FOX_HEADS + GLA_RANK].set(
            _full_matrix("w_alpha2", gl["w_alpha2"]))
        layers.append(dict(
            w_in=lay.to_p(_full_matrix("w_in", gl["w_in"])), w_alpha_p=w_alpha_p, w_o_fox=gl["w_o_fox"],
            w_o_gla=_full_matrix("w_o_gla", gl["w_o_gla"]), w_out=_full_matrix("w_out", gl["w_out"]),
            w_ff1=gl["w_ff1"], w_ff2=_full_matrix("w_ff2", gl["w_ff2"]),
            norm_mix_g=norm_mix_g[l][None], norm_mlp_g=norm_mlp_g[l][None], gla_norm_g=gla_norm_g[l][None],
            b_alpha=b_alpha[l][None],
            b_forget_p=jnp.pad(b_forget[l][None], ((0, 0), (0, LANES - FOX_HEADS)))))

    loss_part, grad_x, d_meta, grads, dg_final = _sequence_step(x[0], loss_target[0], meta_full, layers,
                                                                final_norm_g[None])
    loss = lax.psum(loss_part[0, 0], ("x", "y", "c"))

    stack = lambda key: jnp.concatenate([grads[l][key] for l in range(DEPTH)], axis=0)
    b_forget_g = jnp.concatenate([grads[l]["b_forget_p"][:, :FOX_HEADS] for l in range(DEPTH)], axis=0)
    packed = _pack_small(d, d_meta, stack("norm_mix_g"), stack("gla_norm_g"), stack("norm_mlp_g"), dg_final,
                         stack("b_alpha"), b_forget_g)
    small_g = _unpack_small(d, _sum_devices(_all_gather_small(packed)))
    small_g["meta"] = lax.dynamic_slice_in_dim(small_g["meta"], chip * (d // N_CHIPS), d // N_CHIPS, axis=1)

    def partial_of(l, n):
        if n in ("w_ff1", "w_o_fox"):
            return grads[l][n]
        if n == "w_alpha2":
            return _shard_major(n, grads[l]["w_alpha_p"][FOX_HEADS:FOX_HEADS + GLA_RANK])
        return _shard_major(n, grads[l][n])

    part0 = [partial_of(0, n) for n in _BIG]
    part1 = [partial_of(1, n) for n in _BIG]
    from_sibling = _swap_layers(part0, part1)
    layer_idx = ci.astype(jnp.int32)[None]
    chip_idx = chip[None]
    sums = [_pair_sum("pair_sum_" + n, p0, p1, r_, layer_idx)
            for n, p0, p1, r_ in zip(_BIG, part0, part1, from_sibling)]
    landed = _scatter_chips([s16 for _, s16 in sums])
    totals = [_chip_sum("chip_sum_" + n, s32, r_, chip_idx, layer_idx) for n, (s32, _), r_ in zip(_BIG, sums, landed)]
    big_g = dict(zip(_BIG, _share_layers(totals)))

    out_g, out_d, out_m, out_v = {}, {}, {}, {}
    for n in _BIG:
        shape = w[n].shape
        res = _adamw("adamw_" + n, _flat2(w[n]), _flat2(big_g[n]), _flat2(m[n]), _flat2(v[n]))
        out_g[n] = big_g[n].reshape(shape)
        out_d[n], out_m[n], out_v[n] = [r.reshape(shape) for r in res]
    sm_w = dict(meta_tokens=meta_tokens, norm_mix_g=norm_mix_g, b_forget=b_forget, b_alpha=b_alpha,
                gla_norm_g=gla_norm_g, norm_mlp_g=norm_mlp_g, final_norm_g=final_norm_g)
    sm_m = dict(meta_tokens=m_meta_tokens, norm_mix_g=m_norm_mix_g, b_forget=m_b_forget, b_alpha=m_b_alpha,
                gla_norm_g=m_gla_norm_g, norm_mlp_g=m_norm_mlp_g, final_norm_g=m_final_norm_g)
    sm_v = dict(meta_tokens=v_meta_tokens, norm_mix_g=v_norm_mix_g, b_forget=v_b_forget, b_alpha=v_b_alpha,
                gla_norm_g=v_gla_norm_g, norm_mlp_g=v_norm_mlp_g, final_norm_g=v_final_norm_g)
    sm_g = dict(meta_tokens=small_g["meta"], norm_mix_g=small_g["norm_mix_g"], b_forget=small_g["b_forget"],
                b_alpha=small_g["b_alpha"], gla_norm_g=small_g["gla_norm_g"], norm_mlp_g=small_g["norm_mlp_g"],
                final_norm_g=small_g["final_norm_g"])
    names_small = list(sm_w)
    sizes = [sm_w[n].size for n in names_small]
    width = 512
    total = -(-sum(sizes) // (8 * width)) * (8 * width)

    def pack_flat(dct, fill):
        flat = jnp.concatenate([dct[n].reshape(-1) for n in names_small])
        return jnp.pad(flat, (0, total - flat.shape[0]), constant_values=fill).reshape(-1, width)

    res = _adamw("adamw_small", pack_flat(sm_w, 0.0), pack_flat(sm_g, 0.0), pack_flat(sm_m, 0.0), pack_flat(sm_v, 1.0))
    offs = [0]
    for sz in sizes:
        offs.append(offs[-1] + sz)
    for i, n in enumerate(names_small):
        out_g[n] = sm_g[n].reshape(sm_w[n].shape)
        out_d[n], out_m[n], out_v[n] = [r.reshape(-1)[offs[i]:offs[i + 1]].reshape(sm_w[n].shape) for r in res]

    order = ["meta_tokens", "norm_mix_g", "w_in", "b_forget", "w_alpha2", "b_alpha", "gla_norm_g", "w_o_fox",
             "w_o_gla", "w_out", "norm_mlp_g", "w_ff1", "w_ff2", "final_norm_g"]
    return (loss, grad_x[None], *[out_g[n] for n in order], *[out_d[n] for n in order],
            *[out_m[n] for n in order], *[out_v[n] for n in order])
```

```python
import functools

import numpy as np

import jax
import jax.numpy as jnp
from jax import lax
from jax.experimental import pallas as pl
from jax.experimental.pallas import tpu as pltpu

F32 = jnp.float32
BF16 = jnp.bfloat16

N_META = 16
PAD = 112
ROW0 = PAD + N_META
EPS = 1e-6
MASK_VALUE = -1e30
FOX_HEADS = 8
FOX_GROUP = 2
GLA_HEADS = 4
GLA_RANK = 16
GLA_TAU = 16.0
GLA_CHUNK = 64
DEPTH = 2
N_CHIPS = 4
N_DEV = 8

ADAM_LR = 0.001
ADAM_B1 = 0.9
ADAM_B2 = 0.999
ADAM_EPS = 1e-08
ADAM_WD = 0.01
ADAM_STEP = 10

LANES = 128
VMEM_LIMIT = 56 * 1024 * 1024
MESH = pl.DeviceIdType.MESH


def _pick(n, target, mult):
    best = None
    for d in range(mult, min(n, target) + 1, mult):
        if n % d == 0:
            best = d
    return n if best is None else best


def _params(sem=None):
    return pltpu.CompilerParams(dimension_semantics=sem, vmem_limit_bytes=VMEM_LIMIT)


def _bf(v):
    return v if v.dtype == BF16 else v.astype(BF16)


def _sigmoid(z):
    return 1.0 / (1.0 + jnp.exp(-z))


def _log_sigmoid(z):
    return jnp.minimum(z, 0.0) - jnp.log(1.0 + jnp.exp(-jnp.abs(z)))


def _split3(v):
    a = v.astype(BF16)
    r = v - a.astype(F32)
    b = r.astype(BF16)
    c = (r - b.astype(F32)).astype(BF16)
    return a, b, c


def _dot(a, b, dims):
    return lax.dot_general(a, b, (dims, ((), ())), preferred_element_type=F32)


NN = ((1,), (0,))
NT = ((1,), (1,))
TN = ((0,), (0,))


def _tri_dot(tri, v, dims=NN):
    a, b, c = _split3(v)
    return _dot(tri, a, dims) + _dot(tri, b, dims) + _dot(tri, c, dims)


def _mm(name, a, b, *, mode, m, n, k, b_c0=0, extras=(), epilogue=None, out_dtypes=(F32,),
        b_shards=1, out_shards=1, tm=1056, tn=1024, tk=2048):
    tm = _pick(m, tm, LANES if mode == "tn" else 16)
    tn = _pick(n // max(b_shards if mode == "nn" else 1, out_shards), tn, LANES)
    if mode == "tn":
        tk = _pick(k, 2112, 16)
    else:
        tk = _pick(k // (b_shards if mode == "nt" else 1), tk, LANES)
    assert b_c0 % (tk if mode == "nt" else tn) == 0 and (b_shards == 1 or b_c0 == 0)
    nk = k // tk
    if mode == "tn":
        a_spec = pl.BlockSpec((tk, tm), lambda i, j, kk: (kk, i))
    else:
        a_spec = pl.BlockSpec((tm, tk), lambda i, j, kk: (i, kk))
    if mode == "nt":
        dims = NT
        if b_shards > 1:
            per = (k // b_shards) // tk
            b_spec = pl.BlockSpec((None, tn, tk), lambda i, j, kk: (kk // per, j, kk % per))
        else:
            b_spec = pl.BlockSpec((tn, tk), lambda i, j, kk: (j, kk + b_c0 // tk))
    else:
        dims = NN if mode == "nn" else TN
        if b_shards > 1:
            per = (n // b_shards) // tn
            b_spec = pl.BlockSpec((None, tk, tn), lambda i, j, kk: (j // per, kk, j % per))
        else:
            b_spec = pl.BlockSpec((tk, tn), lambda i, j, kk: (kk, j + b_c0 // tn))
    ex_specs = [pl.BlockSpec((tm, tn), lambda i, j, kk: (i, j)) for _ in extras]
    if out_shards > 1:
        oper = (n // out_shards) // tn
        out_specs = [pl.BlockSpec((None, tm, tn), lambda i, j, kk: (j // oper, i, j % oper)) for _ in out_dtypes]
        out_shape = [jax.ShapeDtypeStruct((out_shards, m, n // out_shards), dt) for dt in out_dtypes]
    else:
        out_specs = [pl.BlockSpec((tm, tn), lambda i, j, kk: (i, j)) for _ in out_dtypes]
        out_shape = [jax.ShapeDtypeStruct((m, n), dt) for dt in out_dtypes]
    n_ex = len(extras)
    n_out = len(out_dtypes)

    def finish(acc, ex_refs, out_refs):
        vals = (acc,) if epilogue is None else epilogue(acc, *[r[...] for r in ex_refs])
        for r, v in zip(out_refs, vals):
            r[...] = v.astype(r.dtype)

    def body(a_ref, b_ref, *rest):
        ex_refs = rest[:n_ex]
        out_refs = rest[n_ex:n_ex + n_out]
        prod = _dot(_bf(a_ref[...]), _bf(b_ref[...]), dims)
        if nk == 1:
            finish(prod, ex_refs, out_refs)
            return
        acc_ref = rest[n_ex + n_out]
        kk = pl.program_id(2)

        @pl.when(kk == 0)
        def _():
            acc_ref[...] = prod

        @pl.when((kk > 0) & (kk < nk - 1))
        def _():
            acc_ref[...] += prod

        @pl.when(kk == nk - 1)
        def _():
            finish(acc_ref[...] + prod, ex_refs, out_refs)

    outs = pl.pallas_call(
        body,
        name=name,
        grid=(m // tm, n // tn, nk),
        in_specs=[a_spec, b_spec] + ex_specs,
        out_specs=out_specs,
        out_shape=out_shape,
        scratch_shapes=[pltpu.VMEM((tm, tn), F32)] if nk > 1 else [],
        compiler_params=_params(("parallel", "parallel", "arbitrary")),
    )(a, b, *extras)
    return outs[0] if n_out == 1 else outs


def _ew(name, fn, ins, outs, rows, tm):
    tm = _pick(rows, tm, 16)
    in_specs, args = [], []
    for spec in ins:
        if spec[0] == "tile":
            _, arr, width, c0 = spec
            assert c0 % width == 0
            in_specs.append(pl.BlockSpec((tm, width), functools.partial(lambda i, o: (i, o), o=c0 // width)))
        else:
            arr = spec[1]
            in_specs.append(pl.BlockSpec(arr.shape, lambda i: (0, 0)))
        args.append(arr)
    out_specs, out_shape = [], []
    for kind, dt, width in outs:
        if kind == "tile":
            out_specs.append(pl.BlockSpec((tm, width), lambda i: (i, 0)))
            out_shape.append(jax.ShapeDtypeStruct((rows, width), dt))
        else:
            out_specs.append(pl.BlockSpec((1, width), lambda i: (0, 0)))
            out_shape.append(jax.ShapeDtypeStruct((1, width), dt))
    n_in = len(ins)
    has_acc = any(o[0] == "acc" for o in outs)

    def body(*refs):
        i = pl.program_id(0)
        vals = fn(i * tm, *[r[...] for r in refs[:n_in]])
        for (kind, _, _), r, v in zip(outs, refs[n_in:], vals):
            if kind == "tile":
                r[...] = v.astype(r.dtype)
            else:
                @pl.when(i == 0)
                def _():
                    r[...] = jnp.zeros_like(r)

                r[...] += v.astype(r.dtype)

    res = pl.pallas_call(
        body,
        name=name,
        grid=(rows // tm,),
        in_specs=in_specs,
        out_specs=out_specs,
        out_shape=out_shape,
        compiler_params=_params(("arbitrary",) if has_acc else ("parallel",)),
    )(*args)
    return res[0] if len(outs) == 1 else res


def _row_ids(row0, tm):
    return row0 + lax.broadcasted_iota(jnp.int32, (tm, 1), 0)


def _colsum(v):
    return jnp.sum(v, axis=0, keepdims=True)


def _rms_fwd(name, h, g, t, d):
    def fn(row0, x, gg):
        r = lax.rsqrt(jnp.mean(x * x, axis=-1, keepdims=True) + EPS)
        return (x * r * gg,)

    return _ew(name, fn, [("tile", h, d, 0), ("full", g)], [("tile", BF16, d)], t, 264)


def _rms_bwd(name, h, g, dy, dres, t, d):
    def fn(row0, x, gg, dyv, dr):
        r = lax.rsqrt(jnp.mean(x * x, axis=-1, keepdims=True) + EPS)
        xh = x * r
        dxh = dyv * gg
        dx = r * (dxh - xh * jnp.mean(dxh * xh, axis=-1, keepdims=True))
        out = jnp.where(_row_ids(row0, x.shape[0]) >= PAD, dr + dx, 0.0)
        return out, _colsum(dyv * xh)

    return _ew(name, fn, [("tile", h, d, 0), ("full", g), ("tile", dy, d, 0), ("tile", dres, d, 0)],
               [("tile", F32, d), ("acc", F32, d)], t, 264)


def _loss_head(h, g, target_p, t, d):
    def fn(row0, x, gg, tgt):
        real = _row_ids(row0, x.shape[0]) >= ROW0
        r = lax.rsqrt(jnp.mean(x * x, axis=-1, keepdims=True) + EPS)
        xh = x * r
        err = jnp.where(real, xh * gg - tgt, 0.0)
        loss_rows = 0.5 * jnp.mean(err * err, axis=-1, keepdims=True)
        dyv = err * (1.0 / d)
        dxh = dyv * gg
        dx = r * (dxh - xh * jnp.mean(dxh * xh, axis=-1, keepdims=True))
        loss_part = jnp.sum(loss_rows, axis=0, keepdims=True) * jnp.ones((1, LANES), F32)
        return jnp.where(real, dx, 0.0), _colsum(dyv * xh), loss_part

    return _ew("loss_head", fn, [("tile", h, d, 0), ("full", g), ("tile", target_p, d, 0)],
               [("tile", F32, d), ("acc", F32, d), ("acc", F32, LANES)], t, 264)


def _merge_fwd(a_fox, a_gla, proj, c_gates, t, d):
    def fn(row0, af, ag, gates):
        gates = gates.astype(F32)
        return (_sigmoid(gates[:, :d]) * af + _sigmoid(gates[:, d:]) * ag,)

    return _ew("merge_fwd", fn, [("tile", a_fox, d, 0), ("tile", a_gla, d, 0), ("tile", proj, 2 * d, c_gates)],
               [("tile", BF16, d)], t, 264)


def _merge_bwd(dy, a_fox, a_gla, proj, c_gates, t, d):
    def fn(row0, dyv, af, ag, gates):
        gates = gates.astype(F32)
        sf = _sigmoid(gates[:, :d])
        sg = _sigmoid(gates[:, d:])
        dgates = jnp.concatenate([dyv * af * sf * (1.0 - sf), dyv * ag * sg * (1.0 - sg)], axis=1)
        return dyv * sf, dyv * sg, dgates

    return _ew("merge_bwd", fn,
               [("tile", dy, d, 0), ("tile", a_fox, d, 0), ("tile", a_gla, d, 0), ("tile", proj, 2 * d, c_gates)],
               [("tile", BF16, d), ("tile", BF16, d), ("tile", BF16, 2 * d)], t, 264)


def _fox_gate_fwd(small, b_forget_p, t):
    tb = _pick(t, 384, LANES)

    def body(s_ref, b_ref, c_ref, carry_ref):
        i = pl.program_id(0)

        @pl.when(i == 0)
        def _():
            carry_ref[...] = jnp.zeros_like(carry_ref)

        logf = _log_sigmoid(s_ref[...] + b_ref[...])
        logf = jnp.where(_row_ids(i * tb, tb) >= PAD, logf, 0.0)
        r = lax.broadcasted_iota(jnp.int32, (tb, tb), 0)
        c = lax.broadcasted_iota(jnp.int32, (tb, tb), 1)
        tri = (c <= r).astype(BF16)
        cs = _tri_dot(tri, logf) + carry_ref[...]
        c_ref[...] = cs
        carry_ref[...] = cs[tb - 1:tb, :]

    return pl.pallas_call(
        body, name="fox_gate_fwd", grid=(t // tb,),
        in_specs=[pl.BlockSpec((tb, LANES), lambda i: (i, 0)), pl.BlockSpec((1, LANES), lambda i: (0, 0))],
        out_specs=pl.BlockSpec((tb, LANES), lambda i: (i, 0)),
        out_shape=jax.ShapeDtypeStruct((t, LANES), F32),
        scratch_shapes=[pltpu.VMEM((1, LANES), F32)],
        compiler_params=_params(("arbitrary",)),
    )(small, b_forget_p)


def _fox_gate_bwd(dc, small, b_forget_p, dga, t):
    tb = _pick(t, 384, LANES)
    nb = t // tb

    def body(dc_ref, s_ref, b_ref, dga_ref, ds_ref, db_ref, carry_ref):
        i = pl.program_id(0)

        @pl.when(i == 0)
        def _():
            carry_ref[...] = jnp.zeros_like(carry_ref)
            db_ref[...] = jnp.zeros_like(db_ref)

        r = lax.broadcasted_iota(jnp.int32, (tb, tb), 0)
        c = lax.broadcasted_iota(jnp.int32, (tb, tb), 1)
        tri = (c >= r).astype(BF16)
        dlogf = _tri_dot(tri, dc_ref[...]) + carry_ref[...]
        carry_ref[...] = dlogf[0:1, :]
        z = s_ref[...] + b_ref[...]
        dff = dlogf * _sigmoid(-z)
        lane = lax.broadcasted_iota(jnp.int32, (tb, LANES), 1)
        keep = (_row_ids((nb - 1 - i) * tb, tb) >= PAD) & (lane < FOX_HEADS)
        dff = jnp.where(keep, dff, 0.0)
        ds_ref[...] = dff + dga_ref[...]
        db_ref[...] += _colsum(dff)

    rev = lambda i: (nb - 1 - i, 0)
    return pl.pallas_call(
        body, name="fox_gate_bwd", grid=(nb,),
        in_specs=[pl.BlockSpec((tb, LANES), rev), pl.BlockSpec((tb, LANES), rev),
                  pl.BlockSpec((1, LANES), lambda i: (0, 0)), pl.BlockSpec((tb, LANES), rev)],
        out_specs=[pl.BlockSpec((tb, LANES), rev), pl.BlockSpec((1, LANES), lambda i: (0, 0))],
        out_shape=[jax.ShapeDtypeStruct((t, LANES), F32), jax.ShapeDtypeStruct((1, LANES), F32)],
        scratch_shapes=[pltpu.VMEM((1, LANES), F32)],
        compiler_params=_params(("arbitrary",)),
    )(dc, small, b_forget_p, dga)


def _fox_pairs(nb, by_key):
    if by_key:
        pairs = [(qi, ki) for ki in range(nb) for qi in range(ki, nb)]
    else:
        pairs = [(qi, ki) for qi in range(nb) for ki in range(qi + 1)]
    return (jnp.asarray(np.array([p[0] for p in pairs], np.int32)),
            jnp.asarray(np.array([p[1] for p in pairs], np.int32)), len(pairs))


def _fox_specs(tb, fd, c_fq, c_fkv):
    gw = FOX_GROUP * fd
    q0, kv0 = c_fq // gw, c_fkv // (2 * gw)
    return dict(
        q=pl.BlockSpec((tb, gw), lambda g, p, qt, kt: (qt[p], q0 + g)),
        kv=pl.BlockSpec((tb, 2 * gw), lambda g, p, qt, kt: (kt[p], kv0 + g)),
        col=pl.BlockSpec((FOX_GROUP, tb, 1), lambda g, p, qt, kt: (g, qt[p], 0)),
        row=pl.BlockSpec((FOX_GROUP, 1, tb), lambda g, p, qt, kt: (g, 0, kt[p])),
        head=pl.BlockSpec((tb, gw), lambda g, p, qt, kt: (qt[p], g)),
        key_kv=pl.BlockSpec((tb, 2 * gw), lambda g, p, qt, kt: (kt[p], g)),
    )


def _fox_mask(qi, ki, tb):
    row = qi * tb + lax.broadcasted_iota(jnp.int32, (tb, tb), 0)
    col = ki * tb + lax.broadcasted_iota(jnp.int32, (tb, tb), 1)
    return (col <= row) & (col >= PAD)


def _fox_heads(q_ref, kv_ref, fd):
    return [(q_ref[:, hh * fd:(hh + 1) * fd], kv_ref[:, 2 * hh * fd:(2 * hh + 1) * fd],
             kv_ref[:, (2 * hh + 1) * fd:(2 * hh + 2) * fd]) for hh in range(FOX_GROUP)]


def _fox_fwd(proj, c_col, c_row, t, fd, c_fq, c_fkv):
    tb = _pick(t, 384, LANES)
    nb = t // tb
    scale = fd ** -0.5
    sp = _fox_specs(tb, fd, c_fq, c_fkv)
    qt, kt, npairs = _fox_pairs(nb, by_key=False)

    def body(qt_ref, kt_ref, q_ref, kv_ref, cq_ref, ck_ref, o_ref, lse_ref, m_ref, l_ref, acc_ref):
        p = pl.program_id(1)
        qi, ki = qt_ref[p], kt_ref[p]

        @pl.when(ki == 0)
        def _():
            m_ref[...] = jnp.full_like(m_ref, -jnp.inf)
            l_ref[...] = jnp.zeros_like(l_ref)
            acc_ref[...] = jnp.zeros_like(acc_ref)

        def update(masked):
            mask = _fox_mask(qi, ki, tb) if masked else None
            for hh, (q, k, v) in enumerate(_fox_heads(q_ref, kv_ref, fd)):
                s = _dot(q, k, NT) * scale + cq_ref[hh] - ck_ref[hh]
                if masked:
                    s = jnp.where(mask, s, MASK_VALUE)
                m_prev = m_ref[hh]
                m_new = jnp.maximum(m_prev, jnp.max(s, axis=-1, keepdims=True))
                alpha = jnp.exp(m_prev - m_new)
                pe = jnp.exp(s - m_new)
                l_ref[hh] = alpha * l_ref[hh] + jnp.sum(pe, axis=-1, keepdims=True)
                acc_ref[hh] = alpha * acc_ref[hh] + _dot(pe.astype(BF16), v, NN)
                m_ref[hh] = m_new

        edge = (ki == 0) | (ki == qi)
        pl.when(edge)(functools.partial(update, True))
        pl.when(jnp.logical_not(edge))(functools.partial(update, False))

        @pl.when(ki == qi)
        def _():
            real = _row_ids(qi * tb, tb) >= PAD
            for hh in range(FOX_GROUP):
                o_ref[:, hh * fd:(hh + 1) * fd] = jnp.where(real, acc_ref[hh] / l_ref[hh], 0.0)
                lse_ref[hh] = m_ref[hh] + jnp.log(l_ref[hh])

    return pl.pallas_call(
        body, name="fox_fwd",
        grid_spec=pltpu.PrefetchScalarGridSpec(
            num_scalar_prefetch=2, grid=(FOX_HEADS // FOX_GROUP, npairs),
            in_specs=[sp["q"], sp["kv"], sp["col"], sp["row"]],
            out_specs=[sp["head"], sp["col"]],
            scratch_shapes=[pltpu.VMEM((FOX_GROUP, tb, 1), F32), pltpu.VMEM((FOX_GROUP, tb, 1), F32),
                            pltpu.VMEM((FOX_GROUP, tb, fd), F32)]),
        out_shape=[jax.ShapeDtypeStruct((t, FOX_HEADS * fd), F32), jax.ShapeDtypeStruct((FOX_HEADS, t, 1), F32)],
        compiler_params=_params(("parallel", "arbitrary")),
    )(qt, kt, proj, proj, c_col, c_row)


def _fox_delta(o_fox, do_fox, t, fd):
    tb = _pick(t, 384, LANES)

    def body(o_ref, do_ref, out_ref):
        for h in range(FOX_HEADS):
            sl = slice(h * fd, (h + 1) * fd)
            out_ref[h] = jnp.sum(o_ref[:, sl] * do_ref[:, sl].astype(BF16).astype(F32), axis=-1, keepdims=True)

    w = FOX_HEADS * fd
    return pl.pallas_call(
        body, name="fox_delta", grid=(t // tb,),
        in_specs=[pl.BlockSpec((tb, w), lambda i: (i, 0)), pl.BlockSpec((tb, w), lambda i: (i, 0))],
        out_specs=pl.BlockSpec((FOX_HEADS, tb, 1), lambda i: (0, i, 0)),
        out_shape=jax.ShapeDtypeStruct((FOX_HEADS, t, 1), F32),
        compiler_params=_params(("parallel",)),
    )(o_fox, do_fox)


def _fox_bwd(proj, c_col, c_row, lse, delta, do_fox, t, fd, c_fq, c_fkv):
    tb = _pick(t, 384, LANES)
    nb = t // tb
    scale = fd ** -0.5
    sp = _fox_specs(tb, fd, c_fq, c_fkv)
    qt, kt, npairs = _fox_pairs(nb, by_key=True)
    gw = FOX_GROUP * fd

    def body(qt_ref, kt_ref, q_ref, kv_ref, cq_ref, ck_ref, lse_ref, dl_ref, do_ref, dq_ref, dkv_ref, dc_ref, dr_ref,
             dq_acc, dk_acc, dv_acc, dc_acc, dr_acc):
        p = pl.program_id(1)
        qi, ki = qt_ref[p], kt_ref[p]

        @pl.when(p == 0)
        def _():
            dq_acc[...] = jnp.zeros_like(dq_acc)
            dr_acc[...] = jnp.zeros_like(dr_acc)

        @pl.when(qi == ki)
        def _():
            dk_acc[...] = jnp.zeros_like(dk_acc)
            dv_acc[...] = jnp.zeros_like(dv_acc)
            dc_acc[...] = jnp.zeros_like(dc_acc)

        rows = pl.ds(pl.multiple_of(qi * tb, LANES), tb)

        def update(masked):
            mask = _fox_mask(qi, ki, tb) if masked else None
            for hh, (q, k, v) in enumerate(_fox_heads(q_ref, kv_ref, fd)):
                do = _bf(do_ref[:, hh * fd:(hh + 1) * fd])
                s = _dot(q, k, NT) * scale + cq_ref[hh] - ck_ref[hh]
                if masked:
                    s = jnp.where(mask, s, MASK_VALUE)
                pr = jnp.exp(s - lse_ref[hh])
                dp = _dot(do, v, NT)
                ds = pr * (dp - dl_ref[hh])
                ds16 = ds.astype(BF16)
                dv_acc[hh] += _dot(pr.astype(BF16), do, TN)
                dk_acc[hh] += _dot(ds16, q, TN)
                dc_acc[hh] += _colsum(ds)
                dr_acc[hh, rows, :] += jnp.sum(ds, axis=-1, keepdims=True)
                dq_acc[hh, rows, :] += _dot(ds16, k, NN)

        edge = (ki == 0) | (ki == qi)
        pl.when(edge)(functools.partial(update, True))
        pl.when(jnp.logical_not(edge))(functools.partial(update, False))

        @pl.when(qi == nb - 1)
        def _():
            for hh in range(FOX_GROUP):
                dkv_ref[:, 2 * hh * fd:(2 * hh + 1) * fd] = (dk_acc[hh] * scale).astype(dkv_ref.dtype)
                dkv_ref[:, (2 * hh + 1) * fd:(2 * hh + 2) * fd] = dv_acc[hh].astype(dkv_ref.dtype)
                dc_ref[hh] = -dc_acc[hh]

        @pl.when(p == npairs - 1)
        def _():
            for hh in range(FOX_GROUP):
                dq_ref[:, hh * fd:(hh + 1) * fd] = (dq_acc[hh] * scale).astype(dq_ref.dtype)
            dr_ref[...] = dr_acc[...]

    return pl.pallas_call(
        body, name="fox_bwd",
        grid_spec=pltpu.PrefetchScalarGridSpec(
            num_scalar_prefetch=2, grid=(FOX_HEADS // FOX_GROUP, npairs),
            in_specs=[sp["q"], sp["kv"], sp["col"], sp["row"], sp["col"], sp["col"], sp["head"]],
            out_specs=[pl.BlockSpec((t, gw), lambda g, p, qt, kt: (0, g)), sp["key_kv"], sp["row"],
                       pl.BlockSpec((FOX_GROUP, t, 1), lambda g, p, qt, kt: (g, 0, 0))],
            scratch_shapes=[pltpu.VMEM((FOX_GROUP, t, fd), F32), pltpu.VMEM((FOX_GROUP, tb, fd), F32),
                            pltpu.VMEM((FOX_GROUP, tb, fd), F32), pltpu.VMEM((FOX_GROUP, 1, tb), F32),
                            pltpu.VMEM((FOX_GROUP, t, 1), F32)]),
        out_shape=[jax.ShapeDtypeStruct((t, FOX_HEADS * fd), BF16), jax.ShapeDtypeStruct((t, 2 * FOX_HEADS * fd), BF16),
                   jax.ShapeDtypeStruct((FOX_HEADS, 1, t), F32), jax.ShapeDtypeStruct((FOX_HEADS, t, 1), F32)],
        compiler_params=_params(("parallel", "arbitrary")),
    )(qt, kt, proj, proj, c_col, c_row, lse, delta, do_fox)


def _gla_gate_fwd(small, w_alpha_p, b_alpha, t, gk):
    def fn(row0, s, w, b):
        z = _dot(s.astype(BF16), w, NN) + b
        return (jnp.where(_row_ids(row0, s.shape[0]) >= PAD, _log_sigmoid(z) * (1.0 / GLA_TAU), 0.0),)

    return _ew("gla_gate_fwd", fn, [("tile", small, LANES, 0), ("full", w_alpha_p), ("full", b_alpha)],
               [("tile", F32, gk)], t, 264)


def _gla_gate_bwd(dglog, small, w_alpha_p, b_alpha, t, gk):
    def fn(row0, dg, s, w, b):
        z = _dot(s.astype(BF16), w, NN) + b
        dz = jnp.where(_row_ids(row0, s.shape[0]) >= PAD, dg * (1.0 / GLA_TAU) * _sigmoid(-z), 0.0)
        return dz, _colsum(dz)

    return _ew("gla_gate_bwd", fn,
               [("tile", dglog, gk, 0), ("tile", small, LANES, 0), ("full", w_alpha_p), ("full", b_alpha)],
               [("tile", BF16, gk), ("acc", F32, gk)], t, 264)


def _gla_chunk(q, k, g, scale, cs):
    r = lax.broadcasted_iota(jnp.int32, (cs, cs), 0)
    c = lax.broadcasted_iota(jnp.int32, (cs, cs), 1)
    causal = c <= r
    b = _tri_dot(causal.astype(BF16), g)
    bl = b[cs - 1:cs, :]
    eb, einv, eend = jnp.exp(b), jnp.exp(-b), jnp.exp(bl - b)
    qd = q.astype(F32) * scale * eb
    kf = k.astype(F32)
    return causal, (eb, einv, eend), bl, qd, kf * einv, kf * eend


def _gla_fwd(proj, glog, t, dk, dv, c_q, c_k, c_v):
    cs = GLA_CHUNK
    nc = t // cs
    wk, wv = GLA_HEADS * dk, GLA_HEADS * dv
    scale = dk ** -0.5

    def body(q_ref, k_ref, v_ref, g_ref, o_ref, sp_ref, st_ref):
        @pl.when(pl.program_id(0) == 0)
        def _():
            st_ref[...] = jnp.zeros_like(st_ref)

        for h in range(GLA_HEADS):
            ks, vs = slice(h * dk, (h + 1) * dk), slice(h * dv, (h + 1) * dv)
            v = v_ref[:, vs]
            causal, _, bl, qd, ki, ke = _gla_chunk(q_ref[:, ks], k_ref[:, ks], g_ref[:, ks], scale, cs)
            st = st_ref[h]
            sp_ref[h] = st
            a = jnp.where(causal, _dot(qd.astype(BF16), ki.astype(BF16), NT), 0.0)
            o_ref[:, vs] = _dot(a.astype(BF16), v, NN) + _dot(qd.astype(BF16), st.astype(BF16), NT)
            st_ref[h] = st * jnp.exp(bl) + _dot(v, ke.astype(BF16), TN)

    return pl.pallas_call(
        body, name="gla_fwd", grid=(nc,),
        in_specs=[pl.BlockSpec((cs, wk), lambda n: (n, c_q // wk)), pl.BlockSpec((cs, wk), lambda n: (n, c_k // wk)),
                  pl.BlockSpec((cs, wv), lambda n: (n, c_v // wv)), pl.BlockSpec((cs, wk), lambda n: (n, 0))],
        out_specs=[pl.BlockSpec((cs, wv), lambda n: (n, 0)),
                   pl.BlockSpec((None, GLA_HEADS, dv, dk), lambda n: (n, 0, 0, 0))],
        out_shape=[jax.ShapeDtypeStruct((t, wv), F32), jax.ShapeDtypeStruct((nc, GLA_HEADS, dv, dk), F32)],
        scratch_shapes=[pltpu.VMEM((GLA_HEADS, dv, dk), F32)],
        compiler_params=_params(("arbitrary",)),
    )(proj, proj, proj, glog)


def _gla_bwd(proj, glog, s_prev, do_raw, t, dk, dv, c_q, c_k, c_v):
    cs = GLA_CHUNK
    nc = t // cs
    wk, wv = GLA_HEADS * dk, GLA_HEADS * dv
    scale = dk ** -0.5

    def body(q_ref, k_ref, v_ref, g_ref, sp_ref, do_ref, dq_ref, dk_ref, dv_ref, dg_ref, dst_ref):
        @pl.when(pl.program_id(0) == 0)
        def _():
            dst_ref[...] = jnp.zeros_like(dst_ref)

        for h in range(GLA_HEADS):
            ks, vs = slice(h * dk, (h + 1) * dk), slice(h * dv, (h + 1) * dv)
            v = v_ref[:, vs]
            do = do_ref[:, vs].astype(BF16)
            causal, (eb, einv, eend), bl, qd, ki, ke = _gla_chunk(q_ref[:, ks], k_ref[:, ks], g_ref[:, ks], scale, cs)
            qd16, ki16, ke16 = qd.astype(BF16), ki.astype(BF16), ke.astype(BF16)
            st = sp_ref[h]
            dst = dst_ref[h]
            dst16 = dst.astype(BF16)
            a = jnp.where(causal, _dot(qd16, ki16, NT), 0.0).astype(BF16)
            da = jnp.where(causal, _dot(do, v, NT), 0.0).astype(BF16)
            dvv = _dot(a, do, TN) + _dot(ke16, dst16, NT)
            dqd = _dot(da, ki16, NN) + _dot(do, st.astype(BF16), NN)
            dki = _dot(da, qd16, TN)
            dke = _dot(v, dst16, NN)
            dl = jnp.exp(bl)
            ddl = _colsum(dst * st)
            dst_ref[h] = dst * dl + _dot(do, qd16, TN)
            dq_ref[:, ks] = (dqd * eb * scale).astype(dq_ref.dtype)
            dk_ref[:, ks] = (dki * einv + dke * eend).astype(dk_ref.dtype)
            dv_ref[:, vs] = dvv.astype(dv_ref.dtype)
            db = dqd * qd - dki * ki - dke * ke
            db_last = _colsum(dke * ke) + ddl * dl
            r = lax.broadcasted_iota(jnp.int32, (cs, cs), 0)
            c = lax.broadcasted_iota(jnp.int32, (cs, cs), 1)
            dg_ref[:, ks] = _tri_dot((c >= r).astype(BF16), db) + db_last

    rev = lambda f: (lambda n: f(nc - 1 - n))
    return pl.pallas_call(
        body, name="gla_bwd", grid=(nc,),
        in_specs=[pl.BlockSpec((cs, wk), rev(lambda n: (n, c_q // wk))), pl.BlockSpec((cs, wk), rev(lambda n: (n, c_k // wk))),
                  pl.BlockSpec((cs, wv), rev(lambda n: (n, c_v // wv))), pl.BlockSpec((cs, wk), rev(lambda n: (n, 0))),
                  pl.BlockSpec((None, GLA_HEADS, dv, dk), rev(lambda n: (n, 0, 0, 0))),
                  pl.BlockSpec((cs, wv), rev(lambda n: (n, 0)))],
        out_specs=[pl.BlockSpec((cs, wk), rev(lambda n: (n, 0))), pl.BlockSpec((cs, wk), rev(lambda n: (n, 0))),
                   pl.BlockSpec((cs, wv), rev(lambda n: (n, 0))), pl.BlockSpec((cs, wk), rev(lambda n: (n, 0)))],
        out_shape=[jax.ShapeDtypeStruct((t, wk), BF16), jax.ShapeDtypeStruct((t, wk), BF16),
                   jax.ShapeDtypeStruct((t, wv), BF16), jax.ShapeDtypeStruct((t, wk), F32)],
        scratch_shapes=[pltpu.VMEM((GLA_HEADS, dv, dk), F32)],
        compiler_params=_params(("arbitrary",)),
    )(proj, proj, proj, glog, s_prev, do_raw)


def _gla_post_fwd(o_raw, proj, gn, t, dv, c_gr):
    w = GLA_HEADS * dv

    def fn(row0, o, gr, g):
        gr = gr.astype(F32)
        outs = []
        for h in range(GLA_HEADS):
            oh = o[:, h * dv:(h + 1) * dv]
            outs.append(oh * lax.rsqrt(jnp.mean(oh * oh, axis=-1, keepdims=True) + EPS))
        on = jnp.concatenate(outs, axis=1) * g
        return (on * (gr * _sigmoid(gr)),)

    return _ew("gla_post_fwd", fn, [("tile", o_raw, w, 0), ("tile", proj, w, c_gr), ("full", gn)],
               [("tile", BF16, w)], t, 264)


def _gla_post_bwd(o_raw, proj, gn, do_gla, t, dv, c_gr):
    w = GLA_HEADS * dv

    def fn(row0, o, gr, g, do):
        gr = gr.astype(F32)
        sg = _sigmoid(gr)
        don = do * (gr * sg)
        ohs, dos = [], []
        for h in range(GLA_HEADS):
            sl = slice(h * dv, (h + 1) * dv)
            oh = o[:, sl]
            r = lax.rsqrt(jnp.mean(oh * oh, axis=-1, keepdims=True) + EPS)
            xh = oh * r
            dxh = don[:, sl] * g[:, sl]
            ohs.append(xh)
            dos.append(r * (dxh - xh * jnp.mean(dxh * xh, axis=-1, keepdims=True)))
        xh = jnp.concatenate(ohs, axis=1)
        dgr = do * (xh * g) * (sg * (1.0 + gr * (1.0 - sg)))
        return jnp.concatenate(dos, axis=1), dgr, _colsum(don * xh)

    return _ew("gla_post_bwd", fn,
               [("tile", o_raw, w, 0), ("tile", proj, w, c_gr), ("full", gn), ("tile", do_gla, w, 0)],
               [("tile", F32, w), ("tile", BF16, w), ("acc", F32, w)], t, 264)


def _adamw(name, w, g, m, v):
    nl, rows, cols = w.shape
    tm = _pick(rows, max(8, (512 * 1024) // max(cols, 1) // 8 * 8), 8)

    def body(w_ref, g_ref, m_ref, v_ref, go_ref, d_ref, nm_ref, nv_ref):
        gg = g_ref[...]
        nm = ADAM_B1 * m_ref[...] + (1.0 - ADAM_B1) * gg
        nv = ADAM_B2 * v_ref[...] + (1.0 - ADAM_B2) * (gg * gg)
        m_hat = nm / (1.0 - ADAM_B1 ** ADAM_STEP)
        v_hat = nv / (1.0 - ADAM_B2 ** ADAM_STEP)
        go_ref[...] = gg
        d_ref[...] = -ADAM_LR * (m_hat / (jnp.sqrt(v_hat) + ADAM_EPS) + ADAM_WD * w_ref[...])
        nm_ref[...] = nm
        nv_ref[...] = nv

    spec = pl.BlockSpec((None, tm, cols), lambda l, i: (l, i, 0))
    return pl.pallas_call(
        body, name=name, grid=(nl, rows // tm), in_specs=[spec] * 4, out_specs=[spec] * 4,
        out_shape=[jax.ShapeDtypeStruct((nl, rows, cols), F32)] * 4,
        compiler_params=_params(("parallel", "parallel")),
    )(w, g, m, v)


def _me():
    return lax.axis_index("x"), lax.axis_index("y"), lax.axis_index("c")


def _hbm_specs(n):
    return [pl.BlockSpec(memory_space=pl.ANY)] * n


def _gather_weights(shards):
    n = len(shards)

    def body(*refs):
        src, dst = refs[:n], refs[n:2 * n]
        send_sems, recv_sems = refs[2 * n:]
        x, y, c = _me()
        sibling = (x, y, 1 - c)
        chips = [(1 - x, y), (x, 1 - y), (1 - x, 1 - y)]

        def slot(i, layer, px, py):
            return dst[i].at[layer, 2 * px + py]

        first, passed = [], []
        for i in range(n):
            for j, chip in enumerate(chips):
                cp = pltpu.make_async_remote_copy(
                    src_ref=src[i].at[c], dst_ref=slot(i, c, x, y), send_sem=send_sems.at[6 * i + j],
                    recv_sem=recv_sems.at[6 * i + j], device_id=(*chip, c), device_id_type=MESH)
                cp.start()
                first.append(cp)

        def landed(i, j, layer, chip):
            return pltpu.make_async_remote_copy(
                src_ref=slot(i, layer, *chip), dst_ref=slot(i, layer, *chip), send_sem=send_sems.at[6 * i + j],
                recv_sem=recv_sems.at[6 * i + j], device_id=sibling, device_id_type=MESH)

        for i in range(n):
            for j, chip in enumerate(chips):
                landed(i, j, c, chip).wait_recv()
                cp = landed(i, 3 + j, c, chip)
                cp.start()
                passed.append(cp)
        for i in range(n):
            for j, chip in enumerate(chips):
                landed(i, 3 + j, 1 - c, chip).wait_recv()
        for cp in first + passed:
            cp.wait_send()

    out_shape = [jax.ShapeDtypeStruct((DEPTH, N_CHIPS) + s.shape[1:], s.dtype) for s in shards]
    return pl.pallas_call(
        body, name="gather_weights", in_specs=_hbm_specs(n), out_specs=_hbm_specs(n), out_shape=out_shape,
        scratch_shapes=[pltpu.SemaphoreType.DMA((6 * n,)), pltpu.SemaphoreType.DMA((6 * n,))],
    )(*shards)


def _all_gather_small(v):
    def body(v_ref, out_ref, send_sems, recv_sems, local_sem):
        x, y, c = _me()
        mine = pltpu.make_async_copy(v_ref, out_ref.at[4 * x + 2 * y + c], local_sem)
        mine.start()
        copies = []
        for k in range(1, N_DEV):
            peer = (x ^ ((k >> 2) & 1), y ^ ((k >> 1) & 1), c ^ (k & 1))
            cp = pltpu.make_async_remote_copy(
                src_ref=v_ref, dst_ref=out_ref.at[4 * x + 2 * y + c], send_sem=send_sems.at[k - 1],
                recv_sem=recv_sems.at[k - 1], device_id=peer, device_id_type=MESH)
            cp.start()
            copies.append(cp)
        for cp in copies:
            cp.wait_recv()
        for cp in copies:
            cp.wait_send()
        mine.wait()

    return pl.pallas_call(
        body, name="all_gather_small", in_specs=_hbm_specs(1), out_specs=pl.BlockSpec(memory_space=pl.ANY),
        out_shape=jax.ShapeDtypeStruct((N_DEV,) + v.shape, v.dtype),
        scratch_shapes=[pltpu.SemaphoreType.DMA((N_DEV - 1,)), pltpu.SemaphoreType.DMA((N_DEV - 1,)),
                        pltpu.SemaphoreType.DMA],
    )(v)


def _swap_layers(layer0, layer1):
    n = len(layer0)

    def body(*refs):
        src0, src1, dst = refs[:n], refs[n:2 * n], refs[2 * n:3 * n]
        send_sems, recv_sems = refs[3 * n:]
        x, y, c = _me()

        def copy(i, src):
            return pltpu.make_async_remote_copy(
                src_ref=src[i], dst_ref=dst[i], send_sem=send_sems.at[i], recv_sem=recv_sems.at[i],
                device_id=(x, y, 1 - c), device_id_type=MESH)

        for i in range(n):
            @pl.when(c == 0)
            def _():
                copy(i, src1).start()

            @pl.when(c == 1)
            def _():
                copy(i, src0).start()

        for i in range(n):
            copy(i, src0).wait_recv()
        for i in range(n):
            copy(i, src0).wait_send()

    out_shape = [jax.ShapeDtypeStruct(g.shape, g.dtype) for g in layer0]
    return pl.pallas_call(
        body, name="swap_layers", in_specs=_hbm_specs(2 * n), out_specs=_hbm_specs(n), out_shape=out_shape,
        scratch_shapes=[pltpu.SemaphoreType.DMA((n,)), pltpu.SemaphoreType.DMA((n,))],
    )(*layer0, *layer1)


def _scatter_chips(parts):
    n = len(parts)

    def body(*refs):
        src, dst = refs[:n], refs[n:2 * n]
        send_sems, recv_sems = refs[2 * n:]
        x, y, c = _me()
        me = 2 * x + y
        chips = [(1 - x, y), (x, 1 - y), (1 - x, 1 - y)]
        copies = []
        for i in range(n):
            for j, (px, py) in enumerate(chips):
                cp = pltpu.make_async_remote_copy(
                    src_ref=src[i].at[2 * px + py], dst_ref=dst[i].at[me], send_sem=send_sems.at[3 * i + j],
                    recv_sem=recv_sems.at[3 * i + j], device_id=(px, py, c), device_id_type=MESH)
                cp.start()
                copies.append(cp)
        for cp in copies:
            cp.wait_recv()
        for cp in copies:
            cp.wait_send()

    out_shape = [jax.ShapeDtypeStruct(p.shape, p.dtype) for p in parts]
    return pl.pallas_call(
        body, name="scatter_chips", in_specs=_hbm_specs(n), out_specs=_hbm_specs(n), out_shape=out_shape,
        scratch_shapes=[pltpu.SemaphoreType.DMA((3 * n,)), pltpu.SemaphoreType.DMA((3 * n,))],
    )(*parts)


def _share_layers(totals):
    n = len(totals)

    def body(*refs):
        buf = refs[n:2 * n]
        send_sems, recv_sems = refs[2 * n:]
        x, y, c = _me()
        copies = []
        for i in range(n):
            cp = pltpu.make_async_remote_copy(
                src_ref=buf[i].at[c], dst_ref=buf[i].at[c], send_sem=send_sems.at[i], recv_sem=recv_sems.at[i],
                device_id=(x, y, 1 - c), device_id_type=MESH)
            cp.start()
            copies.append(cp)
        for cp in copies:
            cp.wait_recv()
        for cp in copies:
            cp.wait_send()

    out_shape = [jax.ShapeDtypeStruct(s.shape, s.dtype) for s in totals]
    return pl.pallas_call(
        body, name="share_layers", in_specs=_hbm_specs(n), out_specs=_hbm_specs(n), out_shape=out_shape,
        input_output_aliases={i: i for i in range(n)},
        scratch_shapes=[pltpu.SemaphoreType.DMA((n,)), pltpu.SemaphoreType.DMA((n,))],
    )(*totals)


def _pair_sum(name, own0, own1, recv, layer):
    nch, r, cdim = recv.shape
    tm = _pick(r, max(8, (512 * 1024) // cdim // 16 * 16), 16)

    def body(l_ref, a0_ref, a1_ref, b_ref, s_ref, s16_ref):
        s = jnp.where(l_ref[0] == 0, a0_ref[...], a1_ref[...]) + b_ref[...]
        s_ref[...] = s
        s16_ref[...] = s.astype(BF16)

    spec = pl.BlockSpec((None, tm, cdim), lambda j, i, l: (j, i, 0))

    def own_spec(which):
        return pl.BlockSpec((None, tm, cdim),
                            lambda j, i, l: (jnp.where(l[0] == which, j, 0), jnp.where(l[0] == which, i, 0), 0))

    return pl.pallas_call(
        body, name=name,
        grid_spec=pltpu.PrefetchScalarGridSpec(
            num_scalar_prefetch=1, grid=(nch, r // tm),
            in_specs=[own_spec(0), own_spec(1), spec], out_specs=[spec, spec]),
        out_shape=[jax.ShapeDtypeStruct((nch, r, cdim), F32), jax.ShapeDtypeStruct((nch, r, cdim), BF16)],
        compiler_params=_params(("arbitrary", "arbitrary")),
    )(layer, own0, own1, recv)


def _chip_sum(name, own, recv, chip, layer):
    nch, r, cdim = own.shape
    tm = _pick(r, max(8, (512 * 1024) // cdim // 16 * 16), 16)

    def body(c_ref, l_ref, own_ref, *rest):
        recv_refs, out_ref = rest[:nch], rest[nch]
        me = c_ref[0]
        acc = None
        for j in range(nch):
            term = jnp.where(me == j, own_ref[...], recv_refs[j][...].astype(F32))
            acc = term if acc is None else acc + term
        out_ref[...] = acc

    recv_specs = [pl.BlockSpec((None, tm, cdim), functools.partial(lambda i, c, l, j: (j, i, 0), j=j))
                  for j in range(nch)]
    return pl.pallas_call(
        body, name=name,
        grid_spec=pltpu.PrefetchScalarGridSpec(
            num_scalar_prefetch=2, grid=(r // tm,),
            in_specs=[pl.BlockSpec((None, tm, cdim), lambda i, c, l: (c[0], i, 0))] + recv_specs,
            out_specs=pl.BlockSpec((None, tm, cdim), lambda i, c, l: (l[0], i, 0))),
        out_shape=jax.ShapeDtypeStruct((DEPTH, r, cdim), F32),
        compiler_params=_params(("parallel",)),
    )(chip, layer, own, *([recv] * nch))


def _sum_devices(gathered):
    _, r, cdim = gathered.shape

    def body(g_ref, o_ref):
        acc = g_ref[0]
        for k in range(1, N_DEV):
            acc = acc + g_ref[k]
        o_ref[...] = acc

    return pl.pallas_call(
        body, name="sum_devices", out_shape=jax.ShapeDtypeStruct((r, cdim), F32),
        compiler_params=_params(),
    )(gathered)


class _Layout:
    def __init__(self, d):
        self.d = d
        self.fw = d // 2
        self.fd = self.fw // FOX_HEADS
        self.gk = d // 2
        self.gv = d
        self.dk = self.gk // GLA_HEADS
        self.dv = self.gv // GLA_HEADS
        self.c_fq = 0
        self.c_gq = self.fw
        self.c_gv = self.c_gq + self.gk
        self.c_gr = self.c_gv + self.gv
        self.c_fkv = self.c_gr + self.gv
        self.c_gates = self.c_fkv + 2 * self.fw
        self.c_gk = self.c_gates + 2 * d
        self.c_small = self.c_gk + self.gk
        self.n_main = self.c_small
        self.n_p = self.c_small + LANES
        self.o_fk = self.fw
        self.o_fv = 2 * self.fw
        self.o_ff = 3 * self.fw
        self.o_gq = self.o_ff + FOX_HEADS
        self.o_gk = self.o_gq + self.gk
        self.o_gv = self.o_gk + self.gk
        self.o_gr = self.o_gv + self.gv
        self.o_ga = self.o_gr + self.gv
        self.o_gf = self.o_ga + GLA_RANK
        self.o_gg = self.o_gf + d
        self.n_orig = self.o_gg + d

    def to_p(self, shards):
        per = self.n_orig // N_CHIPS
        ranges = [(0, self.fw), (self.o_gq, self.gk), (self.o_gv, self.gv), (self.o_gr, self.gv)]
        for h in range(FOX_HEADS):
            ranges += [(self.o_fk + h * self.fd, self.fd), (self.o_fv + h * self.fd, self.fd)]
        ranges += [(self.o_gf, 2 * self.d), (self.o_gk, self.gk), (self.o_ff, FOX_HEADS), (self.o_ga, GLA_RANK)]
        pieces = []
        for a, width in ranges:
            for j in range(a // per, (a + width - 1) // per + 1):
                lo, hi = max(a, j * per), min(a + width, (j + 1) * per)
                pieces.append(shards[j][:, lo - j * per:hi - j * per])
        pieces.append(jnp.zeros((shards.shape[1], LANES - FOX_HEADS - GLA_RANK), shards.dtype))
        return jnp.concatenate(pieces, axis=1)

    def from_segments(self, seg):
        per = self.n_orig // N_CHIPS
        fd = self.fd
        atoms = [("fq", 0, self.fw)]
        atoms += [("fkv", 2 * h * fd, fd) for h in range(FOX_HEADS)]
        atoms += [("fkv", (2 * h + 1) * fd, fd) for h in range(FOX_HEADS)]
        atoms += [("small", 0, FOX_HEADS), ("gq", 0, self.gk), ("gk", 0, self.gk), ("gv", 0, self.gv),
                  ("gr", 0, self.gv), ("small", FOX_HEADS, GLA_RANK), ("gates", 0, 2 * self.d)]
        shards = [[] for _ in range(N_CHIPS)]
        pos = 0
        for name, c0, width in atoms:
            for j in range(pos // per, (pos + width - 1) // per + 1):
                lo, hi = max(pos, j * per), min(pos + width, (j + 1) * per)
                shards[j].append(seg[name][:, c0 + lo - pos:c0 + hi - pos])
            pos += width
        assert pos == self.n_orig
        return jnp.stack([jnp.concatenate(s, axis=1) for s in shards])


def _layer_fwd(lay, h, p, t):
    d = lay.d
    xn = _rms_fwd("rms_mix_fwd", h, p["norm_mix_g"], t, d)
    proj = _mm("mm_proj", xn, p["w_in"], mode="nn", m=t, n=lay.n_main, k=d, out_dtypes=(BF16,))
    small = _mm("mm_small", xn, p["w_in"], mode="nn", m=t, n=LANES, k=d, b_c0=lay.c_small)
    cs = _fox_gate_fwd(small, p["b_forget_p"], t)
    ct = cs[:, :FOX_HEADS].T
    c_col, c_row = ct[:, :, None], ct[:, None, :]
    o_fox, lse = _fox_fwd(proj, c_col, c_row, t, lay.fd, lay.c_fq, lay.c_fkv)
    glog = _gla_gate_fwd(small, p["w_alpha_p"], p["b_alpha"], t, lay.gk)
    o_raw, s_prev = _gla_fwd(proj, glog, t, lay.dk, lay.dv, lay.c_gq, lay.c_gk, lay.c_gv)
    o_gla = _gla_post_fwd(o_raw, proj, p["gla_norm_g"], t, lay.dv, lay.c_gr)
    a_fox = _mm("mm_o_fox", o_fox, p["w_o_fox"], mode="nn", m=t, n=d, k=lay.fw, b_shards=N_CHIPS)
    a_gla = _mm("mm_o_gla", o_gla, p["w_o_gla"], mode="nn", m=t, n=d, k=lay.gv)
    y = _merge_fwd(a_fox, a_gla, proj, lay.c_gates, t, d)
    h1 = _mm("mm_out", y, p["w_out"], mode="nn", m=t, n=d, k=d, extras=[h], epilogue=lambda acc, res: (res + acc,))
    xn2 = _rms_fwd("rms_mlp_fwd", h1, p["norm_mlp_g"], t, d)
    u, act = _mm("mm_ff1", xn2, p["w_ff1"], mode="nn", m=t, n=4 * d, k=d, out_dtypes=(BF16, BF16), b_shards=N_CHIPS,
                 epilogue=lambda acc: (acc, jnp.square(jnp.maximum(acc, 0.0))))
    h2 = _mm("mm_ff2", act, p["w_ff2"], mode="nn", m=t, n=d, k=4 * d, extras=[h1],
             epilogue=lambda acc, res: (res + acc,))
    saved = dict(h=h, xn=xn, proj=proj, small=small, c_col=c_col, c_row=c_row, o_fox=o_fox, lse=lse, glog=glog,
                 o_raw=o_raw, s_prev=s_prev, o_gla=o_gla, a_fox=a_fox, a_gla=a_gla, y=y, h1=h1, xn2=xn2, u=u, act=act)
    return h2, saved


def _layer_bwd(lay, dh2, p, s, t):
    d = lay.d
    g = {}
    du = _mm("mm_dact", dh2, p["w_ff2"], mode="nt", m=t, n=4 * d, k=d, extras=[s["u"]], out_dtypes=(BF16,),
             epilogue=lambda acc, u: (acc * (2.0 * jnp.maximum(u.astype(F32), 0.0)),))
    g["w_ff2"] = _mm("mm_dw_ff2", s["act"], dh2, mode="tn", m=4 * d, n=d, k=t)
    g["w_ff1"] = _mm("mm_dw_ff1", s["xn2"], du, mode="tn", m=d, n=4 * d, k=t, out_shards=N_CHIPS)
    dxn2 = _mm("mm_dxn2", du, p["w_ff1"], mode="nt", m=t, n=d, k=4 * d, b_shards=N_CHIPS)
    dh1, g["norm_mlp_g"] = _rms_bwd("rms_mlp_bwd", s["h1"], p["norm_mlp_g"], dxn2, dh2, t, d)
    dy = _mm("mm_dy", dh1, p["w_out"], mode="nt", m=t, n=d, k=d)
    g["w_out"] = _mm("mm_dw_out", s["y"], dh1, mode="tn", m=d, n=d, k=t)
    da_fox, da_gla, dgates = _merge_bwd(dy, s["a_fox"], s["a_gla"], s["proj"], lay.c_gates, t, d)
    g["w_o_fox"] = _mm("mm_dw_o_fox", s["o_fox"], da_fox, mode="tn", m=lay.fw, n=d, k=t, out_shards=N_CHIPS)
    do_fox = _mm("mm_do_fox", da_fox, p["w_o_fox"], mode="nt", m=t, n=lay.fw, k=d, b_shards=N_CHIPS)
    g["w_o_gla"] = _mm("mm_dw_o_gla", s["o_gla"], da_gla, mode="tn", m=lay.gv, n=d, k=t)
    do_gla = _mm("mm_do_gla", da_gla, p["w_o_gla"], mode="nt", m=t, n=lay.gv, k=d)
    do_raw, dgr, g["gla_norm_g"] = _gla_post_bwd(s["o_raw"], s["proj"], p["gla_norm_g"], do_gla, t, lay.dv, lay.c_gr)
    dgq, dgk, dgv, dglog = _gla_bwd(s["proj"], s["glog"], s["s_prev"], do_raw, t, lay.dk, lay.dv,
                                    lay.c_gq, lay.c_gk, lay.c_gv)
    dz, g["b_alpha"] = _gla_gate_bwd(dglog, s["small"], p["w_alpha_p"], p["b_alpha"], t, lay.gk)
    g["w_alpha_p"] = _mm("mm_dw_alpha", s["small"], dz, mode="tn", m=LANES, n=lay.gk, k=t)
    dga = _mm("mm_dga", dz, p["w_alpha_p"], mode="nt", m=t, n=LANES, k=lay.gk)
    delta = _fox_delta(s["o_fox"], do_fox, t, lay.fd)
    dfq, dfkv, dc, dr = _fox_bwd(s["proj"], s["c_col"], s["c_row"], s["lse"], delta, do_fox, t, lay.fd,
                                 lay.c_fq, lay.c_fkv)
    dc_p = jnp.pad((dc[:, 0, :] + dr[:, :, 0]).T, ((0, 0), (0, LANES - FOX_HEADS)))
    dsmall, g["b_forget_p"] = _fox_gate_bwd(dc_p, s["small"], p["b_forget_p"], dga, t)
    segs = [("fq", dfq, lay.c_fq), ("gq", dgq, lay.c_gq), ("gv", dgv, lay.c_gv), ("gr", dgr, lay.c_gr),
            ("fkv", dfkv, lay.c_fkv), ("gates", dgates, lay.c_gates), ("gk", dgk, lay.c_gk),
            ("small", dsmall, lay.c_small)]
    dxn = None
    dw_in = {}
    for nm, dseg, c0 in segs:
        width = dseg.shape[1]
        dw_in[nm] = _mm("mm_dw_in_" + nm, s["xn"], dseg, mode="tn", m=d, n=width, k=t)
        if dxn is None:
            dxn = _mm("mm_dxn_" + nm, dseg, p["w_in"], mode="nt", m=t, n=d, k=width, b_c0=c0)
        else:
            dxn = _mm("mm_dxn_" + nm, dseg, p["w_in"], mode="nt", m=t, n=d, k=width, b_c0=c0, extras=[dxn],
                      epilogue=lambda acc, prev: (prev + acc,))
    g["w_in"] = lay.from_segments(dw_in)
    dh, g["norm_mix_g"] = _rms_bwd("rms_mix_bwd", s["h"], p["norm_mix_g"], dxn, dh1, t, d)
    return dh, g


def _sequence_step(x, target, meta, layers, final_g):
    seq, d = x.shape
    t = seq + ROW0
    lay = _Layout(d)
    h = jnp.pad(x, ((ROW0, 0), (0, 0))).at[PAD:ROW0].set(meta)
    target_p = jnp.pad(target, ((ROW0, 0), (0, 0)))
    saved = []
    for p in layers:
        h, s = _layer_fwd(lay, h, p, t)
        saved.append(s)
    dh, dg_final, loss_part = _loss_head(h, final_g, target_p, t, d)
    grads = [None] * len(layers)
    for l in reversed(range(len(layers))):
        dh, grads[l] = _layer_bwd(lay, dh, layers[l], saved[l], t)
    return loss_part, dh[ROW0:], dh[PAD:ROW0], grads, dg_final


_SMALL_ROWS = 32


def _pack_small(d, meta, mix, gla, mlp, final, b_alpha, b_forget):
    rows = [meta.reshape(N_META, d), mix.reshape(DEPTH, d), gla.reshape(DEPTH, d), mlp.reshape(DEPTH, d),
            final.reshape(1, d), b_alpha.reshape(1, d),
            jnp.pad(b_forget.reshape(1, DEPTH * FOX_HEADS), ((0, 0), (0, d - DEPTH * FOX_HEADS)))]
    packed = jnp.concatenate(rows, axis=0)
    return jnp.pad(packed, ((0, _SMALL_ROWS - packed.shape[0]), (0, 0)))


def _unpack_small(d, packed):
    return dict(meta=packed[:N_META], norm_mix_g=packed[16:18], gla_norm_g=packed[18:20], norm_mlp_g=packed[20:22],
                final_norm_g=packed[22], b_alpha=packed[23].reshape(DEPTH, d // 2),
                b_forget=packed[24, :DEPTH * FOX_HEADS].reshape(DEPTH, FOX_HEADS))


_BIG = ("w_in", "w_alpha2", "w_o_fox", "w_o_gla", "w_out", "w_ff1", "w_ff2")
_COL_SHARDED = ("w_in", "w_alpha2", "w_o_fox", "w_ff1")


def _full_matrix(name, gathered_l):
    nch, r, c = gathered_l.shape
    if name in _COL_SHARDED:
        return gathered_l.transpose(1, 0, 2).reshape(r, nch * c)
    return gathered_l.reshape(nch * r, c)


def _shard_major(name, full):
    r, c = full.shape
    if name in _COL_SHARDED:
        return full.reshape(r, N_CHIPS, c // N_CHIPS).transpose(1, 0, 2)
    return full.reshape(N_CHIPS, r // N_CHIPS, c)


def kernel(x, meta_tokens, norm_mix_g, w_in, b_forget, w_alpha2, b_alpha, gla_norm_g, w_o_fox, w_o_gla, w_out, norm_mlp_g, w_ff1, w_ff2, final_norm_g, loss_target, m_meta_tokens, m_norm_mix_g, m_w_in, m_b_forget, m_w_alpha2, m_b_alpha, m_gla_norm_g, m_w_o_fox, m_w_o_gla, m_w_out, m_norm_mlp_g, m_w_ff1, m_w_ff2, m_final_norm_g, v_meta_tokens, v_norm_mix_g, v_w_in, v_b_forget, v_w_alpha2, v_b_alpha, v_gla_norm_g, v_w_o_fox, v_w_o_gla, v_w_out, v_norm_mlp_g, v_w_ff1, v_w_ff2, v_final_norm_g):
    d = x.shape[2]
    lay = _Layout(d)
    xi, yi, ci = lax.axis_index("x"), lax.axis_index("y"), lax.axis_index("c")
    chip = (2 * xi + yi).astype(jnp.int32)
    w = dict(w_in=w_in, w_alpha2=w_alpha2, w_o_fox=w_o_fox, w_o_gla=w_o_gla, w_out=w_out, w_ff1=w_ff1, w_ff2=w_ff2)
    m = dict(w_in=m_w_in, w_alpha2=m_w_alpha2, w_o_fox=m_w_o_fox, w_o_gla=m_w_o_gla, w_out=m_w_out, w_ff1=m_w_ff1,
             w_ff2=m_w_ff2)
    v = dict(w_in=v_w_in, w_alpha2=v_w_alpha2, w_o_fox=v_w_o_fox, w_o_gla=v_w_o_gla, w_out=v_w_out, w_ff1=v_w_ff1,
             w_ff2=v_w_ff2)

    own16 = [w[n].astype(BF16) for n in _BIG]
    gathered = _gather_weights(own16)
    gathered = [lax.dynamic_update_slice(g, o[:, None], (0, chip, 0, 0)) for g, o in zip(gathered, own16)]
    meta_all = _all_gather_small(meta_tokens)
    meta_full = meta_all[0::2].transpose(1, 0, 2).reshape(N_META, d)
    layers = []
    for l in range(DEPTH):
        gl = {n: g[l] for n, g in zip(_BIG, gathered)}
        w_alpha_p = jnp.zeros((LANES, lay.gk), BF16).at[FOX_HEADS:FOX_HEADS + GLA_RANK].set(
            _full_matrix("w_alpha2", gl["w_alpha2"]))
        layers.append(dict(
            w_in=lay.to_p(gl["w_in"]), w_alpha_p=w_alpha_p, w_o_fox=gl["w_o_fox"],
            w_o_gla=_full_matrix("w_o_gla", gl["w_o_gla"]), w_out=_full_matrix("w_out", gl["w_out"]),
            w_ff1=gl["w_ff1"], w_ff2=_full_matrix("w_ff2", gl["w_ff2"]),
            norm_mix_g=norm_mix_g[l][None], norm_mlp_g=norm_mlp_g[l][None], gla_norm_g=gla_norm_g[l][None],
            b_alpha=b_alpha[l][None],
            b_forget_p=jnp.pad(b_forget[l][None], ((0, 0), (0, LANES - FOX_HEADS)))))

    loss_part, grad_x, d_meta, grads, dg_final = _sequence_step(x[0], loss_target[0], meta_full, layers,
                                                                final_norm_g[None])
    loss = lax.psum(loss_part[0, 0], ("x", "y", "c"))

    stack = lambda key: jnp.concatenate([grads[l][key] for l in range(DEPTH)], axis=0)
    b_forget_g = jnp.concatenate([grads[l]["b_forget_p"][:, :FOX_HEADS] for l in range(DEPTH)], axis=0)
    packed = _pack_small(d, d_meta, stack("norm_mix_g"), stack("gla_norm_g"), stack("norm_mlp_g"), dg_final,
                         stack("b_alpha"), b_forget_g)
    small_g = _unpack_small(d, _sum_devices(_all_gather_small(packed)))
    small_g["meta"] = lax.dynamic_slice_in_dim(small_g["meta"], chip * (d // N_CHIPS), d // N_CHIPS, axis=1)

    def partial_of(l, n):
        if n in ("w_ff1", "w_o_fox", "w_in"):
            return grads[l][n]
        if n == "w_alpha2":
            return _shard_major(n, grads[l]["w_alpha_p"][FOX_HEADS:FOX_HEADS + GLA_RANK])
        return _shard_major(n, grads[l][n])

    part0 = [partial_of(0, n) for n in _BIG]
    part1 = [partial_of(1, n) for n in _BIG]
    from_sibling = _swap_layers(part0, part1)
    layer_idx = ci.astype(jnp.int32)[None]
    chip_idx = chip[None]
    sums = [_pair_sum("pair_sum_" + n, p0, p1, r_, layer_idx)
            for n, p0, p1, r_ in zip(_BIG, part0, part1, from_sibling)]
    landed = _scatter_chips([s16 for _, s16 in sums])
    totals = [_chip_sum("chip_sum_" + n, s32, r_, chip_idx, layer_idx) for n, (s32, _), r_ in zip(_BIG, sums, landed)]
    big_g = dict(zip(_BIG, _share_layers(totals)))

    out_g, out_d, out_m, out_v = {}, {}, {}, {}
    for n in _BIG:
        out_g[n], out_d[n], out_m[n], out_v[n] = _adamw("adamw_" + n, w[n], big_g[n], m[n], v[n])
    sm_w = dict(meta_tokens=meta_tokens, norm_mix_g=norm_mix_g, b_forget=b_forget, b_alpha=b_alpha,
                gla_norm_g=gla_norm_g, norm_mlp_g=norm_mlp_g, final_norm_g=final_norm_g)
    sm_m = dict(meta_tokens=m_meta_tokens, norm_mix_g=m_norm_mix_g, b_forget=m_b_forget, b_alpha=m_b_alpha,
                gla_norm_g=m_gla_norm_g, norm_mlp_g=m_norm_mlp_g, final_norm_g=m_final_norm_g)
    sm_v = dict(meta_tokens=v_meta_tokens, norm_mix_g=v_norm_mix_g, b_forget=v_b_forget, b_alpha=v_b_alpha,
                gla_norm_g=v_gla_norm_g, norm_mlp_g=v_norm_mlp_g, final_norm_g=v_final_norm_g)
    sm_g = dict(meta_tokens=small_g["meta"], norm_mix_g=small_g["norm_mix_g"], b_forget=small_g["b_forget"],
                b_alpha=small_g["b_alpha"], gla_norm_g=small_g["gla_norm_g"], norm_mlp_g=small_g["norm_mlp_g"],
                final_norm_g=small_g["final_norm_g"])
    names_small = list(sm_w)
    sizes = [sm_w[n].size for n in names_small]
    width = 512
    total = -(-sum(sizes) // (8 * width)) * (8 * width)

    def pack_flat(dct, fill):
        flat = jnp.concatenate([dct[n].reshape(-1) for n in names_small])
        return jnp.pad(flat, (0, total - flat.shape[0]), constant_values=fill).reshape(1, -1, width)

    res = _adamw("adamw_small", pack_flat(sm_w, 0.0), pack_flat(sm_g, 0.0), pack_flat(sm_m, 0.0), pack_flat(sm_v, 1.0))
    offs = [0]
    for sz in sizes:
        offs.append(offs[-1] + sz)
    for i, n in enumerate(names_small):
        out_g[n] = sm_g[n].reshape(sm_w[n].shape)
        out_d[n], out_m[n], out_v[n] = [r.reshape(-1)[offs[i]:offs[i + 1]].reshape(sm_w[n].shape) for r in res[1:]]

    order = ["meta_tokens", "norm_mix_g", "w_in", "b_forget", "w_alpha2", "b_alpha", "gla_norm_g", "w_o_fox",
             "w_o_gla", "w_out", "norm_mlp_g", "w_ff1", "w_ff2", "final_norm_g"]
    return (loss, grad_x[None], *[out_g[n] for n in order], *[out_d[n] for n in order],
            *[out_m[n] for n in order], *[out_v[n] for n in order])
```

```python
import functools

import numpy as np

import jax
import jax.numpy as jnp
from jax import lax
from jax.experimental import pallas as pl
from jax.experimental.pallas import tpu as pltpu

F32 = jnp.float32
BF16 = jnp.bfloat16

N_META = 16
PAD = 112
ROW0 = PAD + N_META
EPS = 1e-6
MASK_VALUE = -1e30
FOX_HEADS = 8
FOX_GROUP = 2
GLA_HEADS = 4
GLA_RANK = 16
GLA_TAU = 16.0
GLA_CHUNK = 64
DEPTH = 2
N_CHIPS = 4
N_DEV = 8

ADAM_LR = 0.001
ADAM_B1 = 0.9
ADAM_B2 = 0.999
ADAM_EPS = 1e-08
ADAM_WD = 0.01
ADAM_STEP = 10

LANES = 128
VMEM_LIMIT = 56 * 1024 * 1024
MESH = pl.DeviceIdType.MESH


def _pick(n, target, mult):
    best = None
    for d in range(mult, min(n, target) + 1, mult):
        if n % d == 0:
            best = d
    return n if best is None else best


def _params(sem=None):
    return pltpu.CompilerParams(dimension_semantics=sem, vmem_limit_bytes=VMEM_LIMIT)


def _bf(v):
    return v if v.dtype == BF16 else v.astype(BF16)


def _sigmoid(z):
    return 1.0 / (1.0 + jnp.exp(-z))


def _log_sigmoid(z):
    return jnp.minimum(z, 0.0) - jnp.log(1.0 + jnp.exp(-jnp.abs(z)))


def _split3(v):
    a = v.astype(BF16)
    r = v - a.astype(F32)
    b = r.astype(BF16)
    c = (r - b.astype(F32)).astype(BF16)
    return a, b, c


def _dot(a, b, dims):
    return lax.dot_general(a, b, (dims, ((), ())), preferred_element_type=F32)


NN = ((1,), (0,))
NT = ((1,), (1,))
TN = ((0,), (0,))


def _tri_dot(tri, v, dims=NN):
    a, b, c = _split3(v)
    return _dot(tri, a, dims) + _dot(tri, b, dims) + _dot(tri, c, dims)


def _mm(name, a, b, *, mode, m, n, k, b_c0=0, extras=(), epilogue=None, out_dtypes=(F32,),
        b_shards=1, out_shards=1, tm=1056, tn=1024, tk=2048):
    tm = _pick(m, tm, LANES if mode == "tn" else 16)
    tn = _pick(n // max(b_shards if mode == "nn" else 1, out_shards), tn, LANES)
    if mode == "tn":
        tk = _pick(k, 2112, 16)
    else:
        tk = _pick(k // (b_shards if mode == "nt" else 1), tk, LANES)
    assert b_c0 % (tk if mode == "nt" else tn) == 0 and (b_shards == 1 or b_c0 == 0)
    nk = k // tk
    if mode == "tn":
        a_spec = pl.BlockSpec((tk, tm), lambda i, j, kk: (kk, i))
    else:
        a_spec = pl.BlockSpec((tm, tk), lambda i, j, kk: (i, kk))
    if mode == "nt":
        dims = NT
        if b_shards > 1:
            per = (k // b_shards) // tk
            b_spec = pl.BlockSpec((None, tn, tk), lambda i, j, kk: (kk // per, j, kk % per))
        else:
            b_spec = pl.BlockSpec((tn, tk), lambda i, j, kk: (j, kk + b_c0 // tk))
    else:
        dims = NN if mode == "nn" else TN
        if b_shards > 1:
            per = (n // b_shards) // tn
            b_spec = pl.BlockSpec((None, tk, tn), lambda i, j, kk: (j // per, kk, j % per))
        else:
            b_spec = pl.BlockSpec((tk, tn), lambda i, j, kk: (kk, j + b_c0 // tn))
    ex_specs = [pl.BlockSpec((tm, tn), lambda i, j, kk: (i, j)) for _ in extras]
    if out_shards > 1:
        oper = (n // out_shards) // tn
        out_specs = [pl.BlockSpec((None, tm, tn), lambda i, j, kk: (j // oper, i, j % oper)) for _ in out_dtypes]
        out_shape = [jax.ShapeDtypeStruct((out_shards, m, n // out_shards), dt) for dt in out_dtypes]
    else:
        out_specs = [pl.BlockSpec((tm, tn), lambda i, j, kk: (i, j)) for _ in out_dtypes]
        out_shape = [jax.ShapeDtypeStruct((m, n), dt) for dt in out_dtypes]
    n_ex = len(extras)
    n_out = len(out_dtypes)

    def finish(acc, ex_refs, out_refs):
        vals = (acc,) if epilogue is None else epilogue(acc, *[r[...] for r in ex_refs])
        for r, v in zip(out_refs, vals):
            r[...] = v.astype(r.dtype)

    def body(a_ref, b_ref, *rest):
        ex_refs = rest[:n_ex]
        out_refs = rest[n_ex:n_ex + n_out]
        prod = _dot(_bf(a_ref[...]), _bf(b_ref[...]), dims)
        if nk == 1:
            finish(prod, ex_refs, out_refs)
            return
        acc_ref = rest[n_ex + n_out]
        kk = pl.program_id(2)

        @pl.when(kk == 0)
        def _():
            acc_ref[...] = prod

        @pl.when((kk > 0) & (kk < nk - 1))
        def _():
            acc_ref[...] += prod

        @pl.when(kk == nk - 1)
        def _():
            finish(acc_ref[...] + prod, ex_refs, out_refs)

    outs = pl.pallas_call(
        body,
        name=name,
        grid=(m // tm, n // tn, nk),
        in_specs=[a_spec, b_spec] + ex_specs,
        out_specs=out_specs,
        out_shape=out_shape,
        scratch_shapes=[pltpu.VMEM((tm, tn), F32)] if nk > 1 else [],
        compiler_params=_params(("parallel", "parallel", "arbitrary")),
    )(a, b, *extras)
    return outs[0] if n_out == 1 else outs


def _ew(name, fn, ins, outs, rows, tm):
    tm = _pick(rows, tm, 16)
    in_specs, args = [], []
    for spec in ins:
        if spec[0] == "tile":
            _, arr, width, c0 = spec
            assert c0 % width == 0
            in_specs.append(pl.BlockSpec((tm, width), functools.partial(lambda i, o: (i, o), o=c0 // width)))
        else:
            arr = spec[1]
            in_specs.append(pl.BlockSpec(arr.shape, lambda i: (0, 0)))
        args.append(arr)
    out_specs, out_shape = [], []
    for kind, dt, width in outs:
        if kind == "tile":
            out_specs.append(pl.BlockSpec((tm, width), lambda i: (i, 0)))
            out_shape.append(jax.ShapeDtypeStruct((rows, width), dt))
        else:
            out_specs.append(pl.BlockSpec((1, width), lambda i: (0, 0)))
            out_shape.append(jax.ShapeDtypeStruct((1, width), dt))
    n_in = len(ins)
    has_acc = any(o[0] == "acc" for o in outs)

    def body(*refs):
        i = pl.program_id(0)
        vals = fn(i * tm, *[r[...] for r in refs[:n_in]])
        for (kind, _, _), r, v in zip(outs, refs[n_in:], vals):
            if kind == "tile":
                r[...] = v.astype(r.dtype)
            else:
                @pl.when(i == 0)
                def _():
                    r[...] = jnp.zeros_like(r)

                r[...] += v.astype(r.dtype)

    res = pl.pallas_call(
        body,
        name=name,
        grid=(rows // tm,),
        in_specs=in_specs,
        out_specs=out_specs,
        out_shape=out_shape,
        compiler_params=_params(("arbitrary",) if has_acc else ("parallel",)),
    )(*args)
    return res[0] if len(outs) == 1 else res


def _row_ids(row0, tm):
    return row0 + lax.broadcasted_iota(jnp.int32, (tm, 1), 0)


def _colsum(v):
    return jnp.sum(v, axis=0, keepdims=True)


def _rms_fwd(name, h, g, t, d):
    def fn(row0, x, gg):
        r = lax.rsqrt(jnp.mean(x * x, axis=-1, keepdims=True) + EPS)
        return (x * r * gg,)

    return _ew(name, fn, [("tile", h, d, 0), ("full", g)], [("tile", BF16, d)], t, 264)


def _rms_bwd(name, h, g, dy, dres, t, d):
    def fn(row0, x, gg, dyv, dr):
        r = lax.rsqrt(jnp.mean(x * x, axis=-1, keepdims=True) + EPS)
        xh = x * r
        dxh = dyv * gg
        dx = r * (dxh - xh * jnp.mean(dxh * xh, axis=-1, keepdims=True))
        out = jnp.where(_row_ids(row0, x.shape[0]) >= PAD, dr + dx, 0.0)
        return out, _colsum(dyv * xh)

    return _ew(name, fn, [("tile", h, d, 0), ("full", g), ("tile", dy, d, 0), ("tile", dres, d, 0)],
               [("tile", F32, d), ("acc", F32, d)], t, 264)


def _loss_head(h, g, target_p, t, d):
    def fn(row0, x, gg, tgt):
        real = _row_ids(row0, x.shape[0]) >= ROW0
        r = lax.rsqrt(jnp.mean(x * x, axis=-1, keepdims=True) + EPS)
        xh = x * r
        err = jnp.where(real, xh * gg - tgt, 0.0)
        loss_rows = 0.5 * jnp.mean(err * err, axis=-1, keepdims=True)
        dyv = err * (1.0 / d)
        dxh = dyv * gg
        dx = r * (dxh - xh * jnp.mean(dxh * xh, axis=-1, keepdims=True))
        loss_part = jnp.sum(loss_rows, axis=0, keepdims=True) * jnp.ones((1, LANES), F32)
        return jnp.where(real, dx, 0.0), _colsum(dyv * xh), loss_part

    return _ew("loss_head", fn, [("tile", h, d, 0), ("full", g), ("tile", target_p, d, 0)],
               [("tile", F32, d), ("acc", F32, d), ("acc", F32, LANES)], t, 264)


def _merge_fwd(a_fox, a_gla, proj, c_gates, t, d):
    def fn(row0, af, ag, gates):
        gates = gates.astype(F32)
        return (_sigmoid(gates[:, :d]) * af + _sigmoid(gates[:, d:]) * ag,)

    return _ew("merge_fwd", fn, [("tile", a_fox, d, 0), ("tile", a_gla, d, 0), ("tile", proj, 2 * d, c_gates)],
               [("tile", BF16, d)], t, 264)


def _merge_bwd(dy, a_fox, a_gla, proj, c_gates, t, d):
    def fn(row0, dyv, af, ag, gates):
        gates = gates.astype(F32)
        sf = _sigmoid(gates[:, :d])
        sg = _sigmoid(gates[:, d:])
        dgates = jnp.concatenate([dyv * af * sf * (1.0 - sf), dyv * ag * sg * (1.0 - sg)], axis=1)
        return dyv * sf, dyv * sg, dgates

    return _ew("merge_bwd", fn,
               [("tile", dy, d, 0), ("tile", a_fox, d, 0), ("tile", a_gla, d, 0), ("tile", proj, 2 * d, c_gates)],
               [("tile", BF16, d), ("tile", BF16, d), ("tile", BF16, 2 * d)], t, 264)


def _fox_gate_fwd(small, b_forget_p, t):
    tb = _pick(t, 384, LANES)

    def body(s_ref, b_ref, c_ref, carry_ref):
        i = pl.program_id(0)

        @pl.when(i == 0)
        def _():
            carry_ref[...] = jnp.zeros_like(carry_ref)

        logf = _log_sigmoid(s_ref[...] + b_ref[...])
        logf = jnp.where(_row_ids(i * tb, tb) >= PAD, logf, 0.0)
        r = lax.broadcasted_iota(jnp.int32, (tb, tb), 0)
        c = lax.broadcasted_iota(jnp.int32, (tb, tb), 1)
        tri = (c <= r).astype(BF16)
        cs = _tri_dot(tri, logf) + carry_ref[...]
        c_ref[...] = cs
        carry_ref[...] = cs[tb - 1:tb, :]

    return pl.pallas_call(
        body, name="fox_gate_fwd", grid=(t // tb,),
        in_specs=[pl.BlockSpec((tb, LANES), lambda i: (i, 0)), pl.BlockSpec((1, LANES), lambda i: (0, 0))],
        out_specs=pl.BlockSpec((tb, LANES), lambda i: (i, 0)),
        out_shape=jax.ShapeDtypeStruct((t, LANES), F32),
        scratch_shapes=[pltpu.VMEM((1, LANES), F32)],
        compiler_params=_params(("arbitrary",)),
    )(small, b_forget_p)


def _fox_gate_bwd(dc, small, b_forget_p, dga, t):
    tb = _pick(t, 384, LANES)
    nb = t // tb

    def body(dc_ref, s_ref, b_ref, dga_ref, ds_ref, db_ref, carry_ref):
        i = pl.program_id(0)

        @pl.when(i == 0)
        def _():
            carry_ref[...] = jnp.zeros_like(carry_ref)
            db_ref[...] = jnp.zeros_like(db_ref)

        r = lax.broadcasted_iota(jnp.int32, (tb, tb), 0)
        c = lax.broadcasted_iota(jnp.int32, (tb, tb), 1)
        tri = (c >= r).astype(BF16)
        dlogf = _tri_dot(tri, dc_ref[...]) + carry_ref[...]
        carry_ref[...] = dlogf[0:1, :]
        z = s_ref[...] + b_ref[...]
        dff = dlogf * _sigmoid(-z)
        lane = lax.broadcasted_iota(jnp.int32, (tb, LANES), 1)
        keep = (_row_ids((nb - 1 - i) * tb, tb) >= PAD) & (lane < FOX_HEADS)
        dff = jnp.where(keep, dff, 0.0)
        ds_ref[...] = dff + dga_ref[...]
        db_ref[...] += _colsum(dff)

    rev = lambda i: (nb - 1 - i, 0)
    return pl.pallas_call(
        body, name="fox_gate_bwd", grid=(nb,),
        in_specs=[pl.BlockSpec((tb, LANES), rev), pl.BlockSpec((tb, LANES), rev),
                  pl.BlockSpec((1, LANES), lambda i: (0, 0)), pl.BlockSpec((tb, LANES), rev)],
        out_specs=[pl.BlockSpec((tb, LANES), rev), pl.BlockSpec((1, LANES), lambda i: (0, 0))],
        out_shape=[jax.ShapeDtypeStruct((t, LANES), F32), jax.ShapeDtypeStruct((1, LANES), F32)],
        scratch_shapes=[pltpu.VMEM((1, LANES), F32)],
        compiler_params=_params(("arbitrary",)),
    )(dc, small, b_forget_p, dga)


def _fox_pairs(nb, by_key):
    if by_key:
        pairs = [(qi, ki) for ki in range(nb) for qi in range(ki, nb)]
    else:
        pairs = [(qi, ki) for qi in range(nb) for ki in range(qi + 1)]
    return (jnp.asarray(np.array([p[0] for p in pairs], np.int32)),
            jnp.asarray(np.array([p[1] for p in pairs], np.int32)), len(pairs))


def _fox_specs(tb, fd, c_fq, c_fkv):
    gw = FOX_GROUP * fd
    q0, kv0 = c_fq // gw, c_fkv // (2 * gw)
    return dict(
        q=pl.BlockSpec((tb, gw), lambda g, p, qt, kt: (qt[p], q0 + g)),
        kv=pl.BlockSpec((tb, 2 * gw), lambda g, p, qt, kt: (kt[p], kv0 + g)),
        col=pl.BlockSpec((FOX_GROUP, tb, 1), lambda g, p, qt, kt: (g, qt[p], 0)),
        row=pl.BlockSpec((FOX_GROUP, 1, tb), lambda g, p, qt, kt: (g, 0, kt[p])),
        head=pl.BlockSpec((tb, gw), lambda g, p, qt, kt: (qt[p], g)),
        key_kv=pl.BlockSpec((tb, 2 * gw), lambda g, p, qt, kt: (kt[p], g)),
    )


def _fox_mask(qi, ki, tb):
    row = qi * tb + lax.broadcasted_iota(jnp.int32, (tb, tb), 0)
    col = ki * tb + lax.broadcasted_iota(jnp.int32, (tb, tb), 1)
    return (col <= row) & (col >= PAD)


def _fox_heads(q_ref, kv_ref, fd):
    return [(q_ref[:, hh * fd:(hh + 1) * fd], kv_ref[:, 2 * hh * fd:(2 * hh + 1) * fd],
             kv_ref[:, (2 * hh + 1) * fd:(2 * hh + 2) * fd]) for hh in range(FOX_GROUP)]


def _fox_fwd(proj, c_col, c_row, t, fd, c_fq, c_fkv):
    tb = _pick(t, 384, LANES)
    nb = t // tb
    scale = fd ** -0.5
    sp = _fox_specs(tb, fd, c_fq, c_fkv)
    qt, kt, npairs = _fox_pairs(nb, by_key=False)

    def body(qt_ref, kt_ref, q_ref, kv_ref, cq_ref, ck_ref, o_ref, lse_ref, m_ref, l_ref, acc_ref):
        p = pl.program_id(1)
        qi, ki = qt_ref[p], kt_ref[p]

        @pl.when(ki == 0)
        def _():
            m_ref[...] = jnp.full_like(m_ref, -jnp.inf)
            l_ref[...] = jnp.zeros_like(l_ref)
            acc_ref[...] = jnp.zeros_like(acc_ref)

        def update(masked):
            mask = _fox_mask(qi, ki, tb) if masked else None
            for hh, (q, k, v) in enumerate(_fox_heads(q_ref, kv_ref, fd)):
                s = _dot(q, k, NT) * scale + cq_ref[hh] - ck_ref[hh]
                if masked:
                    s = jnp.where(mask, s, MASK_VALUE)
                m_prev = m_ref[hh]
                m_new = jnp.maximum(m_prev, jnp.max(s, axis=-1, keepdims=True))
                alpha = jnp.exp(m_prev - m_new)
                pe = jnp.exp(s - m_new)
                l_ref[hh] = alpha * l_ref[hh] + jnp.sum(pe, axis=-1, keepdims=True)
                acc_ref[hh] = alpha * acc_ref[hh] + _dot(pe.astype(BF16), v, NN)
                m_ref[hh] = m_new

        edge = (ki == 0) | (ki == qi)
        pl.when(edge)(functools.partial(update, True))
        pl.when(jnp.logical_not(edge))(functools.partial(update, False))

        @pl.when(ki == qi)
        def _():
            real = _row_ids(qi * tb, tb) >= PAD
            for hh in range(FOX_GROUP):
                o_ref[:, hh * fd:(hh + 1) * fd] = jnp.where(real, acc_ref[hh] / l_ref[hh], 0.0)
                lse_ref[hh] = m_ref[hh] + jnp.log(l_ref[hh])

    return pl.pallas_call(
        body, name="fox_fwd",
        grid_spec=pltpu.PrefetchScalarGridSpec(
            num_scalar_prefetch=2, grid=(FOX_HEADS // FOX_GROUP, npairs),
            in_specs=[sp["q"], sp["kv"], sp["col"], sp["row"]],
            out_specs=[sp["head"], sp["col"]],
            scratch_shapes=[pltpu.VMEM((FOX_GROUP, tb, 1), F32), pltpu.VMEM((FOX_GROUP, tb, 1), F32),
                            pltpu.VMEM((FOX_GROUP, tb, fd), F32)]),
        out_shape=[jax.ShapeDtypeStruct((t, FOX_HEADS * fd), F32), jax.ShapeDtypeStruct((FOX_HEADS, t, 1), F32)],
        compiler_params=_params(("parallel", "arbitrary")),
    )(qt, kt, proj, proj, c_col, c_row)


def _fox_delta(o_fox, do_fox, t, fd):
    tb = _pick(t, 384, LANES)

    def body(o_ref, do_ref, out_ref):
        for h in range(FOX_HEADS):
            sl = slice(h * fd, (h + 1) * fd)
            out_ref[h] = jnp.sum(o_ref[:, sl] * do_ref[:, sl].astype(BF16).astype(F32), axis=-1, keepdims=True)

    w = FOX_HEADS * fd
    return pl.pallas_call(
        body, name="fox_delta", grid=(t // tb,),
        in_specs=[pl.BlockSpec((tb, w), lambda i: (i, 0)), pl.BlockSpec((tb, w), lambda i: (i, 0))],
        out_specs=pl.BlockSpec((FOX_HEADS, tb, 1), lambda i: (0, i, 0)),
        out_shape=jax.ShapeDtypeStruct((FOX_HEADS, t, 1), F32),
        compiler_params=_params(("parallel",)),
    )(o_fox, do_fox)


def _fox_bwd(proj, c_col, c_row, lse, delta, do_fox, t, fd, c_fq, c_fkv):
    tb = _pick(t, 384, LANES)
    nb = t // tb
    scale = fd ** -0.5
    sp = _fox_specs(tb, fd, c_fq, c_fkv)
    qt, kt, npairs = _fox_pairs(nb, by_key=True)
    gw = FOX_GROUP * fd

    def body(qt_ref, kt_ref, q_ref, kv_ref, cq_ref, ck_ref, lse_ref, dl_ref, do_ref, dq_ref, dkv_ref, dc_ref, dr_ref,
             dq_acc, dk_acc, dv_acc, dc_acc, dr_acc):
        p = pl.program_id(1)
        qi, ki = qt_ref[p], kt_ref[p]

        @pl.when(p == 0)
        def _():
            dq_acc[...] = jnp.zeros_like(dq_acc)
            dr_acc[...] = jnp.zeros_like(dr_acc)

        @pl.when(qi == ki)
        def _():
            dk_acc[...] = jnp.zeros_like(dk_acc)
            dv_acc[...] = jnp.zeros_like(dv_acc)
            dc_acc[...] = jnp.zeros_like(dc_acc)

        rows = pl.ds(pl.multiple_of(qi * tb, LANES), tb)

        def update(masked):
            mask = _fox_mask(qi, ki, tb) if masked else None
            for hh, (q, k, v) in enumerate(_fox_heads(q_ref, kv_ref, fd)):
                do = _bf(do_ref[:, hh * fd:(hh + 1) * fd])
                s = _dot(q, k, NT) * scale + cq_ref[hh] - ck_ref[hh]
                if masked:
                    s = jnp.where(mask, s, MASK_VALUE)
                pr = jnp.exp(s - lse_ref[hh])
                dp = _dot(do, v, NT)
                ds = pr * (dp - dl_ref[hh])
                ds16 = ds.astype(BF16)
                dv_acc[hh] += _dot(pr.astype(BF16), do, TN)
                dk_acc[hh] += _dot(ds16, q, TN)
                dc_acc[hh] += _colsum(ds)
                dr_acc[hh, rows, :] += jnp.sum(ds, axis=-1, keepdims=True)
                dq_acc[hh, rows, :] += _dot(ds16, k, NN)

        edge = (ki == 0) | (ki == qi)
        pl.when(edge)(functools.partial(update, True))
        pl.when(jnp.logical_not(edge))(functools.partial(update, False))

        @pl.when(qi == nb - 1)
        def _():
            for hh in range(FOX_GROUP):
                dkv_ref[:, 2 * hh * fd:(2 * hh + 1) * fd] = (dk_acc[hh] * scale).astype(dkv_ref.dtype)
                dkv_ref[:, (2 * hh + 1) * fd:(2 * hh + 2) * fd] = dv_acc[hh].astype(dkv_ref.dtype)
                dc_ref[hh] = -dc_acc[hh]

        @pl.when(p == npairs - 1)
        def _():
            for hh in range(FOX_GROUP):
                dq_ref[:, hh * fd:(hh + 1) * fd] = (dq_acc[hh] * scale).astype(dq_ref.dtype)
            dr_ref[...] = dr_acc[...]

    return pl.pallas_call(
        body, name="fox_bwd",
        grid_spec=pltpu.PrefetchScalarGridSpec(
            num_scalar_prefetch=2, grid=(FOX_HEADS // FOX_GROUP, npairs),
            in_specs=[sp["q"], sp["kv"], sp["col"], sp["row"], sp["col"], sp["col"], sp["head"]],
            out_specs=[pl.BlockSpec((t, gw), lambda g, p, qt, kt: (0, g)), sp["key_kv"], sp["row"],
                       pl.BlockSpec((FOX_GROUP, t, 1), lambda g, p, qt, kt: (g, 0, 0))],
            scratch_shapes=[pltpu.VMEM((FOX_GROUP, t, fd), F32), pltpu.VMEM((FOX_GROUP, tb, fd), F32),
                            pltpu.VMEM((FOX_GROUP, tb, fd), F32), pltpu.VMEM((FOX_GROUP, 1, tb), F32),
                            pltpu.VMEM((FOX_GROUP, t, 1), F32)]),
        out_shape=[jax.ShapeDtypeStruct((t, FOX_HEADS * fd), BF16), jax.ShapeDtypeStruct((t, 2 * FOX_HEADS * fd), BF16),
                   jax.ShapeDtypeStruct((FOX_HEADS, 1, t), F32), jax.ShapeDtypeStruct((FOX_HEADS, t, 1), F32)],
        compiler_params=_params(("parallel", "arbitrary")),
    )(qt, kt, proj, proj, c_col, c_row, lse, delta, do_fox)


def _gla_gate_fwd(small, w_alpha_p, b_alpha, t, gk):
    def fn(row0, s, w, b):
        z = _dot(s.astype(BF16), w, NN) + b
        return (jnp.where(_row_ids(row0, s.shape[0]) >= PAD, _log_sigmoid(z) * (1.0 / GLA_TAU), 0.0),)

    return _ew("gla_gate_fwd", fn, [("tile", small, LANES, 0), ("full", w_alpha_p), ("full", b_alpha)],
               [("tile", F32, gk)], t, 264)


def _gla_gate_bwd(dglog, small, w_alpha_p, b_alpha, t, gk):
    def fn(row0, dg, s, w, b):
        z = _dot(s.astype(BF16), w, NN) + b
        dz = jnp.where(_row_ids(row0, s.shape[0]) >= PAD, dg * (1.0 / GLA_TAU) * _sigmoid(-z), 0.0)
        return dz, _colsum(dz)

    return _ew("gla_gate_bwd", fn,
               [("tile", dglog, gk, 0), ("tile", small, LANES, 0), ("full", w_alpha_p), ("full", b_alpha)],
               [("tile", BF16, gk), ("acc", F32, gk)], t, 264)


def _gla_chunk(q, k, g, scale, cs):
    r = lax.broadcasted_iota(jnp.int32, (cs, cs), 0)
    c = lax.broadcasted_iota(jnp.int32, (cs, cs), 1)
    causal = c <= r
    b = _tri_dot(causal.astype(BF16), g)
    bl = b[cs - 1:cs, :]
    eb, einv, eend = jnp.exp(b), jnp.exp(-b), jnp.exp(bl - b)
    qd = q.astype(F32) * scale * eb
    kf = k.astype(F32)
    return causal, (eb, einv, eend), bl, qd, kf * einv, kf * eend


def _gla_fwd(proj, glog, t, dk, dv, c_q, c_k, c_v):
    cs = GLA_CHUNK
    nc = t // cs
    wk, wv = GLA_HEADS * dk, GLA_HEADS * dv
    scale = dk ** -0.5

    def body(q_ref, k_ref, v_ref, g_ref, o_ref, sp_ref, st_ref):
        @pl.when(pl.program_id(0) == 0)
        def _():
            st_ref[...] = jnp.zeros_like(st_ref)

        for h in range(GLA_HEADS):
            ks, vs = slice(h * dk, (h + 1) * dk), slice(h * dv, (h + 1) * dv)
            v = v_ref[:, vs]
            causal, _, bl, qd, ki, ke = _gla_chunk(q_ref[:, ks], k_ref[:, ks], g_ref[:, ks], scale, cs)
            st = st_ref[h]
            sp_ref[h] = st
            a = jnp.where(causal, _dot(qd.astype(BF16), ki.astype(BF16), NT), 0.0)
            o_ref[:, vs] = _dot(a.astype(BF16), v, NN) + _dot(qd.astype(BF16), st.astype(BF16), NT)
            st_ref[h] = st * jnp.exp(bl) + _dot(v, ke.astype(BF16), TN)

    return pl.pallas_call(
        body, name="gla_fwd", grid=(nc,),
        in_specs=[pl.BlockSpec((cs, wk), lambda n: (n, c_q // wk)), pl.BlockSpec((cs, wk), lambda n: (n, c_k // wk)),
                  pl.BlockSpec((cs, wv), lambda n: (n, c_v // wv)), pl.BlockSpec((cs, wk), lambda n: (n, 0))],
        out_specs=[pl.BlockSpec((cs, wv), lambda n: (n, 0)),
                   pl.BlockSpec((None, GLA_HEADS, dv, dk), lambda n: (n, 0, 0, 0))],
        out_shape=[jax.ShapeDtypeStruct((t, wv), F32), jax.ShapeDtypeStruct((nc, GLA_HEADS, dv, dk), F32)],
        scratch_shapes=[pltpu.VMEM((GLA_HEADS, dv, dk), F32)],
        compiler_params=_params(("arbitrary",)),
    )(proj, proj, proj, glog)


def _gla_bwd(proj, glog, s_prev, do_raw, t, dk, dv, c_q, c_k, c_v):
    cs = GLA_CHUNK
    nc = t // cs
    wk, wv = GLA_HEADS * dk, GLA_HEADS * dv
    scale = dk ** -0.5

    def body(q_ref, k_ref, v_ref, g_ref, sp_ref, do_ref, dq_ref, dk_ref, dv_ref, dg_ref, dst_ref):
        @pl.when(pl.program_id(0) == 0)
        def _():
            dst_ref[...] = jnp.zeros_like(dst_ref)

        for h in range(GLA_HEADS):
            ks, vs = slice(h * dk, (h + 1) * dk), slice(h * dv, (h + 1) * dv)
            v = v_ref[:, vs]
            do = do_ref[:, vs].astype(BF16)
            causal, (eb, einv, eend), bl, qd, ki, ke = _gla_chunk(q_ref[:, ks], k_ref[:, ks], g_ref[:, ks], scale, cs)
            qd16, ki16, ke16 = qd.astype(BF16), ki.astype(BF16), ke.astype(BF16)
            st = sp_ref[h]
            dst = dst_ref[h]
            dst16 = dst.astype(BF16)
            a = jnp.where(causal, _dot(qd16, ki16, NT), 0.0).astype(BF16)
            da = jnp.where(causal, _dot(do, v, NT), 0.0).astype(BF16)
            dvv = _dot(a, do, TN) + _dot(ke16, dst16, NT)
            dqd = _dot(da, ki16, NN) + _dot(do, st.astype(BF16), NN)
            dki = _dot(da, qd16, TN)
            dke = _dot(v, dst16, NN)
            dl = jnp.exp(bl)
            ddl = _colsum(dst * st)
            dst_ref[h] = dst * dl + _dot(do, qd16, TN)
            dq_ref[:, ks] = (dqd * eb * scale).astype(dq_ref.dtype)
            dk_ref[:, ks] = (dki * einv + dke * eend).astype(dk_ref.dtype)
            dv_ref[:, vs] = dvv.astype(dv_ref.dtype)
            db = dqd * qd - dki * ki - dke * ke
            db_last = _colsum(dke * ke) + ddl * dl
            r = lax.broadcasted_iota(jnp.int32, (cs, cs), 0)
            c = lax.broadcasted_iota(jnp.int32, (cs, cs), 1)
            dg_ref[:, ks] = _tri_dot((c >= r).astype(BF16), db) + db_last

    rev = lambda f: (lambda n: f(nc - 1 - n))
    return pl.pallas_call(
        body, name="gla_bwd", grid=(nc,),
        in_specs=[pl.BlockSpec((cs, wk), rev(lambda n: (n, c_q // wk))), pl.BlockSpec((cs, wk), rev(lambda n: (n, c_k // wk))),
                  pl.BlockSpec((cs, wv), rev(lambda n: (n, c_v // wv))), pl.BlockSpec((cs, wk), rev(lambda n: (n, 0))),
                  pl.BlockSpec((None, GLA_HEADS, dv, dk), rev(lambda n: (n, 0, 0, 0))),
                  pl.BlockSpec((cs, wv), rev(lambda n: (n, 0)))],
        out_specs=[pl.BlockSpec((cs, wk), rev(lambda n: (n, 0))), pl.BlockSpec((cs, wk), rev(lambda n: (n, 0))),
                   pl.BlockSpec((cs, wv), rev(lambda n: (n, 0))), pl.BlockSpec((cs, wk), rev(lambda n: (n, 0)))],
        out_shape=[jax.ShapeDtypeStruct((t, wk), BF16), jax.ShapeDtypeStruct((t, wk), BF16),
                   jax.ShapeDtypeStruct((t, wv), BF16), jax.ShapeDtypeStruct((t, wk), F32)],
        scratch_shapes=[pltpu.VMEM((GLA_HEADS, dv, dk), F32)],
        compiler_params=_params(("arbitrary",)),
    )(proj, proj, proj, glog, s_prev, do_raw)


def _gla_post_fwd(o_raw, proj, gn, t, dv, c_gr):
    w = GLA_HEADS * dv

    def fn(row0, o, gr, g):
        gr = gr.astype(F32)
        outs = []
        for h in range(GLA_HEADS):
            oh = o[:, h * dv:(h + 1) * dv]
            outs.append(oh * lax.rsqrt(jnp.mean(oh * oh, axis=-1, keepdims=True) + EPS))
        on = jnp.concatenate(outs, axis=1) * g
        return (on * (gr * _sigmoid(gr)),)

    return _ew("gla_post_fwd", fn, [("tile", o_raw, w, 0), ("tile", proj, w, c_gr), ("full", gn)],
               [("tile", BF16, w)], t, 264)


def _gla_post_bwd(o_raw, proj, gn, do_gla, t, dv, c_gr):
    w = GLA_HEADS * dv

    def fn(row0, o, gr, g, do):
        gr = gr.astype(F32)
        sg = _sigmoid(gr)
        don = do * (gr * sg)
        ohs, dos = [], []
        for h in range(GLA_HEADS):
            sl = slice(h * dv, (h + 1) * dv)
            oh = o[:, sl]
            r = lax.rsqrt(jnp.mean(oh * oh, axis=-1, keepdims=True) + EPS)
            xh = oh * r
            dxh = don[:, sl] * g[:, sl]
            ohs.append(xh)
            dos.append(r * (dxh - xh * jnp.mean(dxh * xh, axis=-1, keepdims=True)))
        xh = jnp.concatenate(ohs, axis=1)
        dgr = do * (xh * g) * (sg * (1.0 + gr * (1.0 - sg)))
        return jnp.concatenate(dos, axis=1), dgr, _colsum(don * xh)

    return _ew("gla_post_bwd", fn,
               [("tile", o_raw, w, 0), ("tile", proj, w, c_gr), ("full", gn), ("tile", do_gla, w, 0)],
               [("tile", F32, w), ("tile", BF16, w), ("acc", F32, w)], t, 264)


def _adamw(name, w, g, m, v):
    nl, rows, cols = w.shape
    tm = _pick(rows, max(8, (512 * 1024) // max(cols, 1) // 8 * 8), 8)

    def body(w_ref, g_ref, m_ref, v_ref, go_ref, d_ref, nm_ref, nv_ref):
        gg = g_ref[...]
        nm = ADAM_B1 * m_ref[...] + (1.0 - ADAM_B1) * gg
        nv = ADAM_B2 * v_ref[...] + (1.0 - ADAM_B2) * (gg * gg)
        m_hat = nm / (1.0 - ADAM_B1 ** ADAM_STEP)
        v_hat = nv / (1.0 - ADAM_B2 ** ADAM_STEP)
        go_ref[...] = gg
        d_ref[...] = -ADAM_LR * (m_hat / (jnp.sqrt(v_hat) + ADAM_EPS) + ADAM_WD * w_ref[...])
        nm_ref[...] = nm
        nv_ref[...] = nv

    spec = pl.BlockSpec((None, tm, cols), lambda l, i: (l, i, 0))
    return pl.pallas_call(
        body, name=name, grid=(nl, rows // tm), in_specs=[spec] * 4, out_specs=[spec] * 4,
        out_shape=[jax.ShapeDtypeStruct((nl, rows, cols), F32)] * 4,
        compiler_params=_params(("parallel", "parallel")),
    )(w, g, m, v)


def _me():
    return lax.axis_index("x"), lax.axis_index("y"), lax.axis_index("c")


def _hbm_specs(n):
    return [pl.BlockSpec(memory_space=pl.ANY)] * n


_HBM = pl.BlockSpec(memory_space=pltpu.HBM)
_SEM = pl.BlockSpec(memory_space=pltpu.SEMAPHORE)
_EFFECT = pltpu.SideEffectType.DATAFLOW_SIDE_EFFECTING
N_PEERS = N_CHIPS - 1


def _other_chips(x, y):
    return [(1 - x, y), (x, 1 - y), (1 - x, 1 - y)]


def _ici_copies(plan, src, land, send_sems, recv_sems):
    x, y, c = _me()
    copies = []
    for i in range(len(src)):
        for j, chip in enumerate(_other_chips(x, y)):
            s, d = plan(i, src[i], land[i], j, chip, (x, y, c))
            copies.append(pltpu.make_async_remote_copy(
                src_ref=s, dst_ref=d, send_sem=send_sems.at[N_PEERS * i + j], recv_sem=recv_sems.at[N_PEERS * i + j],
                device_id=(*chip, c), device_id_type=MESH))
    return copies


def _ici_start(name, srcs, lands, plan, after=None):
    n = len(srcs)
    extra = [] if after is None else [after]

    def body(*refs):
        src, land = refs[:n], refs[n:2 * n]
        send_sems, recv_sems = refs[2 * n + len(extra)], refs[2 * n + len(extra) + 1]
        token = refs[-1]
        for cp in _ici_copies(plan, src, land, send_sems, recv_sems):
            cp.start()
        token[...] = jnp.zeros_like(token)

    out_shape = ([pltpu.SemaphoreType.DMA((N_PEERS * n,)), pltpu.SemaphoreType.DMA((N_PEERS * n,))]
                 + [pltpu.HBM(a.shape, a.dtype) for a in list(srcs) + list(lands)]
                 + [jax.ShapeDtypeStruct((8, LANES), F32)])
    res = pl.pallas_call(
        body, name=name, out_shape=out_shape,
        in_specs=[_HBM] * (2 * n) + [pl.BlockSpec(memory_space=pl.ANY)] * len(extra),
        out_specs=[_SEM, _SEM] + [_HBM] * (2 * n) + [pl.BlockSpec(memory_space=pltpu.VMEM)],
        input_output_aliases={i: 2 + i for i in range(2 * n)},
        compiler_params=pltpu.CompilerParams(has_side_effects=_EFFECT),
    )(*[pltpu.with_memory_space_constraint(a, pltpu.HBM) for a in list(srcs) + list(lands)], *extra)
    return res[0], res[1], res[2:2 + n], res[2 + n:2 + 2 * n], res[-1]


def _ici_wait(name, send_sems, recv_sems, srcs, lands, plan, after):
    n = len(srcs)

    def body(*refs):
        src, land = refs[:n], refs[n:2 * n]
        s_sems, r_sems = refs[2 * n], refs[2 * n + 1]
        for cp in _ici_copies(plan, src, land, s_sems, r_sems):
            cp.wait_send()
            cp.wait_recv()

    res = pl.pallas_call(
        body, name=name, out_shape=[pltpu.HBM(a.shape, a.dtype) for a in list(srcs) + list(lands)],
        in_specs=[_HBM] * (2 * n) + [_SEM, _SEM, pl.BlockSpec(memory_space=pl.ANY)], out_specs=[_HBM] * (2 * n),
        input_output_aliases={i: i for i in range(2 * n)},
        compiler_params=pltpu.CompilerParams(has_side_effects=_EFFECT),
    )(*srcs, *lands, send_sems, recv_sems, after)
    return res[:n], res[n:]


def _half(ref_rows, c):
    half = ref_rows // 2
    return pl.ds(c * half, half)


def _gather_plan(i, src, land, j, chip, me):
    x, y, c = me
    rows = _half(src.shape[0], c)
    return src.at[rows], land.at[2 * x + y, rows]


def _scatter_plan(i, src, land, j, chip, me):
    x, y, _ = me
    return src.at[2 * chip[0] + chip[1]], land.at[2 * x + y]


def _forward_halves(name, gathered):
    n = len(gathered)

    def body(*refs):
        buf = refs[n:2 * n]
        send_sems, recv_sems = refs[2 * n:]
        x, y, c = _me()
        copies = []
        for i in range(n):
            rows = _half(buf[i].shape[1], c)
            for j, (px, py) in enumerate(_other_chips(x, y)):
                cp = pltpu.make_async_remote_copy(
                    src_ref=buf[i].at[2 * px + py, rows], dst_ref=buf[i].at[2 * px + py, rows],
                    send_sem=send_sems.at[N_PEERS * i + j], recv_sem=recv_sems.at[N_PEERS * i + j],
                    device_id=(x, y, 1 - c), device_id_type=MESH)
                cp.start()
                copies.append(cp)
        for cp in copies:
            cp.wait_recv()
        for cp in copies:
            cp.wait_send()

    return pl.pallas_call(
        body, name=name, in_specs=_hbm_specs(n), out_specs=_hbm_specs(n),
        out_shape=[jax.ShapeDtypeStruct(g.shape, g.dtype) for g in gathered],
        input_output_aliases={i: i for i in range(n)},
        scratch_shapes=[pltpu.SemaphoreType.DMA((N_PEERS * n,)), pltpu.SemaphoreType.DMA((N_PEERS * n,))],
    )(*gathered)


def _all_gather_small(v):
    def body(v_ref, out_ref, send_sems, recv_sems, local_sem):
        x, y, c = _me()
        mine = pltpu.make_async_copy(v_ref, out_ref.at[4 * x + 2 * y + c], local_sem)
        mine.start()
        copies = []
        for k in range(1, N_DEV):
            peer = (x ^ ((k >> 2) & 1), y ^ ((k >> 1) & 1), c ^ (k & 1))
            cp = pltpu.make_async_remote_copy(
                src_ref=v_ref, dst_ref=out_ref.at[4 * x + 2 * y + c], send_sem=send_sems.at[k - 1],
                recv_sem=recv_sems.at[k - 1], device_id=peer, device_id_type=MESH)
            cp.start()
            copies.append(cp)
        for cp in copies:
            cp.wait_recv()
        for cp in copies:
            cp.wait_send()
        mine.wait()

    return pl.pallas_call(
        body, name="all_gather_small", in_specs=_hbm_specs(1), out_specs=pl.BlockSpec(memory_space=pl.ANY),
        out_shape=jax.ShapeDtypeStruct((N_DEV,) + v.shape, v.dtype),
        scratch_shapes=[pltpu.SemaphoreType.DMA((N_DEV - 1,)), pltpu.SemaphoreType.DMA((N_DEV - 1,)),
                        pltpu.SemaphoreType.DMA],
    )(v)


def _swap_halves(name, parts):
    n = len(parts)

    def body(*refs):
        src, dst = refs[:n], refs[n:2 * n]
        send_sems, recv_sems = refs[2 * n:]
        x, y, c = _me()
        copies = []
        for i in range(n):
            cp = pltpu.make_async_remote_copy(
                src_ref=src[i].at[:, _half(src[i].shape[1], 1 - c)], dst_ref=dst[i], send_sem=send_sems.at[i],
                recv_sem=recv_sems.at[i], device_id=(x, y, 1 - c), device_id_type=MESH)
            cp.start()
            copies.append(cp)
        for cp in copies:
            cp.wait_recv()
        for cp in copies:
            cp.wait_send()

    out_shape = [jax.ShapeDtypeStruct((g.shape[0], g.shape[1] // 2, g.shape[2]), g.dtype) for g in parts]
    return pl.pallas_call(
        body, name=name, in_specs=_hbm_specs(n), out_specs=_hbm_specs(n), out_shape=out_shape,
        scratch_shapes=[pltpu.SemaphoreType.DMA((n,)), pltpu.SemaphoreType.DMA((n,))],
    )(*parts)


def _share_halves(grads):
    n = len(grads)

    def body(*refs):
        buf = refs[n:2 * n]
        send_sems, recv_sems = refs[2 * n:]
        x, y, c = _me()
        copies = []
        for i in range(n):
            rows = _half(buf[i].shape[1], c)
            cp = pltpu.make_async_remote_copy(
                src_ref=buf[i].at[:, rows], dst_ref=buf[i].at[:, rows], send_sem=send_sems.at[i],
                recv_sem=recv_sems.at[i], device_id=(x, y, 1 - c), device_id_type=MESH)
            cp.start()
            copies.append(cp)
        for cp in copies:
            cp.wait_recv()
        for cp in copies:
            cp.wait_send()

    out_shape = [jax.ShapeDtypeStruct(s.shape, s.dtype) for s in grads]
    return pl.pallas_call(
        body, name="share_halves", in_specs=_hbm_specs(n), out_specs=_hbm_specs(n), out_shape=out_shape,
        input_output_aliases={i: i for i in range(n)},
        scratch_shapes=[pltpu.SemaphoreType.DMA((n,)), pltpu.SemaphoreType.DMA((n,))],
    )(*grads)


def _pair_sum(name, own, recv, core):
    nch, half, cdim = recv.shape
    tm = _pick(half, max(8, (512 * 1024) // cdim // 16 * 16), 16)
    nt = half // tm

    def body(c_ref, a_ref, b_ref, s_ref, s16_ref):
        s = a_ref[...] + b_ref[...]
        s_ref[...] = s
        s16_ref[...] = s.astype(BF16)

    spec = pl.BlockSpec((None, tm, cdim), lambda j, i, c: (j, i, 0))
    return pl.pallas_call(
        body, name=name,
        grid_spec=pltpu.PrefetchScalarGridSpec(
            num_scalar_prefetch=1, grid=(nch, nt),
            in_specs=[pl.BlockSpec((None, tm, cdim), lambda j, i, c: (j, c[0] * nt + i, 0)), spec],
            out_specs=[spec, spec]),
        out_shape=[jax.ShapeDtypeStruct((nch, half, cdim), F32), jax.ShapeDtypeStruct((nch, half, cdim), BF16)],
        compiler_params=_params(("parallel", "parallel")),
    )(core, own, recv)


def _chip_sum(name, own, recv, chip, core, layer, into=None):
    nch, half, cdim = own.shape
    tm = _pick(half, max(8, (512 * 1024) // cdim // 16 * 16), 16)
    nt = half // tm

    def body(c_ref, k_ref, own_ref, *rest):
        recv_refs, out_ref = rest[:nch], rest[-1]
        me = c_ref[0]
        acc = None
        for j in range(nch):
            term = jnp.where(me == j, own_ref[...], recv_refs[j][...].astype(F32))
            acc = term if acc is None else acc + term
        out_ref[...] = acc

    recv_specs = [pl.BlockSpec((None, tm, cdim), functools.partial(lambda i, c, k, j: (j, i, 0), j=j))
                  for j in range(nch)]
    in_specs = [pl.BlockSpec((None, tm, cdim), lambda i, c, k: (c[0], i, 0))] + recv_specs
    args = [chip, core, own] + [recv] * nch
    aliases = {}
    if into is not None:
        in_specs.append(pl.BlockSpec(memory_space=pl.ANY))
        args.append(into)
        aliases = {len(args) - 1: 0}
    return pl.pallas_call(
        body, name=name,
        grid_spec=pltpu.PrefetchScalarGridSpec(
            num_scalar_prefetch=2, grid=(nt,), in_specs=in_specs,
            out_specs=pl.BlockSpec((None, tm, cdim), lambda i, c, k: (layer, k[0] * nt + i, 0))),
        out_shape=jax.ShapeDtypeStruct((DEPTH, 2 * half, cdim), F32),
        input_output_aliases=aliases,
        compiler_params=_params(("parallel",)),
    )(*args)


def _sum_devices(gathered):
    _, r, cdim = gathered.shape

    def body(g_ref, o_ref):
        acc = g_ref[0]
        for k in range(1, N_DEV):
            acc = acc + g_ref[k]
        o_ref[...] = acc

    return pl.pallas_call(
        body, name="sum_devices", out_shape=jax.ShapeDtypeStruct((r, cdim), F32),
        compiler_params=_params(),
    )(gathered)


class _Layout:
    def __init__(self, d):
        self.d = d
        self.fw = d // 2
        self.fd = self.fw // FOX_HEADS
        self.gk = d // 2
        self.gv = d
        self.dk = self.gk // GLA_HEADS
        self.dv = self.gv // GLA_HEADS
        self.c_fq = 0
        self.c_gq = self.fw
        self.c_gv = self.c_gq + self.gk
        self.c_gr = self.c_gv + self.gv
        self.c_fkv = self.c_gr + self.gv
        self.c_gates = self.c_fkv + 2 * self.fw
        self.c_gk = self.c_gates + 2 * d
        self.c_small = self.c_gk + self.gk
        self.n_main = self.c_small
        self.n_p = self.c_small + LANES
        self.o_fk = self.fw
        self.o_fv = 2 * self.fw
        self.o_ff = 3 * self.fw
        self.o_gq = self.o_ff + FOX_HEADS
        self.o_gk = self.o_gq + self.gk
        self.o_gv = self.o_gk + self.gk
        self.o_gr = self.o_gv + self.gv
        self.o_ga = self.o_gr + self.gv
        self.o_gf = self.o_ga + GLA_RANK
        self.o_gg = self.o_gf + d
        self.n_orig = self.o_gg + d

    def to_p(self, shards):
        per = self.n_orig // N_CHIPS
        ranges = [(0, self.fw), (self.o_gq, self.gk), (self.o_gv, self.gv), (self.o_gr, self.gv)]
        for h in range(FOX_HEADS):
            ranges += [(self.o_fk + h * self.fd, self.fd), (self.o_fv + h * self.fd, self.fd)]
        ranges += [(self.o_gf, 2 * self.d), (self.o_gk, self.gk), (self.o_ff, FOX_HEADS), (self.o_ga, GLA_RANK)]
        pieces = []
        for a, width in ranges:
            for j in range(a // per, (a + width - 1) // per + 1):
                lo, hi = max(a, j * per), min(a + width, (j + 1) * per)
                pieces.append(shards[j][:, lo - j * per:hi - j * per])
        pieces.append(jnp.zeros((shards.shape[1], LANES - FOX_HEADS - GLA_RANK), shards.dtype))
        return jnp.concatenate(pieces, axis=1)

    def from_segments(self, seg):
        per = self.n_orig // N_CHIPS
        fd = self.fd
        atoms = [("fq", 0, self.fw)]
        atoms += [("fkv", 2 * h * fd, fd) for h in range(FOX_HEADS)]
        atoms += [("fkv", (2 * h + 1) * fd, fd) for h in range(FOX_HEADS)]
        atoms += [("small", 0, FOX_HEADS), ("gq", 0, self.gk), ("gk", 0, self.gk), ("gv", 0, self.gv),
                  ("gr", 0, self.gv), ("small", FOX_HEADS, GLA_RANK), ("gates", 0, 2 * self.d)]
        shards = [[] for _ in range(N_CHIPS)]
        pos = 0
        for name, c0, width in atoms:
            for j in range(pos // per, (pos + width - 1) // per + 1):
                lo, hi = max(pos, j * per), min(pos + width, (j + 1) * per)
                shards[j].append(seg[name][:, c0 + lo - pos:c0 + hi - pos])
            pos += width
        assert pos == self.n_orig
        return jnp.stack([jnp.concatenate(s, axis=1) for s in shards])


def _layer_fwd(lay, h, p, t):
    d = lay.d
    xn = _rms_fwd("rms_mix_fwd", h, p["norm_mix_g"], t, d)
    proj = _mm("mm_proj", xn, p["w_in"], mode="nn", m=t, n=lay.n_main, k=d, out_dtypes=(BF16,))
    small = _mm("mm_small", xn, p["w_in"], mode="nn", m=t, n=LANES, k=d, b_c0=lay.c_small)
    cs = _fox_gate_fwd(small, p["b_forget_p"], t)
    ct = cs[:, :FOX_HEADS].T
    c_col, c_row = ct[:, :, None], ct[:, None, :]
    o_fox, lse = _fox_fwd(proj, c_col, c_row, t, lay.fd, lay.c_fq, lay.c_fkv)
    glog = _gla_gate_fwd(small, p["w_alpha_p"], p["b_alpha"], t, lay.gk)
    o_raw, s_prev = _gla_fwd(proj, glog, t, lay.dk, lay.dv, lay.c_gq, lay.c_gk, lay.c_gv)
    o_gla = _gla_post_fwd(o_raw, proj, p["gla_norm_g"], t, lay.dv, lay.c_gr)
    a_fox = _mm("mm_o_fox", o_fox, p["w_o_fox"], mode="nn", m=t, n=d, k=lay.fw, b_shards=N_CHIPS)
    a_gla = _mm("mm_o_gla", o_gla, p["w_o_gla"], mode="nn", m=t, n=d, k=lay.gv)
    y = _merge_fwd(a_fox, a_gla, proj, lay.c_gates, t, d)
    h1 = _mm("mm_out", y, p["w_out"], mode="nn", m=t, n=d, k=d, extras=[h], epilogue=lambda acc, res: (res + acc,))
    xn2 = _rms_fwd("rms_mlp_fwd", h1, p["norm_mlp_g"], t, d)
    u, act = _mm("mm_ff1", xn2, p["w_ff1"], mode="nn", m=t, n=4 * d, k=d, out_dtypes=(BF16, BF16), b_shards=N_CHIPS,
                 epilogue=lambda acc: (acc, jnp.square(jnp.maximum(acc, 0.0))))
    h2 = _mm("mm_ff2", act, p["w_ff2"], mode="nn", m=t, n=d, k=4 * d, extras=[h1],
             epilogue=lambda acc, res: (res + acc,))
    saved = dict(h=h, xn=xn, proj=proj, small=small, c_col=c_col, c_row=c_row, o_fox=o_fox, lse=lse, glog=glog,
                 o_raw=o_raw, s_prev=s_prev, o_gla=o_gla, a_fox=a_fox, a_gla=a_gla, y=y, h1=h1, xn2=xn2, u=u, act=act)
    return h2, saved


def _layer_bwd(lay, dh2, p, s, t, gate=None):
    d = lay.d
    g = {}
    du = _mm("mm_dact", dh2, p["w_ff2"], mode="nt", m=t, n=4 * d, k=d, extras=[s["u"]], out_dtypes=(BF16,),
             epilogue=lambda acc, u: (acc * (2.0 * jnp.maximum(u.astype(F32), 0.0)),))
    g["w_ff2"] = _mm("mm_dw_ff2", s["act"], dh2, mode="tn", m=4 * d, n=d, k=t)
    g["w_ff1"] = _mm("mm_dw_ff1", s["xn2"], du, mode="tn", m=d, n=4 * d, k=t, out_shards=N_CHIPS)
    dxn2 = _mm("mm_dxn2", du, p["w_ff1"], mode="nt", m=t, n=d, k=4 * d, b_shards=N_CHIPS)
    dh1, g["norm_mlp_g"] = _rms_bwd("rms_mlp_bwd", s["h1"], p["norm_mlp_g"], dxn2, dh2, t, d)
    dy = _mm("mm_dy", dh1, p["w_out"], mode="nt", m=t, n=d, k=d)
    g["w_out"] = _mm("mm_dw_out", s["y"], dh1, mode="tn", m=d, n=d, k=t)
    da_fox, da_gla, dgates = _merge_bwd(dy, s["a_fox"], s["a_gla"], s["proj"], lay.c_gates, t, d)
    g["w_o_fox"] = _mm("mm_dw_o_fox", s["o_fox"], da_fox, mode="tn", m=lay.fw, n=d, k=t, out_shards=N_CHIPS)
    do_fox = _mm("mm_do_fox", da_fox, p["w_o_fox"], mode="nt", m=t, n=lay.fw, k=d, b_shards=N_CHIPS)
    g["w_o_gla"] = _mm("mm_dw_o_gla", s["o_gla"], da_gla, mode="tn", m=lay.gv, n=d, k=t)
    do_gla = _mm("mm_do_gla", da_gla, p["w_o_gla"], mode="nt", m=t, n=lay.gv, k=d)
    do_raw, dgr, g["gla_norm_g"] = _gla_post_bwd(s["o_raw"], s["proj"], p["gla_norm_g"], do_gla, t, lay.dv, lay.c_gr)
    dgq, dgk, dgv, dglog = _gla_bwd(s["proj"], s["glog"], s["s_prev"], do_raw, t, lay.dk, lay.dv,
                                    lay.c_gq, lay.c_gk, lay.c_gv)
    dz, g["b_alpha"] = _gla_gate_bwd(dglog, s["small"], p["w_alpha_p"], p["b_alpha"], t, lay.gk)
    g["w_alpha_p"] = _mm("mm_dw_alpha", s["small"], dz, mode="tn", m=LANES, n=lay.gk, k=t)
    dga = _mm("mm_dga", dz, p["w_alpha_p"], mode="nt", m=t, n=LANES, k=lay.gk)
    delta = _fox_delta(s["o_fox"], do_fox, t, lay.fd)
    dfq, dfkv, dc, dr = _fox_bwd(s["proj"], s["c_col"], s["c_row"], s["lse"], delta, do_fox, t, lay.fd,
                                 lay.c_fq, lay.c_fkv)
    dc_p = jnp.pad((dc[:, 0, :] + dr[:, :, 0]).T, ((0, 0), (0, LANES - FOX_HEADS)))
    dsmall, g["b_forget_p"] = _fox_gate_bwd(dc_p, s["small"], p["b_forget_p"], dga, t)
    segs = [("fq", dfq, lay.c_fq), ("gq", dgq, lay.c_gq), ("gv", dgv, lay.c_gv), ("gr", dgr, lay.c_gr),
            ("fkv", dfkv, lay.c_fkv), ("gates", dgates, lay.c_gates), ("gk", dgk, lay.c_gk),
            ("small", dsmall, lay.c_small)]
    dxn = None
    dw_in = {}
    for nm, dseg, c0 in segs:
        width = dseg.shape[1]
        dw_in[nm] = _mm("mm_dw_in_" + nm, s["xn"], dseg, mode="tn", m=d, n=width, k=t)
        if dxn is None:
            dxn = _mm("mm_dxn_" + nm, dseg, p["w_in"], mode="nt", m=t, n=d, k=width, b_c0=c0)
        else:
            dxn = _mm("mm_dxn_" + nm, dseg, p["w_in"], mode="nt", m=t, n=d, k=width, b_c0=c0, extras=[dxn],
                      epilogue=lambda acc, prev: (prev + acc,))
    g["w_in"] = lay.from_segments(dw_in)
    gain = p["norm_mix_g"] if gate is None else p["norm_mix_g"] + gate(g)
    dh, g["norm_mix_g"] = _rms_bwd("rms_mix_bwd", s["h"], gain, dxn, dh1, t, d)
    return dh, g


def _sequence_step(x, target, meta, layers, final_g):
    seq, d = x.shape
    t = seq + ROW0
    lay = _Layout(d)
    h = jnp.pad(x, ((ROW0, 0), (0, 0))).at[PAD:ROW0].set(meta)
    target_p = jnp.pad(target, ((ROW0, 0), (0, 0)))
    saved = []
    for p in layers:
        h, s = _layer_fwd(lay, h, p, t)
        saved.append(s)
    dh, dg_final, loss_part = _loss_head(h, final_g, target_p, t, d)
    grads = [None] * len(layers)
    for l in reversed(range(len(layers))):
        dh, grads[l] = _layer_bwd(lay, dh, layers[l], saved[l], t)
    return loss_part, dh[ROW0:], dh[PAD:ROW0], grads, dg_final


_SMALL_ROWS = 48


def _pack_small(d, meta, mix, gla, mlp, final, b_alpha, b_forget, w_alpha2):
    rows = [meta.reshape(N_META, d), mix.reshape(DEPTH, d), gla.reshape(DEPTH, d), mlp.reshape(DEPTH, d),
            final.reshape(1, d), b_alpha.reshape(1, d),
            jnp.pad(b_forget.reshape(1, DEPTH * FOX_HEADS), ((0, 0), (0, d - DEPTH * FOX_HEADS))),
            jnp.zeros((7, d), F32), w_alpha2.reshape(GLA_RANK, d)]
    return jnp.concatenate(rows, axis=0)


def _unpack_small(d, packed):
    return dict(meta=packed[:N_META], norm_mix_g=packed[16:18], gla_norm_g=packed[18:20], norm_mlp_g=packed[20:22],
                final_norm_g=packed[22], b_alpha=packed[23].reshape(DEPTH, d // 2),
                b_forget=packed[24, :DEPTH * FOX_HEADS].reshape(DEPTH, FOX_HEADS),
                w_alpha2=packed[32:48].reshape(DEPTH, GLA_RANK, d // 2))


_BIG = ("w_in", "w_o_fox", "w_o_gla", "w_out", "w_ff1", "w_ff2")
_COL_SHARDED = ("w_in", "w_o_fox", "w_ff1")


def _full_matrix(name, gathered_l):
    nch, r, c = gathered_l.shape
    if name in _COL_SHARDED:
        return gathered_l.transpose(1, 0, 2).reshape(r, nch * c)
    return gathered_l.reshape(nch * r, c)


def _shard_major(name, full):
    r, c = full.shape
    if name in _COL_SHARDED:
        return full.reshape(r, N_CHIPS, c // N_CHIPS).transpose(1, 0, 2)
    return full.reshape(N_CHIPS, r // N_CHIPS, c)


def kernel(x, meta_tokens, norm_mix_g, w_in, b_forget, w_alpha2, b_alpha, gla_norm_g, w_o_fox, w_o_gla, w_out, norm_mlp_g, w_ff1, w_ff2, final_norm_g, loss_target, m_meta_tokens, m_norm_mix_g, m_w_in, m_b_forget, m_w_alpha2, m_b_alpha, m_gla_norm_g, m_w_o_fox, m_w_o_gla, m_w_out, m_norm_mlp_g, m_w_ff1, m_w_ff2, m_final_norm_g, v_meta_tokens, v_norm_mix_g, v_w_in, v_b_forget, v_w_alpha2, v_b_alpha, v_gla_norm_g, v_w_o_fox, v_w_o_gla, v_w_out, v_norm_mlp_g, v_w_ff1, v_w_ff2, v_final_norm_g):
    d = x.shape[2]
    lay = _Layout(d)
    xi, yi, ci = lax.axis_index("x"), lax.axis_index("y"), lax.axis_index("c")
    chip = (2 * xi + yi).astype(jnp.int32)
    w = dict(w_in=w_in, w_alpha2=w_alpha2, w_o_fox=w_o_fox, w_o_gla=w_o_gla, w_out=w_out, w_ff1=w_ff1, w_ff2=w_ff2)
    m = dict(w_in=m_w_in, w_alpha2=m_w_alpha2, w_o_fox=m_w_o_fox, w_o_gla=m_w_o_gla, w_out=m_w_out, w_ff1=m_w_ff1,
             w_ff2=m_w_ff2)
    v = dict(w_in=v_w_in, w_alpha2=v_w_alpha2, w_o_fox=v_w_o_fox, w_o_gla=v_w_o_gla, w_out=v_w_out, w_ff1=v_w_ff1,
             w_ff2=v_w_ff2)

    seq = x.shape[1]
    t = seq + ROW0
    core_idx = ci.astype(jnp.int32)[None]
    chip_idx = chip[None]

    started, token = [], None
    for l in range(DEPTH):
        own16 = [w[n][l].astype(BF16) for n in _BIG]
        lands = [lax.empty((N_CHIPS,) + o.shape, BF16) for o in own16]
        started.append(_ici_start("gather_start_%d" % l, own16, lands, _gather_plan, after=token))
        token = started[l][4]
    cols = d // N_CHIPS
    small_w = jnp.concatenate([meta_tokens + token[0, 0], w_alpha2.reshape(-1, cols)], axis=0)
    small_all = _all_gather_small(small_w)[0::2]
    meta_full = small_all[:, :N_META].transpose(1, 0, 2).reshape(N_META, d)
    alpha_full = small_all[:, N_META:].reshape(N_CHIPS, DEPTH, GLA_RANK, lay.gk // N_CHIPS)
    alpha_full = alpha_full.transpose(1, 2, 0, 3).reshape(DEPTH, GLA_RANK, lay.gk)

    def layer_weights(l, after):
        send_sems, recv_sems, srcs, lands, _ = started[l]
        srcs, lands = _ici_wait("gather_wait_%d" % l, send_sems, recv_sems, srcs, lands, _gather_plan, after)
        lands = _forward_halves("gather_forward_%d" % l, lands)
        gl = {n: lax.dynamic_update_slice(g, o[None], (chip, 0, 0)) for n, g, o in zip(_BIG, lands, srcs)}
        w_alpha_p = jnp.zeros((LANES, lay.gk), BF16).at[FOX_HEADS:FOX_HEADS + GLA_RANK].set(
            alpha_full[l].astype(BF16))
        return dict(
            w_in=lay.to_p(gl["w_in"]), w_alpha_p=w_alpha_p, w_o_fox=gl["w_o_fox"],
            w_o_gla=_full_matrix("w_o_gla", gl["w_o_gla"]), w_out=_full_matrix("w_out", gl["w_out"]),
            w_ff1=gl["w_ff1"], w_ff2=_full_matrix("w_ff2", gl["w_ff2"]),
            norm_mix_g=norm_mix_g[l][None], norm_mlp_g=norm_mlp_g[l][None], gla_norm_g=gla_norm_g[l][None],
            b_alpha=b_alpha[l][None],
            b_forget_p=jnp.pad(b_forget[l][None], ((0, 0), (0, LANES - FOX_HEADS))))

    h = jnp.pad(x[0], ((ROW0, 0), (0, 0))).at[PAD:ROW0].set(meta_full)
    layers, saved = [], []
    for l in range(DEPTH):
        layers.append(layer_weights(l, after=h))
        h, s = _layer_fwd(lay, h, layers[l], t)
        saved.append(s)
    dh, dg_final, loss_part = _loss_head(h, final_norm_g[None], jnp.pad(loss_target[0], ((ROW0, 0), (0, 0))), t, d)
    loss = lax.psum(loss_part[0, 0], ("x", "y", "c"))

    def partial_of(g, n):
        return g[n] if n in ("w_ff1", "w_o_fox", "w_in") else _shard_major(n, g[n])

    scattered = [None] * DEPTH

    def start_scatter(l, g):
        parts = [partial_of(g, n) for n in _BIG]
        from_sibling = _swap_halves("swap_halves_%d" % l, parts)
        sums = [_pair_sum("pair_sum_%d_%s" % (l, n), p_, r_, core_idx) for n, p_, r_ in zip(_BIG, parts, from_sibling)]
        lands = [lax.empty(s16.shape, BF16) for _, s16 in sums]
        send_sems, recv_sems, srcs, lands, token = _ici_start("scatter_start_%d" % l, [s16 for _, s16 in sums], lands,
                                                              _scatter_plan)
        scattered[l] = (send_sems, recv_sems, srcs, lands, [s32 for s32, _ in sums])
        return token[0, 0]

    grads = [None] * DEPTH
    for l in reversed(range(DEPTH)):
        dh, grads[l] = _layer_bwd(lay, dh, layers[l], saved[l], t, gate=functools.partial(start_scatter, l))
    grad_x, d_meta = dh[ROW0:], dh[PAD:ROW0]

    stack = lambda key: jnp.concatenate([grads[l][key] for l in range(DEPTH)], axis=0)
    b_forget_g = jnp.concatenate([grads[l]["b_forget_p"][:, :FOX_HEADS] for l in range(DEPTH)], axis=0)
    alpha_g = jnp.stack([grads[l]["w_alpha_p"][FOX_HEADS:FOX_HEADS + GLA_RANK] for l in range(DEPTH)])
    packed = _pack_small(d, d_meta, stack("norm_mix_g"), stack("gla_norm_g"), stack("norm_mlp_g"), dg_final,
                         stack("b_alpha"), b_forget_g, alpha_g)
    small_g = _unpack_small(d, _sum_devices(_all_gather_small(packed)))
    small_g["meta"] = lax.dynamic_slice_in_dim(small_g["meta"], chip * (d // N_CHIPS), d // N_CHIPS, axis=1)
    alpha_shard = lax.dynamic_slice_in_dim(small_g["w_alpha2"], chip * (lay.gk // N_CHIPS), lay.gk // N_CHIPS, axis=2)

    big_g = None
    after = small_g["final_norm_g"]
    for l in reversed(range(DEPTH)):
        send_sems, recv_sems, srcs, lands, sums32 = scattered[l]
        _, lands = _ici_wait("scatter_wait_%d" % l, send_sems, recv_sems, srcs, lands, _scatter_plan, after)
        big_g = [_chip_sum("chip_sum_%d_%s" % (l, n), s32, r_, chip_idx, core_idx, l,
                           into=None if big_g is None else big_g[i])
                 for i, (n, s32, r_) in enumerate(zip(_BIG, sums32, lands))]
        after = big_g[0]
    big_g = dict(zip(_BIG, _share_halves(big_g)))
    big_g["w_alpha2"] = alpha_shard

    out_g, out_d, out_m, out_v = {}, {}, {}, {}
    for n in _BIG + ("w_alpha2",):
        out_g[n], out_d[n], out_m[n], out_v[n] = _adamw("adamw_" + n, w[n], big_g[n], m[n], v[n])
    sm_w = dict(meta_tokens=meta_tokens, norm_mix_g=norm_mix_g, b_forget=b_forget, b_alpha=b_alpha,
                gla_norm_g=gla_norm_g, norm_mlp_g=norm_mlp_g, final_norm_g=final_norm_g)
    sm_m = dict(meta_tokens=m_meta_tokens, norm_mix_g=m_norm_mix_g, b_forget=m_b_forget, b_alpha=m_b_alpha,
                gla_norm_g=m_gla_norm_g, norm_mlp_g=m_norm_mlp_g, final_norm_g=m_final_norm_g)
    sm_v = dict(meta_tokens=v_meta_tokens, norm_mix_g=v_norm_mix_g, b_forget=v_b_forget, b_alpha=v_b_alpha,
                gla_norm_g=v_gla_norm_g, norm_mlp_g=v_norm_mlp_g, final_norm_g=v_final_norm_g)
    sm_g = dict(meta_tokens=small_g["meta"], norm_mix_g=small_g["norm_mix_g"], b_forget=small_g["b_forget"],
                b_alpha=small_g["b_alpha"], gla_norm_g=small_g["gla_norm_g"], norm_mlp_g=small_g["norm_mlp_g"],
                final_norm_g=small_g["final_norm_g"])
    names_small = list(sm_w)
    sizes = [sm_w[n].size for n in names_small]
    width = 512
    total = -(-sum(sizes) // (8 * width)) * (8 * width)

    def pack_flat(dct, fill):
        flat = jnp.concatenate([dct[n].reshape(-1) for n in names_small])
        return jnp.pad(flat, (0, total - flat.shape[0]), constant_values=fill).reshape(1, -1, width)

    res = _adamw("adamw_small", pack_flat(sm_w, 0.0), pack_flat(sm_g, 0.0), pack_flat(sm_m, 0.0), pack_flat(sm_v, 1.0))
    offs = [0]
    for sz in sizes:
        offs.append(offs[-1] + sz)
    for i, n in enumerate(names_small):
        out_g[n] = sm_g[n].reshape(sm_w[n].shape)
        out_d[n], out_m[n], out_v[n] = [r.reshape(-1)[offs[i]:offs[i + 1]].reshape(sm_w[n].shape) for r in res[1:]]

    order = ["meta_tokens", "norm_mix_g", "w_in", "b_forget", "w_alpha2", "b_alpha", "gla_norm_g", "w_o_fox",
             "w_o_gla", "w_out", "norm_mlp_g", "w_ff1", "w_ff2", "final_norm_g"]
    return (loss, grad_x[None], *[out_g[n] for n in order], *[out_d[n] for n in order],
            *[out_m[n] for n in order], *[out_v[n] for n in order])
```

```python
import functools

import numpy as np

import jax
import jax.numpy as jnp
from jax import lax
from jax.experimental import pallas as pl
from jax.experimental.pallas import tpu as pltpu

F32 = jnp.float32
BF16 = jnp.bfloat16

N_META = 16
PAD = 112
ROW0 = PAD + N_META
EPS = 1e-6
MASK_VALUE = -1e30
FOX_HEADS = 8
FOX_GROUP = 2
GLA_HEADS = 4
GLA_RANK = 16
GLA_TAU = 16.0
GLA_CHUNK = 64
DEPTH = 2
N_CHIPS = 4
N_DEV = 8

ADAM_LR = 0.001
ADAM_B1 = 0.9
ADAM_B2 = 0.999
ADAM_EPS = 1e-08
ADAM_WD = 0.01
ADAM_STEP = 10

LANES = 128
VMEM_LIMIT = 56 * 1024 * 1024
MESH = pl.DeviceIdType.MESH


def _pick(n, target, mult):
    best = None
    for d in range(mult, min(n, target) + 1, mult):
        if n % d == 0:
            best = d
    return n if best is None else best


def _params(sem=None):
    return pltpu.CompilerParams(dimension_semantics=sem, vmem_limit_bytes=VMEM_LIMIT)


def _bf(v):
    return v if v.dtype == BF16 else v.astype(BF16)


def _sigmoid(z):
    return 1.0 / (1.0 + jnp.exp(-z))


def _log_sigmoid(z):
    return jnp.minimum(z, 0.0) - jnp.log(1.0 + jnp.exp(-jnp.abs(z)))


def _split3(v):
    a = v.astype(BF16)
    r = v - a.astype(F32)
    b = r.astype(BF16)
    c = (r - b.astype(F32)).astype(BF16)
    return a, b, c


def _dot(a, b, dims):
    return lax.dot_general(a, b, (dims, ((), ())), preferred_element_type=F32)


NN = ((1,), (0,))
NT = ((1,), (1,))
TN = ((0,), (0,))


def _tri_dot(tri, v, dims=NN):
    a, b, c = _split3(v)
    return _dot(tri, a, dims) + _dot(tri, b, dims) + _dot(tri, c, dims)


def _mm(name, a, b, *, mode, m, n, k, b_c0=0, extras=(), epilogue=None, out_dtypes=(F32,),
        b_shards=1, out_shards=1, tm=1056, tn=1024, tk=2048):
    tm = _pick(m, tm, LANES if mode == "tn" else 16)
    tn = _pick(n // max(b_shards if mode == "nn" else 1, out_shards), tn, LANES)
    if mode == "tn":
        tk = _pick(k, 2112, 16)
    else:
        tk = _pick(k // (b_shards if mode == "nt" else 1), tk, LANES)
    assert b_c0 % (tk if mode == "nt" else tn) == 0 and (b_shards == 1 or b_c0 == 0)
    nk = k // tk
    if mode == "tn":
        a_spec = pl.BlockSpec((tk, tm), lambda i, j, kk: (kk, i))
    else:
        a_spec = pl.BlockSpec((tm, tk), lambda i, j, kk: (i, kk))
    if mode == "nt":
        dims = NT
        if b_shards > 1:
            per = (k // b_shards) // tk
            b_spec = pl.BlockSpec((None, tn, tk), lambda i, j, kk: (kk // per, j, kk % per))
        else:
            b_spec = pl.BlockSpec((tn, tk), lambda i, j, kk: (j, kk + b_c0 // tk))
    else:
        dims = NN if mode == "nn" else TN
        if b_shards > 1:
            per = (n // b_shards) // tn
            b_spec = pl.BlockSpec((None, tk, tn), lambda i, j, kk: (j // per, kk, j % per))
        else:
            b_spec = pl.BlockSpec((tk, tn), lambda i, j, kk: (kk, j + b_c0 // tn))
    ex_specs = [pl.BlockSpec((tm, tn), lambda i, j, kk: (i, j)) for _ in extras]
    if out_shards > 1:
        oper = (n // out_shards) // tn
        out_specs = [pl.BlockSpec((None, tm, tn), lambda i, j, kk: (j // oper, i, j % oper)) for _ in out_dtypes]
        out_shape = [jax.ShapeDtypeStruct((out_shards, m, n // out_shards), dt) for dt in out_dtypes]
    else:
        out_specs = [pl.BlockSpec((tm, tn), lambda i, j, kk: (i, j)) for _ in out_dtypes]
        out_shape = [jax.ShapeDtypeStruct((m, n), dt) for dt in out_dtypes]
    n_ex = len(extras)
    n_out = len(out_dtypes)

    def finish(acc, ex_refs, out_refs):
        vals = (acc,) if epilogue is None else epilogue(acc, *[r[...] for r in ex_refs])
        for r, v in zip(out_refs, vals):
            r[...] = v.astype(r.dtype)

    def body(a_ref, b_ref, *rest):
        ex_refs = rest[:n_ex]
        out_refs = rest[n_ex:n_ex + n_out]
        prod = _dot(_bf(a_ref[...]), _bf(b_ref[...]), dims)
        if nk == 1:
            finish(prod, ex_refs, out_refs)
            return
        acc_ref = rest[n_ex + n_out]
        kk = pl.program_id(2)

        @pl.when(kk == 0)
        def _():
            acc_ref[...] = prod

        @pl.when((kk > 0) & (kk < nk - 1))
        def _():
            acc_ref[...] += prod

        @pl.when(kk == nk - 1)
        def _():
            finish(acc_ref[...] + prod, ex_refs, out_refs)

    outs = pl.pallas_call(
        body,
        name=name,
        grid=(m // tm, n // tn, nk),
        in_specs=[a_spec, b_spec] + ex_specs,
        out_specs=out_specs,
        out_shape=out_shape,
        scratch_shapes=[pltpu.VMEM((tm, tn), F32)] if nk > 1 else [],
        compiler_params=_params(("parallel", "parallel", "arbitrary")),
    )(a, b, *extras)
    return outs[0] if n_out == 1 else outs


def _ew(name, fn, ins, outs, rows, tm):
    tm = _pick(rows, tm, 16)
    in_specs, args = [], []
    for spec in ins:
        if spec[0] == "tile":
            _, arr, width, c0 = spec
            assert c0 % width == 0
            in_specs.append(pl.BlockSpec((tm, width), functools.partial(lambda i, o: (i, o), o=c0 // width)))
        else:
            arr = spec[1]
            in_specs.append(pl.BlockSpec(arr.shape, lambda i: (0, 0)))
        args.append(arr)
    out_specs, out_shape = [], []
    for kind, dt, width in outs:
        if kind == "tile":
            out_specs.append(pl.BlockSpec((tm, width), lambda i: (i, 0)))
            out_shape.append(jax.ShapeDtypeStruct((rows, width), dt))
        else:
            out_specs.append(pl.BlockSpec((1, width), lambda i: (0, 0)))
            out_shape.append(jax.ShapeDtypeStruct((1, width), dt))
    n_in = len(ins)
    has_acc = any(o[0] == "acc" for o in outs)

    def body(*refs):
        i = pl.program_id(0)
        vals = fn(i * tm, *[r[...] for r in refs[:n_in]])
        for (kind, _, _), r, v in zip(outs, refs[n_in:], vals):
            if kind == "tile":
                r[...] = v.astype(r.dtype)
            else:
                @pl.when(i == 0)
                def _():
                    r[...] = jnp.zeros_like(r)

                r[...] += v.astype(r.dtype)

    res = pl.pallas_call(
        body,
        name=name,
        grid=(rows // tm,),
        in_specs=in_specs,
        out_specs=out_specs,
        out_shape=out_shape,
        compiler_params=_params(("arbitrary",) if has_acc else ("parallel",)),
    )(*args)
    return res[0] if len(outs) == 1 else res


def _row_ids(row0, tm):
    return row0 + lax.broadcasted_iota(jnp.int32, (tm, 1), 0)


def _colsum(v):
    return jnp.sum(v, axis=0, keepdims=True)


def _rms_fwd(name, h, g, t, d):
    def fn(row0, x, gg):
        r = lax.rsqrt(jnp.mean(x * x, axis=-1, keepdims=True) + EPS)
        return (x * r * gg,)

    return _ew(name, fn, [("tile", h, d, 0), ("full", g)], [("tile", BF16, d)], t, 264)


def _rms_bwd(name, h, g, dy, dres, t, d):
    def fn(row0, x, gg, dyv, dr):
        r = lax.rsqrt(jnp.mean(x * x, axis=-1, keepdims=True) + EPS)
        xh = x * r
        dxh = dyv * gg
        dx = r * (dxh - xh * jnp.mean(dxh * xh, axis=-1, keepdims=True))
        out = jnp.where(_row_ids(row0, x.shape[0]) >= PAD, dr + dx, 0.0)
        return out, _colsum(dyv * xh)

    return _ew(name, fn, [("tile", h, d, 0), ("full", g), ("tile", dy, d, 0), ("tile", dres, d, 0)],
               [("tile", F32, d), ("acc", F32, d)], t, 264)


def _loss_head(h, g, target_p, t, d):
    def fn(row0, x, gg, tgt):
        real = _row_ids(row0, x.shape[0]) >= ROW0
        r = lax.rsqrt(jnp.mean(x * x, axis=-1, keepdims=True) + EPS)
        xh = x * r
        err = jnp.where(real, xh * gg - tgt, 0.0)
        loss_rows = 0.5 * jnp.mean(err * err, axis=-1, keepdims=True)
        dyv = err * (1.0 / d)
        dxh = dyv * gg
        dx = r * (dxh - xh * jnp.mean(dxh * xh, axis=-1, keepdims=True))
        loss_part = jnp.sum(loss_rows, axis=0, keepdims=True) * jnp.ones((1, LANES), F32)
        return jnp.where(real, dx, 0.0), _colsum(dyv * xh), loss_part

    return _ew("loss_head", fn, [("tile", h, d, 0), ("full", g), ("tile", target_p, d, 0)],
               [("tile", F32, d), ("acc", F32, d), ("acc", F32, LANES)], t, 264)


def _merge_fwd(a_fox, a_gla, proj, c_gates, t, d):
    def fn(row0, af, ag, gates):
        gates = gates.astype(F32)
        return (_sigmoid(gates[:, :d]) * af + _sigmoid(gates[:, d:]) * ag,)

    return _ew("merge_fwd", fn, [("tile", a_fox, d, 0), ("tile", a_gla, d, 0), ("tile", proj, 2 * d, c_gates)],
               [("tile", BF16, d)], t, 264)


def _merge_bwd(dy, a_fox, a_gla, proj, c_gates, t, d):
    def fn(row0, dyv, af, ag, gates):
        gates = gates.astype(F32)
        sf = _sigmoid(gates[:, :d])
        sg = _sigmoid(gates[:, d:])
        dgates = jnp.concatenate([dyv * af * sf * (1.0 - sf), dyv * ag * sg * (1.0 - sg)], axis=1)
        return dyv * sf, dyv * sg, dgates

    return _ew("merge_bwd", fn,
               [("tile", dy, d, 0), ("tile", a_fox, d, 0), ("tile", a_gla, d, 0), ("tile", proj, 2 * d, c_gates)],
               [("tile", BF16, d), ("tile", BF16, d), ("tile", BF16, 2 * d)], t, 264)


def _fox_gate_fwd(small, b_forget_p, t):
    tb = _pick(t, 384, LANES)

    def body(s_ref, b_ref, c_ref, carry_ref):
        i = pl.program_id(0)

        @pl.when(i == 0)
        def _():
            carry_ref[...] = jnp.zeros_like(carry_ref)

        logf = _log_sigmoid(s_ref[...] + b_ref[...])
        logf = jnp.where(_row_ids(i * tb, tb) >= PAD, logf, 0.0)
        r = lax.broadcasted_iota(jnp.int32, (tb, tb), 0)
        c = lax.broadcasted_iota(jnp.int32, (tb, tb), 1)
        tri = (c <= r).astype(BF16)
        cs = _tri_dot(tri, logf) + carry_ref[...]
        c_ref[...] = cs
        carry_ref[...] = cs[tb - 1:tb, :]

    return pl.pallas_call(
        body, name="fox_gate_fwd", grid=(t // tb,),
        in_specs=[pl.BlockSpec((tb, LANES), lambda i: (i, 0)), pl.BlockSpec((1, LANES), lambda i: (0, 0))],
        out_specs=pl.BlockSpec((tb, LANES), lambda i: (i, 0)),
        out_shape=jax.ShapeDtypeStruct((t, LANES), F32),
        scratch_shapes=[pltpu.VMEM((1, LANES), F32)],
        compiler_params=_params(("arbitrary",)),
    )(small, b_forget_p)


def _fox_gate_bwd(dc, small, b_forget_p, dga, t):
    tb = _pick(t, 384, LANES)
    nb = t // tb

    def body(dc_ref, s_ref, b_ref, dga_ref, ds_ref, db_ref, carry_ref):
        i = pl.program_id(0)

        @pl.when(i == 0)
        def _():
            carry_ref[...] = jnp.zeros_like(carry_ref)
            db_ref[...] = jnp.zeros_like(db_ref)

        r = lax.broadcasted_iota(jnp.int32, (tb, tb), 0)
        c = lax.broadcasted_iota(jnp.int32, (tb, tb), 1)
        tri = (c >= r).astype(BF16)
        dlogf = _tri_dot(tri, dc_ref[...]) + carry_ref[...]
        carry_ref[...] = dlogf[0:1, :]
        z = s_ref[...] + b_ref[...]
        dff = dlogf * _sigmoid(-z)
        lane = lax.broadcasted_iota(jnp.int32, (tb, LANES), 1)
        keep = (_row_ids((nb - 1 - i) * tb, tb) >= PAD) & (lane < FOX_HEADS)
        dff = jnp.where(keep, dff, 0.0)
        ds_ref[...] = dff + dga_ref[...]
        db_ref[...] += _colsum(dff)

    rev = lambda i: (nb - 1 - i, 0)
    return pl.pallas_call(
        body, name="fox_gate_bwd", grid=(nb,),
        in_specs=[pl.BlockSpec((tb, LANES), rev), pl.BlockSpec((tb, LANES), rev),
                  pl.BlockSpec((1, LANES), lambda i: (0, 0)), pl.BlockSpec((tb, LANES), rev)],
        out_specs=[pl.BlockSpec((tb, LANES), rev), pl.BlockSpec((1, LANES), lambda i: (0, 0))],
        out_shape=[jax.ShapeDtypeStruct((t, LANES), F32), jax.ShapeDtypeStruct((1, LANES), F32)],
        scratch_shapes=[pltpu.VMEM((1, LANES), F32)],
        compiler_params=_params(("arbitrary",)),
    )(dc, small, b_forget_p, dga)


def _fox_pairs(nb, by_key):
    if by_key:
        pairs = [(qi, ki) for ki in range(nb) for qi in range(ki, nb)]
    else:
        pairs = [(qi, ki) for qi in range(nb) for ki in range(qi + 1)]
    return (jnp.asarray(np.array([p[0] for p in pairs], np.int32)),
            jnp.asarray(np.array([p[1] for p in pairs], np.int32)), len(pairs))


def _fox_specs(tb, fd, c_fq, c_fkv):
    gw = FOX_GROUP * fd
    q0, kv0 = c_fq // gw, c_fkv // (2 * gw)
    return dict(
        q=pl.BlockSpec((tb, gw), lambda g, p, qt, kt: (qt[p], q0 + g)),
        kv=pl.BlockSpec((tb, 2 * gw), lambda g, p, qt, kt: (kt[p], kv0 + g)),
        col=pl.BlockSpec((FOX_GROUP, tb, 1), lambda g, p, qt, kt: (g, qt[p], 0)),
        row=pl.BlockSpec((FOX_GROUP, 1, tb), lambda g, p, qt, kt: (g, 0, kt[p])),
        head=pl.BlockSpec((tb, gw), lambda g, p, qt, kt: (qt[p], g)),
        key_kv=pl.BlockSpec((tb, 2 * gw), lambda g, p, qt, kt: (kt[p], g)),
    )


def _fox_mask(qi, ki, tb):
    row = qi * tb + lax.broadcasted_iota(jnp.int32, (tb, tb), 0)
    col = ki * tb + lax.broadcasted_iota(jnp.int32, (tb, tb), 1)
    return (col <= row) & (col >= PAD)


def _fox_heads(q_ref, kv_ref, fd):
    return [(q_ref[:, hh * fd:(hh + 1) * fd], kv_ref[:, 2 * hh * fd:(2 * hh + 1) * fd],
             kv_ref[:, (2 * hh + 1) * fd:(2 * hh + 2) * fd]) for hh in range(FOX_GROUP)]


def _fox_fwd(proj, c_col, c_row, t, fd, c_fq, c_fkv):
    tb = _pick(t, 384, LANES)
    nb = t // tb
    scale = fd ** -0.5
    sp = _fox_specs(tb, fd, c_fq, c_fkv)
    qt, kt, npairs = _fox_pairs(nb, by_key=False)

    def body(qt_ref, kt_ref, q_ref, kv_ref, cq_ref, ck_ref, o_ref, lse_ref, m_ref, l_ref, acc_ref):
        p = pl.program_id(1)
        qi, ki = qt_ref[p], kt_ref[p]

        @pl.when(ki == 0)
        def _():
            m_ref[...] = jnp.full_like(m_ref, -jnp.inf)
            l_ref[...] = jnp.zeros_like(l_ref)
            acc_ref[...] = jnp.zeros_like(acc_ref)

        def update(masked):
            mask = _fox_mask(qi, ki, tb) if masked else None
            for hh, (q, k, v) in enumerate(_fox_heads(q_ref, kv_ref, fd)):
                s = _dot(q, k, NT) * scale + cq_ref[hh] - ck_ref[hh]
                if masked:
                    s = jnp.where(mask, s, MASK_VALUE)
                m_prev = m_ref[hh]
                m_new = jnp.maximum(m_prev, jnp.max(s, axis=-1, keepdims=True))
                alpha = jnp.exp(m_prev - m_new)
                pe = jnp.exp(s - m_new)
                l_ref[hh] = alpha * l_ref[hh] + jnp.sum(pe, axis=-1, keepdims=True)
                acc_ref[hh] = alpha * acc_ref[hh] + _dot(pe.astype(BF16), v, NN)
                m_ref[hh] = m_new

        edge = (ki == 0) | (ki == qi)
        pl.when(edge)(functools.partial(update, True))
        pl.when(jnp.logical_not(edge))(functools.partial(update, False))

        @pl.when(ki == qi)
        def _():
            real = _row_ids(qi * tb, tb) >= PAD
            for hh in range(FOX_GROUP):
                o_ref[:, hh * fd:(hh + 1) * fd] = jnp.where(real, acc_ref[hh] / l_ref[hh], 0.0)
                lse_ref[hh] = m_ref[hh] + jnp.log(l_ref[hh])

    return pl.pallas_call(
        body, name="fox_fwd",
        grid_spec=pltpu.PrefetchScalarGridSpec(
            num_scalar_prefetch=2, grid=(FOX_HEADS // FOX_GROUP, npairs),
            in_specs=[sp["q"], sp["kv"], sp["col"], sp["row"]],
            out_specs=[sp["head"], sp["col"]],
            scratch_shapes=[pltpu.VMEM((FOX_GROUP, tb, 1), F32), pltpu.VMEM((FOX_GROUP, tb, 1), F32),
                            pltpu.VMEM((FOX_GROUP, tb, fd), F32)]),
        out_shape=[jax.ShapeDtypeStruct((t, FOX_HEADS * fd), F32), jax.ShapeDtypeStruct((FOX_HEADS, t, 1), F32)],
        compiler_params=_params(("parallel", "arbitrary")),
    )(qt, kt, proj, proj, c_col, c_row)


def _fox_delta(o_fox, do_fox, t, fd):
    tb = _pick(t, 384, LANES)

    def body(o_ref, do_ref, out_ref):
        for h in range(FOX_HEADS):
            sl = slice(h * fd, (h + 1) * fd)
            out_ref[h] = jnp.sum(o_ref[:, sl] * do_ref[:, sl].astype(BF16).astype(F32), axis=-1, keepdims=True)

    w = FOX_HEADS * fd
    return pl.pallas_call(
        body, name="fox_delta", grid=(t // tb,),
        in_specs=[pl.BlockSpec((tb, w), lambda i: (i, 0)), pl.BlockSpec((tb, w), lambda i: (i, 0))],
        out_specs=pl.BlockSpec((FOX_HEADS, tb, 1), lambda i: (0, i, 0)),
        out_shape=jax.ShapeDtypeStruct((FOX_HEADS, t, 1), F32),
        compiler_params=_params(("parallel",)),
    )(o_fox, do_fox)


def _fox_bwd(proj, c_col, c_row, lse, delta, do_fox, t, fd, c_fq, c_fkv):
    tb = _pick(t, 384, LANES)
    nb = t // tb
    scale = fd ** -0.5
    sp = _fox_specs(tb, fd, c_fq, c_fkv)
    qt, kt, npairs = _fox_pairs(nb, by_key=True)
    gw = FOX_GROUP * fd

    def body(qt_ref, kt_ref, q_ref, kv_ref, cq_ref, ck_ref, lse_ref, dl_ref, do_ref, dq_ref, dkv_ref, dc_ref, dr_ref,
             dq_acc, dk_acc, dv_acc, dc_acc, dr_acc):
        p = pl.program_id(1)
        qi, ki = qt_ref[p], kt_ref[p]

        @pl.when(p == 0)
        def _():
            dq_acc[...] = jnp.zeros_like(dq_acc)
            dr_acc[...] = jnp.zeros_like(dr_acc)

        @pl.when(qi == ki)
        def _():
            dk_acc[...] = jnp.zeros_like(dk_acc)
            dv_acc[...] = jnp.zeros_like(dv_acc)
            dc_acc[...] = jnp.zeros_like(dc_acc)

        rows = pl.ds(pl.multiple_of(qi * tb, LANES), tb)

        def update(masked):
            mask = _fox_mask(qi, ki, tb) if masked else None
            for hh, (q, k, v) in enumerate(_fox_heads(q_ref, kv_ref, fd)):
                do = _bf(do_ref[:, hh * fd:(hh + 1) * fd])
                s = _dot(q, k, NT) * scale + cq_ref[hh] - ck_ref[hh]
                if masked:
                    s = jnp.where(mask, s, MASK_VALUE)
                pr = jnp.exp(s - lse_ref[hh])
                dp = _dot(do, v, NT)
                ds = pr * (dp - dl_ref[hh])
                ds16 = ds.astype(BF16)
                dv_acc[hh] += _dot(pr.astype(BF16), do, TN)
                dk_acc[hh] += _dot(ds16, q, TN)
                dc_acc[hh] += _colsum(ds)
                dr_acc[hh, rows, :] += jnp.sum(ds, axis=-1, keepdims=True)
                dq_acc[hh, rows, :] += _dot(ds16, k, NN)

        edge = (ki == 0) | (ki == qi)
        pl.when(edge)(functools.partial(update, True))
        pl.when(jnp.logical_not(edge))(functools.partial(update, False))

        @pl.when(qi == nb - 1)
        def _():
            for hh in range(FOX_GROUP):
                dkv_ref[:, 2 * hh * fd:(2 * hh + 1) * fd] = (dk_acc[hh] * scale).astype(dkv_ref.dtype)
                dkv_ref[:, (2 * hh + 1) * fd:(2 * hh + 2) * fd] = dv_acc[hh].astype(dkv_ref.dtype)
                dc_ref[hh] = -dc_acc[hh]

        @pl.when(p == npairs - 1)
        def _():
            for hh in range(FOX_GROUP):
                dq_ref[:, hh * fd:(hh + 1) * fd] = (dq_acc[hh] * scale).astype(dq_ref.dtype)
            dr_ref[...] = dr_acc[...]

    return pl.pallas_call(
        body, name="fox_bwd",
        grid_spec=pltpu.PrefetchScalarGridSpec(
            num_scalar_prefetch=2, grid=(FOX_HEADS // FOX_GROUP, npairs),
            in_specs=[sp["q"], sp["kv"], sp["col"], sp["row"], sp["col"], sp["col"], sp["head"]],
            out_specs=[pl.BlockSpec((t, gw), lambda g, p, qt, kt: (0, g)), sp["key_kv"], sp["row"],
                       pl.BlockSpec((FOX_GROUP, t, 1), lambda g, p, qt, kt: (g, 0, 0))],
            scratch_shapes=[pltpu.VMEM((FOX_GROUP, t, fd), F32), pltpu.VMEM((FOX_GROUP, tb, fd), F32),
                            pltpu.VMEM((FOX_GROUP, tb, fd), F32), pltpu.VMEM((FOX_GROUP, 1, tb), F32),
                            pltpu.VMEM((FOX_GROUP, t, 1), F32)]),
        out_shape=[jax.ShapeDtypeStruct((t, FOX_HEADS * fd), BF16), jax.ShapeDtypeStruct((t, 2 * FOX_HEADS * fd), BF16),
                   jax.ShapeDtypeStruct((FOX_HEADS, 1, t), F32), jax.ShapeDtypeStruct((FOX_HEADS, t, 1), F32)],
        compiler_params=_params(("parallel", "arbitrary")),
    )(qt, kt, proj, proj, c_col, c_row, lse, delta, do_fox)


def _gla_gate_fwd(small, w_alpha_p, b_alpha, t, gk):
    def fn(row0, s, w, b):
        z = _dot(s.astype(BF16), w, NN) + b
        return (jnp.where(_row_ids(row0, s.shape[0]) >= PAD, _log_sigmoid(z) * (1.0 / GLA_TAU), 0.0),)

    return _ew("gla_gate_fwd", fn, [("tile", small, LANES, 0), ("full", w_alpha_p), ("full", b_alpha)],
               [("tile", F32, gk)], t, 264)


def _gla_gate_bwd(dglog, small, w_alpha_p, b_alpha, t, gk):
    def fn(row0, dg, s, w, b):
        z = _dot(s.astype(BF16), w, NN) + b
        dz = jnp.where(_row_ids(row0, s.shape[0]) >= PAD, dg * (1.0 / GLA_TAU) * _sigmoid(-z), 0.0)
        return dz, _colsum(dz)

    return _ew("gla_gate_bwd", fn,
               [("tile", dglog, gk, 0), ("tile", small, LANES, 0), ("full", w_alpha_p), ("full", b_alpha)],
               [("tile", BF16, gk), ("acc", F32, gk)], t, 264)


def _gla_chunk(q, k, g, scale, cs):
    r = lax.broadcasted_iota(jnp.int32, (cs, cs), 0)
    c = lax.broadcasted_iota(jnp.int32, (cs, cs), 1)
    causal = c <= r
    b = _tri_dot(causal.astype(BF16), g)
    bl = b[cs - 1:cs, :]
    eb, einv, eend = jnp.exp(b), jnp.exp(-b), jnp.exp(bl - b)
    qd = q.astype(F32) * scale * eb
    kf = k.astype(F32)
    return causal, (eb, einv, eend), bl, qd, kf * einv, kf * eend


def _gla_fwd(proj, glog, t, dk, dv, c_q, c_k, c_v):
    cs = GLA_CHUNK
    nc = t // cs
    wk, wv = GLA_HEADS * dk, GLA_HEADS * dv
    scale = dk ** -0.5

    def body(q_ref, k_ref, v_ref, g_ref, o_ref, sp_ref, st_ref):
        @pl.when(pl.program_id(0) == 0)
        def _():
            st_ref[...] = jnp.zeros_like(st_ref)

        for h in range(GLA_HEADS):
            ks, vs = slice(h * dk, (h + 1) * dk), slice(h * dv, (h + 1) * dv)
            v = v_ref[:, vs]
            causal, _, bl, qd, ki, ke = _gla_chunk(q_ref[:, ks], k_ref[:, ks], g_ref[:, ks], scale, cs)
            st = st_ref[h]
            sp_ref[h] = st
            a = jnp.where(causal, _dot(qd.astype(BF16), ki.astype(BF16), NT), 0.0)
            o_ref[:, vs] = _dot(a.astype(BF16), v, NN) + _dot(qd.astype(BF16), st.astype(BF16), NT)
            st_ref[h] = st * jnp.exp(bl) + _dot(v, ke.astype(BF16), TN)

    return pl.pallas_call(
        body, name="gla_fwd", grid=(nc,),
        in_specs=[pl.BlockSpec((cs, wk), lambda n: (n, c_q // wk)), pl.BlockSpec((cs, wk), lambda n: (n, c_k // wk)),
                  pl.BlockSpec((cs, wv), lambda n: (n, c_v // wv)), pl.BlockSpec((cs, wk), lambda n: (n, 0))],
        out_specs=[pl.BlockSpec((cs, wv), lambda n: (n, 0)),
                   pl.BlockSpec((None, GLA_HEADS, dv, dk), lambda n: (n, 0, 0, 0))],
        out_shape=[jax.ShapeDtypeStruct((t, wv), F32), jax.ShapeDtypeStruct((nc, GLA_HEADS, dv, dk), F32)],
        scratch_shapes=[pltpu.VMEM((GLA_HEADS, dv, dk), F32)],
        compiler_params=_params(("arbitrary",)),
    )(proj, proj, proj, glog)


def _gla_bwd(proj, glog, s_prev, do_raw, t, dk, dv, c_q, c_k, c_v):
    cs = GLA_CHUNK
    nc = t // cs
    wk, wv = GLA_HEADS * dk, GLA_HEADS * dv
    scale = dk ** -0.5

    def body(q_ref, k_ref, v_ref, g_ref, sp_ref, do_ref, dq_ref, dk_ref, dv_ref, dg_ref, dst_ref):
        @pl.when(pl.program_id(0) == 0)
        def _():
            dst_ref[...] = jnp.zeros_like(dst_ref)

        for h in range(GLA_HEADS):
            ks, vs = slice(h * dk, (h + 1) * dk), slice(h * dv, (h + 1) * dv)
            v = v_ref[:, vs]
            do = do_ref[:, vs].astype(BF16)
            causal, (eb, einv, eend), bl, qd, ki, ke = _gla_chunk(q_ref[:, ks], k_ref[:, ks], g_ref[:, ks], scale, cs)
            qd16, ki16, ke16 = qd.astype(BF16), ki.astype(BF16), ke.astype(BF16)
            st = sp_ref[h]
            dst = dst_ref[h]
            dst16 = dst.astype(BF16)
            a = jnp.where(causal, _dot(qd16, ki16, NT), 0.0).astype(BF16)
            da = jnp.where(causal, _dot(do, v, NT), 0.0).astype(BF16)
            dvv = _dot(a, do, TN) + _dot(ke16, dst16, NT)
            dqd = _dot(da, ki16, NN) + _dot(do, st.astype(BF16), NN)
            dki = _dot(da, qd16, TN)
            dke = _dot(v, dst16, NN)
            dl = jnp.exp(bl)
            ddl = _colsum(dst * st)
            dst_ref[h] = dst * dl + _dot(do, qd16, TN)
            dq_ref[:, ks] = (dqd * eb * scale).astype(dq_ref.dtype)
            dk_ref[:, ks] = (dki * einv + dke * eend).astype(dk_ref.dtype)
            dv_ref[:, vs] = dvv.astype(dv_ref.dtype)
            db = dqd * qd - dki * ki - dke * ke
            db_last = _colsum(dke * ke) + ddl * dl
            r = lax.broadcasted_iota(jnp.int32, (cs, cs), 0)
            c = lax.broadcasted_iota(jnp.int32, (cs, cs), 1)
            dg_ref[:, ks] = _tri_dot((c >= r).astype(BF16), db) + db_last

    rev = lambda f: (lambda n: f(nc - 1 - n))
    return pl.pallas_call(
        body, name="gla_bwd", grid=(nc,),
        in_specs=[pl.BlockSpec((cs, wk), rev(lambda n: (n, c_q // wk))), pl.BlockSpec((cs, wk), rev(lambda n: (n, c_k // wk))),
                  pl.BlockSpec((cs, wv), rev(lambda n: (n, c_v // wv))), pl.BlockSpec((cs, wk), rev(lambda n: (n, 0))),
                  pl.BlockSpec((None, GLA_HEADS, dv, dk), rev(lambda n: (n, 0, 0, 0))),
                  pl.BlockSpec((cs, wv), rev(lambda n: (n, 0)))],
        out_specs=[pl.BlockSpec((cs, wk), rev(lambda n: (n, 0))), pl.BlockSpec((cs, wk), rev(lambda n: (n, 0))),
                   pl.BlockSpec((cs, wv), rev(lambda n: (n, 0))), pl.BlockSpec((cs, wk), rev(lambda n: (n, 0)))],
        out_shape=[jax.ShapeDtypeStruct((t, wk), BF16), jax.ShapeDtypeStruct((t, wk), BF16),
                   jax.ShapeDtypeStruct((t, wv), BF16), jax.ShapeDtypeStruct((t, wk), F32)],
        scratch_shapes=[pltpu.VMEM((GLA_HEADS, dv, dk), F32)],
        compiler_params=_params(("arbitrary",)),
    )(proj, proj, proj, glog, s_prev, do_raw)


def _gla_post_fwd(o_raw, proj, gn, t, dv, c_gr):
    w = GLA_HEADS * dv

    def fn(row0, o, gr, g):
        gr = gr.astype(F32)
        outs = []
        for h in range(GLA_HEADS):
            oh = o[:, h * dv:(h + 1) * dv]
            outs.append(oh * lax.rsqrt(jnp.mean(oh * oh, axis=-1, keepdims=True) + EPS))
        on = jnp.concatenate(outs, axis=1) * g
        return (on * (gr * _sigmoid(gr)),)

    return _ew("gla_post_fwd", fn, [("tile", o_raw, w, 0), ("tile", proj, w, c_gr), ("full", gn)],
               [("tile", BF16, w)], t, 264)


def _gla_post_bwd(o_raw, proj, gn, do_gla, t, dv, c_gr):
    w = GLA_HEADS * dv

    def fn(row0, o, gr, g, do):
        gr = gr.astype(F32)
        sg = _sigmoid(gr)
        don = do * (gr * sg)
        ohs, dos = [], []
        for h in range(GLA_HEADS):
            sl = slice(h * dv, (h + 1) * dv)
            oh = o[:, sl]
            r = lax.rsqrt(jnp.mean(oh * oh, axis=-1, keepdims=True) + EPS)
            xh = oh * r
            dxh = don[:, sl] * g[:, sl]
            ohs.append(xh)
            dos.append(r * (dxh - xh * jnp.mean(dxh * xh, axis=-1, keepdims=True)))
        xh = jnp.concatenate(ohs, axis=1)
        dgr = do * (xh * g) * (sg * (1.0 + gr * (1.0 - sg)))
        return jnp.concatenate(dos, axis=1), dgr, _colsum(don * xh)

    return _ew("gla_post_bwd", fn,
               [("tile", o_raw, w, 0), ("tile", proj, w, c_gr), ("full", gn), ("tile", do_gla, w, 0)],
               [("tile", F32, w), ("tile", BF16, w), ("acc", F32, w)], t, 264)


def _adamw(name, w, g, m, v, layer=None, into=None):
    nl, rows, cols = w.shape
    tm = _pick(rows, max(8, (512 * 1024) // max(cols, 1) // 8 * 8), 8)

    def body(w_ref, g_ref, m_ref, v_ref, *rest):
        go_ref, d_ref, nm_ref, nv_ref = rest[-4:]
        gg = g_ref[...]
        nm = ADAM_B1 * m_ref[...] + (1.0 - ADAM_B1) * gg
        nv = ADAM_B2 * v_ref[...] + (1.0 - ADAM_B2) * (gg * gg)
        m_hat = nm / (1.0 - ADAM_B1 ** ADAM_STEP)
        v_hat = nv / (1.0 - ADAM_B2 ** ADAM_STEP)
        go_ref[...] = gg
        d_ref[...] = -ADAM_LR * (m_hat / (jnp.sqrt(v_hat) + ADAM_EPS) + ADAM_WD * w_ref[...])
        nm_ref[...] = nm
        nv_ref[...] = nv

    out_shape = [jax.ShapeDtypeStruct((nl, rows, cols), F32)] * 4
    if layer is None:
        spec = pl.BlockSpec((None, tm, cols), lambda l, i: (l, i, 0))
        return pl.pallas_call(
            body, name=name, grid=(nl, rows // tm), in_specs=[spec] * 4, out_specs=[spec] * 4, out_shape=out_shape,
            compiler_params=_params(("parallel", "parallel")),
        )(w, g, m, v)
    spec = pl.BlockSpec((None, tm, cols), lambda i: (layer, i, 0))
    in_specs = [spec, pl.BlockSpec((tm, cols), lambda i: (i, 0)), spec, spec]
    args, aliases = [w, g, m, v], {}
    if into is not None:
        in_specs += [pl.BlockSpec(memory_space=pl.ANY)] * 4
        args += list(into)
        aliases = {4 + k: k for k in range(4)}
    return pl.pallas_call(
        body, name=name, grid=(rows // tm,), in_specs=in_specs, out_specs=[spec] * 4, out_shape=out_shape,
        input_output_aliases=aliases, compiler_params=_params(("parallel",)),
    )(*args)


def _me():
    return lax.axis_index("x"), lax.axis_index("y"), lax.axis_index("c")


def _hbm_specs(n):
    return [pl.BlockSpec(memory_space=pl.ANY)] * n


_HBM = pl.BlockSpec(memory_space=pltpu.HBM)
_SEM = pl.BlockSpec(memory_space=pltpu.SEMAPHORE)
_EFFECT = pltpu.SideEffectType.DATAFLOW_SIDE_EFFECTING
N_PEERS = N_CHIPS - 1


def _other_chips(x, y):
    return [(1 - x, y), (x, 1 - y), (1 - x, 1 - y)]


def _ici_copies(plan, src, land, send_sems, recv_sems):
    x, y, c = _me()
    copies = []
    for i in range(len(src)):
        for j, chip in enumerate(_other_chips(x, y)):
            s, d = plan(i, src[i], land[i], j, chip, (x, y, c))
            copies.append(pltpu.make_async_remote_copy(
                src_ref=s, dst_ref=d, send_sem=send_sems.at[N_PEERS * i + j], recv_sem=recv_sems.at[N_PEERS * i + j],
                device_id=(*chip, c), device_id_type=MESH))
    return copies


def _ici_start(name, srcs, lands, plan, after=None):
    n = len(srcs)
    extra = [] if after is None else [after]

    def body(*refs):
        src, land = refs[:n], refs[n:2 * n]
        send_sems, recv_sems = refs[2 * n + len(extra)], refs[2 * n + len(extra) + 1]
        token = refs[-1]
        for cp in _ici_copies(plan, src, land, send_sems, recv_sems):
            cp.start()
        token[...] = jnp.zeros_like(token)

    out_shape = ([pltpu.SemaphoreType.DMA((N_PEERS * n,)), pltpu.SemaphoreType.DMA((N_PEERS * n,))]
                 + [pltpu.HBM(a.shape, a.dtype) for a in list(srcs) + list(lands)]
                 + [jax.ShapeDtypeStruct((8, LANES), F32)])
    res = pl.pallas_call(
        body, name=name, out_shape=out_shape,
        in_specs=[_HBM] * (2 * n) + [pl.BlockSpec(memory_space=pl.ANY)] * len(extra),
        out_specs=[_SEM, _SEM] + [_HBM] * (2 * n) + [pl.BlockSpec(memory_space=pltpu.VMEM)],
        input_output_aliases={i: 2 + i for i in range(2 * n)},
        compiler_params=pltpu.CompilerParams(has_side_effects=_EFFECT),
    )(*[pltpu.with_memory_space_constraint(a, pltpu.HBM) for a in list(srcs) + list(lands)], *extra)
    return res[0], res[1], res[2:2 + n], res[2 + n:2 + 2 * n], res[-1]


def _ici_wait(name, send_sems, recv_sems, srcs, lands, plan, after):
    n = len(srcs)

    def body(*refs):
        src, land = refs[:n], refs[n:2 * n]
        s_sems, r_sems = refs[2 * n], refs[2 * n + 1]
        for cp in _ici_copies(plan, src, land, s_sems, r_sems):
            cp.wait_send()
            cp.wait_recv()

    res = pl.pallas_call(
        body, name=name, out_shape=[pltpu.HBM(a.shape, a.dtype) for a in list(srcs) + list(lands)],
        in_specs=[_HBM] * (2 * n) + [_SEM, _SEM, pl.BlockSpec(memory_space=pl.ANY)], out_specs=[_HBM] * (2 * n),
        input_output_aliases={i: i for i in range(2 * n)},
        compiler_params=pltpu.CompilerParams(has_side_effects=_EFFECT),
    )(*srcs, *lands, send_sems, recv_sems, after)
    return res[:n], res[n:]


def _half(ref_rows, c):
    half = ref_rows // 2
    return pl.ds(c * half, half)


def _gather_plan(i, src, land, j, chip, me):
    x, y, c = me
    rows = _half(src.shape[0], c)
    return src.at[rows], land.at[2 * x + y, rows]


def _scatter_plan(i, src, land, j, chip, me):
    x, y, _ = me
    return src.at[2 * chip[0] + chip[1]], land.at[2 * x + y]


def _forward_halves(name, gathered):
    n = len(gathered)

    def body(*refs):
        buf = refs[n:2 * n]
        send_sems, recv_sems = refs[2 * n:]
        x, y, c = _me()
        copies = []
        for i in range(n):
            rows = _half(buf[i].shape[1], c)
            for j, (px, py) in enumerate(_other_chips(x, y)):
                cp = pltpu.make_async_remote_copy(
                    src_ref=buf[i].at[2 * px + py, rows], dst_ref=buf[i].at[2 * px + py, rows],
                    send_sem=send_sems.at[N_PEERS * i + j], recv_sem=recv_sems.at[N_PEERS * i + j],
                    device_id=(x, y, 1 - c), device_id_type=MESH)
                cp.start()
                copies.append(cp)
        for cp in copies:
            cp.wait_recv()
        for cp in copies:
            cp.wait_send()

    return pl.pallas_call(
        body, name=name, in_specs=_hbm_specs(n), out_specs=_hbm_specs(n),
        out_shape=[jax.ShapeDtypeStruct(g.shape, g.dtype) for g in gathered],
        input_output_aliases={i: i for i in range(n)},
        scratch_shapes=[pltpu.SemaphoreType.DMA((N_PEERS * n,)), pltpu.SemaphoreType.DMA((N_PEERS * n,))],
    )(*gathered)


def _all_gather_small(v):
    def body(v_ref, out_ref, send_sems, recv_sems, local_sem):
        x, y, c = _me()
        mine = pltpu.make_async_copy(v_ref, out_ref.at[4 * x + 2 * y + c], local_sem)
        mine.start()
        copies = []
        for k in range(1, N_DEV):
            peer = (x ^ ((k >> 2) & 1), y ^ ((k >> 1) & 1), c ^ (k & 1))
            cp = pltpu.make_async_remote_copy(
                src_ref=v_ref, dst_ref=out_ref.at[4 * x + 2 * y + c], send_sem=send_sems.at[k - 1],
                recv_sem=recv_sems.at[k - 1], device_id=peer, device_id_type=MESH)
            cp.start()
            copies.append(cp)
        for cp in copies:
            cp.wait_recv()
        for cp in copies:
            cp.wait_send()
        mine.wait()

    return pl.pallas_call(
        body, name="all_gather_small", in_specs=_hbm_specs(1), out_specs=pl.BlockSpec(memory_space=pl.ANY),
        out_shape=jax.ShapeDtypeStruct((N_DEV,) + v.shape, v.dtype),
        scratch_shapes=[pltpu.SemaphoreType.DMA((N_DEV - 1,)), pltpu.SemaphoreType.DMA((N_DEV - 1,)),
                        pltpu.SemaphoreType.DMA],
    )(v)


def _swap_halves(name, parts):
    n = len(parts)

    def body(*refs):
        src, dst = refs[:n], refs[n:2 * n]
        send_sems, recv_sems = refs[2 * n:]
        x, y, c = _me()
        copies = []
        for i in range(n):
            cp = pltpu.make_async_remote_copy(
                src_ref=src[i].at[:, _half(src[i].shape[1], 1 - c)], dst_ref=dst[i], send_sem=send_sems.at[i],
                recv_sem=recv_sems.at[i], device_id=(x, y, 1 - c), device_id_type=MESH)
            cp.start()
            copies.append(cp)
        for cp in copies:
            cp.wait_recv()
        for cp in copies:
            cp.wait_send()

    out_shape = [jax.ShapeDtypeStruct((g.shape[0], g.shape[1] // 2, g.shape[2]), g.dtype) for g in parts]
    return pl.pallas_call(
        body, name=name, in_specs=_hbm_specs(n), out_specs=_hbm_specs(n), out_shape=out_shape,
        scratch_shapes=[pltpu.SemaphoreType.DMA((n,)), pltpu.SemaphoreType.DMA((n,))],
    )(*parts)


def _share_halves(name, grads):
    n = len(grads)

    def body(*refs):
        buf = refs[n:2 * n]
        send_sems, recv_sems = refs[2 * n:]
        x, y, c = _me()
        copies = []
        for i in range(n):
            rows = _half(buf[i].shape[0], c)
            cp = pltpu.make_async_remote_copy(
                src_ref=buf[i].at[rows], dst_ref=buf[i].at[rows], send_sem=send_sems.at[i],
                recv_sem=recv_sems.at[i], device_id=(x, y, 1 - c), device_id_type=MESH)
            cp.start()
            copies.append(cp)
        for cp in copies:
            cp.wait_recv()
        for cp in copies:
            cp.wait_send()

    out_shape = [jax.ShapeDtypeStruct(s.shape, s.dtype) for s in grads]
    return pl.pallas_call(
        body, name=name, in_specs=_hbm_specs(n), out_specs=_hbm_specs(n), out_shape=out_shape,
        input_output_aliases={i: i for i in range(n)},
        scratch_shapes=[pltpu.SemaphoreType.DMA((n,)), pltpu.SemaphoreType.DMA((n,))],
    )(*grads)


def _pair_sum(name, own, recv, core):
    nch, half, cdim = recv.shape
    tm = _pick(half, max(8, (512 * 1024) // cdim // 16 * 16), 16)
    nt = half // tm

    def body(c_ref, a_ref, b_ref, s_ref, s16_ref):
        s = a_ref[...] + b_ref[...]
        s_ref[...] = s
        s16_ref[...] = s.astype(BF16)

    spec = pl.BlockSpec((None, tm, cdim), lambda j, i, c: (j, i, 0))
    return pl.pallas_call(
        body, name=name,
        grid_spec=pltpu.PrefetchScalarGridSpec(
            num_scalar_prefetch=1, grid=(nch, nt),
            in_specs=[pl.BlockSpec((None, tm, cdim), lambda j, i, c: (j, c[0] * nt + i, 0)), spec],
            out_specs=[spec, spec]),
        out_shape=[jax.ShapeDtypeStruct((nch, half, cdim), F32), jax.ShapeDtypeStruct((nch, half, cdim), BF16)],
        compiler_params=_params(("parallel", "parallel")),
    )(core, own, recv)


def _chip_sum(name, own, recv, chip, core):
    nch, half, cdim = own.shape
    tm = _pick(half, max(8, (512 * 1024) // cdim // 16 * 16), 16)
    nt = half // tm

    def body(c_ref, k_ref, own_ref, *rest):
        recv_refs, out_ref = rest[:nch], rest[-1]
        me = c_ref[0]
        acc = None
        for j in range(nch):
            term = jnp.where(me == j, own_ref[...], recv_refs[j][...].astype(F32))
            acc = term if acc is None else acc + term
        out_ref[...] = acc

    recv_specs = [pl.BlockSpec((None, tm, cdim), functools.partial(lambda i, c, k, j: (j, i, 0), j=j))
                  for j in range(nch)]
    return pl.pallas_call(
        body, name=name,
        grid_spec=pltpu.PrefetchScalarGridSpec(
            num_scalar_prefetch=2, grid=(nt,),
            in_specs=[pl.BlockSpec((None, tm, cdim), lambda i, c, k: (c[0], i, 0))] + recv_specs,
            out_specs=pl.BlockSpec((tm, cdim), lambda i, c, k: (k[0] * nt + i, 0))),
        out_shape=jax.ShapeDtypeStruct((2 * half, cdim), F32),
        compiler_params=_params(("parallel",)),
    )(chip, core, own, *([recv] * nch))


def _sum_devices(gathered):
    _, r, cdim = gathered.shape

    def body(g_ref, o_ref):
        acc = g_ref[0]
        for k in range(1, N_DEV):
            acc = acc + g_ref[k]
        o_ref[...] = acc

    return pl.pallas_call(
        body, name="sum_devices", out_shape=jax.ShapeDtypeStruct((r, cdim), F32),
        compiler_params=_params(),
    )(gathered)


class _Layout:
    def __init__(self, d):
        self.d = d
        self.fw = d // 2
        self.fd = self.fw // FOX_HEADS
        self.gk = d // 2
        self.gv = d
        self.dk = self.gk // GLA_HEADS
        self.dv = self.gv // GLA_HEADS
        self.c_fq = 0
        self.c_gq = self.fw
        self.c_gv = self.c_gq + self.gk
        self.c_gr = self.c_gv + self.gv
        self.c_fkv = self.c_gr + self.gv
        self.c_gates = self.c_fkv + 2 * self.fw
        self.c_gk = self.c_gates + 2 * d
        self.c_small = self.c_gk + self.gk
        self.n_main = self.c_small
        self.n_p = self.c_small + LANES
        self.o_fk = self.fw
        self.o_fv = 2 * self.fw
        self.o_ff = 3 * self.fw
        self.o_gq = self.o_ff + FOX_HEADS
        self.o_gk = self.o_gq + self.gk
        self.o_gv = self.o_gk + self.gk
        self.o_gr = self.o_gv + self.gv
        self.o_ga = self.o_gr + self.gv
        self.o_gf = self.o_ga + GLA_RANK
        self.o_gg = self.o_gf + d
        self.n_orig = self.o_gg + d

    def to_p(self, shards):
        per = self.n_orig // N_CHIPS
        ranges = [(0, self.fw), (self.o_gq, self.gk), (self.o_gv, self.gv), (self.o_gr, self.gv)]
        for h in range(FOX_HEADS):
            ranges += [(self.o_fk + h * self.fd, self.fd), (self.o_fv + h * self.fd, self.fd)]
        ranges += [(self.o_gf, 2 * self.d), (self.o_gk, self.gk), (self.o_ff, FOX_HEADS), (self.o_ga, GLA_RANK)]
        pieces = []
        for a, width in ranges:
            for j in range(a // per, (a + width - 1) // per + 1):
                lo, hi = max(a, j * per), min(a + width, (j + 1) * per)
                pieces.append(shards[j][:, lo - j * per:hi - j * per])
        pieces.append(jnp.zeros((shards.shape[1], LANES - FOX_HEADS - GLA_RANK), shards.dtype))
        return jnp.concatenate(pieces, axis=1)

    def from_segments(self, seg):
        per = self.n_orig // N_CHIPS
        fd = self.fd
        atoms = [("fq", 0, self.fw)]
        atoms += [("fkv", 2 * h * fd, fd) for h in range(FOX_HEADS)]
        atoms += [("fkv", (2 * h + 1) * fd, fd) for h in range(FOX_HEADS)]
        atoms += [("small", 0, FOX_HEADS), ("gq", 0, self.gk), ("gk", 0, self.gk), ("gv", 0, self.gv),
                  ("gr", 0, self.gv), ("small", FOX_HEADS, GLA_RANK), ("gates", 0, 2 * self.d)]
        shards = [[] for _ in range(N_CHIPS)]
        pos = 0
        for name, c0, width in atoms:
            for j in range(pos // per, (pos + width - 1) // per + 1):
                lo, hi = max(pos, j * per), min(pos + width, (j + 1) * per)
                shards[j].append(seg[name][:, c0 + lo - pos:c0 + hi - pos])
            pos += width
        assert pos == self.n_orig
        return jnp.stack([jnp.concatenate(s, axis=1) for s in shards])


def _layer_fwd(lay, h, p, t, late=None):
    d = lay.d
    xn = _rms_fwd("rms_mix_fwd", h, p["norm_mix_g"], t, d)
    proj = _mm("mm_proj", xn, p["w_in"], mode="nn", m=t, n=lay.n_main, k=d, out_dtypes=(BF16,))
    small = _mm("mm_small", xn, p["w_in"], mode="nn", m=t, n=LANES, k=d, b_c0=lay.c_small)
    cs = _fox_gate_fwd(small, p["b_forget_p"], t)
    ct = cs[:, :FOX_HEADS].T
    c_col, c_row = ct[:, :, None], ct[:, None, :]
    o_fox, lse = _fox_fwd(proj, c_col, c_row, t, lay.fd, lay.c_fq, lay.c_fkv)
    glog = _gla_gate_fwd(small, p["w_alpha_p"], p["b_alpha"], t, lay.gk)
    o_raw, s_prev = _gla_fwd(proj, glog, t, lay.dk, lay.dv, lay.c_gq, lay.c_gk, lay.c_gv)
    o_gla = _gla_post_fwd(o_raw, proj, p["gla_norm_g"], t, lay.dv, lay.c_gr)
    if late is not None:
        p.update(late(o_gla))
    a_fox = _mm("mm_o_fox", o_fox, p["w_o_fox"], mode="nn", m=t, n=d, k=lay.fw, b_shards=N_CHIPS)
    a_gla = _mm("mm_o_gla", o_gla, p["w_o_gla"], mode="nn", m=t, n=d, k=lay.gv)
    y = _merge_fwd(a_fox, a_gla, proj, lay.c_gates, t, d)
    h1 = _mm("mm_out", y, p["w_out"], mode="nn", m=t, n=d, k=d, extras=[h], epilogue=lambda acc, res: (res + acc,))
    xn2 = _rms_fwd("rms_mlp_fwd", h1, p["norm_mlp_g"], t, d)
    u, act = _mm("mm_ff1", xn2, p["w_ff1"], mode="nn", m=t, n=4 * d, k=d, out_dtypes=(BF16, BF16), b_shards=N_CHIPS,
                 epilogue=lambda acc: (acc, jnp.square(jnp.maximum(acc, 0.0))))
    h2 = _mm("mm_ff2", act, p["w_ff2"], mode="nn", m=t, n=d, k=4 * d, extras=[h1],
             epilogue=lambda acc, res: (res + acc,))
    saved = dict(h=h, xn=xn, proj=proj, small=small, c_col=c_col, c_row=c_row, o_fox=o_fox, lse=lse, glog=glog,
                 o_raw=o_raw, s_prev=s_prev, o_gla=o_gla, a_fox=a_fox, a_gla=a_gla, y=y, h1=h1, xn2=xn2, u=u, act=act)
    return h2, saved


def _layer_bwd(lay, dh2, p, s, t, gate=None):
    d = lay.d
    g = {}
    du = _mm("mm_dact", dh2, p["w_ff2"], mode="nt", m=t, n=4 * d, k=d, extras=[s["u"]], out_dtypes=(BF16,),
             epilogue=lambda acc, u: (acc * (2.0 * jnp.maximum(u.astype(F32), 0.0)),))
    g["w_ff2"] = _mm("mm_dw_ff2", s["act"], dh2, mode="tn", m=4 * d, n=d, k=t)
    g["w_ff1"] = _mm("mm_dw_ff1", s["xn2"], du, mode="tn", m=d, n=4 * d, k=t, out_shards=N_CHIPS)
    dxn2 = _mm("mm_dxn2", du, p["w_ff1"], mode="nt", m=t, n=d, k=4 * d, b_shards=N_CHIPS)
    dh1, g["norm_mlp_g"] = _rms_bwd("rms_mlp_bwd", s["h1"], p["norm_mlp_g"], dxn2, dh2, t, d)
    dy = _mm("mm_dy", dh1, p["w_out"], mode="nt", m=t, n=d, k=d)
    g["w_out"] = _mm("mm_dw_out", s["y"], dh1, mode="tn", m=d, n=d, k=t)
    da_fox, da_gla, dgates = _merge_bwd(dy, s["a_fox"], s["a_gla"], s["proj"], lay.c_gates, t, d)
    g["w_o_fox"] = _mm("mm_dw_o_fox", s["o_fox"], da_fox, mode="tn", m=lay.fw, n=d, k=t, out_shards=N_CHIPS)
    do_fox = _mm("mm_do_fox", da_fox, p["w_o_fox"], mode="nt", m=t, n=lay.fw, k=d, b_shards=N_CHIPS)
    g["w_o_gla"] = _mm("mm_dw_o_gla", s["o_gla"], da_gla, mode="tn", m=lay.gv, n=d, k=t)
    do_gla = _mm("mm_do_gla", da_gla, p["w_o_gla"], mode="nt", m=t, n=lay.gv, k=d)
    do_raw, dgr, g["gla_norm_g"] = _gla_post_bwd(s["o_raw"], s["proj"], p["gla_norm_g"], do_gla, t, lay.dv, lay.c_gr)
    dgq, dgk, dgv, dglog = _gla_bwd(s["proj"], s["glog"], s["s_prev"], do_raw, t, lay.dk, lay.dv,
                                    lay.c_gq, lay.c_gk, lay.c_gv)
    dz, g["b_alpha"] = _gla_gate_bwd(dglog, s["small"], p["w_alpha_p"], p["b_alpha"], t, lay.gk)
    g["w_alpha_p"] = _mm("mm_dw_alpha", s["small"], dz, mode="tn", m=LANES, n=lay.gk, k=t)
    dga = _mm("mm_dga", dz, p["w_alpha_p"], mode="nt", m=t, n=LANES, k=lay.gk)
    delta = _fox_delta(s["o_fox"], do_fox, t, lay.fd)
    dfq, dfkv, dc, dr = _fox_bwd(s["proj"], s["c_col"], s["c_row"], s["lse"], delta, do_fox, t, lay.fd,
                                 lay.c_fq, lay.c_fkv)
    dc_p = jnp.pad((dc[:, 0, :] + dr[:, :, 0]).T, ((0, 0), (0, LANES - FOX_HEADS)))
    dsmall, g["b_forget_p"] = _fox_gate_bwd(dc_p, s["small"], p["b_forget_p"], dga, t)
    segs = [("fq", dfq, lay.c_fq), ("gq", dgq, lay.c_gq), ("gv", dgv, lay.c_gv), ("gr", dgr, lay.c_gr),
            ("fkv", dfkv, lay.c_fkv), ("gates", dgates, lay.c_gates), ("gk", dgk, lay.c_gk),
            ("small", dsmall, lay.c_small)]
    dxn = None
    dw_in = {}
    for nm, dseg, c0 in segs:
        width = dseg.shape[1]
        dw_in[nm] = _mm("mm_dw_in_" + nm, s["xn"], dseg, mode="tn", m=d, n=width, k=t)
        if dxn is None:
            dxn = _mm("mm_dxn_" + nm, dseg, p["w_in"], mode="nt", m=t, n=d, k=width, b_c0=c0)
        else:
            dxn = _mm("mm_dxn_" + nm, dseg, p["w_in"], mode="nt", m=t, n=d, k=width, b_c0=c0, extras=[dxn],
                      epilogue=lambda acc, prev: (prev + acc,))
    g["w_in"] = lay.from_segments(dw_in)
    gain = p["norm_mix_g"] if gate is None else p["norm_mix_g"] + gate(g)
    dh, g["norm_mix_g"] = _rms_bwd("rms_mix_bwd", s["h"], gain, dxn, dh1, t, d)
    return dh, g


def _sequence_step(x, target, meta, layers, final_g):
    seq, d = x.shape
    t = seq + ROW0
    lay = _Layout(d)
    h = jnp.pad(x, ((ROW0, 0), (0, 0))).at[PAD:ROW0].set(meta)
    target_p = jnp.pad(target, ((ROW0, 0), (0, 0)))
    saved = []
    for p in layers:
        h, s = _layer_fwd(lay, h, p, t)
        saved.append(s)
    dh, dg_final, loss_part = _loss_head(h, final_g, target_p, t, d)
    grads = [None] * len(layers)
    for l in reversed(range(len(layers))):
        dh, grads[l] = _layer_bwd(lay, dh, layers[l], saved[l], t)
    return loss_part, dh[ROW0:], dh[PAD:ROW0], grads, dg_final


_SMALL_ROWS = 48


def _pack_small(d, meta, mix, gla, mlp, final, b_alpha, b_forget, w_alpha2):
    rows = [meta.reshape(N_META, d), mix.reshape(DEPTH, d), gla.reshape(DEPTH, d), mlp.reshape(DEPTH, d),
            final.reshape(1, d), b_alpha.reshape(1, d),
            jnp.pad(b_forget.reshape(1, DEPTH * FOX_HEADS), ((0, 0), (0, d - DEPTH * FOX_HEADS))),
            jnp.zeros((7, d), F32), w_alpha2.reshape(GLA_RANK, d)]
    return jnp.concatenate(rows, axis=0)


def _unpack_small(d, packed):
    return dict(meta=packed[:N_META], norm_mix_g=packed[16:18], gla_norm_g=packed[18:20], norm_mlp_g=packed[20:22],
                final_norm_g=packed[22], b_alpha=packed[23].reshape(DEPTH, d // 2),
                b_forget=packed[24, :DEPTH * FOX_HEADS].reshape(DEPTH, FOX_HEADS),
                w_alpha2=packed[32:48].reshape(DEPTH, GLA_RANK, d // 2))


_BIG = ("w_in", "w_o_fox", "w_o_gla", "w_out", "w_ff1", "w_ff2")
_COL_SHARDED = ("w_in", "w_o_fox", "w_ff1")


def _full_matrix(name, gathered_l):
    nch, r, c = gathered_l.shape
    if name in _COL_SHARDED:
        return gathered_l.transpose(1, 0, 2).reshape(r, nch * c)
    return gathered_l.reshape(nch * r, c)


def _shard_major(name, full):
    r, c = full.shape
    if name in _COL_SHARDED:
        return full.reshape(r, N_CHIPS, c // N_CHIPS).transpose(1, 0, 2)
    return full.reshape(N_CHIPS, r // N_CHIPS, c)


def kernel(x, meta_tokens, norm_mix_g, w_in, b_forget, w_alpha2, b_alpha, gla_norm_g, w_o_fox, w_o_gla, w_out, norm_mlp_g, w_ff1, w_ff2, final_norm_g, loss_target, m_meta_tokens, m_norm_mix_g, m_w_in, m_b_forget, m_w_alpha2, m_b_alpha, m_gla_norm_g, m_w_o_fox, m_w_o_gla, m_w_out, m_norm_mlp_g, m_w_ff1, m_w_ff2, m_final_norm_g, v_meta_tokens, v_norm_mix_g, v_w_in, v_b_forget, v_w_alpha2, v_b_alpha, v_gla_norm_g, v_w_o_fox, v_w_o_gla, v_w_out, v_norm_mlp_g, v_w_ff1, v_w_ff2, v_final_norm_g):
    d = x.shape[2]
    lay = _Layout(d)
    xi, yi, ci = lax.axis_index("x"), lax.axis_index("y"), lax.axis_index("c")
    chip = (2 * xi + yi).astype(jnp.int32)
    w = dict(w_in=w_in, w_alpha2=w_alpha2, w_o_fox=w_o_fox, w_o_gla=w_o_gla, w_out=w_out, w_ff1=w_ff1, w_ff2=w_ff2)
    m = dict(w_in=m_w_in, w_alpha2=m_w_alpha2, w_o_fox=m_w_o_fox, w_o_gla=m_w_o_gla, w_out=m_w_out, w_ff1=m_w_ff1,
             w_ff2=m_w_ff2)
    v = dict(w_in=v_w_in, w_alpha2=v_w_alpha2, w_o_fox=v_w_o_fox, w_o_gla=v_w_o_gla, w_out=v_w_out, w_ff1=v_w_ff1,
             w_ff2=v_w_ff2)

    seq = x.shape[1]
    t = seq + ROW0
    core_idx = ci.astype(jnp.int32)[None]
    chip_idx = chip[None]

    cols = d // N_CHIPS
    small_w = jnp.concatenate([meta_tokens, w_alpha2.reshape(-1, cols)], axis=0)
    small_raw = _all_gather_small(small_w)
    small_all = small_raw[0::2]
    alpha_full = small_all[:, N_META:].reshape(N_CHIPS, DEPTH, GLA_RANK, lay.gk // N_CHIPS)
    alpha_full = alpha_full.transpose(1, 2, 0, 3).reshape(DEPTH, GLA_RANK, lay.gk)
    groups = [(0, ("w_in",)), (0, _BIG[1:]), (1, _BIG)]
    started, after = [], small_raw
    for gi, (l, names) in enumerate(groups):
        own16 = [w[n][l].astype(BF16) for n in names]
        lands = [lax.empty((N_CHIPS,) + o.shape, BF16) for o in own16]
        started.append(_ici_start("gather_start_%d" % gi, own16, lands, _gather_plan, after=after))
        after = started[gi][4]
    meta_full = small_all[:, :N_META].transpose(1, 0, 2).reshape(N_META, d) + after[0, 0]

    def gathered(gi, after):
        send_sems, recv_sems, srcs, lands, _ = started[gi]
        srcs, lands = _ici_wait("gather_wait_%d" % gi, send_sems, recv_sems, srcs, lands, _gather_plan, after)
        lands = _forward_halves("gather_forward_%d" % gi, lands)
        return {n: lax.dynamic_update_slice(g, o[None], (chip, 0, 0)) for n, g, o in zip(groups[gi][1], lands, srcs)}

    def early_weights(l, gl):
        w_alpha_p = jnp.zeros((LANES, lay.gk), BF16).at[FOX_HEADS:FOX_HEADS + GLA_RANK].set(
            alpha_full[l].astype(BF16))
        return dict(
            w_in=lay.to_p(gl["w_in"]), w_alpha_p=w_alpha_p,
            norm_mix_g=norm_mix_g[l][None], norm_mlp_g=norm_mlp_g[l][None], gla_norm_g=gla_norm_g[l][None],
            b_alpha=b_alpha[l][None],
            b_forget_p=jnp.pad(b_forget[l][None], ((0, 0), (0, LANES - FOX_HEADS))))

    def late_weights(gl):
        return dict(w_o_fox=gl["w_o_fox"], w_o_gla=_full_matrix("w_o_gla", gl["w_o_gla"]),
                    w_out=_full_matrix("w_out", gl["w_out"]), w_ff1=gl["w_ff1"],
                    w_ff2=_full_matrix("w_ff2", gl["w_ff2"]))

    h = jnp.pad(x[0], ((ROW0, 0), (0, 0))).at[PAD:ROW0].set(meta_full)
    layers, saved = [], []
    layers.append(early_weights(0, gathered(0, after=h)))
    h, s = _layer_fwd(lay, h, layers[0], t, late=lambda after: late_weights(gathered(1, after)))
    saved.append(s)
    gl = gathered(2, after=h)
    layers.append({**early_weights(1, gl), **late_weights(gl)})
    h, s = _layer_fwd(lay, h, layers[1], t)
    saved.append(s)
    dh, dg_final, loss_part = _loss_head(h, final_norm_g[None], jnp.pad(loss_target[0], ((ROW0, 0), (0, 0))), t, d)
    loss = lax.psum(loss_part[0, 0], ("x", "y", "c"))

    def partial_of(g, n):
        return g[n] if n in ("w_ff1", "w_o_fox", "w_in") else _shard_major(n, g[n])

    scattered = [None] * DEPTH

    def start_scatter(l, g, after=None):
        parts = [partial_of(g, n) for n in _BIG]
        from_sibling = _swap_halves("swap_halves_%d" % l, parts)
        sums = [_pair_sum("pair_sum_%d_%s" % (l, n), p_, r_, core_idx) for n, p_, r_ in zip(_BIG, parts, from_sibling)]
        lands = [lax.empty(s16.shape, BF16) for _, s16 in sums]
        send_sems, recv_sems, srcs, lands, token = _ici_start("scatter_start_%d" % l, [s16 for _, s16 in sums], lands,
                                                              _scatter_plan, after=after)
        scattered[l] = (send_sems, recv_sems, srcs, lands, [s32 for s32, _ in sums])
        return token

    grads = [None] * DEPTH
    dh, grads[1] = _layer_bwd(lay, dh, layers[1], saved[1], t, gate=lambda g: start_scatter(1, g)[0, 0])
    dh, grads[0] = _layer_bwd(lay, dh, layers[0], saved[0], t)
    grad_x, d_meta = dh[ROW0:], dh[PAD:ROW0]

    stack = lambda key: jnp.concatenate([grads[l][key] for l in range(DEPTH)], axis=0)
    b_forget_g = jnp.concatenate([grads[l]["b_forget_p"][:, :FOX_HEADS] for l in range(DEPTH)], axis=0)
    alpha_g = jnp.stack([grads[l]["w_alpha_p"][FOX_HEADS:FOX_HEADS + GLA_RANK] for l in range(DEPTH)])
    packed = _pack_small(d, d_meta, stack("norm_mix_g"), stack("gla_norm_g"), stack("norm_mlp_g"), dg_final,
                         stack("b_alpha"), b_forget_g, alpha_g)
    small_g = _unpack_small(d, _sum_devices(_all_gather_small(packed)))
    small_g["meta"] = lax.dynamic_slice_in_dim(small_g["meta"], chip * (d // N_CHIPS), d // N_CHIPS, axis=1)
    alpha_shard = lax.dynamic_slice_in_dim(small_g["w_alpha2"], chip * (lay.gk // N_CHIPS), lay.gk // N_CHIPS, axis=2)

    after = start_scatter(0, grads[0], after=small_g["final_norm_g"])
    outs = {n: None for n in _BIG}
    for l in reversed(range(DEPTH)):
        send_sems, recv_sems, srcs, lands, sums32 = scattered[l]
        _, lands = _ici_wait("scatter_wait_%d" % l, send_sems, recv_sems, srcs, lands, _scatter_plan, after)
        layer_g = [_chip_sum("chip_sum_%d_%s" % (l, n), s32, r_, chip_idx, core_idx)
                   for n, s32, r_ in zip(_BIG, sums32, lands)]
        layer_g = _share_halves("share_halves_%d" % l, layer_g)
        for n, g in zip(_BIG, layer_g):
            outs[n] = _adamw("adamw_%d_%s" % (l, n), w[n], g, m[n], v[n], layer=l, into=outs[n])
        after = outs[_BIG[-1]][0]
    out_g, out_d, out_m, out_v = {}, {}, {}, {}
    for n in _BIG:
        out_g[n], out_d[n], out_m[n], out_v[n] = outs[n]
    out_g["w_alpha2"], out_d["w_alpha2"], out_m["w_alpha2"], out_v["w_alpha2"] = _adamw(
        "adamw_w_alpha2", w["w_alpha2"], alpha_shard, m["w_alpha2"], v["w_alpha2"])
    sm_w = dict(meta_tokens=meta_tokens, norm_mix_g=norm_mix_g, b_forget=b_forget, b_alpha=b_alpha,
                gla_norm_g=gla_norm_g, norm_mlp_g=norm_mlp_g, final_norm_g=final_norm_g)
    sm_m = dict(meta_tokens=m_meta_tokens, norm_mix_g=m_norm_mix_g, b_forget=m_b_forget, b_alpha=m_b_alpha,
                gla_norm_g=m_gla_norm_g, norm_mlp_g=m_norm_mlp_g, final_norm_g=m_final_norm_g)
    sm_v = dict(meta_tokens=v_meta_tokens, norm_mix_g=v_norm_mix_g, b_forget=v_b_forget, b_alpha=v_b_alpha,
                gla_norm_g=v_gla_norm_g, norm_mlp_g=v_norm_mlp_g, final_norm_g=v_final_norm_g)
    sm_g = dict(meta_tokens=small_g["meta"], norm_mix_g=small_g["norm_mix_g"], b_forget=small_g["b_forget"],
                b_alpha=small_g["b_alpha"], gla_norm_g=small_g["gla_norm_g"], norm_mlp_g=small_g["norm_mlp_g"],
                final_norm_g=small_g["final_norm_g"])
    names_small = list(sm_w)
    sizes = [sm_w[n].size for n in names_small]
    width = 512
    total = -(-sum(sizes) // (8 * width)) * (8 * width)

    def pack_flat(dct, fill):
        flat = jnp.concatenate([dct[n].reshape(-1) for n in names_small])
        return jnp.pad(flat, (0, total - flat.shape[0]), constant_values=fill).reshape(1, -1, width)

    res = _adamw("adamw_small", pack_flat(sm_w, 0.0), pack_flat(sm_g, 0.0), pack_flat(sm_m, 0.0), pack_flat(sm_v, 1.0))
    offs = [0]
    for sz in sizes:
        offs.append(offs[-1] + sz)
    for i, n in enumerate(names_small):
        out_g[n] = sm_g[n].reshape(sm_w[n].shape)
        out_d[n], out_m[n], out_v[n] = [r.reshape(-1)[offs[i]:offs[i + 1]].reshape(sm_w[n].shape) for r in res[1:]]

    order = ["meta_tokens", "norm_mix_g", "w_in", "b_forget", "w_alpha2", "b_alpha", "gla_norm_g", "w_o_fox",
             "w_o_gla", "w_out", "norm_mlp_g", "w_ff1", "w_ff2", "final_norm_g"]
    return (loss, grad_x[None], *[out_g[n] for n in order], *[out_d[n] for n in order],
            *[out_m[n] for n in order], *[out_v[n] for n in order])
```

```python
import functools

import numpy as np

import jax
import jax.numpy as jnp
from jax import lax
from jax.experimental import pallas as pl
from jax.experimental.pallas import tpu as pltpu

F32 = jnp.float32
BF16 = jnp.bfloat16

N_META = 16
PAD = 112
ROW0 = PAD + N_META
EPS = 1e-6
MASK_VALUE = -1e30
FOX_HEADS = 8
FOX_GROUP = 2
GLA_HEADS = 4
GLA_RANK = 16
GLA_TAU = 16.0
GLA_CHUNK = 64
DEPTH = 2
N_CHIPS = 4
N_DEV = 8

ADAM_LR = 0.001
ADAM_B1 = 0.9
ADAM_B2 = 0.999
ADAM_EPS = 1e-08
ADAM_WD = 0.01
ADAM_STEP = 10

LANES = 128
VMEM_LIMIT = 56 * 1024 * 1024
MESH = pl.DeviceIdType.MESH


def _pick(n, target, mult):
    best = None
    for d in range(mult, min(n, target) + 1, mult):
        if n % d == 0:
            best = d
    return n if best is None else best


def _params(sem=None):
    return pltpu.CompilerParams(dimension_semantics=sem, vmem_limit_bytes=VMEM_LIMIT)


def _bf(v):
    return v if v.dtype == BF16 else v.astype(BF16)


def _sigmoid(z):
    return 1.0 / (1.0 + jnp.exp(-z))


def _log_sigmoid(z):
    return jnp.minimum(z, 0.0) - jnp.log(1.0 + jnp.exp(-jnp.abs(z)))


def _split3(v):
    a = v.astype(BF16)
    r = v - a.astype(F32)
    b = r.astype(BF16)
    c = (r - b.astype(F32)).astype(BF16)
    return a, b, c


def _dot(a, b, dims):
    return lax.dot_general(a, b, (dims, ((), ())), preferred_element_type=F32)


NN = ((1,), (0,))
NT = ((1,), (1,))
TN = ((0,), (0,))


def _tri_dot(tri, v, dims=NN):
    a, b, c = _split3(v)
    return _dot(tri, a, dims) + _dot(tri, b, dims) + _dot(tri, c, dims)


def _mm(name, a, b, *, mode, m, n, k, b_c0=0, extras=(), epilogue=None, out_dtypes=(F32,),
        b_shards=1, out_shards=1, tm=1056, tn=1024, tk=2048):
    tm = _pick(m, tm, LANES if mode == "tn" else 16)
    tn = _pick(n // max(b_shards if mode == "nn" else 1, out_shards), tn, LANES)
    if mode == "tn":
        tk = _pick(k, 2112, 16)
    else:
        tk = _pick(k // (b_shards if mode == "nt" else 1), tk, LANES)
    assert b_c0 % (tk if mode == "nt" else tn) == 0 and (b_shards == 1 or b_c0 == 0)
    nk = k // tk
    if mode == "tn":
        a_spec = pl.BlockSpec((tk, tm), lambda i, j, kk: (kk, i))
    else:
        a_spec = pl.BlockSpec((tm, tk), lambda i, j, kk: (i, kk))
    if mode == "nt":
        dims = NT
        if b_shards > 1:
            per = (k // b_shards) // tk
            b_spec = pl.BlockSpec((None, tn, tk), lambda i, j, kk: (kk // per, j, kk % per))
        else:
            b_spec = pl.BlockSpec((tn, tk), lambda i, j, kk: (j, kk + b_c0 // tk))
    else:
        dims = NN if mode == "nn" else TN
        if b_shards > 1:
            per = (n // b_shards) // tn
            b_spec = pl.BlockSpec((None, tk, tn), lambda i, j, kk: (j // per, kk, j % per))
        else:
            b_spec = pl.BlockSpec((tk, tn), lambda i, j, kk: (kk, j + b_c0 // tn))
    ex_specs = [pl.BlockSpec((tm, tn), lambda i, j, kk: (i, j)) for _ in extras]
    if out_shards > 1:
        oper = (n // out_shards) // tn
        out_specs = [pl.BlockSpec((None, tm, tn), lambda i, j, kk: (j // oper, i, j % oper)) for _ in out_dtypes]
        out_shape = [jax.ShapeDtypeStruct((out_shards, m, n // out_shards), dt) for dt in out_dtypes]
    else:
        out_specs = [pl.BlockSpec((tm, tn), lambda i, j, kk: (i, j)) for _ in out_dtypes]
        out_shape = [jax.ShapeDtypeStruct((m, n), dt) for dt in out_dtypes]
    n_ex = len(extras)
    n_out = len(out_dtypes)

    def finish(acc, ex_refs, out_refs):
        vals = (acc,) if epilogue is None else epilogue(acc, *[r[...] for r in ex_refs])
        for r, v in zip(out_refs, vals):
            r[...] = v.astype(r.dtype)

    def body(a_ref, b_ref, *rest):
        ex_refs = rest[:n_ex]
        out_refs = rest[n_ex:n_ex + n_out]
        prod = _dot(_bf(a_ref[...]), _bf(b_ref[...]), dims)
        if nk == 1:
            finish(prod, ex_refs, out_refs)
            return
        acc_ref = rest[n_ex + n_out]
        kk = pl.program_id(2)

        @pl.when(kk == 0)
        def _():
            acc_ref[...] = prod

        @pl.when((kk > 0) & (kk < nk - 1))
        def _():
            acc_ref[...] += prod

        @pl.when(kk == nk - 1)
        def _():
            finish(acc_ref[...] + prod, ex_refs, out_refs)

    outs = pl.pallas_call(
        body,
        name=name,
        grid=(m // tm, n // tn, nk),
        in_specs=[a_spec, b_spec] + ex_specs,
        out_specs=out_specs,
        out_shape=out_shape,
        scratch_shapes=[pltpu.VMEM((tm, tn), F32)] if nk > 1 else [],
        compiler_params=_params(("parallel", "parallel", "arbitrary")),
    )(a, b, *extras)
    return outs[0] if n_out == 1 else outs


def _ew(name, fn, ins, outs, rows, tm):
    tm = _pick(rows, tm, 16)
    in_specs, args = [], []
    for spec in ins:
        if spec[0] == "tile":
            _, arr, width, c0 = spec
            assert c0 % width == 0
            in_specs.append(pl.BlockSpec((tm, width), functools.partial(lambda i, o: (i, o), o=c0 // width)))
        else:
            arr = spec[1]
            in_specs.append(pl.BlockSpec(arr.shape, lambda i: (0, 0)))
        args.append(arr)
    out_specs, out_shape = [], []
    for kind, dt, width in outs:
        if kind == "tile":
            out_specs.append(pl.BlockSpec((tm, width), lambda i: (i, 0)))
            out_shape.append(jax.ShapeDtypeStruct((rows, width), dt))
        else:
            out_specs.append(pl.BlockSpec((1, width), lambda i: (0, 0)))
            out_shape.append(jax.ShapeDtypeStruct((1, width), dt))
    n_in = len(ins)
    has_acc = any(o[0] == "acc" for o in outs)

    def body(*refs):
        i = pl.program_id(0)
        vals = fn(i * tm, *[r[...] for r in refs[:n_in]])
        for (kind, _, _), r, v in zip(outs, refs[n_in:], vals):
            if kind == "tile":
                r[...] = v.astype(r.dtype)
            else:
                @pl.when(i == 0)
                def _():
                    r[...] = jnp.zeros_like(r)

                r[...] += v.astype(r.dtype)

    res = pl.pallas_call(
        body,
        name=name,
        grid=(rows // tm,),
        in_specs=in_specs,
        out_specs=out_specs,
        out_shape=out_shape,
        compiler_params=_params(("arbitrary",) if has_acc else ("parallel",)),
    )(*args)
    return res[0] if len(outs) == 1 else res


def _row_ids(row0, tm):
    return row0 + lax.broadcasted_iota(jnp.int32, (tm, 1), 0)


def _colsum(v):
    return jnp.sum(v, axis=0, keepdims=True)


def _rms_fwd(name, h, g, t, d):
    def fn(row0, x, gg):
        r = lax.rsqrt(jnp.mean(x * x, axis=-1, keepdims=True) + EPS)
        return (x * r * gg,)

    return _ew(name, fn, [("tile", h, d, 0), ("full", g)], [("tile", BF16, d)], t, 264)


def _rms_bwd(name, h, g, dy, dres, t, d):
    def fn(row0, x, gg, dyv, dr):
        r = lax.rsqrt(jnp.mean(x * x, axis=-1, keepdims=True) + EPS)
        xh = x * r
        dxh = dyv * gg
        dx = r * (dxh - xh * jnp.mean(dxh * xh, axis=-1, keepdims=True))
        out = jnp.where(_row_ids(row0, x.shape[0]) >= PAD, dr + dx, 0.0)
        return out, _colsum(dyv * xh)

    return _ew(name, fn, [("tile", h, d, 0), ("full", g), ("tile", dy, d, 0), ("tile", dres, d, 0)],
               [("tile", F32, d), ("acc", F32, d)], t, 264)


def _loss_head(h, g, target_p, t, d):
    def fn(row0, x, gg, tgt):
        real = _row_ids(row0, x.shape[0]) >= ROW0
        r = lax.rsqrt(jnp.mean(x * x, axis=-1, keepdims=True) + EPS)
        xh = x * r
        err = jnp.where(real, xh * gg - tgt, 0.0)
        loss_rows = 0.5 * jnp.mean(err * err, axis=-1, keepdims=True)
        dyv = err * (1.0 / d)
        dxh = dyv * gg
        dx = r * (dxh - xh * jnp.mean(dxh * xh, axis=-1, keepdims=True))
        loss_part = jnp.sum(loss_rows, axis=0, keepdims=True) * jnp.ones((1, LANES), F32)
        return jnp.where(real, dx, 0.0), _colsum(dyv * xh), loss_part

    return _ew("loss_head", fn, [("tile", h, d, 0), ("full", g), ("tile", target_p, d, 0)],
               [("tile", F32, d), ("acc", F32, d), ("acc", F32, LANES)], t, 264)


def _merge_fwd(a_fox, a_gla, proj, c_gates, t, d):
    def fn(row0, af, ag, gates):
        gates = gates.astype(F32)
        return (_sigmoid(gates[:, :d]) * af + _sigmoid(gates[:, d:]) * ag,)

    return _ew("merge_fwd", fn, [("tile", a_fox, d, 0), ("tile", a_gla, d, 0), ("tile", proj, 2 * d, c_gates)],
               [("tile", BF16, d)], t, 264)


def _merge_bwd(dy, a_fox, a_gla, proj, c_gates, t, d):
    def fn(row0, dyv, af, ag, gates):
        gates = gates.astype(F32)
        sf = _sigmoid(gates[:, :d])
        sg = _sigmoid(gates[:, d:])
        dgates = jnp.concatenate([dyv * af * sf * (1.0 - sf), dyv * ag * sg * (1.0 - sg)], axis=1)
        return dyv * sf, dyv * sg, dgates

    return _ew("merge_bwd", fn,
               [("tile", dy, d, 0), ("tile", a_fox, d, 0), ("tile", a_gla, d, 0), ("tile", proj, 2 * d, c_gates)],
               [("tile", BF16, d), ("tile", BF16, d), ("tile", BF16, 2 * d)], t, 264)


def _fox_gate_fwd(small, b_forget_p, t):
    tb = _pick(t, 384, LANES)

    def body(s_ref, b_ref, c_ref, carry_ref):
        i = pl.program_id(0)

        @pl.when(i == 0)
        def _():
            carry_ref[...] = jnp.zeros_like(carry_ref)

        logf = _log_sigmoid(s_ref[...] + b_ref[...])
        logf = jnp.where(_row_ids(i * tb, tb) >= PAD, logf, 0.0)
        r = lax.broadcasted_iota(jnp.int32, (tb, tb), 0)
        c = lax.broadcasted_iota(jnp.int32, (tb, tb), 1)
        tri = (c <= r).astype(BF16)
        cs = _tri_dot(tri, logf) + carry_ref[...]
        c_ref[...] = cs
        carry_ref[...] = cs[tb - 1:tb, :]

    return pl.pallas_call(
        body, name="fox_gate_fwd", grid=(t // tb,),
        in_specs=[pl.BlockSpec((tb, LANES), lambda i: (i, 0)), pl.BlockSpec((1, LANES), lambda i: (0, 0))],
        out_specs=pl.BlockSpec((tb, LANES), lambda i: (i, 0)),
        out_shape=jax.ShapeDtypeStruct((t, LANES), F32),
        scratch_shapes=[pltpu.VMEM((1, LANES), F32)],
        compiler_params=_params(("arbitrary",)),
    )(small, b_forget_p)


def _fox_gate_bwd(dc, small, b_forget_p, dga, t):
    tb = _pick(t, 384, LANES)
    nb = t // tb

    def body(dc_ref, s_ref, b_ref, dga_ref, ds_ref, db_ref, carry_ref):
        i = pl.program_id(0)

        @pl.when(i == 0)
        def _():
            carry_ref[...] = jnp.zeros_like(carry_ref)
            db_ref[...] = jnp.zeros_like(db_ref)

        r = lax.broadcasted_iota(jnp.int32, (tb, tb), 0)
        c = lax.broadcasted_iota(jnp.int32, (tb, tb), 1)
        tri = (c >= r).astype(BF16)
        dlogf = _tri_dot(tri, dc_ref[...]) + carry_ref[...]
        carry_ref[...] = dlogf[0:1, :]
        z = s_ref[...] + b_ref[...]
        dff = dlogf * _sigmoid(-z)
        lane = lax.broadcasted_iota(jnp.int32, (tb, LANES), 1)
        keep = (_row_ids((nb - 1 - i) * tb, tb) >= PAD) & (lane < FOX_HEADS)
        dff = jnp.where(keep, dff, 0.0)
        ds_ref[...] = dff + dga_ref[...]
        db_ref[...] += _colsum(dff)

    rev = lambda i: (nb - 1 - i, 0)
    return pl.pallas_call(
        body, name="fox_gate_bwd", grid=(nb,),
        in_specs=[pl.BlockSpec((tb, LANES), rev), pl.BlockSpec((tb, LANES), rev),
                  pl.BlockSpec((1, LANES), lambda i: (0, 0)), pl.BlockSpec((tb, LANES), rev)],
        out_specs=[pl.BlockSpec((tb, LANES), rev), pl.BlockSpec((1, LANES), lambda i: (0, 0))],
        out_shape=[jax.ShapeDtypeStruct((t, LANES), F32), jax.ShapeDtypeStruct((1, LANES), F32)],
        scratch_shapes=[pltpu.VMEM((1, LANES), F32)],
        compiler_params=_params(("arbitrary",)),
    )(dc, small, b_forget_p, dga)


def _fox_pairs(nb, by_key):
    if by_key:
        pairs = [(qi, ki) for ki in range(nb) for qi in range(ki, nb)]
    else:
        pairs = [(qi, ki) for qi in range(nb) for ki in range(qi + 1)]
    return (jnp.asarray(np.array([p[0] for p in pairs], np.int32)),
            jnp.asarray(np.array([p[1] for p in pairs], np.int32)), len(pairs))


def _fox_specs(tb, fd, c_fq, c_fkv):
    gw = FOX_GROUP * fd
    q0, kv0 = c_fq // gw, c_fkv // (2 * gw)
    return dict(
        q=pl.BlockSpec((tb, gw), lambda g, p, qt, kt: (qt[p], q0 + g)),
        kv=pl.BlockSpec((tb, 2 * gw), lambda g, p, qt, kt: (kt[p], kv0 + g)),
        col=pl.BlockSpec((FOX_GROUP, tb, 1), lambda g, p, qt, kt: (g, qt[p], 0)),
        row=pl.BlockSpec((FOX_GROUP, 1, tb), lambda g, p, qt, kt: (g, 0, kt[p])),
        head=pl.BlockSpec((tb, gw), lambda g, p, qt, kt: (qt[p], g)),
        key_kv=pl.BlockSpec((tb, 2 * gw), lambda g, p, qt, kt: (kt[p], g)),
    )


def _fox_mask(qi, ki, tb):
    row = qi * tb + lax.broadcasted_iota(jnp.int32, (tb, tb), 0)
    col = ki * tb + lax.broadcasted_iota(jnp.int32, (tb, tb), 1)
    return (col <= row) & (col >= PAD)


def _fox_heads(q_ref, kv_ref, fd):
    return [(q_ref[:, hh * fd:(hh + 1) * fd], kv_ref[:, 2 * hh * fd:(2 * hh + 1) * fd],
             kv_ref[:, (2 * hh + 1) * fd:(2 * hh + 2) * fd]) for hh in range(FOX_GROUP)]


def _fox_fwd(proj, c_col, c_row, t, fd, c_fq, c_fkv):
    tb = _pick(t, 384, LANES)
    nb = t // tb
    scale = fd ** -0.5
    sp = _fox_specs(tb, fd, c_fq, c_fkv)
    qt, kt, npairs = _fox_pairs(nb, by_key=False)

    def body(qt_ref, kt_ref, q_ref, kv_ref, cq_ref, ck_ref, o_ref, lse_ref, m_ref, l_ref, acc_ref):
        p = pl.program_id(1)
        qi, ki = qt_ref[p], kt_ref[p]

        @pl.when(ki == 0)
        def _():
            m_ref[...] = jnp.full_like(m_ref, -jnp.inf)
            l_ref[...] = jnp.zeros_like(l_ref)
            acc_ref[...] = jnp.zeros_like(acc_ref)

        def update(masked):
            mask = _fox_mask(qi, ki, tb) if masked else None
            for hh, (q, k, v) in enumerate(_fox_heads(q_ref, kv_ref, fd)):
                s = _dot(q, k, NT) * scale + cq_ref[hh] - ck_ref[hh]
                if masked:
                    s = jnp.where(mask, s, MASK_VALUE)
                m_prev = m_ref[hh]
                m_new = jnp.maximum(m_prev, jnp.max(s, axis=-1, keepdims=True))
                alpha = jnp.exp(m_prev - m_new)
                pe = jnp.exp(s - m_new)
                l_ref[hh] = alpha * l_ref[hh] + jnp.sum(pe, axis=-1, keepdims=True)
                acc_ref[hh] = alpha * acc_ref[hh] + _dot(pe.astype(BF16), v, NN)
                m_ref[hh] = m_new

        edge = (ki == 0) | (ki == qi)
        pl.when(edge)(functools.partial(update, True))
        pl.when(jnp.logical_not(edge))(functools.partial(update, False))

        @pl.when(ki == qi)
        def _():
            real = _row_ids(qi * tb, tb) >= PAD
            for hh in range(FOX_GROUP):
                o_ref[:, hh * fd:(hh + 1) * fd] = jnp.where(real, acc_ref[hh] / l_ref[hh], 0.0)
                lse_ref[hh] = m_ref[hh] + jnp.log(l_ref[hh])

    return pl.pallas_call(
        body, name="fox_fwd",
        grid_spec=pltpu.PrefetchScalarGridSpec(
            num_scalar_prefetch=2, grid=(FOX_HEADS // FOX_GROUP, npairs),
            in_specs=[sp["q"], sp["kv"], sp["col"], sp["row"]],
            out_specs=[sp["head"], sp["col"]],
            scratch_shapes=[pltpu.VMEM((FOX_GROUP, tb, 1), F32), pltpu.VMEM((FOX_GROUP, tb, 1), F32),
                            pltpu.VMEM((FOX_GROUP, tb, fd), F32)]),
        out_shape=[jax.ShapeDtypeStruct((t, FOX_HEADS * fd), F32), jax.ShapeDtypeStruct((FOX_HEADS, t, 1), F32)],
        compiler_params=_params(("parallel", "arbitrary")),
    )(qt, kt, proj, proj, c_col, c_row)


def _fox_delta(o_fox, do_fox, t, fd):
    tb = _pick(t, 384, LANES)

    def body(o_ref, do_ref, out_ref):
        for h in range(FOX_HEADS):
            sl = slice(h * fd, (h + 1) * fd)
            out_ref[h] = jnp.sum(o_ref[:, sl] * do_ref[:, sl].astype(BF16).astype(F32), axis=-1, keepdims=True)

    w = FOX_HEADS * fd
    return pl.pallas_call(
        body, name="fox_delta", grid=(t // tb,),
        in_specs=[pl.BlockSpec((tb, w), lambda i: (i, 0)), pl.BlockSpec((tb, w), lambda i: (i, 0))],
        out_specs=pl.BlockSpec((FOX_HEADS, tb, 1), lambda i: (0, i, 0)),
        out_shape=jax.ShapeDtypeStruct((FOX_HEADS, t, 1), F32),
        compiler_params=_params(("parallel",)),
    )(o_fox, do_fox)


def _fox_bwd(proj, c_col, c_row, lse, delta, do_fox, t, fd, c_fq, c_fkv):
    tb = _pick(t, 384, LANES)
    nb = t // tb
    scale = fd ** -0.5
    sp = _fox_specs(tb, fd, c_fq, c_fkv)
    qt, kt, npairs = _fox_pairs(nb, by_key=True)
    gw = FOX_GROUP * fd

    def body(qt_ref, kt_ref, q_ref, kv_ref, cq_ref, ck_ref, lse_ref, dl_ref, do_ref, dq_ref, dkv_ref, dc_ref, dr_ref,
             dq_acc, dk_acc, dv_acc, dc_acc, dr_acc):
        p = pl.program_id(1)
        qi, ki = qt_ref[p], kt_ref[p]

        @pl.when(p == 0)
        def _():
            dq_acc[...] = jnp.zeros_like(dq_acc)
            dr_acc[...] = jnp.zeros_like(dr_acc)

        @pl.when(qi == ki)
        def _():
            dk_acc[...] = jnp.zeros_like(dk_acc)
            dv_acc[...] = jnp.zeros_like(dv_acc)
            dc_acc[...] = jnp.zeros_like(dc_acc)

        rows = pl.ds(pl.multiple_of(qi * tb, LANES), tb)

        def update(masked):
            mask = _fox_mask(qi, ki, tb) if masked else None
            for hh, (q, k, v) in enumerate(_fox_heads(q_ref, kv_ref, fd)):
                do = _bf(do_ref[:, hh * fd:(hh + 1) * fd])
                s = _dot(q, k, NT) * scale + cq_ref[hh] - ck_ref[hh]
                if masked:
                    s = jnp.where(mask, s, MASK_VALUE)
                pr = jnp.exp(s - lse_ref[hh])
                dp = _dot(do, v, NT)
                ds = pr * (dp - dl_ref[hh])
                ds16 = ds.astype(BF16)
                dv_acc[hh] += _dot(pr.astype(BF16), do, TN)
                dk_acc[hh] += _dot(ds16, q, TN)
                dc_acc[hh] += _colsum(ds)
                dr_acc[hh, rows, :] += jnp.sum(ds, axis=-1, keepdims=True)
                dq_acc[hh, rows, :] += _dot(ds16, k, NN)

        edge = (ki == 0) | (ki == qi)
        pl.when(edge)(functools.partial(update, True))
        pl.when(jnp.logical_not(edge))(functools.partial(update, False))

        @pl.when(qi == nb - 1)
        def _():
            for hh in range(FOX_GROUP):
                dkv_ref[:, 2 * hh * fd:(2 * hh + 1) * fd] = (dk_acc[hh] * scale).astype(dkv_ref.dtype)
                dkv_ref[:, (2 * hh + 1) * fd:(2 * hh + 2) * fd] = dv_acc[hh].astype(dkv_ref.dtype)
                dc_ref[hh] = -dc_acc[hh]

        @pl.when(p == npairs - 1)
        def _():
            for hh in range(FOX_GROUP):
                dq_ref[:, hh * fd:(hh + 1) * fd] = (dq_acc[hh] * scale).astype(dq_ref.dtype)
            dr_ref[...] = dr_acc[...]

    return pl.pallas_call(
        body, name="fox_bwd",
        grid_spec=pltpu.PrefetchScalarGridSpec(
            num_scalar_prefetch=2, grid=(FOX_HEADS // FOX_GROUP, npairs),
            in_specs=[sp["q"], sp["kv"], sp["col"], sp["row"], sp["col"], sp["col"], sp["head"]],
            out_specs=[pl.BlockSpec((t, gw), lambda g, p, qt, kt: (0, g)), sp["key_kv"], sp["row"],
                       pl.BlockSpec((FOX_GROUP, t, 1), lambda g, p, qt, kt: (g, 0, 0))],
            scratch_shapes=[pltpu.VMEM((FOX_GROUP, t, fd), F32), pltpu.VMEM((FOX_GROUP, tb, fd), F32),
                            pltpu.VMEM((FOX_GROUP, tb, fd), F32), pltpu.VMEM((FOX_GROUP, 1, tb), F32),
                            pltpu.VMEM((FOX_GROUP, t, 1), F32)]),
        out_shape=[jax.ShapeDtypeStruct((t, FOX_HEADS * fd), BF16), jax.ShapeDtypeStruct((t, 2 * FOX_HEADS * fd), BF16),
                   jax.ShapeDtypeStruct((FOX_HEADS, 1, t), F32), jax.ShapeDtypeStruct((FOX_HEADS, t, 1), F32)],
        compiler_params=_params(("parallel", "arbitrary")),
    )(qt, kt, proj, proj, c_col, c_row, lse, delta, do_fox)


def _gla_gate_fwd(small, w_alpha_p, b_alpha, t, gk):
    def fn(row0, s, w, b):
        z = _dot(s.astype(BF16), w, NN) + b
        return (jnp.where(_row_ids(row0, s.shape[0]) >= PAD, _log_sigmoid(z) * (1.0 / GLA_TAU), 0.0),)

    return _ew("gla_gate_fwd", fn, [("tile", small, LANES, 0), ("full", w_alpha_p), ("full", b_alpha)],
               [("tile", F32, gk)], t, 264)


def _gla_gate_bwd(dglog, small, w_alpha_p, b_alpha, t, gk):
    def fn(row0, dg, s, w, b):
        z = _dot(s.astype(BF16), w, NN) + b
        dz = jnp.where(_row_ids(row0, s.shape[0]) >= PAD, dg * (1.0 / GLA_TAU) * _sigmoid(-z), 0.0)
        return dz, _colsum(dz)

    return _ew("gla_gate_bwd", fn,
               [("tile", dglog, gk, 0), ("tile", small, LANES, 0), ("full", w_alpha_p), ("full", b_alpha)],
               [("tile", BF16, gk), ("acc", F32, gk)], t, 264)


def _gla_chunk(q, k, g, scale, cs):
    r = lax.broadcasted_iota(jnp.int32, (cs, cs), 0)
    c = lax.broadcasted_iota(jnp.int32, (cs, cs), 1)
    causal = c <= r
    b = _tri_dot(causal.astype(BF16), g)
    bl = b[cs - 1:cs, :]
    eb, einv, eend = jnp.exp(b), jnp.exp(-b), jnp.exp(bl - b)
    qd = q.astype(F32) * scale * eb
    kf = k.astype(F32)
    return causal, (eb, einv, eend), bl, qd, kf * einv, kf * eend


def _gla_fwd(proj, glog, t, dk, dv, c_q, c_k, c_v):
    cs = GLA_CHUNK
    nc = t // cs
    wk, wv = GLA_HEADS * dk, GLA_HEADS * dv
    scale = dk ** -0.5

    def body(q_ref, k_ref, v_ref, g_ref, o_ref, sp_ref, st_ref):
        @pl.when(pl.program_id(0) == 0)
        def _():
            st_ref[...] = jnp.zeros_like(st_ref)

        for h in range(GLA_HEADS):
            ks, vs = slice(h * dk, (h + 1) * dk), slice(h * dv, (h + 1) * dv)
            v = v_ref[:, vs]
            causal, _, bl, qd, ki, ke = _gla_chunk(q_ref[:, ks], k_ref[:, ks], g_ref[:, ks], scale, cs)
            st = st_ref[h]
            sp_ref[h] = st
            a = jnp.where(causal, _dot(qd.astype(BF16), ki.astype(BF16), NT), 0.0)
            o_ref[:, vs] = _dot(a.astype(BF16), v, NN) + _dot(qd.astype(BF16), st.astype(BF16), NT)
            st_ref[h] = st * jnp.exp(bl) + _dot(v, ke.astype(BF16), TN)

    return pl.pallas_call(
        body, name="gla_fwd", grid=(nc,),
        in_specs=[pl.BlockSpec((cs, wk), lambda n: (n, c_q // wk)), pl.BlockSpec((cs, wk), lambda n: (n, c_k // wk)),
                  pl.BlockSpec((cs, wv), lambda n: (n, c_v // wv)), pl.BlockSpec((cs, wk), lambda n: (n, 0))],
        out_specs=[pl.BlockSpec((cs, wv), lambda n: (n, 0)),
                   pl.BlockSpec((None, GLA_HEADS, dv, dk), lambda n: (n, 0, 0, 0))],
        out_shape=[jax.ShapeDtypeStruct((t, wv), F32), jax.ShapeDtypeStruct((nc, GLA_HEADS, dv, dk), F32)],
        scratch_shapes=[pltpu.VMEM((GLA_HEADS, dv, dk), F32)],
        compiler_params=_params(("arbitrary",)),
    )(proj, proj, proj, glog)


def _gla_bwd(proj, glog, s_prev, do_raw, t, dk, dv, c_q, c_k, c_v):
    cs = GLA_CHUNK
    nc = t // cs
    wk, wv = GLA_HEADS * dk, GLA_HEADS * dv
    scale = dk ** -0.5

    def body(q_ref, k_ref, v_ref, g_ref, sp_ref, do_ref, dq_ref, dk_ref, dv_ref, dg_ref, dst_ref):
        @pl.when(pl.program_id(0) == 0)
        def _():
            dst_ref[...] = jnp.zeros_like(dst_ref)

        for h in range(GLA_HEADS):
            ks, vs = slice(h * dk, (h + 1) * dk), slice(h * dv, (h + 1) * dv)
            v = v_ref[:, vs]
            do = do_ref[:, vs].astype(BF16)
            causal, (eb, einv, eend), bl, qd, ki, ke = _gla_chunk(q_ref[:, ks], k_ref[:, ks], g_ref[:, ks], scale, cs)
            qd16, ki16, ke16 = qd.astype(BF16), ki.astype(BF16), ke.astype(BF16)
            st = sp_ref[h]
            dst = dst_ref[h]
            dst16 = dst.astype(BF16)
            a = jnp.where(causal, _dot(qd16, ki16, NT), 0.0).astype(BF16)
            da = jnp.where(causal, _dot(do, v, NT), 0.0).astype(BF16)
            dvv = _dot(a, do, TN) + _dot(ke16, dst16, NT)
            dqd = _dot(da, ki16, NN) + _dot(do, st.astype(BF16), NN)
            dki = _dot(da, qd16, TN)
            dke = _dot(v, dst16, NN)
            dl = jnp.exp(bl)
            ddl = _colsum(dst * st)
            dst_ref[h] = dst * dl + _dot(do, qd16, TN)
            dq_ref[:, ks] = (dqd * eb * scale).astype(dq_ref.dtype)
            dk_ref[:, ks] = (dki * einv + dke * eend).astype(dk_ref.dtype)
            dv_ref[:, vs] = dvv.astype(dv_ref.dtype)
            db = dqd * qd - dki * ki - dke * ke
            db_last = _colsum(dke * ke) + ddl * dl
            r = lax.broadcasted_iota(jnp.int32, (cs, cs), 0)
            c = lax.broadcasted_iota(jnp.int32, (cs, cs), 1)
            dg_ref[:, ks] = _tri_dot((c >= r).astype(BF16), db) + db_last

    rev = lambda f: (lambda n: f(nc - 1 - n))
    return pl.pallas_call(
        body, name="gla_bwd", grid=(nc,),
        in_specs=[pl.BlockSpec((cs, wk), rev(lambda n: (n, c_q // wk))), pl.BlockSpec((cs, wk), rev(lambda n: (n, c_k // wk))),
                  pl.BlockSpec((cs, wv), rev(lambda n: (n, c_v // wv))), pl.BlockSpec((cs, wk), rev(lambda n: (n, 0))),
                  pl.BlockSpec((None, GLA_HEADS, dv, dk), rev(lambda n: (n, 0, 0, 0))),
                  pl.BlockSpec((cs, wv), rev(lambda n: (n, 0)))],
        out_specs=[pl.BlockSpec((cs, wk), rev(lambda n: (n, 0))), pl.BlockSpec((cs, wk), rev(lambda n: (n, 0))),
                   pl.BlockSpec((cs, wv), rev(lambda n: (n, 0))), pl.BlockSpec((cs, wk), rev(lambda n: (n, 0)))],
        out_shape=[jax.ShapeDtypeStruct((t, wk), BF16), jax.ShapeDtypeStruct((t, wk), BF16),
                   jax.ShapeDtypeStruct((t, wv), BF16), jax.ShapeDtypeStruct((t, wk), F32)],
        scratch_shapes=[pltpu.VMEM((GLA_HEADS, dv, dk), F32)],
        compiler_params=_params(("arbitrary",)),
    )(proj, proj, proj, glog, s_prev, do_raw)


def _gla_post_fwd(o_raw, proj, gn, t, dv, c_gr):
    w = GLA_HEADS * dv

    def fn(row0, o, gr, g):
        gr = gr.astype(F32)
        outs = []
        for h in range(GLA_HEADS):
            oh = o[:, h * dv:(h + 1) * dv]
            outs.append(oh * lax.rsqrt(jnp.mean(oh * oh, axis=-1, keepdims=True) + EPS))
        on = jnp.concatenate(outs, axis=1) * g
        return (on * (gr * _sigmoid(gr)),)

    return _ew("gla_post_fwd", fn, [("tile", o_raw, w, 0), ("tile", proj, w, c_gr), ("full", gn)],
               [("tile", BF16, w)], t, 264)


def _gla_post_bwd(o_raw, proj, gn, do_gla, t, dv, c_gr):
    w = GLA_HEADS * dv

    def fn(row0, o, gr, g, do):
        gr = gr.astype(F32)
        sg = _sigmoid(gr)
        don = do * (gr * sg)
        ohs, dos = [], []
        for h in range(GLA_HEADS):
            sl = slice(h * dv, (h + 1) * dv)
            oh = o[:, sl]
            r = lax.rsqrt(jnp.mean(oh * oh, axis=-1, keepdims=True) + EPS)
            xh = oh * r
            dxh = don[:, sl] * g[:, sl]
            ohs.append(xh)
            dos.append(r * (dxh - xh * jnp.mean(dxh * xh, axis=-1, keepdims=True)))
        xh = jnp.concatenate(ohs, axis=1)
        dgr = do * (xh * g) * (sg * (1.0 + gr * (1.0 - sg)))
        return jnp.concatenate(dos, axis=1), dgr, _colsum(don * xh)

    return _ew("gla_post_bwd", fn,
               [("tile", o_raw, w, 0), ("tile", proj, w, c_gr), ("full", gn), ("tile", do_gla, w, 0)],
               [("tile", F32, w), ("tile", BF16, w), ("acc", F32, w)], t, 264)


def _adamw(name, w, g, m, v, layer=None, into=None):
    nl, rows, cols = w.shape
    tm = _pick(rows, max(8, (512 * 1024) // max(cols, 1) // 8 * 8), 8)

    def body(w_ref, g_ref, m_ref, v_ref, *rest):
        go_ref, d_ref, nm_ref, nv_ref = rest[-4:]
        gg = g_ref[...]
        nm = ADAM_B1 * m_ref[...] + (1.0 - ADAM_B1) * gg
        nv = ADAM_B2 * v_ref[...] + (1.0 - ADAM_B2) * (gg * gg)
        m_hat = nm / (1.0 - ADAM_B1 ** ADAM_STEP)
        v_hat = nv / (1.0 - ADAM_B2 ** ADAM_STEP)
        go_ref[...] = gg
        d_ref[...] = -ADAM_LR * (m_hat / (jnp.sqrt(v_hat) + ADAM_EPS) + ADAM_WD * w_ref[...])
        nm_ref[...] = nm
        nv_ref[...] = nv

    out_shape = [jax.ShapeDtypeStruct((nl, rows, cols), F32)] * 4
    if layer is None:
        spec = pl.BlockSpec((None, tm, cols), lambda l, i: (l, i, 0))
        return pl.pallas_call(
            body, name=name, grid=(nl, rows // tm), in_specs=[spec] * 4, out_specs=[spec] * 4, out_shape=out_shape,
            compiler_params=_params(("parallel", "parallel")),
        )(w, g, m, v)
    spec = pl.BlockSpec((None, tm, cols), lambda i: (layer, i, 0))
    in_specs = [spec, pl.BlockSpec((tm, cols), lambda i: (i, 0)), spec, spec]
    args, aliases = [w, g, m, v], {}
    if into is not None:
        in_specs += [pl.BlockSpec(memory_space=pl.ANY)] * 4
        args += list(into)
        aliases = {4 + k: k for k in range(4)}
    return pl.pallas_call(
        body, name=name, grid=(rows // tm,), in_specs=in_specs, out_specs=[spec] * 4, out_shape=out_shape,
        input_output_aliases=aliases, compiler_params=_params(("parallel",)),
    )(*args)


def _me():
    return lax.axis_index("x"), lax.axis_index("y"), lax.axis_index("c")


def _hbm_specs(n):
    return [pl.BlockSpec(memory_space=pl.ANY)] * n


_HBM = pl.BlockSpec(memory_space=pltpu.HBM)
_SEM = pl.BlockSpec(memory_space=pltpu.SEMAPHORE)
_EFFECT = pltpu.SideEffectType.DATAFLOW_SIDE_EFFECTING
N_PEERS = N_CHIPS - 1


def _other_chips(x, y):
    return [(1 - x, y), (x, 1 - y), (1 - x, 1 - y)]


def _ici_copies(plan, src, land, send_sems, recv_sems):
    x, y, c = _me()
    copies = []
    for i in range(len(src)):
        for j, chip in enumerate(_other_chips(x, y)):
            s, d = plan(i, src[i], land[i], j, chip, (x, y, c))
            copies.append(pltpu.make_async_remote_copy(
                src_ref=s, dst_ref=d, send_sem=send_sems.at[N_PEERS * i + j], recv_sem=recv_sems.at[N_PEERS * i + j],
                device_id=(*chip, c), device_id_type=MESH))
    return copies


def _ici_start(name, srcs, lands, plan, after=None):
    n = len(srcs)
    extra = [] if after is None else [after]

    def body(*refs):
        src, land = refs[:n], refs[n:2 * n]
        send_sems, recv_sems = refs[2 * n + len(extra)], refs[2 * n + len(extra) + 1]
        token = refs[-1]
        for cp in _ici_copies(plan, src, land, send_sems, recv_sems):
            cp.start()
        token[...] = jnp.zeros_like(token)

    out_shape = ([pltpu.SemaphoreType.DMA((N_PEERS * n,)), pltpu.SemaphoreType.DMA((N_PEERS * n,))]
                 + [pltpu.HBM(a.shape, a.dtype) for a in list(srcs) + list(lands)]
                 + [jax.ShapeDtypeStruct((8, LANES), F32)])
    res = pl.pallas_call(
        body, name=name, out_shape=out_shape,
        in_specs=[_HBM] * (2 * n) + [pl.BlockSpec(memory_space=pl.ANY)] * len(extra),
        out_specs=[_SEM, _SEM] + [_HBM] * (2 * n) + [pl.BlockSpec(memory_space=pltpu.VMEM)],
        input_output_aliases={i: 2 + i for i in range(2 * n)},
        compiler_params=pltpu.CompilerParams(has_side_effects=_EFFECT),
    )(*[pltpu.with_memory_space_constraint(a, pltpu.HBM) for a in list(srcs) + list(lands)], *extra)
    return res[0], res[1], res[2:2 + n], res[2 + n:2 + 2 * n], res[-1]


def _ici_wait(name, send_sems, recv_sems, srcs, lands, plan, after):
    n = len(srcs)

    def body(*refs):
        src, land = refs[:n], refs[n:2 * n]
        s_sems, r_sems = refs[2 * n], refs[2 * n + 1]
        for cp in _ici_copies(plan, src, land, s_sems, r_sems):
            cp.wait_send()
            cp.wait_recv()

    res = pl.pallas_call(
        body, name=name, out_shape=[pltpu.HBM(a.shape, a.dtype) for a in list(srcs) + list(lands)],
        in_specs=[_HBM] * (2 * n) + [_SEM, _SEM, pl.BlockSpec(memory_space=pl.ANY)], out_specs=[_HBM] * (2 * n),
        input_output_aliases={i: i for i in range(2 * n)},
        compiler_params=pltpu.CompilerParams(has_side_effects=_EFFECT),
    )(*srcs, *lands, send_sems, recv_sems, after)
    return res[:n], res[n:]


def _half(ref_rows, c):
    half = ref_rows // 2
    return pl.ds(c * half, half)


def _gather_plan(i, src, land, j, chip, me):
    x, y, c = me
    rows = _half(src.shape[0], c)
    return src.at[rows], land.at[2 * x + y, rows]


def _scatter_plan(i, src, land, j, chip, me):
    x, y, _ = me
    return src.at[2 * chip[0] + chip[1]], land.at[2 * x + y]


def _forward_halves(name, gathered):
    n = len(gathered)

    def body(*refs):
        buf = refs[n:2 * n]
        send_sems, recv_sems = refs[2 * n:]
        x, y, c = _me()
        copies = []
        for i in range(n):
            rows = _half(buf[i].shape[1], c)
            for j, (px, py) in enumerate(_other_chips(x, y)):
                cp = pltpu.make_async_remote_copy(
                    src_ref=buf[i].at[2 * px + py, rows], dst_ref=buf[i].at[2 * px + py, rows],
                    send_sem=send_sems.at[N_PEERS * i + j], recv_sem=recv_sems.at[N_PEERS * i + j],
                    device_id=(x, y, 1 - c), device_id_type=MESH)
                cp.start()
                copies.append(cp)
        for cp in copies:
            cp.wait_recv()
        for cp in copies:
            cp.wait_send()

    return pl.pallas_call(
        body, name=name, in_specs=_hbm_specs(n), out_specs=_hbm_specs(n),
        out_shape=[jax.ShapeDtypeStruct(g.shape, g.dtype) for g in gathered],
        input_output_aliases={i: i for i in range(n)},
        scratch_shapes=[pltpu.SemaphoreType.DMA((N_PEERS * n,)), pltpu.SemaphoreType.DMA((N_PEERS * n,))],
    )(*gathered)


def _all_gather_small(v):
    def body(v_ref, out_ref, send_sems, recv_sems, local_sem):
        x, y, c = _me()
        mine = pltpu.make_async_copy(v_ref, out_ref.at[4 * x + 2 * y + c], local_sem)
        mine.start()
        copies = []
        for k in range(1, N_DEV):
            peer = (x ^ ((k >> 2) & 1), y ^ ((k >> 1) & 1), c ^ (k & 1))
            cp = pltpu.make_async_remote_copy(
                src_ref=v_ref, dst_ref=out_ref.at[4 * x + 2 * y + c], send_sem=send_sems.at[k - 1],
                recv_sem=recv_sems.at[k - 1], device_id=peer, device_id_type=MESH)
            cp.start()
            copies.append(cp)
        for cp in copies:
            cp.wait_recv()
        for cp in copies:
            cp.wait_send()
        mine.wait()

    return pl.pallas_call(
        body, name="all_gather_small", in_specs=_hbm_specs(1), out_specs=pl.BlockSpec(memory_space=pl.ANY),
        out_shape=jax.ShapeDtypeStruct((N_DEV,) + v.shape, v.dtype),
        scratch_shapes=[pltpu.SemaphoreType.DMA((N_DEV - 1,)), pltpu.SemaphoreType.DMA((N_DEV - 1,)),
                        pltpu.SemaphoreType.DMA],
    )(v)


def _swap_halves(name, parts):
    n = len(parts)

    def body(*refs):
        src, dst = refs[:n], refs[n:2 * n]
        send_sems, recv_sems = refs[2 * n:]
        x, y, c = _me()
        copies = []
        for i in range(n):
            cp = pltpu.make_async_remote_copy(
                src_ref=src[i].at[:, _half(src[i].shape[1], 1 - c)], dst_ref=dst[i], send_sem=send_sems.at[i],
                recv_sem=recv_sems.at[i], device_id=(x, y, 1 - c), device_id_type=MESH)
            cp.start()
            copies.append(cp)
        for cp in copies:
            cp.wait_recv()
        for cp in copies:
            cp.wait_send()

    out_shape = [jax.ShapeDtypeStruct((g.shape[0], g.shape[1] // 2, g.shape[2]), g.dtype) for g in parts]
    return pl.pallas_call(
        body, name=name, in_specs=_hbm_specs(n), out_specs=_hbm_specs(n), out_shape=out_shape,
        scratch_shapes=[pltpu.SemaphoreType.DMA((n,)), pltpu.SemaphoreType.DMA((n,))],
    )(*parts)


def _share_halves(name, grads):
    n = len(grads)

    def body(*refs):
        buf = refs[n:2 * n]
        send_sems, recv_sems = refs[2 * n:]
        x, y, c = _me()
        copies = []
        for i in range(n):
            rows = _half(buf[i].shape[0], c)
            cp = pltpu.make_async_remote_copy(
                src_ref=buf[i].at[rows], dst_ref=buf[i].at[rows], send_sem=send_sems.at[i],
                recv_sem=recv_sems.at[i], device_id=(x, y, 1 - c), device_id_type=MESH)
            cp.start()
            copies.append(cp)
        for cp in copies:
            cp.wait_recv()
        for cp in copies:
            cp.wait_send()

    out_shape = [jax.ShapeDtypeStruct(s.shape, s.dtype) for s in grads]
    return pl.pallas_call(
        body, name=name, in_specs=_hbm_specs(n), out_specs=_hbm_specs(n), out_shape=out_shape,
        input_output_aliases={i: i for i in range(n)},
        scratch_shapes=[pltpu.SemaphoreType.DMA((n,)), pltpu.SemaphoreType.DMA((n,))],
    )(*grads)


def _pair_sum(name, own, recv, core):
    nch, half, cdim = recv.shape
    tm = _pick(half, max(8, (512 * 1024) // cdim // 16 * 16), 16)
    nt = half // tm

    def body(c_ref, a_ref, b_ref, s_ref, s16_ref):
        s = a_ref[...] + b_ref[...]
        s_ref[...] = s
        s16_ref[...] = s.astype(BF16)

    spec = pl.BlockSpec((None, tm, cdim), lambda j, i, c: (j, i, 0))
    return pl.pallas_call(
        body, name=name,
        grid_spec=pltpu.PrefetchScalarGridSpec(
            num_scalar_prefetch=1, grid=(nch, nt),
            in_specs=[pl.BlockSpec((None, tm, cdim), lambda j, i, c: (j, c[0] * nt + i, 0)), spec],
            out_specs=[spec, spec]),
        out_shape=[jax.ShapeDtypeStruct((nch, half, cdim), F32), jax.ShapeDtypeStruct((nch, half, cdim), BF16)],
        compiler_params=_params(("parallel", "parallel")),
    )(core, own, recv)


def _chip_sum(name, own, recv, chip, core):
    nch, half, cdim = own.shape
    tm = _pick(half, max(8, (512 * 1024) // cdim // 16 * 16), 16)
    nt = half // tm

    def body(c_ref, k_ref, own_ref, *rest):
        recv_refs, out_ref = rest[:nch], rest[-1]
        me = c_ref[0]
        acc = None
        for j in range(nch):
            term = jnp.where(me == j, own_ref[...], recv_refs[j][...].astype(F32))
            acc = term if acc is None else acc + term
        out_ref[...] = acc

    recv_specs = [pl.BlockSpec((None, tm, cdim), functools.partial(lambda i, c, k, j: (j, i, 0), j=j))
                  for j in range(nch)]
    return pl.pallas_call(
        body, name=name,
        grid_spec=pltpu.PrefetchScalarGridSpec(
            num_scalar_prefetch=2, grid=(nt,),
            in_specs=[pl.BlockSpec((None, tm, cdim), lambda i, c, k: (c[0], i, 0))] + recv_specs,
            out_specs=pl.BlockSpec((tm, cdim), lambda i, c, k: (k[0] * nt + i, 0))),
        out_shape=jax.ShapeDtypeStruct((2 * half, cdim), F32),
        compiler_params=_params(("parallel",)),
    )(chip, core, own, *([recv] * nch))


def _sum_devices(gathered):
    _, r, cdim = gathered.shape

    def body(g_ref, o_ref):
        acc = g_ref[0]
        for k in range(1, N_DEV):
            acc = acc + g_ref[k]
        o_ref[...] = acc

    return pl.pallas_call(
        body, name="sum_devices", out_shape=jax.ShapeDtypeStruct((r, cdim), F32),
        compiler_params=_params(),
    )(gathered)


class _Layout:
    def __init__(self, d):
        self.d = d
        self.fw = d // 2
        self.fd = self.fw // FOX_HEADS
        self.gk = d // 2
        self.gv = d
        self.dk = self.gk // GLA_HEADS
        self.dv = self.gv // GLA_HEADS
        self.c_fq = 0
        self.c_gq = self.fw
        self.c_gv = self.c_gq + self.gk
        self.c_gr = self.c_gv + self.gv
        self.c_fkv = self.c_gr + self.gv
        self.c_gates = self.c_fkv + 2 * self.fw
        self.c_gk = self.c_gates + 2 * d
        self.c_small = self.c_gk + self.gk
        self.n_main = self.c_small
        self.n_p = self.c_small + LANES
        self.o_fk = self.fw
        self.o_fv = 2 * self.fw
        self.o_ff = 3 * self.fw
        self.o_gq = self.o_ff + FOX_HEADS
        self.o_gk = self.o_gq + self.gk
        self.o_gv = self.o_gk + self.gk
        self.o_gr = self.o_gv + self.gv
        self.o_ga = self.o_gr + self.gv
        self.o_gf = self.o_ga + GLA_RANK
        self.o_gg = self.o_gf + d
        self.n_orig = self.o_gg + d

    def to_p(self, shards):
        per = self.n_orig // N_CHIPS
        ranges = [(0, self.fw), (self.o_gq, self.gk), (self.o_gv, self.gv), (self.o_gr, self.gv)]
        for h in range(FOX_HEADS):
            ranges += [(self.o_fk + h * self.fd, self.fd), (self.o_fv + h * self.fd, self.fd)]
        ranges += [(self.o_gf, 2 * self.d), (self.o_gk, self.gk), (self.o_ff, FOX_HEADS), (self.o_ga, GLA_RANK)]
        pieces = []
        for a, width in ranges:
            for j in range(a // per, (a + width - 1) // per + 1):
                lo, hi = max(a, j * per), min(a + width, (j + 1) * per)
                pieces.append(shards[j][:, lo - j * per:hi - j * per])
        pieces.append(jnp.zeros((shards.shape[1], LANES - FOX_HEADS - GLA_RANK), shards.dtype))
        return jnp.concatenate(pieces, axis=1)

    def from_segments(self, seg):
        per = self.n_orig // N_CHIPS
        fd = self.fd
        atoms = [("fq", 0, self.fw)]
        atoms += [("fkv", 2 * h * fd, fd) for h in range(FOX_HEADS)]
        atoms += [("fkv", (2 * h + 1) * fd, fd) for h in range(FOX_HEADS)]
        atoms += [("small", 0, FOX_HEADS), ("gq", 0, self.gk), ("gk", 0, self.gk), ("gv", 0, self.gv),
                  ("gr", 0, self.gv), ("small", FOX_HEADS, GLA_RANK), ("gates", 0, 2 * self.d)]
        shards = [[] for _ in range(N_CHIPS)]
        pos = 0
        for name, c0, width in atoms:
            for j in range(pos // per, (pos + width - 1) // per + 1):
                lo, hi = max(pos, j * per), min(pos + width, (j + 1) * per)
                shards[j].append(seg[name][:, c0 + lo - pos:c0 + hi - pos])
            pos += width
        assert pos == self.n_orig
        return jnp.stack([jnp.concatenate(s, axis=1) for s in shards])


def _layer_fwd(lay, h, p, t, late=None):
    d = lay.d
    xn = _rms_fwd("rms_mix_fwd", h, p["norm_mix_g"], t, d)
    proj = _mm("mm_proj", xn, p["w_in"], mode="nn", m=t, n=lay.n_main, k=d, out_dtypes=(BF16,))
    small = _mm("mm_small", xn, p["w_in"], mode="nn", m=t, n=LANES, k=d, b_c0=lay.c_small)
    cs = _fox_gate_fwd(small, p["b_forget_p"], t)
    ct = cs[:, :FOX_HEADS].T
    c_col, c_row = ct[:, :, None], ct[:, None, :]
    o_fox, lse = _fox_fwd(proj, c_col, c_row, t, lay.fd, lay.c_fq, lay.c_fkv)
    glog = _gla_gate_fwd(small, p["w_alpha_p"], p["b_alpha"], t, lay.gk)
    o_raw, s_prev = _gla_fwd(proj, glog, t, lay.dk, lay.dv, lay.c_gq, lay.c_gk, lay.c_gv)
    o_gla = _gla_post_fwd(o_raw, proj, p["gla_norm_g"], t, lay.dv, lay.c_gr)
    if late is not None:
        p.update(late(o_gla))
    a_fox = _mm("mm_o_fox", o_fox, p["w_o_fox"], mode="nn", m=t, n=d, k=lay.fw, b_shards=N_CHIPS)
    a_gla = _mm("mm_o_gla", o_gla, p["w_o_gla"], mode="nn", m=t, n=d, k=lay.gv)
    y = _merge_fwd(a_fox, a_gla, proj, lay.c_gates, t, d)
    h1 = _mm("mm_out", y, p["w_out"], mode="nn", m=t, n=d, k=d, extras=[h], epilogue=lambda acc, res: (res + acc,))
    xn2 = _rms_fwd("rms_mlp_fwd", h1, p["norm_mlp_g"], t, d)
    u, act = _mm("mm_ff1", xn2, p["w_ff1"], mode="nn", m=t, n=4 * d, k=d, out_dtypes=(BF16, BF16), b_shards=N_CHIPS,
                 epilogue=lambda acc: (acc, jnp.square(jnp.maximum(acc, 0.0))))
    h2 = _mm("mm_ff2", act, p["w_ff2"], mode="nn", m=t, n=d, k=4 * d, extras=[h1],
             epilogue=lambda acc, res: (res + acc,))
    saved = dict(h=h, xn=xn, proj=proj, small=small, c_col=c_col, c_row=c_row, o_fox=o_fox, lse=lse, glog=glog,
                 o_raw=o_raw, s_prev=s_prev, o_gla=o_gla, a_fox=a_fox, a_gla=a_gla, y=y, h1=h1, xn2=xn2, u=u, act=act)
    return h2, saved


def _layer_bwd(lay, dh2, p, s, t, gates=None):
    d = lay.d
    g = {}

    def gated(gain, point):
        return gain + gates[point](g) if gates and point in gates else gain
    du = _mm("mm_dact", dh2, p["w_ff2"], mode="nt", m=t, n=4 * d, k=d, extras=[s["u"]], out_dtypes=(BF16,),
             epilogue=lambda acc, u: (acc * (2.0 * jnp.maximum(u.astype(F32), 0.0)),))
    g["w_ff2"] = _mm("mm_dw_ff2", s["act"], dh2, mode="tn", m=4 * d, n=d, k=t)
    g["w_ff1"] = _mm("mm_dw_ff1", s["xn2"], du, mode="tn", m=d, n=4 * d, k=t, out_shards=N_CHIPS)
    dxn2 = _mm("mm_dxn2", du, p["w_ff1"], mode="nt", m=t, n=d, k=4 * d, b_shards=N_CHIPS)
    dh1, g["norm_mlp_g"] = _rms_bwd("rms_mlp_bwd", s["h1"], gated(p["norm_mlp_g"], "mlp"), dxn2, dh2, t, d)
    dy = _mm("mm_dy", dh1, p["w_out"], mode="nt", m=t, n=d, k=d)
    g["w_out"] = _mm("mm_dw_out", s["y"], dh1, mode="tn", m=d, n=d, k=t)
    da_fox, da_gla, dgates = _merge_bwd(dy, s["a_fox"], s["a_gla"], s["proj"], lay.c_gates, t, d)
    g["w_o_fox"] = _mm("mm_dw_o_fox", s["o_fox"], da_fox, mode="tn", m=lay.fw, n=d, k=t, out_shards=N_CHIPS)
    do_fox = _mm("mm_do_fox", da_fox, p["w_o_fox"], mode="nt", m=t, n=lay.fw, k=d, b_shards=N_CHIPS)
    g["w_o_gla"] = _mm("mm_dw_o_gla", s["o_gla"], da_gla, mode="tn", m=lay.gv, n=d, k=t)
    do_gla = _mm("mm_do_gla", da_gla, p["w_o_gla"], mode="nt", m=t, n=lay.gv, k=d)
    do_raw, dgr, g["gla_norm_g"] = _gla_post_bwd(s["o_raw"], s["proj"], gated(p["gla_norm_g"], "out"), do_gla, t,
                                                 lay.dv, lay.c_gr)
    dgq, dgk, dgv, dglog = _gla_bwd(s["proj"], s["glog"], s["s_prev"], do_raw, t, lay.dk, lay.dv,
                                    lay.c_gq, lay.c_gk, lay.c_gv)
    dz, g["b_alpha"] = _gla_gate_bwd(dglog, s["small"], p["w_alpha_p"], p["b_alpha"], t, lay.gk)
    g["w_alpha_p"] = _mm("mm_dw_alpha", s["small"], dz, mode="tn", m=LANES, n=lay.gk, k=t)
    dga = _mm("mm_dga", dz, p["w_alpha_p"], mode="nt", m=t, n=LANES, k=lay.gk)
    delta = _fox_delta(s["o_fox"], do_fox, t, lay.fd)
    dfq, dfkv, dc, dr = _fox_bwd(s["proj"], s["c_col"], s["c_row"], s["lse"], delta, do_fox, t, lay.fd,
                                 lay.c_fq, lay.c_fkv)
    dc_p = jnp.pad((dc[:, 0, :] + dr[:, :, 0]).T, ((0, 0), (0, LANES - FOX_HEADS)))
    dsmall, g["b_forget_p"] = _fox_gate_bwd(dc_p, s["small"], p["b_forget_p"], dga, t)
    segs = [("fq", dfq, lay.c_fq), ("gq", dgq, lay.c_gq), ("gv", dgv, lay.c_gv), ("gr", dgr, lay.c_gr),
            ("fkv", dfkv, lay.c_fkv), ("gates", dgates, lay.c_gates), ("gk", dgk, lay.c_gk),
            ("small", dsmall, lay.c_small)]
    dxn = None
    dw_in = {}
    for nm, dseg, c0 in segs:
        width = dseg.shape[1]
        dw_in[nm] = _mm("mm_dw_in_" + nm, s["xn"], dseg, mode="tn", m=d, n=width, k=t)
        if dxn is None:
            dxn = _mm("mm_dxn_" + nm, dseg, p["w_in"], mode="nt", m=t, n=d, k=width, b_c0=c0)
        else:
            dxn = _mm("mm_dxn_" + nm, dseg, p["w_in"], mode="nt", m=t, n=d, k=width, b_c0=c0, extras=[dxn],
                      epilogue=lambda acc, prev: (prev + acc,))
    g["w_in"] = lay.from_segments(dw_in)
    dh, g["norm_mix_g"] = _rms_bwd("rms_mix_bwd", s["h"], gated(p["norm_mix_g"], "in"), dxn, dh1, t, d)
    return dh, g


def _sequence_step(x, target, meta, layers, final_g):
    seq, d = x.shape
    t = seq + ROW0
    lay = _Layout(d)
    h = jnp.pad(x, ((ROW0, 0), (0, 0))).at[PAD:ROW0].set(meta)
    target_p = jnp.pad(target, ((ROW0, 0), (0, 0)))
    saved = []
    for p in layers:
        h, s = _layer_fwd(lay, h, p, t)
        saved.append(s)
    dh, dg_final, loss_part = _loss_head(h, final_g, target_p, t, d)
    grads = [None] * len(layers)
    for l in reversed(range(len(layers))):
        dh, grads[l] = _layer_bwd(lay, dh, layers[l], saved[l], t)
    return loss_part, dh[ROW0:], dh[PAD:ROW0], grads, dg_final


_SMALL_ROWS = 48


def _pack_small(d, meta, mix, gla, mlp, final, b_alpha, b_forget, w_alpha2):
    rows = [meta.reshape(N_META, d), mix.reshape(DEPTH, d), gla.reshape(DEPTH, d), mlp.reshape(DEPTH, d),
            final.reshape(1, d), b_alpha.reshape(1, d),
            jnp.pad(b_forget.reshape(1, DEPTH * FOX_HEADS), ((0, 0), (0, d - DEPTH * FOX_HEADS))),
            jnp.zeros((7, d), F32), w_alpha2.reshape(GLA_RANK, d)]
    return jnp.concatenate(rows, axis=0)


def _unpack_small(d, packed):
    return dict(meta=packed[:N_META], norm_mix_g=packed[16:18], gla_norm_g=packed[18:20], norm_mlp_g=packed[20:22],
                final_norm_g=packed[22], b_alpha=packed[23].reshape(DEPTH, d // 2),
                b_forget=packed[24, :DEPTH * FOX_HEADS].reshape(DEPTH, FOX_HEADS),
                w_alpha2=packed[32:48].reshape(DEPTH, GLA_RANK, d // 2))


_BIG = ("w_in", "w_o_fox", "w_o_gla", "w_out", "w_ff1", "w_ff2")
_COL_SHARDED = ("w_in", "w_o_fox", "w_ff1")


def _full_matrix(name, gathered_l):
    nch, r, c = gathered_l.shape
    if name in _COL_SHARDED:
        return gathered_l.transpose(1, 0, 2).reshape(r, nch * c)
    return gathered_l.reshape(nch * r, c)


def _shard_major(name, full):
    r, c = full.shape
    if name in _COL_SHARDED:
        return full.reshape(r, N_CHIPS, c // N_CHIPS).transpose(1, 0, 2)
    return full.reshape(N_CHIPS, r // N_CHIPS, c)


def kernel(x, meta_tokens, norm_mix_g, w_in, b_forget, w_alpha2, b_alpha, gla_norm_g, w_o_fox, w_o_gla, w_out, norm_mlp_g, w_ff1, w_ff2, final_norm_g, loss_target, m_meta_tokens, m_norm_mix_g, m_w_in, m_b_forget, m_w_alpha2, m_b_alpha, m_gla_norm_g, m_w_o_fox, m_w_o_gla, m_w_out, m_norm_mlp_g, m_w_ff1, m_w_ff2, m_final_norm_g, v_meta_tokens, v_norm_mix_g, v_w_in, v_b_forget, v_w_alpha2, v_b_alpha, v_gla_norm_g, v_w_o_fox, v_w_o_gla, v_w_out, v_norm_mlp_g, v_w_ff1, v_w_ff2, v_final_norm_g):
    d = x.shape[2]
    lay = _Layout(d)
    xi, yi, ci = lax.axis_index("x"), lax.axis_index("y"), lax.axis_index("c")
    chip = (2 * xi + yi).astype(jnp.int32)
    w = dict(w_in=w_in, w_alpha2=w_alpha2, w_o_fox=w_o_fox, w_o_gla=w_o_gla, w_out=w_out, w_ff1=w_ff1, w_ff2=w_ff2)
    m = dict(w_in=m_w_in, w_alpha2=m_w_alpha2, w_o_fox=m_w_o_fox, w_o_gla=m_w_o_gla, w_out=m_w_out, w_ff1=m_w_ff1,
             w_ff2=m_w_ff2)
    v = dict(w_in=v_w_in, w_alpha2=v_w_alpha2, w_o_fox=v_w_o_fox, w_o_gla=v_w_o_gla, w_out=v_w_out, w_ff1=v_w_ff1,
             w_ff2=v_w_ff2)

    seq = x.shape[1]
    t = seq + ROW0
    core_idx = ci.astype(jnp.int32)[None]
    chip_idx = chip[None]

    cols = d // N_CHIPS
    small_w = jnp.concatenate([meta_tokens, w_alpha2.reshape(-1, cols)], axis=0)
    small_raw = _all_gather_small(small_w)
    small_all = small_raw[0::2]
    alpha_full = small_all[:, N_META:].reshape(N_CHIPS, DEPTH, GLA_RANK, lay.gk // N_CHIPS)
    alpha_full = alpha_full.transpose(1, 2, 0, 3).reshape(DEPTH, GLA_RANK, lay.gk)
    groups = [(0, ("w_in",)), (0, _BIG[1:]), (1, _BIG)]
    started, after = [], small_raw
    for gi, (l, names) in enumerate(groups):
        own16 = [w[n][l].astype(BF16) for n in names]
        lands = [lax.empty((N_CHIPS,) + o.shape, BF16) for o in own16]
        started.append(_ici_start("gather_start_%d" % gi, own16, lands, _gather_plan, after=after))
        after = started[gi][4]
    meta_full = small_all[:, :N_META].transpose(1, 0, 2).reshape(N_META, d) + after[0, 0]

    def gathered(gi, after):
        send_sems, recv_sems, srcs, lands, _ = started[gi]
        srcs, lands = _ici_wait("gather_wait_%d" % gi, send_sems, recv_sems, srcs, lands, _gather_plan, after)
        lands = _forward_halves("gather_forward_%d" % gi, lands)
        return {n: lax.dynamic_update_slice(g, o[None], (chip, 0, 0)) for n, g, o in zip(groups[gi][1], lands, srcs)}

    def early_weights(l, gl):
        w_alpha_p = jnp.zeros((LANES, lay.gk), BF16).at[FOX_HEADS:FOX_HEADS + GLA_RANK].set(
            alpha_full[l].astype(BF16))
        return dict(
            w_in=lay.to_p(gl["w_in"]), w_alpha_p=w_alpha_p,
            norm_mix_g=norm_mix_g[l][None], norm_mlp_g=norm_mlp_g[l][None], gla_norm_g=gla_norm_g[l][None],
            b_alpha=b_alpha[l][None],
            b_forget_p=jnp.pad(b_forget[l][None], ((0, 0), (0, LANES - FOX_HEADS))))

    def late_weights(gl):
        return dict(w_o_fox=gl["w_o_fox"], w_o_gla=_full_matrix("w_o_gla", gl["w_o_gla"]),
                    w_out=_full_matrix("w_out", gl["w_out"]), w_ff1=gl["w_ff1"],
                    w_ff2=_full_matrix("w_ff2", gl["w_ff2"]))

    h = jnp.pad(x[0], ((ROW0, 0), (0, 0))).at[PAD:ROW0].set(meta_full)
    layers, saved = [], []
    layers.append(early_weights(0, gathered(0, after=h)))
    h, s = _layer_fwd(lay, h, layers[0], t, late=lambda after: late_weights(gathered(1, after)))
    saved.append(s)
    gl = gathered(2, after=h)
    layers.append({**early_weights(1, gl), **late_weights(gl)})
    h, s = _layer_fwd(lay, h, layers[1], t)
    saved.append(s)
    dh, dg_final, loss_part = _loss_head(h, final_norm_g[None], jnp.pad(loss_target[0], ((ROW0, 0), (0, 0))), t, d)
    loss = lax.psum(loss_part[0, 0], ("x", "y", "c"))

    def partial_of(g, n):
        return g[n] if n in ("w_ff1", "w_o_fox", "w_in") else _shard_major(n, g[n])

    scatter_groups = dict(mlp=("w_ff1", "w_ff2"), out=("w_o_fox", "w_o_gla", "w_out"))
    scatter_groups["in"] = ("w_in",)
    scattered = {}

    def start_scatter(l, grp, g, after=None):
        names = scatter_groups[grp]
        tag = "%d_%s" % (l, grp)
        parts = [partial_of(g, n) for n in names]
        from_sibling = _swap_halves("swap_halves_" + tag, parts)
        sums = [_pair_sum("pair_sum_%d_%s" % (l, n), p_, r_, core_idx) for n, p_, r_ in zip(names, parts, from_sibling)]
        lands = [lax.empty(s16.shape, BF16) for _, s16 in sums]
        send_sems, recv_sems, srcs, lands, token = _ici_start("scatter_start_" + tag, [s16 for _, s16 in sums], lands,
                                                              _scatter_plan, after=after)
        scattered[l, grp] = (send_sems, recv_sems, srcs, lands, [s32 for s32, _ in sums])
        return token

    def gates_for(l, points):
        return {grp: (lambda g, grp=grp: start_scatter(l, grp, g)[0, 0]) for grp in points}

    grads = [None] * DEPTH
    dh, grads[1] = _layer_bwd(lay, dh, layers[1], saved[1], t, gates=gates_for(1, ("mlp", "out", "in")))
    dh, grads[0] = _layer_bwd(lay, dh, layers[0], saved[0], t, gates=gates_for(0, ("mlp", "out")))
    grad_x, d_meta = dh[ROW0:], dh[PAD:ROW0]

    stack = lambda key: jnp.concatenate([grads[l][key] for l in range(DEPTH)], axis=0)
    b_forget_g = jnp.concatenate([grads[l]["b_forget_p"][:, :FOX_HEADS] for l in range(DEPTH)], axis=0)
    alpha_g = jnp.stack([grads[l]["w_alpha_p"][FOX_HEADS:FOX_HEADS + GLA_RANK] for l in range(DEPTH)])
    packed = _pack_small(d, d_meta, stack("norm_mix_g"), stack("gla_norm_g"), stack("norm_mlp_g"), dg_final,
                         stack("b_alpha"), b_forget_g, alpha_g)
    small_g = _unpack_small(d, _sum_devices(_all_gather_small(packed)))
    small_g["meta"] = lax.dynamic_slice_in_dim(small_g["meta"], chip * (d // N_CHIPS), d // N_CHIPS, axis=1)
    alpha_shard = lax.dynamic_slice_in_dim(small_g["w_alpha2"], chip * (lay.gk // N_CHIPS), lay.gk // N_CHIPS, axis=2)

    after = start_scatter(0, "in", grads[0], after=small_g["final_norm_g"])
    outs = {n: None for n in _BIG}
    for l in reversed(range(DEPTH)):
        names, layer_g = [], []
        for grp in ("mlp", "out", "in"):
            send_sems, recv_sems, srcs, lands, sums32 = scattered[l, grp]
            _, lands = _ici_wait("scatter_wait_%d_%s" % (l, grp), send_sems, recv_sems, srcs, lands, _scatter_plan,
                                 after)
            names += scatter_groups[grp]
            layer_g += [_chip_sum("chip_sum_%d_%s" % (l, n), s32, r_, chip_idx, core_idx)
                        for n, s32, r_ in zip(scatter_groups[grp], sums32, lands)]
        layer_g = _share_halves("share_halves_%d" % l, layer_g)
        for n, g in zip(names, layer_g):
            outs[n] = _adamw("adamw_%d_%s" % (l, n), w[n], g, m[n], v[n], layer=l, into=outs[n])
        after = outs[names[-1]][0]
    out_g, out_d, out_m, out_v = {}, {}, {}, {}
    for n in _BIG:
        out_g[n], out_d[n], out_m[n], out_v[n] = outs[n]
    out_g["w_alpha2"], out_d["w_alpha2"], out_m["w_alpha2"], out_v["w_alpha2"] = _adamw(
        "adamw_w_alpha2", w["w_alpha2"], alpha_shard, m["w_alpha2"], v["w_alpha2"])
    sm_w = dict(meta_tokens=meta_tokens, norm_mix_g=norm_mix_g, b_forget=b_forget, b_alpha=b_alpha,
                gla_norm_g=gla_norm_g, norm_mlp_g=norm_mlp_g, final_norm_g=final_norm_g)
    sm_m = dict(meta_tokens=m_meta_tokens, norm_mix_g=m_norm_mix_g, b_forget=m_b_forget, b_alpha=m_b_alpha,
                gla_norm_g=m_gla_norm_g, norm_mlp_g=m_norm_mlp_g, final_norm_g=m_final_norm_g)
    sm_v = dict(meta_tokens=v_meta_tokens, norm_mix_g=v_norm_mix_g, b_forget=v_b_forget, b_alpha=v_b_alpha,
                gla_norm_g=v_gla_norm_g, norm_mlp_g=v_norm_mlp_g, final_norm_g=v_final_norm_g)
    sm_g = dict(meta_tokens=small_g["meta"], norm_mix_g=small_g["norm_mix_g"], b_forget=small_g["b_forget"],
                b_alpha=small_g["b_alpha"], gla_norm_g=small_g["gla_norm_g"], norm_mlp_g=small_g["norm_mlp_g"],
                final_norm_g=small_g["final_norm_g"])
    names_small = list(sm_w)
    sizes = [sm_w[n].size for n in names_small]
    width = 512
    total = -(-sum(sizes) // (8 * width)) * (8 * width)

    def pack_flat(dct, fill):
        flat = jnp.concatenate([dct[n].reshape(-1) for n in names_small])
        return jnp.pad(flat, (0, total - flat.shape[0]), constant_values=fill).reshape(1, -1, width)

    res = _adamw("adamw_small", pack_flat(sm_w, 0.0), pack_flat(sm_g, 0.0), pack_flat(sm_m, 0.0), pack_flat(sm_v, 1.0))
    offs = [0]
    for sz in sizes:
        offs.append(offs[-1] + sz)
    for i, n in enumerate(names_small):
        out_g[n] = sm_g[n].reshape(sm_w[n].shape)
        out_d[n], out_m[n], out_v[n] = [r.reshape(-1)[offs[i]:offs[i + 1]].reshape(sm_w[n].shape) for r in res[1:]]

    order = ["meta_tokens", "norm_mix_g", "w_in", "b_forget", "w_alpha2", "b_alpha", "gla_norm_g", "w_o_fox",
             "w_o_gla", "w_out", "norm_mlp_g", "w_ff1", "w_ff2", "final_norm_g"]
    return (loss, grad_x[None], *[out_g[n] for n in order], *[out_d[n] for n in order],
            *[out_m[n] for n in order], *[out_v[n] for n in order])
```

```python
import functools

import numpy as np

import jax
import jax.numpy as jnp
from jax import lax
from jax.experimental import pallas as pl
from jax.experimental.pallas import tpu as pltpu

F32 = jnp.float32
BF16 = jnp.bfloat16

N_META = 16
PAD = 112
ROW0 = PAD + N_META
EPS = 1e-6
MASK_VALUE = -1e30
FOX_HEADS = 8
FOX_GROUP = 2
GLA_HEADS = 4
GLA_RANK = 16
GLA_TAU = 16.0
GLA_CHUNK = 64
DEPTH = 2
N_CHIPS = 4
N_DEV = 8

ADAM_LR = 0.001
ADAM_B1 = 0.9
ADAM_B2 = 0.999
ADAM_EPS = 1e-08
ADAM_WD = 0.01
ADAM_STEP = 10

LANES = 128
VMEM_LIMIT = 56 * 1024 * 1024
MESH = pl.DeviceIdType.MESH


def _pick(n, target, mult):
    best = None
    for d in range(mult, min(n, target) + 1, mult):
        if n % d == 0:
            best = d
    return n if best is None else best


def _params(sem=None):
    return pltpu.CompilerParams(dimension_semantics=sem, vmem_limit_bytes=VMEM_LIMIT)


def _bf(v):
    return v if v.dtype == BF16 else v.astype(BF16)


def _sigmoid(z):
    return 1.0 / (1.0 + jnp.exp(-z))


def _log_sigmoid(z):
    return jnp.minimum(z, 0.0) - jnp.log(1.0 + jnp.exp(-jnp.abs(z)))


def _split3(v):
    a = v.astype(BF16)
    r = v - a.astype(F32)
    b = r.astype(BF16)
    c = (r - b.astype(F32)).astype(BF16)
    return a, b, c


def _dot(a, b, dims):
    return lax.dot_general(a, b, (dims, ((), ())), preferred_element_type=F32)


NN = ((1,), (0,))
NT = ((1,), (1,))
TN = ((0,), (0,))


def _tri_dot(tri, v, dims=NN):
    a, b, c = _split3(v)
    return _dot(tri, a, dims) + _dot(tri, b, dims) + _dot(tri, c, dims)


def _mm(name, a, b, *, mode, m, n, k, b_c0=0, extras=(), epilogue=None, out_dtypes=(F32,),
        b_shards=1, out_shards=1, tm=1056, tn=1024, tk=2048):
    tm = _pick(m, tm, LANES if mode == "tn" else 16)
    tn = _pick(n // max(b_shards if mode == "nn" else 1, out_shards), tn, LANES)
    if mode == "tn":
        tk = _pick(k, 2112, 16)
    else:
        tk = _pick(k // (b_shards if mode == "nt" else 1), tk, LANES)
    assert b_c0 % (tk if mode == "nt" else tn) == 0 and (b_shards == 1 or b_c0 == 0)
    nk = k // tk
    if mode == "tn":
        a_spec = pl.BlockSpec((tk, tm), lambda i, j, kk: (kk, i))
    else:
        a_spec = pl.BlockSpec((tm, tk), lambda i, j, kk: (i, kk))
    if mode == "nt":
        dims = NT
        if b_shards > 1:
            per = (k // b_shards) // tk
            b_spec = pl.BlockSpec((None, tn, tk), lambda i, j, kk: (kk // per, j, kk % per))
        else:
            b_spec = pl.BlockSpec((tn, tk), lambda i, j, kk: (j, kk + b_c0 // tk))
    else:
        dims = NN if mode == "nn" else TN
        if b_shards > 1:
            per = (n // b_shards) // tn
            b_spec = pl.BlockSpec((None, tk, tn), lambda i, j, kk: (j // per, kk, j % per))
        else:
            b_spec = pl.BlockSpec((tk, tn), lambda i, j, kk: (kk, j + b_c0 // tn))
    ex_specs = [pl.BlockSpec((tm, tn), lambda i, j, kk: (i, j)) for _ in extras]
    if out_shards > 1:
        oper = (n // out_shards) // tn
        out_specs = [pl.BlockSpec((None, tm, tn), lambda i, j, kk: (j // oper, i, j % oper)) for _ in out_dtypes]
        out_shape = [jax.ShapeDtypeStruct((out_shards, m, n // out_shards), dt) for dt in out_dtypes]
    else:
        out_specs = [pl.BlockSpec((tm, tn), lambda i, j, kk: (i, j)) for _ in out_dtypes]
        out_shape = [jax.ShapeDtypeStruct((m, n), dt) for dt in out_dtypes]
    n_ex = len(extras)
    n_out = len(out_dtypes)

    def finish(acc, ex_refs, out_refs):
        vals = (acc,) if epilogue is None else epilogue(acc, *[r[...] for r in ex_refs])
        for r, v in zip(out_refs, vals):
            r[...] = v.astype(r.dtype)

    def body(a_ref, b_ref, *rest):
        ex_refs = rest[:n_ex]
        out_refs = rest[n_ex:n_ex + n_out]
        prod = _dot(_bf(a_ref[...]), _bf(b_ref[...]), dims)
        if nk == 1:
            finish(prod, ex_refs, out_refs)
            return
        acc_ref = rest[n_ex + n_out]
        kk = pl.program_id(2)

        @pl.when(kk == 0)
        def _():
            acc_ref[...] = prod

        @pl.when((kk > 0) & (kk < nk - 1))
        def _():
            acc_ref[...] += prod

        @pl.when(kk == nk - 1)
        def _():
            finish(acc_ref[...] + prod, ex_refs, out_refs)

    outs = pl.pallas_call(
        body,
        name=name,
        grid=(m // tm, n // tn, nk),
        in_specs=[a_spec, b_spec] + ex_specs,
        out_specs=out_specs,
        out_shape=out_shape,
        scratch_shapes=[pltpu.VMEM((tm, tn), F32)] if nk > 1 else [],
        compiler_params=_params(("parallel", "parallel", "arbitrary")),
    )(a, b, *extras)
    return outs[0] if n_out == 1 else outs


def _mm_grouped_nt(name, segs, b, prev, *, m, n, tm=528, tn=1024, tk=1024):
    tm = _pick(m, tm, 16)
    tn = _pick(n, tn, LANES)
    starts, counts, nk = [], [], 0
    for a, c0 in segs:
        assert a.shape[1] % tk == 0 and c0 % tk == 0
        starts.append(nk)
        counts.append(a.shape[1] // tk)
        nk += counts[-1]
    n_seg = len(segs)
    a_specs = [pl.BlockSpec((tm, tk), functools.partial(
        lambda i, j, kk, s0, nb: (i, jnp.clip(kk - s0, 0, nb - 1)), s0=starts[s], nb=counts[s]))
        for s in range(n_seg)]

    def b_block(kk):
        offset = segs[0][1] // tk - starts[0]
        for s in range(1, n_seg):
            offset = jnp.where(kk >= starts[s], segs[s][1] // tk - starts[s], offset)
        return kk + offset

    def body(*refs):
        a_refs, b_ref, prev_ref, out_ref, acc_ref = refs[:n_seg], refs[n_seg], refs[n_seg + 1], refs[n_seg + 2], refs[-1]
        kk = pl.program_id(2)

        @pl.when(kk == 0)
        def _():
            acc_ref[...] = prev_ref[...]

        for s in range(n_seg):
            @pl.when((kk >= starts[s]) & (kk < starts[s] + counts[s]))
            def _():
                acc_ref[...] += _dot(a_refs[s][...], b_ref[...], NT)

        @pl.when(kk == nk - 1)
        def _():
            out_ref[...] = acc_ref[...]

    tile = pl.BlockSpec((tm, tn), lambda i, j, kk: (i, j))
    return pl.pallas_call(
        body, name=name, grid=(m // tm, n // tn, nk),
        in_specs=a_specs + [pl.BlockSpec((tn, tk), lambda i, j, kk: (j, b_block(kk))), tile],
        out_specs=tile, scratch_shapes=[pltpu.VMEM((tm, tn), F32)],
        out_shape=jax.ShapeDtypeStruct((m, n), F32),
        compiler_params=_params(("parallel", "parallel", "arbitrary")),
    )(*[a for a, _ in segs], b, prev)


def _ew(name, fn, ins, outs, rows, tm):
    tm = _pick(rows, tm, 16)
    in_specs, args = [], []
    for spec in ins:
        if spec[0] == "tile":
            _, arr, width, c0 = spec
            assert c0 % width == 0
            in_specs.append(pl.BlockSpec((tm, width), functools.partial(lambda i, o: (i, o), o=c0 // width)))
        else:
            arr = spec[1]
            in_specs.append(pl.BlockSpec(arr.shape, lambda i: (0, 0)))
        args.append(arr)
    out_specs, out_shape = [], []
    for kind, dt, width in outs:
        if kind == "tile":
            out_specs.append(pl.BlockSpec((tm, width), lambda i: (i, 0)))
            out_shape.append(jax.ShapeDtypeStruct((rows, width), dt))
        else:
            out_specs.append(pl.BlockSpec((1, width), lambda i: (0, 0)))
            out_shape.append(jax.ShapeDtypeStruct((1, width), dt))
    n_in = len(ins)
    has_acc = any(o[0] == "acc" for o in outs)

    def body(*refs):
        i = pl.program_id(0)
        vals = fn(i * tm, *[r[...] for r in refs[:n_in]])
        for (kind, _, _), r, v in zip(outs, refs[n_in:], vals):
            if kind == "tile":
                r[...] = v.astype(r.dtype)
            else:
                @pl.when(i == 0)
                def _():
                    r[...] = jnp.zeros_like(r)

                r[...] += v.astype(r.dtype)

    res = pl.pallas_call(
        body,
        name=name,
        grid=(rows // tm,),
        in_specs=in_specs,
        out_specs=out_specs,
        out_shape=out_shape,
        compiler_params=_params(("arbitrary",) if has_acc else ("parallel",)),
    )(*args)
    return res[0] if len(outs) == 1 else res


def _row_ids(row0, tm):
    return row0 + lax.broadcasted_iota(jnp.int32, (tm, 1), 0)


def _colsum(v):
    return jnp.sum(v, axis=0, keepdims=True)


def _rms_fwd(name, h, g, t, d):
    def fn(row0, x, gg):
        r = lax.rsqrt(jnp.mean(x * x, axis=-1, keepdims=True) + EPS)
        return (x * r * gg,)

    return _ew(name, fn, [("tile", h, d, 0), ("full", g)], [("tile", BF16, d)], t, 264)


def _rms_bwd(name, h, g, dy, dres, t, d):
    def fn(row0, x, gg, dyv, dr):
        r = lax.rsqrt(jnp.mean(x * x, axis=-1, keepdims=True) + EPS)
        xh = x * r
        dxh = dyv * gg
        dx = r * (dxh - xh * jnp.mean(dxh * xh, axis=-1, keepdims=True))
        out = jnp.where(_row_ids(row0, x.shape[0]) >= PAD, dr + dx, 0.0)
        return out, _colsum(dyv * xh)

    return _ew(name, fn, [("tile", h, d, 0), ("full", g), ("tile", dy, d, 0), ("tile", dres, d, 0)],
               [("tile", F32, d), ("acc", F32, d)], t, 264)


def _loss_head(h, g, target_p, t, d):
    def fn(row0, x, gg, tgt):
        real = _row_ids(row0, x.shape[0]) >= ROW0
        r = lax.rsqrt(jnp.mean(x * x, axis=-1, keepdims=True) + EPS)
        xh = x * r
        err = jnp.where(real, xh * gg - tgt, 0.0)
        loss_rows = 0.5 * jnp.mean(err * err, axis=-1, keepdims=True)
        dyv = err * (1.0 / d)
        dxh = dyv * gg
        dx = r * (dxh - xh * jnp.mean(dxh * xh, axis=-1, keepdims=True))
        loss_part = jnp.sum(loss_rows, axis=0, keepdims=True) * jnp.ones((1, LANES), F32)
        return jnp.where(real, dx, 0.0), _colsum(dyv * xh), loss_part

    return _ew("loss_head", fn, [("tile", h, d, 0), ("full", g), ("tile", target_p, d, 0)],
               [("tile", F32, d), ("acc", F32, d), ("acc", F32, LANES)], t, 264)


def _merge_fwd(a_fox, a_gla, proj, c_gates, t, d):
    def fn(row0, af, ag, gates):
        gates = gates.astype(F32)
        return (_sigmoid(gates[:, :d]) * af + _sigmoid(gates[:, d:]) * ag,)

    return _ew("merge_fwd", fn, [("tile", a_fox, d, 0), ("tile", a_gla, d, 0), ("tile", proj, 2 * d, c_gates)],
               [("tile", BF16, d)], t, 264)


def _merge_bwd(dy, a_fox, a_gla, proj, c_gates, t, d):
    def fn(row0, dyv, af, ag, gates):
        gates = gates.astype(F32)
        sf = _sigmoid(gates[:, :d])
        sg = _sigmoid(gates[:, d:])
        dgates = jnp.concatenate([dyv * af * sf * (1.0 - sf), dyv * ag * sg * (1.0 - sg)], axis=1)
        return dyv * sf, dyv * sg, dgates

    return _ew("merge_bwd", fn,
               [("tile", dy, d, 0), ("tile", a_fox, d, 0), ("tile", a_gla, d, 0), ("tile", proj, 2 * d, c_gates)],
               [("tile", BF16, d), ("tile", BF16, d), ("tile", BF16, 2 * d)], t, 264)


def _fox_gate_fwd(small, b_forget_p, t):
    tb = _pick(t, 384, LANES)

    def body(s_ref, b_ref, c_ref, carry_ref):
        i = pl.program_id(0)

        @pl.when(i == 0)
        def _():
            carry_ref[...] = jnp.zeros_like(carry_ref)

        logf = _log_sigmoid(s_ref[...] + b_ref[...])
        logf = jnp.where(_row_ids(i * tb, tb) >= PAD, logf, 0.0)
        r = lax.broadcasted_iota(jnp.int32, (tb, tb), 0)
        c = lax.broadcasted_iota(jnp.int32, (tb, tb), 1)
        tri = (c <= r).astype(BF16)
        cs = _tri_dot(tri, logf) + carry_ref[...]
        c_ref[...] = cs
        carry_ref[...] = cs[tb - 1:tb, :]

    return pl.pallas_call(
        body, name="fox_gate_fwd", grid=(t // tb,),
        in_specs=[pl.BlockSpec((tb, LANES), lambda i: (i, 0)), pl.BlockSpec((1, LANES), lambda i: (0, 0))],
        out_specs=pl.BlockSpec((tb, LANES), lambda i: (i, 0)),
        out_shape=jax.ShapeDtypeStruct((t, LANES), F32),
        scratch_shapes=[pltpu.VMEM((1, LANES), F32)],
        compiler_params=_params(("arbitrary",)),
    )(small, b_forget_p)


def _fox_gate_bwd(dc, small, b_forget_p, dga, t):
    tb = _pick(t, 384, LANES)
    nb = t // tb

    def body(dc_ref, s_ref, b_ref, dga_ref, ds_ref, db_ref, carry_ref):
        i = pl.program_id(0)

        @pl.when(i == 0)
        def _():
            carry_ref[...] = jnp.zeros_like(carry_ref)
            db_ref[...] = jnp.zeros_like(db_ref)

        r = lax.broadcasted_iota(jnp.int32, (tb, tb), 0)
        c = lax.broadcasted_iota(jnp.int32, (tb, tb), 1)
        tri = (c >= r).astype(BF16)
        dlogf = _tri_dot(tri, dc_ref[...]) + carry_ref[...]
        carry_ref[...] = dlogf[0:1, :]
        z = s_ref[...] + b_ref[...]
        dff = dlogf * _sigmoid(-z)
        lane = lax.broadcasted_iota(jnp.int32, (tb, LANES), 1)
        keep = (_row_ids((nb - 1 - i) * tb, tb) >= PAD) & (lane < FOX_HEADS)
        dff = jnp.where(keep, dff, 0.0)
        ds_ref[...] = dff + dga_ref[...]
        db_ref[...] += _colsum(dff)

    rev = lambda i: (nb - 1 - i, 0)
    return pl.pallas_call(
        body, name="fox_gate_bwd", grid=(nb,),
        in_specs=[pl.BlockSpec((tb, LANES), rev), pl.BlockSpec((tb, LANES), rev),
                  pl.BlockSpec((1, LANES), lambda i: (0, 0)), pl.BlockSpec((tb, LANES), rev)],
        out_specs=[pl.BlockSpec((tb, LANES), rev), pl.BlockSpec((1, LANES), lambda i: (0, 0))],
        out_shape=[jax.ShapeDtypeStruct((t, LANES), F32), jax.ShapeDtypeStruct((1, LANES), F32)],
        scratch_shapes=[pltpu.VMEM((1, LANES), F32)],
        compiler_params=_params(("arbitrary",)),
    )(dc, small, b_forget_p, dga)


def _fox_pairs(nb, by_key):
    if by_key:
        pairs = [(qi, ki) for ki in range(nb) for qi in range(ki, nb)]
    else:
        pairs = [(qi, ki) for qi in range(nb) for ki in range(qi + 1)]
    return (jnp.asarray(np.array([p[0] for p in pairs], np.int32)),
            jnp.asarray(np.array([p[1] for p in pairs], np.int32)), len(pairs))


def _fox_specs(tb, fd, c_fq, c_fkv):
    gw = FOX_GROUP * fd
    q0, kv0 = c_fq // gw, c_fkv // (2 * gw)
    return dict(
        q=pl.BlockSpec((tb, gw), lambda g, p, qt, kt: (qt[p], q0 + g)),
        kv=pl.BlockSpec((tb, 2 * gw), lambda g, p, qt, kt: (kt[p], kv0 + g)),
        col=pl.BlockSpec((FOX_GROUP, tb, 1), lambda g, p, qt, kt: (g, qt[p], 0)),
        row=pl.BlockSpec((FOX_GROUP, 1, tb), lambda g, p, qt, kt: (g, 0, kt[p])),
        head=pl.BlockSpec((tb, gw), lambda g, p, qt, kt: (qt[p], g)),
        key_kv=pl.BlockSpec((tb, 2 * gw), lambda g, p, qt, kt: (kt[p], g)),
    )


def _fox_mask(qi, ki, tb):
    row = qi * tb + lax.broadcasted_iota(jnp.int32, (tb, tb), 0)
    col = ki * tb + lax.broadcasted_iota(jnp.int32, (tb, tb), 1)
    return (col <= row) & (col >= PAD)


def _fox_heads(q_ref, kv_ref, fd):
    return [(q_ref[:, hh * fd:(hh + 1) * fd], kv_ref[:, 2 * hh * fd:(2 * hh + 1) * fd],
             kv_ref[:, (2 * hh + 1) * fd:(2 * hh + 2) * fd]) for hh in range(FOX_GROUP)]


def _fox_fwd(proj, c_col, c_row, t, fd, c_fq, c_fkv):
    tb = _pick(t, 384, LANES)
    nb = t // tb
    scale = fd ** -0.5
    sp = _fox_specs(tb, fd, c_fq, c_fkv)
    qt, kt, npairs = _fox_pairs(nb, by_key=False)

    def body(qt_ref, kt_ref, q_ref, kv_ref, cq_ref, ck_ref, o_ref, lse_ref, m_ref, l_ref, acc_ref):
        p = pl.program_id(1)
        qi, ki = qt_ref[p], kt_ref[p]

        @pl.when(ki == 0)
        def _():
            m_ref[...] = jnp.full_like(m_ref, -jnp.inf)
            l_ref[...] = jnp.zeros_like(l_ref)
            acc_ref[...] = jnp.zeros_like(acc_ref)

        def update(masked):
            mask = _fox_mask(qi, ki, tb) if masked else None
            for hh, (q, k, v) in enumerate(_fox_heads(q_ref, kv_ref, fd)):
                s = _dot(q, k, NT) * scale + cq_ref[hh] - ck_ref[hh]
                if masked:
                    s = jnp.where(mask, s, MASK_VALUE)
                m_prev = m_ref[hh]
                m_new = jnp.maximum(m_prev, jnp.max(s, axis=-1, keepdims=True))
                alpha = jnp.exp(m_prev - m_new)
                pe = jnp.exp(s - m_new)
                l_ref[hh] = alpha * l_ref[hh] + jnp.sum(pe, axis=-1, keepdims=True)
                acc_ref[hh] = alpha * acc_ref[hh] + _dot(pe.astype(BF16), v, NN)
                m_ref[hh] = m_new

        edge = (ki == 0) | (ki == qi)
        pl.when(edge)(functools.partial(update, True))
        pl.when(jnp.logical_not(edge))(functools.partial(update, False))

        @pl.when(ki == qi)
        def _():
            real = _row_ids(qi * tb, tb) >= PAD
            for hh in range(FOX_GROUP):
                o_ref[:, hh * fd:(hh + 1) * fd] = jnp.where(real, acc_ref[hh] / l_ref[hh], 0.0)
                lse_ref[hh] = m_ref[hh] + jnp.log(l_ref[hh])

    return pl.pallas_call(
        body, name="fox_fwd",
        grid_spec=pltpu.PrefetchScalarGridSpec(
            num_scalar_prefetch=2, grid=(FOX_HEADS // FOX_GROUP, npairs),
            in_specs=[sp["q"], sp["kv"], sp["col"], sp["row"]],
            out_specs=[sp["head"], sp["col"]],
            scratch_shapes=[pltpu.VMEM((FOX_GROUP, tb, 1), F32), pltpu.VMEM((FOX_GROUP, tb, 1), F32),
                            pltpu.VMEM((FOX_GROUP, tb, fd), F32)]),
        out_shape=[jax.ShapeDtypeStruct((t, FOX_HEADS * fd), F32), jax.ShapeDtypeStruct((FOX_HEADS, t, 1), F32)],
        compiler_params=_params(("parallel", "arbitrary")),
    )(qt, kt, proj, proj, c_col, c_row)


def _fox_delta(o_fox, do_fox, t, fd):
    tb = _pick(t, 384, LANES)

    def body(o_ref, do_ref, out_ref):
        for h in range(FOX_HEADS):
            sl = slice(h * fd, (h + 1) * fd)
            out_ref[h] = jnp.sum(o_ref[:, sl] * do_ref[:, sl].astype(BF16).astype(F32), axis=-1, keepdims=True)

    w = FOX_HEADS * fd
    return pl.pallas_call(
        body, name="fox_delta", grid=(t // tb,),
        in_specs=[pl.BlockSpec((tb, w), lambda i: (i, 0)), pl.BlockSpec((tb, w), lambda i: (i, 0))],
        out_specs=pl.BlockSpec((FOX_HEADS, tb, 1), lambda i: (0, i, 0)),
        out_shape=jax.ShapeDtypeStruct((FOX_HEADS, t, 1), F32),
        compiler_params=_params(("parallel",)),
    )(o_fox, do_fox)


def _fox_bwd(proj, c_col, c_row, lse, delta, do_fox, t, fd, c_fq, c_fkv):
    tb = _pick(t, 384, LANES)
    nb = t // tb
    scale = fd ** -0.5
    sp = _fox_specs(tb, fd, c_fq, c_fkv)
    qt, kt, npairs = _fox_pairs(nb, by_key=True)
    gw = FOX_GROUP * fd

    def body(qt_ref, kt_ref, q_ref, kv_ref, cq_ref, ck_ref, lse_ref, dl_ref, do_ref, dq_ref, dkv_ref, dc_ref, dr_ref,
             dq_acc, dk_acc, dv_acc, dc_acc, dr_acc):
        p = pl.program_id(1)
        qi, ki = qt_ref[p], kt_ref[p]

        @pl.when(p == 0)
        def _():
            dq_acc[...] = jnp.zeros_like(dq_acc)
            dr_acc[...] = jnp.zeros_like(dr_acc)

        @pl.when(qi == ki)
        def _():
            dk_acc[...] = jnp.zeros_like(dk_acc)
            dv_acc[...] = jnp.zeros_like(dv_acc)
            dc_acc[...] = jnp.zeros_like(dc_acc)

        rows = pl.ds(pl.multiple_of(qi * tb, LANES), tb)

        def update(masked):
            mask = _fox_mask(qi, ki, tb) if masked else None
            for hh, (q, k, v) in enumerate(_fox_heads(q_ref, kv_ref, fd)):
                do = _bf(do_ref[:, hh * fd:(hh + 1) * fd])
                s = _dot(q, k, NT) * scale + cq_ref[hh] - ck_ref[hh]
                if masked:
                    s = jnp.where(mask, s, MASK_VALUE)
                pr = jnp.exp(s - lse_ref[hh])
                dp = _dot(do, v, NT)
                ds = pr * (dp - dl_ref[hh])
                ds16 = ds.astype(BF16)
                dv_acc[hh] += _dot(pr.astype(BF16), do, TN)
                dk_acc[hh] += _dot(ds16, q, TN)
                dc_acc[hh] += _colsum(ds)
                dr_acc[hh, rows, :] += jnp.sum(ds, axis=-1, keepdims=True)
                dq_acc[hh, rows, :] += _dot(ds16, k, NN)

        edge = (ki == 0) | (ki == qi)
        pl.when(edge)(functools.partial(update, True))
        pl.when(jnp.logical_not(edge))(functools.partial(update, False))

        @pl.when(qi == nb - 1)
        def _():
            for hh in range(FOX_GROUP):
                dkv_ref[:, 2 * hh * fd:(2 * hh + 1) * fd] = (dk_acc[hh] * scale).astype(dkv_ref.dtype)
                dkv_ref[:, (2 * hh + 1) * fd:(2 * hh + 2) * fd] = dv_acc[hh].astype(dkv_ref.dtype)
                dc_ref[hh] = -dc_acc[hh]

        @pl.when(p == npairs - 1)
        def _():
            for hh in range(FOX_GROUP):
                dq_ref[:, hh * fd:(hh + 1) * fd] = (dq_acc[hh] * scale).astype(dq_ref.dtype)
            dr_ref[...] = dr_acc[...]

    return pl.pallas_call(
        body, name="fox_bwd",
        grid_spec=pltpu.PrefetchScalarGridSpec(
            num_scalar_prefetch=2, grid=(FOX_HEADS // FOX_GROUP, npairs),
            in_specs=[sp["q"], sp["kv"], sp["col"], sp["row"], sp["col"], sp["col"], sp["head"]],
            out_specs=[pl.BlockSpec((t, gw), lambda g, p, qt, kt: (0, g)), sp["key_kv"], sp["row"],
                       pl.BlockSpec((FOX_GROUP, t, 1), lambda g, p, qt, kt: (g, 0, 0))],
            scratch_shapes=[pltpu.VMEM((FOX_GROUP, t, fd), F32), pltpu.VMEM((FOX_GROUP, tb, fd), F32),
                            pltpu.VMEM((FOX_GROUP, tb, fd), F32), pltpu.VMEM((FOX_GROUP, 1, tb), F32),
                            pltpu.VMEM((FOX_GROUP, t, 1), F32)]),
        out_shape=[jax.ShapeDtypeStruct((t, FOX_HEADS * fd), BF16), jax.ShapeDtypeStruct((t, 2 * FOX_HEADS * fd), BF16),
                   jax.ShapeDtypeStruct((FOX_HEADS, 1, t), F32), jax.ShapeDtypeStruct((FOX_HEADS, t, 1), F32)],
        compiler_params=_params(("parallel", "arbitrary")),
    )(qt, kt, proj, proj, c_col, c_row, lse, delta, do_fox)


def _gla_gate_fwd(small, w_alpha_p, b_alpha, t, gk):
    def fn(row0, s, w, b):
        z = _dot(s.astype(BF16), w, NN) + b
        return (jnp.where(_row_ids(row0, s.shape[0]) >= PAD, _log_sigmoid(z) * (1.0 / GLA_TAU), 0.0),)

    return _ew("gla_gate_fwd", fn, [("tile", small, LANES, 0), ("full", w_alpha_p), ("full", b_alpha)],
               [("tile", F32, gk)], t, 264)


def _gla_gate_bwd(dglog, small, w_alpha_p, b_alpha, t, gk):
    def fn(row0, dg, s, w, b):
        z = _dot(s.astype(BF16), w, NN) + b
        dz = jnp.where(_row_ids(row0, s.shape[0]) >= PAD, dg * (1.0 / GLA_TAU) * _sigmoid(-z), 0.0)
        return dz, _colsum(dz)

    return _ew("gla_gate_bwd", fn,
               [("tile", dglog, gk, 0), ("tile", small, LANES, 0), ("full", w_alpha_p), ("full", b_alpha)],
               [("tile", BF16, gk), ("acc", F32, gk)], t, 264)


def _gla_chunk(q, k, g, scale, cs):
    r = lax.broadcasted_iota(jnp.int32, (cs, cs), 0)
    c = lax.broadcasted_iota(jnp.int32, (cs, cs), 1)
    causal = c <= r
    b = _tri_dot(causal.astype(BF16), g)
    bl = b[cs - 1:cs, :]
    eb, einv, eend = jnp.exp(b), jnp.exp(-b), jnp.exp(bl - b)
    qd = q.astype(F32) * scale * eb
    kf = k.astype(F32)
    return causal, (eb, einv, eend), bl, qd, kf * einv, kf * eend


def _gla_fwd(proj, glog, t, dk, dv, c_q, c_k, c_v):
    cs = GLA_CHUNK
    nc = t // cs
    wk, wv = GLA_HEADS * dk, GLA_HEADS * dv
    scale = dk ** -0.5

    def body(q_ref, k_ref, v_ref, g_ref, o_ref, sp_ref, st_ref):
        @pl.when(pl.program_id(0) == 0)
        def _():
            st_ref[...] = jnp.zeros_like(st_ref)

        for h in range(GLA_HEADS):
            ks, vs = slice(h * dk, (h + 1) * dk), slice(h * dv, (h + 1) * dv)
            v = v_ref[:, vs]
            causal, _, bl, qd, ki, ke = _gla_chunk(q_ref[:, ks], k_ref[:, ks], g_ref[:, ks], scale, cs)
            st = st_ref[h]
            sp_ref[h] = st
            a = jnp.where(causal, _dot(qd.astype(BF16), ki.astype(BF16), NT), 0.0)
            o_ref[:, vs] = _dot(a.astype(BF16), v, NN) + _dot(qd.astype(BF16), st.astype(BF16), NT)
            st_ref[h] = st * jnp.exp(bl) + _dot(v, ke.astype(BF16), TN)

    return pl.pallas_call(
        body, name="gla_fwd", grid=(nc,),
        in_specs=[pl.BlockSpec((cs, wk), lambda n: (n, c_q // wk)), pl.BlockSpec((cs, wk), lambda n: (n, c_k // wk)),
                  pl.BlockSpec((cs, wv), lambda n: (n, c_v // wv)), pl.BlockSpec((cs, wk), lambda n: (n, 0))],
        out_specs=[pl.BlockSpec((cs, wv), lambda n: (n, 0)),
                   pl.BlockSpec((None, GLA_HEADS, dv, dk), lambda n: (n, 0, 0, 0))],
        out_shape=[jax.ShapeDtypeStruct((t, wv), F32), jax.ShapeDtypeStruct((nc, GLA_HEADS, dv, dk), F32)],
        scratch_shapes=[pltpu.VMEM((GLA_HEADS, dv, dk), F32)],
        compiler_params=_params(("arbitrary",)),
    )(proj, proj, proj, glog)


def _gla_bwd(proj, glog, s_prev, do_raw, t, dk, dv, c_q, c_k, c_v):
    cs = GLA_CHUNK
    nc = t // cs
    wk, wv = GLA_HEADS * dk, GLA_HEADS * dv
    scale = dk ** -0.5

    def body(q_ref, k_ref, v_ref, g_ref, sp_ref, do_ref, dq_ref, dk_ref, dv_ref, dg_ref, dst_ref):
        @pl.when(pl.program_id(0) == 0)
        def _():
            dst_ref[...] = jnp.zeros_like(dst_ref)

        for h in range(GLA_HEADS):
            ks, vs = slice(h * dk, (h + 1) * dk), slice(h * dv, (h + 1) * dv)
            v = v_ref[:, vs]
            do = do_ref[:, vs].astype(BF16)
            causal, (eb, einv, eend), bl, qd, ki, ke = _gla_chunk(q_ref[:, ks], k_ref[:, ks], g_ref[:, ks], scale, cs)
            qd16, ki16, ke16 = qd.astype(BF16), ki.astype(BF16), ke.astype(BF16)
            st = sp_ref[h]
            dst = dst_ref[h]
            dst16 = dst.astype(BF16)
            a = jnp.where(causal, _dot(qd16, ki16, NT), 0.0).astype(BF16)
            da = jnp.where(causal, _dot(do, v, NT), 0.0).astype(BF16)
            dvv = _dot(a, do, TN) + _dot(ke16, dst16, NT)
            dqd = _dot(da, ki16, NN) + _dot(do, st.astype(BF16), NN)
            dki = _dot(da, qd16, TN)
            dke = _dot(v, dst16, NN)
            dl = jnp.exp(bl)
            ddl = _colsum(dst * st)
            dst_ref[h] = dst * dl + _dot(do, qd16, TN)
            dq_ref[:, ks] = (dqd * eb * scale).astype(dq_ref.dtype)
            dk_ref[:, ks] = (dki * einv + dke * eend).astype(dk_ref.dtype)
            dv_ref[:, vs] = dvv.astype(dv_ref.dtype)
            db = dqd * qd - dki * ki - dke * ke
            db_last = _colsum(dke * ke) + ddl * dl
            r = lax.broadcasted_iota(jnp.int32, (cs, cs), 0)
            c = lax.broadcasted_iota(jnp.int32, (cs, cs), 1)
            dg_ref[:, ks] = _tri_dot((c >= r).astype(BF16), db) + db_last

    rev = lambda f: (lambda n: f(nc - 1 - n))
    return pl.pallas_call(
        body, name="gla_bwd", grid=(nc,),
        in_specs=[pl.BlockSpec((cs, wk), rev(lambda n: (n, c_q // wk))), pl.BlockSpec((cs, wk), rev(lambda n: (n, c_k // wk))),
                  pl.BlockSpec((cs, wv), rev(lambda n: (n, c_v // wv))), pl.BlockSpec((cs, wk), rev(lambda n: (n, 0))),
                  pl.BlockSpec((None, GLA_HEADS, dv, dk), rev(lambda n: (n, 0, 0, 0))),
                  pl.BlockSpec((cs, wv), rev(lambda n: (n, 0)))],
        out_specs=[pl.BlockSpec((cs, wk), rev(lambda n: (n, 0))), pl.BlockSpec((cs, wk), rev(lambda n: (n, 0))),
                   pl.BlockSpec((cs, wv), rev(lambda n: (n, 0))), pl.BlockSpec((cs, wk), rev(lambda n: (n, 0)))],
        out_shape=[jax.ShapeDtypeStruct((t, wk), BF16), jax.ShapeDtypeStruct((t, wk), BF16),
                   jax.ShapeDtypeStruct((t, wv), BF16), jax.ShapeDtypeStruct((t, wk), F32)],
        scratch_shapes=[pltpu.VMEM((GLA_HEADS, dv, dk), F32)],
        compiler_params=_params(("arbitrary",)),
    )(proj, proj, proj, glog, s_prev, do_raw)


def _gla_post_fwd(o_raw, proj, gn, t, dv, c_gr):
    w = GLA_HEADS * dv

    def fn(row0, o, gr, g):
        gr = gr.astype(F32)
        outs = []
        for h in range(GLA_HEADS):
            oh = o[:, h * dv:(h + 1) * dv]
            outs.append(oh * lax.rsqrt(jnp.mean(oh * oh, axis=-1, keepdims=True) + EPS))
        on = jnp.concatenate(outs, axis=1) * g
        return (on * (gr * _sigmoid(gr)),)

    return _ew("gla_post_fwd", fn, [("tile", o_raw, w, 0), ("tile", proj, w, c_gr), ("full", gn)],
               [("tile", BF16, w)], t, 264)


def _gla_post_bwd(o_raw, proj, gn, do_gla, t, dv, c_gr):
    w = GLA_HEADS * dv

    def fn(row0, o, gr, g, do):
        gr = gr.astype(F32)
        sg = _sigmoid(gr)
        don = do * (gr * sg)
        ohs, dos = [], []
        for h in range(GLA_HEADS):
            sl = slice(h * dv, (h + 1) * dv)
            oh = o[:, sl]
            r = lax.rsqrt(jnp.mean(oh * oh, axis=-1, keepdims=True) + EPS)
            xh = oh * r
            dxh = don[:, sl] * g[:, sl]
            ohs.append(xh)
            dos.append(r * (dxh - xh * jnp.mean(dxh * xh, axis=-1, keepdims=True)))
        xh = jnp.concatenate(ohs, axis=1)
        dgr = do * (xh * g) * (sg * (1.0 + gr * (1.0 - sg)))
        return jnp.concatenate(dos, axis=1), dgr, _colsum(don * xh)

    return _ew("gla_post_bwd", fn,
               [("tile", o_raw, w, 0), ("tile", proj, w, c_gr), ("full", gn), ("tile", do_gla, w, 0)],
               [("tile", F32, w), ("tile", BF16, w), ("acc", F32, w)], t, 264)


def _adamw(name, w, g, m, v, layer=None, into=None):
    nl, rows, cols = w.shape
    tm = _pick(rows, max(8, (512 * 1024) // max(cols, 1) // 8 * 8), 8)

    def body(w_ref, g_ref, m_ref, v_ref, *rest):
        go_ref, d_ref, nm_ref, nv_ref = rest[-4:]
        gg = g_ref[...]
        nm = ADAM_B1 * m_ref[...] + (1.0 - ADAM_B1) * gg
        nv = ADAM_B2 * v_ref[...] + (1.0 - ADAM_B2) * (gg * gg)
        m_hat = nm / (1.0 - ADAM_B1 ** ADAM_STEP)
        v_hat = nv / (1.0 - ADAM_B2 ** ADAM_STEP)
        go_ref[...] = gg
        d_ref[...] = -ADAM_LR * (m_hat / (jnp.sqrt(v_hat) + ADAM_EPS) + ADAM_WD * w_ref[...])
        nm_ref[...] = nm
        nv_ref[...] = nv

    out_shape = [jax.ShapeDtypeStruct((nl, rows, cols), F32)] * 4
    if layer is None:
        spec = pl.BlockSpec((None, tm, cols), lambda l, i: (l, i, 0))
        return pl.pallas_call(
            body, name=name, grid=(nl, rows // tm), in_specs=[spec] * 4, out_specs=[spec] * 4, out_shape=out_shape,
            compiler_params=_params(("parallel", "parallel")),
        )(w, g, m, v)
    spec = pl.BlockSpec((None, tm, cols), lambda i: (layer, i, 0))
    in_specs = [spec, pl.BlockSpec((tm, cols), lambda i: (i, 0)), spec, spec]
    args, aliases = [w, g, m, v], {}
    if into is not None:
        in_specs += [pl.BlockSpec(memory_space=pl.ANY)] * 4
        args += list(into)
        aliases = {4 + k: k for k in range(4)}
    return pl.pallas_call(
        body, name=name, grid=(rows // tm,), in_specs=in_specs, out_specs=[spec] * 4, out_shape=out_shape,
        input_output_aliases=aliases, compiler_params=_params(("parallel",)),
    )(*args)


def _me():
    return lax.axis_index("x"), lax.axis_index("y"), lax.axis_index("c")


def _hbm_specs(n):
    return [pl.BlockSpec(memory_space=pl.ANY)] * n


_HBM = pl.BlockSpec(memory_space=pltpu.HBM)
_SEM = pl.BlockSpec(memory_space=pltpu.SEMAPHORE)
_EFFECT = pltpu.SideEffectType.DATAFLOW_SIDE_EFFECTING
N_PEERS = N_CHIPS - 1


def _other_chips(x, y):
    return [(1 - x, y), (x, 1 - y), (1 - x, 1 - y)]


def _split_copies(plan, src, land, send_sems, recv_sems):
    me = _me()
    copies = []
    for i in range(len(src)):
        for j, (s, d, peer) in enumerate(plan(src[i], land[i], me)):
            k = plan.copies * i + j
            copies.append(pltpu.make_async_remote_copy(
                src_ref=s, dst_ref=d, send_sem=send_sems.at[k], recv_sem=recv_sems.at[k], device_id=peer,
                device_id_type=MESH))
    return copies


def _split_start(name, srcs, lands, plan, after=None):
    n = len(srcs)
    extra = [] if after is None else [after]

    def body(*refs):
        src, land = refs[:n], refs[n:2 * n]
        send_sems, recv_sems = refs[2 * n + len(extra)], refs[2 * n + len(extra) + 1]
        token = refs[-1]
        for cp in _split_copies(plan, src, land, send_sems, recv_sems):
            cp.start()
        token[...] = jnp.zeros_like(token)

    out_shape = ([pltpu.SemaphoreType.DMA((plan.copies * n,)), pltpu.SemaphoreType.DMA((plan.copies * n,))]
                 + [pltpu.HBM(a.shape, a.dtype) for a in list(srcs) + list(lands)]
                 + [jax.ShapeDtypeStruct((8, LANES), F32)])
    res = pl.pallas_call(
        body, name=name, out_shape=out_shape,
        in_specs=[_HBM] * (2 * n) + [pl.BlockSpec(memory_space=pl.ANY)] * len(extra),
        out_specs=[_SEM, _SEM] + [_HBM] * (2 * n) + [pl.BlockSpec(memory_space=pltpu.VMEM)],
        input_output_aliases={i: 2 + i for i in range(2 * n)},
        compiler_params=pltpu.CompilerParams(has_side_effects=_EFFECT),
    )(*[pltpu.with_memory_space_constraint(a, pltpu.HBM) for a in list(srcs) + list(lands)], *extra)
    return res[0], res[1], res[2:2 + n], res[2 + n:2 + 2 * n], res[-1]


def _split_wait(name, send_sems, recv_sems, srcs, lands, plan, after):
    n = len(srcs)

    def body(*refs):
        src, land = refs[:n], refs[n:2 * n]
        s_sems, r_sems = refs[2 * n], refs[2 * n + 1]
        for cp in _split_copies(plan, src, land, s_sems, r_sems):
            cp.wait_send()
            cp.wait_recv()

    res = pl.pallas_call(
        body, name=name, out_shape=[pltpu.HBM(a.shape, a.dtype) for a in list(srcs) + list(lands)],
        in_specs=[_HBM] * (2 * n) + [_SEM, _SEM, pl.BlockSpec(memory_space=pl.ANY)], out_specs=[_HBM] * (2 * n),
        input_output_aliases={i: i for i in range(2 * n)},
        compiler_params=pltpu.CompilerParams(has_side_effects=_EFFECT),
    )(*srcs, *lands, send_sems, recv_sems, after)
    return res[:n], res[n:]


def _half(ref_rows, c):
    half = ref_rows // 2
    return pl.ds(c * half, half)


def _gather_plan(src, land, me):
    x, y, c = me
    rows = _half(src.shape[0], c)
    return [(src.at[rows], land.at[2 * x + y, rows], (px, py, c)) for px, py in _other_chips(x, y)]


def _scatter_plan(src, land, me):
    x, y, c = me
    return [(src.at[2 * px + py], land.at[2 * x + y], (px, py, c)) for px, py in _other_chips(x, y)]


def _swap_plan(src, land, me):
    x, y, c = me
    return [(src.at[:, _half(src.shape[1], 1 - c)], land, (x, y, 1 - c))]


_gather_plan.copies = N_PEERS
_scatter_plan.copies = N_PEERS
_swap_plan.copies = 1


def _forward_halves(name, gathered):
    n = len(gathered)

    def body(*refs):
        buf = refs[n:2 * n]
        send_sems, recv_sems = refs[2 * n:]
        x, y, c = _me()
        copies = []
        for i in range(n):
            rows = _half(buf[i].shape[1], c)
            for j, (px, py) in enumerate(_other_chips(x, y)):
                cp = pltpu.make_async_remote_copy(
                    src_ref=buf[i].at[2 * px + py, rows], dst_ref=buf[i].at[2 * px + py, rows],
                    send_sem=send_sems.at[N_PEERS * i + j], recv_sem=recv_sems.at[N_PEERS * i + j],
                    device_id=(x, y, 1 - c), device_id_type=MESH)
                cp.start()
                copies.append(cp)
        for cp in copies:
            cp.wait_recv()
        for cp in copies:
            cp.wait_send()

    return pl.pallas_call(
        body, name=name, in_specs=_hbm_specs(n), out_specs=_hbm_specs(n),
        out_shape=[jax.ShapeDtypeStruct(g.shape, g.dtype) for g in gathered],
        input_output_aliases={i: i for i in range(n)},
        scratch_shapes=[pltpu.SemaphoreType.DMA((N_PEERS * n,)), pltpu.SemaphoreType.DMA((N_PEERS * n,))],
    )(*gathered)


def _all_gather_small(v):
    def body(v_ref, out_ref, send_sems, recv_sems, local_sem):
        x, y, c = _me()
        mine = pltpu.make_async_copy(v_ref, out_ref.at[4 * x + 2 * y + c], local_sem)
        mine.start()
        copies = []
        for k in range(1, N_DEV):
            peer = (x ^ ((k >> 2) & 1), y ^ ((k >> 1) & 1), c ^ (k & 1))
            cp = pltpu.make_async_remote_copy(
                src_ref=v_ref, dst_ref=out_ref.at[4 * x + 2 * y + c], send_sem=send_sems.at[k - 1],
                recv_sem=recv_sems.at[k - 1], device_id=peer, device_id_type=MESH)
            cp.start()
            copies.append(cp)
        for cp in copies:
            cp.wait_recv()
        for cp in copies:
            cp.wait_send()
        mine.wait()

    return pl.pallas_call(
        body, name="all_gather_small", in_specs=_hbm_specs(1), out_specs=pl.BlockSpec(memory_space=pl.ANY),
        out_shape=jax.ShapeDtypeStruct((N_DEV,) + v.shape, v.dtype),
        scratch_shapes=[pltpu.SemaphoreType.DMA((N_DEV - 1,)), pltpu.SemaphoreType.DMA((N_DEV - 1,)),
                        pltpu.SemaphoreType.DMA],
    )(v)


def _share_halves(name, grads):
    n = len(grads)

    def body(*refs):
        buf = refs[n:2 * n]
        send_sems, recv_sems = refs[2 * n:]
        x, y, c = _me()
        copies = []
        for i in range(n):
            rows = _half(buf[i].shape[0], c)
            cp = pltpu.make_async_remote_copy(
                src_ref=buf[i].at[rows], dst_ref=buf[i].at[rows], send_sem=send_sems.at[i],
                recv_sem=recv_sems.at[i], device_id=(x, y, 1 - c), device_id_type=MESH)
            cp.start()
            copies.append(cp)
        for cp in copies:
            cp.wait_recv()
        for cp in copies:
            cp.wait_send()

    out_shape = [jax.ShapeDtypeStruct(s.shape, s.dtype) for s in grads]
    return pl.pallas_call(
        body, name=name, in_specs=_hbm_specs(n), out_specs=_hbm_specs(n), out_shape=out_shape,
        input_output_aliases={i: i for i in range(n)},
        scratch_shapes=[pltpu.SemaphoreType.DMA((n,)), pltpu.SemaphoreType.DMA((n,))],
    )(*grads)


def _pair_sum(name, own, recv, core):
    nch, half, cdim = recv.shape
    tm = _pick(half, max(8, (512 * 1024) // cdim // 16 * 16), 16)
    nt = half // tm

    def body(c_ref, a_ref, b_ref, s_ref, s16_ref):
        s = a_ref[...] + b_ref[...]
        s_ref[...] = s
        s16_ref[...] = s.astype(BF16)

    spec = pl.BlockSpec((None, tm, cdim), lambda j, i, c: (j, i, 0))
    return pl.pallas_call(
        body, name=name,
        grid_spec=pltpu.PrefetchScalarGridSpec(
            num_scalar_prefetch=1, grid=(nch, nt),
            in_specs=[pl.BlockSpec((None, tm, cdim), lambda j, i, c: (j, c[0] * nt + i, 0)), spec],
            out_specs=[spec, spec]),
        out_shape=[jax.ShapeDtypeStruct((nch, half, cdim), F32), jax.ShapeDtypeStruct((nch, half, cdim), BF16)],
        compiler_params=_params(("parallel", "parallel")),
    )(core, own, recv)


def _chip_sum(name, own, recv, chip, core):
    nch, half, cdim = own.shape
    tm = _pick(half, max(8, (512 * 1024) // cdim // 16 * 16), 16)
    nt = half // tm

    def body(c_ref, k_ref, own_ref, *rest):
        recv_refs, out_ref = rest[:nch], rest[-1]
        me = c_ref[0]
        acc = None
        for j in range(nch):
            term = jnp.where(me == j, own_ref[...], recv_refs[j][...].astype(F32))
            acc = term if acc is None else acc + term
        out_ref[...] = acc

    recv_specs = [pl.BlockSpec((None, tm, cdim), functools.partial(lambda i, c, k, j: (j, i, 0), j=j))
                  for j in range(nch)]
    return pl.pallas_call(
        body, name=name,
        grid_spec=pltpu.PrefetchScalarGridSpec(
            num_scalar_prefetch=2, grid=(nt,),
            in_specs=[pl.BlockSpec((None, tm, cdim), lambda i, c, k: (c[0], i, 0))] + recv_specs,
            out_specs=pl.BlockSpec((tm, cdim), lambda i, c, k: (k[0] * nt + i, 0))),
        out_shape=jax.ShapeDtypeStruct((2 * half, cdim), F32),
        compiler_params=_params(("parallel",)),
    )(chip, core, own, *([recv] * nch))


def _sum_devices(gathered):
    _, r, cdim = gathered.shape

    def body(g_ref, o_ref):
        acc = g_ref[0]
        for k in range(1, N_DEV):
            acc = acc + g_ref[k]
        o_ref[...] = acc

    return pl.pallas_call(
        body, name="sum_devices", out_shape=jax.ShapeDtypeStruct((r, cdim), F32),
        compiler_params=_params(),
    )(gathered)


class _Layout:
    def __init__(self, d):
        self.d = d
        self.fw = d // 2
        self.fd = self.fw // FOX_HEADS
        self.gk = d // 2
        self.gv = d
        self.dk = self.gk // GLA_HEADS
        self.dv = self.gv // GLA_HEADS
        self.c_fq = 0
        self.c_gq = self.fw
        self.c_gv = self.c_gq + self.gk
        self.c_gr = self.c_gv + self.gv
        self.c_fkv = self.c_gr + self.gv
        self.c_gates = self.c_fkv + 2 * self.fw
        self.c_gk = self.c_gates + 2 * d
        self.c_small = self.c_gk + self.gk
        self.n_main = self.c_small
        self.n_p = self.c_small + LANES
        self.o_fk = self.fw
        self.o_fv = 2 * self.fw
        self.o_ff = 3 * self.fw
        self.o_gq = self.o_ff + FOX_HEADS
        self.o_gk = self.o_gq + self.gk
        self.o_gv = self.o_gk + self.gk
        self.o_gr = self.o_gv + self.gv
        self.o_ga = self.o_gr + self.gv
        self.o_gf = self.o_ga + GLA_RANK
        self.o_gg = self.o_gf + d
        self.n_orig = self.o_gg + d

    def to_p(self, shards):
        per = self.n_orig // N_CHIPS
        ranges = [(0, self.fw), (self.o_gq, self.gk), (self.o_gv, self.gv), (self.o_gr, self.gv)]
        for h in range(FOX_HEADS):
            ranges += [(self.o_fk + h * self.fd, self.fd), (self.o_fv + h * self.fd, self.fd)]
        ranges += [(self.o_gf, 2 * self.d), (self.o_gk, self.gk), (self.o_ff, FOX_HEADS), (self.o_ga, GLA_RANK)]
        pieces = []
        for a, width in ranges:
            for j in range(a // per, (a + width - 1) // per + 1):
                lo, hi = max(a, j * per), min(a + width, (j + 1) * per)
                pieces.append(shards[j][:, lo - j * per:hi - j * per])
        pieces.append(jnp.zeros((shards.shape[1], LANES - FOX_HEADS - GLA_RANK), shards.dtype))
        return jnp.concatenate(pieces, axis=1)

    def from_segments(self, seg):
        per = self.n_orig // N_CHIPS
        fd = self.fd
        atoms = [("fq", 0, self.fw)]
        atoms += [("fkv", 2 * h * fd, fd) for h in range(FOX_HEADS)]
        atoms += [("fkv", (2 * h + 1) * fd, fd) for h in range(FOX_HEADS)]
        atoms += [("small", 0, FOX_HEADS), ("gq", 0, self.gk), ("gk", 0, self.gk), ("gv", 0, self.gv),
                  ("gr", 0, self.gv), ("small", FOX_HEADS, GLA_RANK), ("gates", 0, 2 * self.d)]
        shards = [[] for _ in range(N_CHIPS)]
        pos = 0
        for name, c0, width in atoms:
            for j in range(pos // per, (pos + width - 1) // per + 1):
                lo, hi = max(pos, j * per), min(pos + width, (j + 1) * per)
                shards[j].append(seg[name][:, c0 + lo - pos:c0 + hi - pos])
            pos += width
        assert pos == self.n_orig
        return jnp.stack([jnp.concatenate(s, axis=1) for s in shards])


def _layer_fwd(lay, h, p, t, late=None):
    d = lay.d
    xn = _rms_fwd("rms_mix_fwd", h, p["norm_mix_g"], t, d)
    proj = _mm("mm_proj", xn, p["w_in"], mode="nn", m=t, n=lay.n_main, k=d, out_dtypes=(BF16,))
    small = _mm("mm_small", xn, p["w_in"], mode="nn", m=t, n=LANES, k=d, b_c0=lay.c_small)
    cs = _fox_gate_fwd(small, p["b_forget_p"], t)
    ct = cs[:, :FOX_HEADS].T
    c_col, c_row = ct[:, :, None], ct[:, None, :]
    o_fox, lse = _fox_fwd(proj, c_col, c_row, t, lay.fd, lay.c_fq, lay.c_fkv)
    glog = _gla_gate_fwd(small, p["w_alpha_p"], p["b_alpha"], t, lay.gk)
    o_raw, s_prev = _gla_fwd(proj, glog, t, lay.dk, lay.dv, lay.c_gq, lay.c_gk, lay.c_gv)
    o_gla = _gla_post_fwd(o_raw, proj, p["gla_norm_g"], t, lay.dv, lay.c_gr)
    if late is not None:
        p.update(late(o_gla))
    a_fox = _mm("mm_o_fox", o_fox, p["w_o_fox"], mode="nn", m=t, n=d, k=lay.fw, b_shards=N_CHIPS)
    a_gla = _mm("mm_o_gla", o_gla, p["w_o_gla"], mode="nn", m=t, n=d, k=lay.gv)
    y = _merge_fwd(a_fox, a_gla, proj, lay.c_gates, t, d)
    h1 = _mm("mm_out", y, p["w_out"], mode="nn", m=t, n=d, k=d, extras=[h], epilogue=lambda acc, res: (res + acc,))
    xn2 = _rms_fwd("rms_mlp_fwd", h1, p["norm_mlp_g"], t, d)
    u, act = _mm("mm_ff1", xn2, p["w_ff1"], mode="nn", m=t, n=4 * d, k=d, out_dtypes=(BF16, BF16), b_shards=N_CHIPS,
                 epilogue=lambda acc: (acc, jnp.square(jnp.maximum(acc, 0.0))))
    h2 = _mm("mm_ff2", act, p["w_ff2"], mode="nn", m=t, n=d, k=4 * d, extras=[h1],
             epilogue=lambda acc, res: (res + acc,))
    saved = dict(h=h, xn=xn, proj=proj, small=small, c_col=c_col, c_row=c_row, o_fox=o_fox, lse=lse, glog=glog,
                 o_raw=o_raw, s_prev=s_prev, o_gla=o_gla, a_fox=a_fox, a_gla=a_gla, y=y, h1=h1, xn2=xn2, u=u, act=act)
    return h2, saved


def _layer_bwd(lay, dh2, p, s, t, gates=None):
    d = lay.d
    g = {}

    def gated(gain, point):
        return gain + gates[point](g) if gates and point in gates else gain
    du = _mm("mm_dact", dh2, p["w_ff2"], mode="nt", m=t, n=4 * d, k=d, extras=[s["u"]], out_dtypes=(BF16,),
             epilogue=lambda acc, u: (acc * (2.0 * jnp.maximum(u.astype(F32), 0.0)),))
    g["w_ff2"] = _mm("mm_dw_ff2", s["act"], dh2, mode="tn", m=4 * d, n=d, k=t)
    g["w_ff1"] = _mm("mm_dw_ff1", s["xn2"], du, mode="tn", m=d, n=4 * d, k=t, out_shards=N_CHIPS)
    dxn2 = _mm("mm_dxn2", du, p["w_ff1"], mode="nt", m=t, n=d, k=4 * d, b_shards=N_CHIPS)
    dh1, g["norm_mlp_g"] = _rms_bwd("rms_mlp_bwd", s["h1"], gated(p["norm_mlp_g"], "mlp"), dxn2, dh2, t, d)
    dy = _mm("mm_dy", dh1, p["w_out"], mode="nt", m=t, n=d, k=d)
    g["w_out"] = _mm("mm_dw_out", s["y"], dh1, mode="tn", m=d, n=d, k=t)
    da_fox, da_gla, dgates = _merge_bwd(dy, s["a_fox"], s["a_gla"], s["proj"], lay.c_gates, t, d)
    g["w_o_fox"] = _mm("mm_dw_o_fox", s["o_fox"], da_fox, mode="tn", m=lay.fw, n=d, k=t, out_shards=N_CHIPS)
    do_fox = _mm("mm_do_fox", da_fox, p["w_o_fox"], mode="nt", m=t, n=lay.fw, k=d, b_shards=N_CHIPS)
    g["w_o_gla"] = _mm("mm_dw_o_gla", s["o_gla"], da_gla, mode="tn", m=lay.gv, n=d, k=t)
    do_gla = _mm("mm_do_gla", da_gla, p["w_o_gla"], mode="nt", m=t, n=lay.gv, k=d)
    do_raw, dgr, g["gla_norm_g"] = _gla_post_bwd(s["o_raw"], s["proj"], gated(p["gla_norm_g"], "out"), do_gla, t,
                                                 lay.dv, lay.c_gr)
    dgq, dgk, dgv, dglog = _gla_bwd(s["proj"], s["glog"], s["s_prev"], do_raw, t, lay.dk, lay.dv,
                                    lay.c_gq, lay.c_gk, lay.c_gv)
    dz, g["b_alpha"] = _gla_gate_bwd(dglog, s["small"], p["w_alpha_p"], p["b_alpha"], t, lay.gk)
    g["w_alpha_p"] = _mm("mm_dw_alpha", s["small"], dz, mode="tn", m=LANES, n=lay.gk, k=t)
    dga = _mm("mm_dga", dz, p["w_alpha_p"], mode="nt", m=t, n=LANES, k=lay.gk)
    delta = _fox_delta(s["o_fox"], do_fox, t, lay.fd)
    dfq, dfkv, dc, dr = _fox_bwd(s["proj"], s["c_col"], s["c_row"], s["lse"], delta, do_fox, t, lay.fd,
                                 lay.c_fq, lay.c_fkv)
    dc_p = jnp.pad((dc[:, 0, :] + dr[:, :, 0]).T, ((0, 0), (0, LANES - FOX_HEADS)))
    dsmall, g["b_forget_p"] = _fox_gate_bwd(dc_p, s["small"], p["b_forget_p"], dga, t)
    segs = [("fq", dfq, lay.c_fq), ("gq", dgq, lay.c_gq), ("gv", dgv, lay.c_gv), ("gr", dgr, lay.c_gr),
            ("fkv", dfkv, lay.c_fkv), ("gates", dgates, lay.c_gates), ("gk", dgk, lay.c_gk),
            ("small", dsmall, lay.c_small)]
    dw_in = {nm: _mm("mm_dw_in_" + nm, s["xn"], dseg, mode="tn", m=d, n=dseg.shape[1], k=t) for nm, dseg, _ in segs}
    g["w_in"] = lay.from_segments(dw_in)
    dxn = _mm("mm_dxn_small", dsmall, p["w_in"], mode="nt", m=t, n=d, k=LANES, b_c0=lay.c_small)
    dxn = _mm_grouped_nt("mm_dxn", [(dseg, c0) for _, dseg, c0 in segs[:-1]], p["w_in"], dxn, m=t, n=d)
    dh, g["norm_mix_g"] = _rms_bwd("rms_mix_bwd", s["h"], gated(p["norm_mix_g"], "in"), dxn, dh1, t, d)
    return dh, g


def _sequence_step(x, target, meta, layers, final_g):
    seq, d = x.shape
    t = seq + ROW0
    lay = _Layout(d)
    h = jnp.pad(x, ((ROW0, 0), (0, 0))).at[PAD:ROW0].set(meta)
    target_p = jnp.pad(target, ((ROW0, 0), (0, 0)))
    saved = []
    for p in layers:
        h, s = _layer_fwd(lay, h, p, t)
        saved.append(s)
    dh, dg_final, loss_part = _loss_head(h, final_g, target_p, t, d)
    grads = [None] * len(layers)
    for l in reversed(range(len(layers))):
        dh, grads[l] = _layer_bwd(lay, dh, layers[l], saved[l], t)
    return loss_part, dh[ROW0:], dh[PAD:ROW0], grads, dg_final


_SMALL_ROWS = 48


def _pack_small(d, meta, mix, gla, mlp, final, b_alpha, b_forget, w_alpha2):
    rows = [meta.reshape(N_META, d), mix.reshape(DEPTH, d), gla.reshape(DEPTH, d), mlp.reshape(DEPTH, d),
            final.reshape(1, d), b_alpha.reshape(1, d),
            jnp.pad(b_forget.reshape(1, DEPTH * FOX_HEADS), ((0, 0), (0, d - DEPTH * FOX_HEADS))),
            jnp.zeros((7, d), F32), w_alpha2.reshape(GLA_RANK, d)]
    return jnp.concatenate(rows, axis=0)


def _unpack_small(d, packed):
    return dict(meta=packed[:N_META], norm_mix_g=packed[16:18], gla_norm_g=packed[18:20], norm_mlp_g=packed[20:22],
                final_norm_g=packed[22], b_alpha=packed[23].reshape(DEPTH, d // 2),
                b_forget=packed[24, :DEPTH * FOX_HEADS].reshape(DEPTH, FOX_HEADS),
                w_alpha2=packed[32:48].reshape(DEPTH, GLA_RANK, d // 2))


_BIG = ("w_in", "w_o_fox", "w_o_gla", "w_out", "w_ff1", "w_ff2")
_COL_SHARDED = ("w_in", "w_o_fox", "w_ff1")


def _full_matrix(name, gathered_l):
    nch, r, c = gathered_l.shape
    if name in _COL_SHARDED:
        return gathered_l.transpose(1, 0, 2).reshape(r, nch * c)
    return gathered_l.reshape(nch * r, c)


def _shard_major(name, full):
    r, c = full.shape
    if name in _COL_SHARDED:
        return full.reshape(r, N_CHIPS, c // N_CHIPS).transpose(1, 0, 2)
    return full.reshape(N_CHIPS, r // N_CHIPS, c)


def kernel(x, meta_tokens, norm_mix_g, w_in, b_forget, w_alpha2, b_alpha, gla_norm_g, w_o_fox, w_o_gla, w_out, norm_mlp_g, w_ff1, w_ff2, final_norm_g, loss_target, m_meta_tokens, m_norm_mix_g, m_w_in, m_b_forget, m_w_alpha2, m_b_alpha, m_gla_norm_g, m_w_o_fox, m_w_o_gla, m_w_out, m_norm_mlp_g, m_w_ff1, m_w_ff2, m_final_norm_g, v_meta_tokens, v_norm_mix_g, v_w_in, v_b_forget, v_w_alpha2, v_b_alpha, v_gla_norm_g, v_w_o_fox, v_w_o_gla, v_w_out, v_norm_mlp_g, v_w_ff1, v_w_ff2, v_final_norm_g):
    d = x.shape[2]
    lay = _Layout(d)
    xi, yi, ci = lax.axis_index("x"), lax.axis_index("y"), lax.axis_index("c")
    chip = (2 * xi + yi).astype(jnp.int32)
    w = dict(w_in=w_in, w_alpha2=w_alpha2, w_o_fox=w_o_fox, w_o_gla=w_o_gla, w_out=w_out, w_ff1=w_ff1, w_ff2=w_ff2)
    m = dict(w_in=m_w_in, w_alpha2=m_w_alpha2, w_o_fox=m_w_o_fox, w_o_gla=m_w_o_gla, w_out=m_w_out, w_ff1=m_w_ff1,
             w_ff2=m_w_ff2)
    v = dict(w_in=v_w_in, w_alpha2=v_w_alpha2, w_o_fox=v_w_o_fox, w_o_gla=v_w_o_gla, w_out=v_w_out, w_ff1=v_w_ff1,
             w_ff2=v_w_ff2)

    seq = x.shape[1]
    t = seq + ROW0
    core_idx = ci.astype(jnp.int32)[None]
    chip_idx = chip[None]

    cols = d // N_CHIPS
    small_w = jnp.concatenate([meta_tokens, w_alpha2.reshape(-1, cols)], axis=0)
    small_raw = _all_gather_small(small_w)
    small_all = small_raw[0::2]
    alpha_full = small_all[:, N_META:].reshape(N_CHIPS, DEPTH, GLA_RANK, lay.gk // N_CHIPS)
    alpha_full = alpha_full.transpose(1, 2, 0, 3).reshape(DEPTH, GLA_RANK, lay.gk)
    groups = [(0, ("w_in",)), (0, _BIG[1:]), (1, _BIG)]
    started, after = [], small_raw
    for gi, (l, names) in enumerate(groups):
        own16 = [w[n][l].astype(BF16) for n in names]
        lands = [lax.empty((N_CHIPS,) + o.shape, BF16) for o in own16]
        started.append(_split_start("gather_start_%d" % gi, own16, lands, _gather_plan, after=after))
        after = started[gi][4]
    meta_full = small_all[:, :N_META].transpose(1, 0, 2).reshape(N_META, d) + after[0, 0]

    def gathered(gi, after):
        send_sems, recv_sems, srcs, lands, _ = started[gi]
        srcs, lands = _split_wait("gather_wait_%d" % gi, send_sems, recv_sems, srcs, lands, _gather_plan, after)
        lands = _forward_halves("gather_forward_%d" % gi, lands)
        return {n: lax.dynamic_update_slice(g, o[None], (chip, 0, 0)) for n, g, o in zip(groups[gi][1], lands, srcs)}

    def early_weights(l, gl):
        w_alpha_p = jnp.zeros((LANES, lay.gk), BF16).at[FOX_HEADS:FOX_HEADS + GLA_RANK].set(
            alpha_full[l].astype(BF16))
        return dict(
            w_in=lay.to_p(gl["w_in"]), w_alpha_p=w_alpha_p,
            norm_mix_g=norm_mix_g[l][None], norm_mlp_g=norm_mlp_g[l][None], gla_norm_g=gla_norm_g[l][None],
            b_alpha=b_alpha[l][None],
            b_forget_p=jnp.pad(b_forget[l][None], ((0, 0), (0, LANES - FOX_HEADS))))

    def late_weights(gl):
        return dict(w_o_fox=gl["w_o_fox"], w_o_gla=_full_matrix("w_o_gla", gl["w_o_gla"]),
                    w_out=_full_matrix("w_out", gl["w_out"]), w_ff1=gl["w_ff1"],
                    w_ff2=_full_matrix("w_ff2", gl["w_ff2"]))

    h = jnp.pad(x[0], ((ROW0, 0), (0, 0))).at[PAD:ROW0].set(meta_full)
    layers, saved = [], []
    layers.append(early_weights(0, gathered(0, after=h)))
    h, s = _layer_fwd(lay, h, layers[0], t, late=lambda after: late_weights(gathered(1, after)))
    saved.append(s)
    gl = gathered(2, after=h)
    layers.append({**early_weights(1, gl), **late_weights(gl)})
    h, s = _layer_fwd(lay, h, layers[1], t)
    saved.append(s)
    dh, dg_final, loss_part = _loss_head(h, final_norm_g[None], jnp.pad(loss_target[0], ((ROW0, 0), (0, 0))), t, d)
    loss = lax.psum(loss_part[0, 0], ("x", "y", "c"))

    def partial_of(g, n):
        return g[n] if n in ("w_ff1", "w_o_fox", "w_in") else _shard_major(n, g[n])

    scatter_groups = dict(mlp=("w_ff1", "w_ff2"), out=("w_o_fox", "w_o_gla", "w_out"))
    scatter_groups["in"] = ("w_in",)
    swapping, scattered = [], {}

    def start_swap(l, grp, g, after=None):
        parts = [partial_of(g, n) for n in scatter_groups[grp]]
        lands = [lax.empty((p_.shape[0], p_.shape[1] // 2, p_.shape[2]), F32) for p_ in parts]
        started_swap = _split_start("swap_start_%d_%s" % (l, grp), parts, lands, _swap_plan, after=after)
        swapping.append((l, grp, started_swap))
        return started_swap[4]

    def start_scatter(after):
        l, grp, (send_sems, recv_sems, srcs, lands, _) = swapping.pop(0)
        names = scatter_groups[grp]
        tag = "%d_%s" % (l, grp)
        parts, from_sibling = _split_wait("swap_wait_" + tag, send_sems, recv_sems, srcs, lands, _swap_plan, after)
        sums = [_pair_sum("pair_sum_%d_%s" % (l, n), p_, r_, core_idx) for n, p_, r_ in zip(names, parts, from_sibling)]
        lands = [lax.empty(s16.shape, BF16) for _, s16 in sums]
        send_sems, recv_sems, srcs, lands, token = _split_start("scatter_start_" + tag, [s16 for _, s16 in sums], lands,
                                                                _scatter_plan)
        scattered[l, grp] = (send_sems, recv_sems, srcs, lands, [s32 for s32, _ in sums])
        return token

    def gate(l, grp, g):
        token = start_swap(l, grp, g)
        if len(swapping) > 1:
            token = start_scatter(after=token)
        return token[0, 0]

    def gates_for(l, points):
        return {grp: functools.partial(gate, l, grp) for grp in points}

    grads = [None] * DEPTH
    dh, grads[1] = _layer_bwd(lay, dh, layers[1], saved[1], t, gates=gates_for(1, ("mlp", "out", "in")))
    dh, grads[0] = _layer_bwd(lay, dh, layers[0], saved[0], t, gates=gates_for(0, ("mlp", "out")))
    grad_x, d_meta = dh[ROW0:], dh[PAD:ROW0]

    stack = lambda key: jnp.concatenate([grads[l][key] for l in range(DEPTH)], axis=0)
    b_forget_g = jnp.concatenate([grads[l]["b_forget_p"][:, :FOX_HEADS] for l in range(DEPTH)], axis=0)
    alpha_g = jnp.stack([grads[l]["w_alpha_p"][FOX_HEADS:FOX_HEADS + GLA_RANK] for l in range(DEPTH)])
    packed = _pack_small(d, d_meta, stack("norm_mix_g"), stack("gla_norm_g"), stack("norm_mlp_g"), dg_final,
                         stack("b_alpha"), b_forget_g, alpha_g)
    small_g = _unpack_small(d, _sum_devices(_all_gather_small(packed)))
    small_g["meta"] = lax.dynamic_slice_in_dim(small_g["meta"], chip * (d // N_CHIPS), d // N_CHIPS, axis=1)
    alpha_shard = lax.dynamic_slice_in_dim(small_g["w_alpha2"], chip * (lay.gk // N_CHIPS), lay.gk // N_CHIPS, axis=2)

    after = start_swap(0, "in", grads[0], after=small_g["final_norm_g"])
    while swapping:
        after = start_scatter(after)
    outs = {n: None for n in _BIG}
    for l in reversed(range(DEPTH)):
        names, layer_g = [], []
        for grp in ("mlp", "out", "in"):
            send_sems, recv_sems, srcs, lands, sums32 = scattered[l, grp]
            _, lands = _split_wait("scatter_wait_%d_%s" % (l, grp), send_sems, recv_sems, srcs, lands, _scatter_plan,
                                 after)
            names += scatter_groups[grp]
            layer_g += [_chip_sum("chip_sum_%d_%s" % (l, n), s32, r_, chip_idx, core_idx)
                        for n, s32, r_ in zip(scatter_groups[grp], sums32, lands)]
        layer_g = _share_halves("share_halves_%d" % l, layer_g)
        for n, g in zip(names, layer_g):
            outs[n] = _adamw("adamw_%d_%s" % (l, n), w[n], g, m[n], v[n], layer=l, into=outs[n])
        after = outs[names[-1]][0]
    out_g, out_d, out_m, out_v = {}, {}, {}, {}
    for n in _BIG:
        out_g[n], out_d[n], out_m[n], out_v[n] = outs[n]
    out_g["w_alpha2"], out_d["w_alpha2"], out_m["w_alpha2"], out_v["w_alpha2"] = _adamw(
        "adamw_w_alpha2", w["w_alpha2"], alpha_shard, m["w_alpha2"], v["w_alpha2"])
    sm_w = dict(meta_tokens=meta_tokens, norm_mix_g=norm_mix_g, b_forget=b_forget, b_alpha=b_alpha,
                gla_norm_g=gla_norm_g, norm_mlp_g=norm_mlp_g, final_norm_g=final_norm_g)
    sm_m = dict(meta_tokens=m_meta_tokens, norm_mix_g=m_norm_mix_g, b_forget=m_b_forget, b_alpha=m_b_alpha,
                gla_norm_g=m_gla_norm_g, norm_mlp_g=m_norm_mlp_g, final_norm_g=m_final_norm_g)
    sm_v = dict(meta_tokens=v_meta_tokens, norm_mix_g=v_norm_mix_g, b_forget=v_b_forget, b_alpha=v_b_alpha,
                gla_norm_g=v_gla_norm_g, norm_mlp_g=v_norm_mlp_g, final_norm_g=v_final_norm_g)
    sm_g = dict(meta_tokens=small_g["meta"], norm_mix_g=small_g["norm_mix_g"], b_forget=small_g["b_forget"],
                b_alpha=small_g["b_alpha"], gla_norm_g=small_g["gla_norm_g"], norm_mlp_g=small_g["norm_mlp_g"],
                final_norm_g=small_g["final_norm_g"])
    names_small = list(sm_w)
    sizes = [sm_w[n].size for n in names_small]
    width = 512
    total = -(-sum(sizes) // (8 * width)) * (8 * width)

    def pack_flat(dct, fill):
        flat = jnp.concatenate([dct[n].reshape(-1) for n in names_small])
        return jnp.pad(flat, (0, total - flat.shape[0]), constant_values=fill).reshape(1, -1, width)

    res = _adamw("adamw_small", pack_flat(sm_w, 0.0), pack_flat(sm_g, 0.0), pack_flat(sm_m, 0.0), pack_flat(sm_v, 1.0))
    offs = [0]
    for sz in sizes:
        offs.append(offs[-1] + sz)
    for i, n in enumerate(names_small):
        out_g[n] = sm_g[n].reshape(sm_w[n].shape)
        out_d[n], out_m[n], out_v[n] = [r.reshape(-1)[offs[i]:offs[i + 1]].reshape(sm_w[n].shape) for r in res[1:]]

    order = ["meta_tokens", "norm_mix_g", "w_in", "b_forget", "w_alpha2", "b_alpha", "gla_norm_g", "w_o_fox",
             "w_o_gla", "w_out", "norm_mlp_g", "w_ff1", "w_ff2", "final_norm_g"]
    return (loss, grad_x[None], *[out_g[n] for n in order], *[out_d[n] for n in order],
            *[out_m[n] for n in order], *[out_v[n] for n in order])
```

```python
import functools

import numpy as np

import jax
import jax.numpy as jnp
from jax import lax
from jax.experimental import pallas as pl
from jax.experimental.pallas import tpu as pltpu

F32 = jnp.float32
BF16 = jnp.bfloat16

N_META = 16
PAD = 112
ROW0 = PAD + N_META
EPS = 1e-6
MASK_VALUE = -1e30
FOX_HEADS = 8
FOX_GROUP = 2
GLA_HEADS = 4
GLA_RANK = 16
GLA_TAU = 16.0
GLA_CHUNK = 64
DEPTH = 2
N_CHIPS = 4
N_DEV = 8

ADAM_LR = 0.001
ADAM_B1 = 0.9
ADAM_B2 = 0.999
ADAM_EPS = 1e-08
ADAM_WD = 0.01
ADAM_STEP = 10

LANES = 128
VMEM_LIMIT = 56 * 1024 * 1024
MESH = pl.DeviceIdType.MESH


def _pick(n, target, mult):
    best = None
    for d in range(mult, min(n, target) + 1, mult):
        if n % d == 0:
            best = d
    return n if best is None else best


def _params(sem=None):
    return pltpu.CompilerParams(dimension_semantics=sem, vmem_limit_bytes=VMEM_LIMIT)


def _bf(v):
    return v if v.dtype == BF16 else v.astype(BF16)


def _sigmoid(z):
    return 1.0 / (1.0 + jnp.exp(-z))


def _log_sigmoid(z):
    return jnp.minimum(z, 0.0) - jnp.log(1.0 + jnp.exp(-jnp.abs(z)))


def _split3(v):
    a = v.astype(BF16)
    r = v - a.astype(F32)
    b = r.astype(BF16)
    c = (r - b.astype(F32)).astype(BF16)
    return a, b, c


def _dot(a, b, dims):
    return lax.dot_general(a, b, (dims, ((), ())), preferred_element_type=F32)


NN = ((1,), (0,))
NT = ((1,), (1,))
TN = ((0,), (0,))


def _tri_dot(tri, v, dims=NN):
    a, b, c = _split3(v)
    return _dot(tri, a, dims) + _dot(tri, b, dims) + _dot(tri, c, dims)


def _mm(name, a, b, *, mode, m, n, k, b_c0=0, extras=(), epilogue=None, out_dtypes=(F32,),
        b_shards=1, out_shards=1, tm=1056, tn=1024, tk=2048):
    tm = _pick(m, tm, LANES if mode == "tn" else 16)
    tn = _pick(n // max(b_shards if mode == "nn" else 1, out_shards), tn, LANES)
    if mode == "tn":
        tk = _pick(k, 2112, 16)
    else:
        tk = _pick(k // (b_shards if mode == "nt" else 1), tk, LANES)
    assert b_c0 % (tk if mode == "nt" else tn) == 0 and (b_shards == 1 or b_c0 == 0)
    nk = k // tk
    if mode == "tn":
        a_spec = pl.BlockSpec((tk, tm), lambda i, j, kk: (kk, i))
    else:
        a_spec = pl.BlockSpec((tm, tk), lambda i, j, kk: (i, kk))
    if mode == "nt":
        dims = NT
        if b_shards > 1:
            per = (k // b_shards) // tk
            b_spec = pl.BlockSpec((None, tn, tk), lambda i, j, kk: (kk // per, j, kk % per))
        else:
            b_spec = pl.BlockSpec((tn, tk), lambda i, j, kk: (j, kk + b_c0 // tk))
    else:
        dims = NN if mode == "nn" else TN
        if b_shards > 1:
            per = (n // b_shards) // tn
            b_spec = pl.BlockSpec((None, tk, tn), lambda i, j, kk: (j // per, kk, j % per))
        else:
            b_spec = pl.BlockSpec((tk, tn), lambda i, j, kk: (kk, j + b_c0 // tn))
    ex_specs = [pl.BlockSpec((tm, tn), lambda i, j, kk: (i, j)) for _ in extras]
    if out_shards > 1:
        oper = (n // out_shards) // tn
        out_specs = [pl.BlockSpec((None, tm, tn), lambda i, j, kk: (j // oper, i, j % oper)) for _ in out_dtypes]
        out_shape = [jax.ShapeDtypeStruct((out_shards, m, n // out_shards), dt) for dt in out_dtypes]
    else:
        out_specs = [pl.BlockSpec((tm, tn), lambda i, j, kk: (i, j)) for _ in out_dtypes]
        out_shape = [jax.ShapeDtypeStruct((m, n), dt) for dt in out_dtypes]
    n_ex = len(extras)
    n_out = len(out_dtypes)

    def finish(acc, ex_refs, out_refs):
        vals = (acc,) if epilogue is None else epilogue(acc, *[r[...] for r in ex_refs])
        for r, v in zip(out_refs, vals):
            r[...] = v.astype(r.dtype)

    def body(a_ref, b_ref, *rest):
        ex_refs = rest[:n_ex]
        out_refs = rest[n_ex:n_ex + n_out]
        prod = _dot(_bf(a_ref[...]), _bf(b_ref[...]), dims)
        if nk == 1:
            finish(prod, ex_refs, out_refs)
            return
        acc_ref = rest[n_ex + n_out]
        kk = pl.program_id(2)

        @pl.when(kk == 0)
        def _():
            acc_ref[...] = prod

        @pl.when((kk > 0) & (kk < nk - 1))
        def _():
            acc_ref[...] += prod

        @pl.when(kk == nk - 1)
        def _():
            finish(acc_ref[...] + prod, ex_refs, out_refs)

    outs = pl.pallas_call(
        body,
        name=name,
        grid=(m // tm, n // tn, nk),
        in_specs=[a_spec, b_spec] + ex_specs,
        out_specs=out_specs,
        out_shape=out_shape,
        scratch_shapes=[pltpu.VMEM((tm, tn), F32)] if nk > 1 else [],
        compiler_params=_params(("parallel", "parallel", "arbitrary")),
    )(a, b, *extras)
    return outs[0] if n_out == 1 else outs


def _ew(name, fn, ins, outs, rows, tm):
    tm = _pick(rows, tm, 16)
    in_specs, args = [], []
    for spec in ins:
        if spec[0] == "tile":
            _, arr, width, c0 = spec
            assert c0 % width == 0
            in_specs.append(pl.BlockSpec((tm, width), functools.partial(lambda i, o: (i, o), o=c0 // width)))
        else:
            arr = spec[1]
            in_specs.append(pl.BlockSpec(arr.shape, lambda i: (0, 0)))
        args.append(arr)
    out_specs, out_shape = [], []
    for kind, dt, width in outs:
        if kind == "tile":
            out_specs.append(pl.BlockSpec((tm, width), lambda i: (i, 0)))
            out_shape.append(jax.ShapeDtypeStruct((rows, width), dt))
        else:
            out_specs.append(pl.BlockSpec((1, width), lambda i: (0, 0)))
            out_shape.append(jax.ShapeDtypeStruct((1, width), dt))
    n_in = len(ins)
    has_acc = any(o[0] == "acc" for o in outs)

    def body(*refs):
        i = pl.program_id(0)
        vals = fn(i * tm, *[r[...] for r in refs[:n_in]])
        for (kind, _, _), r, v in zip(outs, refs[n_in:], vals):
            if kind == "tile":
                r[...] = v.astype(r.dtype)
            else:
                @pl.when(i == 0)
                def _():
                    r[...] = jnp.zeros_like(r)

                r[...] += v.astype(r.dtype)

    res = pl.pallas_call(
        body,
        name=name,
        grid=(rows // tm,),
        in_specs=in_specs,
        out_specs=out_specs,
        out_shape=out_shape,
        compiler_params=_params(("arbitrary",) if has_acc else ("parallel",)),
    )(*args)
    return res[0] if len(outs) == 1 else res


def _row_ids(row0, tm):
    return row0 + lax.broadcasted_iota(jnp.int32, (tm, 1), 0)


def _colsum(v):
    return jnp.sum(v, axis=0, keepdims=True)


def _rms_fwd_t(name, h, g, t, d):
    tm = _pick(t, 384, LANES)

    def body(x_ref, g_ref, y_ref, yt_ref):
        x = x_ref[...]
        y = x * lax.rsqrt(jnp.mean(x * x, axis=-1, keepdims=True) + EPS) * g_ref[...]
        y_ref[...] = y.astype(BF16)
        yt_ref[...] = y.T.astype(BF16)

    return pl.pallas_call(
        body, name=name, grid=(t // tm,),
        in_specs=[pl.BlockSpec((tm, d), lambda i: (i, 0)), pl.BlockSpec((1, d), lambda i: (0, 0))],
        out_specs=[pl.BlockSpec((tm, d), lambda i: (i, 0)), pl.BlockSpec((d, tm), lambda i: (0, i))],
        out_shape=[jax.ShapeDtypeStruct((t, d), BF16), jax.ShapeDtypeStruct((d, t), BF16)],
        compiler_params=_params(("parallel",)),
    )(h, g)


def _rms_bwd(name, h, g, dy, dres, t, d):
    def fn(row0, x, gg, dyv, dr):
        r = lax.rsqrt(jnp.mean(x * x, axis=-1, keepdims=True) + EPS)
        xh = x * r
        dxh = dyv * gg
        dx = r * (dxh - xh * jnp.mean(dxh * xh, axis=-1, keepdims=True))
        out = jnp.where(_row_ids(row0, x.shape[0]) >= PAD, dr + dx, 0.0)
        return out, _colsum(dyv * xh)

    return _ew(name, fn, [("tile", h, d, 0), ("full", g), ("tile", dy, d, 0), ("tile", dres, d, 0)],
               [("tile", F32, d), ("acc", F32, d)], t, 264)


def _loss_head(h, g, target_p, t, d):
    def fn(row0, x, gg, tgt):
        real = _row_ids(row0, x.shape[0]) >= ROW0
        r = lax.rsqrt(jnp.mean(x * x, axis=-1, keepdims=True) + EPS)
        xh = x * r
        err = jnp.where(real, xh * gg - tgt, 0.0)
        loss_rows = 0.5 * jnp.mean(err * err, axis=-1, keepdims=True)
        dyv = err * (1.0 / d)
        dxh = dyv * gg
        dx = r * (dxh - xh * jnp.mean(dxh * xh, axis=-1, keepdims=True))
        loss_part = jnp.sum(loss_rows, axis=0, keepdims=True) * jnp.ones((1, LANES), F32)
        return jnp.where(real, dx, 0.0), _colsum(dyv * xh), loss_part

    return _ew("loss_head", fn, [("tile", h, d, 0), ("full", g), ("tile", target_p, d, 0)],
               [("tile", F32, d), ("acc", F32, d), ("acc", F32, LANES)], t, 264)


def _merge_fwd(a_fox, a_gla, proj, c_gates, t, d):
    def fn(row0, af, ag, gates):
        gates = gates.astype(F32)
        return (_sigmoid(gates[:, :d]) * af + _sigmoid(gates[:, d:]) * ag,)

    return _ew("merge_fwd", fn, [("tile", a_fox, d, 0), ("tile", a_gla, d, 0), ("tile", proj, 2 * d, c_gates)],
               [("tile", BF16, d)], t, 264)


def _merge_bwd(dy, a_fox, a_gla, proj, c_gates, t, d):
    def fn(row0, dyv, af, ag, gates):
        gates = gates.astype(F32)
        sf = _sigmoid(gates[:, :d])
        sg = _sigmoid(gates[:, d:])
        dgates = jnp.concatenate([dyv * af * sf * (1.0 - sf), dyv * ag * sg * (1.0 - sg)], axis=1)
        return dyv * sf, dyv * sg, dgates

    return _ew("merge_bwd", fn,
               [("tile", dy, d, 0), ("tile", a_fox, d, 0), ("tile", a_gla, d, 0), ("tile", proj, 2 * d, c_gates)],
               [("tile", BF16, d), ("tile", BF16, d), ("tile", BF16, 2 * d)], t, 264)


def _fox_gate_fwd(small, b_forget_p, t):
    tb = _pick(t, 384, LANES)

    def body(s_ref, b_ref, c_ref, carry_ref):
        i = pl.program_id(0)

        @pl.when(i == 0)
        def _():
            carry_ref[...] = jnp.zeros_like(carry_ref)

        logf = _log_sigmoid(s_ref[...] + b_ref[...])
        logf = jnp.where(_row_ids(i * tb, tb) >= PAD, logf, 0.0)
        r = lax.broadcasted_iota(jnp.int32, (tb, tb), 0)
        c = lax.broadcasted_iota(jnp.int32, (tb, tb), 1)
        tri = (c <= r).astype(BF16)
        cs = _tri_dot(tri, logf) + carry_ref[...]
        c_ref[...] = cs
        carry_ref[...] = cs[tb - 1:tb, :]

    return pl.pallas_call(
        body, name="fox_gate_fwd", grid=(t // tb,),
        in_specs=[pl.BlockSpec((tb, LANES), lambda i: (i, 0)), pl.BlockSpec((1, LANES), lambda i: (0, 0))],
        out_specs=pl.BlockSpec((tb, LANES), lambda i: (i, 0)),
        out_shape=jax.ShapeDtypeStruct((t, LANES), F32),
        scratch_shapes=[pltpu.VMEM((1, LANES), F32)],
        compiler_params=_params(("arbitrary",)),
    )(small, b_forget_p)


def _fox_gate_bwd(dc, small, b_forget_p, dga, t):
    tb = _pick(t, 384, LANES)
    nb = t // tb

    def body(dc_ref, s_ref, b_ref, dga_ref, ds_ref, db_ref, carry_ref):
        i = pl.program_id(0)

        @pl.when(i == 0)
        def _():
            carry_ref[...] = jnp.zeros_like(carry_ref)
            db_ref[...] = jnp.zeros_like(db_ref)

        r = lax.broadcasted_iota(jnp.int32, (tb, tb), 0)
        c = lax.broadcasted_iota(jnp.int32, (tb, tb), 1)
        tri = (c >= r).astype(BF16)
        dlogf = _tri_dot(tri, dc_ref[...]) + carry_ref[...]
        carry_ref[...] = dlogf[0:1, :]
        z = s_ref[...] + b_ref[...]
        dff = dlogf * _sigmoid(-z)
        lane = lax.broadcasted_iota(jnp.int32, (tb, LANES), 1)
        keep = (_row_ids((nb - 1 - i) * tb, tb) >= PAD) & (lane < FOX_HEADS)
        dff = jnp.where(keep, dff, 0.0)
        ds_ref[...] = dff + dga_ref[...]
        db_ref[...] += _colsum(dff)

    rev = lambda i: (nb - 1 - i, 0)
    return pl.pallas_call(
        body, name="fox_gate_bwd", grid=(nb,),
        in_specs=[pl.BlockSpec((tb, LANES), rev), pl.BlockSpec((tb, LANES), rev),
                  pl.BlockSpec((1, LANES), lambda i: (0, 0)), pl.BlockSpec((tb, LANES), rev)],
        out_specs=[pl.BlockSpec((tb, LANES), rev), pl.BlockSpec((1, LANES), lambda i: (0, 0))],
        out_shape=[jax.ShapeDtypeStruct((t, LANES), F32), jax.ShapeDtypeStruct((1, LANES), F32)],
        scratch_shapes=[pltpu.VMEM((1, LANES), F32)],
        compiler_params=_params(("arbitrary",)),
    )(dc, small, b_forget_p, dga)


def _fox_pairs(nb, by_key):
    if by_key:
        pairs = [(qi, ki) for ki in range(nb) for qi in range(ki, nb)]
    else:
        pairs = [(qi, ki) for qi in range(nb) for ki in range(qi + 1)]
    return (jnp.asarray(np.array([p[0] for p in pairs], np.int32)),
            jnp.asarray(np.array([p[1] for p in pairs], np.int32)), len(pairs))


def _fox_specs(tb, fd, c_fq, c_fkv):
    gw = FOX_GROUP * fd
    q0, kv0 = c_fq // gw, c_fkv // (2 * gw)
    return dict(
        q=pl.BlockSpec((tb, gw), lambda g, p, qt, kt: (qt[p], q0 + g)),
        kv=pl.BlockSpec((tb, 2 * gw), lambda g, p, qt, kt: (kt[p], kv0 + g)),
        col=pl.BlockSpec((FOX_GROUP, tb, 1), lambda g, p, qt, kt: (g, qt[p], 0)),
        row=pl.BlockSpec((FOX_GROUP, 1, tb), lambda g, p, qt, kt: (g, 0, kt[p])),
        head=pl.BlockSpec((tb, gw), lambda g, p, qt, kt: (qt[p], g)),
        key_kv=pl.BlockSpec((tb, 2 * gw), lambda g, p, qt, kt: (kt[p], g)),
    )


def _fox_mask(qi, ki, tb):
    row = qi * tb + lax.broadcasted_iota(jnp.int32, (tb, tb), 0)
    col = ki * tb + lax.broadcasted_iota(jnp.int32, (tb, tb), 1)
    return (col <= row) & (col >= PAD)


def _fox_heads(q_ref, kv_ref, fd):
    return [(q_ref[:, hh * fd:(hh + 1) * fd], kv_ref[:, 2 * hh * fd:(2 * hh + 1) * fd],
             kv_ref[:, (2 * hh + 1) * fd:(2 * hh + 2) * fd]) for hh in range(FOX_GROUP)]


def _fox_fwd(proj, c_col, c_row, t, fd, c_fq, c_fkv):
    tb = _pick(t, 384, LANES)
    nb = t // tb
    scale = fd ** -0.5
    sp = _fox_specs(tb, fd, c_fq, c_fkv)
    qt, kt, npairs = _fox_pairs(nb, by_key=False)

    def body(qt_ref, kt_ref, q_ref, kv_ref, cq_ref, ck_ref, o_ref, lse_ref, m_ref, l_ref, acc_ref):
        p = pl.program_id(1)
        qi, ki = qt_ref[p], kt_ref[p]

        @pl.when(ki == 0)
        def _():
            m_ref[...] = jnp.full_like(m_ref, -jnp.inf)
            l_ref[...] = jnp.zeros_like(l_ref)
            acc_ref[...] = jnp.zeros_like(acc_ref)

        def update(masked):
            mask = _fox_mask(qi, ki, tb) if masked else None
            for hh, (q, k, v) in enumerate(_fox_heads(q_ref, kv_ref, fd)):
                s = _dot(q, k, NT) * scale + cq_ref[hh] - ck_ref[hh]
                if masked:
                    s = jnp.where(mask, s, MASK_VALUE)
                m_prev = m_ref[hh]
                m_new = jnp.maximum(m_prev, jnp.max(s, axis=-1, keepdims=True))
                alpha = jnp.exp(m_prev - m_new)
                pe = jnp.exp(s - m_new)
                l_ref[hh] = alpha * l_ref[hh] + jnp.sum(pe, axis=-1, keepdims=True)
                acc_ref[hh] = alpha * acc_ref[hh] + _dot(pe.astype(BF16), v, NN)
                m_ref[hh] = m_new

        edge = (ki == 0) | (ki == qi)
        pl.when(edge)(functools.partial(update, True))
        pl.when(jnp.logical_not(edge))(functools.partial(update, False))

        @pl.when(ki == qi)
        def _():
            real = _row_ids(qi * tb, tb) >= PAD
            for hh in range(FOX_GROUP):
                o_ref[:, hh * fd:(hh + 1) * fd] = jnp.where(real, acc_ref[hh] / l_ref[hh], 0.0)
                lse_ref[hh] = m_ref[hh] + jnp.log(l_ref[hh])

    return pl.pallas_call(
        body, name="fox_fwd",
        grid_spec=pltpu.PrefetchScalarGridSpec(
            num_scalar_prefetch=2, grid=(FOX_HEADS // FOX_GROUP, npairs),
            in_specs=[sp["q"], sp["kv"], sp["col"], sp["row"]],
            out_specs=[sp["head"], sp["col"]],
            scratch_shapes=[pltpu.VMEM((FOX_GROUP, tb, 1), F32), pltpu.VMEM((FOX_GROUP, tb, 1), F32),
                            pltpu.VMEM((FOX_GROUP, tb, fd), F32)]),
        out_shape=[jax.ShapeDtypeStruct((t, FOX_HEADS * fd), F32), jax.ShapeDtypeStruct((FOX_HEADS, t, 1), F32)],
        compiler_params=_params(("parallel", "arbitrary")),
    )(qt, kt, proj, proj, c_col, c_row)


def _fox_delta(o_fox, do_fox, t, fd):
    tb = _pick(t, 384, LANES)

    def body(o_ref, do_ref, out_ref):
        for h in range(FOX_HEADS):
            sl = slice(h * fd, (h + 1) * fd)
            out_ref[h] = jnp.sum(o_ref[:, sl] * do_ref[:, sl].astype(BF16).astype(F32), axis=-1, keepdims=True)

    w = FOX_HEADS * fd
    return pl.pallas_call(
        body, name="fox_delta", grid=(t // tb,),
        in_specs=[pl.BlockSpec((tb, w), lambda i: (i, 0)), pl.BlockSpec((tb, w), lambda i: (i, 0))],
        out_specs=pl.BlockSpec((FOX_HEADS, tb, 1), lambda i: (0, i, 0)),
        out_shape=jax.ShapeDtypeStruct((FOX_HEADS, t, 1), F32),
        compiler_params=_params(("parallel",)),
    )(o_fox, do_fox)


def _fox_bwd(proj, c_col, c_row, lse, delta, do_fox, t, fd, c_fq, c_fkv):
    tb = _pick(t, 384, LANES)
    nb = t // tb
    scale = fd ** -0.5
    sp = _fox_specs(tb, fd, c_fq, c_fkv)
    qt, kt, npairs = _fox_pairs(nb, by_key=True)
    gw = FOX_GROUP * fd

    def body(qt_ref, kt_ref, q_ref, kv_ref, cq_ref, ck_ref, lse_ref, dl_ref, do_ref, dq_ref, dkv_ref, dc_ref, dr_ref,
             dq_acc, dk_acc, dv_acc, dc_acc, dr_acc):
        p = pl.program_id(1)
        qi, ki = qt_ref[p], kt_ref[p]

        @pl.when(p == 0)
        def _():
            dq_acc[...] = jnp.zeros_like(dq_acc)
            dr_acc[...] = jnp.zeros_like(dr_acc)

        @pl.when(qi == ki)
        def _():
            dk_acc[...] = jnp.zeros_like(dk_acc)
            dv_acc[...] = jnp.zeros_like(dv_acc)
            dc_acc[...] = jnp.zeros_like(dc_acc)

        rows = pl.ds(pl.multiple_of(qi * tb, LANES), tb)

        def update(masked):
            mask = _fox_mask(qi, ki, tb) if masked else None
            for hh, (q, k, v) in enumerate(_fox_heads(q_ref, kv_ref, fd)):
                do = _bf(do_ref[:, hh * fd:(hh + 1) * fd])
                s = _dot(q, k, NT) * scale + cq_ref[hh] - ck_ref[hh]
                if masked:
                    s = jnp.where(mask, s, MASK_VALUE)
                pr = jnp.exp(s - lse_ref[hh])
                dp = _dot(do, v, NT)
                ds = pr * (dp - dl_ref[hh])
                ds16 = ds.astype(BF16)
                dv_acc[hh] += _dot(pr.astype(BF16), do, TN)
                dk_acc[hh] += _dot(ds16, q, TN)
                dc_acc[hh] += _colsum(ds)
                dr_acc[hh, rows, :] += jnp.sum(ds, axis=-1, keepdims=True)
                dq_acc[hh, rows, :] += _dot(ds16, k, NN)

        edge = (ki == 0) | (ki == qi)
        pl.when(edge)(functools.partial(update, True))
        pl.when(jnp.logical_not(edge))(functools.partial(update, False))

        @pl.when(qi == nb - 1)
        def _():
            for hh in range(FOX_GROUP):
                dkv_ref[:, 2 * hh * fd:(2 * hh + 1) * fd] = (dk_acc[hh] * scale).astype(dkv_ref.dtype)
                dkv_ref[:, (2 * hh + 1) * fd:(2 * hh + 2) * fd] = dv_acc[hh].astype(dkv_ref.dtype)
                dc_ref[hh] = -dc_acc[hh]

        @pl.when(p == npairs - 1)
        def _():
            for hh in range(FOX_GROUP):
                dq_ref[:, hh * fd:(hh + 1) * fd] = (dq_acc[hh] * scale).astype(dq_ref.dtype)
            dr_ref[...] = dr_acc[...]

    return pl.pallas_call(
        body, name="fox_bwd",
        grid_spec=pltpu.PrefetchScalarGridSpec(
            num_scalar_prefetch=2, grid=(FOX_HEADS // FOX_GROUP, npairs),
            in_specs=[sp["q"], sp["kv"], sp["col"], sp["row"], sp["col"], sp["col"], sp["head"]],
            out_specs=[pl.BlockSpec((t, gw), lambda g, p, qt, kt: (0, g)), sp["key_kv"], sp["row"],
                       pl.BlockSpec((FOX_GROUP, t, 1), lambda g, p, qt, kt: (g, 0, 0))],
            scratch_shapes=[pltpu.VMEM((FOX_GROUP, t, fd), F32), pltpu.VMEM((FOX_GROUP, tb, fd), F32),
                            pltpu.VMEM((FOX_GROUP, tb, fd), F32), pltpu.VMEM((FOX_GROUP, 1, tb), F32),
                            pltpu.VMEM((FOX_GROUP, t, 1), F32)]),
        out_shape=[jax.ShapeDtypeStruct((t, FOX_HEADS * fd), BF16), jax.ShapeDtypeStruct((t, 2 * FOX_HEADS * fd), BF16),
                   jax.ShapeDtypeStruct((FOX_HEADS, 1, t), F32), jax.ShapeDtypeStruct((FOX_HEADS, t, 1), F32)],
        compiler_params=_params(("parallel", "arbitrary")),
    )(qt, kt, proj, proj, c_col, c_row, lse, delta, do_fox)


def _gla_gate_fwd(small, w_alpha_p, b_alpha, t, gk):
    def fn(row0, s, w, b):
        z = _dot(s.astype(BF16), w, NN) + b
        return (jnp.where(_row_ids(row0, s.shape[0]) >= PAD, _log_sigmoid(z) * (1.0 / GLA_TAU), 0.0),)

    return _ew("gla_gate_fwd", fn, [("tile", small, LANES, 0), ("full", w_alpha_p), ("full", b_alpha)],
               [("tile", F32, gk)], t, 264)


def _gla_gate_bwd(dglog, small, w_alpha_p, b_alpha, t, gk):
    def fn(row0, dg, s, w, b):
        z = _dot(s.astype(BF16), w, NN) + b
        dz = jnp.where(_row_ids(row0, s.shape[0]) >= PAD, dg * (1.0 / GLA_TAU) * _sigmoid(-z), 0.0)
        return dz, _colsum(dz)

    return _ew("gla_gate_bwd", fn,
               [("tile", dglog, gk, 0), ("tile", small, LANES, 0), ("full", w_alpha_p), ("full", b_alpha)],
               [("tile", BF16, gk), ("acc", F32, gk)], t, 264)


def _gla_chunk(q, k, g, scale, cs):
    r = lax.broadcasted_iota(jnp.int32, (cs, cs), 0)
    c = lax.broadcasted_iota(jnp.int32, (cs, cs), 1)
    causal = c <= r
    b = _tri_dot(causal.astype(BF16), g)
    bl = b[cs - 1:cs, :]
    eb, einv, eend = jnp.exp(b), jnp.exp(-b), jnp.exp(bl - b)
    qd = q.astype(F32) * scale * eb
    kf = k.astype(F32)
    return causal, (eb, einv, eend), bl, qd, kf * einv, kf * eend


def _gla_fwd(proj, glog, t, dk, dv, c_q, c_k, c_v):
    cs = GLA_CHUNK
    nc = t // cs
    wk, wv = GLA_HEADS * dk, GLA_HEADS * dv
    scale = dk ** -0.5

    def body(q_ref, k_ref, v_ref, g_ref, o_ref, sp_ref, st_ref):
        @pl.when(pl.program_id(0) == 0)
        def _():
            st_ref[...] = jnp.zeros_like(st_ref)

        for h in range(GLA_HEADS):
            ks, vs = slice(h * dk, (h + 1) * dk), slice(h * dv, (h + 1) * dv)
            v = v_ref[:, vs]
            causal, _, bl, qd, ki, ke = _gla_chunk(q_ref[:, ks], k_ref[:, ks], g_ref[:, ks], scale, cs)
            st = st_ref[h]
            sp_ref[h] = st
            a = jnp.where(causal, _dot(qd.astype(BF16), ki.astype(BF16), NT), 0.0)
            o_ref[:, vs] = _dot(a.astype(BF16), v, NN) + _dot(qd.astype(BF16), st.astype(BF16), NT)
            st_ref[h] = st * jnp.exp(bl) + _dot(v, ke.astype(BF16), TN)

    return pl.pallas_call(
        body, name="gla_fwd", grid=(nc,),
        in_specs=[pl.BlockSpec((cs, wk), lambda n: (n, c_q // wk)), pl.BlockSpec((cs, wk), lambda n: (n, c_k // wk)),
                  pl.BlockSpec((cs, wv), lambda n: (n, c_v // wv)), pl.BlockSpec((cs, wk), lambda n: (n, 0))],
        out_specs=[pl.BlockSpec((cs, wv), lambda n: (n, 0)),
                   pl.BlockSpec((None, GLA_HEADS, dv, dk), lambda n: (n, 0, 0, 0))],
        out_shape=[jax.ShapeDtypeStruct((t, wv), F32), jax.ShapeDtypeStruct((nc, GLA_HEADS, dv, dk), F32)],
        scratch_shapes=[pltpu.VMEM((GLA_HEADS, dv, dk), F32)],
        compiler_params=_params(("arbitrary",)),
    )(proj, proj, proj, glog)


def _gla_bwd(proj, glog, s_prev, do_raw, t, dk, dv, c_q, c_k, c_v):
    cs = GLA_CHUNK
    nc = t // cs
    wk, wv = GLA_HEADS * dk, GLA_HEADS * dv
    scale = dk ** -0.5

    def body(q_ref, k_ref, v_ref, g_ref, sp_ref, do_ref, dq_ref, dk_ref, dv_ref, dg_ref, dst_ref):
        @pl.when(pl.program_id(0) == 0)
        def _():
            dst_ref[...] = jnp.zeros_like(dst_ref)

        for h in range(GLA_HEADS):
            ks, vs = slice(h * dk, (h + 1) * dk), slice(h * dv, (h + 1) * dv)
            v = v_ref[:, vs]
            do = do_ref[:, vs].astype(BF16)
            causal, (eb, einv, eend), bl, qd, ki, ke = _gla_chunk(q_ref[:, ks], k_ref[:, ks], g_ref[:, ks], scale, cs)
            qd16, ki16, ke16 = qd.astype(BF16), ki.astype(BF16), ke.astype(BF16)
            st = sp_ref[h]
            dst = dst_ref[h]
            dst16 = dst.astype(BF16)
            a = jnp.where(causal, _dot(qd16, ki16, NT), 0.0).astype(BF16)
            da = jnp.where(causal, _dot(do, v, NT), 0.0).astype(BF16)
            dvv = _dot(a, do, TN) + _dot(ke16, dst16, NT)
            dqd = _dot(da, ki16, NN) + _dot(do, st.astype(BF16), NN)
            dki = _dot(da, qd16, TN)
            dke = _dot(v, dst16, NN)
            dl = jnp.exp(bl)
            ddl = _colsum(dst * st)
            dst_ref[h] = dst * dl + _dot(do, qd16, TN)
            dq_ref[:, ks] = (dqd * eb * scale).astype(dq_ref.dtype)
            dk_ref[:, ks] = (dki * einv + dke * eend).astype(dk_ref.dtype)
            dv_ref[:, vs] = dvv.astype(dv_ref.dtype)
            db = dqd * qd - dki * ki - dke * ke
            db_last = _colsum(dke * ke) + ddl * dl
            r = lax.broadcasted_iota(jnp.int32, (cs, cs), 0)
            c = lax.broadcasted_iota(jnp.int32, (cs, cs), 1)
            dg_ref[:, ks] = _tri_dot((c >= r).astype(BF16), db) + db_last

    rev = lambda f: (lambda n: f(nc - 1 - n))
    return pl.pallas_call(
        body, name="gla_bwd", grid=(nc,),
        in_specs=[pl.BlockSpec((cs, wk), rev(lambda n: (n, c_q // wk))), pl.BlockSpec((cs, wk), rev(lambda n: (n, c_k // wk))),
                  pl.BlockSpec((cs, wv), rev(lambda n: (n, c_v // wv))), pl.BlockSpec((cs, wk), rev(lambda n: (n, 0))),
                  pl.BlockSpec((None, GLA_HEADS, dv, dk), rev(lambda n: (n, 0, 0, 0))),
                  pl.BlockSpec((cs, wv), rev(lambda n: (n, 0)))],
        out_specs=[pl.BlockSpec((cs, wk), rev(lambda n: (n, 0))), pl.BlockSpec((cs, wk), rev(lambda n: (n, 0))),
                   pl.BlockSpec((cs, wv), rev(lambda n: (n, 0))), pl.BlockSpec((cs, wk), rev(lambda n: (n, 0)))],
        out_shape=[jax.ShapeDtypeStruct((t, wk), BF16), jax.ShapeDtypeStruct((t, wk), BF16),
                   jax.ShapeDtypeStruct((t, wv), BF16), jax.ShapeDtypeStruct((t, wk), F32)],
        scratch_shapes=[pltpu.VMEM((GLA_HEADS, dv, dk), F32)],
        compiler_params=_params(("arbitrary",)),
    )(proj, proj, proj, glog, s_prev, do_raw)


def _gla_post_fwd(o_raw, proj, gn, t, dv, c_gr):
    w = GLA_HEADS * dv

    def fn(row0, o, gr, g):
        gr = gr.astype(F32)
        outs = []
        for h in range(GLA_HEADS):
            oh = o[:, h * dv:(h + 1) * dv]
            outs.append(oh * lax.rsqrt(jnp.mean(oh * oh, axis=-1, keepdims=True) + EPS))
        on = jnp.concatenate(outs, axis=1) * g
        return (on * (gr * _sigmoid(gr)),)

    return _ew("gla_post_fwd", fn, [("tile", o_raw, w, 0), ("tile", proj, w, c_gr), ("full", gn)],
               [("tile", BF16, w)], t, 264)


def _gla_post_bwd(o_raw, proj, gn, do_gla, t, dv, c_gr):
    w = GLA_HEADS * dv

    def fn(row0, o, gr, g, do):
        gr = gr.astype(F32)
        sg = _sigmoid(gr)
        don = do * (gr * sg)
        ohs, dos = [], []
        for h in range(GLA_HEADS):
            sl = slice(h * dv, (h + 1) * dv)
            oh = o[:, sl]
            r = lax.rsqrt(jnp.mean(oh * oh, axis=-1, keepdims=True) + EPS)
            xh = oh * r
            dxh = don[:, sl] * g[:, sl]
            ohs.append(xh)
            dos.append(r * (dxh - xh * jnp.mean(dxh * xh, axis=-1, keepdims=True)))
        xh = jnp.concatenate(ohs, axis=1)
        dgr = do * (xh * g) * (sg * (1.0 + gr * (1.0 - sg)))
        return jnp.concatenate(dos, axis=1), dgr, _colsum(don * xh)

    return _ew("gla_post_bwd", fn,
               [("tile", o_raw, w, 0), ("tile", proj, w, c_gr), ("full", gn), ("tile", do_gla, w, 0)],
               [("tile", F32, w), ("tile", BF16, w), ("acc", F32, w)], t, 264)


def _adamw(name, w, g, m, v, layer=None, into=None):
    nl, rows, cols = w.shape
    tm = _pick(rows, max(8, (512 * 1024) // max(cols, 1) // 8 * 8), 8)

    def body(w_ref, g_ref, m_ref, v_ref, *rest):
        go_ref, d_ref, nm_ref, nv_ref = rest[-4:]
        gg = g_ref[...]
        nm = ADAM_B1 * m_ref[...] + (1.0 - ADAM_B1) * gg
        nv = ADAM_B2 * v_ref[...] + (1.0 - ADAM_B2) * (gg * gg)
        m_hat = nm / (1.0 - ADAM_B1 ** ADAM_STEP)
        v_hat = nv / (1.0 - ADAM_B2 ** ADAM_STEP)
        go_ref[...] = gg
        d_ref[...] = -ADAM_LR * (m_hat / (jnp.sqrt(v_hat) + ADAM_EPS) + ADAM_WD * w_ref[...])
        nm_ref[...] = nm
        nv_ref[...] = nv

    out_shape = [jax.ShapeDtypeStruct((nl, rows, cols), F32)] * 4
    if layer is None:
        spec = pl.BlockSpec((None, tm, cols), lambda l, i: (l, i, 0))
        return pl.pallas_call(
            body, name=name, grid=(nl, rows // tm), in_specs=[spec] * 4, out_specs=[spec] * 4, out_shape=out_shape,
            compiler_params=_params(("parallel", "parallel")),
        )(w, g, m, v)
    spec = pl.BlockSpec((None, tm, cols), lambda i: (layer, i, 0))
    in_specs = [spec, pl.BlockSpec((tm, cols), lambda i: (i, 0)), spec, spec]
    args, aliases = [w, g, m, v], {}
    if into is not None:
        in_specs += [pl.BlockSpec(memory_space=pl.ANY)] * 4
        args += list(into)
        aliases = {4 + k: k for k in range(4)}
    return pl.pallas_call(
        body, name=name, grid=(rows // tm,), in_specs=in_specs, out_specs=[spec] * 4, out_shape=out_shape,
        input_output_aliases=aliases, compiler_params=_params(("parallel",)),
    )(*args)


def _me():
    return lax.axis_index("x"), lax.axis_index("y"), lax.axis_index("c")


def _hbm_specs(n):
    return [pl.BlockSpec(memory_space=pl.ANY)] * n


_HBM = pl.BlockSpec(memory_space=pltpu.HBM)
_SEM = pl.BlockSpec(memory_space=pltpu.SEMAPHORE)
_EFFECT = pltpu.SideEffectType.DATAFLOW_SIDE_EFFECTING
N_PEERS = N_CHIPS - 1


def _other_chips(x, y):
    return [(1 - x, y), (x, 1 - y), (1 - x, 1 - y)]


def _split_copies(plan, src, land, send_sems, recv_sems):
    me = _me()
    copies = []
    for i in range(len(src)):
        for j, (s, d, peer) in enumerate(plan(src[i], land[i], me)):
            k = plan.copies * i + j
            copies.append(pltpu.make_async_remote_copy(
                src_ref=s, dst_ref=d, send_sem=send_sems.at[k], recv_sem=recv_sems.at[k], device_id=peer,
                device_id_type=MESH))
    return copies


def _split_start(name, srcs, lands, plan, after=None):
    n = len(srcs)
    extra = [] if after is None else [after]

    def body(*refs):
        src, land = refs[:n], refs[n:2 * n]
        send_sems, recv_sems = refs[2 * n + len(extra)], refs[2 * n + len(extra) + 1]
        token = refs[-1]
        for cp in _split_copies(plan, src, land, send_sems, recv_sems):
            cp.start()
        token[...] = jnp.zeros_like(token)

    out_shape = ([pltpu.SemaphoreType.DMA((plan.copies * n,)), pltpu.SemaphoreType.DMA((plan.copies * n,))]
                 + [pltpu.HBM(a.shape, a.dtype) for a in list(srcs) + list(lands)]
                 + [jax.ShapeDtypeStruct((8, LANES), F32)])
    res = pl.pallas_call(
        body, name=name, out_shape=out_shape,
        in_specs=[_HBM] * (2 * n) + [pl.BlockSpec(memory_space=pl.ANY)] * len(extra),
        out_specs=[_SEM, _SEM] + [_HBM] * (2 * n) + [pl.BlockSpec(memory_space=pltpu.VMEM)],
        input_output_aliases={i: 2 + i for i in range(2 * n)},
        compiler_params=pltpu.CompilerParams(has_side_effects=_EFFECT),
    )(*[pltpu.with_memory_space_constraint(a, pltpu.HBM) for a in list(srcs) + list(lands)], *extra)
    return res[0], res[1], res[2:2 + n], res[2 + n:2 + 2 * n], res[-1]


def _split_wait(name, send_sems, recv_sems, srcs, lands, plan, after):
    n = len(srcs)

    def body(*refs):
        src, land = refs[:n], refs[n:2 * n]
        s_sems, r_sems = refs[2 * n], refs[2 * n + 1]
        for cp in _split_copies(plan, src, land, s_sems, r_sems):
            cp.wait_send()
            cp.wait_recv()

    res = pl.pallas_call(
        body, name=name, out_shape=[pltpu.HBM(a.shape, a.dtype) for a in list(srcs) + list(lands)],
        in_specs=[_HBM] * (2 * n) + [_SEM, _SEM, pl.BlockSpec(memory_space=pl.ANY)], out_specs=[_HBM] * (2 * n),
        input_output_aliases={i: i for i in range(2 * n)},
        compiler_params=pltpu.CompilerParams(has_side_effects=_EFFECT),
    )(*srcs, *lands, send_sems, recv_sems, after)
    return res[:n], res[n:]


def _half(ref_rows, c):
    half = ref_rows // 2
    return pl.ds(c * half, half)


def _gather_plan(src, land, me):
    x, y, c = me
    rows = _half(src.shape[0], c)
    return [(src.at[rows], land.at[2 * x + y, rows], (px, py, c)) for px, py in _other_chips(x, y)]


def _scatter_plan(src, land, me):
    x, y, c = me
    return [(src.at[2 * px + py], land.at[2 * x + y], (px, py, c)) for px, py in _other_chips(x, y)]


def _swap_plan(src, land, me):
    x, y, c = me
    return [(src.at[:, _half(src.shape[1], 1 - c)], land, (x, y, 1 - c))]


_gather_plan.copies = N_PEERS
_scatter_plan.copies = N_PEERS
_swap_plan.copies = 1


def _forward_halves(name, gathered):
    n = len(gathered)

    def body(*refs):
        buf = refs[n:2 * n]
        send_sems, recv_sems = refs[2 * n:]
        x, y, c = _me()
        copies = []
        for i in range(n):
            rows = _half(buf[i].shape[1], c)
            for j, (px, py) in enumerate(_other_chips(x, y)):
                cp = pltpu.make_async_remote_copy(
                    src_ref=buf[i].at[2 * px + py, rows], dst_ref=buf[i].at[2 * px + py, rows],
                    send_sem=send_sems.at[N_PEERS * i + j], recv_sem=recv_sems.at[N_PEERS * i + j],
                    device_id=(x, y, 1 - c), device_id_type=MESH)
                cp.start()
                copies.append(cp)
        for cp in copies:
            cp.wait_recv()
        for cp in copies:
            cp.wait_send()

    return pl.pallas_call(
        body, name=name, in_specs=_hbm_specs(n), out_specs=_hbm_specs(n),
        out_shape=[jax.ShapeDtypeStruct(g.shape, g.dtype) for g in gathered],
        input_output_aliases={i: i for i in range(n)},
        scratch_shapes=[pltpu.SemaphoreType.DMA((N_PEERS * n,)), pltpu.SemaphoreType.DMA((N_PEERS * n,))],
    )(*gathered)


def _all_gather_small(v):
    def body(v_ref, out_ref, send_sems, recv_sems, local_sem):
        x, y, c = _me()
        mine = pltpu.make_async_copy(v_ref, out_ref.at[4 * x + 2 * y + c], local_sem)
        mine.start()
        copies = []
        for k in range(1, N_DEV):
            peer = (x ^ ((k >> 2) & 1), y ^ ((k >> 1) & 1), c ^ (k & 1))
            cp = pltpu.make_async_remote_copy(
                src_ref=v_ref, dst_ref=out_ref.at[4 * x + 2 * y + c], send_sem=send_sems.at[k - 1],
                recv_sem=recv_sems.at[k - 1], device_id=peer, device_id_type=MESH)
            cp.start()
            copies.append(cp)
        for cp in copies:
            cp.wait_recv()
        for cp in copies:
            cp.wait_send()
        mine.wait()

    return pl.pallas_call(
        body, name="all_gather_small", in_specs=_hbm_specs(1), out_specs=pl.BlockSpec(memory_space=pl.ANY),
        out_shape=jax.ShapeDtypeStruct((N_DEV,) + v.shape, v.dtype),
        scratch_shapes=[pltpu.SemaphoreType.DMA((N_DEV - 1,)), pltpu.SemaphoreType.DMA((N_DEV - 1,)),
                        pltpu.SemaphoreType.DMA],
    )(v)


def _share_halves(name, grads):
    n = len(grads)

    def body(*refs):
        buf = refs[n:2 * n]
        send_sems, recv_sems = refs[2 * n:]
        x, y, c = _me()
        copies = []
        for i in range(n):
            rows = _half(buf[i].shape[0], c)
            cp = pltpu.make_async_remote_copy(
                src_ref=buf[i].at[rows], dst_ref=buf[i].at[rows], send_sem=send_sems.at[i],
                recv_sem=recv_sems.at[i], device_id=(x, y, 1 - c), device_id_type=MESH)
            cp.start()
            copies.append(cp)
        for cp in copies:
            cp.wait_recv()
        for cp in copies:
            cp.wait_send()

    out_shape = [jax.ShapeDtypeStruct(s.shape, s.dtype) for s in grads]
    return pl.pallas_call(
        body, name=name, in_specs=_hbm_specs(n), out_specs=_hbm_specs(n), out_shape=out_shape,
        input_output_aliases={i: i for i in range(n)},
        scratch_shapes=[pltpu.SemaphoreType.DMA((n,)), pltpu.SemaphoreType.DMA((n,))],
    )(*grads)


def _pair_sum(name, own, recv, core):
    nch, half, cdim = recv.shape
    tm = _pick(half, max(8, (512 * 1024) // cdim // 16 * 16), 16)
    nt = half // tm

    def body(c_ref, a_ref, b_ref, s_ref, s16_ref):
        s = a_ref[...] + b_ref[...]
        s_ref[...] = s
        s16_ref[...] = s.astype(BF16)

    spec = pl.BlockSpec((None, tm, cdim), lambda j, i, c: (j, i, 0))
    return pl.pallas_call(
        body, name=name,
        grid_spec=pltpu.PrefetchScalarGridSpec(
            num_scalar_prefetch=1, grid=(nch, nt),
            in_specs=[pl.BlockSpec((None, tm, cdim), lambda j, i, c: (j, c[0] * nt + i, 0)), spec],
            out_specs=[spec, spec]),
        out_shape=[jax.ShapeDtypeStruct((nch, half, cdim), F32), jax.ShapeDtypeStruct((nch, half, cdim), BF16)],
        compiler_params=_params(("parallel", "parallel")),
    )(core, own, recv)


def _chip_sum(name, own, recv, chip, core):
    nch, half, cdim = own.shape
    tm = _pick(half, max(8, (512 * 1024) // cdim // 16 * 16), 16)
    nt = half // tm

    def body(c_ref, k_ref, own_ref, *rest):
        recv_refs, out_ref = rest[:nch], rest[-1]
        me = c_ref[0]
        acc = None
        for j in range(nch):
            term = jnp.where(me == j, own_ref[...], recv_refs[j][...].astype(F32))
            acc = term if acc is None else acc + term
        out_ref[...] = acc

    recv_specs = [pl.BlockSpec((None, tm, cdim), functools.partial(lambda i, c, k, j: (j, i, 0), j=j))
                  for j in range(nch)]
    return pl.pallas_call(
        body, name=name,
        grid_spec=pltpu.PrefetchScalarGridSpec(
            num_scalar_prefetch=2, grid=(nt,),
            in_specs=[pl.BlockSpec((None, tm, cdim), lambda i, c, k: (c[0], i, 0))] + recv_specs,
            out_specs=pl.BlockSpec((tm, cdim), lambda i, c, k: (k[0] * nt + i, 0))),
        out_shape=jax.ShapeDtypeStruct((2 * half, cdim), F32),
        compiler_params=_params(("parallel",)),
    )(chip, core, own, *([recv] * nch))


def _sum_devices(gathered):
    _, r, cdim = gathered.shape

    def body(g_ref, o_ref):
        acc = g_ref[0]
        for k in range(1, N_DEV):
            acc = acc + g_ref[k]
        o_ref[...] = acc

    return pl.pallas_call(
        body, name="sum_devices", out_shape=jax.ShapeDtypeStruct((r, cdim), F32),
        compiler_params=_params(),
    )(gathered)


class _Layout:
    def __init__(self, d):
        self.d = d
        self.fw = d // 2
        self.fd = self.fw // FOX_HEADS
        self.gk = d // 2
        self.gv = d
        self.dk = self.gk // GLA_HEADS
        self.dv = self.gv // GLA_HEADS
        self.c_fq = 0
        self.c_gq = self.fw
        self.c_gv = self.c_gq + self.gk
        self.c_gr = self.c_gv + self.gv
        self.c_fkv = self.c_gr + self.gv
        self.c_gates = self.c_fkv + 2 * self.fw
        self.c_gk = self.c_gates + 2 * d
        self.c_small = self.c_gk + self.gk
        self.n_main = self.c_small
        self.n_p = self.c_small + LANES
        self.o_fk = self.fw
        self.o_fv = 2 * self.fw
        self.o_ff = 3 * self.fw
        self.o_gq = self.o_ff + FOX_HEADS
        self.o_gk = self.o_gq + self.gk
        self.o_gv = self.o_gk + self.gk
        self.o_gr = self.o_gv + self.gv
        self.o_ga = self.o_gr + self.gv
        self.o_gf = self.o_ga + GLA_RANK
        self.o_gg = self.o_gf + d
        self.n_orig = self.o_gg + d

    def to_p(self, shards):
        per = self.n_orig // N_CHIPS
        ranges = [(0, self.fw), (self.o_gq, self.gk), (self.o_gv, self.gv), (self.o_gr, self.gv)]
        for h in range(FOX_HEADS):
            ranges += [(self.o_fk + h * self.fd, self.fd), (self.o_fv + h * self.fd, self.fd)]
        ranges += [(self.o_gf, 2 * self.d), (self.o_gk, self.gk), (self.o_ff, FOX_HEADS), (self.o_ga, GLA_RANK)]
        pieces = []
        for a, width in ranges:
            for j in range(a // per, (a + width - 1) // per + 1):
                lo, hi = max(a, j * per), min(a + width, (j + 1) * per)
                pieces.append(shards[j][:, lo - j * per:hi - j * per])
        pieces.append(jnp.zeros((shards.shape[1], LANES - FOX_HEADS - GLA_RANK), shards.dtype))
        return jnp.concatenate(pieces, axis=1)

    def from_segments(self, seg):
        per = self.n_orig // N_CHIPS
        fd = self.fd
        atoms = [("fq", 0, self.fw)]
        atoms += [("fkv", 2 * h * fd, fd) for h in range(FOX_HEADS)]
        atoms += [("fkv", (2 * h + 1) * fd, fd) for h in range(FOX_HEADS)]
        atoms += [("small", 0, FOX_HEADS), ("gq", 0, self.gk), ("gk", 0, self.gk), ("gv", 0, self.gv),
                  ("gr", 0, self.gv), ("small", FOX_HEADS, GLA_RANK), ("gates", 0, 2 * self.d)]
        shards = [[] for _ in range(N_CHIPS)]
        pos = 0
        for name, c0, width in atoms:
            for j in range(pos // per, (pos + width - 1) // per + 1):
                lo, hi = max(pos, j * per), min(pos + width, (j + 1) * per)
                shards[j].append(seg[name][:, c0 + lo - pos:c0 + hi - pos])
            pos += width
        assert pos == self.n_orig
        return jnp.stack([jnp.concatenate(s, axis=1) for s in shards])


def _layer_fwd(lay, h, p, t, late=None):
    d = lay.d
    xn, xn_t = _rms_fwd_t("rms_mix_fwd", h, p["norm_mix_g"], t, d)
    proj = _mm("mm_proj", xn, p["w_in"], mode="nn", m=t, n=lay.n_main, k=d, out_dtypes=(BF16,))
    small = _mm("mm_small", xn, p["w_in"], mode="nn", m=t, n=LANES, k=d, b_c0=lay.c_small)
    cs = _fox_gate_fwd(small, p["b_forget_p"], t)
    ct = cs[:, :FOX_HEADS].T
    c_col, c_row = ct[:, :, None], ct[:, None, :]
    o_fox, lse = _fox_fwd(proj, c_col, c_row, t, lay.fd, lay.c_fq, lay.c_fkv)
    glog = _gla_gate_fwd(small, p["w_alpha_p"], p["b_alpha"], t, lay.gk)
    o_raw, s_prev = _gla_fwd(proj, glog, t, lay.dk, lay.dv, lay.c_gq, lay.c_gk, lay.c_gv)
    o_gla = _gla_post_fwd(o_raw, proj, p["gla_norm_g"], t, lay.dv, lay.c_gr)
    if late is not None:
        p.update(late(o_gla))
    a_fox = _mm("mm_o_fox", o_fox, p["w_o_fox"], mode="nn", m=t, n=d, k=lay.fw, b_shards=N_CHIPS)
    a_gla = _mm("mm_o_gla", o_gla, p["w_o_gla"], mode="nn", m=t, n=d, k=lay.gv)
    y = _merge_fwd(a_fox, a_gla, proj, lay.c_gates, t, d)
    h1 = _mm("mm_out", y, p["w_out"], mode="nn", m=t, n=d, k=d, extras=[h], epilogue=lambda acc, res: (res + acc,))
    xn2, xn2_t = _rms_fwd_t("rms_mlp_fwd", h1, p["norm_mlp_g"], t, d)
    u, act = _mm("mm_ff1", xn2, p["w_ff1"], mode="nn", m=t, n=4 * d, k=d, out_dtypes=(BF16, BF16), b_shards=N_CHIPS,
                 epilogue=lambda acc: (acc, jnp.square(jnp.maximum(acc, 0.0))))
    h2 = _mm("mm_ff2", act, p["w_ff2"], mode="nn", m=t, n=d, k=4 * d, extras=[h1],
             epilogue=lambda acc, res: (res + acc,))
    saved = dict(h=h, xn_t=xn_t, proj=proj, small=small, c_col=c_col, c_row=c_row, o_fox=o_fox, lse=lse, glog=glog,
                 o_raw=o_raw, s_prev=s_prev, o_gla=o_gla, a_fox=a_fox, a_gla=a_gla, y=y, h1=h1, xn2_t=xn2_t, u=u, act=act)
    return h2, saved


def _layer_bwd(lay, dh2, p, s, t, gates=None):
    d = lay.d
    g = {}

    def gated(gain, point):
        return gain + gates[point](g) if gates and point in gates else gain
    du = _mm("mm_dact", dh2, p["w_ff2"], mode="nt", m=t, n=4 * d, k=d, extras=[s["u"]], out_dtypes=(BF16,),
             epilogue=lambda acc, u: (acc * (2.0 * jnp.maximum(u.astype(F32), 0.0)),))
    g["w_ff2"] = _mm("mm_dw_ff2", s["act"], dh2, mode="tn", m=4 * d, n=d, k=t)
    g["w_ff1"] = _mm("mm_dw_ff1", s["xn2_t"], du, mode="nn", m=d, n=4 * d, k=t, tk=t, out_shards=N_CHIPS)
    dxn2 = _mm("mm_dxn2", du, p["w_ff1"], mode="nt", m=t, n=d, k=4 * d, b_shards=N_CHIPS)
    dh1, g["norm_mlp_g"] = _rms_bwd("rms_mlp_bwd", s["h1"], gated(p["norm_mlp_g"], "mlp"), dxn2, dh2, t, d)
    dy = _mm("mm_dy", dh1, p["w_out"], mode="nt", m=t, n=d, k=d)
    g["w_out"] = _mm("mm_dw_out", s["y"], dh1, mode="tn", m=d, n=d, k=t)
    da_fox, da_gla, dgates = _merge_bwd(dy, s["a_fox"], s["a_gla"], s["proj"], lay.c_gates, t, d)
    g["w_o_fox"] = _mm("mm_dw_o_fox", s["o_fox"], da_fox, mode="tn", m=lay.fw, n=d, k=t, out_shards=N_CHIPS)
    do_fox = _mm("mm_do_fox", da_fox, p["w_o_fox"], mode="nt", m=t, n=lay.fw, k=d, b_shards=N_CHIPS)
    g["w_o_gla"] = _mm("mm_dw_o_gla", s["o_gla"], da_gla, mode="tn", m=lay.gv, n=d, k=t)
    do_gla = _mm("mm_do_gla", da_gla, p["w_o_gla"], mode="nt", m=t, n=lay.gv, k=d)
    do_raw, dgr, g["gla_norm_g"] = _gla_post_bwd(s["o_raw"], s["proj"], gated(p["gla_norm_g"], "out"), do_gla, t,
                                                 lay.dv, lay.c_gr)
    dgq, dgk, dgv, dglog = _gla_bwd(s["proj"], s["glog"], s["s_prev"], do_raw, t, lay.dk, lay.dv,
                                    lay.c_gq, lay.c_gk, lay.c_gv)
    dz, g["b_alpha"] = _gla_gate_bwd(dglog, s["small"], p["w_alpha_p"], p["b_alpha"], t, lay.gk)
    g["w_alpha_p"] = _mm("mm_dw_alpha", s["small"], dz, mode="tn", m=LANES, n=lay.gk, k=t)
    dga = _mm("mm_dga", dz, p["w_alpha_p"], mode="nt", m=t, n=LANES, k=lay.gk)
    delta = _fox_delta(s["o_fox"], do_fox, t, lay.fd)
    dfq, dfkv, dc, dr = _fox_bwd(s["proj"], s["c_col"], s["c_row"], s["lse"], delta, do_fox, t, lay.fd,
                                 lay.c_fq, lay.c_fkv)
    dc_p = jnp.pad((dc[:, 0, :] + dr[:, :, 0]).T, ((0, 0), (0, LANES - FOX_HEADS)))
    dsmall, g["b_forget_p"] = _fox_gate_bwd(dc_p, s["small"], p["b_forget_p"], dga, t)
    segs = [("fq", dfq, lay.c_fq), ("gq", dgq, lay.c_gq), ("gv", dgv, lay.c_gv), ("gr", dgr, lay.c_gr),
            ("fkv", dfkv, lay.c_fkv), ("gates", dgates, lay.c_gates), ("gk", dgk, lay.c_gk),
            ("small", dsmall, lay.c_small)]
    dxn = None
    dw_in = {}
    for nm, dseg, c0 in segs:
        width = dseg.shape[1]
        dw_in[nm] = _mm("mm_dw_in_" + nm, s["xn_t"], dseg, mode="nn", m=d, n=width, k=t, tk=t)
        if dxn is None:
            dxn = _mm("mm_dxn_" + nm, dseg, p["w_in"], mode="nt", m=t, n=d, k=width, b_c0=c0)
        else:
            dxn = _mm("mm_dxn_" + nm, dseg, p["w_in"], mode="nt", m=t, n=d, k=width, b_c0=c0, extras=[dxn],
                      epilogue=lambda acc, prev: (prev + acc,))
    g["w_in"] = lay.from_segments(dw_in)
    dh, g["norm_mix_g"] = _rms_bwd("rms_mix_bwd", s["h"], gated(p["norm_mix_g"], "in"), dxn, dh1, t, d)
    return dh, g


def _sequence_step(x, target, meta, layers, final_g):
    seq, d = x.shape
    t = seq + ROW0
    lay = _Layout(d)
    h = jnp.pad(x, ((ROW0, 0), (0, 0))).at[PAD:ROW0].set(meta)
    target_p = jnp.pad(target, ((ROW0, 0), (0, 0)))
    saved = []
    for p in layers:
        h, s = _layer_fwd(lay, h, p, t)
        saved.append(s)
    dh, dg_final, loss_part = _loss_head(h, final_g, target_p, t, d)
    grads = [None] * len(layers)
    for l in reversed(range(len(layers))):
        dh, grads[l] = _layer_bwd(lay, dh, layers[l], saved[l], t)
    return loss_part, dh[ROW0:], dh[PAD:ROW0], grads, dg_final


_SMALL_ROWS = 48


def _pack_small(d, meta, mix, gla, mlp, final, b_alpha, b_forget, w_alpha2):
    rows = [meta.reshape(N_META, d), mix.reshape(DEPTH, d), gla.reshape(DEPTH, d), mlp.reshape(DEPTH, d),
            final.reshape(1, d), b_alpha.reshape(1, d),
            jnp.pad(b_forget.reshape(1, DEPTH * FOX_HEADS), ((0, 0), (0, d - DEPTH * FOX_HEADS))),
            jnp.zeros((7, d), F32), w_alpha2.reshape(GLA_RANK, d)]
    return jnp.concatenate(rows, axis=0)


def _unpack_small(d, packed):
    return dict(meta=packed[:N_META], norm_mix_g=packed[16:18], gla_norm_g=packed[18:20], norm_mlp_g=packed[20:22],
                final_norm_g=packed[22], b_alpha=packed[23].reshape(DEPTH, d // 2),
                b_forget=packed[24, :DEPTH * FOX_HEADS].reshape(DEPTH, FOX_HEADS),
                w_alpha2=packed[32:48].reshape(DEPTH, GLA_RANK, d // 2))


_BIG = ("w_in", "w_o_fox", "w_o_gla", "w_out", "w_ff1", "w_ff2")
_COL_SHARDED = ("w_in", "w_o_fox", "w_ff1")


def _full_matrix(name, gathered_l):
    nch, r, c = gathered_l.shape
    if name in _COL_SHARDED:
        return gathered_l.transpose(1, 0, 2).reshape(r, nch * c)
    return gathered_l.reshape(nch * r, c)


def _shard_major(name, full):
    r, c = full.shape
    if name in _COL_SHARDED:
        return full.reshape(r, N_CHIPS, c // N_CHIPS).transpose(1, 0, 2)
    return full.reshape(N_CHIPS, r // N_CHIPS, c)


def kernel(x, meta_tokens, norm_mix_g, w_in, b_forget, w_alpha2, b_alpha, gla_norm_g, w_o_fox, w_o_gla, w_out, norm_mlp_g, w_ff1, w_ff2, final_norm_g, loss_target, m_meta_tokens, m_norm_mix_g, m_w_in, m_b_forget, m_w_alpha2, m_b_alpha, m_gla_norm_g, m_w_o_fox, m_w_o_gla, m_w_out, m_norm_mlp_g, m_w_ff1, m_w_ff2, m_final_norm_g, v_meta_tokens, v_norm_mix_g, v_w_in, v_b_forget, v_w_alpha2, v_b_alpha, v_gla_norm_g, v_w_o_fox, v_w_o_gla, v_w_out, v_norm_mlp_g, v_w_ff1, v_w_ff2, v_final_norm_g):
    d = x.shape[2]
    lay = _Layout(d)
    xi, yi, ci = lax.axis_index("x"), lax.axis_index("y"), lax.axis_index("c")
    chip = (2 * xi + yi).astype(jnp.int32)
    w = dict(w_in=w_in, w_alpha2=w_alpha2, w_o_fox=w_o_fox, w_o_gla=w_o_gla, w_out=w_out, w_ff1=w_ff1, w_ff2=w_ff2)
    m = dict(w_in=m_w_in, w_alpha2=m_w_alpha2, w_o_fox=m_w_o_fox, w_o_gla=m_w_o_gla, w_out=m_w_out, w_ff1=m_w_ff1,
             w_ff2=m_w_ff2)
    v = dict(w_in=v_w_in, w_alpha2=v_w_alpha2, w_o_fox=v_w_o_fox, w_o_gla=v_w_o_gla, w_out=v_w_out, w_ff1=v_w_ff1,
             w_ff2=v_w_ff2)

    seq = x.shape[1]
    t = seq + ROW0
    core_idx = ci.astype(jnp.int32)[None]
    chip_idx = chip[None]

    cols = d // N_CHIPS
    small_w = jnp.concatenate([meta_tokens, w_alpha2.reshape(-1, cols)], axis=0)
    small_raw = _all_gather_small(small_w)
    small_all = small_raw[0::2]
    alpha_full = small_all[:, N_META:].reshape(N_CHIPS, DEPTH, GLA_RANK, lay.gk // N_CHIPS)
    alpha_full = alpha_full.transpose(1, 2, 0, 3).reshape(DEPTH, GLA_RANK, lay.gk)
    groups = [(0, ("w_in",)), (0, _BIG[1:]), (1, _BIG)]
    started, after = [], small_raw
    for gi, (l, names) in enumerate(groups):
        own16 = [w[n][l].astype(BF16) for n in names]
        lands = [lax.empty((N_CHIPS,) + o.shape, BF16) for o in own16]
        started.append(_split_start("gather_start_%d" % gi, own16, lands, _gather_plan, after=after))
        after = started[gi][4]
    meta_full = small_all[:, :N_META].transpose(1, 0, 2).reshape(N_META, d) + after[0, 0]

    def gathered(gi, after):
        send_sems, recv_sems, srcs, lands, _ = started[gi]
        srcs, lands = _split_wait("gather_wait_%d" % gi, send_sems, recv_sems, srcs, lands, _gather_plan, after)
        lands = _forward_halves("gather_forward_%d" % gi, lands)
        return {n: lax.dynamic_update_slice(g, o[None], (chip, 0, 0)) for n, g, o in zip(groups[gi][1], lands, srcs)}

    def early_weights(l, gl):
        w_alpha_p = jnp.zeros((LANES, lay.gk), BF16).at[FOX_HEADS:FOX_HEADS + GLA_RANK].set(
            alpha_full[l].astype(BF16))
        return dict(
            w_in=lay.to_p(gl["w_in"]), w_alpha_p=w_alpha_p,
            norm_mix_g=norm_mix_g[l][None], norm_mlp_g=norm_mlp_g[l][None], gla_norm_g=gla_norm_g[l][None],
            b_alpha=b_alpha[l][None],
            b_forget_p=jnp.pad(b_forget[l][None], ((0, 0), (0, LANES - FOX_HEADS))))

    def late_weights(gl):
        return dict(w_o_fox=gl["w_o_fox"], w_o_gla=_full_matrix("w_o_gla", gl["w_o_gla"]),
                    w_out=_full_matrix("w_out", gl["w_out"]), w_ff1=gl["w_ff1"],
                    w_ff2=_full_matrix("w_ff2", gl["w_ff2"]))

    h = jnp.pad(x[0], ((ROW0, 0), (0, 0))).at[PAD:ROW0].set(meta_full)
    layers, saved = [], []
    layers.append(early_weights(0, gathered(0, after=h)))
    h, s = _layer_fwd(lay, h, layers[0], t, late=lambda after: late_weights(gathered(1, after)))
    saved.append(s)
    gl = gathered(2, after=h)
    layers.append({**early_weights(1, gl), **late_weights(gl)})
    h, s = _layer_fwd(lay, h, layers[1], t)
    saved.append(s)
    dh, dg_final, loss_part = _loss_head(h, final_norm_g[None], jnp.pad(loss_target[0], ((ROW0, 0), (0, 0))), t, d)
    loss = lax.psum(loss_part[0, 0], ("x", "y", "c"))

    def partial_of(g, n):
        return g[n] if n in ("w_ff1", "w_o_fox", "w_in") else _shard_major(n, g[n])

    scatter_groups = dict(mlp=("w_ff1", "w_ff2"), out=("w_o_fox", "w_o_gla", "w_out"))
    scatter_groups["in"] = ("w_in",)
    swapping, scattered = [], {}

    def start_swap(l, grp, g, after=None):
        parts = [partial_of(g, n) for n in scatter_groups[grp]]
        lands = [lax.empty((p_.shape[0], p_.shape[1] // 2, p_.shape[2]), F32) for p_ in parts]
        started_swap = _split_start("swap_start_%d_%s" % (l, grp), parts, lands, _swap_plan, after=after)
        swapping.append((l, grp, started_swap))
        return started_swap[4]

    def start_scatter(after):
        l, grp, (send_sems, recv_sems, srcs, lands, _) = swapping.pop(0)
        names = scatter_groups[grp]
        tag = "%d_%s" % (l, grp)
        parts, from_sibling = _split_wait("swap_wait_" + tag, send_sems, recv_sems, srcs, lands, _swap_plan, after)
        sums = [_pair_sum("pair_sum_%d_%s" % (l, n), p_, r_, core_idx) for n, p_, r_ in zip(names, parts, from_sibling)]
        lands = [lax.empty(s16.shape, BF16) for _, s16 in sums]
        send_sems, recv_sems, srcs, lands, token = _split_start("scatter_start_" + tag, [s16 for _, s16 in sums], lands,
                                                                _scatter_plan)
        scattered[l, grp] = (send_sems, recv_sems, srcs, lands, [s32 for s32, _ in sums])
        return token

    def gate(l, grp, g):
        token = start_swap(l, grp, g)
        if len(swapping) > 1:
            token = start_scatter(after=token)
        return token[0, 0]

    def gates_for(l, points):
        return {grp: functools.partial(gate, l, grp) for grp in points}

    grads = [None] * DEPTH
    dh, grads[1] = _layer_bwd(lay, dh, layers[1], saved[1], t, gates=gates_for(1, ("mlp", "out", "in")))
    dh, grads[0] = _layer_bwd(lay, dh, layers[0], saved[0], t, gates=gates_for(0, ("mlp", "out")))
    grad_x, d_meta = dh[ROW0:], dh[PAD:ROW0]

    stack = lambda key: jnp.concatenate([grads[l][key] for l in range(DEPTH)], axis=0)
    b_forget_g = jnp.concatenate([grads[l]["b_forget_p"][:, :FOX_HEADS] for l in range(DEPTH)], axis=0)
    alpha_g = jnp.stack([grads[l]["w_alpha_p"][FOX_HEADS:FOX_HEADS + GLA_RANK] for l in range(DEPTH)])
    packed = _pack_small(d, d_meta, stack("norm_mix_g"), stack("gla_norm_g"), stack("norm_mlp_g"), dg_final,
                         stack("b_alpha"), b_forget_g, alpha_g)
    small_g = _unpack_small(d, _sum_devices(_all_gather_small(packed)))
    small_g["meta"] = lax.dynamic_slice_in_dim(small_g["meta"], chip * (d // N_CHIPS), d // N_CHIPS, axis=1)
    alpha_shard = lax.dynamic_slice_in_dim(small_g["w_alpha2"], chip * (lay.gk // N_CHIPS), lay.gk // N_CHIPS, axis=2)

    after = start_swap(0, "in", grads[0], after=small_g["final_norm_g"])
    while swapping:
        after = start_scatter(after)
    outs = {n: None for n in _BIG}
    for l in reversed(range(DEPTH)):
        names, layer_g = [], []
        for grp in ("mlp", "out", "in"):
            send_sems, recv_sems, srcs, lands, sums32 = scattered[l, grp]
            _, lands = _split_wait("scatter_wait_%d_%s" % (l, grp), send_sems, recv_sems, srcs, lands, _scatter_plan,
                                 after)
            names += scatter_groups[grp]
            layer_g += [_chip_sum("chip_sum_%d_%s" % (l, n), s32, r_, chip_idx, core_idx)
                        for n, s32, r_ in zip(scatter_groups[grp], sums32, lands)]
        layer_g = _share_halves("share_halves_%d" % l, layer_g)
        for n, g in zip(names, layer_g):
            outs[n] = _adamw("adamw_%d_%s" % (l, n), w[n], g, m[n], v[n], layer=l, into=outs[n])
        after = outs[names[-1]][0]
    out_g, out_d, out_m, out_v = {}, {}, {}, {}
    for n in _BIG:
        out_g[n], out_d[n], out_m[n], out_v[n] = outs[n]
    out_g["w_alpha2"], out_d["w_alpha2"], out_m["w_alpha2"], out_v["w_alpha2"] = _adamw(
        "adamw_w_alpha2", w["w_alpha2"], alpha_shard, m["w_alpha2"], v["w_alpha2"])
    sm_w = dict(meta_tokens=meta_tokens, norm_mix_g=norm_mix_g, b_forget=b_forget, b_alpha=b_alpha,
                gla_norm_g=gla_norm_g, norm_mlp_g=norm_mlp_g, final_norm_g=final_norm_g)
    sm_m = dict(meta_tokens=m_meta_tokens, norm_mix_g=m_norm_mix_g, b_forget=m_b_forget, b_alpha=m_b_alpha,
                gla_norm_g=m_gla_norm_g, norm_mlp_g=m_norm_mlp_g, final_norm_g=m_final_norm_g)
    sm_v = dict(meta_tokens=v_meta_tokens, norm_mix_g=v_norm_mix_g, b_forget=v_b_forget, b_alpha=v_b_alpha,
                gla_norm_g=v_gla_norm_g, norm_mlp_g=v_norm_mlp_g, final_norm_g=v_final_norm_g)
    sm_g = dict(meta_tokens=small_g["meta"], norm_mix_g=small_g["norm_mix_g"], b_forget=small_g["b_forget"],
                b_alpha=small_g["b_alpha"], gla_norm_g=small_g["gla_norm_g"], norm_mlp_g=small_g["norm_mlp_g"],
                final_norm_g=small_g["final_norm_g"])
    names_small = list(sm_w)
    sizes = [sm_w[n].size for n in names_small]
    width = 512
    total = -(-sum(sizes) // (8 * width)) * (8 * width)

    def pack_flat(dct, fill):
        flat = jnp.concatenate([dct[n].reshape(-1) for n in names_small])
        return jnp.pad(flat, (0, total - flat.shape[0]), constant_values=fill).reshape(1, -1, width)

    res = _adamw("adamw_small", pack_flat(sm_w, 0.0), pack_flat(sm_g, 0.0), pack_flat(sm_m, 0.0), pack_flat(sm_v, 1.0))
    offs = [0]
    for sz in sizes:
        offs.append(offs[-1] + sz)
    for i, n in enumerate(names_small):
        out_g[n] = sm_g[n].reshape(sm_w[n].shape)
        out_d[n], out_m[n], out_v[n] = [r.reshape(-1)[offs[i]:offs[i + 1]].reshape(sm_w[n].shape) for r in res[1:]]

    order = ["meta_tokens", "norm_mix_g", "w_in", "b_forget", "w_alpha2", "b_alpha", "gla_norm_g", "w_o_fox",
             "w_o_gla", "w_out", "norm_mlp_g", "w_ff1", "w_ff2", "final_norm_g"]
    return (loss, grad_x[None], *[out_g[n] for n in order], *[out_d[n] for n in order],
            *[out_m[n] for n in order], *[out_v[n] for n in order])
```

```python
import functools

import numpy as np

import jax
import jax.numpy as jnp
from jax import lax
from jax.experimental import pallas as pl
from jax.experimental.pallas import tpu as pltpu

F32 = jnp.float32
BF16 = jnp.bfloat16

N_META = 16
PAD = 112
ROW0 = PAD + N_META
EPS = 1e-6
MASK_VALUE = -1e30
FOX_HEADS = 8
FOX_GROUP = 2
GLA_HEADS = 4
GLA_RANK = 16
GLA_TAU = 16.0
GLA_CHUNK = 64
DEPTH = 2
N_CHIPS = 4
N_DEV = 8

ADAM_LR = 0.001
ADAM_B1 = 0.9
ADAM_B2 = 0.999
ADAM_EPS = 1e-08
ADAM_WD = 0.01
ADAM_STEP = 10

LANES = 128
VMEM_LIMIT = 56 * 1024 * 1024
MESH = pl.DeviceIdType.MESH


def _pick(n, target, mult):
    best = None
    for d in range(mult, min(n, target) + 1, mult):
        if n % d == 0:
            best = d
    return n if best is None else best


def _params(sem=None):
    return pltpu.CompilerParams(dimension_semantics=sem, vmem_limit_bytes=VMEM_LIMIT)


def _bf(v):
    return v if v.dtype == BF16 else v.astype(BF16)


def _sigmoid(z):
    return 1.0 / (1.0 + jnp.exp(-z))


def _log_sigmoid(z):
    return jnp.minimum(z, 0.0) - jnp.log(1.0 + jnp.exp(-jnp.abs(z)))


def _split3(v):
    a = v.astype(BF16)
    r = v - a.astype(F32)
    b = r.astype(BF16)
    c = (r - b.astype(F32)).astype(BF16)
    return a, b, c


def _dot(a, b, dims):
    return lax.dot_general(a, b, (dims, ((), ())), preferred_element_type=F32)


NN = ((1,), (0,))
NT = ((1,), (1,))
TN = ((0,), (0,))


def _tri_dot(tri, v, dims=NN):
    a, b, c = _split3(v)
    return _dot(tri, a, dims) + _dot(tri, b, dims) + _dot(tri, c, dims)


def _mm(name, a, b, *, mode, m, n, k, b_c0=0, extras=(), epilogue=None, out_dtypes=(F32,),
        b_shards=1, out_shards=1, tm=1056, tn=1024, tk=2048):
    tm = _pick(m, tm, LANES if mode == "tn" else 16)
    tn = _pick(n // max(b_shards if mode == "nn" else 1, out_shards), tn, LANES)
    if mode == "tn":
        tk = _pick(k, 2112, 16)
    else:
        tk = _pick(k // (b_shards if mode == "nt" else 1), tk, LANES)
    assert b_c0 % (tk if mode == "nt" else tn) == 0 and (b_shards == 1 or b_c0 == 0)
    nk = k // tk
    if mode == "tn":
        a_spec = pl.BlockSpec((tk, tm), lambda i, j, kk: (kk, i))
    else:
        a_spec = pl.BlockSpec((tm, tk), lambda i, j, kk: (i, kk))
    if mode == "nt":
        dims = NT
        if b_shards > 1:
            per = (k // b_shards) // tk
            b_spec = pl.BlockSpec((None, tn, tk), lambda i, j, kk: (kk // per, j, kk % per))
        else:
            b_spec = pl.BlockSpec((tn, tk), lambda i, j, kk: (j, kk + b_c0 // tk))
    else:
        dims = NN if mode == "nn" else TN
        if b_shards > 1:
            per = (n // b_shards) // tn
            b_spec = pl.BlockSpec((None, tk, tn), lambda i, j, kk: (j // per, kk, j % per))
        else:
            b_spec = pl.BlockSpec((tk, tn), lambda i, j, kk: (kk, j + b_c0 // tn))
    ex_specs = [pl.BlockSpec((tm, tn), lambda i, j, kk: (i, j)) for _ in extras]
    if out_shards > 1:
        oper = (n // out_shards) // tn
        out_specs = [pl.BlockSpec((None, tm, tn), lambda i, j, kk: (j // oper, i, j % oper)) for _ in out_dtypes]
        out_shape = [jax.ShapeDtypeStruct((out_shards, m, n // out_shards), dt) for dt in out_dtypes]
    else:
        out_specs = [pl.BlockSpec((tm, tn), lambda i, j, kk: (i, j)) for _ in out_dtypes]
        out_shape = [jax.ShapeDtypeStruct((m, n), dt) for dt in out_dtypes]
    n_ex = len(extras)
    n_out = len(out_dtypes)

    def finish(acc, ex_refs, out_refs):
        vals = (acc,) if epilogue is None else epilogue(acc, *[r[...] for r in ex_refs])
        for r, v in zip(out_refs, vals):
            r[...] = v.astype(r.dtype)

    def body(a_ref, b_ref, *rest):
        ex_refs = rest[:n_ex]
        out_refs = rest[n_ex:n_ex + n_out]
        prod = _dot(_bf(a_ref[...]), _bf(b_ref[...]), dims)
        if nk == 1:
            finish(prod, ex_refs, out_refs)
            return
        acc_ref = rest[n_ex + n_out]
        kk = pl.program_id(2)

        @pl.when(kk == 0)
        def _():
            acc_ref[...] = prod

        @pl.when((kk > 0) & (kk < nk - 1))
        def _():
            acc_ref[...] += prod

        @pl.when(kk == nk - 1)
        def _():
            finish(acc_ref[...] + prod, ex_refs, out_refs)

    outs = pl.pallas_call(
        body,
        name=name,
        grid=(m // tm, n // tn, nk),
        in_specs=[a_spec, b_spec] + ex_specs,
        out_specs=out_specs,
        out_shape=out_shape,
        scratch_shapes=[pltpu.VMEM((tm, tn), F32)] if nk > 1 else [],
        compiler_params=_params(("parallel", "parallel", "arbitrary")),
    )(a, b, *extras)
    return outs[0] if n_out == 1 else outs


def _ew(name, fn, ins, outs, rows, tm):
    tm = _pick(rows, tm, 16)
    in_specs, args = [], []
    for spec in ins:
        if spec[0] == "tile":
            _, arr, width, c0 = spec
            assert c0 % width == 0
            in_specs.append(pl.BlockSpec((tm, width), functools.partial(lambda i, o: (i, o), o=c0 // width)))
        else:
            arr = spec[1]
            in_specs.append(pl.BlockSpec(arr.shape, lambda i: (0, 0)))
        args.append(arr)
    out_specs, out_shape = [], []
    for kind, dt, width in outs:
        if kind == "tile":
            out_specs.append(pl.BlockSpec((tm, width), lambda i: (i, 0)))
            out_shape.append(jax.ShapeDtypeStruct((rows, width), dt))
        else:
            out_specs.append(pl.BlockSpec((1, width), lambda i: (0, 0)))
            out_shape.append(jax.ShapeDtypeStruct((1, width), dt))
    n_in = len(ins)
    has_acc = any(o[0] == "acc" for o in outs)

    def body(*refs):
        i = pl.program_id(0)
        vals = fn(i * tm, *[r[...] for r in refs[:n_in]])
        for (kind, _, _), r, v in zip(outs, refs[n_in:], vals):
            if kind == "tile":
                r[...] = v.astype(r.dtype)
            else:
                @pl.when(i == 0)
                def _():
                    r[...] = jnp.zeros_like(r)

                r[...] += v.astype(r.dtype)

    res = pl.pallas_call(
        body,
        name=name,
        grid=(rows // tm,),
        in_specs=in_specs,
        out_specs=out_specs,
        out_shape=out_shape,
        compiler_params=_params(("arbitrary",) if has_acc else ("parallel",)),
    )(*args)
    return res[0] if len(outs) == 1 else res


def _row_ids(row0, tm):
    return row0 + lax.broadcasted_iota(jnp.int32, (tm, 1), 0)


def _colsum(v):
    return jnp.sum(v, axis=0, keepdims=True)


def _rms_fwd_t(name, h, g, t, d):
    tm = _pick(t, 384, LANES)

    def body(x_ref, g_ref, y_ref, yt_ref):
        x = x_ref[...]
        y = x * lax.rsqrt(jnp.mean(x * x, axis=-1, keepdims=True) + EPS) * g_ref[...]
        y_ref[...] = y.astype(BF16)
        yt_ref[...] = y.T.astype(BF16)

    return pl.pallas_call(
        body, name=name, grid=(t // tm,),
        in_specs=[pl.BlockSpec((tm, d), lambda i: (i, 0)), pl.BlockSpec((1, d), lambda i: (0, 0))],
        out_specs=[pl.BlockSpec((tm, d), lambda i: (i, 0)), pl.BlockSpec((d, tm), lambda i: (0, i))],
        out_shape=[jax.ShapeDtypeStruct((t, d), BF16), jax.ShapeDtypeStruct((d, t), BF16)],
        compiler_params=_params(("parallel",)),
    )(h, g)


def _rms_bwd(name, h, g, dy, dres, t, d):
    def fn(row0, x, gg, dyv, dr):
        r = lax.rsqrt(jnp.mean(x * x, axis=-1, keepdims=True) + EPS)
        xh = x * r
        dxh = dyv * gg
        dx = r * (dxh - xh * jnp.mean(dxh * xh, axis=-1, keepdims=True))
        out = jnp.where(_row_ids(row0, x.shape[0]) >= PAD, dr + dx, 0.0)
        return out, _colsum(dyv * xh)

    return _ew(name, fn, [("tile", h, d, 0), ("full", g), ("tile", dy, d, 0), ("tile", dres, d, 0)],
               [("tile", F32, d), ("acc", F32, d)], t, 264)


def _rms_bwd_x(name, h, g, dy, dres, t, d):
    tm = ROW0

    def body(x_ref, g_ref, dy_ref, dr_ref, dx_ref, head_ref, dg_ref):
        i = pl.program_id(0)
        x = x_ref[...]
        r = lax.rsqrt(jnp.mean(x * x, axis=-1, keepdims=True) + EPS)
        xh = x * r
        dxh = dy_ref[...] * g_ref[...]
        dx = r * (dxh - xh * jnp.mean(dxh * xh, axis=-1, keepdims=True))
        out = jnp.where(_row_ids(i * tm, tm) >= PAD, dr_ref[...] + dx, 0.0)

        @pl.when(i == 0)
        def _():
            head_ref[...] = out
            dg_ref[...] = jnp.zeros_like(dg_ref)

        dx_ref[...] = out
        dg_ref[...] += _colsum(dy_ref[...] * xh)

    tile = pl.BlockSpec((tm, d), lambda i: (i, 0))
    fixed = lambda shape: pl.BlockSpec(shape, lambda i: (0, 0))
    return pl.pallas_call(
        body, name=name, grid=(t // tm,),
        in_specs=[tile, fixed((1, d)), tile, tile],
        out_specs=[pl.BlockSpec((tm, d), lambda i: (jnp.maximum(i - 1, 0), 0)), fixed((tm, d)), fixed((1, d))],
        out_shape=[jax.ShapeDtypeStruct((t - ROW0, d), F32), jax.ShapeDtypeStruct((ROW0, d), F32),
                   jax.ShapeDtypeStruct((1, d), F32)],
        compiler_params=_params(("arbitrary",)),
    )(h, g, dy, dres)


def _loss_head(h, g, target_p, t, d):
    def fn(row0, x, gg, tgt):
        real = _row_ids(row0, x.shape[0]) >= ROW0
        r = lax.rsqrt(jnp.mean(x * x, axis=-1, keepdims=True) + EPS)
        xh = x * r
        err = jnp.where(real, xh * gg - tgt, 0.0)
        loss_rows = 0.5 * jnp.mean(err * err, axis=-1, keepdims=True)
        dyv = err * (1.0 / d)
        dxh = dyv * gg
        dx = r * (dxh - xh * jnp.mean(dxh * xh, axis=-1, keepdims=True))
        loss_part = jnp.sum(loss_rows, axis=0, keepdims=True) * jnp.ones((1, LANES), F32)
        return jnp.where(real, dx, 0.0), _colsum(dyv * xh), loss_part

    return _ew("loss_head", fn, [("tile", h, d, 0), ("full", g), ("tile", target_p, d, 0)],
               [("tile", F32, d), ("acc", F32, d), ("acc", F32, LANES)], t, 264)


def _merge_fwd(a_fox, a_gla, proj, c_gates, t, d):
    def fn(row0, af, ag, gates):
        gates = gates.astype(F32)
        return (_sigmoid(gates[:, :d]) * af + _sigmoid(gates[:, d:]) * ag,)

    return _ew("merge_fwd", fn, [("tile", a_fox, d, 0), ("tile", a_gla, d, 0), ("tile", proj, 2 * d, c_gates)],
               [("tile", BF16, d)], t, 264)


def _merge_bwd(dy, a_fox, a_gla, proj, c_gates, t, d):
    def fn(row0, dyv, af, ag, gates):
        gates = gates.astype(F32)
        sf = _sigmoid(gates[:, :d])
        sg = _sigmoid(gates[:, d:])
        dgates = jnp.concatenate([dyv * af * sf * (1.0 - sf), dyv * ag * sg * (1.0 - sg)], axis=1)
        return dyv * sf, dyv * sg, dgates

    return _ew("merge_bwd", fn,
               [("tile", dy, d, 0), ("tile", a_fox, d, 0), ("tile", a_gla, d, 0), ("tile", proj, 2 * d, c_gates)],
               [("tile", BF16, d), ("tile", BF16, d), ("tile", BF16, 2 * d)], t, 264)


def _fox_gate_fwd(small, b_forget_p, t):
    tb = _pick(t, 384, LANES)

    def body(s_ref, b_ref, c_ref, carry_ref):
        i = pl.program_id(0)

        @pl.when(i == 0)
        def _():
            carry_ref[...] = jnp.zeros_like(carry_ref)

        logf = _log_sigmoid(s_ref[...] + b_ref[...])
        logf = jnp.where(_row_ids(i * tb, tb) >= PAD, logf, 0.0)
        r = lax.broadcasted_iota(jnp.int32, (tb, tb), 0)
        c = lax.broadcasted_iota(jnp.int32, (tb, tb), 1)
        tri = (c <= r).astype(BF16)
        cs = _tri_dot(tri, logf) + carry_ref[...]
        c_ref[...] = cs
        carry_ref[...] = cs[tb - 1:tb, :]

    return pl.pallas_call(
        body, name="fox_gate_fwd", grid=(t // tb,),
        in_specs=[pl.BlockSpec((tb, LANES), lambda i: (i, 0)), pl.BlockSpec((1, LANES), lambda i: (0, 0))],
        out_specs=pl.BlockSpec((tb, LANES), lambda i: (i, 0)),
        out_shape=jax.ShapeDtypeStruct((t, LANES), F32),
        scratch_shapes=[pltpu.VMEM((1, LANES), F32)],
        compiler_params=_params(("arbitrary",)),
    )(small, b_forget_p)


def _fox_gate_bwd(dc, small, b_forget_p, dga, t):
    tb = _pick(t, 384, LANES)
    nb = t // tb

    def body(dc_ref, s_ref, b_ref, dga_ref, ds_ref, db_ref, carry_ref):
        i = pl.program_id(0)

        @pl.when(i == 0)
        def _():
            carry_ref[...] = jnp.zeros_like(carry_ref)
            db_ref[...] = jnp.zeros_like(db_ref)

        r = lax.broadcasted_iota(jnp.int32, (tb, tb), 0)
        c = lax.broadcasted_iota(jnp.int32, (tb, tb), 1)
        tri = (c >= r).astype(BF16)
        dlogf = _tri_dot(tri, dc_ref[...]) + carry_ref[...]
        carry_ref[...] = dlogf[0:1, :]
        z = s_ref[...] + b_ref[...]
        dff = dlogf * _sigmoid(-z)
        lane = lax.broadcasted_iota(jnp.int32, (tb, LANES), 1)
        keep = (_row_ids((nb - 1 - i) * tb, tb) >= PAD) & (lane < FOX_HEADS)
        dff = jnp.where(keep, dff, 0.0)
        ds_ref[...] = dff + dga_ref[...]
        db_ref[...] += _colsum(dff)

    rev = lambda i: (nb - 1 - i, 0)
    return pl.pallas_call(
        body, name="fox_gate_bwd", grid=(nb,),
        in_specs=[pl.BlockSpec((tb, LANES), rev), pl.BlockSpec((tb, LANES), rev),
                  pl.BlockSpec((1, LANES), lambda i: (0, 0)), pl.BlockSpec((tb, LANES), rev)],
        out_specs=[pl.BlockSpec((tb, LANES), rev), pl.BlockSpec((1, LANES), lambda i: (0, 0))],
        out_shape=[jax.ShapeDtypeStruct((t, LANES), F32), jax.ShapeDtypeStruct((1, LANES), F32)],
        scratch_shapes=[pltpu.VMEM((1, LANES), F32)],
        compiler_params=_params(("arbitrary",)),
    )(dc, small, b_forget_p, dga)


def _fox_pairs(nb, by_key):
    if by_key:
        pairs = [(qi, ki) for ki in range(nb) for qi in range(ki, nb)]
    else:
        pairs = [(qi, ki) for qi in range(nb) for ki in range(qi + 1)]
    return (jnp.asarray(np.array([p[0] for p in pairs], np.int32)),
            jnp.asarray(np.array([p[1] for p in pairs], np.int32)), len(pairs))


def _fox_specs(tb, fd, c_fq, c_fkv):
    gw = FOX_GROUP * fd
    q0, kv0 = c_fq // gw, c_fkv // (2 * gw)
    return dict(
        q=pl.BlockSpec((tb, gw), lambda g, p, qt, kt: (qt[p], q0 + g)),
        kv=pl.BlockSpec((tb, 2 * gw), lambda g, p, qt, kt: (kt[p], kv0 + g)),
        col=pl.BlockSpec((FOX_GROUP, tb, 1), lambda g, p, qt, kt: (g, qt[p], 0)),
        row=pl.BlockSpec((FOX_GROUP, 1, tb), lambda g, p, qt, kt: (g, 0, kt[p])),
        head=pl.BlockSpec((tb, gw), lambda g, p, qt, kt: (qt[p], g)),
        key_kv=pl.BlockSpec((tb, 2 * gw), lambda g, p, qt, kt: (kt[p], g)),
    )


def _fox_mask(qi, ki, tb):
    row = qi * tb + lax.broadcasted_iota(jnp.int32, (tb, tb), 0)
    col = ki * tb + lax.broadcasted_iota(jnp.int32, (tb, tb), 1)
    return (col <= row) & (col >= PAD)


def _fox_heads(q_ref, kv_ref, fd):
    return [(q_ref[:, hh * fd:(hh + 1) * fd], kv_ref[:, 2 * hh * fd:(2 * hh + 1) * fd],
             kv_ref[:, (2 * hh + 1) * fd:(2 * hh + 2) * fd]) for hh in range(FOX_GROUP)]


def _fox_fwd(proj, c_col, c_row, t, fd, c_fq, c_fkv):
    tb = _pick(t, 384, LANES)
    nb = t // tb
    scale = fd ** -0.5
    sp = _fox_specs(tb, fd, c_fq, c_fkv)
    qt, kt, npairs = _fox_pairs(nb, by_key=False)

    def body(qt_ref, kt_ref, q_ref, kv_ref, cq_ref, ck_ref, o_ref, lse_ref, m_ref, l_ref, acc_ref):
        p = pl.program_id(1)
        qi, ki = qt_ref[p], kt_ref[p]

        @pl.when(ki == 0)
        def _():
            m_ref[...] = jnp.full_like(m_ref, -jnp.inf)
            l_ref[...] = jnp.zeros_like(l_ref)
            acc_ref[...] = jnp.zeros_like(acc_ref)

        def update(masked):
            mask = _fox_mask(qi, ki, tb) if masked else None
            for hh, (q, k, v) in enumerate(_fox_heads(q_ref, kv_ref, fd)):
                s = _dot(q, k, NT) * scale + cq_ref[hh] - ck_ref[hh]
                if masked:
                    s = jnp.where(mask, s, MASK_VALUE)
                m_prev = m_ref[hh]
                m_new = jnp.maximum(m_prev, jnp.max(s, axis=-1, keepdims=True))
                alpha = jnp.exp(m_prev - m_new)
                pe = jnp.exp(s - m_new)
                l_ref[hh] = alpha * l_ref[hh] + jnp.sum(pe, axis=-1, keepdims=True)
                acc_ref[hh] = alpha * acc_ref[hh] + _dot(pe.astype(BF16), v, NN)
                m_ref[hh] = m_new

        edge = (ki == 0) | (ki == qi)
        pl.when(edge)(functools.partial(update, True))
        pl.when(jnp.logical_not(edge))(functools.partial(update, False))

        @pl.when(ki == qi)
        def _():
            real = _row_ids(qi * tb, tb) >= PAD
            for hh in range(FOX_GROUP):
                o_ref[:, hh * fd:(hh + 1) * fd] = jnp.where(real, acc_ref[hh] / l_ref[hh], 0.0)
                lse_ref[hh] = m_ref[hh] + jnp.log(l_ref[hh])

    return pl.pallas_call(
        body, name="fox_fwd",
        grid_spec=pltpu.PrefetchScalarGridSpec(
            num_scalar_prefetch=2, grid=(FOX_HEADS // FOX_GROUP, npairs),
            in_specs=[sp["q"], sp["kv"], sp["col"], sp["row"]],
            out_specs=[sp["head"], sp["col"]],
            scratch_shapes=[pltpu.VMEM((FOX_GROUP, tb, 1), F32), pltpu.VMEM((FOX_GROUP, tb, 1), F32),
                            pltpu.VMEM((FOX_GROUP, tb, fd), F32)]),
        out_shape=[jax.ShapeDtypeStruct((t, FOX_HEADS * fd), F32), jax.ShapeDtypeStruct((FOX_HEADS, t, 1), F32)],
        compiler_params=_params(("parallel", "arbitrary")),
    )(qt, kt, proj, proj, c_col, c_row)


def _fox_delta(o_fox, do_fox, t, fd):
    tb = _pick(t, 384, LANES)

    def body(o_ref, do_ref, out_ref):
        for h in range(FOX_HEADS):
            sl = slice(h * fd, (h + 1) * fd)
            out_ref[h] = jnp.sum(o_ref[:, sl] * do_ref[:, sl].astype(BF16).astype(F32), axis=-1, keepdims=True)

    w = FOX_HEADS * fd
    return pl.pallas_call(
        body, name="fox_delta", grid=(t // tb,),
        in_specs=[pl.BlockSpec((tb, w), lambda i: (i, 0)), pl.BlockSpec((tb, w), lambda i: (i, 0))],
        out_specs=pl.BlockSpec((FOX_HEADS, tb, 1), lambda i: (0, i, 0)),
        out_shape=jax.ShapeDtypeStruct((FOX_HEADS, t, 1), F32),
        compiler_params=_params(("parallel",)),
    )(o_fox, do_fox)


def _fox_bwd(proj, c_col, c_row, lse, delta, do_fox, t, fd, c_fq, c_fkv):
    tb = _pick(t, 384, LANES)
    nb = t // tb
    scale = fd ** -0.5
    sp = _fox_specs(tb, fd, c_fq, c_fkv)
    qt, kt, npairs = _fox_pairs(nb, by_key=True)
    gw = FOX_GROUP * fd

    def body(qt_ref, kt_ref, q_ref, kv_ref, cq_ref, ck_ref, lse_ref, dl_ref, do_ref, dq_ref, dkv_ref, dc_ref, dr_ref,
             dq_acc, dk_acc, dv_acc, dc_acc, dr_acc):
        p = pl.program_id(1)
        qi, ki = qt_ref[p], kt_ref[p]

        @pl.when(p == 0)
        def _():
            dq_acc[...] = jnp.zeros_like(dq_acc)
            dr_acc[...] = jnp.zeros_like(dr_acc)

        @pl.when(qi == ki)
        def _():
            dk_acc[...] = jnp.zeros_like(dk_acc)
            dv_acc[...] = jnp.zeros_like(dv_acc)
            dc_acc[...] = jnp.zeros_like(dc_acc)

        rows = pl.ds(pl.multiple_of(qi * tb, LANES), tb)

        def update(masked):
            mask = _fox_mask(qi, ki, tb) if masked else None
            for hh, (q, k, v) in enumerate(_fox_heads(q_ref, kv_ref, fd)):
                do = _bf(do_ref[:, hh * fd:(hh + 1) * fd])
                s = _dot(q, k, NT) * scale + cq_ref[hh] - ck_ref[hh]
                if masked:
                    s = jnp.where(mask, s, MASK_VALUE)
                pr = jnp.exp(s - lse_ref[hh])
                dp = _dot(do, v, NT)
                ds = pr * (dp - dl_ref[hh])
                ds16 = ds.astype(BF16)
                dv_acc[hh] += _dot(pr.astype(BF16), do, TN)
                dk_acc[hh] += _dot(ds16, q, TN)
                dc_acc[hh] += _colsum(ds)
                dr_acc[hh, rows, :] += jnp.sum(ds, axis=-1, keepdims=True)
                dq_acc[hh, rows, :] += _dot(ds16, k, NN)

        edge = (ki == 0) | (ki == qi)
        pl.when(edge)(functools.partial(update, True))
        pl.when(jnp.logical_not(edge))(functools.partial(update, False))

        @pl.when(qi == nb - 1)
        def _():
            for hh in range(FOX_GROUP):
                dkv_ref[:, 2 * hh * fd:(2 * hh + 1) * fd] = (dk_acc[hh] * scale).astype(dkv_ref.dtype)
                dkv_ref[:, (2 * hh + 1) * fd:(2 * hh + 2) * fd] = dv_acc[hh].astype(dkv_ref.dtype)
                dc_ref[hh] = -dc_acc[hh]

        @pl.when(p == npairs - 1)
        def _():
            for hh in range(FOX_GROUP):
                dq_ref[:, hh * fd:(hh + 1) * fd] = (dq_acc[hh] * scale).astype(dq_ref.dtype)
            dr_ref[...] = dr_acc[...]

    return pl.pallas_call(
        body, name="fox_bwd",
        grid_spec=pltpu.PrefetchScalarGridSpec(
            num_scalar_prefetch=2, grid=(FOX_HEADS // FOX_GROUP, npairs),
            in_specs=[sp["q"], sp["kv"], sp["col"], sp["row"], sp["col"], sp["col"], sp["head"]],
            out_specs=[pl.BlockSpec((t, gw), lambda g, p, qt, kt: (0, g)), sp["key_kv"], sp["row"],
                       pl.BlockSpec((FOX_GROUP, t, 1), lambda g, p, qt, kt: (g, 0, 0))],
            scratch_shapes=[pltpu.VMEM((FOX_GROUP, t, fd), F32), pltpu.VMEM((FOX_GROUP, tb, fd), F32),
                            pltpu.VMEM((FOX_GROUP, tb, fd), F32), pltpu.VMEM((FOX_GROUP, 1, tb), F32),
                            pltpu.VMEM((FOX_GROUP, t, 1), F32)]),
        out_shape=[jax.ShapeDtypeStruct((t, FOX_HEADS * fd), BF16), jax.ShapeDtypeStruct((t, 2 * FOX_HEADS * fd), BF16),
                   jax.ShapeDtypeStruct((FOX_HEADS, 1, t), F32), jax.ShapeDtypeStruct((FOX_HEADS, t, 1), F32)],
        compiler_params=_params(("parallel", "arbitrary")),
    )(qt, kt, proj, proj, c_col, c_row, lse, delta, do_fox)


def _gla_gate_fwd(small, w_alpha_p, b_alpha, t, gk):
    def fn(row0, s, w, b):
        z = _dot(s.astype(BF16), w, NN) + b
        return (jnp.where(_row_ids(row0, s.shape[0]) >= PAD, _log_sigmoid(z) * (1.0 / GLA_TAU), 0.0),)

    return _ew("gla_gate_fwd", fn, [("tile", small, LANES, 0), ("full", w_alpha_p), ("full", b_alpha)],
               [("tile", F32, gk)], t, 264)


def _gla_gate_bwd(dglog, small, w_alpha_p, b_alpha, t, gk):
    def fn(row0, dg, s, w, b):
        z = _dot(s.astype(BF16), w, NN) + b
        dz = jnp.where(_row_ids(row0, s.shape[0]) >= PAD, dg * (1.0 / GLA_TAU) * _sigmoid(-z), 0.0)
        return dz, _colsum(dz)

    return _ew("gla_gate_bwd", fn,
               [("tile", dglog, gk, 0), ("tile", small, LANES, 0), ("full", w_alpha_p), ("full", b_alpha)],
               [("tile", BF16, gk), ("acc", F32, gk)], t, 264)


def _gla_chunk(q, k, g, scale, cs):
    r = lax.broadcasted_iota(jnp.int32, (cs, cs), 0)
    c = lax.broadcasted_iota(jnp.int32, (cs, cs), 1)
    causal = c <= r
    b = _tri_dot(causal.astype(BF16), g)
    bl = b[cs - 1:cs, :]
    eb, einv, eend = jnp.exp(b), jnp.exp(-b), jnp.exp(bl - b)
    qd = q.astype(F32) * scale * eb
    kf = k.astype(F32)
    return causal, (eb, einv, eend), bl, qd, kf * einv, kf * eend


def _gla_fwd(proj, glog, t, dk, dv, c_q, c_k, c_v):
    cs = GLA_CHUNK
    nc = t // cs
    wk, wv = GLA_HEADS * dk, GLA_HEADS * dv
    scale = dk ** -0.5

    def body(q_ref, k_ref, v_ref, g_ref, o_ref, sp_ref, st_ref):
        @pl.when(pl.program_id(0) == 0)
        def _():
            st_ref[...] = jnp.zeros_like(st_ref)

        for h in range(GLA_HEADS):
            ks, vs = slice(h * dk, (h + 1) * dk), slice(h * dv, (h + 1) * dv)
            v = v_ref[:, vs]
            causal, _, bl, qd, ki, ke = _gla_chunk(q_ref[:, ks], k_ref[:, ks], g_ref[:, ks], scale, cs)
            st = st_ref[h]
            sp_ref[h] = st
            a = jnp.where(causal, _dot(qd.astype(BF16), ki.astype(BF16), NT), 0.0)
            o_ref[:, vs] = _dot(a.astype(BF16), v, NN) + _dot(qd.astype(BF16), st.astype(BF16), NT)
            st_ref[h] = st * jnp.exp(bl) + _dot(v, ke.astype(BF16), TN)

    return pl.pallas_call(
        body, name="gla_fwd", grid=(nc,),
        in_specs=[pl.BlockSpec((cs, wk), lambda n: (n, c_q // wk)), pl.BlockSpec((cs, wk), lambda n: (n, c_k // wk)),
                  pl.BlockSpec((cs, wv), lambda n: (n, c_v // wv)), pl.BlockSpec((cs, wk), lambda n: (n, 0))],
        out_specs=[pl.BlockSpec((cs, wv), lambda n: (n, 0)),
                   pl.BlockSpec((None, GLA_HEADS, dv, dk), lambda n: (n, 0, 0, 0))],
        out_shape=[jax.ShapeDtypeStruct((t, wv), F32), jax.ShapeDtypeStruct((nc, GLA_HEADS, dv, dk), F32)],
        scratch_shapes=[pltpu.VMEM((GLA_HEADS, dv, dk), F32)],
        compiler_params=_params(("arbitrary",)),
    )(proj, proj, proj, glog)


def _gla_bwd(proj, glog, s_prev, do_raw, t, dk, dv, c_q, c_k, c_v):
    cs = GLA_CHUNK
    nc = t // cs
    wk, wv = GLA_HEADS * dk, GLA_HEADS * dv
    scale = dk ** -0.5

    def body(q_ref, k_ref, v_ref, g_ref, sp_ref, do_ref, dq_ref, dk_ref, dv_ref, dg_ref, dst_ref):
        @pl.when(pl.program_id(0) == 0)
        def _():
            dst_ref[...] = jnp.zeros_like(dst_ref)

        for h in range(GLA_HEADS):
            ks, vs = slice(h * dk, (h + 1) * dk), slice(h * dv, (h + 1) * dv)
            v = v_ref[:, vs]
            do = do_ref[:, vs].astype(BF16)
            causal, (eb, einv, eend), bl, qd, ki, ke = _gla_chunk(q_ref[:, ks], k_ref[:, ks], g_ref[:, ks], scale, cs)
            qd16, ki16, ke16 = qd.astype(BF16), ki.astype(BF16), ke.astype(BF16)
            st = sp_ref[h]
            dst = dst_ref[h]
            dst16 = dst.astype(BF16)
            a = jnp.where(causal, _dot(qd16, ki16, NT), 0.0).astype(BF16)
            da = jnp.where(causal, _dot(do, v, NT), 0.0).astype(BF16)
            dvv = _dot(a, do, TN) + _dot(ke16, dst16, NT)
            dqd = _dot(da, ki16, NN) + _dot(do, st.astype(BF16), NN)
            dki = _dot(da, qd16, TN)
            dke = _dot(v, dst16, NN)
            dl = jnp.exp(bl)
            ddl = _colsum(dst * st)
            dst_ref[h] = dst * dl + _dot(do, qd16, TN)
            dq_ref[:, ks] = (dqd * eb * scale).astype(dq_ref.dtype)
            dk_ref[:, ks] = (dki * einv + dke * eend).astype(dk_ref.dtype)
            dv_ref[:, vs] = dvv.astype(dv_ref.dtype)
            db = dqd * qd - dki * ki - dke * ke
            db_last = _colsum(dke * ke) + ddl * dl
            r = lax.broadcasted_iota(jnp.int32, (cs, cs), 0)
            c = lax.broadcasted_iota(jnp.int32, (cs, cs), 1)
            dg_ref[:, ks] = _tri_dot((c >= r).astype(BF16), db) + db_last

    rev = lambda f: (lambda n: f(nc - 1 - n))
    return pl.pallas_call(
        body, name="gla_bwd", grid=(nc,),
        in_specs=[pl.BlockSpec((cs, wk), rev(lambda n: (n, c_q // wk))), pl.BlockSpec((cs, wk), rev(lambda n: (n, c_k // wk))),
                  pl.BlockSpec((cs, wv), rev(lambda n: (n, c_v // wv))), pl.BlockSpec((cs, wk), rev(lambda n: (n, 0))),
                  pl.BlockSpec((None, GLA_HEADS, dv, dk), rev(lambda n: (n, 0, 0, 0))),
                  pl.BlockSpec((cs, wv), rev(lambda n: (n, 0)))],
        out_specs=[pl.BlockSpec((cs, wk), rev(lambda n: (n, 0))), pl.BlockSpec((cs, wk), rev(lambda n: (n, 0))),
                   pl.BlockSpec((cs, wv), rev(lambda n: (n, 0))), pl.BlockSpec((cs, wk), rev(lambda n: (n, 0)))],
        out_shape=[jax.ShapeDtypeStruct((t, wk), BF16), jax.ShapeDtypeStruct((t, wk), BF16),
                   jax.ShapeDtypeStruct((t, wv), BF16), jax.ShapeDtypeStruct((t, wk), F32)],
        scratch_shapes=[pltpu.VMEM((GLA_HEADS, dv, dk), F32)],
        compiler_params=_params(("arbitrary",)),
    )(proj, proj, proj, glog, s_prev, do_raw)


def _gla_post_fwd(o_raw, proj, gn, t, dv, c_gr):
    w = GLA_HEADS * dv

    def fn(row0, o, gr, g):
        gr = gr.astype(F32)
        outs = []
        for h in range(GLA_HEADS):
            oh = o[:, h * dv:(h + 1) * dv]
            outs.append(oh * lax.rsqrt(jnp.mean(oh * oh, axis=-1, keepdims=True) + EPS))
        on = jnp.concatenate(outs, axis=1) * g
        return (on * (gr * _sigmoid(gr)),)

    return _ew("gla_post_fwd", fn, [("tile", o_raw, w, 0), ("tile", proj, w, c_gr), ("full", gn)],
               [("tile", BF16, w)], t, 264)


def _gla_post_bwd(o_raw, proj, gn, do_gla, t, dv, c_gr):
    w = GLA_HEADS * dv

    def fn(row0, o, gr, g, do):
        gr = gr.astype(F32)
        sg = _sigmoid(gr)
        don = do * (gr * sg)
        ohs, dos = [], []
        for h in range(GLA_HEADS):
            sl = slice(h * dv, (h + 1) * dv)
            oh = o[:, sl]
            r = lax.rsqrt(jnp.mean(oh * oh, axis=-1, keepdims=True) + EPS)
            xh = oh * r
            dxh = don[:, sl] * g[:, sl]
            ohs.append(xh)
            dos.append(r * (dxh - xh * jnp.mean(dxh * xh, axis=-1, keepdims=True)))
        xh = jnp.concatenate(ohs, axis=1)
        dgr = do * (xh * g) * (sg * (1.0 + gr * (1.0 - sg)))
        return jnp.concatenate(dos, axis=1), dgr, _colsum(don * xh)

    return _ew("gla_post_bwd", fn,
               [("tile", o_raw, w, 0), ("tile", proj, w, c_gr), ("full", gn), ("tile", do_gla, w, 0)],
               [("tile", F32, w), ("tile", BF16, w), ("acc", F32, w)], t, 264)


def _adamw(name, w, g, m, v, layer=None, into=None):
    nl, rows, cols = w.shape
    tm = _pick(rows, max(8, (512 * 1024) // max(cols, 1) // 8 * 8), 8)

    def body(w_ref, g_ref, m_ref, v_ref, *rest):
        go_ref, d_ref, nm_ref, nv_ref = rest[-4:]
        gg = g_ref[...]
        nm = ADAM_B1 * m_ref[...] + (1.0 - ADAM_B1) * gg
        nv = ADAM_B2 * v_ref[...] + (1.0 - ADAM_B2) * (gg * gg)
        m_hat = nm / (1.0 - ADAM_B1 ** ADAM_STEP)
        v_hat = nv / (1.0 - ADAM_B2 ** ADAM_STEP)
        go_ref[...] = gg
        d_ref[...] = -ADAM_LR * (m_hat / (jnp.sqrt(v_hat) + ADAM_EPS) + ADAM_WD * w_ref[...])
        nm_ref[...] = nm
        nv_ref[...] = nv

    out_shape = [jax.ShapeDtypeStruct((nl, rows, cols), F32)] * 4
    if layer is None:
        spec = pl.BlockSpec((None, tm, cols), lambda l, i: (l, i, 0))
        return pl.pallas_call(
            body, name=name, grid=(nl, rows // tm), in_specs=[spec] * 4, out_specs=[spec] * 4, out_shape=out_shape,
            compiler_params=_params(("parallel", "parallel")),
        )(w, g, m, v)
    spec = pl.BlockSpec((None, tm, cols), lambda i: (layer, i, 0))
    in_specs = [spec, pl.BlockSpec((tm, cols), lambda i: (i, 0)), spec, spec]
    args, aliases = [w, g, m, v], {}
    if into is not None:
        in_specs += [pl.BlockSpec(memory_space=pl.ANY)] * 4
        args += list(into)
        aliases = {4 + k: k for k in range(4)}
    return pl.pallas_call(
        body, name=name, grid=(rows // tm,), in_specs=in_specs, out_specs=[spec] * 4, out_shape=out_shape,
        input_output_aliases=aliases, compiler_params=_params(("parallel",)),
    )(*args)


def _me():
    return lax.axis_index("x"), lax.axis_index("y"), lax.axis_index("c")


def _hbm_specs(n):
    return [pl.BlockSpec(memory_space=pl.ANY)] * n


_HBM = pl.BlockSpec(memory_space=pltpu.HBM)
_SEM = pl.BlockSpec(memory_space=pltpu.SEMAPHORE)
_EFFECT = pltpu.SideEffectType.DATAFLOW_SIDE_EFFECTING
N_PEERS = N_CHIPS - 1


def _other_chips(x, y):
    return [(1 - x, y), (x, 1 - y), (1 - x, 1 - y)]


def _split_copies(plan, src, land, send_sems, recv_sems):
    me = _me()
    copies = []
    for i in range(len(src)):
        for j, (s, d, peer) in enumerate(plan(src[i], land[i], me)):
            k = plan.copies * i + j
            copies.append(pltpu.make_async_remote_copy(
                src_ref=s, dst_ref=d, send_sem=send_sems.at[k], recv_sem=recv_sems.at[k], device_id=peer,
                device_id_type=MESH))
    return copies


def _split_start(name, srcs, lands, plan, after=None):
    n = len(srcs)
    extra = [] if after is None else [after]

    def body(*refs):
        src, land = refs[:n], refs[n:2 * n]
        send_sems, recv_sems = refs[2 * n + len(extra)], refs[2 * n + len(extra) + 1]
        token = refs[-1]
        for cp in _split_copies(plan, src, land, send_sems, recv_sems):
            cp.start()
        token[...] = jnp.zeros_like(token)

    out_shape = ([pltpu.SemaphoreType.DMA((plan.copies * n,)), pltpu.SemaphoreType.DMA((plan.copies * n,))]
                 + [pltpu.HBM(a.shape, a.dtype) for a in list(srcs) + list(lands)]
                 + [jax.ShapeDtypeStruct((8, LANES), F32)])
    res = pl.pallas_call(
        body, name=name, out_shape=out_shape,
        in_specs=[_HBM] * (2 * n) + [pl.BlockSpec(memory_space=pl.ANY)] * len(extra),
        out_specs=[_SEM, _SEM] + [_HBM] * (2 * n) + [pl.BlockSpec(memory_space=pltpu.VMEM)],
        input_output_aliases={i: 2 + i for i in range(2 * n)},
        compiler_params=pltpu.CompilerParams(has_side_effects=_EFFECT),
    )(*[pltpu.with_memory_space_constraint(a, pltpu.HBM) for a in list(srcs) + list(lands)], *extra)
    return res[0], res[1], res[2:2 + n], res[2 + n:2 + 2 * n], res[-1]


def _split_wait(name, send_sems, recv_sems, srcs, lands, plan, after):
    n = len(srcs)

    def body(*refs):
        src, land = refs[:n], refs[n:2 * n]
        s_sems, r_sems = refs[2 * n], refs[2 * n + 1]
        for cp in _split_copies(plan, src, land, s_sems, r_sems):
            cp.wait_send()
            cp.wait_recv()

    res = pl.pallas_call(
        body, name=name, out_shape=[pltpu.HBM(a.shape, a.dtype) for a in list(srcs) + list(lands)],
        in_specs=[_HBM] * (2 * n) + [_SEM, _SEM, pl.BlockSpec(memory_space=pl.ANY)], out_specs=[_HBM] * (2 * n),
        input_output_aliases={i: i for i in range(2 * n)},
        compiler_params=pltpu.CompilerParams(has_side_effects=_EFFECT),
    )(*srcs, *lands, send_sems, recv_sems, after)
    return res[:n], res[n:]


def _half(ref_rows, c):
    half = ref_rows // 2
    return pl.ds(c * half, half)


def _gather_plan(src, land, me):
    x, y, c = me
    rows = _half(src.shape[0], c)
    return [(src.at[rows], land.at[2 * x + y, rows], (px, py, c)) for px, py in _other_chips(x, y)]


def _scatter_plan(src, land, me):
    x, y, c = me
    return [(src.at[2 * px + py], land.at[2 * x + y], (px, py, c)) for px, py in _other_chips(x, y)]


def _swap_plan(src, land, me):
    x, y, c = me
    return [(src.at[:, _half(src.shape[1], 1 - c)], land, (x, y, 1 - c))]


_gather_plan.copies = N_PEERS
_scatter_plan.copies = N_PEERS
_swap_plan.copies = 1


def _forward_halves(name, gathered):
    n = len(gathered)

    def body(*refs):
        buf = refs[n:2 * n]
        send_sems, recv_sems = refs[2 * n:]
        x, y, c = _me()
        copies = []
        for i in range(n):
            rows = _half(buf[i].shape[1], c)
            for j, (px, py) in enumerate(_other_chips(x, y)):
                cp = pltpu.make_async_remote_copy(
                    src_ref=buf[i].at[2 * px + py, rows], dst_ref=buf[i].at[2 * px + py, rows],
                    send_sem=send_sems.at[N_PEERS * i + j], recv_sem=recv_sems.at[N_PEERS * i + j],
                    device_id=(x, y, 1 - c), device_id_type=MESH)
                cp.start()
                copies.append(cp)
        for cp in copies:
            cp.wait_recv()
        for cp in copies:
            cp.wait_send()

    return pl.pallas_call(
        body, name=name, in_specs=_hbm_specs(n), out_specs=_hbm_specs(n),
        out_shape=[jax.ShapeDtypeStruct(g.shape, g.dtype) for g in gathered],
        input_output_aliases={i: i for i in range(n)},
        scratch_shapes=[pltpu.SemaphoreType.DMA((N_PEERS * n,)), pltpu.SemaphoreType.DMA((N_PEERS * n,))],
    )(*gathered)


def _all_gather_small(v):
    def body(v_ref, out_ref, send_sems, recv_sems, local_sem):
        x, y, c = _me()
        mine = pltpu.make_async_copy(v_ref, out_ref.at[4 * x + 2 * y + c], local_sem)
        mine.start()
        copies = []
        for k in range(1, N_DEV):
            peer = (x ^ ((k >> 2) & 1), y ^ ((k >> 1) & 1), c ^ (k & 1))
            cp = pltpu.make_async_remote_copy(
                src_ref=v_ref, dst_ref=out_ref.at[4 * x + 2 * y + c], send_sem=send_sems.at[k - 1],
                recv_sem=recv_sems.at[k - 1], device_id=peer, device_id_type=MESH)
            cp.start()
            copies.append(cp)
        for cp in copies:
            cp.wait_recv()
        for cp in copies:
            cp.wait_send()
        mine.wait()

    return pl.pallas_call(
        body, name="all_gather_small", in_specs=_hbm_specs(1), out_specs=pl.BlockSpec(memory_space=pl.ANY),
        out_shape=jax.ShapeDtypeStruct((N_DEV,) + v.shape, v.dtype),
        scratch_shapes=[pltpu.SemaphoreType.DMA((N_DEV - 1,)), pltpu.SemaphoreType.DMA((N_DEV - 1,)),
                        pltpu.SemaphoreType.DMA],
    )(v)


def _share_halves(name, grads):
    n = len(grads)

    def body(*refs):
        buf = refs[n:2 * n]
        send_sems, recv_sems = refs[2 * n:]
        x, y, c = _me()
        copies = []
        for i in range(n):
            rows = _half(buf[i].shape[0], c)
            cp = pltpu.make_async_remote_copy(
                src_ref=buf[i].at[rows], dst_ref=buf[i].at[rows], send_sem=send_sems.at[i],
                recv_sem=recv_sems.at[i], device_id=(x, y, 1 - c), device_id_type=MESH)
            cp.start()
            copies.append(cp)
        for cp in copies:
            cp.wait_recv()
        for cp in copies:
            cp.wait_send()

    out_shape = [jax.ShapeDtypeStruct(s.shape, s.dtype) for s in grads]
    return pl.pallas_call(
        body, name=name, in_specs=_hbm_specs(n), out_specs=_hbm_specs(n), out_shape=out_shape,
        input_output_aliases={i: i for i in range(n)},
        scratch_shapes=[pltpu.SemaphoreType.DMA((n,)), pltpu.SemaphoreType.DMA((n,))],
    )(*grads)


def _pair_sum(name, own, recv, chip, core):
    nch, half, cdim = recv.shape
    tm = _pick(half, max(8, (512 * 1024) // cdim // 16 * 16), 16)
    nt = half // tm

    def body(c_ref, k_ref, a_ref, b_ref, s16_ref):
        s16_ref[...] = (a_ref[...] + b_ref[...]).astype(BF16)

    other = lambda j, c: (c[0] + 1 + j) % nch
    spec = pl.BlockSpec((None, tm, cdim), lambda j, i, c, k: (other(j, c), i, 0))
    return pl.pallas_call(
        body, name=name,
        grid_spec=pltpu.PrefetchScalarGridSpec(
            num_scalar_prefetch=2, grid=(nch - 1, nt),
            in_specs=[pl.BlockSpec((None, tm, cdim), lambda j, i, c, k: (other(j, c), k[0] * nt + i, 0)), spec],
            out_specs=spec),
        out_shape=jax.ShapeDtypeStruct((nch, half, cdim), BF16),
        compiler_params=_params(("parallel", "parallel")),
    )(chip, core, own, recv)


def _chip_sum(name, own, recv, landed, chip, core):
    nch, half, cdim = recv.shape
    tm = _pick(half, max(8, (512 * 1024) // cdim // 16 * 16), 16)
    nt = half // tm

    def body(c_ref, k_ref, own_ref, recv_ref, *rest):
        landed_refs, out_ref = rest[:nch], rest[-1]
        me = c_ref[0]
        mine = own_ref[...] + recv_ref[...]
        acc = None
        for j in range(nch):
            term = jnp.where(me == j, mine, landed_refs[j][...].astype(F32))
            acc = term if acc is None else acc + term
        out_ref[...] = acc

    landed_specs = [pl.BlockSpec((None, tm, cdim), functools.partial(lambda i, c, k, j: (j, i, 0), j=j))
                    for j in range(nch)]
    return pl.pallas_call(
        body, name=name,
        grid_spec=pltpu.PrefetchScalarGridSpec(
            num_scalar_prefetch=2, grid=(nt,),
            in_specs=[pl.BlockSpec((None, tm, cdim), lambda i, c, k: (c[0], k[0] * nt + i, 0)),
                      pl.BlockSpec((None, tm, cdim), lambda i, c, k: (c[0], i, 0))] + landed_specs,
            out_specs=pl.BlockSpec((tm, cdim), lambda i, c, k: (k[0] * nt + i, 0))),
        out_shape=jax.ShapeDtypeStruct((2 * half, cdim), F32),
        compiler_params=_params(("parallel",)),
    )(chip, core, own, recv, *([landed] * nch))


def _sum_devices(gathered):
    _, r, cdim = gathered.shape

    def body(g_ref, o_ref):
        acc = g_ref[0]
        for k in range(1, N_DEV):
            acc = acc + g_ref[k]
        o_ref[...] = acc

    return pl.pallas_call(
        body, name="sum_devices", out_shape=jax.ShapeDtypeStruct((r, cdim), F32),
        compiler_params=_params(),
    )(gathered)


class _Layout:
    def __init__(self, d):
        self.d = d
        self.fw = d // 2
        self.fd = self.fw // FOX_HEADS
        self.gk = d // 2
        self.gv = d
        self.dk = self.gk // GLA_HEADS
        self.dv = self.gv // GLA_HEADS
        self.c_fq = 0
        self.c_gq = self.fw
        self.c_gv = self.c_gq + self.gk
        self.c_gr = self.c_gv + self.gv
        self.c_fkv = self.c_gr + self.gv
        self.c_gates = self.c_fkv + 2 * self.fw
        self.c_gk = self.c_gates + 2 * d
        self.c_small = self.c_gk + self.gk
        self.n_main = self.c_small
        self.n_p = self.c_small + LANES
        self.o_fk = self.fw
        self.o_fv = 2 * self.fw
        self.o_ff = 3 * self.fw
        self.o_gq = self.o_ff + FOX_HEADS
        self.o_gk = self.o_gq + self.gk
        self.o_gv = self.o_gk + self.gk
        self.o_gr = self.o_gv + self.gv
        self.o_ga = self.o_gr + self.gv
        self.o_gf = self.o_ga + GLA_RANK
        self.o_gg = self.o_gf + d
        self.n_orig = self.o_gg + d

    def to_p(self, shards):
        per = self.n_orig // N_CHIPS
        ranges = [(0, self.fw), (self.o_gq, self.gk), (self.o_gv, self.gv), (self.o_gr, self.gv)]
        for h in range(FOX_HEADS):
            ranges += [(self.o_fk + h * self.fd, self.fd), (self.o_fv + h * self.fd, self.fd)]
        ranges += [(self.o_gf, 2 * self.d), (self.o_gk, self.gk), (self.o_ff, FOX_HEADS), (self.o_ga, GLA_RANK)]
        pieces = []
        for a, width in ranges:
            for j in range(a // per, (a + width - 1) // per + 1):
                lo, hi = max(a, j * per), min(a + width, (j + 1) * per)
                pieces.append(shards[j][:, lo - j * per:hi - j * per])
        pieces.append(jnp.zeros((shards.shape[1], LANES - FOX_HEADS - GLA_RANK), shards.dtype))
        return jnp.concatenate(pieces, axis=1)

    def from_segments(self, seg):
        per = self.n_orig // N_CHIPS
        fd = self.fd
        atoms = [("fq", 0, self.fw)]
        atoms += [("fkv", 2 * h * fd, fd) for h in range(FOX_HEADS)]
        atoms += [("fkv", (2 * h + 1) * fd, fd) for h in range(FOX_HEADS)]
        atoms += [("small", 0, FOX_HEADS), ("gq", 0, self.gk), ("gk", 0, self.gk), ("gv", 0, self.gv),
                  ("gr", 0, self.gv), ("small", FOX_HEADS, GLA_RANK), ("gates", 0, 2 * self.d)]
        shards = [[] for _ in range(N_CHIPS)]
        pos = 0
        for name, c0, width in atoms:
            for j in range(pos // per, (pos + width - 1) // per + 1):
                lo, hi = max(pos, j * per), min(pos + width, (j + 1) * per)
                shards[j].append(seg[name][:, c0 + lo - pos:c0 + hi - pos])
            pos += width
        assert pos == self.n_orig
        return jnp.stack([jnp.concatenate(s, axis=1) for s in shards])


def _layer_fwd(lay, h, p, t, late=None):
    d = lay.d
    xn, xn_t = _rms_fwd_t("rms_mix_fwd", h, p["norm_mix_g"], t, d)
    proj = _mm("mm_proj", xn, p["w_in"], mode="nn", m=t, n=lay.n_main, k=d, out_dtypes=(BF16,))
    small = _mm("mm_small", xn, p["w_in"], mode="nn", m=t, n=LANES, k=d, b_c0=lay.c_small)
    cs = _fox_gate_fwd(small, p["b_forget_p"], t)
    ct = cs[:, :FOX_HEADS].T
    c_col, c_row = ct[:, :, None], ct[:, None, :]
    o_fox, lse = _fox_fwd(proj, c_col, c_row, t, lay.fd, lay.c_fq, lay.c_fkv)
    glog = _gla_gate_fwd(small, p["w_alpha_p"], p["b_alpha"], t, lay.gk)
    o_raw, s_prev = _gla_fwd(proj, glog, t, lay.dk, lay.dv, lay.c_gq, lay.c_gk, lay.c_gv)
    o_gla = _gla_post_fwd(o_raw, proj, p["gla_norm_g"], t, lay.dv, lay.c_gr)
    if late is not None:
        p.update(late(o_gla))
    a_fox = _mm("mm_o_fox", o_fox, p["w_o_fox"], mode="nn", m=t, n=d, k=lay.fw, b_shards=N_CHIPS)
    a_gla = _mm("mm_o_gla", o_gla, p["w_o_gla"], mode="nn", m=t, n=d, k=lay.gv)
    y = _merge_fwd(a_fox, a_gla, proj, lay.c_gates, t, d)
    h1 = _mm("mm_out", y, p["w_out"], mode="nn", m=t, n=d, k=d, extras=[h], epilogue=lambda acc, res: (res + acc,))
    xn2, xn2_t = _rms_fwd_t("rms_mlp_fwd", h1, p["norm_mlp_g"], t, d)
    u, act = _mm("mm_ff1", xn2, p["w_ff1"], mode="nn", m=t, n=4 * d, k=d, out_dtypes=(BF16, BF16), b_shards=N_CHIPS,
                 epilogue=lambda acc: (acc, jnp.square(jnp.maximum(acc, 0.0))))
    h2 = _mm("mm_ff2", act, p["w_ff2"], mode="nn", m=t, n=d, k=4 * d, extras=[h1],
             epilogue=lambda acc, res: (res + acc,))
    saved = dict(h=h, xn_t=xn_t, proj=proj, small=small, c_col=c_col, c_row=c_row, o_fox=o_fox, lse=lse, glog=glog,
                 o_raw=o_raw, s_prev=s_prev, o_gla=o_gla, a_fox=a_fox, a_gla=a_gla, y=y, h1=h1, xn2_t=xn2_t, u=u, act=act)
    return h2, saved


def _layer_bwd(lay, dh2, p, s, t, gates=None, first=False):
    d = lay.d
    g = {}

    def gated(gain, point):
        return gain + gates[point](g) if gates and point in gates else gain
    du = _mm("mm_dact", dh2, p["w_ff2"], mode="nt", m=t, n=4 * d, k=d, extras=[s["u"]], out_dtypes=(BF16,),
             epilogue=lambda acc, u: (acc * (2.0 * jnp.maximum(u.astype(F32), 0.0)),))
    g["w_ff2"] = _mm("mm_dw_ff2", s["act"], dh2, mode="tn", m=4 * d, n=d, k=t)
    g["w_ff1"] = _mm("mm_dw_ff1", s["xn2_t"], du, mode="nn", m=d, n=4 * d, k=t, tk=t, out_shards=N_CHIPS)
    dxn2 = _mm("mm_dxn2", du, p["w_ff1"], mode="nt", m=t, n=d, k=4 * d, b_shards=N_CHIPS)
    dh1, g["norm_mlp_g"] = _rms_bwd("rms_mlp_bwd", s["h1"], gated(p["norm_mlp_g"], "mlp"), dxn2, dh2, t, d)
    dy = _mm("mm_dy", dh1, p["w_out"], mode="nt", m=t, n=d, k=d)
    g["w_out"] = _mm("mm_dw_out", s["y"], dh1, mode="tn", m=d, n=d, k=t)
    da_fox, da_gla, dgates = _merge_bwd(dy, s["a_fox"], s["a_gla"], s["proj"], lay.c_gates, t, d)
    g["w_o_fox"] = _mm("mm_dw_o_fox", s["o_fox"], da_fox, mode="tn", m=lay.fw, n=d, k=t, out_shards=N_CHIPS)
    do_fox = _mm("mm_do_fox", da_fox, p["w_o_fox"], mode="nt", m=t, n=lay.fw, k=d, b_shards=N_CHIPS)
    g["w_o_gla"] = _mm("mm_dw_o_gla", s["o_gla"], da_gla, mode="tn", m=lay.gv, n=d, k=t)
    do_gla = _mm("mm_do_gla", da_gla, p["w_o_gla"], mode="nt", m=t, n=lay.gv, k=d)
    do_raw, dgr, g["gla_norm_g"] = _gla_post_bwd(s["o_raw"], s["proj"], gated(p["gla_norm_g"], "out"), do_gla, t,
                                                 lay.dv, lay.c_gr)
    dgq, dgk, dgv, dglog = _gla_bwd(s["proj"], s["glog"], s["s_prev"], do_raw, t, lay.dk, lay.dv,
                                    lay.c_gq, lay.c_gk, lay.c_gv)
    dz, g["b_alpha"] = _gla_gate_bwd(dglog, s["small"], p["w_alpha_p"], p["b_alpha"], t, lay.gk)
    g["w_alpha_p"] = _mm("mm_dw_alpha", s["small"], dz, mode="tn", m=LANES, n=lay.gk, k=t)
    dga = _mm("mm_dga", dz, p["w_alpha_p"], mode="nt", m=t, n=LANES, k=lay.gk)
    delta = _fox_delta(s["o_fox"], do_fox, t, lay.fd)
    dfq, dfkv, dc, dr = _fox_bwd(s["proj"], s["c_col"], s["c_row"], s["lse"], delta, do_fox, t, lay.fd,
                                 lay.c_fq, lay.c_fkv)
    dc_p = jnp.pad((dc[:, 0, :] + dr[:, :, 0]).T, ((0, 0), (0, LANES - FOX_HEADS)))
    dsmall, g["b_forget_p"] = _fox_gate_bwd(dc_p, s["small"], p["b_forget_p"], dga, t)
    segs = [("fq", dfq, lay.c_fq), ("gq", dgq, lay.c_gq), ("gv", dgv, lay.c_gv), ("gr", dgr, lay.c_gr),
            ("fkv", dfkv, lay.c_fkv), ("gates", dgates, lay.c_gates), ("gk", dgk, lay.c_gk),
            ("small", dsmall, lay.c_small)]
    dxn = None
    dw_in = {}
    for nm, dseg, c0 in segs:
        width = dseg.shape[1]
        dw_in[nm] = _mm("mm_dw_in_" + nm, s["xn_t"], dseg, mode="nn", m=d, n=width, k=t, tk=t)
        if dxn is None:
            dxn = _mm("mm_dxn_" + nm, dseg, p["w_in"], mode="nt", m=t, n=d, k=width, b_c0=c0)
        else:
            dxn = _mm("mm_dxn_" + nm, dseg, p["w_in"], mode="nt", m=t, n=d, k=width, b_c0=c0, extras=[dxn],
                      epilogue=lambda acc, prev: (prev + acc,))
    g["w_in"] = lay.from_segments(dw_in)
    if first:
        dx, head, g["norm_mix_g"] = _rms_bwd_x("rms_mix_bwd_x", s["h"], gated(p["norm_mix_g"], "in"), dxn, dh1, t, d)
        return (dx, head), g
    dh, g["norm_mix_g"] = _rms_bwd("rms_mix_bwd", s["h"], gated(p["norm_mix_g"], "in"), dxn, dh1, t, d)
    return dh, g


def _sequence_step(x, target, meta, layers, final_g):
    seq, d = x.shape
    t = seq + ROW0
    lay = _Layout(d)
    h = jnp.pad(x, ((ROW0, 0), (0, 0))).at[PAD:ROW0].set(meta)
    target_p = jnp.pad(target, ((ROW0, 0), (0, 0)))
    saved = []
    for p in layers:
        h, s = _layer_fwd(lay, h, p, t)
        saved.append(s)
    dh, dg_final, loss_part = _loss_head(h, final_g, target_p, t, d)
    grads = [None] * len(layers)
    for l in reversed(range(len(layers))):
        dh, grads[l] = _layer_bwd(lay, dh, layers[l], saved[l], t)
    return loss_part, dh[ROW0:], dh[PAD:ROW0], grads, dg_final


_SMALL_ROWS = 48


def _pack_small(d, meta, mix, gla, mlp, final, b_alpha, b_forget, w_alpha2):
    rows = [meta.reshape(N_META, d), mix.reshape(DEPTH, d), gla.reshape(DEPTH, d), mlp.reshape(DEPTH, d),
            final.reshape(1, d), b_alpha.reshape(1, d),
            jnp.pad(b_forget.reshape(1, DEPTH * FOX_HEADS), ((0, 0), (0, d - DEPTH * FOX_HEADS))),
            jnp.zeros((7, d), F32), w_alpha2.reshape(GLA_RANK, d)]
    return jnp.concatenate(rows, axis=0)


def _unpack_small(d, packed):
    return dict(meta=packed[:N_META], norm_mix_g=packed[16:18], gla_norm_g=packed[18:20], norm_mlp_g=packed[20:22],
                final_norm_g=packed[22], b_alpha=packed[23].reshape(DEPTH, d // 2),
                b_forget=packed[24, :DEPTH * FOX_HEADS].reshape(DEPTH, FOX_HEADS),
                w_alpha2=packed[32:48].reshape(DEPTH, GLA_RANK, d // 2))


_BIG = ("w_in", "w_o_fox", "w_o_gla", "w_out", "w_ff1", "w_ff2")
_COL_SHARDED = ("w_in", "w_o_fox", "w_ff1")


def _full_matrix(name, gathered_l):
    nch, r, c = gathered_l.shape
    if name in _COL_SHARDED:
        return gathered_l.transpose(1, 0, 2).reshape(r, nch * c)
    return gathered_l.reshape(nch * r, c)


def _shard_major(name, full):
    r, c = full.shape
    if name in _COL_SHARDED:
        return full.reshape(r, N_CHIPS, c // N_CHIPS).transpose(1, 0, 2)
    return full.reshape(N_CHIPS, r // N_CHIPS, c)


def kernel(x, meta_tokens, norm_mix_g, w_in, b_forget, w_alpha2, b_alpha, gla_norm_g, w_o_fox, w_o_gla, w_out, norm_mlp_g, w_ff1, w_ff2, final_norm_g, loss_target, m_meta_tokens, m_norm_mix_g, m_w_in, m_b_forget, m_w_alpha2, m_b_alpha, m_gla_norm_g, m_w_o_fox, m_w_o_gla, m_w_out, m_norm_mlp_g, m_w_ff1, m_w_ff2, m_final_norm_g, v_meta_tokens, v_norm_mix_g, v_w_in, v_b_forget, v_w_alpha2, v_b_alpha, v_gla_norm_g, v_w_o_fox, v_w_o_gla, v_w_out, v_norm_mlp_g, v_w_ff1, v_w_ff2, v_final_norm_g):
    d = x.shape[2]
    lay = _Layout(d)
    xi, yi, ci = lax.axis_index("x"), lax.axis_index("y"), lax.axis_index("c")
    chip = (2 * xi + yi).astype(jnp.int32)
    w = dict(w_in=w_in, w_alpha2=w_alpha2, w_o_fox=w_o_fox, w_o_gla=w_o_gla, w_out=w_out, w_ff1=w_ff1, w_ff2=w_ff2)
    m = dict(w_in=m_w_in, w_alpha2=m_w_alpha2, w_o_fox=m_w_o_fox, w_o_gla=m_w_o_gla, w_out=m_w_out, w_ff1=m_w_ff1,
             w_ff2=m_w_ff2)
    v = dict(w_in=v_w_in, w_alpha2=v_w_alpha2, w_o_fox=v_w_o_fox, w_o_gla=v_w_o_gla, w_out=v_w_out, w_ff1=v_w_ff1,
             w_ff2=v_w_ff2)

    seq = x.shape[1]
    t = seq + ROW0
    core_idx = ci.astype(jnp.int32)[None]
    chip_idx = chip[None]

    cols = d // N_CHIPS
    small_w = jnp.concatenate([meta_tokens, w_alpha2.reshape(-1, cols)], axis=0)
    small_raw = _all_gather_small(small_w)
    small_all = small_raw[0::2]
    alpha_full = small_all[:, N_META:].reshape(N_CHIPS, DEPTH, GLA_RANK, lay.gk // N_CHIPS)
    alpha_full = alpha_full.transpose(1, 2, 0, 3).reshape(DEPTH, GLA_RANK, lay.gk)
    groups = [(0, ("w_in",)), (0, _BIG[1:]), (1, _BIG)]
    started, after = [], small_raw
    for gi, (l, names) in enumerate(groups):
        own16 = [w[n][l].astype(BF16) for n in names]
        lands = [lax.empty((N_CHIPS,) + o.shape, BF16) for o in own16]
        started.append(_split_start("gather_start_%d" % gi, own16, lands, _gather_plan, after=after))
        after = started[gi][4]
    meta_full = small_all[:, :N_META].transpose(1, 0, 2).reshape(N_META, d) + after[0, 0]

    def gathered(gi, after):
        send_sems, recv_sems, srcs, lands, _ = started[gi]
        srcs, lands = _split_wait("gather_wait_%d" % gi, send_sems, recv_sems, srcs, lands, _gather_plan, after)
        lands = _forward_halves("gather_forward_%d" % gi, lands)
        return {n: lax.dynamic_update_slice(g, o[None], (chip, 0, 0)) for n, g, o in zip(groups[gi][1], lands, srcs)}

    def early_weights(l, gl):
        w_alpha_p = jnp.zeros((LANES, lay.gk), BF16).at[FOX_HEADS:FOX_HEADS + GLA_RANK].set(
            alpha_full[l].astype(BF16))
        return dict(
            w_in=lay.to_p(gl["w_in"]), w_alpha_p=w_alpha_p,
            norm_mix_g=norm_mix_g[l][None], norm_mlp_g=norm_mlp_g[l][None], gla_norm_g=gla_norm_g[l][None],
            b_alpha=b_alpha[l][None],
            b_forget_p=jnp.pad(b_forget[l][None], ((0, 0), (0, LANES - FOX_HEADS))))

    def late_weights(gl):
        return dict(w_o_fox=gl["w_o_fox"], w_o_gla=_full_matrix("w_o_gla", gl["w_o_gla"]),
                    w_out=_full_matrix("w_out", gl["w_out"]), w_ff1=gl["w_ff1"],
                    w_ff2=_full_matrix("w_ff2", gl["w_ff2"]))

    h = jnp.pad(x[0], ((ROW0, 0), (0, 0))).at[PAD:ROW0].set(meta_full)
    layers, saved = [], []
    layers.append(early_weights(0, gathered(0, after=h)))
    h, s = _layer_fwd(lay, h, layers[0], t, late=lambda after: late_weights(gathered(1, after)))
    saved.append(s)
    gl = gathered(2, after=h)
    layers.append({**early_weights(1, gl), **late_weights(gl)})
    h, s = _layer_fwd(lay, h, layers[1], t)
    saved.append(s)
    dh, dg_final, loss_part = _loss_head(h, final_norm_g[None], jnp.pad(loss_target[0], ((ROW0, 0), (0, 0))), t, d)
    loss = lax.psum(loss_part[0, 0], ("x", "y", "c"))

    def partial_of(g, n):
        return g[n] if n in ("w_ff1", "w_o_fox", "w_in") else _shard_major(n, g[n])

    scatter_groups = dict(mlp=("w_ff1", "w_ff2"), out=("w_o_fox", "w_o_gla", "w_out"))
    scatter_groups["in"] = ("w_in",)
    swapping, scattered = [], {}

    def start_swap(l, grp, g, after=None):
        parts = [partial_of(g, n) for n in scatter_groups[grp]]
        lands = [lax.empty((p_.shape[0], p_.shape[1] // 2, p_.shape[2]), F32) for p_ in parts]
        started_swap = _split_start("swap_start_%d_%s" % (l, grp), parts, lands, _swap_plan, after=after)
        swapping.append((l, grp, started_swap))
        return started_swap[4]

    def start_scatter(after):
        l, grp, (send_sems, recv_sems, srcs, lands, _) = swapping.pop(0)
        names = scatter_groups[grp]
        tag = "%d_%s" % (l, grp)
        parts, from_sibling = _split_wait("swap_wait_" + tag, send_sems, recv_sems, srcs, lands, _swap_plan, after)
        sums = [_pair_sum("pair_sum_%d_%s" % (l, n), p_, r_, chip_idx, core_idx)
                for n, p_, r_ in zip(names, parts, from_sibling)]
        lands = [lax.empty(s16.shape, BF16) for s16 in sums]
        send_sems, recv_sems, srcs, lands, token = _split_start("scatter_start_" + tag, sums, lands, _scatter_plan)
        scattered[l, grp] = (send_sems, recv_sems, srcs, lands, parts, from_sibling)
        return token

    def gate(l, grp, g):
        token = start_swap(l, grp, g)
        if len(swapping) > 1:
            token = start_scatter(after=token)
        return token[0, 0]

    def gates_for(l, points):
        return {grp: functools.partial(gate, l, grp) for grp in points}

    grads = [None] * DEPTH
    dh, grads[1] = _layer_bwd(lay, dh, layers[1], saved[1], t, gates=gates_for(1, ("mlp", "out", "in")))
    (grad_x, head), grads[0] = _layer_bwd(lay, dh, layers[0], saved[0], t, gates=gates_for(0, ("mlp", "out")),
                                          first=True)
    d_meta = head[PAD:ROW0]

    stack = lambda key: jnp.concatenate([grads[l][key] for l in range(DEPTH)], axis=0)
    b_forget_g = jnp.concatenate([grads[l]["b_forget_p"][:, :FOX_HEADS] for l in range(DEPTH)], axis=0)
    alpha_g = jnp.stack([grads[l]["w_alpha_p"][FOX_HEADS:FOX_HEADS + GLA_RANK] for l in range(DEPTH)])
    packed = _pack_small(d, d_meta, stack("norm_mix_g"), stack("gla_norm_g"), stack("norm_mlp_g"), dg_final,
                         stack("b_alpha"), b_forget_g, alpha_g)
    small_g = _unpack_small(d, _sum_devices(_all_gather_small(packed)))
    small_g["meta"] = lax.dynamic_slice_in_dim(small_g["meta"], chip * (d // N_CHIPS), d // N_CHIPS, axis=1)
    alpha_shard = lax.dynamic_slice_in_dim(small_g["w_alpha2"], chip * (lay.gk // N_CHIPS), lay.gk // N_CHIPS, axis=2)

    after = start_swap(0, "in", grads[0], after=small_g["final_norm_g"])
    while swapping:
        after = start_scatter(after)
    outs = {n: None for n in _BIG}
    for l in reversed(range(DEPTH)):
        names, layer_g = [], []
        for grp in ("mlp", "out", "in"):
            send_sems, recv_sems, srcs, lands, parts, from_sibling = scattered[l, grp]
            _, lands = _split_wait("scatter_wait_%d_%s" % (l, grp), send_sems, recv_sems, srcs, lands, _scatter_plan,
                                 after)
            names += scatter_groups[grp]
            layer_g += [_chip_sum("chip_sum_%d_%s" % (l, n), p_, r_, landed, chip_idx, core_idx)
                        for n, p_, r_, landed in zip(scatter_groups[grp], parts, from_sibling, lands)]
        layer_g = _share_halves("share_halves_%d" % l, layer_g)
        for n, g in zip(names, layer_g):
            outs[n] = _adamw("adamw_%d_%s" % (l, n), w[n], g, m[n], v[n], layer=l, into=outs[n])
        after = outs[names[-1]][0]
    out_g, out_d, out_m, out_v = {}, {}, {}, {}
    for n in _BIG:
        out_g[n], out_d[n], out_m[n], out_v[n] = outs[n]
    out_g["w_alpha2"], out_d["w_alpha2"], out_m["w_alpha2"], out_v["w_alpha2"] = _adamw(
        "adamw_w_alpha2", w["w_alpha2"], alpha_shard, m["w_alpha2"], v["w_alpha2"])
    sm_w = dict(meta_tokens=meta_tokens, norm_mix_g=norm_mix_g, b_forget=b_forget, b_alpha=b_alpha,
                gla_norm_g=gla_norm_g, norm_mlp_g=norm_mlp_g, final_norm_g=final_norm_g)
    sm_m = dict(meta_tokens=m_meta_tokens, norm_mix_g=m_norm_mix_g, b_forget=m_b_forget, b_alpha=m_b_alpha,
                gla_norm_g=m_gla_norm_g, norm_mlp_g=m_norm_mlp_g, final_norm_g=m_final_norm_g)
    sm_v = dict(meta_tokens=v_meta_tokens, norm_mix_g=v_norm_mix_g, b_forget=v_b_forget, b_alpha=v_b_alpha,
                gla_norm_g=v_gla_norm_g, norm_mlp_g=v_norm_mlp_g, final_norm_g=v_final_norm_g)
    sm_g = dict(meta_tokens=small_g["meta"], norm_mix_g=small_g["norm_mix_g"], b_forget=small_g["b_forget"],
                b_alpha=small_g["b_alpha"], gla_norm_g=small_g["gla_norm_g"], norm_mlp_g=small_g["norm_mlp_g"],
                final_norm_g=small_g["final_norm_g"])
    names_small = list(sm_w)
    sizes = [sm_w[n].size for n in names_small]
    width = 512
    total = -(-sum(sizes) // (8 * width)) * (8 * width)

    def pack_flat(dct, fill):
        flat = jnp.concatenate([dct[n].reshape(-1) for n in names_small])
        return jnp.pad(flat, (0, total - flat.shape[0]), constant_values=fill).reshape(1, -1, width)

    res = _adamw("adamw_small", pack_flat(sm_w, 0.0), pack_flat(sm_g, 0.0), pack_flat(sm_m, 0.0), pack_flat(sm_v, 1.0))
    offs = [0]
    for sz in sizes:
        offs.append(offs[-1] + sz)
    for i, n in enumerate(names_small):
        out_g[n] = sm_g[n].reshape(sm_w[n].shape)
        out_d[n], out_m[n], out_v[n] = [r.reshape(-1)[offs[i]:offs[i + 1]].reshape(sm_w[n].shape) for r in res[1:]]

    order = ["meta_tokens", "norm_mix_g", "w_in", "b_forget", "w_alpha2", "b_alpha", "gla_norm_g", "w_o_fox",
             "w_o_gla", "w_out", "norm_mlp_g", "w_ff1", "w_ff2", "final_norm_g"]
    return (loss, grad_x[None], *[out_g[n] for n in order], *[out_d[n] for n in order],
            *[out_m[n] for n in order], *[out_v[n] for n in order])
```

```python
import functools

import numpy as np

import jax
import jax.numpy as jnp
from jax import lax
from jax.experimental import pallas as pl
from jax.experimental.pallas import tpu as pltpu

F32 = jnp.float32
BF16 = jnp.bfloat16

N_META = 16
PAD = 112
ROW0 = PAD + N_META
EPS = 1e-6
MASK_VALUE = -1e30
FOX_HEADS = 8
FOX_GROUP = 2
GLA_HEADS = 4
GLA_RANK = 16
GLA_TAU = 16.0
GLA_CHUNK = 64
DEPTH = 2
N_CHIPS = 4
N_DEV = 8

ADAM_LR = 0.001
ADAM_B1 = 0.9
ADAM_B2 = 0.999
ADAM_EPS = 1e-08
ADAM_WD = 0.01
ADAM_STEP = 10

LANES = 128
VMEM_LIMIT = 56 * 1024 * 1024
MESH = pl.DeviceIdType.MESH


def _pick(n, target, mult):
    best = None
    for d in range(mult, min(n, target) + 1, mult):
        if n % d == 0:
            best = d
    return n if best is None else best


def _params(sem=None):
    return pltpu.CompilerParams(dimension_semantics=sem, vmem_limit_bytes=VMEM_LIMIT)


def _bf(v):
    return v if v.dtype == BF16 else v.astype(BF16)


def _sigmoid(z):
    return 1.0 / (1.0 + jnp.exp(-z))


def _log_sigmoid(z):
    return jnp.minimum(z, 0.0) - jnp.log(1.0 + jnp.exp(-jnp.abs(z)))


def _split3(v):
    a = v.astype(BF16)
    r = v - a.astype(F32)
    b = r.astype(BF16)
    c = (r - b.astype(F32)).astype(BF16)
    return a, b, c


def _dot(a, b, dims):
    return lax.dot_general(a, b, (dims, ((), ())), preferred_element_type=F32)


NN = ((1,), (0,))
NT = ((1,), (1,))
TN = ((0,), (0,))


def _tri_dot(tri, v, dims=NN):
    a, b, c = _split3(v)
    return _dot(tri, a, dims) + _dot(tri, b, dims) + _dot(tri, c, dims)


def _mm(name, a, b, *, mode, m, n, k, b_c0=0, extras=(), epilogue=None, out_dtypes=(F32,),
        b_shards=1, out_shards=1, tm=1056, tn=1024, tk=2048):
    tm = _pick(m, tm, LANES if mode == "tn" else 16)
    tn = _pick(n // max(b_shards if mode == "nn" else 1, out_shards), tn, LANES)
    if mode == "tn":
        tk = _pick(k, 2112, 16)
    else:
        tk = _pick(k // (b_shards if mode == "nt" else 1), tk, LANES)
    assert b_c0 % (tk if mode == "nt" else tn) == 0 and (b_shards == 1 or b_c0 == 0)
    nk = k // tk
    if mode == "tn":
        a_spec = pl.BlockSpec((tk, tm), lambda i, j, kk: (kk, i))
    else:
        a_spec = pl.BlockSpec((tm, tk), lambda i, j, kk: (i, kk))
    if mode == "nt":
        dims = NT
        if b_shards > 1:
            per = (k // b_shards) // tk
            b_spec = pl.BlockSpec((None, tn, tk), lambda i, j, kk: (kk // per, j, kk % per))
        else:
            b_spec = pl.BlockSpec((tn, tk), lambda i, j, kk: (j, kk + b_c0 // tk))
    else:
        dims = NN if mode == "nn" else TN
        if b_shards > 1:
            per = (n // b_shards) // tn
            b_spec = pl.BlockSpec((None, tk, tn), lambda i, j, kk: (j // per, kk, j % per))
        else:
            b_spec = pl.BlockSpec((tk, tn), lambda i, j, kk: (kk, j + b_c0 // tn))
    ex_specs = [pl.BlockSpec((tm, tn), lambda i, j, kk: (i, j)) for _ in extras]
    if out_shards > 1:
        oper = (n // out_shards) // tn
        out_specs = [pl.BlockSpec((None, tm, tn), lambda i, j, kk: (j // oper, i, j % oper)) for _ in out_dtypes]
        out_shape = [jax.ShapeDtypeStruct((out_shards, m, n // out_shards), dt) for dt in out_dtypes]
    else:
        out_specs = [pl.BlockSpec((tm, tn), lambda i, j, kk: (i, j)) for _ in out_dtypes]
        out_shape = [jax.ShapeDtypeStruct((m, n), dt) for dt in out_dtypes]
    n_ex = len(extras)
    n_out = len(out_dtypes)

    def finish(acc, ex_refs, out_refs):
        vals = (acc,) if epilogue is None else epilogue(acc, *[r[...] for r in ex_refs])
        for r, v in zip(out_refs, vals):
            r[...] = v.astype(r.dtype)

    def body(a_ref, b_ref, *rest):
        ex_refs = rest[:n_ex]
        out_refs = rest[n_ex:n_ex + n_out]
        prod = _dot(_bf(a_ref[...]), _bf(b_ref[...]), dims)
        if nk == 1:
            finish(prod, ex_refs, out_refs)
            return
        acc_ref = rest[n_ex + n_out]
        kk = pl.program_id(2)

        @pl.when(kk == 0)
        def _():
            acc_ref[...] = prod

        @pl.when((kk > 0) & (kk < nk - 1))
        def _():
            acc_ref[...] += prod

        @pl.when(kk == nk - 1)
        def _():
            finish(acc_ref[...] + prod, ex_refs, out_refs)

    outs = pl.pallas_call(
        body,
        name=name,
        grid=(m // tm, n // tn, nk),
        in_specs=[a_spec, b_spec] + ex_specs,
        out_specs=out_specs,
        out_shape=out_shape,
        scratch_shapes=[pltpu.VMEM((tm, tn), F32)] if nk > 1 else [],
        compiler_params=_params(("parallel", "parallel", "arbitrary")),
    )(a, b, *extras)
    return outs[0] if n_out == 1 else outs


def _mm_nt2(name, a1, c1, a2, c2, b, prev, *, m, n, tm=528, tn=1024):
    k1, k2 = a1.shape[1], a2.shape[1]
    assert c1 % k1 == 0 and c2 % k2 == 0
    tm = _pick(m, tm, 16)
    tn = _pick(n, tn, LANES)

    def body(a1_ref, a2_ref, b1_ref, b2_ref, *rest):
        acc = _dot(a1_ref[...], b1_ref[...], NT) + _dot(a2_ref[...], b2_ref[...], NT)
        if prev is not None:
            acc = rest[0][...] + acc
        rest[-1][...] = acc

    tile = pl.BlockSpec((tm, tn), lambda i, j: (i, j))
    in_specs = [pl.BlockSpec((tm, k1), lambda i, j: (i, 0)), pl.BlockSpec((tm, k2), lambda i, j: (i, 0)),
                pl.BlockSpec((tn, k1), lambda i, j: (j, c1 // k1)), pl.BlockSpec((tn, k2), lambda i, j: (j, c2 // k2))]
    args = [a1, a2, b, b]
    if prev is not None:
        in_specs.append(tile)
        args.append(prev)
    return pl.pallas_call(
        body, name=name, grid=(m // tm, n // tn), in_specs=in_specs, out_specs=tile,
        out_shape=jax.ShapeDtypeStruct((m, n), F32),
        compiler_params=_params(("parallel", "parallel")),
    )(*args)


def _ew(name, fn, ins, outs, rows, tm):
    tm = _pick(rows, tm, 16)
    in_specs, args = [], []
    for spec in ins:
        if spec[0] == "tile":
            _, arr, width, c0 = spec
            assert c0 % width == 0
            in_specs.append(pl.BlockSpec((tm, width), functools.partial(lambda i, o: (i, o), o=c0 // width)))
        else:
            arr = spec[1]
            in_specs.append(pl.BlockSpec(arr.shape, lambda i: (0, 0)))
        args.append(arr)
    out_specs, out_shape = [], []
    for kind, dt, width in outs:
        if kind == "tile":
            out_specs.append(pl.BlockSpec((tm, width), lambda i: (i, 0)))
            out_shape.append(jax.ShapeDtypeStruct((rows, width), dt))
        else:
            out_specs.append(pl.BlockSpec((1, width), lambda i: (0, 0)))
            out_shape.append(jax.ShapeDtypeStruct((1, width), dt))
    n_in = len(ins)
    has_acc = any(o[0] == "acc" for o in outs)

    def body(*refs):
        i = pl.program_id(0)
        vals = fn(i * tm, *[r[...] for r in refs[:n_in]])
        for (kind, _, _), r, v in zip(outs, refs[n_in:], vals):
            if kind == "tile":
                r[...] = v.astype(r.dtype)
            else:
                @pl.when(i == 0)
                def _():
                    r[...] = jnp.zeros_like(r)

                r[...] += v.astype(r.dtype)

    res = pl.pallas_call(
        body,
        name=name,
        grid=(rows // tm,),
        in_specs=in_specs,
        out_specs=out_specs,
        out_shape=out_shape,
        compiler_params=_params(("arbitrary",) if has_acc else ("parallel",)),
    )(*args)
    return res[0] if len(outs) == 1 else res


def _row_ids(row0, tm):
    return row0 + lax.broadcasted_iota(jnp.int32, (tm, 1), 0)


def _colsum(v):
    return jnp.sum(v, axis=0, keepdims=True)


def _rms_fwd_t(name, h, g, t, d):
    tm = _pick(t, 384, LANES)

    def body(x_ref, g_ref, y_ref, yt_ref):
        x = x_ref[...]
        y = x * lax.rsqrt(jnp.mean(x * x, axis=-1, keepdims=True) + EPS) * g_ref[...]
        y_ref[...] = y.astype(BF16)
        yt_ref[...] = y.T.astype(BF16)

    return pl.pallas_call(
        body, name=name, grid=(t // tm,),
        in_specs=[pl.BlockSpec((tm, d), lambda i: (i, 0)), pl.BlockSpec((1, d), lambda i: (0, 0))],
        out_specs=[pl.BlockSpec((tm, d), lambda i: (i, 0)), pl.BlockSpec((d, tm), lambda i: (0, i))],
        out_shape=[jax.ShapeDtypeStruct((t, d), BF16), jax.ShapeDtypeStruct((d, t), BF16)],
        compiler_params=_params(("parallel",)),
    )(h, g)


def _rms_bwd(name, h, g, dy, dres, t, d):
    def fn(row0, x, gg, dyv, dr):
        r = lax.rsqrt(jnp.mean(x * x, axis=-1, keepdims=True) + EPS)
        xh = x * r
        dxh = dyv * gg
        dx = r * (dxh - xh * jnp.mean(dxh * xh, axis=-1, keepdims=True))
        out = jnp.where(_row_ids(row0, x.shape[0]) >= PAD, dr + dx, 0.0)
        return out, _colsum(dyv * xh)

    return _ew(name, fn, [("tile", h, d, 0), ("full", g), ("tile", dy, d, 0), ("tile", dres, d, 0)],
               [("tile", F32, d), ("acc", F32, d)], t, 264)


def _rms_bwd_x(name, h, g, dy, dres, t, d):
    tm = ROW0

    def body(x_ref, g_ref, dy_ref, dr_ref, dx_ref, head_ref, dg_ref):
        i = pl.program_id(0)
        x = x_ref[...]
        r = lax.rsqrt(jnp.mean(x * x, axis=-1, keepdims=True) + EPS)
        xh = x * r
        dxh = dy_ref[...] * g_ref[...]
        dx = r * (dxh - xh * jnp.mean(dxh * xh, axis=-1, keepdims=True))
        out = jnp.where(_row_ids(i * tm, tm) >= PAD, dr_ref[...] + dx, 0.0)

        @pl.when(i == 0)
        def _():
            head_ref[...] = out
            dg_ref[...] = jnp.zeros_like(dg_ref)

        dx_ref[...] = out
        dg_ref[...] += _colsum(dy_ref[...] * xh)

    tile = pl.BlockSpec((tm, d), lambda i: (i, 0))
    fixed = lambda shape: pl.BlockSpec(shape, lambda i: (0, 0))
    return pl.pallas_call(
        body, name=name, grid=(t // tm,),
        in_specs=[tile, fixed((1, d)), tile, tile],
        out_specs=[pl.BlockSpec((tm, d), lambda i: (jnp.maximum(i - 1, 0), 0)), fixed((tm, d)), fixed((1, d))],
        out_shape=[jax.ShapeDtypeStruct((t - ROW0, d), F32), jax.ShapeDtypeStruct((ROW0, d), F32),
                   jax.ShapeDtypeStruct((1, d), F32)],
        compiler_params=_params(("arbitrary",)),
    )(h, g, dy, dres)


def _loss_head(h, g, target_p, t, d):
    def fn(row0, x, gg, tgt):
        real = _row_ids(row0, x.shape[0]) >= ROW0
        r = lax.rsqrt(jnp.mean(x * x, axis=-1, keepdims=True) + EPS)
        xh = x * r
        err = jnp.where(real, xh * gg - tgt, 0.0)
        loss_rows = 0.5 * jnp.mean(err * err, axis=-1, keepdims=True)
        dyv = err * (1.0 / d)
        dxh = dyv * gg
        dx = r * (dxh - xh * jnp.mean(dxh * xh, axis=-1, keepdims=True))
        loss_part = jnp.sum(loss_rows, axis=0, keepdims=True) * jnp.ones((1, LANES), F32)
        return jnp.where(real, dx, 0.0), _colsum(dyv * xh), loss_part

    return _ew("loss_head", fn, [("tile", h, d, 0), ("full", g), ("tile", target_p, d, 0)],
               [("tile", F32, d), ("acc", F32, d), ("acc", F32, LANES)], t, 264)


def _merge_fwd(a_fox, a_gla, proj, c_gates, t, d):
    def fn(row0, af, ag, gates):
        gates = gates.astype(F32)
        return (_sigmoid(gates[:, :d]) * af + _sigmoid(gates[:, d:]) * ag,)

    return _ew("merge_fwd", fn, [("tile", a_fox, d, 0), ("tile", a_gla, d, 0), ("tile", proj, 2 * d, c_gates)],
               [("tile", BF16, d)], t, 264)


def _merge_bwd(dy, a_fox, a_gla, proj, c_gates, t, d):
    def fn(row0, dyv, af, ag, gates):
        gates = gates.astype(F32)
        sf = _sigmoid(gates[:, :d])
        sg = _sigmoid(gates[:, d:])
        dgates = jnp.concatenate([dyv * af * sf * (1.0 - sf), dyv * ag * sg * (1.0 - sg)], axis=1)
        return dyv * sf, dyv * sg, dgates

    return _ew("merge_bwd", fn,
               [("tile", dy, d, 0), ("tile", a_fox, d, 0), ("tile", a_gla, d, 0), ("tile", proj, 2 * d, c_gates)],
               [("tile", BF16, d), ("tile", BF16, d), ("tile", BF16, 2 * d)], t, 264)


def _fox_gate_fwd(small, b_forget_p, t):
    tb = _pick(t, 384, LANES)

    def body(s_ref, b_ref, c_ref, carry_ref):
        i = pl.program_id(0)

        @pl.when(i == 0)
        def _():
            carry_ref[...] = jnp.zeros_like(carry_ref)

        logf = _log_sigmoid(s_ref[...] + b_ref[...])
        logf = jnp.where(_row_ids(i * tb, tb) >= PAD, logf, 0.0)
        r = lax.broadcasted_iota(jnp.int32, (tb, tb), 0)
        c = lax.broadcasted_iota(jnp.int32, (tb, tb), 1)
        tri = (c <= r).astype(BF16)
        cs = _tri_dot(tri, logf) + carry_ref[...]
        c_ref[...] = cs
        carry_ref[...] = cs[tb - 1:tb, :]

    return pl.pallas_call(
        body, name="fox_gate_fwd", grid=(t // tb,),
        in_specs=[pl.BlockSpec((tb, LANES), lambda i: (i, 0)), pl.BlockSpec((1, LANES), lambda i: (0, 0))],
        out_specs=pl.BlockSpec((tb, LANES), lambda i: (i, 0)),
        out_shape=jax.ShapeDtypeStruct((t, LANES), F32),
        scratch_shapes=[pltpu.VMEM((1, LANES), F32)],
        compiler_params=_params(("arbitrary",)),
    )(small, b_forget_p)


def _fox_gate_bwd(dc, small, b_forget_p, dga, t):
    tb = _pick(t, 384, LANES)
    nb = t // tb

    def body(dc_ref, s_ref, b_ref, dga_ref, ds_ref, db_ref, carry_ref):
        i = pl.program_id(0)

        @pl.when(i == 0)
        def _():
            carry_ref[...] = jnp.zeros_like(carry_ref)
            db_ref[...] = jnp.zeros_like(db_ref)

        r = lax.broadcasted_iota(jnp.int32, (tb, tb), 0)
        c = lax.broadcasted_iota(jnp.int32, (tb, tb), 1)
        tri = (c >= r).astype(BF16)
        dlogf = _tri_dot(tri, dc_ref[...]) + carry_ref[...]
        carry_ref[...] = dlogf[0:1, :]
        z = s_ref[...] + b_ref[...]
        dff = dlogf * _sigmoid(-z)
        lane = lax.broadcasted_iota(jnp.int32, (tb, LANES), 1)
        keep = (_row_ids((nb - 1 - i) * tb, tb) >= PAD) & (lane < FOX_HEADS)
        dff = jnp.where(keep, dff, 0.0)
        ds_ref[...] = dff + dga_ref[...]
        db_ref[...] += _colsum(dff)

    rev = lambda i: (nb - 1 - i, 0)
    return pl.pallas_call(
        body, name="fox_gate_bwd", grid=(nb,),
        in_specs=[pl.BlockSpec((tb, LANES), rev), pl.BlockSpec((tb, LANES), rev),
                  pl.BlockSpec((1, LANES), lambda i: (0, 0)), pl.BlockSpec((tb, LANES), rev)],
        out_specs=[pl.BlockSpec((tb, LANES), rev), pl.BlockSpec((1, LANES), lambda i: (0, 0))],
        out_shape=[jax.ShapeDtypeStruct((t, LANES), F32), jax.ShapeDtypeStruct((1, LANES), F32)],
        scratch_shapes=[pltpu.VMEM((1, LANES), F32)],
        compiler_params=_params(("arbitrary",)),
    )(dc, small, b_forget_p, dga)


def _fox_pairs(nb, by_key):
    if by_key:
        pairs = [(qi, ki) for ki in range(nb) for qi in range(ki, nb)]
    else:
        pairs = [(qi, ki) for qi in range(nb) for ki in range(qi + 1)]
    return (jnp.asarray(np.array([p[0] for p in pairs], np.int32)),
            jnp.asarray(np.array([p[1] for p in pairs], np.int32)), len(pairs))


def _fox_specs(tb, fd, c_fq, c_fkv):
    gw = FOX_GROUP * fd
    q0, kv0 = c_fq // gw, c_fkv // (2 * gw)
    return dict(
        q=pl.BlockSpec((tb, gw), lambda g, p, qt, kt: (qt[p], q0 + g)),
        kv=pl.BlockSpec((tb, 2 * gw), lambda g, p, qt, kt: (kt[p], kv0 + g)),
        col=pl.BlockSpec((FOX_GROUP, tb, 1), lambda g, p, qt, kt: (g, qt[p], 0)),
        row=pl.BlockSpec((FOX_GROUP, 1, tb), lambda g, p, qt, kt: (g, 0, kt[p])),
        head=pl.BlockSpec((tb, gw), lambda g, p, qt, kt: (qt[p], g)),
        key_kv=pl.BlockSpec((tb, 2 * gw), lambda g, p, qt, kt: (kt[p], g)),
    )


def _fox_mask(qi, ki, tb):
    row = qi * tb + lax.broadcasted_iota(jnp.int32, (tb, tb), 0)
    col = ki * tb + lax.broadcasted_iota(jnp.int32, (tb, tb), 1)
    return (col <= row) & (col >= PAD)


def _fox_heads(q_ref, kv_ref, fd):
    return [(q_ref[:, hh * fd:(hh + 1) * fd], kv_ref[:, 2 * hh * fd:(2 * hh + 1) * fd],
             kv_ref[:, (2 * hh + 1) * fd:(2 * hh + 2) * fd]) for hh in range(FOX_GROUP)]


def _fox_fwd(proj, c_col, c_row, t, fd, c_fq, c_fkv):
    tb = _pick(t, 384, LANES)
    nb = t // tb
    scale = fd ** -0.5
    sp = _fox_specs(tb, fd, c_fq, c_fkv)
    qt, kt, npairs = _fox_pairs(nb, by_key=False)

    def body(qt_ref, kt_ref, q_ref, kv_ref, cq_ref, ck_ref, o_ref, lse_ref, m_ref, l_ref, acc_ref):
        p = pl.program_id(1)
        qi, ki = qt_ref[p], kt_ref[p]

        @pl.when(ki == 0)
        def _():
            m_ref[...] = jnp.full_like(m_ref, -jnp.inf)
            l_ref[...] = jnp.zeros_like(l_ref)
            acc_ref[...] = jnp.zeros_like(acc_ref)

        def update(masked):
            mask = _fox_mask(qi, ki, tb) if masked else None
            for hh, (q, k, v) in enumerate(_fox_heads(q_ref, kv_ref, fd)):
                s = _dot(q, k, NT) * scale + cq_ref[hh] - ck_ref[hh]
                if masked:
                    s = jnp.where(mask, s, MASK_VALUE)
                m_prev = m_ref[hh]
                m_new = jnp.maximum(m_prev, jnp.max(s, axis=-1, keepdims=True))
                alpha = jnp.exp(m_prev - m_new)
                pe = jnp.exp(s - m_new)
                l_ref[hh] = alpha * l_ref[hh] + jnp.sum(pe, axis=-1, keepdims=True)
                acc_ref[hh] = alpha * acc_ref[hh] + _dot(pe.astype(BF16), v, NN)
                m_ref[hh] = m_new

        edge = (ki == 0) | (ki == qi)
        pl.when(edge)(functools.partial(update, True))
        pl.when(jnp.logical_not(edge))(functools.partial(update, False))

        @pl.when(ki == qi)
        def _():
            real = _row_ids(qi * tb, tb) >= PAD
            for hh in range(FOX_GROUP):
                o_ref[:, hh * fd:(hh + 1) * fd] = jnp.where(real, acc_ref[hh] / l_ref[hh], 0.0)
                lse_ref[hh] = m_ref[hh] + jnp.log(l_ref[hh])

    return pl.pallas_call(
        body, name="fox_fwd",
        grid_spec=pltpu.PrefetchScalarGridSpec(
            num_scalar_prefetch=2, grid=(FOX_HEADS // FOX_GROUP, npairs),
            in_specs=[sp["q"], sp["kv"], sp["col"], sp["row"]],
            out_specs=[sp["head"], sp["col"]],
            scratch_shapes=[pltpu.VMEM((FOX_GROUP, tb, 1), F32), pltpu.VMEM((FOX_GROUP, tb, 1), F32),
                            pltpu.VMEM((FOX_GROUP, tb, fd), F32)]),
        out_shape=[jax.ShapeDtypeStruct((t, FOX_HEADS * fd), F32), jax.ShapeDtypeStruct((FOX_HEADS, t, 1), F32)],
        compiler_params=_params(("parallel", "arbitrary")),
    )(qt, kt, proj, proj, c_col, c_row)


def _fox_delta(o_fox, do_fox, t, fd):
    tb = _pick(t, 384, LANES)

    def body(o_ref, do_ref, out_ref):
        for h in range(FOX_HEADS):
            sl = slice(h * fd, (h + 1) * fd)
            out_ref[h] = jnp.sum(o_ref[:, sl] * do_ref[:, sl].astype(BF16).astype(F32), axis=-1, keepdims=True)

    w = FOX_HEADS * fd
    return pl.pallas_call(
        body, name="fox_delta", grid=(t // tb,),
        in_specs=[pl.BlockSpec((tb, w), lambda i: (i, 0)), pl.BlockSpec((tb, w), lambda i: (i, 0))],
        out_specs=pl.BlockSpec((FOX_HEADS, tb, 1), lambda i: (0, i, 0)),
        out_shape=jax.ShapeDtypeStruct((FOX_HEADS, t, 1), F32),
        compiler_params=_params(("parallel",)),
    )(o_fox, do_fox)


def _fox_bwd(proj, c_col, c_row, lse, delta, do_fox, t, fd, c_fq, c_fkv):
    tb = _pick(t, 384, LANES)
    nb = t // tb
    scale = fd ** -0.5
    sp = _fox_specs(tb, fd, c_fq, c_fkv)
    qt, kt, npairs = _fox_pairs(nb, by_key=True)
    gw = FOX_GROUP * fd

    def body(qt_ref, kt_ref, q_ref, kv_ref, cq_ref, ck_ref, lse_ref, dl_ref, do_ref, dq_ref, dkv_ref, dc_ref, dr_ref,
             dq_acc, dk_acc, dv_acc, dc_acc, dr_acc):
        p = pl.program_id(1)
        qi, ki = qt_ref[p], kt_ref[p]

        @pl.when(p == 0)
        def _():
            dq_acc[...] = jnp.zeros_like(dq_acc)
            dr_acc[...] = jnp.zeros_like(dr_acc)

        @pl.when(qi == ki)
        def _():
            dk_acc[...] = jnp.zeros_like(dk_acc)
            dv_acc[...] = jnp.zeros_like(dv_acc)
            dc_acc[...] = jnp.zeros_like(dc_acc)

        rows = pl.ds(pl.multiple_of(qi * tb, LANES), tb)

        def update(masked):
            mask = _fox_mask(qi, ki, tb) if masked else None
            for hh, (q, k, v) in enumerate(_fox_heads(q_ref, kv_ref, fd)):
                do = _bf(do_ref[:, hh * fd:(hh + 1) * fd])
                s = _dot(q, k, NT) * scale + cq_ref[hh] - ck_ref[hh]
                if masked:
                    s = jnp.where(mask, s, MASK_VALUE)
                pr = jnp.exp(s - lse_ref[hh])
                dp = _dot(do, v, NT)
                ds = pr * (dp - dl_ref[hh])
                ds16 = ds.astype(BF16)
                dv_acc[hh] += _dot(pr.astype(BF16), do, TN)
                dk_acc[hh] += _dot(ds16, q, TN)
                dc_acc[hh] += _colsum(ds)
                dr_acc[hh, rows, :] += jnp.sum(ds, axis=-1, keepdims=True)
                dq_acc[hh, rows, :] += _dot(ds16, k, NN)

        edge = (ki == 0) | (ki == qi)
        pl.when(edge)(functools.partial(update, True))
        pl.when(jnp.logical_not(edge))(functools.partial(update, False))

        @pl.when(qi == nb - 1)
        def _():
            for hh in range(FOX_GROUP):
                dkv_ref[:, 2 * hh * fd:(2 * hh + 1) * fd] = (dk_acc[hh] * scale).astype(dkv_ref.dtype)
                dkv_ref[:, (2 * hh + 1) * fd:(2 * hh + 2) * fd] = dv_acc[hh].astype(dkv_ref.dtype)
                dc_ref[hh] = -dc_acc[hh]

        @pl.when(p == npairs - 1)
        def _():
            for hh in range(FOX_GROUP):
                dq_ref[:, hh * fd:(hh + 1) * fd] = (dq_acc[hh] * scale).astype(dq_ref.dtype)
            dr_ref[...] = dr_acc[...]

    return pl.pallas_call(
        body, name="fox_bwd",
        grid_spec=pltpu.PrefetchScalarGridSpec(
            num_scalar_prefetch=2, grid=(FOX_HEADS // FOX_GROUP, npairs),
            in_specs=[sp["q"], sp["kv"], sp["col"], sp["row"], sp["col"], sp["col"], sp["head"]],
            out_specs=[pl.BlockSpec((t, gw), lambda g, p, qt, kt: (0, g)), sp["key_kv"], sp["row"],
                       pl.BlockSpec((FOX_GROUP, t, 1), lambda g, p, qt, kt: (g, 0, 0))],
            scratch_shapes=[pltpu.VMEM((FOX_GROUP, t, fd), F32), pltpu.VMEM((FOX_GROUP, tb, fd), F32),
                            pltpu.VMEM((FOX_GROUP, tb, fd), F32), pltpu.VMEM((FOX_GROUP, 1, tb), F32),
                            pltpu.VMEM((FOX_GROUP, t, 1), F32)]),
        out_shape=[jax.ShapeDtypeStruct((t, FOX_HEADS * fd), BF16), jax.ShapeDtypeStruct((t, 2 * FOX_HEADS * fd), BF16),
                   jax.ShapeDtypeStruct((FOX_HEADS, 1, t), F32), jax.ShapeDtypeStruct((FOX_HEADS, t, 1), F32)],
        compiler_params=_params(("parallel", "arbitrary")),
    )(qt, kt, proj, proj, c_col, c_row, lse, delta, do_fox)


def _gla_gate_fwd(small, w_alpha_p, b_alpha, t, gk):
    def fn(row0, s, w, b):
        z = _dot(s.astype(BF16), w, NN) + b
        return (jnp.where(_row_ids(row0, s.shape[0]) >= PAD, _log_sigmoid(z) * (1.0 / GLA_TAU), 0.0),)

    return _ew("gla_gate_fwd", fn, [("tile", small, LANES, 0), ("full", w_alpha_p), ("full", b_alpha)],
               [("tile", F32, gk)], t, 264)


def _gla_gate_bwd(dglog, small, w_alpha_p, b_alpha, t, gk):
    def fn(row0, dg, s, w, b):
        z = _dot(s.astype(BF16), w, NN) + b
        dz = jnp.where(_row_ids(row0, s.shape[0]) >= PAD, dg * (1.0 / GLA_TAU) * _sigmoid(-z), 0.0)
        return dz, _colsum(dz)

    return _ew("gla_gate_bwd", fn,
               [("tile", dglog, gk, 0), ("tile", small, LANES, 0), ("full", w_alpha_p), ("full", b_alpha)],
               [("tile", BF16, gk), ("acc", F32, gk)], t, 264)


def _gla_chunk(q, k, g, scale, cs):
    r = lax.broadcasted_iota(jnp.int32, (cs, cs), 0)
    c = lax.broadcasted_iota(jnp.int32, (cs, cs), 1)
    causal = c <= r
    b = _tri_dot(causal.astype(BF16), g)
    bl = b[cs - 1:cs, :]
    eb, einv, eend = jnp.exp(b), jnp.exp(-b), jnp.exp(bl - b)
    qd = q.astype(F32) * scale * eb
    kf = k.astype(F32)
    return causal, (eb, einv, eend), bl, qd, kf * einv, kf * eend


def _gla_fwd(proj, glog, t, dk, dv, c_q, c_k, c_v):
    cs = GLA_CHUNK
    nc = t // cs
    wk, wv = GLA_HEADS * dk, GLA_HEADS * dv
    scale = dk ** -0.5

    def body(q_ref, k_ref, v_ref, g_ref, o_ref, sp_ref, st_ref):
        @pl.when(pl.program_id(0) == 0)
        def _():
            st_ref[...] = jnp.zeros_like(st_ref)

        for h in range(GLA_HEADS):
            ks, vs = slice(h * dk, (h + 1) * dk), slice(h * dv, (h + 1) * dv)
            v = v_ref[:, vs]
            causal, _, bl, qd, ki, ke = _gla_chunk(q_ref[:, ks], k_ref[:, ks], g_ref[:, ks], scale, cs)
            st = st_ref[h]
            sp_ref[h] = st
            a = jnp.where(causal, _dot(qd.astype(BF16), ki.astype(BF16), NT), 0.0)
            o_ref[:, vs] = _dot(a.astype(BF16), v, NN) + _dot(qd.astype(BF16), st.astype(BF16), NT)
            st_ref[h] = st * jnp.exp(bl) + _dot(v, ke.astype(BF16), TN)

    return pl.pallas_call(
        body, name="gla_fwd", grid=(nc,),
        in_specs=[pl.BlockSpec((cs, wk), lambda n: (n, c_q // wk)), pl.BlockSpec((cs, wk), lambda n: (n, c_k // wk)),
                  pl.BlockSpec((cs, wv), lambda n: (n, c_v // wv)), pl.BlockSpec((cs, wk), lambda n: (n, 0))],
        out_specs=[pl.BlockSpec((cs, wv), lambda n: (n, 0)),
                   pl.BlockSpec((None, GLA_HEADS, dv, dk), lambda n: (n, 0, 0, 0))],
        out_shape=[jax.ShapeDtypeStruct((t, wv), F32), jax.ShapeDtypeStruct((nc, GLA_HEADS, dv, dk), F32)],
        scratch_shapes=[pltpu.VMEM((GLA_HEADS, dv, dk), F32)],
        compiler_params=_params(("arbitrary",)),
    )(proj, proj, proj, glog)


def _gla_bwd(proj, glog, s_prev, do_raw, t, dk, dv, c_q, c_k, c_v):
    cs = GLA_CHUNK
    nc = t // cs
    wk, wv = GLA_HEADS * dk, GLA_HEADS * dv
    scale = dk ** -0.5

    def body(q_ref, k_ref, v_ref, g_ref, sp_ref, do_ref, dq_ref, dk_ref, dv_ref, dg_ref, dst_ref):
        @pl.when(pl.program_id(0) == 0)
        def _():
            dst_ref[...] = jnp.zeros_like(dst_ref)

        for h in range(GLA_HEADS):
            ks, vs = slice(h * dk, (h + 1) * dk), slice(h * dv, (h + 1) * dv)
            v = v_ref[:, vs]
            do = do_ref[:, vs].astype(BF16)
            causal, (eb, einv, eend), bl, qd, ki, ke = _gla_chunk(q_ref[:, ks], k_ref[:, ks], g_ref[:, ks], scale, cs)
            qd16, ki16, ke16 = qd.astype(BF16), ki.astype(BF16), ke.astype(BF16)
            st = sp_ref[h]
            dst = dst_ref[h]
            dst16 = dst.astype(BF16)
            a = jnp.where(causal, _dot(qd16, ki16, NT), 0.0).astype(BF16)
            da = jnp.where(causal, _dot(do, v, NT), 0.0).astype(BF16)
            dvv = _dot(a, do, TN) + _dot(ke16, dst16, NT)
            dqd = _dot(da, ki16, NN) + _dot(do, st.astype(BF16), NN)
            dki = _dot(da, qd16, TN)
            dke = _dot(v, dst16, NN)
            dl = jnp.exp(bl)
            ddl = _colsum(dst * st)
            dst_ref[h] = dst * dl + _dot(do, qd16, TN)
            dq_ref[:, ks] = (dqd * eb * scale).astype(dq_ref.dtype)
            dk_ref[:, ks] = (dki * einv + dke * eend).astype(dk_ref.dtype)
            dv_ref[:, vs] = dvv.astype(dv_ref.dtype)
            db = dqd * qd - dki * ki - dke * ke
            db_last = _colsum(dke * ke) + ddl * dl
            r = lax.broadcasted_iota(jnp.int32, (cs, cs), 0)
            c = lax.broadcasted_iota(jnp.int32, (cs, cs), 1)
            dg_ref[:, ks] = _tri_dot((c >= r).astype(BF16), db) + db_last

    rev = lambda f: (lambda n: f(nc - 1 - n))
    return pl.pallas_call(
        body, name="gla_bwd", grid=(nc,),
        in_specs=[pl.BlockSpec((cs, wk), rev(lambda n: (n, c_q // wk))), pl.BlockSpec((cs, wk), rev(lambda n: (n, c_k // wk))),
                  pl.BlockSpec((cs, wv), rev(lambda n: (n, c_v // wv))), pl.BlockSpec((cs, wk), rev(lambda n: (n, 0))),
                  pl.BlockSpec((None, GLA_HEADS, dv, dk), rev(lambda n: (n, 0, 0, 0))),
                  pl.BlockSpec((cs, wv), rev(lambda n: (n, 0)))],
        out_specs=[pl.BlockSpec((cs, wk), rev(lambda n: (n, 0))), pl.BlockSpec((cs, wk), rev(lambda n: (n, 0))),
                   pl.BlockSpec((cs, wv), rev(lambda n: (n, 0))), pl.BlockSpec((cs, wk), rev(lambda n: (n, 0)))],
        out_shape=[jax.ShapeDtypeStruct((t, wk), BF16), jax.ShapeDtypeStruct((t, wk), BF16),
                   jax.ShapeDtypeStruct((t, wv), BF16), jax.ShapeDtypeStruct((t, wk), F32)],
        scratch_shapes=[pltpu.VMEM((GLA_HEADS, dv, dk), F32)],
        compiler_params=_params(("arbitrary",)),
    )(proj, proj, proj, glog, s_prev, do_raw)


def _gla_post_fwd(o_raw, proj, gn, t, dv, c_gr):
    w = GLA_HEADS * dv

    def fn(row0, o, gr, g):
        gr = gr.astype(F32)
        outs = []
        for h in range(GLA_HEADS):
            oh = o[:, h * dv:(h + 1) * dv]
            outs.append(oh * lax.rsqrt(jnp.mean(oh * oh, axis=-1, keepdims=True) + EPS))
        on = jnp.concatenate(outs, axis=1) * g
        return (on * (gr * _sigmoid(gr)),)

    return _ew("gla_post_fwd", fn, [("tile", o_raw, w, 0), ("tile", proj, w, c_gr), ("full", gn)],
               [("tile", BF16, w)], t, 264)


def _gla_post_bwd(o_raw, proj, gn, do_gla, t, dv, c_gr):
    w = GLA_HEADS * dv

    def fn(row0, o, gr, g, do):
        gr = gr.astype(F32)
        sg = _sigmoid(gr)
        don = do * (gr * sg)
        ohs, dos = [], []
        for h in range(GLA_HEADS):
            sl = slice(h * dv, (h + 1) * dv)
            oh = o[:, sl]
            r = lax.rsqrt(jnp.mean(oh * oh, axis=-1, keepdims=True) + EPS)
            xh = oh * r
            dxh = don[:, sl] * g[:, sl]
            ohs.append(xh)
            dos.append(r * (dxh - xh * jnp.mean(dxh * xh, axis=-1, keepdims=True)))
        xh = jnp.concatenate(ohs, axis=1)
        dgr = do * (xh * g) * (sg * (1.0 + gr * (1.0 - sg)))
        return jnp.concatenate(dos, axis=1), dgr, _colsum(don * xh)

    return _ew("gla_post_bwd", fn,
               [("tile", o_raw, w, 0), ("tile", proj, w, c_gr), ("full", gn), ("tile", do_gla, w, 0)],
               [("tile", F32, w), ("tile", BF16, w), ("acc", F32, w)], t, 264)


def _adamw(name, w, g, m, v, layer=None, into=None):
    nl, rows, cols = w.shape
    tm = _pick(rows, max(8, (512 * 1024) // max(cols, 1) // 8 * 8), 8)

    def body(w_ref, g_ref, m_ref, v_ref, *rest):
        go_ref, d_ref, nm_ref, nv_ref = rest[-4:]
        gg = g_ref[...]
        nm = ADAM_B1 * m_ref[...] + (1.0 - ADAM_B1) * gg
        nv = ADAM_B2 * v_ref[...] + (1.0 - ADAM_B2) * (gg * gg)
        m_hat = nm / (1.0 - ADAM_B1 ** ADAM_STEP)
        v_hat = nv / (1.0 - ADAM_B2 ** ADAM_STEP)
        go_ref[...] = gg
        d_ref[...] = -ADAM_LR * (m_hat / (jnp.sqrt(v_hat) + ADAM_EPS) + ADAM_WD * w_ref[...])
        nm_ref[...] = nm
        nv_ref[...] = nv

    out_shape = [jax.ShapeDtypeStruct((nl, rows, cols), F32)] * 4
    if layer is None:
        spec = pl.BlockSpec((None, tm, cols), lambda l, i: (l, i, 0))
        return pl.pallas_call(
            body, name=name, grid=(nl, rows // tm), in_specs=[spec] * 4, out_specs=[spec] * 4, out_shape=out_shape,
            compiler_params=_params(("parallel", "parallel")),
        )(w, g, m, v)
    spec = pl.BlockSpec((None, tm, cols), lambda i: (layer, i, 0))
    in_specs = [spec, pl.BlockSpec((tm, cols), lambda i: (i, 0)), spec, spec]
    args, aliases = [w, g, m, v], {}
    if into is not None:
        in_specs += [pl.BlockSpec(memory_space=pl.ANY)] * 4
        args += list(into)
        aliases = {4 + k: k for k in range(4)}
    return pl.pallas_call(
        body, name=name, grid=(rows // tm,), in_specs=in_specs, out_specs=[spec] * 4, out_shape=out_shape,
        input_output_aliases=aliases, compiler_params=_params(("parallel",)),
    )(*args)


def _me():
    return lax.axis_index("x"), lax.axis_index("y"), lax.axis_index("c")


def _hbm_specs(n):
    return [pl.BlockSpec(memory_space=pl.ANY)] * n


_HBM = pl.BlockSpec(memory_space=pltpu.HBM)
_SEM = pl.BlockSpec(memory_space=pltpu.SEMAPHORE)
_EFFECT = pltpu.SideEffectType.DATAFLOW_SIDE_EFFECTING
N_PEERS = N_CHIPS - 1


def _other_chips(x, y):
    return [(1 - x, y), (x, 1 - y), (1 - x, 1 - y)]


def _split_copies(plan, src, land, send_sems, recv_sems):
    me = _me()
    copies = []
    for i in range(len(src)):
        for j, (s, d, peer) in enumerate(plan(src[i], land[i], me)):
            k = plan.copies * i + j
            copies.append(pltpu.make_async_remote_copy(
                src_ref=s, dst_ref=d, send_sem=send_sems.at[k], recv_sem=recv_sems.at[k], device_id=peer,
                device_id_type=MESH))
    return copies


def _split_start(name, srcs, lands, plan, after=None):
    n = len(srcs)
    extra = [] if after is None else [after]

    def body(*refs):
        src, land = refs[:n], refs[n:2 * n]
        send_sems, recv_sems = refs[2 * n + len(extra)], refs[2 * n + len(extra) + 1]
        token = refs[-1]
        for cp in _split_copies(plan, src, land, send_sems, recv_sems):
            cp.start()
        token[...] = jnp.zeros_like(token)

    out_shape = ([pltpu.SemaphoreType.DMA((plan.copies * n,)), pltpu.SemaphoreType.DMA((plan.copies * n,))]
                 + [pltpu.HBM(a.shape, a.dtype) for a in list(srcs) + list(lands)]
                 + [jax.ShapeDtypeStruct((8, LANES), F32)])
    res = pl.pallas_call(
        body, name=name, out_shape=out_shape,
        in_specs=[_HBM] * (2 * n) + [pl.BlockSpec(memory_space=pl.ANY)] * len(extra),
        out_specs=[_SEM, _SEM] + [_HBM] * (2 * n) + [pl.BlockSpec(memory_space=pltpu.VMEM)],
        input_output_aliases={i: 2 + i for i in range(2 * n)},
        compiler_params=pltpu.CompilerParams(has_side_effects=_EFFECT),
    )(*[pltpu.with_memory_space_constraint(a, pltpu.HBM) for a in list(srcs) + list(lands)], *extra)
    return res[0], res[1], res[2:2 + n], res[2 + n:2 + 2 * n], res[-1]


def _split_wait(name, send_sems, recv_sems, srcs, lands, plan, after):
    n = len(srcs)

    def body(*refs):
        src, land = refs[:n], refs[n:2 * n]
        s_sems, r_sems = refs[2 * n], refs[2 * n + 1]
        for cp in _split_copies(plan, src, land, s_sems, r_sems):
            cp.wait_send()
            cp.wait_recv()

    res = pl.pallas_call(
        body, name=name, out_shape=[pltpu.HBM(a.shape, a.dtype) for a in list(srcs) + list(lands)],
        in_specs=[_HBM] * (2 * n) + [_SEM, _SEM, pl.BlockSpec(memory_space=pl.ANY)], out_specs=[_HBM] * (2 * n),
        input_output_aliases={i: i for i in range(2 * n)},
        compiler_params=pltpu.CompilerParams(has_side_effects=_EFFECT),
    )(*srcs, *lands, send_sems, recv_sems, after)
    return res[:n], res[n:]


def _half(ref_rows, c):
    half = ref_rows // 2
    return pl.ds(c * half, half)


def _gather_plan(src, land, me):
    x, y, c = me
    rows = _half(src.shape[0], c)
    return [(src.at[rows], land.at[2 * x + y, rows], (px, py, c)) for px, py in _other_chips(x, y)]


def _scatter_plan(src, land, me):
    x, y, c = me
    return [(src.at[2 * px + py], land.at[2 * x + y], (px, py, c)) for px, py in _other_chips(x, y)]


def _swap_plan(src, land, me):
    x, y, c = me
    return [(src.at[:, _half(src.shape[1], 1 - c)], land, (x, y, 1 - c))]


_gather_plan.copies = N_PEERS
_scatter_plan.copies = N_PEERS
_swap_plan.copies = 1


def _forward_halves(name, gathered):
    n = len(gathered)

    def body(*refs):
        buf = refs[n:2 * n]
        send_sems, recv_sems = refs[2 * n:]
        x, y, c = _me()
        copies = []
        for i in range(n):
            rows = _half(buf[i].shape[1], c)
            for j, (px, py) in enumerate(_other_chips(x, y)):
                cp = pltpu.make_async_remote_copy(
                    src_ref=buf[i].at[2 * px + py, rows], dst_ref=buf[i].at[2 * px + py, rows],
                    send_sem=send_sems.at[N_PEERS * i + j], recv_sem=recv_sems.at[N_PEERS * i + j],
                    device_id=(x, y, 1 - c), device_id_type=MESH)
                cp.start()
                copies.append(cp)
        for cp in copies:
            cp.wait_recv()
        for cp in copies:
            cp.wait_send()

    return pl.pallas_call(
        body, name=name, in_specs=_hbm_specs(n), out_specs=_hbm_specs(n),
        out_shape=[jax.ShapeDtypeStruct(g.shape, g.dtype) for g in gathered],
        input_output_aliases={i: i for i in range(n)},
        scratch_shapes=[pltpu.SemaphoreType.DMA((N_PEERS * n,)), pltpu.SemaphoreType.DMA((N_PEERS * n,))],
    )(*gathered)


def _all_gather_small(v):
    def body(v_ref, out_ref, send_sems, recv_sems, local_sem):
        x, y, c = _me()
        mine = pltpu.make_async_copy(v_ref, out_ref.at[4 * x + 2 * y + c], local_sem)
        mine.start()
        copies = []
        for k in range(1, N_DEV):
            peer = (x ^ ((k >> 2) & 1), y ^ ((k >> 1) & 1), c ^ (k & 1))
            cp = pltpu.make_async_remote_copy(
                src_ref=v_ref, dst_ref=out_ref.at[4 * x + 2 * y + c], send_sem=send_sems.at[k - 1],
                recv_sem=recv_sems.at[k - 1], device_id=peer, device_id_type=MESH)
            cp.start()
            copies.append(cp)
        for cp in copies:
            cp.wait_recv()
        for cp in copies:
            cp.wait_send()
        mine.wait()

    return pl.pallas_call(
        body, name="all_gather_small", in_specs=_hbm_specs(1), out_specs=pl.BlockSpec(memory_space=pl.ANY),
        out_shape=jax.ShapeDtypeStruct((N_DEV,) + v.shape, v.dtype),
        scratch_shapes=[pltpu.SemaphoreType.DMA((N_DEV - 1,)), pltpu.SemaphoreType.DMA((N_DEV - 1,)),
                        pltpu.SemaphoreType.DMA],
    )(v)


def _share_halves(name, grads):
    n = len(grads)

    def body(*refs):
        buf = refs[n:2 * n]
        send_sems, recv_sems = refs[2 * n:]
        x, y, c = _me()
        copies = []
        for i in range(n):
            rows = _half(buf[i].shape[0], c)
            cp = pltpu.make_async_remote_copy(
                src_ref=buf[i].at[rows], dst_ref=buf[i].at[rows], send_sem=send_sems.at[i],
                recv_sem=recv_sems.at[i], device_id=(x, y, 1 - c), device_id_type=MESH)
            cp.start()
            copies.append(cp)
        for cp in copies:
            cp.wait_recv()
        for cp in copies:
            cp.wait_send()

    out_shape = [jax.ShapeDtypeStruct(s.shape, s.dtype) for s in grads]
    return pl.pallas_call(
        body, name=name, in_specs=_hbm_specs(n), out_specs=_hbm_specs(n), out_shape=out_shape,
        input_output_aliases={i: i for i in range(n)},
        scratch_shapes=[pltpu.SemaphoreType.DMA((n,)), pltpu.SemaphoreType.DMA((n,))],
    )(*grads)


def _pair_sum(name, own, recv, chip, core):
    nch, half, cdim = recv.shape
    tm = _pick(half, max(8, (512 * 1024) // cdim // 16 * 16), 16)
    nt = half // tm

    def body(c_ref, k_ref, a_ref, b_ref, s16_ref):
        s16_ref[...] = (a_ref[...] + b_ref[...]).astype(BF16)

    other = lambda j, c: (c[0] + 1 + j) % nch
    spec = pl.BlockSpec((None, tm, cdim), lambda j, i, c, k: (other(j, c), i, 0))
    return pl.pallas_call(
        body, name=name,
        grid_spec=pltpu.PrefetchScalarGridSpec(
            num_scalar_prefetch=2, grid=(nch - 1, nt),
            in_specs=[pl.BlockSpec((None, tm, cdim), lambda j, i, c, k: (other(j, c), k[0] * nt + i, 0)), spec],
            out_specs=spec),
        out_shape=jax.ShapeDtypeStruct((nch, half, cdim), BF16),
        compiler_params=_params(("parallel", "parallel")),
    )(chip, core, own, recv)


def _chip_sum(name, own, recv, landed, chip, core):
    nch, half, cdim = recv.shape
    tm = _pick(half, max(8, (512 * 1024) // cdim // 16 * 16), 16)
    nt = half // tm

    def body(c_ref, k_ref, own_ref, recv_ref, *rest):
        landed_refs, out_ref = rest[:nch], rest[-1]
        me = c_ref[0]
        mine = own_ref[...] + recv_ref[...]
        acc = None
        for j in range(nch):
            term = jnp.where(me == j, mine, landed_refs[j][...].astype(F32))
            acc = term if acc is None else acc + term
        out_ref[...] = acc

    landed_specs = [pl.BlockSpec((None, tm, cdim), functools.partial(lambda i, c, k, j: (j, i, 0), j=j))
                    for j in range(nch)]
    return pl.pallas_call(
        body, name=name,
        grid_spec=pltpu.PrefetchScalarGridSpec(
            num_scalar_prefetch=2, grid=(nt,),
            in_specs=[pl.BlockSpec((None, tm, cdim), lambda i, c, k: (c[0], k[0] * nt + i, 0)),
                      pl.BlockSpec((None, tm, cdim), lambda i, c, k: (c[0], i, 0))] + landed_specs,
            out_specs=pl.BlockSpec((tm, cdim), lambda i, c, k: (k[0] * nt + i, 0))),
        out_shape=jax.ShapeDtypeStruct((2 * half, cdim), F32),
        compiler_params=_params(("parallel",)),
    )(chip, core, own, recv, *([landed] * nch))


def _sum_devices(gathered):
    _, r, cdim = gathered.shape

    def body(g_ref, o_ref):
        acc = g_ref[0]
        for k in range(1, N_DEV):
            acc = acc + g_ref[k]
        o_ref[...] = acc

    return pl.pallas_call(
        body, name="sum_devices", out_shape=jax.ShapeDtypeStruct((r, cdim), F32),
        compiler_params=_params(),
    )(gathered)


class _Layout:
    def __init__(self, d):
        self.d = d
        self.fw = d // 2
        self.fd = self.fw // FOX_HEADS
        self.gk = d // 2
        self.gv = d
        self.dk = self.gk // GLA_HEADS
        self.dv = self.gv // GLA_HEADS
        self.c_fq = 0
        self.c_gq = self.fw
        self.c_gv = self.c_gq + self.gk
        self.c_gr = self.c_gv + self.gv
        self.c_fkv = self.c_gr + self.gv
        self.c_gates = self.c_fkv + 2 * self.fw
        self.c_gk = self.c_gates + 2 * d
        self.c_small = self.c_gk + self.gk
        self.n_main = self.c_small
        self.n_p = self.c_small + LANES
        self.o_fk = self.fw
        self.o_fv = 2 * self.fw
        self.o_ff = 3 * self.fw
        self.o_gq = self.o_ff + FOX_HEADS
        self.o_gk = self.o_gq + self.gk
        self.o_gv = self.o_gk + self.gk
        self.o_gr = self.o_gv + self.gv
        self.o_ga = self.o_gr + self.gv
        self.o_gf = self.o_ga + GLA_RANK
        self.o_gg = self.o_gf + d
        self.n_orig = self.o_gg + d

    def to_p(self, shards):
        per = self.n_orig // N_CHIPS
        ranges = [(0, self.fw), (self.o_gq, self.gk), (self.o_gv, self.gv), (self.o_gr, self.gv)]
        for h in range(FOX_HEADS):
            ranges += [(self.o_fk + h * self.fd, self.fd), (self.o_fv + h * self.fd, self.fd)]
        ranges += [(self.o_gf, 2 * self.d), (self.o_gk, self.gk), (self.o_ff, FOX_HEADS), (self.o_ga, GLA_RANK)]
        pieces = []
        for a, width in ranges:
            for j in range(a // per, (a + width - 1) // per + 1):
                lo, hi = max(a, j * per), min(a + width, (j + 1) * per)
                pieces.append(shards[j][:, lo - j * per:hi - j * per])
        pieces.append(jnp.zeros((shards.shape[1], LANES - FOX_HEADS - GLA_RANK), shards.dtype))
        return jnp.concatenate(pieces, axis=1)

    def from_segments(self, seg):
        per = self.n_orig // N_CHIPS
        fd = self.fd
        atoms = [("fq", 0, self.fw)]
        atoms += [("fkv", 2 * h * fd, fd) for h in range(FOX_HEADS)]
        atoms += [("fkv", (2 * h + 1) * fd, fd) for h in range(FOX_HEADS)]
        atoms += [("small", 0, FOX_HEADS), ("gq", 0, self.gk), ("gk", 0, self.gk), ("gv", 0, self.gv),
                  ("gr", 0, self.gv), ("small", FOX_HEADS, GLA_RANK), ("gates", 0, 2 * self.d)]
        shards = [[] for _ in range(N_CHIPS)]
        pos = 0
        for name, c0, width in atoms:
            for j in range(pos // per, (pos + width - 1) // per + 1):
                lo, hi = max(pos, j * per), min(pos + width, (j + 1) * per)
                shards[j].append(seg[name][:, c0 + lo - pos:c0 + hi - pos])
            pos += width
        assert pos == self.n_orig
        return jnp.stack([jnp.concatenate(s, axis=1) for s in shards])


def _layer_fwd(lay, h, p, t, late=None):
    d = lay.d
    xn, xn_t = _rms_fwd_t("rms_mix_fwd", h, p["norm_mix_g"], t, d)
    proj = _mm("mm_proj", xn, p["w_in"], mode="nn", m=t, n=lay.n_main, k=d, out_dtypes=(BF16,))
    small = _mm("mm_small", xn, p["w_in"], mode="nn", m=t, n=LANES, k=d, b_c0=lay.c_small)
    cs = _fox_gate_fwd(small, p["b_forget_p"], t)
    ct = cs[:, :FOX_HEADS].T
    c_col, c_row = ct[:, :, None], ct[:, None, :]
    o_fox, lse = _fox_fwd(proj, c_col, c_row, t, lay.fd, lay.c_fq, lay.c_fkv)
    glog = _gla_gate_fwd(small, p["w_alpha_p"], p["b_alpha"], t, lay.gk)
    o_raw, s_prev = _gla_fwd(proj, glog, t, lay.dk, lay.dv, lay.c_gq, lay.c_gk, lay.c_gv)
    o_gla = _gla_post_fwd(o_raw, proj, p["gla_norm_g"], t, lay.dv, lay.c_gr)
    if late is not None:
        p.update(late(o_gla))
    a_fox = _mm("mm_o_fox", o_fox, p["w_o_fox"], mode="nn", m=t, n=d, k=lay.fw, b_shards=N_CHIPS)
    a_gla = _mm("mm_o_gla", o_gla, p["w_o_gla"], mode="nn", m=t, n=d, k=lay.gv)
    y = _merge_fwd(a_fox, a_gla, proj, lay.c_gates, t, d)
    h1 = _mm("mm_out", y, p["w_out"], mode="nn", m=t, n=d, k=d, extras=[h], epilogue=lambda acc, res: (res + acc,))
    xn2, xn2_t = _rms_fwd_t("rms_mlp_fwd", h1, p["norm_mlp_g"], t, d)
    u, act = _mm("mm_ff1", xn2, p["w_ff1"], mode="nn", m=t, n=4 * d, k=d, out_dtypes=(BF16, BF16), b_shards=N_CHIPS,
                 epilogue=lambda acc: (acc, jnp.square(jnp.maximum(acc, 0.0))))
    h2 = _mm("mm_ff2", act, p["w_ff2"], mode="nn", m=t, n=d, k=4 * d, extras=[h1],
             epilogue=lambda acc, res: (res + acc,))
    saved = dict(h=h, xn_t=xn_t, proj=proj, small=small, c_col=c_col, c_row=c_row, o_fox=o_fox, lse=lse, glog=glog,
                 o_raw=o_raw, s_prev=s_prev, o_gla=o_gla, a_fox=a_fox, a_gla=a_gla, y=y, h1=h1, xn2_t=xn2_t, u=u, act=act)
    return h2, saved


def _layer_bwd(lay, dh2, p, s, t, gates=None, first=False):
    d = lay.d
    g = {}

    def gated(gain, point):
        return gain + gates[point](g) if gates and point in gates else gain
    du = _mm("mm_dact", dh2, p["w_ff2"], mode="nt", m=t, n=4 * d, k=d, extras=[s["u"]], out_dtypes=(BF16,),
             epilogue=lambda acc, u: (acc * (2.0 * jnp.maximum(u.astype(F32), 0.0)),))
    g["w_ff2"] = _mm("mm_dw_ff2", s["act"], dh2, mode="tn", m=4 * d, n=d, k=t)
    g["w_ff1"] = _mm("mm_dw_ff1", s["xn2_t"], du, mode="nn", m=d, n=4 * d, k=t, tk=t, out_shards=N_CHIPS)
    dxn2 = _mm("mm_dxn2", du, p["w_ff1"], mode="nt", m=t, n=d, k=4 * d, b_shards=N_CHIPS)
    dh1, g["norm_mlp_g"] = _rms_bwd("rms_mlp_bwd", s["h1"], gated(p["norm_mlp_g"], "mlp"), dxn2, dh2, t, d)
    dy = _mm("mm_dy", dh1, p["w_out"], mode="nt", m=t, n=d, k=d)
    g["w_out"] = _mm("mm_dw_out", s["y"], dh1, mode="tn", m=d, n=d, k=t)
    da_fox, da_gla, dgates = _merge_bwd(dy, s["a_fox"], s["a_gla"], s["proj"], lay.c_gates, t, d)
    g["w_o_fox"] = _mm("mm_dw_o_fox", s["o_fox"], da_fox, mode="tn", m=lay.fw, n=d, k=t, out_shards=N_CHIPS)
    do_fox = _mm("mm_do_fox", da_fox, p["w_o_fox"], mode="nt", m=t, n=lay.fw, k=d, b_shards=N_CHIPS)
    g["w_o_gla"] = _mm("mm_dw_o_gla", s["o_gla"], da_gla, mode="tn", m=lay.gv, n=d, k=t)
    do_gla = _mm("mm_do_gla", da_gla, p["w_o_gla"], mode="nt", m=t, n=lay.gv, k=d)
    do_raw, dgr, g["gla_norm_g"] = _gla_post_bwd(s["o_raw"], s["proj"], gated(p["gla_norm_g"], "out"), do_gla, t,
                                                 lay.dv, lay.c_gr)
    dgq, dgk, dgv, dglog = _gla_bwd(s["proj"], s["glog"], s["s_prev"], do_raw, t, lay.dk, lay.dv,
                                    lay.c_gq, lay.c_gk, lay.c_gv)
    dz, g["b_alpha"] = _gla_gate_bwd(dglog, s["small"], p["w_alpha_p"], p["b_alpha"], t, lay.gk)
    g["w_alpha_p"] = _mm("mm_dw_alpha", s["small"], dz, mode="tn", m=LANES, n=lay.gk, k=t)
    dga = _mm("mm_dga", dz, p["w_alpha_p"], mode="nt", m=t, n=LANES, k=lay.gk)
    delta = _fox_delta(s["o_fox"], do_fox, t, lay.fd)
    dfq, dfkv, dc, dr = _fox_bwd(s["proj"], s["c_col"], s["c_row"], s["lse"], delta, do_fox, t, lay.fd,
                                 lay.c_fq, lay.c_fkv)
    dc_p = jnp.pad((dc[:, 0, :] + dr[:, :, 0]).T, ((0, 0), (0, LANES - FOX_HEADS)))
    dsmall, g["b_forget_p"] = _fox_gate_bwd(dc_p, s["small"], p["b_forget_p"], dga, t)
    segs = [("fq", dfq, lay.c_fq), ("gq", dgq, lay.c_gq), ("gv", dgv, lay.c_gv), ("gr", dgr, lay.c_gr),
            ("fkv", dfkv, lay.c_fkv), ("gates", dgates, lay.c_gates), ("gk", dgk, lay.c_gk),
            ("small", dsmall, lay.c_small)]
    dw_in = {nm: _mm("mm_dw_in_" + nm, s["xn_t"], dseg, mode="nn", m=d, n=dseg.shape[1], k=t, tk=t)
             for nm, dseg, _ in segs}
    g["w_in"] = lay.from_segments(dw_in)
    dxn = _mm("mm_dxn_gates", dgates, p["w_in"], mode="nt", m=t, n=d, k=2 * d, b_c0=lay.c_gates)
    dxn = _mm("mm_dxn_small", dsmall, p["w_in"], mode="nt", m=t, n=d, k=LANES, b_c0=lay.c_small, extras=[dxn],
              epilogue=lambda acc, prev: (prev + acc,))
    for nm, (a1, c1), (a2, c2) in (("q", (dfq, lay.c_fq), (dgq, lay.c_gq)), ("v", (dgv, lay.c_gv), (dgr, lay.c_gr)),
                                   ("k", (dfkv, lay.c_fkv), (dgk, lay.c_gk))):
        dxn = _mm_nt2("mm_dxn_" + nm, a1, c1, a2, c2, p["w_in"], dxn, m=t, n=d)
    if first:
        dx, head, g["norm_mix_g"] = _rms_bwd_x("rms_mix_bwd_x", s["h"], gated(p["norm_mix_g"], "in"), dxn, dh1, t, d)
        return (dx, head), g
    dh, g["norm_mix_g"] = _rms_bwd("rms_mix_bwd", s["h"], gated(p["norm_mix_g"], "in"), dxn, dh1, t, d)
    return dh, g


def _sequence_step(x, target, meta, layers, final_g):
    seq, d = x.shape
    t = seq + ROW0
    lay = _Layout(d)
    h = jnp.pad(x, ((ROW0, 0), (0, 0))).at[PAD:ROW0].set(meta)
    target_p = jnp.pad(target, ((ROW0, 0), (0, 0)))
    saved = []
    for p in layers:
        h, s = _layer_fwd(lay, h, p, t)
        saved.append(s)
    dh, dg_final, loss_part = _loss_head(h, final_g, target_p, t, d)
    grads = [None] * len(layers)
    for l in reversed(range(len(layers))):
        dh, grads[l] = _layer_bwd(lay, dh, layers[l], saved[l], t)
    return loss_part, dh[ROW0:], dh[PAD:ROW0], grads, dg_final


_SMALL_ROWS = 48


def _pack_small(d, meta, mix, gla, mlp, final, b_alpha, b_forget, w_alpha2):
    rows = [meta.reshape(N_META, d), mix.reshape(DEPTH, d), gla.reshape(DEPTH, d), mlp.reshape(DEPTH, d),
            final.reshape(1, d), b_alpha.reshape(1, d),
            jnp.pad(b_forget.reshape(1, DEPTH * FOX_HEADS), ((0, 0), (0, d - DEPTH * FOX_HEADS))),
            jnp.zeros((7, d), F32), w_alpha2.reshape(GLA_RANK, d)]
    return jnp.concatenate(rows, axis=0)


def _unpack_small(d, packed):
    return dict(meta=packed[:N_META], norm_mix_g=packed[16:18], gla_norm_g=packed[18:20], norm_mlp_g=packed[20:22],
                final_norm_g=packed[22], b_alpha=packed[23].reshape(DEPTH, d // 2),
                b_forget=packed[24, :DEPTH * FOX_HEADS].reshape(DEPTH, FOX_HEADS),
                w_alpha2=packed[32:48].reshape(DEPTH, GLA_RANK, d // 2))


_BIG = ("w_in", "w_o_fox", "w_o_gla", "w_out", "w_ff1", "w_ff2")
_COL_SHARDED = ("w_in", "w_o_fox", "w_ff1")


def _full_matrix(name, gathered_l):
    nch, r, c = gathered_l.shape
    if name in _COL_SHARDED:
        return gathered_l.transpose(1, 0, 2).reshape(r, nch * c)
    return gathered_l.reshape(nch * r, c)


def _shard_major(name, full):
    r, c = full.shape
    if name in _COL_SHARDED:
        return full.reshape(r, N_CHIPS, c // N_CHIPS).transpose(1, 0, 2)
    return full.reshape(N_CHIPS, r // N_CHIPS, c)


def kernel(x, meta_tokens, norm_mix_g, w_in, b_forget, w_alpha2, b_alpha, gla_norm_g, w_o_fox, w_o_gla, w_out, norm_mlp_g, w_ff1, w_ff2, final_norm_g, loss_target, m_meta_tokens, m_norm_mix_g, m_w_in, m_b_forget, m_w_alpha2, m_b_alpha, m_gla_norm_g, m_w_o_fox, m_w_o_gla, m_w_out, m_norm_mlp_g, m_w_ff1, m_w_ff2, m_final_norm_g, v_meta_tokens, v_norm_mix_g, v_w_in, v_b_forget, v_w_alpha2, v_b_alpha, v_gla_norm_g, v_w_o_fox, v_w_o_gla, v_w_out, v_norm_mlp_g, v_w_ff1, v_w_ff2, v_final_norm_g):
    d = x.shape[2]
    lay = _Layout(d)
    xi, yi, ci = lax.axis_index("x"), lax.axis_index("y"), lax.axis_index("c")
    chip = (2 * xi + yi).astype(jnp.int32)
    w = dict(w_in=w_in, w_alpha2=w_alpha2, w_o_fox=w_o_fox, w_o_gla=w_o_gla, w_out=w_out, w_ff1=w_ff1, w_ff2=w_ff2)
    m = dict(w_in=m_w_in, w_alpha2=m_w_alpha2, w_o_fox=m_w_o_fox, w_o_gla=m_w_o_gla, w_out=m_w_out, w_ff1=m_w_ff1,
             w_ff2=m_w_ff2)
    v = dict(w_in=v_w_in, w_alpha2=v_w_alpha2, w_o_fox=v_w_o_fox, w_o_gla=v_w_o_gla, w_out=v_w_out, w_ff1=v_w_ff1,
             w_ff2=v_w_ff2)

    seq = x.shape[1]
    t = seq + ROW0
    core_idx = ci.astype(jnp.int32)[None]
    chip_idx = chip[None]

    cols = d // N_CHIPS
    small_w = jnp.concatenate([meta_tokens, w_alpha2.reshape(-1, cols)], axis=0)
    small_raw = _all_gather_small(small_w)
    small_all = small_raw[0::2]
    alpha_full = small_all[:, N_META:].reshape(N_CHIPS, DEPTH, GLA_RANK, lay.gk // N_CHIPS)
    alpha_full = alpha_full.transpose(1, 2, 0, 3).reshape(DEPTH, GLA_RANK, lay.gk)
    groups = [(0, ("w_in",)), (0, _BIG[1:]), (1, _BIG)]
    started, after = [], small_raw
    for gi, (l, names) in enumerate(groups):
        own16 = [w[n][l].astype(BF16) for n in names]
        lands = [lax.empty((N_CHIPS,) + o.shape, BF16) for o in own16]
        started.append(_split_start("gather_start_%d" % gi, own16, lands, _gather_plan, after=after))
        after = started[gi][4]
    meta_full = small_all[:, :N_META].transpose(1, 0, 2).reshape(N_META, d) + after[0, 0]

    def gathered(gi, after):
        send_sems, recv_sems, srcs, lands, _ = started[gi]
        srcs, lands = _split_wait("gather_wait_%d" % gi, send_sems, recv_sems, srcs, lands, _gather_plan, after)
        lands = _forward_halves("gather_forward_%d" % gi, lands)
        return {n: lax.dynamic_update_slice(g, o[None], (chip, 0, 0)) for n, g, o in zip(groups[gi][1], lands, srcs)}

    def early_weights(l, gl):
        w_alpha_p = jnp.zeros((LANES, lay.gk), BF16).at[FOX_HEADS:FOX_HEADS + GLA_RANK].set(
            alpha_full[l].astype(BF16))
        return dict(
            w_in=lay.to_p(gl["w_in"]), w_alpha_p=w_alpha_p,
            norm_mix_g=norm_mix_g[l][None], norm_mlp_g=norm_mlp_g[l][None], gla_norm_g=gla_norm_g[l][None],
            b_alpha=b_alpha[l][None],
            b_forget_p=jnp.pad(b_forget[l][None], ((0, 0), (0, LANES - FOX_HEADS))))

    def late_weights(gl):
        return dict(w_o_fox=gl["w_o_fox"], w_o_gla=_full_matrix("w_o_gla", gl["w_o_gla"]),
                    w_out=_full_matrix("w_out", gl["w_out"]), w_ff1=gl["w_ff1"],
                    w_ff2=_full_matrix("w_ff2", gl["w_ff2"]))

    h = jnp.pad(x[0], ((ROW0, 0), (0, 0))).at[PAD:ROW0].set(meta_full)
    layers, saved = [], []
    layers.append(early_weights(0, gathered(0, after=h)))
    h, s = _layer_fwd(lay, h, layers[0], t, late=lambda after: late_weights(gathered(1, after)))
    saved.append(s)
    gl = gathered(2, after=h)
    layers.append({**early_weights(1, gl), **late_weights(gl)})
    h, s = _layer_fwd(lay, h, layers[1], t)
    saved.append(s)
    dh, dg_final, loss_part = _loss_head(h, final_norm_g[None], jnp.pad(loss_target[0], ((ROW0, 0), (0, 0))), t, d)
    loss = lax.psum(loss_part[0, 0], ("x", "y", "c"))

    def partial_of(g, n):
        return g[n] if n in ("w_ff1", "w_o_fox", "w_in") else _shard_major(n, g[n])

    scatter_groups = dict(mlp=("w_ff1", "w_ff2"), out=("w_o_fox", "w_o_gla", "w_out"))
    scatter_groups["in"] = ("w_in",)
    swapping, scattered = [], {}

    def start_swap(l, grp, g, after=None):
        parts = [partial_of(g, n) for n in scatter_groups[grp]]
        lands = [lax.empty((p_.shape[0], p_.shape[1] // 2, p_.shape[2]), F32) for p_ in parts]
        started_swap = _split_start("swap_start_%d_%s" % (l, grp), parts, lands, _swap_plan, after=after)
        swapping.append((l, grp, started_swap))
        return started_swap[4]

    def start_scatter(after):
        l, grp, (send_sems, recv_sems, srcs, lands, _) = swapping.pop(0)
        names = scatter_groups[grp]
        tag = "%d_%s" % (l, grp)
        parts, from_sibling = _split_wait("swap_wait_" + tag, send_sems, recv_sems, srcs, lands, _swap_plan, after)
        sums = [_pair_sum("pair_sum_%d_%s" % (l, n), p_, r_, chip_idx, core_idx)
                for n, p_, r_ in zip(names, parts, from_sibling)]
        lands = [lax.empty(s16.shape, BF16) for s16 in sums]
        send_sems, recv_sems, srcs, lands, token = _split_start("scatter_start_" + tag, sums, lands, _scatter_plan)
        scattered[l, grp] = (send_sems, recv_sems, srcs, lands, parts, from_sibling)
        return token

    def gate(l, grp, g):
        token = start_swap(l, grp, g)
        if len(swapping) > 1:
            token = start_scatter(after=token)
        return token[0, 0]

    def gates_for(l, points):
        return {grp: functools.partial(gate, l, grp) for grp in points}

    grads = [None] * DEPTH
    dh, grads[1] = _layer_bwd(lay, dh, layers[1], saved[1], t, gates=gates_for(1, ("mlp", "out", "in")))
    (grad_x, head), grads[0] = _layer_bwd(lay, dh, layers[0], saved[0], t, gates=gates_for(0, ("mlp", "out")),
                                          first=True)
    d_meta = head[PAD:ROW0]

    stack = lambda key: jnp.concatenate([grads[l][key] for l in range(DEPTH)], axis=0)
    b_forget_g = jnp.concatenate([grads[l]["b_forget_p"][:, :FOX_HEADS] for l in range(DEPTH)], axis=0)
    alpha_g = jnp.stack([grads[l]["w_alpha_p"][FOX_HEADS:FOX_HEADS + GLA_RANK] for l in range(DEPTH)])
    packed = _pack_small(d, d_meta, stack("norm_mix_g"), stack("gla_norm_g"), stack("norm_mlp_g"), dg_final,
                         stack("b_alpha"), b_forget_g, alpha_g)
    small_g = _unpack_small(d, _sum_devices(_all_gather_small(packed)))
    small_g["meta"] = lax.dynamic_slice_in_dim(small_g["meta"], chip * (d // N_CHIPS), d // N_CHIPS, axis=1)
    alpha_shard = lax.dynamic_slice_in_dim(small_g["w_alpha2"], chip * (lay.gk // N_CHIPS), lay.gk // N_CHIPS, axis=2)

    after = start_swap(0, "in", grads[0], after=small_g["final_norm_g"])
    while swapping:
        after = start_scatter(after)
    outs = {n: None for n in _BIG}
    for l in reversed(range(DEPTH)):
        names, layer_g = [], []
        for grp in ("mlp", "out", "in"):
            send_sems, recv_sems, srcs, lands, parts, from_sibling = scattered[l, grp]
            _, lands = _split_wait("scatter_wait_%d_%s" % (l, grp), send_sems, recv_sems, srcs, lands, _scatter_plan,
                                 after)
            names += scatter_groups[grp]
            layer_g += [_chip_sum("chip_sum_%d_%s" % (l, n), p_, r_, landed, chip_idx, core_idx)
                        for n, p_, r_, landed in zip(scatter_groups[grp], parts, from_sibling, lands)]
        layer_g = _share_halves("share_halves_%d" % l, layer_g)
        for n, g in zip(names, layer_g):
            outs[n] = _adamw("adamw_%d_%s" % (l, n), w[n], g, m[n], v[n], layer=l, into=outs[n])
        after = outs[names[-1]][0]
    out_g, out_d, out_m, out_v = {}, {}, {}, {}
    for n in _BIG:
        out_g[n], out_d[n], out_m[n], out_v[n] = outs[n]
    out_g["w_alpha2"], out_d["w_alpha2"], out_m["w_alpha2"], out_v["w_alpha2"] = _adamw(
        "adamw_w_alpha2", w["w_alpha2"], alpha_shard, m["w_alpha2"], v["w_alpha2"])
    sm_w = dict(meta_tokens=meta_tokens, norm_mix_g=norm_mix_g, b_forget=b_forget, b_alpha=b_alpha,
                gla_norm_g=gla_norm_g, norm_mlp_g=norm_mlp_g, final_norm_g=final_norm_g)
    sm_m = dict(meta_tokens=m_meta_tokens, norm_mix_g=m_norm_mix_g, b_forget=m_b_forget, b_alpha=m_b_alpha,
                gla_norm_g=m_gla_norm_g, norm_mlp_g=m_norm_mlp_g, final_norm_g=m_final_norm_g)
    sm_v = dict(meta_tokens=v_meta_tokens, norm_mix_g=v_norm_mix_g, b_forget=v_b_forget, b_alpha=v_b_alpha,
                gla_norm_g=v_gla_norm_g, norm_mlp_g=v_norm_mlp_g, final_norm_g=v_final_norm_g)
    sm_g = dict(meta_tokens=small_g["meta"], norm_mix_g=small_g["norm_mix_g"], b_forget=small_g["b_forget"],
                b_alpha=small_g["b_alpha"], gla_norm_g=small_g["gla_norm_g"], norm_mlp_g=small_g["norm_mlp_g"],
                final_norm_g=small_g["final_norm_g"])
    names_small = list(sm_w)
    sizes = [sm_w[n].size for n in names_small]
    width = 512
    total = -(-sum(sizes) // (8 * width)) * (8 * width)

    def pack_flat(dct, fill):
        flat = jnp.concatenate([dct[n].reshape(-1) for n in names_small])
        return jnp.pad(flat, (0, total - flat.shape[0]), constant_values=fill).reshape(1, -1, width)

    res = _adamw("adamw_small", pack_flat(sm_w, 0.0), pack_flat(sm_g, 0.0), pack_flat(sm_m, 0.0), pack_flat(sm_v, 1.0))
    offs = [0]
    for sz in sizes:
        offs.append(offs[-1] + sz)
    for i, n in enumerate(names_small):
        out_g[n] = sm_g[n].reshape(sm_w[n].shape)
        out_d[n], out_m[n], out_v[n] = [r.reshape(-1)[offs[i]:offs[i + 1]].reshape(sm_w[n].shape) for r in res[1:]]

    order = ["meta_tokens", "norm_mix_g", "w_in", "b_forget", "w_alpha2", "b_alpha", "gla_norm_g", "w_o_fox",
             "w_o_gla", "w_out", "norm_mlp_g", "w_ff1", "w_ff2", "final_norm_g"]
    return (loss, grad_x[None], *[out_g[n] for n in order], *[out_d[n] for n in order],
            *[out_m[n] for n in order], *[out_v[n] for n in order])
```

```python
import functools

import numpy as np

import jax
import jax.numpy as jnp
from jax import lax
from jax.experimental import pallas as pl
from jax.experimental.pallas import tpu as pltpu

F32 = jnp.float32
BF16 = jnp.bfloat16

N_META = 16
PAD = 112
ROW0 = PAD + N_META
EPS = 1e-6
MASK_VALUE = -1e30
FOX_HEADS = 8
FOX_GROUP = 2
GLA_HEADS = 4
GLA_RANK = 16
GLA_TAU = 16.0
GLA_CHUNK = 64
DEPTH = 2
N_CHIPS = 4
N_DEV = 8

ADAM_LR = 0.001
ADAM_B1 = 0.9
ADAM_B2 = 0.999
ADAM_EPS = 1e-08
ADAM_WD = 0.01
ADAM_STEP = 10

LANES = 128
VMEM_LIMIT = 56 * 1024 * 1024
MESH = pl.DeviceIdType.MESH


def _pick(n, target, mult):
    best = None
    for d in range(mult, min(n, target) + 1, mult):
        if n % d == 0:
            best = d
    return n if best is None else best


def _params(sem=None):
    return pltpu.CompilerParams(dimension_semantics=sem, vmem_limit_bytes=VMEM_LIMIT)


def _bf(v):
    return v if v.dtype == BF16 else v.astype(BF16)


def _sigmoid(z):
    return 1.0 / (1.0 + jnp.exp(-z))


def _log_sigmoid(z):
    return jnp.minimum(z, 0.0) - jnp.log(1.0 + jnp.exp(-jnp.abs(z)))


def _split3(v):
    a = v.astype(BF16)
    r = v - a.astype(F32)
    b = r.astype(BF16)
    c = (r - b.astype(F32)).astype(BF16)
    return a, b, c


def _dot(a, b, dims):
    return lax.dot_general(a, b, (dims, ((), ())), preferred_element_type=F32)


NN = ((1,), (0,))
NT = ((1,), (1,))
TN = ((0,), (0,))


def _tri_dot(tri, v, dims=NN):
    a, b, c = _split3(v)
    return _dot(tri, a, dims) + _dot(tri, b, dims) + _dot(tri, c, dims)


def _mm(name, a, b, *, mode, m, n, k, b_c0=0, extras=(), epilogue=None, out_dtypes=(F32,),
        b_shards=1, out_shards=1, tm=1056, tn=1024, tk=2048):
    tm = _pick(m, tm, LANES if mode == "tn" else 16)
    tn = _pick(n // max(b_shards if mode == "nn" else 1, out_shards), tn, LANES)
    if mode == "tn":
        tk = _pick(k, 2112, 16)
    else:
        tk = _pick(k // (b_shards if mode == "nt" else 1), tk, LANES)
    assert b_c0 % (tk if mode == "nt" else tn) == 0 and (b_shards == 1 or b_c0 == 0)
    nk = k // tk
    if mode == "tn":
        a_spec = pl.BlockSpec((tk, tm), lambda i, j, kk: (kk, i))
    else:
        a_spec = pl.BlockSpec((tm, tk), lambda i, j, kk: (i, kk))
    if mode == "nt":
        dims = NT
        if b_shards > 1:
            per = (k // b_shards) // tk
            b_spec = pl.BlockSpec((None, tn, tk), lambda i, j, kk: (kk // per, j, kk % per))
        else:
            b_spec = pl.BlockSpec((tn, tk), lambda i, j, kk: (j, kk + b_c0 // tk))
    else:
        dims = NN if mode == "nn" else TN
        if b_shards > 1:
            per = (n // b_shards) // tn
            b_spec = pl.BlockSpec((None, tk, tn), lambda i, j, kk: (j // per, kk, j % per))
        else:
            b_spec = pl.BlockSpec((tk, tn), lambda i, j, kk: (kk, j + b_c0 // tn))
    ex_specs = [pl.BlockSpec((tm, tn), lambda i, j, kk: (i, j)) for _ in extras]
    if out_shards > 1:
        oper = (n // out_shards) // tn
        out_specs = [pl.BlockSpec((None, tm, tn), lambda i, j, kk: (j // oper, i, j % oper)) for _ in out_dtypes]
        out_shape = [jax.ShapeDtypeStruct((out_shards, m, n // out_shards), dt) for dt in out_dtypes]
    else:
        out_specs = [pl.BlockSpec((tm, tn), lambda i, j, kk: (i, j)) for _ in out_dtypes]
        out_shape = [jax.ShapeDtypeStruct((m, n), dt) for dt in out_dtypes]
    n_ex = len(extras)
    n_out = len(out_dtypes)

    def finish(acc, ex_refs, out_refs):
        vals = (acc,) if epilogue is None else epilogue(acc, *[r[...] for r in ex_refs])
        for r, v in zip(out_refs, vals):
            r[...] = v.astype(r.dtype)

    def body(a_ref, b_ref, *rest):
        ex_refs = rest[:n_ex]
        out_refs = rest[n_ex:n_ex + n_out]
        prod = _dot(_bf(a_ref[...]), _bf(b_ref[...]), dims)
        if nk == 1:
            finish(prod, ex_refs, out_refs)
            return
        acc_ref = rest[n_ex + n_out]
        kk = pl.program_id(2)

        @pl.when(kk == 0)
        def _():
            acc_ref[...] = prod

        @pl.when((kk > 0) & (kk < nk - 1))
        def _():
            acc_ref[...] += prod

        @pl.when(kk == nk - 1)
        def _():
            finish(acc_ref[...] + prod, ex_refs, out_refs)

    outs = pl.pallas_call(
        body,
        name=name,
        grid=(m // tm, n // tn, nk),
        in_specs=[a_spec, b_spec] + ex_specs,
        out_specs=out_specs,
        out_shape=out_shape,
        scratch_shapes=[pltpu.VMEM((tm, tn), F32)] if nk > 1 else [],
        compiler_params=_params(("parallel", "parallel", "arbitrary")),
    )(a, b, *extras)
    return outs[0] if n_out == 1 else outs


def _mm_nt2(name, a1, c1, a2, c2, b, prev, *, m, n, tm=528, tn=1024):
    k1, k2 = a1.shape[1], a2.shape[1]
    assert c1 % k1 == 0 and c2 % k2 == 0
    tm = _pick(m, tm, 16)
    tn = _pick(n, tn, LANES)

    def body(a1_ref, a2_ref, b1_ref, b2_ref, *rest):
        acc = _dot(a1_ref[...], b1_ref[...], NT) + _dot(a2_ref[...], b2_ref[...], NT)
        if prev is not None:
            acc = rest[0][...] + acc
        rest[-1][...] = acc

    tile = pl.BlockSpec((tm, tn), lambda i, j: (i, j))
    in_specs = [pl.BlockSpec((tm, k1), lambda i, j: (i, 0)), pl.BlockSpec((tm, k2), lambda i, j: (i, 0)),
                pl.BlockSpec((tn, k1), lambda i, j: (j, c1 // k1)), pl.BlockSpec((tn, k2), lambda i, j: (j, c2 // k2))]
    args = [a1, a2, b, b]
    if prev is not None:
        in_specs.append(tile)
        args.append(prev)
    return pl.pallas_call(
        body, name=name, grid=(m // tm, n // tn), in_specs=in_specs, out_specs=tile,
        out_shape=jax.ShapeDtypeStruct((m, n), F32),
        compiler_params=_params(("parallel", "parallel")),
    )(*args)


def _ew(name, fn, ins, outs, rows, tm):
    tm = _pick(rows, tm, 16)
    in_specs, args = [], []
    for spec in ins:
        if spec[0] == "tile":
            _, arr, width, c0 = spec
            assert c0 % width == 0
            in_specs.append(pl.BlockSpec((tm, width), functools.partial(lambda i, o: (i, o), o=c0 // width)))
        else:
            arr = spec[1]
            in_specs.append(pl.BlockSpec(arr.shape, lambda i: (0, 0)))
        args.append(arr)
    out_specs, out_shape = [], []
    for kind, dt, width in outs:
        if kind == "tile":
            out_specs.append(pl.BlockSpec((tm, width), lambda i: (i, 0)))
            out_shape.append(jax.ShapeDtypeStruct((rows, width), dt))
        else:
            out_specs.append(pl.BlockSpec((1, width), lambda i: (0, 0)))
            out_shape.append(jax.ShapeDtypeStruct((1, width), dt))
    n_in = len(ins)
    has_acc = any(o[0] == "acc" for o in outs)

    def body(*refs):
        i = pl.program_id(0)
        vals = fn(i * tm, *[r[...] for r in refs[:n_in]])
        for (kind, _, _), r, v in zip(outs, refs[n_in:], vals):
            if kind == "tile":
                r[...] = v.astype(r.dtype)
            else:
                @pl.when(i == 0)
                def _():
                    r[...] = jnp.zeros_like(r)

                r[...] += v.astype(r.dtype)

    res = pl.pallas_call(
        body,
        name=name,
        grid=(rows // tm,),
        in_specs=in_specs,
        out_specs=out_specs,
        out_shape=out_shape,
        compiler_params=_params(("arbitrary",) if has_acc else ("parallel",)),
    )(*args)
    return res[0] if len(outs) == 1 else res


def _row_ids(row0, tm):
    return row0 + lax.broadcasted_iota(jnp.int32, (tm, 1), 0)


def _colsum(v):
    return jnp.sum(v, axis=0, keepdims=True)


def _rms_fwd_t(name, h, g, t, d):
    tm = _pick(t, 384, LANES)

    def body(x_ref, g_ref, y_ref, yt_ref):
        x = x_ref[...]
        y = x * lax.rsqrt(jnp.mean(x * x, axis=-1, keepdims=True) + EPS) * g_ref[...]
        y_ref[...] = y.astype(BF16)
        yt_ref[...] = y.T.astype(BF16)

    return pl.pallas_call(
        body, name=name, grid=(t // tm,),
        in_specs=[pl.BlockSpec((tm, d), lambda i: (i, 0)), pl.BlockSpec((1, d), lambda i: (0, 0))],
        out_specs=[pl.BlockSpec((tm, d), lambda i: (i, 0)), pl.BlockSpec((d, tm), lambda i: (0, i))],
        out_shape=[jax.ShapeDtypeStruct((t, d), BF16), jax.ShapeDtypeStruct((d, t), BF16)],
        compiler_params=_params(("parallel",)),
    )(h, g)


def _rms_bwd(name, h, g, dy, dres, t, d):
    def fn(row0, x, gg, dyv, dr):
        r = lax.rsqrt(jnp.mean(x * x, axis=-1, keepdims=True) + EPS)
        xh = x * r
        dxh = dyv * gg
        dx = r * (dxh - xh * jnp.mean(dxh * xh, axis=-1, keepdims=True))
        out = jnp.where(_row_ids(row0, x.shape[0]) >= PAD, dr + dx, 0.0)
        return out, _colsum(dyv * xh)

    return _ew(name, fn, [("tile", h, d, 0), ("full", g), ("tile", dy, d, 0), ("tile", dres, d, 0)],
               [("tile", F32, d), ("acc", F32, d)], t, 264)


def _rms_bwd_x(name, h, g, dy, dres, t, d):
    tm = ROW0

    def body(x_ref, g_ref, dy_ref, dr_ref, dx_ref, head_ref, dg_ref):
        i = pl.program_id(0)
        x = x_ref[...]
        r = lax.rsqrt(jnp.mean(x * x, axis=-1, keepdims=True) + EPS)
        xh = x * r
        dxh = dy_ref[...] * g_ref[...]
        dx = r * (dxh - xh * jnp.mean(dxh * xh, axis=-1, keepdims=True))
        out = jnp.where(_row_ids(i * tm, tm) >= PAD, dr_ref[...] + dx, 0.0)

        @pl.when(i == 0)
        def _():
            head_ref[...] = out
            dg_ref[...] = jnp.zeros_like(dg_ref)

        dx_ref[...] = out
        dg_ref[...] += _colsum(dy_ref[...] * xh)

    tile = pl.BlockSpec((tm, d), lambda i: (i, 0))
    fixed = lambda shape: pl.BlockSpec(shape, lambda i: (0, 0))
    return pl.pallas_call(
        body, name=name, grid=(t // tm,),
        in_specs=[tile, fixed((1, d)), tile, tile],
        out_specs=[pl.BlockSpec((tm, d), lambda i: (jnp.maximum(i - 1, 0), 0)), fixed((tm, d)), fixed((1, d))],
        out_shape=[jax.ShapeDtypeStruct((t - ROW0, d), F32), jax.ShapeDtypeStruct((ROW0, d), F32),
                   jax.ShapeDtypeStruct((1, d), F32)],
        compiler_params=_params(("arbitrary",)),
    )(h, g, dy, dres)


def _loss_head(h, g, target_p, t, d):
    def fn(row0, x, gg, tgt):
        real = _row_ids(row0, x.shape[0]) >= ROW0
        r = lax.rsqrt(jnp.mean(x * x, axis=-1, keepdims=True) + EPS)
        xh = x * r
        err = jnp.where(real, xh * gg - tgt, 0.0)
        loss_rows = 0.5 * jnp.mean(err * err, axis=-1, keepdims=True)
        dyv = err * (1.0 / d)
        dxh = dyv * gg
        dx = r * (dxh - xh * jnp.mean(dxh * xh, axis=-1, keepdims=True))
        loss_part = jnp.sum(loss_rows, axis=0, keepdims=True) * jnp.ones((1, LANES), F32)
        return jnp.where(real, dx, 0.0), _colsum(dyv * xh), loss_part

    return _ew("loss_head", fn, [("tile", h, d, 0), ("full", g), ("tile", target_p, d, 0)],
               [("tile", F32, d), ("acc", F32, d), ("acc", F32, LANES)], t, 264)


def _merge_fwd(a_fox, a_gla, proj, c_gates, t, d):
    def fn(row0, af, ag, gates):
        gates = gates.astype(F32)
        return (_sigmoid(gates[:, :d]) * af + _sigmoid(gates[:, d:]) * ag,)

    return _ew("merge_fwd", fn, [("tile", a_fox, d, 0), ("tile", a_gla, d, 0), ("tile", proj, 2 * d, c_gates)],
               [("tile", BF16, d)], t, 264)


def _merge_bwd(dy, a_fox, a_gla, proj, c_gates, t, d):
    def fn(row0, dyv, af, ag, gates):
        gates = gates.astype(F32)
        sf = _sigmoid(gates[:, :d])
        sg = _sigmoid(gates[:, d:])
        dgates = jnp.concatenate([dyv * af * sf * (1.0 - sf), dyv * ag * sg * (1.0 - sg)], axis=1)
        return dyv * sf, dyv * sg, dgates

    return _ew("merge_bwd", fn,
               [("tile", dy, d, 0), ("tile", a_fox, d, 0), ("tile", a_gla, d, 0), ("tile", proj, 2 * d, c_gates)],
               [("tile", BF16, d), ("tile", BF16, d), ("tile", BF16, 2 * d)], t, 264)


def _fox_gate_fwd(small, b_forget_p, t):
    tb = _pick(t, 384, LANES)

    def body(s_ref, b_ref, c_ref, carry_ref):
        i = pl.program_id(0)

        @pl.when(i == 0)
        def _():
            carry_ref[...] = jnp.zeros_like(carry_ref)

        logf = _log_sigmoid(s_ref[...] + b_ref[...])
        logf = jnp.where(_row_ids(i * tb, tb) >= PAD, logf, 0.0)
        r = lax.broadcasted_iota(jnp.int32, (tb, tb), 0)
        c = lax.broadcasted_iota(jnp.int32, (tb, tb), 1)
        tri = (c <= r).astype(BF16)
        cs = _tri_dot(tri, logf) + carry_ref[...]
        c_ref[...] = cs
        carry_ref[...] = cs[tb - 1:tb, :]

    return pl.pallas_call(
        body, name="fox_gate_fwd", grid=(t // tb,),
        in_specs=[pl.BlockSpec((tb, LANES), lambda i: (i, 0)), pl.BlockSpec((1, LANES), lambda i: (0, 0))],
        out_specs=pl.BlockSpec((tb, LANES), lambda i: (i, 0)),
        out_shape=jax.ShapeDtypeStruct((t, LANES), F32),
        scratch_shapes=[pltpu.VMEM((1, LANES), F32)],
        compiler_params=_params(("arbitrary",)),
    )(small, b_forget_p)


def _fox_gate_bwd(dc, small, b_forget_p, dga, t):
    tb = _pick(t, 384, LANES)
    nb = t // tb

    def body(dc_ref, s_ref, b_ref, dga_ref, ds_ref, db_ref, carry_ref):
        i = pl.program_id(0)

        @pl.when(i == 0)
        def _():
            carry_ref[...] = jnp.zeros_like(carry_ref)
            db_ref[...] = jnp.zeros_like(db_ref)

        r = lax.broadcasted_iota(jnp.int32, (tb, tb), 0)
        c = lax.broadcasted_iota(jnp.int32, (tb, tb), 1)
        tri = (c >= r).astype(BF16)
        dlogf = _tri_dot(tri, dc_ref[...]) + carry_ref[...]
        carry_ref[...] = dlogf[0:1, :]
        z = s_ref[...] + b_ref[...]
        dff = dlogf * _sigmoid(-z)
        lane = lax.broadcasted_iota(jnp.int32, (tb, LANES), 1)
        keep = (_row_ids((nb - 1 - i) * tb, tb) >= PAD) & (lane < FOX_HEADS)
        dff = jnp.where(keep, dff, 0.0)
        ds_ref[...] = dff + dga_ref[...]
        db_ref[...] += _colsum(dff)

    rev = lambda i: (nb - 1 - i, 0)
    return pl.pallas_call(
        body, name="fox_gate_bwd", grid=(nb,),
        in_specs=[pl.BlockSpec((tb, LANES), rev), pl.BlockSpec((tb, LANES), rev),
                  pl.BlockSpec((1, LANES), lambda i: (0, 0)), pl.BlockSpec((tb, LANES), rev)],
        out_specs=[pl.BlockSpec((tb, LANES), rev), pl.BlockSpec((1, LANES), lambda i: (0, 0))],
        out_shape=[jax.ShapeDtypeStruct((t, LANES), F32), jax.ShapeDtypeStruct((1, LANES), F32)],
        scratch_shapes=[pltpu.VMEM((1, LANES), F32)],
        compiler_params=_params(("arbitrary",)),
    )(dc, small, b_forget_p, dga)


def _fox_pairs(nb, by_key):
    if by_key:
        pairs = [(qi, ki) for ki in range(nb) for qi in range(ki, nb)]
    else:
        pairs = [(qi, ki) for qi in range(nb) for ki in range(qi + 1)]
    return (jnp.asarray(np.array([p[0] for p in pairs], np.int32)),
            jnp.asarray(np.array([p[1] for p in pairs], np.int32)), len(pairs))


def _fox_specs(tb, fd, c_fq, c_fkv):
    gw = FOX_GROUP * fd
    q0, kv0 = c_fq // gw, c_fkv // (2 * gw)
    return dict(
        q=pl.BlockSpec((tb, gw), lambda g, p, qt, kt: (qt[p], q0 + g)),
        kv=pl.BlockSpec((tb, 2 * gw), lambda g, p, qt, kt: (kt[p], kv0 + g)),
        col=pl.BlockSpec((FOX_GROUP, tb, 1), lambda g, p, qt, kt: (g, qt[p], 0)),
        row=pl.BlockSpec((FOX_GROUP, 1, tb), lambda g, p, qt, kt: (g, 0, kt[p])),
        head=pl.BlockSpec((tb, gw), lambda g, p, qt, kt: (qt[p], g)),
        key_kv=pl.BlockSpec((tb, 2 * gw), lambda g, p, qt, kt: (kt[p], g)),
    )


def _fox_mask(qi, ki, tb):
    row = qi * tb + lax.broadcasted_iota(jnp.int32, (tb, tb), 0)
    col = ki * tb + lax.broadcasted_iota(jnp.int32, (tb, tb), 1)
    return (col <= row) & (col >= PAD)


def _fox_heads(q_ref, kv_ref, fd):
    return [(q_ref[:, hh * fd:(hh + 1) * fd], kv_ref[:, 2 * hh * fd:(2 * hh + 1) * fd],
             kv_ref[:, (2 * hh + 1) * fd:(2 * hh + 2) * fd]) for hh in range(FOX_GROUP)]


def _fox_fwd(proj, c_col, c_row, t, fd, c_fq, c_fkv):
    tb = _pick(t, 384, LANES)
    nb = t // tb
    scale = fd ** -0.5
    sp = _fox_specs(tb, fd, c_fq, c_fkv)
    qt, kt, npairs = _fox_pairs(nb, by_key=False)

    def body(qt_ref, kt_ref, q_ref, kv_ref, cq_ref, ck_ref, o_ref, lse_ref, m_ref, l_ref, acc_ref):
        p = pl.program_id(1)
        qi, ki = qt_ref[p], kt_ref[p]

        @pl.when(ki == 0)
        def _():
            m_ref[...] = jnp.full_like(m_ref, -jnp.inf)
            l_ref[...] = jnp.zeros_like(l_ref)
            acc_ref[...] = jnp.zeros_like(acc_ref)

        def update(masked):
            mask = _fox_mask(qi, ki, tb) if masked else None
            for hh, (q, k, v) in enumerate(_fox_heads(q_ref, kv_ref, fd)):
                s = _dot(q, k, NT) * scale + cq_ref[hh] - ck_ref[hh]
                if masked:
                    s = jnp.where(mask, s, MASK_VALUE)
                m_prev = m_ref[hh]
                m_new = jnp.maximum(m_prev, jnp.max(s, axis=-1, keepdims=True))
                alpha = jnp.exp(m_prev - m_new)
                pe = jnp.exp(s - m_new)
                l_ref[hh] = alpha * l_ref[hh] + jnp.sum(pe, axis=-1, keepdims=True)
                acc_ref[hh] = alpha * acc_ref[hh] + _dot(pe.astype(BF16), v, NN)
                m_ref[hh] = m_new

        edge = (ki == 0) | (ki == qi)
        pl.when(edge)(functools.partial(update, True))
        pl.when(jnp.logical_not(edge))(functools.partial(update, False))

        @pl.when(ki == qi)
        def _():
            real = _row_ids(qi * tb, tb) >= PAD
            for hh in range(FOX_GROUP):
                o_ref[:, hh * fd:(hh + 1) * fd] = jnp.where(real, acc_ref[hh] / l_ref[hh], 0.0)
                lse_ref[hh] = m_ref[hh] + jnp.log(l_ref[hh])

    return pl.pallas_call(
        body, name="fox_fwd",
        grid_spec=pltpu.PrefetchScalarGridSpec(
            num_scalar_prefetch=2, grid=(FOX_HEADS // FOX_GROUP, npairs),
            in_specs=[sp["q"], sp["kv"], sp["col"], sp["row"]],
            out_specs=[sp["head"], sp["col"]],
            scratch_shapes=[pltpu.VMEM((FOX_GROUP, tb, 1), F32), pltpu.VMEM((FOX_GROUP, tb, 1), F32),
                            pltpu.VMEM((FOX_GROUP, tb, fd), F32)]),
        out_shape=[jax.ShapeDtypeStruct((t, FOX_HEADS * fd), F32), jax.ShapeDtypeStruct((FOX_HEADS, t, 1), F32)],
        compiler_params=_params(("parallel", "arbitrary")),
    )(qt, kt, proj, proj, c_col, c_row)


def _fox_delta(o_fox, do_fox, t, fd):
    tb = _pick(t, 384, LANES)

    def body(o_ref, do_ref, out_ref):
        for h in range(FOX_HEADS):
            sl = slice(h * fd, (h + 1) * fd)
            out_ref[h] = jnp.sum(o_ref[:, sl] * do_ref[:, sl].astype(BF16).astype(F32), axis=-1, keepdims=True)

    w = FOX_HEADS * fd
    return pl.pallas_call(
        body, name="fox_delta", grid=(t // tb,),
        in_specs=[pl.BlockSpec((tb, w), lambda i: (i, 0)), pl.BlockSpec((tb, w), lambda i: (i, 0))],
        out_specs=pl.BlockSpec((FOX_HEADS, tb, 1), lambda i: (0, i, 0)),
        out_shape=jax.ShapeDtypeStruct((FOX_HEADS, t, 1), F32),
        compiler_params=_params(("parallel",)),
    )(o_fox, do_fox)


def _fox_bwd(proj, c_col, c_row, lse, delta, do_fox, t, fd, c_fq, c_fkv):
    tb = _pick(t, 384, LANES)
    nb = t // tb
    scale = fd ** -0.5
    sp = _fox_specs(tb, fd, c_fq, c_fkv)
    qt, kt, npairs = _fox_pairs(nb, by_key=True)
    gw = FOX_GROUP * fd

    def body(qt_ref, kt_ref, q_ref, kv_ref, cq_ref, ck_ref, lse_ref, dl_ref, do_ref, dq_ref, dkv_ref, dc_ref, dr_ref,
             dq_acc, dk_acc, dv_acc, dc_acc, dr_acc):
        p = pl.program_id(1)
        qi, ki = qt_ref[p], kt_ref[p]

        @pl.when(p == 0)
        def _():
            dq_acc[...] = jnp.zeros_like(dq_acc)
            dr_acc[...] = jnp.zeros_like(dr_acc)

        @pl.when(qi == ki)
        def _():
            dk_acc[...] = jnp.zeros_like(dk_acc)
            dv_acc[...] = jnp.zeros_like(dv_acc)
            dc_acc[...] = jnp.zeros_like(dc_acc)

        rows = pl.ds(pl.multiple_of(qi * tb, LANES), tb)

        def update(masked):
            mask = _fox_mask(qi, ki, tb) if masked else None
            for hh, (q, k, v) in enumerate(_fox_heads(q_ref, kv_ref, fd)):
                do = _bf(do_ref[:, hh * fd:(hh + 1) * fd])
                s = _dot(q, k, NT) * scale + cq_ref[hh] - ck_ref[hh]
                if masked:
                    s = jnp.where(mask, s, MASK_VALUE)
                pr = jnp.exp(s - lse_ref[hh])
                dp = _dot(do, v, NT)
                ds = pr * (dp - dl_ref[hh])
                ds16 = ds.astype(BF16)
                dv_acc[hh] += _dot(pr.astype(BF16), do, TN)
                dk_acc[hh] += _dot(ds16, q, TN)
                dc_acc[hh] += _colsum(ds)
                dr_acc[hh, rows, :] += jnp.sum(ds, axis=-1, keepdims=True)
                dq_acc[hh, rows, :] += _dot(ds16, k, NN)

        edge = (ki == 0) | (ki == qi)
        pl.when(edge)(functools.partial(update, True))
        pl.when(jnp.logical_not(edge))(functools.partial(update, False))

        @pl.when(qi == nb - 1)
        def _():
            for hh in range(FOX_GROUP):
                dkv_ref[:, 2 * hh * fd:(2 * hh + 1) * fd] = (dk_acc[hh] * scale).astype(dkv_ref.dtype)
                dkv_ref[:, (2 * hh + 1) * fd:(2 * hh + 2) * fd] = dv_acc[hh].astype(dkv_ref.dtype)
                dc_ref[hh] = -dc_acc[hh]

        @pl.when(p == npairs - 1)
        def _():
            for hh in range(FOX_GROUP):
                dq_ref[:, hh * fd:(hh + 1) * fd] = (dq_acc[hh] * scale).astype(dq_ref.dtype)
            dr_ref[...] = dr_acc[...]

    return pl.pallas_call(
        body, name="fox_bwd",
        grid_spec=pltpu.PrefetchScalarGridSpec(
            num_scalar_prefetch=2, grid=(FOX_HEADS // FOX_GROUP, npairs),
            in_specs=[sp["q"], sp["kv"], sp["col"], sp["row"], sp["col"], sp["col"], sp["head"]],
            out_specs=[pl.BlockSpec((t, gw), lambda g, p, qt, kt: (0, g)), sp["key_kv"], sp["row"],
                       pl.BlockSpec((FOX_GROUP, t, 1), lambda g, p, qt, kt: (g, 0, 0))],
            scratch_shapes=[pltpu.VMEM((FOX_GROUP, t, fd), F32), pltpu.VMEM((FOX_GROUP, tb, fd), F32),
                            pltpu.VMEM((FOX_GROUP, tb, fd), F32), pltpu.VMEM((FOX_GROUP, 1, tb), F32),
                            pltpu.VMEM((FOX_GROUP, t, 1), F32)]),
        out_shape=[jax.ShapeDtypeStruct((t, FOX_HEADS * fd), BF16), jax.ShapeDtypeStruct((t, 2 * FOX_HEADS * fd), BF16),
                   jax.ShapeDtypeStruct((FOX_HEADS, 1, t), F32), jax.ShapeDtypeStruct((FOX_HEADS, t, 1), F32)],
        compiler_params=_params(("parallel", "arbitrary")),
    )(qt, kt, proj, proj, c_col, c_row, lse, delta, do_fox)


def _gla_gate_fwd(small, w_alpha_p, b_alpha, t, gk):
    def fn(row0, s, w, b):
        z = _dot(s.astype(BF16), w, NN) + b
        return (jnp.where(_row_ids(row0, s.shape[0]) >= PAD, _log_sigmoid(z) * (1.0 / GLA_TAU), 0.0),)

    return _ew("gla_gate_fwd", fn, [("tile", small, LANES, 0), ("full", w_alpha_p), ("full", b_alpha)],
               [("tile", F32, gk)], t, 264)


def _gla_gate_bwd(dglog, small, w_alpha_p, b_alpha, t, gk):
    def fn(row0, dg, s, w, b):
        z = _dot(s.astype(BF16), w, NN) + b
        dz = jnp.where(_row_ids(row0, s.shape[0]) >= PAD, dg * (1.0 / GLA_TAU) * _sigmoid(-z), 0.0)
        return dz, _colsum(dz)

    return _ew("gla_gate_bwd", fn,
               [("tile", dglog, gk, 0), ("tile", small, LANES, 0), ("full", w_alpha_p), ("full", b_alpha)],
               [("tile", BF16, gk), ("acc", F32, gk)], t, 264)


def _gla_chunk(q, k, g, scale, cs):
    r = lax.broadcasted_iota(jnp.int32, (cs, cs), 0)
    c = lax.broadcasted_iota(jnp.int32, (cs, cs), 1)
    causal = c <= r
    b = _tri_dot(causal.astype(BF16), g)
    bl = b[cs - 1:cs, :]
    eb, einv, eend = jnp.exp(b), jnp.exp(-b), jnp.exp(bl - b)
    qd = q.astype(F32) * scale * eb
    kf = k.astype(F32)
    return causal, (eb, einv, eend), bl, qd, kf * einv, kf * eend


def _gla_fwd(proj, glog, t, dk, dv, c_q, c_k, c_v):
    cs = GLA_CHUNK
    nc = t // cs
    wk, wv = GLA_HEADS * dk, GLA_HEADS * dv
    scale = dk ** -0.5

    def body(q_ref, k_ref, v_ref, g_ref, o_ref, sp_ref, st_ref):
        @pl.when(pl.program_id(0) == 0)
        def _():
            st_ref[...] = jnp.zeros_like(st_ref)

        for h in range(GLA_HEADS):
            ks, vs = slice(h * dk, (h + 1) * dk), slice(h * dv, (h + 1) * dv)
            v = v_ref[:, vs]
            causal, _, bl, qd, ki, ke = _gla_chunk(q_ref[:, ks], k_ref[:, ks], g_ref[:, ks], scale, cs)
            st = st_ref[h]
            sp_ref[h] = st
            a = jnp.where(causal, _dot(qd.astype(BF16), ki.astype(BF16), NT), 0.0)
            o_ref[:, vs] = _dot(a.astype(BF16), v, NN) + _dot(qd.astype(BF16), st.astype(BF16), NT)
            st_ref[h] = st * jnp.exp(bl) + _dot(v, ke.astype(BF16), TN)

    return pl.pallas_call(
        body, name="gla_fwd", grid=(nc,),
        in_specs=[pl.BlockSpec((cs, wk), lambda n: (n, c_q // wk)), pl.BlockSpec((cs, wk), lambda n: (n, c_k // wk)),
                  pl.BlockSpec((cs, wv), lambda n: (n, c_v // wv)), pl.BlockSpec((cs, wk), lambda n: (n, 0))],
        out_specs=[pl.BlockSpec((cs, wv), lambda n: (n, 0)),
                   pl.BlockSpec((None, GLA_HEADS, dv, dk), lambda n: (n, 0, 0, 0))],
        out_shape=[jax.ShapeDtypeStruct((t, wv), F32), jax.ShapeDtypeStruct((nc, GLA_HEADS, dv, dk), F32)],
        scratch_shapes=[pltpu.VMEM((GLA_HEADS, dv, dk), F32)],
        compiler_params=_params(("arbitrary",)),
    )(proj, proj, proj, glog)


def _gla_bwd(proj, glog, s_prev, do_raw, t, dk, dv, c_q, c_k, c_v):
    cs = GLA_CHUNK
    nc = t // cs
    wk, wv = GLA_HEADS * dk, GLA_HEADS * dv
    scale = dk ** -0.5

    def body(q_ref, k_ref, v_ref, g_ref, sp_ref, do_ref, dq_ref, dk_ref, dv_ref, dg_ref, dst_ref):
        @pl.when(pl.program_id(0) == 0)
        def _():
            dst_ref[...] = jnp.zeros_like(dst_ref)

        for h in range(GLA_HEADS):
            ks, vs = slice(h * dk, (h + 1) * dk), slice(h * dv, (h + 1) * dv)
            v = v_ref[:, vs]
            do = do_ref[:, vs].astype(BF16)
            causal, (eb, einv, eend), bl, qd, ki, ke = _gla_chunk(q_ref[:, ks], k_ref[:, ks], g_ref[:, ks], scale, cs)
            qd16, ki16, ke16 = qd.astype(BF16), ki.astype(BF16), ke.astype(BF16)
            st = sp_ref[h]
            dst = dst_ref[h]
            dst16 = dst.astype(BF16)
            a = jnp.where(causal, _dot(qd16, ki16, NT), 0.0).astype(BF16)
            da = jnp.where(causal, _dot(do, v, NT), 0.0).astype(BF16)
            dvv = _dot(a, do, TN) + _dot(ke16, dst16, NT)
            dqd = _dot(da, ki16, NN) + _dot(do, st.astype(BF16), NN)
            dki = _dot(da, qd16, TN)
            dke = _dot(v, dst16, NN)
            dl = jnp.exp(bl)
            ddl = _colsum(dst * st)
            dst_ref[h] = dst * dl + _dot(do, qd16, TN)
            dq_ref[:, ks] = (dqd * eb * scale).astype(dq_ref.dtype)
            dk_ref[:, ks] = (dki * einv + dke * eend).astype(dk_ref.dtype)
            dv_ref[:, vs] = dvv.astype(dv_ref.dtype)
            db = dqd * qd - dki * ki - dke * ke
            db_last = _colsum(dke * ke) + ddl * dl
            r = lax.broadcasted_iota(jnp.int32, (cs, cs), 0)
            c = lax.broadcasted_iota(jnp.int32, (cs, cs), 1)
            dg_ref[:, ks] = _tri_dot((c >= r).astype(BF16), db) + db_last

    rev = lambda f: (lambda n: f(nc - 1 - n))
    return pl.pallas_call(
        body, name="gla_bwd", grid=(nc,),
        in_specs=[pl.BlockSpec((cs, wk), rev(lambda n: (n, c_q // wk))), pl.BlockSpec((cs, wk), rev(lambda n: (n, c_k // wk))),
                  pl.BlockSpec((cs, wv), rev(lambda n: (n, c_v // wv))), pl.BlockSpec((cs, wk), rev(lambda n: (n, 0))),
                  pl.BlockSpec((None, GLA_HEADS, dv, dk), rev(lambda n: (n, 0, 0, 0))),
                  pl.BlockSpec((cs, wv), rev(lambda n: (n, 0)))],
        out_specs=[pl.BlockSpec((cs, wk), rev(lambda n: (n, 0))), pl.BlockSpec((cs, wk), rev(lambda n: (n, 0))),
                   pl.BlockSpec((cs, wv), rev(lambda n: (n, 0))), pl.BlockSpec((cs, wk), rev(lambda n: (n, 0)))],
        out_shape=[jax.ShapeDtypeStruct((t, wk), BF16), jax.ShapeDtypeStruct((t, wk), BF16),
                   jax.ShapeDtypeStruct((t, wv), BF16), jax.ShapeDtypeStruct((t, wk), F32)],
        scratch_shapes=[pltpu.VMEM((GLA_HEADS, dv, dk), F32)],
        compiler_params=_params(("arbitrary",)),
    )(proj, proj, proj, glog, s_prev, do_raw)


def _gla_post_fwd(o_raw, proj, gn, t, dv, c_gr):
    w = GLA_HEADS * dv

    def fn(row0, o, gr, g):
        gr = gr.astype(F32)
        outs = []
        for h in range(GLA_HEADS):
            oh = o[:, h * dv:(h + 1) * dv]
            outs.append(oh * lax.rsqrt(jnp.mean(oh * oh, axis=-1, keepdims=True) + EPS))
        on = jnp.concatenate(outs, axis=1) * g
        return (on * (gr * _sigmoid(gr)),)

    return _ew("gla_post_fwd", fn, [("tile", o_raw, w, 0), ("tile", proj, w, c_gr), ("full", gn)],
               [("tile", BF16, w)], t, 264)


def _gla_post_bwd(o_raw, proj, gn, do_gla, t, dv, c_gr):
    w = GLA_HEADS * dv

    def fn(row0, o, gr, g, do):
        gr = gr.astype(F32)
        sg = _sigmoid(gr)
        don = do * (gr * sg)
        ohs, dos = [], []
        for h in range(GLA_HEADS):
            sl = slice(h * dv, (h + 1) * dv)
            oh = o[:, sl]
            r = lax.rsqrt(jnp.mean(oh * oh, axis=-1, keepdims=True) + EPS)
            xh = oh * r
            dxh = don[:, sl] * g[:, sl]
            ohs.append(xh)
            dos.append(r * (dxh - xh * jnp.mean(dxh * xh, axis=-1, keepdims=True)))
        xh = jnp.concatenate(ohs, axis=1)
        dgr = do * (xh * g) * (sg * (1.0 + gr * (1.0 - sg)))
        return jnp.concatenate(dos, axis=1), dgr, _colsum(don * xh)

    return _ew("gla_post_bwd", fn,
               [("tile", o_raw, w, 0), ("tile", proj, w, c_gr), ("full", gn), ("tile", do_gla, w, 0)],
               [("tile", F32, w), ("tile", BF16, w), ("acc", F32, w)], t, 264)


def _adamw(name, w, g, m, v, layer=None, into=None):
    nl, rows, cols = w.shape
    tm = _pick(rows, max(8, (512 * 1024) // max(cols, 1) // 8 * 8), 8)

    def body(w_ref, g_ref, m_ref, v_ref, *rest):
        go_ref, d_ref, nm_ref, nv_ref = rest[-4:]
        gg = g_ref[...]
        nm = ADAM_B1 * m_ref[...] + (1.0 - ADAM_B1) * gg
        nv = ADAM_B2 * v_ref[...] + (1.0 - ADAM_B2) * (gg * gg)
        m_hat = nm / (1.0 - ADAM_B1 ** ADAM_STEP)
        v_hat = nv / (1.0 - ADAM_B2 ** ADAM_STEP)
        go_ref[...] = gg
        d_ref[...] = -ADAM_LR * (m_hat / (jnp.sqrt(v_hat) + ADAM_EPS) + ADAM_WD * w_ref[...])
        nm_ref[...] = nm
        nv_ref[...] = nv

    out_shape = [jax.ShapeDtypeStruct((nl, rows, cols), F32)] * 4
    if layer is None:
        spec = pl.BlockSpec((None, tm, cols), lambda l, i: (l, i, 0))
        return pl.pallas_call(
            body, name=name, grid=(nl, rows // tm), in_specs=[spec] * 4, out_specs=[spec] * 4, out_shape=out_shape,
            compiler_params=_params(("parallel", "parallel")),
        )(w, g, m, v)
    spec = pl.BlockSpec((None, tm, cols), lambda i: (layer, i, 0))
    in_specs = [spec, pl.BlockSpec((tm, cols), lambda i: (i, 0)), spec, spec]
    args, aliases = [w, g, m, v], {}
    if into is not None:
        in_specs += [pl.BlockSpec(memory_space=pl.ANY)] * 4
        args += list(into)
        aliases = {4 + k: k for k in range(4)}
    return pl.pallas_call(
        body, name=name, grid=(rows // tm,), in_specs=in_specs, out_specs=[spec] * 4, out_shape=out_shape,
        input_output_aliases=aliases, compiler_params=_params(("parallel",)),
    )(*args)


def _me():
    return lax.axis_index("x"), lax.axis_index("y"), lax.axis_index("c")


def _hbm_specs(n):
    return [pl.BlockSpec(memory_space=pl.ANY)] * n


_HBM = pl.BlockSpec(memory_space=pltpu.HBM)
_SEM = pl.BlockSpec(memory_space=pltpu.SEMAPHORE)
_EFFECT = pltpu.SideEffectType.DATAFLOW_SIDE_EFFECTING
N_PEERS = N_CHIPS - 1


def _other_chips(x, y):
    return [(1 - x, y), (x, 1 - y), (1 - x, 1 - y)]


def _split_copies(plan, src, land, send_sems, recv_sems):
    me = _me()
    copies = []
    for i in range(len(src)):
        for j, (s, d, peer) in enumerate(plan(src[i], land[i], me)):
            k = plan.copies * i + j
            copies.append(pltpu.make_async_remote_copy(
                src_ref=s, dst_ref=d, send_sem=send_sems.at[k], recv_sem=recv_sems.at[k], device_id=peer,
                device_id_type=MESH))
    return copies


def _split_start(name, srcs, lands, plan, after=None):
    n = len(srcs)
    extra = [] if after is None else [after]

    def body(*refs):
        src, land = refs[:n], refs[n:2 * n]
        send_sems, recv_sems = refs[2 * n + len(extra)], refs[2 * n + len(extra) + 1]
        token = refs[-1]
        for cp in _split_copies(plan, src, land, send_sems, recv_sems):
            cp.start()
        token[...] = jnp.zeros_like(token)

    out_shape = ([pltpu.SemaphoreType.DMA((plan.copies * n,)), pltpu.SemaphoreType.DMA((plan.copies * n,))]
                 + [pltpu.HBM(a.shape, a.dtype) for a in list(srcs) + list(lands)]
                 + [jax.ShapeDtypeStruct((8, LANES), F32)])
    res = pl.pallas_call(
        body, name=name, out_shape=out_shape,
        in_specs=[_HBM] * (2 * n) + [pl.BlockSpec(memory_space=pl.ANY)] * len(extra),
        out_specs=[_SEM, _SEM] + [_HBM] * (2 * n) + [pl.BlockSpec(memory_space=pltpu.VMEM)],
        input_output_aliases={i: 2 + i for i in range(2 * n)},
        compiler_params=pltpu.CompilerParams(has_side_effects=_EFFECT),
    )(*[pltpu.with_memory_space_constraint(a, pltpu.HBM) for a in list(srcs) + list(lands)], *extra)
    return res[0], res[1], res[2:2 + n], res[2 + n:2 + 2 * n], res[-1]


def _split_wait(name, send_sems, recv_sems, srcs, lands, plan, after):
    n = len(srcs)

    def body(*refs):
        src, land = refs[:n], refs[n:2 * n]
        s_sems, r_sems = refs[2 * n], refs[2 * n + 1]
        for cp in _split_copies(plan, src, land, s_sems, r_sems):
            cp.wait_send()
            cp.wait_recv()

    res = pl.pallas_call(
        body, name=name, out_shape=[pltpu.HBM(a.shape, a.dtype) for a in list(srcs) + list(lands)],
        in_specs=[_HBM] * (2 * n) + [_SEM, _SEM, pl.BlockSpec(memory_space=pl.ANY)], out_specs=[_HBM] * (2 * n),
        input_output_aliases={i: i for i in range(2 * n)},
        compiler_params=pltpu.CompilerParams(has_side_effects=_EFFECT),
    )(*srcs, *lands, send_sems, recv_sems, after)
    return res[:n], res[n:]


def _exchange_start(name, bufs, plan, after):
    n = len(bufs)

    def body(*refs):
        buf = refs[:n]
        for cp in _split_copies(plan, buf, buf, refs[n + 1], refs[n + 2]):
            cp.start()
        refs[-1][...] = jnp.zeros_like(refs[-1])

    out_shape = ([pltpu.SemaphoreType.DMA((plan.copies * n,)), pltpu.SemaphoreType.DMA((plan.copies * n,))]
                 + [pltpu.HBM(a.shape, a.dtype) for a in bufs] + [jax.ShapeDtypeStruct((8, LANES), F32)])
    res = pl.pallas_call(
        body, name=name, out_shape=out_shape, in_specs=[_HBM] * n + [pl.BlockSpec(memory_space=pl.ANY)],
        out_specs=[_SEM, _SEM] + [_HBM] * n + [pl.BlockSpec(memory_space=pltpu.VMEM)],
        input_output_aliases={i: 2 + i for i in range(n)},
        compiler_params=pltpu.CompilerParams(has_side_effects=_EFFECT),
    )(*[pltpu.with_memory_space_constraint(a, pltpu.HBM) for a in bufs], after)
    return res[0], res[1], res[2:2 + n], res[-1]


def _exchange_wait(name, send_sems, recv_sems, bufs, plan, after):
    n = len(bufs)

    def body(*refs):
        buf = refs[:n]
        for cp in _split_copies(plan, buf, buf, refs[n], refs[n + 1]):
            cp.wait_send()
            cp.wait_recv()

    return pl.pallas_call(
        body, name=name, out_shape=[pltpu.HBM(a.shape, a.dtype) for a in bufs],
        in_specs=[_HBM] * n + [_SEM, _SEM, pl.BlockSpec(memory_space=pl.ANY)], out_specs=[_HBM] * n,
        input_output_aliases={i: i for i in range(n)},
        compiler_params=pltpu.CompilerParams(has_side_effects=_EFFECT),
    )(*bufs, send_sems, recv_sems, after)


def _forward_plan(src, land, me):
    x, y, c = me
    rows = _half(src.shape[1], c)
    return [(src.at[2 * px + py, rows], land.at[2 * px + py, rows], (x, y, 1 - c)) for px, py in _other_chips(x, y)]


def _share_plan(src, land, me):
    x, y, c = me
    rows = _half(src.shape[0], c)
    return [(src.at[rows], land.at[rows], (x, y, 1 - c))]


_forward_plan.copies = N_CHIPS - 1
_share_plan.copies = 1


def _half(ref_rows, c):
    half = ref_rows // 2
    return pl.ds(c * half, half)


def _gather_plan(src, land, me):
    x, y, c = me
    rows = _half(src.shape[0], c)
    return [(src.at[rows], land.at[2 * x + y, rows], (px, py, c)) for px, py in _other_chips(x, y)]


def _scatter_plan(src, land, me):
    x, y, c = me
    return [(src.at[2 * px + py], land.at[2 * x + y], (px, py, c)) for px, py in _other_chips(x, y)]


def _swap_plan(src, land, me):
    x, y, c = me
    return [(src.at[:, _half(src.shape[1], 1 - c)], land, (x, y, 1 - c))]


_gather_plan.copies = N_PEERS
_scatter_plan.copies = N_PEERS
_swap_plan.copies = 1


def _all_gather_small(v):
    def body(v_ref, out_ref, send_sems, recv_sems, local_sem):
        x, y, c = _me()
        mine = pltpu.make_async_copy(v_ref, out_ref.at[4 * x + 2 * y + c], local_sem)
        mine.start()
        copies = []
        for k in range(1, N_DEV):
            peer = (x ^ ((k >> 2) & 1), y ^ ((k >> 1) & 1), c ^ (k & 1))
            cp = pltpu.make_async_remote_copy(
                src_ref=v_ref, dst_ref=out_ref.at[4 * x + 2 * y + c], send_sem=send_sems.at[k - 1],
                recv_sem=recv_sems.at[k - 1], device_id=peer, device_id_type=MESH)
            cp.start()
            copies.append(cp)
        for cp in copies:
            cp.wait_recv()
        for cp in copies:
            cp.wait_send()
        mine.wait()

    return pl.pallas_call(
        body, name="all_gather_small", in_specs=_hbm_specs(1), out_specs=pl.BlockSpec(memory_space=pl.ANY),
        out_shape=jax.ShapeDtypeStruct((N_DEV,) + v.shape, v.dtype),
        scratch_shapes=[pltpu.SemaphoreType.DMA((N_DEV - 1,)), pltpu.SemaphoreType.DMA((N_DEV - 1,)),
                        pltpu.SemaphoreType.DMA],
    )(v)


def _pair_sum(name, own, recv, chip, core):
    nch, half, cdim = recv.shape
    tm = _pick(half, max(8, (512 * 1024) // cdim // 16 * 16), 16)
    nt = half // tm

    def body(c_ref, k_ref, a_ref, b_ref, s16_ref):
        s16_ref[...] = (a_ref[...] + b_ref[...]).astype(BF16)

    other = lambda j, c: (c[0] + 1 + j) % nch
    spec = pl.BlockSpec((None, tm, cdim), lambda j, i, c, k: (other(j, c), i, 0))
    return pl.pallas_call(
        body, name=name,
        grid_spec=pltpu.PrefetchScalarGridSpec(
            num_scalar_prefetch=2, grid=(nch - 1, nt),
            in_specs=[pl.BlockSpec((None, tm, cdim), lambda j, i, c, k: (other(j, c), k[0] * nt + i, 0)), spec],
            out_specs=spec),
        out_shape=jax.ShapeDtypeStruct((nch, half, cdim), BF16),
        compiler_params=_params(("parallel", "parallel")),
    )(chip, core, own, recv)


def _chip_sum(name, own, recv, landed, chip, core):
    nch, half, cdim = recv.shape
    tm = _pick(half, max(8, (512 * 1024) // cdim // 16 * 16), 16)
    nt = half // tm

    def body(c_ref, k_ref, own_ref, recv_ref, *rest):
        landed_refs, out_ref = rest[:nch], rest[-1]
        me = c_ref[0]
        mine = own_ref[...] + recv_ref[...]
        acc = None
        for j in range(nch):
            term = jnp.where(me == j, mine, landed_refs[j][...].astype(F32))
            acc = term if acc is None else acc + term
        out_ref[...] = acc

    landed_specs = [pl.BlockSpec((None, tm, cdim), functools.partial(lambda i, c, k, j: (j, i, 0), j=j))
                    for j in range(nch)]
    return pl.pallas_call(
        body, name=name,
        grid_spec=pltpu.PrefetchScalarGridSpec(
            num_scalar_prefetch=2, grid=(nt,),
            in_specs=[pl.BlockSpec((None, tm, cdim), lambda i, c, k: (c[0], k[0] * nt + i, 0)),
                      pl.BlockSpec((None, tm, cdim), lambda i, c, k: (c[0], i, 0))] + landed_specs,
            out_specs=pl.BlockSpec((tm, cdim), lambda i, c, k: (k[0] * nt + i, 0))),
        out_shape=jax.ShapeDtypeStruct((2 * half, cdim), F32),
        compiler_params=_params(("parallel",)),
    )(chip, core, own, recv, *([landed] * nch))


def _sum_devices(gathered):
    _, r, cdim = gathered.shape

    def body(g_ref, o_ref):
        acc = g_ref[0]
        for k in range(1, N_DEV):
            acc = acc + g_ref[k]
        o_ref[...] = acc

    return pl.pallas_call(
        body, name="sum_devices", out_shape=jax.ShapeDtypeStruct((r, cdim), F32),
        compiler_params=_params(),
    )(gathered)


class _Layout:
    def __init__(self, d):
        self.d = d
        self.fw = d // 2
        self.fd = self.fw // FOX_HEADS
        self.gk = d // 2
        self.gv = d
        self.dk = self.gk // GLA_HEADS
        self.dv = self.gv // GLA_HEADS
        self.c_fq = 0
        self.c_gq = self.fw
        self.c_gv = self.c_gq + self.gk
        self.c_gr = self.c_gv + self.gv
        self.c_fkv = self.c_gr + self.gv
        self.c_gates = self.c_fkv + 2 * self.fw
        self.c_gk = self.c_gates + 2 * d
        self.c_small = self.c_gk + self.gk
        self.n_main = self.c_small
        self.n_p = self.c_small + LANES
        self.o_fk = self.fw
        self.o_fv = 2 * self.fw
        self.o_ff = 3 * self.fw
        self.o_gq = self.o_ff + FOX_HEADS
        self.o_gk = self.o_gq + self.gk
        self.o_gv = self.o_gk + self.gk
        self.o_gr = self.o_gv + self.gv
        self.o_ga = self.o_gr + self.gv
        self.o_gf = self.o_ga + GLA_RANK
        self.o_gg = self.o_gf + d
        self.n_orig = self.o_gg + d

    def to_p(self, shards):
        per = self.n_orig // N_CHIPS
        ranges = [(0, self.fw), (self.o_gq, self.gk), (self.o_gv, self.gv), (self.o_gr, self.gv)]
        for h in range(FOX_HEADS):
            ranges += [(self.o_fk + h * self.fd, self.fd), (self.o_fv + h * self.fd, self.fd)]
        ranges += [(self.o_gf, 2 * self.d), (self.o_gk, self.gk), (self.o_ff, FOX_HEADS), (self.o_ga, GLA_RANK)]
        pieces = []
        for a, width in ranges:
            for j in range(a // per, (a + width - 1) // per + 1):
                lo, hi = max(a, j * per), min(a + width, (j + 1) * per)
                pieces.append(shards[j][:, lo - j * per:hi - j * per])
        pieces.append(jnp.zeros((shards.shape[1], LANES - FOX_HEADS - GLA_RANK), shards.dtype))
        return jnp.concatenate(pieces, axis=1)

    def from_segments(self, seg):
        per = self.n_orig // N_CHIPS
        fd = self.fd
        atoms = [("fq", 0, self.fw)]
        atoms += [("fkv", 2 * h * fd, fd) for h in range(FOX_HEADS)]
        atoms += [("fkv", (2 * h + 1) * fd, fd) for h in range(FOX_HEADS)]
        atoms += [("small", 0, FOX_HEADS), ("gq", 0, self.gk), ("gk", 0, self.gk), ("gv", 0, self.gv),
                  ("gr", 0, self.gv), ("small", FOX_HEADS, GLA_RANK), ("gates", 0, 2 * self.d)]
        shards = [[] for _ in range(N_CHIPS)]
        pos = 0
        for name, c0, width in atoms:
            for j in range(pos // per, (pos + width - 1) // per + 1):
                lo, hi = max(pos, j * per), min(pos + width, (j + 1) * per)
                shards[j].append(seg[name][:, c0 + lo - pos:c0 + hi - pos])
            pos += width
        assert pos == self.n_orig
        return jnp.stack([jnp.concatenate(s, axis=1) for s in shards])


def _layer_fwd(lay, h, p, t, hooks=None):
    hooks = hooks or {}
    d = lay.d
    xn, xn_t = _rms_fwd_t("rms_mix_fwd", h, p["norm_mix_g"], t, d)
    proj = _mm("mm_proj", xn, p["w_in"], mode="nn", m=t, n=lay.n_main, k=d, out_dtypes=(BF16,))
    small = _mm("mm_small", xn, p["w_in"], mode="nn", m=t, n=LANES, k=d, b_c0=lay.c_small)
    cs = _fox_gate_fwd(small, p["b_forget_p"], t)
    ct = cs[:, :FOX_HEADS].T
    c_col, c_row = ct[:, :, None], ct[:, None, :]
    o_fox, lse = _fox_fwd(proj, c_col, c_row, t, lay.fd, lay.c_fq, lay.c_fkv)
    if "mixers" in hooks:
        hooks["mixers"](o_fox)
    glog = _gla_gate_fwd(small, p["w_alpha_p"], p["b_alpha"], t, lay.gk)
    o_raw, s_prev = _gla_fwd(proj, glog, t, lay.dk, lay.dv, lay.c_gq, lay.c_gk, lay.c_gv)
    o_gla = _gla_post_fwd(o_raw, proj, p["gla_norm_g"], t, lay.dv, lay.c_gr)
    if "late" in hooks:
        p.update(hooks["late"](o_gla))
    a_fox = _mm("mm_o_fox", o_fox, p["w_o_fox"], mode="nn", m=t, n=d, k=lay.fw, b_shards=N_CHIPS)
    a_gla = _mm("mm_o_gla", o_gla, p["w_o_gla"], mode="nn", m=t, n=d, k=lay.gv)
    y = _merge_fwd(a_fox, a_gla, proj, lay.c_gates, t, d)
    h1 = _mm("mm_out", y, p["w_out"], mode="nn", m=t, n=d, k=d, extras=[h], epilogue=lambda acc, res: (res + acc,))
    xn2, xn2_t = _rms_fwd_t("rms_mlp_fwd", h1, p["norm_mlp_g"], t, d)
    u, act = _mm("mm_ff1", xn2, p["w_ff1"], mode="nn", m=t, n=4 * d, k=d, out_dtypes=(BF16, BF16), b_shards=N_CHIPS,
                 epilogue=lambda acc: (acc, jnp.square(jnp.maximum(acc, 0.0))))
    if "mlp" in hooks:
        hooks["mlp"](act)
    h2 = _mm("mm_ff2", act, p["w_ff2"], mode="nn", m=t, n=d, k=4 * d, extras=[h1],
             epilogue=lambda acc, res: (res + acc,))
    saved = dict(h=h, xn_t=xn_t, proj=proj, small=small, c_col=c_col, c_row=c_row, o_fox=o_fox, lse=lse, glog=glog,
                 o_raw=o_raw, s_prev=s_prev, o_gla=o_gla, a_fox=a_fox, a_gla=a_gla, y=y, h1=h1, xn2_t=xn2_t, u=u, act=act)
    return h2, saved


def _layer_bwd(lay, dh2, p, s, t, gates=None, first=False):
    d = lay.d
    g = {}

    def gated(gain, point):
        return gain + gates[point](g) if gates and point in gates else gain
    du = _mm("mm_dact", dh2, p["w_ff2"], mode="nt", m=t, n=4 * d, k=d, extras=[s["u"]], out_dtypes=(BF16,),
             epilogue=lambda acc, u: (acc * (2.0 * jnp.maximum(u.astype(F32), 0.0)),))
    g["w_ff2"] = _mm("mm_dw_ff2", s["act"], dh2, mode="tn", m=4 * d, n=d, k=t)
    g["w_ff1"] = _mm("mm_dw_ff1", s["xn2_t"], du, mode="nn", m=d, n=4 * d, k=t, tk=t, out_shards=N_CHIPS)
    dxn2 = _mm("mm_dxn2", du, p["w_ff1"], mode="nt", m=t, n=d, k=4 * d, b_shards=N_CHIPS)
    dh1, g["norm_mlp_g"] = _rms_bwd("rms_mlp_bwd", s["h1"], gated(p["norm_mlp_g"], "mlp"), dxn2, dh2, t, d)
    dy = _mm("mm_dy", dh1, p["w_out"], mode="nt", m=t, n=d, k=d)
    g["w_out"] = _mm("mm_dw_out", s["y"], dh1, mode="tn", m=d, n=d, k=t)
    da_fox, da_gla, dgates = _merge_bwd(dy, s["a_fox"], s["a_gla"], s["proj"], lay.c_gates, t, d)
    g["w_o_fox"] = _mm("mm_dw_o_fox", s["o_fox"], da_fox, mode="tn", m=lay.fw, n=d, k=t, out_shards=N_CHIPS)
    do_fox = _mm("mm_do_fox", da_fox, p["w_o_fox"], mode="nt", m=t, n=lay.fw, k=d, b_shards=N_CHIPS)
    g["w_o_gla"] = _mm("mm_dw_o_gla", s["o_gla"], da_gla, mode="tn", m=lay.gv, n=d, k=t)
    do_gla = _mm("mm_do_gla", da_gla, p["w_o_gla"], mode="nt", m=t, n=lay.gv, k=d)
    do_raw, dgr, g["gla_norm_g"] = _gla_post_bwd(s["o_raw"], s["proj"], gated(p["gla_norm_g"], "out"), do_gla, t,
                                                 lay.dv, lay.c_gr)
    dgq, dgk, dgv, dglog = _gla_bwd(s["proj"], s["glog"], s["s_prev"], do_raw, t, lay.dk, lay.dv,
                                    lay.c_gq, lay.c_gk, lay.c_gv)
    dz, g["b_alpha"] = _gla_gate_bwd(dglog, s["small"], p["w_alpha_p"], p["b_alpha"], t, lay.gk)
    g["w_alpha_p"] = _mm("mm_dw_alpha", s["small"], dz, mode="tn", m=LANES, n=lay.gk, k=t)
    dga = _mm("mm_dga", dz, p["w_alpha_p"], mode="nt", m=t, n=LANES, k=lay.gk)
    delta = _fox_delta(s["o_fox"], do_fox, t, lay.fd)
    dfq, dfkv, dc, dr = _fox_bwd(s["proj"], s["c_col"], s["c_row"], s["lse"], delta, do_fox, t, lay.fd,
                                 lay.c_fq, lay.c_fkv)
    dc_p = jnp.pad((dc[:, 0, :] + dr[:, :, 0]).T, ((0, 0), (0, LANES - FOX_HEADS)))
    dsmall, g["b_forget_p"] = _fox_gate_bwd(dc_p, s["small"], p["b_forget_p"], dga, t)
    segs = [("fq", dfq, lay.c_fq), ("gq", dgq, lay.c_gq), ("gv", dgv, lay.c_gv), ("gr", dgr, lay.c_gr),
            ("fkv", dfkv, lay.c_fkv), ("gates", dgates, lay.c_gates), ("gk", dgk, lay.c_gk),
            ("small", dsmall, lay.c_small)]
    dw_in = {nm: _mm("mm_dw_in_" + nm, s["xn_t"], dseg, mode="nn", m=d, n=dseg.shape[1], k=t, tk=t)
             for nm, dseg, _ in segs}
    g["w_in"] = lay.from_segments(dw_in)
    dxn = _mm("mm_dxn_gates", dgates, p["w_in"], mode="nt", m=t, n=d, k=2 * d, b_c0=lay.c_gates)
    dxn = _mm("mm_dxn_small", dsmall, p["w_in"], mode="nt", m=t, n=d, k=LANES, b_c0=lay.c_small, extras=[dxn],
              epilogue=lambda acc, prev: (prev + acc,))
    for nm, (a1, c1), (a2, c2) in (("q", (dfq, lay.c_fq), (dgq, lay.c_gq)), ("v", (dgv, lay.c_gv), (dgr, lay.c_gr)),
                                   ("k", (dfkv, lay.c_fkv), (dgk, lay.c_gk))):
        dxn = _mm_nt2("mm_dxn_" + nm, a1, c1, a2, c2, p["w_in"], dxn, m=t, n=d)
    if first:
        dx, head, g["norm_mix_g"] = _rms_bwd_x("rms_mix_bwd_x", s["h"], gated(p["norm_mix_g"], "in"), dxn, dh1, t, d)
        return (dx, head), g
    dh, g["norm_mix_g"] = _rms_bwd("rms_mix_bwd", s["h"], gated(p["norm_mix_g"], "in"), dxn, dh1, t, d)
    return dh, g


def _sequence_step(x, target, meta, layers, final_g):
    seq, d = x.shape
    t = seq + ROW0
    lay = _Layout(d)
    h = jnp.pad(x, ((ROW0, 0), (0, 0))).at[PAD:ROW0].set(meta)
    target_p = jnp.pad(target, ((ROW0, 0), (0, 0)))
    saved = []
    for p in layers:
        h, s = _layer_fwd(lay, h, p, t)
        saved.append(s)
    dh, dg_final, loss_part = _loss_head(h, final_g, target_p, t, d)
    grads = [None] * len(layers)
    for l in reversed(range(len(layers))):
        dh, grads[l] = _layer_bwd(lay, dh, layers[l], saved[l], t)
    return loss_part, dh[ROW0:], dh[PAD:ROW0], grads, dg_final


_SMALL_ROWS = 48


def _pack_small(d, meta, mix, gla, mlp, final, b_alpha, b_forget, w_alpha2):
    rows = [meta.reshape(N_META, d), mix.reshape(DEPTH, d), gla.reshape(DEPTH, d), mlp.reshape(DEPTH, d),
            final.reshape(1, d), b_alpha.reshape(1, d),
            jnp.pad(b_forget.reshape(1, DEPTH * FOX_HEADS), ((0, 0), (0, d - DEPTH * FOX_HEADS))),
            jnp.zeros((7, d), F32), w_alpha2.reshape(GLA_RANK, d)]
    return jnp.concatenate(rows, axis=0)


def _unpack_small(d, packed):
    return dict(meta=packed[:N_META], norm_mix_g=packed[16:18], gla_norm_g=packed[18:20], norm_mlp_g=packed[20:22],
                final_norm_g=packed[22], b_alpha=packed[23].reshape(DEPTH, d // 2),
                b_forget=packed[24, :DEPTH * FOX_HEADS].reshape(DEPTH, FOX_HEADS),
                w_alpha2=packed[32:48].reshape(DEPTH, GLA_RANK, d // 2))


_BIG = ("w_in", "w_o_fox", "w_o_gla", "w_out", "w_ff1", "w_ff2")
_COL_SHARDED = ("w_in", "w_o_fox", "w_ff1")


def _full_matrix(name, gathered_l):
    nch, r, c = gathered_l.shape
    if name in _COL_SHARDED:
        return gathered_l.transpose(1, 0, 2).reshape(r, nch * c)
    return gathered_l.reshape(nch * r, c)


def _shard_major(name, full):
    r, c = full.shape
    if name in _COL_SHARDED:
        return full.reshape(r, N_CHIPS, c // N_CHIPS).transpose(1, 0, 2)
    return full.reshape(N_CHIPS, r // N_CHIPS, c)


def kernel(x, meta_tokens, norm_mix_g, w_in, b_forget, w_alpha2, b_alpha, gla_norm_g, w_o_fox, w_o_gla, w_out, norm_mlp_g, w_ff1, w_ff2, final_norm_g, loss_target, m_meta_tokens, m_norm_mix_g, m_w_in, m_b_forget, m_w_alpha2, m_b_alpha, m_gla_norm_g, m_w_o_fox, m_w_o_gla, m_w_out, m_norm_mlp_g, m_w_ff1, m_w_ff2, m_final_norm_g, v_meta_tokens, v_norm_mix_g, v_w_in, v_b_forget, v_w_alpha2, v_b_alpha, v_gla_norm_g, v_w_o_fox, v_w_o_gla, v_w_out, v_norm_mlp_g, v_w_ff1, v_w_ff2, v_final_norm_g):
    d = x.shape[2]
    lay = _Layout(d)
    xi, yi, ci = lax.axis_index("x"), lax.axis_index("y"), lax.axis_index("c")
    chip = (2 * xi + yi).astype(jnp.int32)
    w = dict(w_in=w_in, w_alpha2=w_alpha2, w_o_fox=w_o_fox, w_o_gla=w_o_gla, w_out=w_out, w_ff1=w_ff1, w_ff2=w_ff2)
    m = dict(w_in=m_w_in, w_alpha2=m_w_alpha2, w_o_fox=m_w_o_fox, w_o_gla=m_w_o_gla, w_out=m_w_out, w_ff1=m_w_ff1,
             w_ff2=m_w_ff2)
    v = dict(w_in=v_w_in, w_alpha2=v_w_alpha2, w_o_fox=v_w_o_fox, w_o_gla=v_w_o_gla, w_out=v_w_out, w_ff1=v_w_ff1,
             w_ff2=v_w_ff2)

    seq = x.shape[1]
    t = seq + ROW0
    core_idx = ci.astype(jnp.int32)[None]
    chip_idx = chip[None]

    cols = d // N_CHIPS
    small_w = jnp.concatenate([meta_tokens, w_alpha2.reshape(-1, cols)], axis=0)
    small_raw = _all_gather_small(small_w)
    small_all = small_raw[0::2]
    alpha_full = small_all[:, N_META:].reshape(N_CHIPS, DEPTH, GLA_RANK, lay.gk // N_CHIPS)
    alpha_full = alpha_full.transpose(1, 2, 0, 3).reshape(DEPTH, GLA_RANK, lay.gk)
    groups = [(0, ("w_in",)), (0, _BIG[1:]), (1, _BIG)]
    started, after = [], small_raw
    for gi, (l, names) in enumerate(groups):
        own16 = [w[n][l].astype(BF16) for n in names]
        lands = [lax.empty((N_CHIPS,) + o.shape, BF16) for o in own16]
        started.append(_split_start("gather_start_%d" % gi, own16, lands, _gather_plan, after=after))
        after = started[gi][4]
    meta_full = small_all[:, :N_META].transpose(1, 0, 2).reshape(N_META, d) + after[0, 0]

    passing = {}

    def arrive(gi, after):
        send_sems, recv_sems, srcs, lands, _ = started[gi]
        srcs, lands = _split_wait("gather_wait_%d" % gi, send_sems, recv_sems, srcs, lands, _gather_plan, after)
        passing[gi] = (srcs, _exchange_start("gather_pass_%d" % gi, lands, _forward_plan, after=srcs[0]))

    def gathered(gi, after):
        if gi not in passing:
            arrive(gi, after)
        srcs, (send_sems, recv_sems, lands, _) = passing[gi]
        lands = _exchange_wait("gather_pass_wait_%d" % gi, send_sems, recv_sems, lands, _forward_plan, after)
        return {n: lax.dynamic_update_slice(g, o[None], (chip, 0, 0)) for n, g, o in zip(groups[gi][1], lands, srcs)}

    def early_weights(l, gl):
        w_alpha_p = jnp.zeros((LANES, lay.gk), BF16).at[FOX_HEADS:FOX_HEADS + GLA_RANK].set(
            alpha_full[l].astype(BF16))
        return dict(
            w_in=lay.to_p(gl["w_in"]), w_alpha_p=w_alpha_p,
            norm_mix_g=norm_mix_g[l][None], norm_mlp_g=norm_mlp_g[l][None], gla_norm_g=gla_norm_g[l][None],
            b_alpha=b_alpha[l][None],
            b_forget_p=jnp.pad(b_forget[l][None], ((0, 0), (0, LANES - FOX_HEADS))))

    def late_weights(gl):
        return dict(w_o_fox=gl["w_o_fox"], w_o_gla=_full_matrix("w_o_gla", gl["w_o_gla"]),
                    w_out=_full_matrix("w_out", gl["w_out"]), w_ff1=gl["w_ff1"],
                    w_ff2=_full_matrix("w_ff2", gl["w_ff2"]))

    h = jnp.pad(x[0], ((ROW0, 0), (0, 0))).at[PAD:ROW0].set(meta_full)
    layers, saved = [], []
    layers.append(early_weights(0, gathered(0, after=h)))
    h, s = _layer_fwd(lay, h, layers[0], t, hooks=dict(
        mixers=lambda after: arrive(1, after), late=lambda after: late_weights(gathered(1, after)),
        mlp=lambda after: arrive(2, after)))
    saved.append(s)
    gl = gathered(2, after=h)
    layers.append({**early_weights(1, gl), **late_weights(gl)})
    h, s = _layer_fwd(lay, h, layers[1], t)
    saved.append(s)
    dh, dg_final, loss_part = _loss_head(h, final_norm_g[None], jnp.pad(loss_target[0], ((ROW0, 0), (0, 0))), t, d)
    loss = lax.psum(loss_part[0, 0], ("x", "y", "c"))

    def partial_of(g, n):
        return g[n] if n in ("w_ff1", "w_o_fox", "w_in") else _shard_major(n, g[n])

    scatter_groups = dict(mlp=("w_ff1", "w_ff2"), out=("w_o_fox", "w_o_gla", "w_out"))
    scatter_groups["in"] = ("w_in",)
    swapping, scattered = [], {}

    def start_swap(l, grp, g, after=None):
        parts = [partial_of(g, n) for n in scatter_groups[grp]]
        lands = [lax.empty((p_.shape[0], p_.shape[1] // 2, p_.shape[2]), F32) for p_ in parts]
        started_swap = _split_start("swap_start_%d_%s" % (l, grp), parts, lands, _swap_plan, after=after)
        swapping.append((l, grp, started_swap))
        return started_swap[4]

    def start_scatter(after):
        l, grp, (send_sems, recv_sems, srcs, lands, _) = swapping.pop(0)
        names = scatter_groups[grp]
        tag = "%d_%s" % (l, grp)
        parts, from_sibling = _split_wait("swap_wait_" + tag, send_sems, recv_sems, srcs, lands, _swap_plan, after)
        sums = [_pair_sum("pair_sum_%d_%s" % (l, n), p_, r_, chip_idx, core_idx)
                for n, p_, r_ in zip(names, parts, from_sibling)]
        lands = [lax.empty(s16.shape, BF16) for s16 in sums]
        send_sems, recv_sems, srcs, lands, token = _split_start("scatter_start_" + tag, sums, lands, _scatter_plan)
        scattered[l, grp] = (send_sems, recv_sems, srcs, lands, parts, from_sibling)
        return token

    def gate(l, grp, g):
        token = start_swap(l, grp, g)
        if len(swapping) > 1:
            token = start_scatter(after=token)
        return token[0, 0]

    def gates_for(l, points):
        return {grp: functools.partial(gate, l, grp) for grp in points}

    grads = [None] * DEPTH
    dh, grads[1] = _layer_bwd(lay, dh, layers[1], saved[1], t, gates=gates_for(1, ("mlp", "out", "in")))
    (grad_x, head), grads[0] = _layer_bwd(lay, dh, layers[0], saved[0], t, gates=gates_for(0, ("mlp", "out")),
                                          first=True)
    d_meta = head[PAD:ROW0]

    stack = lambda key: jnp.concatenate([grads[l][key] for l in range(DEPTH)], axis=0)
    b_forget_g = jnp.concatenate([grads[l]["b_forget_p"][:, :FOX_HEADS] for l in range(DEPTH)], axis=0)
    alpha_g = jnp.stack([grads[l]["w_alpha_p"][FOX_HEADS:FOX_HEADS + GLA_RANK] for l in range(DEPTH)])
    packed = _pack_small(d, d_meta, stack("norm_mix_g"), stack("gla_norm_g"), stack("norm_mlp_g"), dg_final,
                         stack("b_alpha"), b_forget_g, alpha_g)
    small_g = _unpack_small(d, _sum_devices(_all_gather_small(packed)))
    small_g["meta"] = lax.dynamic_slice_in_dim(small_g["meta"], chip * (d // N_CHIPS), d // N_CHIPS, axis=1)
    alpha_shard = lax.dynamic_slice_in_dim(small_g["w_alpha2"], chip * (lay.gk // N_CHIPS), lay.gk // N_CHIPS, axis=2)

    after = start_swap(0, "in", grads[0], after=small_g["final_norm_g"])
    while swapping:
        after = start_scatter(after)
    names = [n for grp in ("mlp", "out", "in") for n in scatter_groups[grp]]
    sharing = {}
    for l in reversed(range(DEPTH)):
        layer_g = []
        for grp in ("mlp", "out", "in"):
            send_sems, recv_sems, srcs, lands, parts, from_sibling = scattered[l, grp]
            _, lands = _split_wait("scatter_wait_%d_%s" % (l, grp), send_sems, recv_sems, srcs, lands, _scatter_plan,
                                 after)
            layer_g += [_chip_sum("chip_sum_%d_%s" % (l, n), p_, r_, landed, chip_idx, core_idx)
                        for n, p_, r_, landed in zip(scatter_groups[grp], parts, from_sibling, lands)]
        sharing[l] = _exchange_start("share_start_%d" % l, layer_g, _share_plan, after=layer_g[-1])
        after = sharing[l][3]
    outs = {n: None for n in _BIG}
    for l in reversed(range(DEPTH)):
        send_sems, recv_sems, layer_g, _ = sharing[l]
        layer_g = _exchange_wait("share_wait_%d" % l, send_sems, recv_sems, layer_g, _share_plan, after)
        for n, g in zip(names, layer_g):
            outs[n] = _adamw("adamw_%d_%s" % (l, n), w[n], g, m[n], v[n], layer=l, into=outs[n])
        after = outs[names[-1]][0]
    out_g, out_d, out_m, out_v = {}, {}, {}, {}
    for n in _BIG:
        out_g[n], out_d[n], out_m[n], out_v[n] = outs[n]
    out_g["w_alpha2"], out_d["w_alpha2"], out_m["w_alpha2"], out_v["w_alpha2"] = _adamw(
        "adamw_w_alpha2", w["w_alpha2"], alpha_shard, m["w_alpha2"], v["w_alpha2"])
    sm_w = dict(meta_tokens=meta_tokens, norm_mix_g=norm_mix_g, b_forget=b_forget, b_alpha=b_alpha,
                gla_norm_g=gla_norm_g, norm_mlp_g=norm_mlp_g, final_norm_g=final_norm_g)
    sm_m = dict(meta_tokens=m_meta_tokens, norm_mix_g=m_norm_mix_g, b_forget=m_b_forget, b_alpha=m_b_alpha,
                gla_norm_g=m_gla_norm_g, norm_mlp_g=m_norm_mlp_g, final_norm_g=m_final_norm_g)
    sm_v = dict(meta_tokens=v_meta_tokens, norm_mix_g=v_norm_mix_g, b_forget=v_b_forget, b_alpha=v_b_alpha,
                gla_norm_g=v_gla_norm_g, norm_mlp_g=v_norm_mlp_g, final_norm_g=v_final_norm_g)
    sm_g = dict(meta_tokens=small_g["meta"], norm_mix_g=small_g["norm_mix_g"], b_forget=small_g["b_forget"],
                b_alpha=small_g["b_alpha"], gla_norm_g=small_g["gla_norm_g"], norm_mlp_g=small_g["norm_mlp_g"],
                final_norm_g=small_g["final_norm_g"])
    names_small = list(sm_w)
    sizes = [sm_w[n].size for n in names_small]
    width = 512
    total = -(-sum(sizes) // (8 * width)) * (8 * width)

    def pack_flat(dct, fill):
        flat = jnp.concatenate([dct[n].reshape(-1) for n in names_small])
        return jnp.pad(flat, (0, total - flat.shape[0]), constant_values=fill).reshape(1, -1, width)

    res = _adamw("adamw_small", pack_flat(sm_w, 0.0), pack_flat(sm_g, 0.0), pack_flat(sm_m, 0.0), pack_flat(sm_v, 1.0))
    offs = [0]
    for sz in sizes:
        offs.append(offs[-1] + sz)
    for i, n in enumerate(names_small):
        out_g[n] = sm_g[n].reshape(sm_w[n].shape)
        out_d[n], out_m[n], out_v[n] = [r.reshape(-1)[offs[i]:offs[i + 1]].reshape(sm_w[n].shape) for r in res[1:]]

    order = ["meta_tokens", "norm_mix_g", "w_in", "b_forget", "w_alpha2", "b_alpha", "gla_norm_g", "w_o_fox",
             "w_o_gla", "w_out", "norm_mlp_g", "w_ff1", "w_ff2", "final_norm_g"]
    return (loss, grad_x[None], *[out_g[n] for n in order], *[out_d[n] for n in order],
            *[out_m[n] for n in order], *[out_v[n] for n in order])
```

```python
import functools

import numpy as np

import jax
import jax.numpy as jnp
from jax import lax
from jax.experimental import pallas as pl
from jax.experimental.pallas import tpu as pltpu

F32 = jnp.float32
BF16 = jnp.bfloat16

N_META = 16
PAD = 112
ROW0 = PAD + N_META
EPS = 1e-6
MASK_VALUE = -1e30
FOX_HEADS = 8
FOX_GROUP = 2
GLA_HEADS = 4
GLA_RANK = 16
GLA_TAU = 16.0
GLA_CHUNK = 64
DEPTH = 2
N_CHIPS = 4
N_DEV = 8

ADAM_LR = 0.001
ADAM_B1 = 0.9
ADAM_B2 = 0.999
ADAM_EPS = 1e-08
ADAM_WD = 0.01
ADAM_STEP = 10

LANES = 128
VMEM_LIMIT = 56 * 1024 * 1024
MESH = pl.DeviceIdType.MESH


def _pick(n, target, mult):
    best = None
    for d in range(mult, min(n, target) + 1, mult):
        if n % d == 0:
            best = d
    return n if best is None else best


def _params(sem=None):
    return pltpu.CompilerParams(dimension_semantics=sem, vmem_limit_bytes=VMEM_LIMIT)


def _bf(v):
    return v if v.dtype == BF16 else v.astype(BF16)


def _sigmoid(z):
    return 1.0 / (1.0 + jnp.exp(-z))


def _log_sigmoid(z):
    return jnp.minimum(z, 0.0) - jnp.log(1.0 + jnp.exp(-jnp.abs(z)))


def _split3(v):
    a = v.astype(BF16)
    r = v - a.astype(F32)
    b = r.astype(BF16)
    c = (r - b.astype(F32)).astype(BF16)
    return a, b, c


def _dot(a, b, dims):
    return lax.dot_general(a, b, (dims, ((), ())), preferred_element_type=F32)


NN = ((1,), (0,))
NT = ((1,), (1,))
TN = ((0,), (0,))


def _tri_dot(tri, v, dims=NN):
    a, b, c = _split3(v)
    return _dot(tri, a, dims) + _dot(tri, b, dims) + _dot(tri, c, dims)


def _mm(name, a, b, *, mode, m, n, k, b_c0=0, extras=(), epilogue=None, out_dtypes=(F32,),
        b_shards=1, out_shards=1, tm=1056, tn=1024, tk=2048):
    tm = _pick(m, tm, LANES if mode == "tn" else 16)
    tn = _pick(n // max(b_shards if mode == "nn" else 1, out_shards), tn, LANES)
    if mode == "tn":
        tk = _pick(k, 2112, 16)
    else:
        tk = _pick(k // (b_shards if mode == "nt" else 1), tk, LANES)
    assert b_c0 % (tk if mode == "nt" else tn) == 0 and (b_shards == 1 or b_c0 == 0)
    nk = k // tk
    if mode == "tn":
        a_spec = pl.BlockSpec((tk, tm), lambda i, j, kk: (kk, i))
    else:
        a_spec = pl.BlockSpec((tm, tk), lambda i, j, kk: (i, kk))
    if mode == "nt":
        dims = NT
        if b_shards > 1:
            per = (k // b_shards) // tk
            b_spec = pl.BlockSpec((None, tn, tk), lambda i, j, kk: (kk // per, j, kk % per))
        else:
            b_spec = pl.BlockSpec((tn, tk), lambda i, j, kk: (j, kk + b_c0 // tk))
    else:
        dims = NN if mode == "nn" else TN
        if b_shards > 1:
            per = (n // b_shards) // tn
            b_spec = pl.BlockSpec((None, tk, tn), lambda i, j, kk: (j // per, kk, j % per))
        else:
            b_spec = pl.BlockSpec((tk, tn), lambda i, j, kk: (kk, j + b_c0 // tn))
    ex_specs = [pl.BlockSpec((tm, tn), lambda i, j, kk: (i, j)) for _ in extras]
    if out_shards > 1:
        oper = (n // out_shards) // tn
        out_specs = [pl.BlockSpec((None, tm, tn), lambda i, j, kk: (j // oper, i, j % oper)) for _ in out_dtypes]
        out_shape = [jax.ShapeDtypeStruct((out_shards, m, n // out_shards), dt) for dt in out_dtypes]
    else:
        out_specs = [pl.BlockSpec((tm, tn), lambda i, j, kk: (i, j)) for _ in out_dtypes]
        out_shape = [jax.ShapeDtypeStruct((m, n), dt) for dt in out_dtypes]
    n_ex = len(extras)
    n_out = len(out_dtypes)

    def finish(acc, ex_refs, out_refs):
        vals = (acc,) if epilogue is None else epilogue(acc, *[r[...] for r in ex_refs])
        for r, v in zip(out_refs, vals):
            r[...] = v.astype(r.dtype)

    def body(a_ref, b_ref, *rest):
        ex_refs = rest[:n_ex]
        out_refs = rest[n_ex:n_ex + n_out]
        prod = _dot(_bf(a_ref[...]), _bf(b_ref[...]), dims)
        if nk == 1:
            finish(prod, ex_refs, out_refs)
            return
        acc_ref = rest[n_ex + n_out]
        kk = pl.program_id(2)

        @pl.when(kk == 0)
        def _():
            acc_ref[...] = prod

        @pl.when((kk > 0) & (kk < nk - 1))
        def _():
            acc_ref[...] += prod

        @pl.when(kk == nk - 1)
        def _():
            finish(acc_ref[...] + prod, ex_refs, out_refs)

    outs = pl.pallas_call(
        body,
        name=name,
        grid=(m // tm, n // tn, nk),
        in_specs=[a_spec, b_spec] + ex_specs,
        out_specs=out_specs,
        out_shape=out_shape,
        scratch_shapes=[pltpu.VMEM((tm, tn), F32)] if nk > 1 else [],
        compiler_params=_params(("parallel", "parallel", "arbitrary")),
    )(a, b, *extras)
    return outs[0] if n_out == 1 else outs


def _mm_nt2(name, a1, c1, a2, c2, b, prev, *, m, n, tm=528, tn=1024):
    k1, k2 = a1.shape[1], a2.shape[1]
    assert c1 % k1 == 0 and c2 % k2 == 0
    tm = _pick(m, tm, 16)
    tn = _pick(n, tn, LANES)

    def body(a1_ref, a2_ref, b1_ref, b2_ref, *rest):
        acc = _dot(a1_ref[...], b1_ref[...], NT) + _dot(a2_ref[...], b2_ref[...], NT)
        if prev is not None:
            acc = rest[0][...] + acc
        rest[-1][...] = acc

    tile = pl.BlockSpec((tm, tn), lambda i, j: (i, j))
    in_specs = [pl.BlockSpec((tm, k1), lambda i, j: (i, 0)), pl.BlockSpec((tm, k2), lambda i, j: (i, 0)),
                pl.BlockSpec((tn, k1), lambda i, j: (j, c1 // k1)), pl.BlockSpec((tn, k2), lambda i, j: (j, c2 // k2))]
    args = [a1, a2, b, b]
    if prev is not None:
        in_specs.append(tile)
        args.append(prev)
    return pl.pallas_call(
        body, name=name, grid=(m // tm, n // tn), in_specs=in_specs, out_specs=tile,
        out_shape=jax.ShapeDtypeStruct((m, n), F32),
        compiler_params=_params(("parallel", "parallel")),
    )(*args)


def _ew(name, fn, ins, outs, rows, tm):
    tm = _pick(rows, tm, 16)
    in_specs, args = [], []
    for spec in ins:
        if spec[0] == "tile":
            _, arr, width, c0 = spec
            assert c0 % width == 0
            in_specs.append(pl.BlockSpec((tm, width), functools.partial(lambda i, o: (i, o), o=c0 // width)))
        else:
            arr = spec[1]
            in_specs.append(pl.BlockSpec(arr.shape, lambda i: (0, 0)))
        args.append(arr)
    out_specs, out_shape = [], []
    for kind, dt, width in outs:
        if kind == "tile":
            out_specs.append(pl.BlockSpec((tm, width), lambda i: (i, 0)))
            out_shape.append(jax.ShapeDtypeStruct((rows, width), dt))
        else:
            out_specs.append(pl.BlockSpec((1, width), lambda i: (0, 0)))
            out_shape.append(jax.ShapeDtypeStruct((1, width), dt))
    n_in = len(ins)
    has_acc = any(o[0] == "acc" for o in outs)

    def body(*refs):
        i = pl.program_id(0)
        vals = fn(i * tm, *[r[...] for r in refs[:n_in]])
        for (kind, _, _), r, v in zip(outs, refs[n_in:], vals):
            if kind == "tile":
                r[...] = v.astype(r.dtype)
            else:
                @pl.when(i == 0)
                def _():
                    r[...] = jnp.zeros_like(r)

                r[...] += v.astype(r.dtype)

    res = pl.pallas_call(
        body,
        name=name,
        grid=(rows // tm,),
        in_specs=in_specs,
        out_specs=out_specs,
        out_shape=out_shape,
        compiler_params=_params(("arbitrary",) if has_acc else ("parallel",)),
    )(*args)
    return res[0] if len(outs) == 1 else res


def _row_ids(row0, tm):
    return row0 + lax.broadcasted_iota(jnp.int32, (tm, 1), 0)


def _colsum(v):
    return jnp.sum(v, axis=0, keepdims=True)


def _rms_fwd_t(name, h, g, t, d):
    tm = _pick(t, 384, LANES)

    def body(x_ref, g_ref, y_ref, yt_ref):
        x = x_ref[...]
        y = x * lax.rsqrt(jnp.mean(x * x, axis=-1, keepdims=True) + EPS) * g_ref[...]
        y_ref[...] = y.astype(BF16)
        yt_ref[...] = y.T.astype(BF16)

    return pl.pallas_call(
        body, name=name, grid=(t // tm,),
        in_specs=[pl.BlockSpec((tm, d), lambda i: (i, 0)), pl.BlockSpec((1, d), lambda i: (0, 0))],
        out_specs=[pl.BlockSpec((tm, d), lambda i: (i, 0)), pl.BlockSpec((d, tm), lambda i: (0, i))],
        out_shape=[jax.ShapeDtypeStruct((t, d), BF16), jax.ShapeDtypeStruct((d, t), BF16)],
        compiler_params=_params(("parallel",)),
    )(h, g)


def _rms_bwd(name, h, g, dy, dres, t, d):
    def fn(row0, x, gg, dyv, dr):
        r = lax.rsqrt(jnp.mean(x * x, axis=-1, keepdims=True) + EPS)
        xh = x * r
        dxh = dyv * gg
        dx = r * (dxh - xh * jnp.mean(dxh * xh, axis=-1, keepdims=True))
        out = jnp.where(_row_ids(row0, x.shape[0]) >= PAD, dr + dx, 0.0)
        return out, _colsum(dyv * xh)

    return _ew(name, fn, [("tile", h, d, 0), ("full", g), ("tile", dy, d, 0), ("tile", dres, d, 0)],
               [("tile", F32, d), ("acc", F32, d)], t, 264)


def _rms_bwd_x(name, h, g, dy, dres, t, d):
    tm = ROW0

    def body(x_ref, g_ref, dy_ref, dr_ref, dx_ref, head_ref, dg_ref):
        i = pl.program_id(0)
        x = x_ref[...]
        r = lax.rsqrt(jnp.mean(x * x, axis=-1, keepdims=True) + EPS)
        xh = x * r
        dxh = dy_ref[...] * g_ref[...]
        dx = r * (dxh - xh * jnp.mean(dxh * xh, axis=-1, keepdims=True))
        out = jnp.where(_row_ids(i * tm, tm) >= PAD, dr_ref[...] + dx, 0.0)

        @pl.when(i == 0)
        def _():
            head_ref[...] = out
            dg_ref[...] = jnp.zeros_like(dg_ref)

        dx_ref[...] = out
        dg_ref[...] += _colsum(dy_ref[...] * xh)

    tile = pl.BlockSpec((tm, d), lambda i: (i, 0))
    fixed = lambda shape: pl.BlockSpec(shape, lambda i: (0, 0))
    return pl.pallas_call(
        body, name=name, grid=(t // tm,),
        in_specs=[tile, fixed((1, d)), tile, tile],
        out_specs=[pl.BlockSpec((tm, d), lambda i: (jnp.maximum(i - 1, 0), 0)), fixed((tm, d)), fixed((1, d))],
        out_shape=[jax.ShapeDtypeStruct((t - ROW0, d), F32), jax.ShapeDtypeStruct((ROW0, d), F32),
                   jax.ShapeDtypeStruct((1, d), F32)],
        compiler_params=_params(("arbitrary",)),
    )(h, g, dy, dres)


def _loss_head(h, g, target_p, t, d):
    def fn(row0, x, gg, tgt):
        real = _row_ids(row0, x.shape[0]) >= ROW0
        r = lax.rsqrt(jnp.mean(x * x, axis=-1, keepdims=True) + EPS)
        xh = x * r
        err = jnp.where(real, xh * gg - tgt, 0.0)
        loss_rows = 0.5 * jnp.mean(err * err, axis=-1, keepdims=True)
        dyv = err * (1.0 / d)
        dxh = dyv * gg
        dx = r * (dxh - xh * jnp.mean(dxh * xh, axis=-1, keepdims=True))
        loss_part = jnp.sum(loss_rows, axis=0, keepdims=True) * jnp.ones((1, LANES), F32)
        return jnp.where(real, dx, 0.0), _colsum(dyv * xh), loss_part

    return _ew("loss_head", fn, [("tile", h, d, 0), ("full", g), ("tile", target_p, d, 0)],
               [("tile", F32, d), ("acc", F32, d), ("acc", F32, LANES)], t, 264)


def _merge_fwd(a_fox, a_gla, proj, c_gates, t, d):
    def fn(row0, af, ag, gates):
        gates = gates.astype(F32)
        return (_sigmoid(gates[:, :d]) * af + _sigmoid(gates[:, d:]) * ag,)

    return _ew("merge_fwd", fn, [("tile", a_fox, d, 0), ("tile", a_gla, d, 0), ("tile", proj, 2 * d, c_gates)],
               [("tile", BF16, d)], t, 264)


def _merge_bwd(dy, a_fox, a_gla, proj, c_gates, t, d):
    def fn(row0, dyv, af, ag, gates):
        gates = gates.astype(F32)
        sf = _sigmoid(gates[:, :d])
        sg = _sigmoid(gates[:, d:])
        dgates = jnp.concatenate([dyv * af * sf * (1.0 - sf), dyv * ag * sg * (1.0 - sg)], axis=1)
        return dyv * sf, dyv * sg, dgates

    return _ew("merge_bwd", fn,
               [("tile", dy, d, 0), ("tile", a_fox, d, 0), ("tile", a_gla, d, 0), ("tile", proj, 2 * d, c_gates)],
               [("tile", BF16, d), ("tile", BF16, d), ("tile", BF16, 2 * d)], t, 264)


def _fox_gate_fwd(small, b_forget_p, t):
    tb = _pick(t, 384, LANES)

    def body(s_ref, b_ref, c_ref, carry_ref):
        i = pl.program_id(0)

        @pl.when(i == 0)
        def _():
            carry_ref[...] = jnp.zeros_like(carry_ref)

        logf = _log_sigmoid(s_ref[...] + b_ref[...])
        logf = jnp.where(_row_ids(i * tb, tb) >= PAD, logf, 0.0)
        r = lax.broadcasted_iota(jnp.int32, (tb, tb), 0)
        c = lax.broadcasted_iota(jnp.int32, (tb, tb), 1)
        tri = (c <= r).astype(BF16)
        cs = _tri_dot(tri, logf) + carry_ref[...]
        c_ref[...] = cs
        carry_ref[...] = cs[tb - 1:tb, :]

    return pl.pallas_call(
        body, name="fox_gate_fwd", grid=(t // tb,),
        in_specs=[pl.BlockSpec((tb, LANES), lambda i: (i, 0)), pl.BlockSpec((1, LANES), lambda i: (0, 0))],
        out_specs=pl.BlockSpec((tb, LANES), lambda i: (i, 0)),
        out_shape=jax.ShapeDtypeStruct((t, LANES), F32),
        scratch_shapes=[pltpu.VMEM((1, LANES), F32)],
        compiler_params=_params(("arbitrary",)),
    )(small, b_forget_p)


def _fox_gate_bwd(dc, small, b_forget_p, dga, t):
    tb = _pick(t, 384, LANES)
    nb = t // tb

    def body(dc_ref, s_ref, b_ref, dga_ref, ds_ref, db_ref, carry_ref):
        i = pl.program_id(0)

        @pl.when(i == 0)
        def _():
            carry_ref[...] = jnp.zeros_like(carry_ref)
            db_ref[...] = jnp.zeros_like(db_ref)

        r = lax.broadcasted_iota(jnp.int32, (tb, tb), 0)
        c = lax.broadcasted_iota(jnp.int32, (tb, tb), 1)
        tri = (c >= r).astype(BF16)
        dlogf = _tri_dot(tri, dc_ref[...]) + carry_ref[...]
        carry_ref[...] = dlogf[0:1, :]
        z = s_ref[...] + b_ref[...]
        dff = dlogf * _sigmoid(-z)
        lane = lax.broadcasted_iota(jnp.int32, (tb, LANES), 1)
        keep = (_row_ids((nb - 1 - i) * tb, tb) >= PAD) & (lane < FOX_HEADS)
        dff = jnp.where(keep, dff, 0.0)
        ds_ref[...] = dff + dga_ref[...]
        db_ref[...] += _colsum(dff)

    rev = lambda i: (nb - 1 - i, 0)
    return pl.pallas_call(
        body, name="fox_gate_bwd", grid=(nb,),
        in_specs=[pl.BlockSpec((tb, LANES), rev), pl.BlockSpec((tb, LANES), rev),
                  pl.BlockSpec((1, LANES), lambda i: (0, 0)), pl.BlockSpec((tb, LANES), rev)],
        out_specs=[pl.BlockSpec((tb, LANES), rev), pl.BlockSpec((1, LANES), lambda i: (0, 0))],
        out_shape=[jax.ShapeDtypeStruct((t, LANES), F32), jax.ShapeDtypeStruct((1, LANES), F32)],
        scratch_shapes=[pltpu.VMEM((1, LANES), F32)],
        compiler_params=_params(("arbitrary",)),
    )(dc, small, b_forget_p, dga)


def _fox_pairs(nb, by_key):
    if by_key:
        pairs = [(qi, ki) for ki in range(nb) for qi in range(ki, nb)]
    else:
        pairs = [(qi, ki) for qi in range(nb) for ki in range(qi + 1)]
    return (jnp.asarray(np.array([p[0] for p in pairs], np.int32)),
            jnp.asarray(np.array([p[1] for p in pairs], np.int32)), len(pairs))


def _fox_specs(tb, fd, c_fq, c_fkv):
    gw = FOX_GROUP * fd
    q0, kv0 = c_fq // gw, c_fkv // (2 * gw)
    return dict(
        q=pl.BlockSpec((tb, gw), lambda g, p, qt, kt: (qt[p], q0 + g)),
        kv=pl.BlockSpec((tb, 2 * gw), lambda g, p, qt, kt: (kt[p], kv0 + g)),
        col=pl.BlockSpec((FOX_GROUP, tb, 1), lambda g, p, qt, kt: (g, qt[p], 0)),
        row=pl.BlockSpec((FOX_GROUP, 1, tb), lambda g, p, qt, kt: (g, 0, kt[p])),
        head=pl.BlockSpec((tb, gw), lambda g, p, qt, kt: (qt[p], g)),
        key_kv=pl.BlockSpec((tb, 2 * gw), lambda g, p, qt, kt: (kt[p], g)),
    )


def _fox_mask(qi, ki, tb):
    row = qi * tb + lax.broadcasted_iota(jnp.int32, (tb, tb), 0)
    col = ki * tb + lax.broadcasted_iota(jnp.int32, (tb, tb), 1)
    return (col <= row) & (col >= PAD)


def _fox_heads(q_ref, kv_ref, fd):
    return [(q_ref[:, hh * fd:(hh + 1) * fd], kv_ref[:, 2 * hh * fd:(2 * hh + 1) * fd],
             kv_ref[:, (2 * hh + 1) * fd:(2 * hh + 2) * fd]) for hh in range(FOX_GROUP)]


def _fox_fwd(proj, c_col, c_row, t, fd, c_fq, c_fkv):
    tb = _pick(t, 384, LANES)
    nb = t // tb
    scale = fd ** -0.5
    sp = _fox_specs(tb, fd, c_fq, c_fkv)
    qt, kt, npairs = _fox_pairs(nb, by_key=False)

    def body(qt_ref, kt_ref, q_ref, kv_ref, cq_ref, ck_ref, o_ref, lse_ref, m_ref, l_ref, acc_ref):
        p = pl.program_id(1)
        qi, ki = qt_ref[p], kt_ref[p]

        @pl.when(ki == 0)
        def _():
            m_ref[...] = jnp.full_like(m_ref, -jnp.inf)
            l_ref[...] = jnp.zeros_like(l_ref)
            acc_ref[...] = jnp.zeros_like(acc_ref)

        def update(masked):
            mask = _fox_mask(qi, ki, tb) if masked else None
            for hh, (q, k, v) in enumerate(_fox_heads(q_ref, kv_ref, fd)):
                s = _dot(q, k, NT) * scale + cq_ref[hh] - ck_ref[hh]
                if masked:
                    s = jnp.where(mask, s, MASK_VALUE)
                m_prev = m_ref[hh]
                m_new = jnp.maximum(m_prev, jnp.max(s, axis=-1, keepdims=True))
                alpha = jnp.exp(m_prev - m_new)
                pe = jnp.exp(s - m_new)
                l_ref[hh] = alpha * l_ref[hh] + jnp.sum(pe, axis=-1, keepdims=True)
                acc_ref[hh] = alpha * acc_ref[hh] + _dot(pe.astype(BF16), v, NN)
                m_ref[hh] = m_new

        edge = (ki == 0) | (ki == qi)
        pl.when(edge)(functools.partial(update, True))
        pl.when(jnp.logical_not(edge))(functools.partial(update, False))

        @pl.when(ki == qi)
        def _():
            real = _row_ids(qi * tb, tb) >= PAD
            for hh in range(FOX_GROUP):
                o_ref[:, hh * fd:(hh + 1) * fd] = jnp.where(real, acc_ref[hh] / l_ref[hh], 0.0)
                lse_ref[hh] = m_ref[hh] + jnp.log(l_ref[hh])

    return pl.pallas_call(
        body, name="fox_fwd",
        grid_spec=pltpu.PrefetchScalarGridSpec(
            num_scalar_prefetch=2, grid=(FOX_HEADS // FOX_GROUP, npairs),
            in_specs=[sp["q"], sp["kv"], sp["col"], sp["row"]],
            out_specs=[sp["head"], sp["col"]],
            scratch_shapes=[pltpu.VMEM((FOX_GROUP, tb, 1), F32), pltpu.VMEM((FOX_GROUP, tb, 1), F32),
                            pltpu.VMEM((FOX_GROUP, tb, fd), F32)]),
        out_shape=[jax.ShapeDtypeStruct((t, FOX_HEADS * fd), F32), jax.ShapeDtypeStruct((FOX_HEADS, t, 1), F32)],
        compiler_params=_params(("parallel", "arbitrary")),
    )(qt, kt, proj, proj, c_col, c_row)


def _fox_delta(o_fox, do_fox, t, fd):
    tb = _pick(t, 384, LANES)

    def body(o_ref, do_ref, out_ref):
        for h in range(FOX_HEADS):
            sl = slice(h * fd, (h + 1) * fd)
            out_ref[h] = jnp.sum(o_ref[:, sl] * do_ref[:, sl].astype(BF16).astype(F32), axis=-1, keepdims=True)

    w = FOX_HEADS * fd
    return pl.pallas_call(
        body, name="fox_delta", grid=(t // tb,),
        in_specs=[pl.BlockSpec((tb, w), lambda i: (i, 0)), pl.BlockSpec((tb, w), lambda i: (i, 0))],
        out_specs=pl.BlockSpec((FOX_HEADS, tb, 1), lambda i: (0, i, 0)),
        out_shape=jax.ShapeDtypeStruct((FOX_HEADS, t, 1), F32),
        compiler_params=_params(("parallel",)),
    )(o_fox, do_fox)


def _fox_bwd(proj, c_col, c_row, lse, delta, do_fox, t, fd, c_fq, c_fkv):
    tb = _pick(t, 384, LANES)
    nb = t // tb
    scale = fd ** -0.5
    sp = _fox_specs(tb, fd, c_fq, c_fkv)
    qt, kt, npairs = _fox_pairs(nb, by_key=True)
    gw = FOX_GROUP * fd

    def body(qt_ref, kt_ref, q_ref, kv_ref, cq_ref, ck_ref, lse_ref, dl_ref, do_ref, dq_ref, dkv_ref, dc_ref, dr_ref,
             dq_acc, dk_acc, dv_acc, dc_acc, dr_acc):
        p = pl.program_id(1)
        qi, ki = qt_ref[p], kt_ref[p]

        @pl.when(p == 0)
        def _():
            dq_acc[...] = jnp.zeros_like(dq_acc)
            dr_acc[...] = jnp.zeros_like(dr_acc)

        @pl.when(qi == ki)
        def _():
            dk_acc[...] = jnp.zeros_like(dk_acc)
            dv_acc[...] = jnp.zeros_like(dv_acc)
            dc_acc[...] = jnp.zeros_like(dc_acc)

        rows = pl.ds(pl.multiple_of(qi * tb, LANES), tb)

        def update(masked):
            mask = _fox_mask(qi, ki, tb) if masked else None
            for hh, (q, k, v) in enumerate(_fox_heads(q_ref, kv_ref, fd)):
                do = _bf(do_ref[:, hh * fd:(hh + 1) * fd])
                s = _dot(q, k, NT) * scale + cq_ref[hh] - ck_ref[hh]
                if masked:
                    s = jnp.where(mask, s, MASK_VALUE)
                pr = jnp.exp(s - lse_ref[hh])
                dp = _dot(do, v, NT)
                ds = pr * (dp - dl_ref[hh])
                ds16 = ds.astype(BF16)
                dv_acc[hh] += _dot(pr.astype(BF16), do, TN)
                dk_acc[hh] += _dot(ds16, q, TN)
                dc_acc[hh] += _colsum(ds)
                dr_acc[hh, rows, :] += jnp.sum(ds, axis=-1, keepdims=True)
                dq_acc[hh, rows, :] += _dot(ds16, k, NN)

        edge = (ki == 0) | (ki == qi)
        pl.when(edge)(functools.partial(update, True))
        pl.when(jnp.logical_not(edge))(functools.partial(update, False))

        @pl.when(qi == nb - 1)
        def _():
            for hh in range(FOX_GROUP):
                dkv_ref[:, 2 * hh * fd:(2 * hh + 1) * fd] = (dk_acc[hh] * scale).astype(dkv_ref.dtype)
                dkv_ref[:, (2 * hh + 1) * fd:(2 * hh + 2) * fd] = dv_acc[hh].astype(dkv_ref.dtype)
                dc_ref[hh] = -dc_acc[hh]

        @pl.when(p == npairs - 1)
        def _():
            for hh in range(FOX_GROUP):
                dq_ref[:, hh * fd:(hh + 1) * fd] = (dq_acc[hh] * scale).astype(dq_ref.dtype)
            dr_ref[...] = dr_acc[...]

    return pl.pallas_call(
        body, name="fox_bwd",
        grid_spec=pltpu.PrefetchScalarGridSpec(
            num_scalar_prefetch=2, grid=(FOX_HEADS // FOX_GROUP, npairs),
            in_specs=[sp["q"], sp["kv"], sp["col"], sp["row"], sp["col"], sp["col"], sp["head"]],
            out_specs=[pl.BlockSpec((t, gw), lambda g, p, qt, kt: (0, g)), sp["key_kv"], sp["row"],
                       pl.BlockSpec((FOX_GROUP, t, 1), lambda g, p, qt, kt: (g, 0, 0))],
            scratch_shapes=[pltpu.VMEM((FOX_GROUP, t, fd), F32), pltpu.VMEM((FOX_GROUP, tb, fd), F32),
                            pltpu.VMEM((FOX_GROUP, tb, fd), F32), pltpu.VMEM((FOX_GROUP, 1, tb), F32),
                            pltpu.VMEM((FOX_GROUP, t, 1), F32)]),
        out_shape=[jax.ShapeDtypeStruct((t, FOX_HEADS * fd), BF16), jax.ShapeDtypeStruct((t, 2 * FOX_HEADS * fd), BF16),
                   jax.ShapeDtypeStruct((FOX_HEADS, 1, t), F32), jax.ShapeDtypeStruct((FOX_HEADS, t, 1), F32)],
        compiler_params=_params(("parallel", "arbitrary")),
    )(qt, kt, proj, proj, c_col, c_row, lse, delta, do_fox)


def _gla_gate_fwd(small, w_alpha_p, b_alpha, t, gk):
    def fn(row0, s, w, b):
        z = _dot(s.astype(BF16), w, NN) + b
        return (jnp.where(_row_ids(row0, s.shape[0]) >= PAD, _log_sigmoid(z) * (1.0 / GLA_TAU), 0.0),)

    return _ew("gla_gate_fwd", fn, [("tile", small, LANES, 0), ("full", w_alpha_p), ("full", b_alpha)],
               [("tile", F32, gk)], t, 264)


def _gla_gate_bwd(dglog, small, w_alpha_p, b_alpha, t, gk):
    def fn(row0, dg, s, w, b):
        z = _dot(s.astype(BF16), w, NN) + b
        dz = jnp.where(_row_ids(row0, s.shape[0]) >= PAD, dg * (1.0 / GLA_TAU) * _sigmoid(-z), 0.0)
        return dz, _colsum(dz)

    return _ew("gla_gate_bwd", fn,
               [("tile", dglog, gk, 0), ("tile", small, LANES, 0), ("full", w_alpha_p), ("full", b_alpha)],
               [("tile", BF16, gk), ("acc", F32, gk)], t, 264)


def _gla_chunk(q, k, g, scale, cs):
    r = lax.broadcasted_iota(jnp.int32, (cs, cs), 0)
    c = lax.broadcasted_iota(jnp.int32, (cs, cs), 1)
    causal = c <= r
    b = _tri_dot(causal.astype(BF16), g)
    bl = b[cs - 1:cs, :]
    eb, einv, eend = jnp.exp(b), jnp.exp(-b), jnp.exp(bl - b)
    qd = q.astype(F32) * scale * eb
    kf = k.astype(F32)
    return causal, (eb, einv, eend), bl, qd, kf * einv, kf * eend


def _gla_fwd(proj, glog, t, dk, dv, c_q, c_k, c_v):
    cs = GLA_CHUNK
    nc = t // cs
    wk, wv = GLA_HEADS * dk, GLA_HEADS * dv
    scale = dk ** -0.5

    def body(q_ref, k_ref, v_ref, g_ref, o_ref, sp_ref, st_ref):
        @pl.when(pl.program_id(0) == 0)
        def _():
            st_ref[...] = jnp.zeros_like(st_ref)

        for h in range(GLA_HEADS):
            ks, vs = slice(h * dk, (h + 1) * dk), slice(h * dv, (h + 1) * dv)
            v = v_ref[:, vs]
            causal, _, bl, qd, ki, ke = _gla_chunk(q_ref[:, ks], k_ref[:, ks], g_ref[:, ks], scale, cs)
            st = st_ref[h]
            sp_ref[h] = st
            a = jnp.where(causal, _dot(qd.astype(BF16), ki.astype(BF16), NT), 0.0)
            o_ref[:, vs] = _dot(a.astype(BF16), v, NN) + _dot(qd.astype(BF16), st.astype(BF16), NT)
            st_ref[h] = st * jnp.exp(bl) + _dot(v, ke.astype(BF16), TN)

    return pl.pallas_call(
        body, name="gla_fwd", grid=(nc,),
        in_specs=[pl.BlockSpec((cs, wk), lambda n: (n, c_q // wk)), pl.BlockSpec((cs, wk), lambda n: (n, c_k // wk)),
                  pl.BlockSpec((cs, wv), lambda n: (n, c_v // wv)), pl.BlockSpec((cs, wk), lambda n: (n, 0))],
        out_specs=[pl.BlockSpec((cs, wv), lambda n: (n, 0)),
                   pl.BlockSpec((None, GLA_HEADS, dv, dk), lambda n: (n, 0, 0, 0))],
        out_shape=[jax.ShapeDtypeStruct((t, wv), F32), jax.ShapeDtypeStruct((nc, GLA_HEADS, dv, dk), F32)],
        scratch_shapes=[pltpu.VMEM((GLA_HEADS, dv, dk), F32)],
        compiler_params=_params(("arbitrary",)),
    )(proj, proj, proj, glog)


def _gla_bwd(proj, glog, s_prev, do_raw, t, dk, dv, c_q, c_k, c_v):
    cs = GLA_CHUNK
    nc = t // cs
    wk, wv = GLA_HEADS * dk, GLA_HEADS * dv
    scale = dk ** -0.5

    def body(q_ref, k_ref, v_ref, g_ref, sp_ref, do_ref, dq_ref, dk_ref, dv_ref, dg_ref, dst_ref):
        @pl.when(pl.program_id(0) == 0)
        def _():
            dst_ref[...] = jnp.zeros_like(dst_ref)

        for h in range(GLA_HEADS):
            ks, vs = slice(h * dk, (h + 1) * dk), slice(h * dv, (h + 1) * dv)
            v = v_ref[:, vs]
            do = do_ref[:, vs].astype(BF16)
            causal, (eb, einv, eend), bl, qd, ki, ke = _gla_chunk(q_ref[:, ks], k_ref[:, ks], g_ref[:, ks], scale, cs)
            qd16, ki16, ke16 = qd.astype(BF16), ki.astype(BF16), ke.astype(BF16)
            st = sp_ref[h]
            dst = dst_ref[h]
            dst16 = dst.astype(BF16)
            a = jnp.where(causal, _dot(qd16, ki16, NT), 0.0).astype(BF16)
            da = jnp.where(causal, _dot(do, v, NT), 0.0).astype(BF16)
            dvv = _dot(a, do, TN) + _dot(ke16, dst16, NT)
            dqd = _dot(da, ki16, NN) + _dot(do, st.astype(BF16), NN)
            dki = _dot(da, qd16, TN)
            dke = _dot(v, dst16, NN)
            dl = jnp.exp(bl)
            ddl = _colsum(dst * st)
            dst_ref[h] = dst * dl + _dot(do, qd16, TN)
            dq_ref[:, ks] = (dqd * eb * scale).astype(dq_ref.dtype)
            dk_ref[:, ks] = (dki * einv + dke * eend).astype(dk_ref.dtype)
            dv_ref[:, vs] = dvv.astype(dv_ref.dtype)
            db = dqd * qd - dki * ki - dke * ke
            db_last = _colsum(dke * ke) + ddl * dl
            r = lax.broadcasted_iota(jnp.int32, (cs, cs), 0)
            c = lax.broadcasted_iota(jnp.int32, (cs, cs), 1)
            dg_ref[:, ks] = _tri_dot((c >= r).astype(BF16), db) + db_last

    rev = lambda f: (lambda n: f(nc - 1 - n))
    return pl.pallas_call(
        body, name="gla_bwd", grid=(nc,),
        in_specs=[pl.BlockSpec((cs, wk), rev(lambda n: (n, c_q // wk))), pl.BlockSpec((cs, wk), rev(lambda n: (n, c_k // wk))),
                  pl.BlockSpec((cs, wv), rev(lambda n: (n, c_v // wv))), pl.BlockSpec((cs, wk), rev(lambda n: (n, 0))),
                  pl.BlockSpec((None, GLA_HEADS, dv, dk), rev(lambda n: (n, 0, 0, 0))),
                  pl.BlockSpec((cs, wv), rev(lambda n: (n, 0)))],
        out_specs=[pl.BlockSpec((cs, wk), rev(lambda n: (n, 0))), pl.BlockSpec((cs, wk), rev(lambda n: (n, 0))),
                   pl.BlockSpec((cs, wv), rev(lambda n: (n, 0))), pl.BlockSpec((cs, wk), rev(lambda n: (n, 0)))],
        out_shape=[jax.ShapeDtypeStruct((t, wk), BF16), jax.ShapeDtypeStruct((t, wk), BF16),
                   jax.ShapeDtypeStruct((t, wv), BF16), jax.ShapeDtypeStruct((t, wk), F32)],
        scratch_shapes=[pltpu.VMEM((GLA_HEADS, dv, dk), F32)],
        compiler_params=_params(("arbitrary",)),
    )(proj, proj, proj, glog, s_prev, do_raw)


def _gla_post_fwd(o_raw, proj, gn, t, dv, c_gr):
    w = GLA_HEADS * dv

    def fn(row0, o, gr, g):
        gr = gr.astype(F32)
        outs = []
        for h in range(GLA_HEADS):
            oh = o[:, h * dv:(h + 1) * dv]
            outs.append(oh * lax.rsqrt(jnp.mean(oh * oh, axis=-1, keepdims=True) + EPS))
        on = jnp.concatenate(outs, axis=1) * g
        return (on * (gr * _sigmoid(gr)),)

    return _ew("gla_post_fwd", fn, [("tile", o_raw, w, 0), ("tile", proj, w, c_gr), ("full", gn)],
               [("tile", BF16, w)], t, 264)


def _gla_post_bwd(o_raw, proj, gn, do_gla, t, dv, c_gr):
    w = GLA_HEADS * dv

    def fn(row0, o, gr, g, do):
        gr = gr.astype(F32)
        sg = _sigmoid(gr)
        don = do * (gr * sg)
        ohs, dos = [], []
        for h in range(GLA_HEADS):
            sl = slice(h * dv, (h + 1) * dv)
            oh = o[:, sl]
            r = lax.rsqrt(jnp.mean(oh * oh, axis=-1, keepdims=True) + EPS)
            xh = oh * r
            dxh = don[:, sl] * g[:, sl]
            ohs.append(xh)
            dos.append(r * (dxh - xh * jnp.mean(dxh * xh, axis=-1, keepdims=True)))
        xh = jnp.concatenate(ohs, axis=1)
        dgr = do * (xh * g) * (sg * (1.0 + gr * (1.0 - sg)))
        return jnp.concatenate(dos, axis=1), dgr, _colsum(don * xh)

    return _ew("gla_post_bwd", fn,
               [("tile", o_raw, w, 0), ("tile", proj, w, c_gr), ("full", gn), ("tile", do_gla, w, 0)],
               [("tile", F32, w), ("tile", BF16, w), ("acc", F32, w)], t, 264)


def _adamw(name, w, g, m, v, layer=None, into=None):
    nl, rows, cols = w.shape
    tm = _pick(rows, max(8, (512 * 1024) // max(cols, 1) // 8 * 8), 8)

    def body(w_ref, g_ref, m_ref, v_ref, *rest):
        go_ref, d_ref, nm_ref, nv_ref = rest[-4:]
        gg = g_ref[...]
        nm = ADAM_B1 * m_ref[...] + (1.0 - ADAM_B1) * gg
        nv = ADAM_B2 * v_ref[...] + (1.0 - ADAM_B2) * (gg * gg)
        m_hat = nm / (1.0 - ADAM_B1 ** ADAM_STEP)
        v_hat = nv / (1.0 - ADAM_B2 ** ADAM_STEP)
        go_ref[...] = gg
        d_ref[...] = -ADAM_LR * (m_hat / (jnp.sqrt(v_hat) + ADAM_EPS) + ADAM_WD * w_ref[...])
        nm_ref[...] = nm
        nv_ref[...] = nv

    out_shape = [jax.ShapeDtypeStruct((nl, rows, cols), F32)] * 4
    if layer is None:
        spec = pl.BlockSpec((None, tm, cols), lambda l, i: (l, i, 0))
        return pl.pallas_call(
            body, name=name, grid=(nl, rows // tm), in_specs=[spec] * 4, out_specs=[spec] * 4, out_shape=out_shape,
            compiler_params=_params(("parallel", "parallel")),
        )(w, g, m, v)
    spec = pl.BlockSpec((None, tm, cols), lambda i: (layer, i, 0))
    in_specs = [spec, pl.BlockSpec((tm, cols), lambda i: (i, 0)), spec, spec]
    args, aliases = [w, g, m, v], {}
    if into is not None:
        in_specs += [pl.BlockSpec(memory_space=pl.ANY)] * 4
        args += list(into)
        aliases = {4 + k: k for k in range(4)}
    return pl.pallas_call(
        body, name=name, grid=(rows // tm,), in_specs=in_specs, out_specs=[spec] * 4, out_shape=out_shape,
        input_output_aliases=aliases, compiler_params=_params(("parallel",)),
    )(*args)


def _me():
    return lax.axis_index("x"), lax.axis_index("y"), lax.axis_index("c")


def _hbm_specs(n):
    return [pl.BlockSpec(memory_space=pl.ANY)] * n


_HBM = pl.BlockSpec(memory_space=pltpu.HBM)
_SEM = pl.BlockSpec(memory_space=pltpu.SEMAPHORE)
_EFFECT = pltpu.SideEffectType.DATAFLOW_SIDE_EFFECTING
N_PEERS = N_CHIPS - 1


def _other_chips(x, y):
    return [(1 - x, y), (x, 1 - y), (1 - x, 1 - y)]


def _split_copies(plan, src, land, send_sems, recv_sems):
    me = _me()
    copies = []
    for i in range(len(src)):
        for j, (s, d, peer) in enumerate(plan(src[i], land[i], me)):
            k = plan.copies * i + j
            copies.append(pltpu.make_async_remote_copy(
                src_ref=s, dst_ref=d, send_sem=send_sems.at[k], recv_sem=recv_sems.at[k], device_id=peer,
                device_id_type=MESH))
    return copies


def _split_start(name, srcs, lands, plan, after=None):
    n = len(srcs)
    extra = [] if after is None else [after]

    def body(*refs):
        src, land = refs[:n], refs[n:2 * n]
        send_sems, recv_sems = refs[2 * n + len(extra)], refs[2 * n + len(extra) + 1]
        token = refs[-1]
        for cp in _split_copies(plan, src, land, send_sems, recv_sems):
            cp.start()
        token[...] = jnp.zeros_like(token)

    out_shape = ([pltpu.SemaphoreType.DMA((plan.copies * n,)), pltpu.SemaphoreType.DMA((plan.copies * n,))]
                 + [pltpu.HBM(a.shape, a.dtype) for a in list(srcs) + list(lands)]
                 + [jax.ShapeDtypeStruct((8, LANES), F32)])
    res = pl.pallas_call(
        body, name=name, out_shape=out_shape,
        in_specs=[_HBM] * (2 * n) + [pl.BlockSpec(memory_space=pl.ANY)] * len(extra),
        out_specs=[_SEM, _SEM] + [_HBM] * (2 * n) + [pl.BlockSpec(memory_space=pltpu.VMEM)],
        input_output_aliases={i: 2 + i for i in range(2 * n)},
        compiler_params=pltpu.CompilerParams(has_side_effects=_EFFECT),
    )(*[pltpu.with_memory_space_constraint(a, pltpu.HBM) for a in list(srcs) + list(lands)], *extra)
    return res[0], res[1], res[2:2 + n], res[2 + n:2 + 2 * n], res[-1]


def _split_wait(name, send_sems, recv_sems, srcs, lands, plan, after):
    n = len(srcs)

    def body(*refs):
        src, land = refs[:n], refs[n:2 * n]
        s_sems, r_sems = refs[2 * n], refs[2 * n + 1]
        for cp in _split_copies(plan, src, land, s_sems, r_sems):
            cp.wait_send()
            cp.wait_recv()

    res = pl.pallas_call(
        body, name=name, out_shape=[pltpu.HBM(a.shape, a.dtype) for a in list(srcs) + list(lands)],
        in_specs=[_HBM] * (2 * n) + [_SEM, _SEM, pl.BlockSpec(memory_space=pl.ANY)], out_specs=[_HBM] * (2 * n),
        input_output_aliases={i: i for i in range(2 * n)},
        compiler_params=pltpu.CompilerParams(has_side_effects=_EFFECT),
    )(*srcs, *lands, send_sems, recv_sems, after)
    return res[:n], res[n:]


def _exchange_start(name, bufs, plan, after):
    n = len(bufs)

    def body(*refs):
        buf = refs[:n]
        for cp in _split_copies(plan, buf, buf, refs[n + 1], refs[n + 2]):
            cp.start()
        refs[-1][...] = jnp.zeros_like(refs[-1])

    out_shape = ([pltpu.SemaphoreType.DMA((plan.copies * n,)), pltpu.SemaphoreType.DMA((plan.copies * n,))]
                 + [pltpu.HBM(a.shape, a.dtype) for a in bufs] + [jax.ShapeDtypeStruct((8, LANES), F32)])
    res = pl.pallas_call(
        body, name=name, out_shape=out_shape, in_specs=[_HBM] * n + [pl.BlockSpec(memory_space=pl.ANY)],
        out_specs=[_SEM, _SEM] + [_HBM] * n + [pl.BlockSpec(memory_space=pltpu.VMEM)],
        input_output_aliases={i: 2 + i for i in range(n)},
        compiler_params=pltpu.CompilerParams(has_side_effects=_EFFECT),
    )(*[pltpu.with_memory_space_constraint(a, pltpu.HBM) for a in bufs], after)
    return res[0], res[1], res[2:2 + n], res[-1]


def _exchange_wait(name, send_sems, recv_sems, bufs, plan, after):
    n = len(bufs)

    def body(*refs):
        buf = refs[:n]
        for cp in _split_copies(plan, buf, buf, refs[n], refs[n + 1]):
            cp.wait_send()
            cp.wait_recv()

    return pl.pallas_call(
        body, name=name, out_shape=[pltpu.HBM(a.shape, a.dtype) for a in bufs],
        in_specs=[_HBM] * n + [_SEM, _SEM, pl.BlockSpec(memory_space=pl.ANY)], out_specs=[_HBM] * n,
        input_output_aliases={i: i for i in range(n)},
        compiler_params=pltpu.CompilerParams(has_side_effects=_EFFECT),
    )(*bufs, send_sems, recv_sems, after)


def _forward_plan(src, land, me):
    x, y, c = me
    rows = _half(src.shape[1], c)
    return [(src.at[2 * px + py, rows], land.at[2 * px + py, rows], (x, y, 1 - c)) for px, py in _other_chips(x, y)]


def _share_plan(src, land, me):
    x, y, c = me
    rows = _half(src.shape[0], c)
    return [(src.at[rows], land.at[rows], (x, y, 1 - c))]


_forward_plan.copies = N_CHIPS - 1
_share_plan.copies = 1


def _half(ref_rows, c):
    half = ref_rows // 2
    return pl.ds(c * half, half)


def _gather_plan(src, land, me):
    x, y, c = me
    rows = _half(src.shape[0], c)
    return [(src.at[rows], land.at[2 * x + y, rows], (px, py, c)) for px, py in _other_chips(x, y)]


def _scatter_plan(src, land, me):
    x, y, c = me
    return [(src.at[2 * px + py], land.at[2 * x + y], (px, py, c)) for px, py in _other_chips(x, y)]


def _swap_plan(src, land, me):
    x, y, c = me
    return [(src.at[:, _half(src.shape[1], 1 - c)], land, (x, y, 1 - c))]


_gather_plan.copies = N_PEERS
_scatter_plan.copies = N_PEERS
_swap_plan.copies = 1


def _all_gather_small(v):
    def body(v_ref, out_ref, send_sems, recv_sems, local_sem):
        x, y, c = _me()
        mine = pltpu.make_async_copy(v_ref, out_ref.at[4 * x + 2 * y + c], local_sem)
        mine.start()
        copies = []
        for k in range(1, N_DEV):
            peer = (x ^ ((k >> 2) & 1), y ^ ((k >> 1) & 1), c ^ (k & 1))
            cp = pltpu.make_async_remote_copy(
                src_ref=v_ref, dst_ref=out_ref.at[4 * x + 2 * y + c], send_sem=send_sems.at[k - 1],
                recv_sem=recv_sems.at[k - 1], device_id=peer, device_id_type=MESH)
            cp.start()
            copies.append(cp)
        for cp in copies:
            cp.wait_recv()
        for cp in copies:
            cp.wait_send()
        mine.wait()

    return pl.pallas_call(
        body, name="all_gather_small", in_specs=_hbm_specs(1), out_specs=pl.BlockSpec(memory_space=pl.ANY),
        out_shape=jax.ShapeDtypeStruct((N_DEV,) + v.shape, v.dtype),
        scratch_shapes=[pltpu.SemaphoreType.DMA((N_DEV - 1,)), pltpu.SemaphoreType.DMA((N_DEV - 1,)),
                        pltpu.SemaphoreType.DMA],
    )(v)


def _pair_sum(name, own, recv, chip, core):
    nch, half, cdim = recv.shape
    tm = _pick(half, max(8, (512 * 1024) // cdim // 16 * 16), 16)
    nt = half // tm

    def body(c_ref, k_ref, a_ref, b_ref, s16_ref):
        s16_ref[...] = (a_ref[...] + b_ref[...]).astype(BF16)

    other = lambda j, c: (c[0] + 1 + j) % nch
    spec = pl.BlockSpec((None, tm, cdim), lambda j, i, c, k: (other(j, c), i, 0))
    return pl.pallas_call(
        body, name=name,
        grid_spec=pltpu.PrefetchScalarGridSpec(
            num_scalar_prefetch=2, grid=(nch - 1, nt),
            in_specs=[pl.BlockSpec((None, tm, cdim), lambda j, i, c, k: (other(j, c), k[0] * nt + i, 0)), spec],
            out_specs=spec),
        out_shape=jax.ShapeDtypeStruct((nch, half, cdim), BF16),
        compiler_params=_params(("parallel", "parallel")),
    )(chip, core, own, recv)


def _chip_sum(name, own, recv, landed, chip, core):
    nch, half, cdim = recv.shape
    tm = _pick(half, max(8, (512 * 1024) // cdim // 16 * 16), 16)
    nt = half // tm

    def body(c_ref, k_ref, own_ref, recv_ref, *rest):
        landed_refs, out_ref = rest[:nch], rest[-1]
        me = c_ref[0]
        mine = own_ref[...] + recv_ref[...]
        acc = None
        for j in range(nch):
            term = jnp.where(me == j, mine, landed_refs[j][...].astype(F32))
            acc = term if acc is None else acc + term
        out_ref[...] = acc

    landed_specs = [pl.BlockSpec((None, tm, cdim), functools.partial(lambda i, c, k, j: (j, i, 0), j=j))
                    for j in range(nch)]
    return pl.pallas_call(
        body, name=name,
        grid_spec=pltpu.PrefetchScalarGridSpec(
            num_scalar_prefetch=2, grid=(nt,),
            in_specs=[pl.BlockSpec((None, tm, cdim), lambda i, c, k: (c[0], k[0] * nt + i, 0)),
                      pl.BlockSpec((None, tm, cdim), lambda i, c, k: (c[0], i, 0))] + landed_specs,
            out_specs=pl.BlockSpec((tm, cdim), lambda i, c, k: (k[0] * nt + i, 0))),
        out_shape=jax.ShapeDtypeStruct((2 * half, cdim), F32),
        compiler_params=_params(("parallel",)),
    )(chip, core, own, recv, *([landed] * nch))


def _sum_devices(gathered):
    _, r, cdim = gathered.shape

    def body(g_ref, o_ref):
        acc = g_ref[0]
        for k in range(1, N_DEV):
            acc = acc + g_ref[k]
        o_ref[...] = acc

    return pl.pallas_call(
        body, name="sum_devices", out_shape=jax.ShapeDtypeStruct((r, cdim), F32),
        compiler_params=_params(),
    )(gathered)


class _Layout:
    def __init__(self, d):
        self.d = d
        self.fw = d // 2
        self.fd = self.fw // FOX_HEADS
        self.gk = d // 2
        self.gv = d
        self.dk = self.gk // GLA_HEADS
        self.dv = self.gv // GLA_HEADS
        self.c_fq = 0
        self.c_gq = self.fw
        self.c_gv = self.c_gq + self.gk
        self.c_gr = self.c_gv + self.gv
        self.c_fkv = self.c_gr + self.gv
        self.c_gates = self.c_fkv + 2 * self.fw
        self.c_gk = self.c_gates + 2 * d
        self.c_small = self.c_gk + self.gk
        self.n_main = self.c_small
        self.n_p = self.c_small + LANES
        self.o_fk = self.fw
        self.o_fv = 2 * self.fw
        self.o_ff = 3 * self.fw
        self.o_gq = self.o_ff + FOX_HEADS
        self.o_gk = self.o_gq + self.gk
        self.o_gv = self.o_gk + self.gk
        self.o_gr = self.o_gv + self.gv
        self.o_ga = self.o_gr + self.gv
        self.o_gf = self.o_ga + GLA_RANK
        self.o_gg = self.o_gf + d
        self.n_orig = self.o_gg + d

    def to_p(self, shards):
        per = self.n_orig // N_CHIPS
        ranges = [(0, self.fw), (self.o_gq, self.gk), (self.o_gv, self.gv), (self.o_gr, self.gv)]
        for h in range(FOX_HEADS):
            ranges += [(self.o_fk + h * self.fd, self.fd), (self.o_fv + h * self.fd, self.fd)]
        ranges += [(self.o_gf, 2 * self.d), (self.o_gk, self.gk), (self.o_ff, FOX_HEADS), (self.o_ga, GLA_RANK)]
        pieces = []
        for a, width in ranges:
            for j in range(a // per, (a + width - 1) // per + 1):
                lo, hi = max(a, j * per), min(a + width, (j + 1) * per)
                pieces.append(shards[j][:, lo - j * per:hi - j * per])
        pieces.append(jnp.zeros((shards.shape[1], LANES - FOX_HEADS - GLA_RANK), shards.dtype))
        return jnp.concatenate(pieces, axis=1)

    def from_segments(self, seg):
        per = self.n_orig // N_CHIPS
        fd = self.fd
        atoms = [("fq", 0, self.fw)]
        atoms += [("fkv", 2 * h * fd, fd) for h in range(FOX_HEADS)]
        atoms += [("fkv", (2 * h + 1) * fd, fd) for h in range(FOX_HEADS)]
        atoms += [("small", 0, FOX_HEADS), ("gq", 0, self.gk), ("gk", 0, self.gk), ("gv", 0, self.gv),
                  ("gr", 0, self.gv), ("small", FOX_HEADS, GLA_RANK), ("gates", 0, 2 * self.d)]
        shards = [[] for _ in range(N_CHIPS)]
        pos = 0
        for name, c0, width in atoms:
            for j in range(pos // per, (pos + width - 1) // per + 1):
                lo, hi = max(pos, j * per), min(pos + width, (j + 1) * per)
                shards[j].append(seg[name][:, c0 + lo - pos:c0 + hi - pos])
            pos += width
        assert pos == self.n_orig
        return jnp.stack([jnp.concatenate(s, axis=1) for s in shards])


def _layer_fwd(lay, h, p, t, hooks=None):
    hooks = hooks or {}
    d = lay.d
    xn, xn_t = _rms_fwd_t("rms_mix_fwd", h, p["norm_mix_g"], t, d)
    proj = _mm("mm_proj", xn, p["w_in"], mode="nn", m=t, n=lay.n_main, k=d, out_dtypes=(BF16,))
    small = _mm("mm_small", xn, p["w_in"], mode="nn", m=t, n=LANES, k=d, b_c0=lay.c_small)
    cs = _fox_gate_fwd(small, p["b_forget_p"], t)
    ct = cs[:, :FOX_HEADS].T
    c_col, c_row = ct[:, :, None], ct[:, None, :]
    o_fox, lse = _fox_fwd(proj, c_col, c_row, t, lay.fd, lay.c_fq, lay.c_fkv)
    if "mixers" in hooks:
        hooks["mixers"](o_fox)
    glog = _gla_gate_fwd(small, p["w_alpha_p"], p["b_alpha"], t, lay.gk)
    o_raw, s_prev = _gla_fwd(proj, glog, t, lay.dk, lay.dv, lay.c_gq, lay.c_gk, lay.c_gv)
    o_gla = _gla_post_fwd(o_raw, proj, p["gla_norm_g"], t, lay.dv, lay.c_gr)
    if "late" in hooks:
        p.update(hooks["late"](o_gla))
    a_fox = _mm("mm_o_fox", o_fox, p["w_o_fox"], mode="nn", m=t, n=d, k=lay.fw, b_shards=N_CHIPS)
    a_gla = _mm("mm_o_gla", o_gla, p["w_o_gla"], mode="nn", m=t, n=d, k=lay.gv)
    y = _merge_fwd(a_fox, a_gla, proj, lay.c_gates, t, d)
    h1 = _mm("mm_out", y, p["w_out"], mode="nn", m=t, n=d, k=d, extras=[h], epilogue=lambda acc, res: (res + acc,))
    xn2, xn2_t = _rms_fwd_t("rms_mlp_fwd", h1, p["norm_mlp_g"], t, d)
    u, act = _mm("mm_ff1", xn2, p["w_ff1"], mode="nn", m=t, n=4 * d, k=d, out_dtypes=(BF16, BF16), b_shards=N_CHIPS,
                 epilogue=lambda acc: (acc, jnp.square(jnp.maximum(acc, 0.0))))
    if "mlp" in hooks:
        hooks["mlp"](act)
    h2 = _mm("mm_ff2", act, p["w_ff2"], mode="nn", m=t, n=d, k=4 * d, extras=[h1],
             epilogue=lambda acc, res: (res + acc,))
    saved = dict(h=h, xn_t=xn_t, proj=proj, small=small, c_col=c_col, c_row=c_row, o_fox=o_fox, lse=lse, glog=glog,
                 o_raw=o_raw, s_prev=s_prev, o_gla=o_gla, a_fox=a_fox, a_gla=a_gla, y=y, h1=h1, xn2_t=xn2_t, u=u, act=act)
    return h2, saved


def _layer_bwd(lay, dh2, p, s, t, gates=None, first=False):
    d = lay.d
    g = {}

    def gated(gain, point):
        return gain + gates[point](g) if gates and point in gates else gain
    du = _mm("mm_dact", dh2, p["w_ff2"], mode="nt", m=t, n=4 * d, k=d, extras=[s["u"]], out_dtypes=(BF16,),
             epilogue=lambda acc, u: (acc * (2.0 * jnp.maximum(u.astype(F32), 0.0)),))
    g["w_ff2"] = _mm("mm_dw_ff2", s["act"], dh2, mode="tn", m=4 * d, n=d, k=t)
    g["w_ff1"] = _mm("mm_dw_ff1", s["xn2_t"], du, mode="nn", m=d, n=4 * d, k=t, tk=t, out_shards=N_CHIPS)
    dxn2 = _mm("mm_dxn2", du, p["w_ff1"], mode="nt", m=t, n=d, k=4 * d, b_shards=N_CHIPS)
    dh1, g["norm_mlp_g"] = _rms_bwd("rms_mlp_bwd", s["h1"], gated(p["norm_mlp_g"], "mlp"), dxn2, dh2, t, d)
    dy = _mm("mm_dy", dh1, p["w_out"], mode="nt", m=t, n=d, k=d)
    g["w_out"] = _mm("mm_dw_out", s["y"], dh1, mode="tn", m=d, n=d, k=t)
    da_fox, da_gla, dgates = _merge_bwd(dy, s["a_fox"], s["a_gla"], s["proj"], lay.c_gates, t, d)
    g["w_o_fox"] = _mm("mm_dw_o_fox", s["o_fox"], da_fox, mode="tn", m=lay.fw, n=d, k=t, out_shards=N_CHIPS)
    do_fox = _mm("mm_do_fox", da_fox, p["w_o_fox"], mode="nt", m=t, n=lay.fw, k=d, b_shards=N_CHIPS)
    g["w_o_gla"] = _mm("mm_dw_o_gla", s["o_gla"], da_gla, mode="tn", m=lay.gv, n=d, k=t)
    do_gla = _mm("mm_do_gla", da_gla, p["w_o_gla"], mode="nt", m=t, n=lay.gv, k=d)
    do_raw, dgr, g["gla_norm_g"] = _gla_post_bwd(s["o_raw"], s["proj"], gated(p["gla_norm_g"], "out"), do_gla, t,
                                                 lay.dv, lay.c_gr)
    dgq, dgk, dgv, dglog = _gla_bwd(s["proj"], s["glog"], s["s_prev"], do_raw, t, lay.dk, lay.dv,
                                    lay.c_gq, lay.c_gk, lay.c_gv)
    dz, g["b_alpha"] = _gla_gate_bwd(dglog, s["small"], p["w_alpha_p"], p["b_alpha"], t, lay.gk)
    g["w_alpha_p"] = _mm("mm_dw_alpha", s["small"], dz, mode="tn", m=LANES, n=lay.gk, k=t)
    dga = _mm("mm_dga", dz, p["w_alpha_p"], mode="nt", m=t, n=LANES, k=lay.gk)
    delta = _fox_delta(s["o_fox"], do_fox, t, lay.fd)
    dfq, dfkv, dc, dr = _fox_bwd(s["proj"], s["c_col"], s["c_row"], s["lse"], delta, do_fox, t, lay.fd,
                                 lay.c_fq, lay.c_fkv)
    dc_p = jnp.pad((dc[:, 0, :] + dr[:, :, 0]).T, ((0, 0), (0, LANES - FOX_HEADS)))
    dsmall, g["b_forget_p"] = _fox_gate_bwd(dc_p, s["small"], p["b_forget_p"], dga, t)
    segs = [("fq", dfq, lay.c_fq), ("gq", dgq, lay.c_gq), ("gv", dgv, lay.c_gv), ("gr", dgr, lay.c_gr),
            ("fkv", dfkv, lay.c_fkv), ("gates", dgates, lay.c_gates), ("gk", dgk, lay.c_gk),
            ("small", dsmall, lay.c_small)]
    dw_in = {nm: _mm("mm_dw_in_" + nm, s["xn_t"], dseg, mode="nn", m=d, n=dseg.shape[1], k=t, tk=t)
             for nm, dseg, _ in segs}
    g["w_in"] = lay.from_segments(dw_in)
    dxn = _mm("mm_dxn_gates", dgates, p["w_in"], mode="nt", m=t, n=d, k=2 * d, b_c0=lay.c_gates)
    dxn = _mm("mm_dxn_small", dsmall, p["w_in"], mode="nt", m=t, n=d, k=LANES, b_c0=lay.c_small, extras=[dxn],
              epilogue=lambda acc, prev: (prev + acc,))
    for nm, (a1, c1), (a2, c2) in (("q", (dfq, lay.c_fq), (dgq, lay.c_gq)), ("v", (dgv, lay.c_gv), (dgr, lay.c_gr)),
                                   ("k", (dfkv, lay.c_fkv), (dgk, lay.c_gk))):
        dxn = _mm_nt2("mm_dxn_" + nm, a1, c1, a2, c2, p["w_in"], dxn, m=t, n=d)
    if first:
        dx, head, g["norm_mix_g"] = _rms_bwd_x("rms_mix_bwd_x", s["h"], gated(p["norm_mix_g"], "in"), dxn, dh1, t, d)
        return (dx, head), g
    dh, g["norm_mix_g"] = _rms_bwd("rms_mix_bwd", s["h"], gated(p["norm_mix_g"], "in"), dxn, dh1, t, d)
    return dh, g


def _sequence_step(x, target, meta, layers, final_g):
    seq, d = x.shape
    t = seq + ROW0
    lay = _Layout(d)
    h = jnp.pad(x, ((ROW0, 0), (0, 0))).at[PAD:ROW0].set(meta)
    target_p = jnp.pad(target, ((ROW0, 0), (0, 0)))
    saved = []
    for p in layers:
        h, s = _layer_fwd(lay, h, p, t)
        saved.append(s)
    dh, dg_final, loss_part = _loss_head(h, final_g, target_p, t, d)
    grads = [None] * len(layers)
    for l in reversed(range(len(layers))):
        dh, grads[l] = _layer_bwd(lay, dh, layers[l], saved[l], t)
    return loss_part, dh[ROW0:], dh[PAD:ROW0], grads, dg_final


_SMALL_ROWS = 48


def _pack_small(d, meta, mix, gla, mlp, final, b_alpha, b_forget, w_alpha2):
    rows = [meta.reshape(N_META, d), mix.reshape(DEPTH, d), gla.reshape(DEPTH, d), mlp.reshape(DEPTH, d),
            final.reshape(1, d), b_alpha.reshape(1, d),
            jnp.pad(b_forget.reshape(1, DEPTH * FOX_HEADS), ((0, 0), (0, d - DEPTH * FOX_HEADS))),
            jnp.zeros((7, d), F32), w_alpha2.reshape(GLA_RANK, d)]
    return jnp.concatenate(rows, axis=0)


def _unpack_small(d, packed):
    return dict(meta=packed[:N_META], norm_mix_g=packed[16:18], gla_norm_g=packed[18:20], norm_mlp_g=packed[20:22],
                final_norm_g=packed[22], b_alpha=packed[23].reshape(DEPTH, d // 2),
                b_forget=packed[24, :DEPTH * FOX_HEADS].reshape(DEPTH, FOX_HEADS),
                w_alpha2=packed[32:48].reshape(DEPTH, GLA_RANK, d // 2))


_BIG = ("w_in", "w_o_fox", "w_o_gla", "w_out", "w_ff1", "w_ff2")
_COL_SHARDED = ("w_in", "w_o_fox", "w_ff1")


def _full_matrix(name, gathered_l):
    nch, r, c = gathered_l.shape
    if name in _COL_SHARDED:
        return gathered_l.transpose(1, 0, 2).reshape(r, nch * c)
    return gathered_l.reshape(nch * r, c)


def _shard_major(name, full):
    r, c = full.shape
    if name in _COL_SHARDED:
        return full.reshape(r, N_CHIPS, c // N_CHIPS).transpose(1, 0, 2)
    return full.reshape(N_CHIPS, r // N_CHIPS, c)


def kernel(x, meta_tokens, norm_mix_g, w_in, b_forget, w_alpha2, b_alpha, gla_norm_g, w_o_fox, w_o_gla, w_out, norm_mlp_g, w_ff1, w_ff2, final_norm_g, loss_target, m_meta_tokens, m_norm_mix_g, m_w_in, m_b_forget, m_w_alpha2, m_b_alpha, m_gla_norm_g, m_w_o_fox, m_w_o_gla, m_w_out, m_norm_mlp_g, m_w_ff1, m_w_ff2, m_final_norm_g, v_meta_tokens, v_norm_mix_g, v_w_in, v_b_forget, v_w_alpha2, v_b_alpha, v_gla_norm_g, v_w_o_fox, v_w_o_gla, v_w_out, v_norm_mlp_g, v_w_ff1, v_w_ff2, v_final_norm_g):
    d = x.shape[2]
    lay = _Layout(d)
    xi, yi, ci = lax.axis_index("x"), lax.axis_index("y"), lax.axis_index("c")
    chip = (2 * xi + yi).astype(jnp.int32)
    w = dict(w_in=w_in, w_alpha2=w_alpha2, w_o_fox=w_o_fox, w_o_gla=w_o_gla, w_out=w_out, w_ff1=w_ff1, w_ff2=w_ff2)
    m = dict(w_in=m_w_in, w_alpha2=m_w_alpha2, w_o_fox=m_w_o_fox, w_o_gla=m_w_o_gla, w_out=m_w_out, w_ff1=m_w_ff1,
             w_ff2=m_w_ff2)
    v = dict(w_in=v_w_in, w_alpha2=v_w_alpha2, w_o_fox=v_w_o_fox, w_o_gla=v_w_o_gla, w_out=v_w_out, w_ff1=v_w_ff1,
             w_ff2=v_w_ff2)

    seq = x.shape[1]
    t = seq + ROW0
    core_idx = ci.astype(jnp.int32)[None]
    chip_idx = chip[None]

    cols = d // N_CHIPS
    small_w = jnp.concatenate([meta_tokens, w_alpha2.reshape(-1, cols)], axis=0)
    small_raw = _all_gather_small(small_w)
    small_all = small_raw[0::2]
    alpha_full = small_all[:, N_META:].reshape(N_CHIPS, DEPTH, GLA_RANK, lay.gk // N_CHIPS)
    alpha_full = alpha_full.transpose(1, 2, 0, 3).reshape(DEPTH, GLA_RANK, lay.gk)
    groups = [(0, ("w_in",)), (0, _BIG[1:]), (1, _BIG)]
    started, after = [], small_raw
    for gi, (l, names) in enumerate(groups):
        own16 = [w[n][l].astype(BF16) for n in names]
        lands = [lax.empty((N_CHIPS,) + o.shape, BF16) for o in own16]
        started.append(_split_start("gather_start_%d" % gi, own16, lands, _gather_plan, after=after))
        after = started[gi][4]
    meta_full = small_all[:, :N_META].transpose(1, 0, 2).reshape(N_META, d) + after[0, 0]

    passing = {}

    def arrive(gi, after):
        send_sems, recv_sems, srcs, lands, _ = started[gi]
        srcs, lands = _split_wait("gather_wait_%d" % gi, send_sems, recv_sems, srcs, lands, _gather_plan, after)
        passing[gi] = (srcs, _exchange_start("gather_pass_%d" % gi, lands, _forward_plan, after=srcs[0]))

    def gathered(gi, after):
        if gi not in passing:
            arrive(gi, after)
        srcs, (send_sems, recv_sems, lands, _) = passing[gi]
        lands = _exchange_wait("gather_pass_wait_%d" % gi, send_sems, recv_sems, lands, _forward_plan, after)
        return {n: lax.dynamic_update_slice(g, o[None], (chip, 0, 0)) for n, g, o in zip(groups[gi][1], lands, srcs)}

    def early_weights(l, gl):
        w_alpha_p = jnp.zeros((LANES, lay.gk), BF16).at[FOX_HEADS:FOX_HEADS + GLA_RANK].set(
            alpha_full[l].astype(BF16))
        return dict(
            w_in=lay.to_p(gl["w_in"]), w_alpha_p=w_alpha_p,
            norm_mix_g=norm_mix_g[l][None], norm_mlp_g=norm_mlp_g[l][None], gla_norm_g=gla_norm_g[l][None],
            b_alpha=b_alpha[l][None],
            b_forget_p=jnp.pad(b_forget[l][None], ((0, 0), (0, LANES - FOX_HEADS))))

    def late_weights(gl):
        return dict(w_o_fox=gl["w_o_fox"], w_o_gla=_full_matrix("w_o_gla", gl["w_o_gla"]),
                    w_out=_full_matrix("w_out", gl["w_out"]), w_ff1=gl["w_ff1"],
                    w_ff2=_full_matrix("w_ff2", gl["w_ff2"]))

    h = jnp.pad(x[0], ((ROW0, 0), (0, 0))).at[PAD:ROW0].set(meta_full)
    layers, saved = [], []
    layers.append(early_weights(0, gathered(0, after=h)))
    h, s = _layer_fwd(lay, h, layers[0], t, hooks=dict(
        mixers=lambda after: arrive(1, after), late=lambda after: late_weights(gathered(1, after)),
        mlp=lambda after: arrive(2, after)))
    saved.append(s)
    gl = gathered(2, after=h)
    layers.append({**early_weights(1, gl), **late_weights(gl)})
    h, s = _layer_fwd(lay, h, layers[1], t)
    saved.append(s)
    dh, dg_final, loss_part = _loss_head(h, final_norm_g[None], jnp.pad(loss_target[0], ((ROW0, 0), (0, 0))), t, d)
    loss = lax.psum(loss_part[0, 0], ("x", "y", "c"))

    def partial_of(g, n):
        return g[n] if n in ("w_ff1", "w_o_fox", "w_in") else _shard_major(n, g[n])

    scatter_groups = dict(mlp=("w_ff1", "w_ff2"), out=("w_o_fox", "w_o_gla", "w_out"))
    scatter_groups["in"] = ("w_in",)
    swapping, scattered = [], {}

    def start_swap(l, grp, g, after=None):
        parts = [partial_of(g, n) for n in scatter_groups[grp]]
        lands = [lax.empty((p_.shape[0], p_.shape[1] // 2, p_.shape[2]), F32) for p_ in parts]
        started_swap = _split_start("swap_start_%d_%s" % (l, grp), parts, lands, _swap_plan, after=after)
        swapping.append((l, grp, started_swap))
        return started_swap[4]

    def start_scatter(after):
        l, grp, (send_sems, recv_sems, srcs, lands, _) = swapping.pop(0)
        names = scatter_groups[grp]
        tag = "%d_%s" % (l, grp)
        parts, from_sibling = _split_wait("swap_wait_" + tag, send_sems, recv_sems, srcs, lands, _swap_plan, after)
        sums = [_pair_sum("pair_sum_%d_%s" % (l, n), p_, r_, chip_idx, core_idx)
                for n, p_, r_ in zip(names, parts, from_sibling)]
        lands = [lax.empty(s16.shape, BF16) for s16 in sums]
        send_sems, recv_sems, srcs, lands, token = _split_start("scatter_start_" + tag, sums, lands, _scatter_plan)
        scattered[l, grp] = (send_sems, recv_sems, srcs, lands, parts, from_sibling)
        return token

    def gate(l, grp, g):
        token = start_swap(l, grp, g)
        if len(swapping) > 1:
            token = start_scatter(after=token)
        return token[0, 0]

    def gates_for(l, points):
        return {grp: functools.partial(gate, l, grp) for grp in points}

    grads = [None] * DEPTH
    dh, grads[1] = _layer_bwd(lay, dh, layers[1], saved[1], t, gates=gates_for(1, ("mlp", "out", "in")))
    (grad_x, head), grads[0] = _layer_bwd(lay, dh, layers[0], saved[0], t, gates=gates_for(0, ("mlp", "out")),
                                          first=True)
    d_meta = head[PAD:ROW0]

    stack = lambda key: jnp.concatenate([grads[l][key] for l in range(DEPTH)], axis=0)
    b_forget_g = jnp.concatenate([grads[l]["b_forget_p"][:, :FOX_HEADS] for l in range(DEPTH)], axis=0)
    alpha_g = jnp.stack([grads[l]["w_alpha_p"][FOX_HEADS:FOX_HEADS + GLA_RANK] for l in range(DEPTH)])
    packed = _pack_small(d, d_meta, stack("norm_mix_g"), stack("gla_norm_g"), stack("norm_mlp_g"), dg_final,
                         stack("b_alpha"), b_forget_g, alpha_g)
    small_g = _unpack_small(d, _sum_devices(_all_gather_small(packed)))
    small_g["meta"] = lax.dynamic_slice_in_dim(small_g["meta"], chip * (d // N_CHIPS), d // N_CHIPS, axis=1)
    alpha_shard = lax.dynamic_slice_in_dim(small_g["w_alpha2"], chip * (lay.gk // N_CHIPS), lay.gk // N_CHIPS, axis=2)

    after = start_swap(0, "in", grads[0], after=small_g["final_norm_g"])
    while swapping:
        after = start_scatter(after)
    outs = {n: None for n in _BIG}

    def reduce_and_share(l, grps, after):
        tag = "%d_%s" % (l, grps[0])
        names, sums = [], []
        for grp in grps:
            send_sems, recv_sems, srcs, lands, parts, from_sibling = scattered[l, grp]
            _, lands = _split_wait("scatter_wait_%d_%s" % (l, grp), send_sems, recv_sems, srcs, lands, _scatter_plan,
                                 after)
            names += scatter_groups[grp]
            sums += [_chip_sum("chip_sum_%d_%s" % (l, n), p_, r_, landed, chip_idx, core_idx)
                     for n, p_, r_, landed in zip(scatter_groups[grp], parts, from_sibling, lands)]
        return l, tag, names, _exchange_start("share_start_" + tag, sums, _share_plan, after=sums[-1])

    def update(shared, after):
        l, tag, names, (send_sems, recv_sems, sums, _) = shared
        sums = _exchange_wait("share_wait_" + tag, send_sems, recv_sems, sums, _share_plan, after)
        for n, g in zip(names, sums):
            outs[n] = _adamw("adamw_%d_%s" % (l, n), w[n], g, m[n], v[n], layer=l, into=outs[n])
        return outs[names[-1]][0]

    shared_1 = reduce_and_share(1, ("mlp", "out", "in"), after)
    shared_0 = reduce_and_share(0, ("mlp", "out"), shared_1[3][3])
    after = update(shared_1, shared_0[3][3])
    shared_in = reduce_and_share(0, ("in",), after)
    after = update(shared_0, shared_in[3][3])
    update(shared_in, after)
    out_g, out_d, out_m, out_v = {}, {}, {}, {}
    for n in _BIG:
        out_g[n], out_d[n], out_m[n], out_v[n] = outs[n]
    out_g["w_alpha2"], out_d["w_alpha2"], out_m["w_alpha2"], out_v["w_alpha2"] = _adamw(
        "adamw_w_alpha2", w["w_alpha2"], alpha_shard, m["w_alpha2"], v["w_alpha2"])
    sm_w = dict(meta_tokens=meta_tokens, norm_mix_g=norm_mix_g, b_forget=b_forget, b_alpha=b_alpha,
                gla_norm_g=gla_norm_g, norm_mlp_g=norm_mlp_g, final_norm_g=final_norm_g)
    sm_m = dict(meta_tokens=m_meta_tokens, norm_mix_g=m_norm_mix_g, b_forget=m_b_forget, b_alpha=m_b_alpha,
                gla_norm_g=m_gla_norm_g, norm_mlp_g=m_norm_mlp_g, final_norm_g=m_final_norm_g)
    sm_v = dict(meta_tokens=v_meta_tokens, norm_mix_g=v_norm_mix_g, b_forget=v_b_forget, b_alpha=v_b_alpha,
                gla_norm_g=v_gla_norm_g, norm_mlp_g=v_norm_mlp_g, final_norm_g=v_final_norm_g)
    sm_g = dict(meta_tokens=small_g["meta"], norm_mix_g=small_g["norm_mix_g"], b_forget=small_g["b_forget"],
                b_alpha=small_g["b_alpha"], gla_norm_g=small_g["gla_norm_g"], norm_mlp_g=small_g["norm_mlp_g"],
                final_norm_g=small_g["final_norm_g"])
    names_small = list(sm_w)
    sizes = [sm_w[n].size for n in names_small]
    width = 512
    total = -(-sum(sizes) // (8 * width)) * (8 * width)

    def pack_flat(dct, fill):
        flat = jnp.concatenate([dct[n].reshape(-1) for n in names_small])
        return jnp.pad(flat, (0, total - flat.shape[0]), constant_values=fill).reshape(1, -1, width)

    res = _adamw("adamw_small", pack_flat(sm_w, 0.0), pack_flat(sm_g, 0.0), pack_flat(sm_m, 0.0), pack_flat(sm_v, 1.0))
    offs = [0]
    for sz in sizes:
        offs.append(offs[-1] + sz)
    for i, n in enumerate(names_small):
        out_g[n] = sm_g[n].reshape(sm_w[n].shape)
        out_d[n], out_m[n], out_v[n] = [r.reshape(-1)[offs[i]:offs[i + 1]].reshape(sm_w[n].shape) for r in res[1:]]

    order = ["meta_tokens", "norm_mix_g", "w_in", "b_forget", "w_alpha2", "b_alpha", "gla_norm_g", "w_o_fox",
             "w_o_gla", "w_out", "norm_mlp_g", "w_ff1", "w_ff2", "final_norm_g"]
    return (loss, grad_x[None], *[out_g[n] for n in order], *[out_d[n] for n in order],
            *[out_m[n] for n in order], *[out_v[n] for n in order])
```

```python
import functools

import numpy as np

import jax
import jax.numpy as jnp
from jax import lax
from jax.experimental import pallas as pl
from jax.experimental.pallas import tpu as pltpu

F32 = jnp.float32
BF16 = jnp.bfloat16

N_META = 16
PAD = 112
ROW0 = PAD + N_META
EPS = 1e-6
MASK_VALUE = -1e30
FOX_HEADS = 8
FOX_GROUP = 2
GLA_HEADS = 4
GLA_RANK = 16
GLA_TAU = 16.0
GLA_CHUNK = 64
DEPTH = 2
N_CHIPS = 4
N_DEV = 8

ADAM_LR = 0.001
ADAM_B1 = 0.9
ADAM_B2 = 0.999
ADAM_EPS = 1e-08
ADAM_WD = 0.01
ADAM_STEP = 10

LANES = 128
VMEM_LIMIT = 56 * 1024 * 1024
MESH = pl.DeviceIdType.MESH


def _pick(n, target, mult):
    best = None
    for d in range(mult, min(n, target) + 1, mult):
        if n % d == 0:
            best = d
    return n if best is None else best


def _params(sem=None):
    return pltpu.CompilerParams(dimension_semantics=sem, vmem_limit_bytes=VMEM_LIMIT)


def _bf(v):
    return v if v.dtype == BF16 else v.astype(BF16)


def _sigmoid(z):
    return 1.0 / (1.0 + jnp.exp(-z))


def _log_sigmoid(z):
    return jnp.minimum(z, 0.0) - jnp.log(1.0 + jnp.exp(-jnp.abs(z)))


def _split3(v):
    a = v.astype(BF16)
    r = v - a.astype(F32)
    b = r.astype(BF16)
    c = (r - b.astype(F32)).astype(BF16)
    return a, b, c


def _dot(a, b, dims):
    return lax.dot_general(a, b, (dims, ((), ())), preferred_element_type=F32)


NN = ((1,), (0,))
NT = ((1,), (1,))
TN = ((0,), (0,))


def _tri_dot(tri, v, dims=NN):
    a, b, c = _split3(v)
    return _dot(tri, a, dims) + _dot(tri, b, dims) + _dot(tri, c, dims)


def _mm(name, a, b, *, mode, m, n, k, b_c0=0, extras=(), epilogue=None, out_dtypes=(F32,),
        b_shards=1, out_shards=1, tm=1056, tn=1024, tk=2048):
    tm = _pick(m, tm, LANES if mode == "tn" else 16)
    tn = _pick(n // max(b_shards if mode == "nn" else 1, out_shards), tn, LANES)
    if mode == "tn":
        tk = _pick(k, 2112, 16)
    else:
        tk = _pick(k // (b_shards if mode == "nt" else 1), tk, LANES)
    assert b_c0 % (tk if mode == "nt" else tn) == 0 and (b_shards == 1 or b_c0 == 0)
    nk = k // tk
    if mode == "tn":
        a_spec = pl.BlockSpec((tk, tm), lambda i, j, kk: (kk, i))
    else:
        a_spec = pl.BlockSpec((tm, tk), lambda i, j, kk: (i, kk))
    if mode == "nt":
        dims = NT
        if b_shards > 1:
            per = (k // b_shards) // tk
            b_spec = pl.BlockSpec((None, tn, tk), lambda i, j, kk: (kk // per, j, kk % per))
        else:
            b_spec = pl.BlockSpec((tn, tk), lambda i, j, kk: (j, kk + b_c0 // tk))
    else:
        dims = NN if mode == "nn" else TN
        if b_shards > 1:
            per = (n // b_shards) // tn
            b_spec = pl.BlockSpec((None, tk, tn), lambda i, j, kk: (j // per, kk, j % per))
        else:
            b_spec = pl.BlockSpec((tk, tn), lambda i, j, kk: (kk, j + b_c0 // tn))
    ex_specs = [pl.BlockSpec((tm, tn), lambda i, j, kk: (i, j)) for _ in extras]
    if out_shards > 1:
        oper = (n // out_shards) // tn
        out_specs = [pl.BlockSpec((None, tm, tn), lambda i, j, kk: (j // oper, i, j % oper)) for _ in out_dtypes]
        out_shape = [jax.ShapeDtypeStruct((out_shards, m, n // out_shards), dt) for dt in out_dtypes]
    else:
        out_specs = [pl.BlockSpec((tm, tn), lambda i, j, kk: (i, j)) for _ in out_dtypes]
        out_shape = [jax.ShapeDtypeStruct((m, n), dt) for dt in out_dtypes]
    n_ex = len(extras)
    n_out = len(out_dtypes)

    def finish(acc, ex_refs, out_refs):
        vals = (acc,) if epilogue is None else epilogue(acc, *[r[...] for r in ex_refs])
        for r, v in zip(out_refs, vals):
            r[...] = v.astype(r.dtype)

    def body(a_ref, b_ref, *rest):
        ex_refs = rest[:n_ex]
        out_refs = rest[n_ex:n_ex + n_out]
        prod = _dot(_bf(a_ref[...]), _bf(b_ref[...]), dims)
        if nk == 1:
            finish(prod, ex_refs, out_refs)
            return
        acc_ref = rest[n_ex + n_out]
        kk = pl.program_id(2)

        @pl.when(kk == 0)
        def _():
            acc_ref[...] = prod

        @pl.when((kk > 0) & (kk < nk - 1))
        def _():
            acc_ref[...] += prod

        @pl.when(kk == nk - 1)
        def _():
            finish(acc_ref[...] + prod, ex_refs, out_refs)

    outs = pl.pallas_call(
        body,
        name=name,
        grid=(m // tm, n // tn, nk),
        in_specs=[a_spec, b_spec] + ex_specs,
        out_specs=out_specs,
        out_shape=out_shape,
        scratch_shapes=[pltpu.VMEM((tm, tn), F32)] if nk > 1 else [],
        compiler_params=_params(("parallel", "parallel", "arbitrary")),
    )(a, b, *extras)
    return outs[0] if n_out == 1 else outs


def _mm_nt2(name, a1, c1, a2, c2, b, prev, *, m, n, tm=528, tn=1024):
    k1, k2 = a1.shape[1], a2.shape[1]
    assert c1 % k1 == 0 and c2 % k2 == 0
    tm = _pick(m, tm, 16)
    tn = _pick(n, tn, LANES)

    def body(a1_ref, a2_ref, b1_ref, b2_ref, *rest):
        acc = _dot(a1_ref[...], b1_ref[...], NT) + _dot(a2_ref[...], b2_ref[...], NT)
        if prev is not None:
            acc = rest[0][...] + acc
        rest[-1][...] = acc

    tile = pl.BlockSpec((tm, tn), lambda i, j: (i, j))
    in_specs = [pl.BlockSpec((tm, k1), lambda i, j: (i, 0)), pl.BlockSpec((tm, k2), lambda i, j: (i, 0)),
                pl.BlockSpec((tn, k1), lambda i, j: (j, c1 // k1)), pl.BlockSpec((tn, k2), lambda i, j: (j, c2 // k2))]
    args = [a1, a2, b, b]
    if prev is not None:
        in_specs.append(tile)
        args.append(prev)
    return pl.pallas_call(
        body, name=name, grid=(m // tm, n // tn), in_specs=in_specs, out_specs=tile,
        out_shape=jax.ShapeDtypeStruct((m, n), F32),
        compiler_params=_params(("parallel", "parallel")),
    )(*args)


def _ew(name, fn, ins, outs, rows, tm):
    tm = _pick(rows, tm, 16)
    in_specs, args = [], []
    for spec in ins:
        if spec[0] == "tile":
            _, arr, width, c0 = spec
            assert c0 % width == 0
            in_specs.append(pl.BlockSpec((tm, width), functools.partial(lambda i, o: (i, o), o=c0 // width)))
        else:
            arr = spec[1]
            in_specs.append(pl.BlockSpec(arr.shape, lambda i: (0, 0)))
        args.append(arr)
    out_specs, out_shape = [], []
    for kind, dt, width in outs:
        if kind == "tile":
            out_specs.append(pl.BlockSpec((tm, width), lambda i: (i, 0)))
            out_shape.append(jax.ShapeDtypeStruct((rows, width), dt))
        else:
            out_specs.append(pl.BlockSpec((1, width), lambda i: (0, 0)))
            out_shape.append(jax.ShapeDtypeStruct((1, width), dt))
    n_in = len(ins)
    has_acc = any(o[0] == "acc" for o in outs)

    def body(*refs):
        i = pl.program_id(0)
        vals = fn(i * tm, *[r[...] for r in refs[:n_in]])
        for (kind, _, _), r, v in zip(outs, refs[n_in:], vals):
            if kind == "tile":
                r[...] = v.astype(r.dtype)
            else:
                @pl.when(i == 0)
                def _():
                    r[...] = jnp.zeros_like(r)

                r[...] += v.astype(r.dtype)

    res = pl.pallas_call(
        body,
        name=name,
        grid=(rows // tm,),
        in_specs=in_specs,
        out_specs=out_specs,
        out_shape=out_shape,
        compiler_params=_params(("arbitrary",) if has_acc else ("parallel",)),
    )(*args)
    return res[0] if len(outs) == 1 else res


def _row_ids(row0, tm):
    return row0 + lax.broadcasted_iota(jnp.int32, (tm, 1), 0)


def _colsum(v):
    return jnp.sum(v, axis=0, keepdims=True)


def _rms_fwd_t(name, h, g, t, d):
    tm = _pick(t, 384, LANES)

    def body(x_ref, g_ref, y_ref, yt_ref):
        x = x_ref[...]
        y = x * lax.rsqrt(jnp.mean(x * x, axis=-1, keepdims=True) + EPS) * g_ref[...]
        y_ref[...] = y.astype(BF16)
        yt_ref[...] = y.T.astype(BF16)

    return pl.pallas_call(
        body, name=name, grid=(t // tm,),
        in_specs=[pl.BlockSpec((tm, d), lambda i: (i, 0)), pl.BlockSpec((1, d), lambda i: (0, 0))],
        out_specs=[pl.BlockSpec((tm, d), lambda i: (i, 0)), pl.BlockSpec((d, tm), lambda i: (0, i))],
        out_shape=[jax.ShapeDtypeStruct((t, d), BF16), jax.ShapeDtypeStruct((d, t), BF16)],
        compiler_params=_params(("parallel",)),
    )(h, g)


def _rms_bwd(name, h, g, dy, dres, t, d):
    def fn(row0, x, gg, dyv, dr):
        r = lax.rsqrt(jnp.mean(x * x, axis=-1, keepdims=True) + EPS)
        xh = x * r
        dxh = dyv * gg
        dx = r * (dxh - xh * jnp.mean(dxh * xh, axis=-1, keepdims=True))
        out = jnp.where(_row_ids(row0, x.shape[0]) >= PAD, dr + dx, 0.0)
        return out, _colsum(dyv * xh)

    return _ew(name, fn, [("tile", h, d, 0), ("full", g), ("tile", dy, d, 0), ("tile", dres, d, 0)],
               [("tile", F32, d), ("acc", F32, d)], t, 264)


def _rms_bwd_x(name, h, g, dy, dres, t, d):
    tm = ROW0

    def body(x_ref, g_ref, dy_ref, dr_ref, dx_ref, head_ref, dg_ref):
        i = pl.program_id(0)
        x = x_ref[...]
        r = lax.rsqrt(jnp.mean(x * x, axis=-1, keepdims=True) + EPS)
        xh = x * r
        dxh = dy_ref[...] * g_ref[...]
        dx = r * (dxh - xh * jnp.mean(dxh * xh, axis=-1, keepdims=True))
        out = jnp.where(_row_ids(i * tm, tm) >= PAD, dr_ref[...] + dx, 0.0)

        @pl.when(i == 0)
        def _():
            head_ref[...] = out
            dg_ref[...] = jnp.zeros_like(dg_ref)

        dx_ref[...] = out
        dg_ref[...] += _colsum(dy_ref[...] * xh)

    tile = pl.BlockSpec((tm, d), lambda i: (i, 0))
    fixed = lambda shape: pl.BlockSpec(shape, lambda i: (0, 0))
    return pl.pallas_call(
        body, name=name, grid=(t // tm,),
        in_specs=[tile, fixed((1, d)), tile, tile],
        out_specs=[pl.BlockSpec((tm, d), lambda i: (jnp.maximum(i - 1, 0), 0)), fixed((tm, d)), fixed((1, d))],
        out_shape=[jax.ShapeDtypeStruct((t - ROW0, d), F32), jax.ShapeDtypeStruct((ROW0, d), F32),
                   jax.ShapeDtypeStruct((1, d), F32)],
        compiler_params=_params(("arbitrary",)),
    )(h, g, dy, dres)


def _loss_head(h, g, target_p, t, d):
    def fn(row0, x, gg, tgt):
        real = _row_ids(row0, x.shape[0]) >= ROW0
        r = lax.rsqrt(jnp.mean(x * x, axis=-1, keepdims=True) + EPS)
        xh = x * r
        err = jnp.where(real, xh * gg - tgt, 0.0)
        loss_rows = 0.5 * jnp.mean(err * err, axis=-1, keepdims=True)
        dyv = err * (1.0 / d)
        dxh = dyv * gg
        dx = r * (dxh - xh * jnp.mean(dxh * xh, axis=-1, keepdims=True))
        loss_part = jnp.sum(loss_rows, axis=0, keepdims=True) * jnp.ones((1, LANES), F32)
        return jnp.where(real, dx, 0.0), _colsum(dyv * xh), loss_part

    return _ew("loss_head", fn, [("tile", h, d, 0), ("full", g), ("tile", target_p, d, 0)],
               [("tile", F32, d), ("acc", F32, d), ("acc", F32, LANES)], t, 264)


def _merge_fwd(a_fox, a_gla, proj, c_gates, t, d):
    def fn(row0, af, ag, gates):
        gates = gates.astype(F32)
        return (_sigmoid(gates[:, :d]) * af + _sigmoid(gates[:, d:]) * ag,)

    return _ew("merge_fwd", fn, [("tile", a_fox, d, 0), ("tile", a_gla, d, 0), ("tile", proj, 2 * d, c_gates)],
               [("tile", BF16, d)], t, 264)


def _merge_bwd(dy, a_fox, a_gla, proj, c_gates, t, d):
    def fn(row0, dyv, af, ag, gates):
        gates = gates.astype(F32)
        sf = _sigmoid(gates[:, :d])
        sg = _sigmoid(gates[:, d:])
        dgates = jnp.concatenate([dyv * af * sf * (1.0 - sf), dyv * ag * sg * (1.0 - sg)], axis=1)
        return dyv * sf, dyv * sg, dgates

    return _ew("merge_bwd", fn,
               [("tile", dy, d, 0), ("tile", a_fox, d, 0), ("tile", a_gla, d, 0), ("tile", proj, 2 * d, c_gates)],
               [("tile", BF16, d), ("tile", BF16, d), ("tile", BF16, 2 * d)], t, 264)


def _fox_gate_fwd(small, b_forget_p, t):
    tb = _pick(t, 384, LANES)

    def body(s_ref, b_ref, c_ref, carry_ref):
        i = pl.program_id(0)

        @pl.when(i == 0)
        def _():
            carry_ref[...] = jnp.zeros_like(carry_ref)

        logf = _log_sigmoid(s_ref[...] + b_ref[...])
        logf = jnp.where(_row_ids(i * tb, tb) >= PAD, logf, 0.0)
        r = lax.broadcasted_iota(jnp.int32, (tb, tb), 0)
        c = lax.broadcasted_iota(jnp.int32, (tb, tb), 1)
        tri = (c <= r).astype(BF16)
        cs = _tri_dot(tri, logf) + carry_ref[...]
        c_ref[...] = cs
        carry_ref[...] = cs[tb - 1:tb, :]

    return pl.pallas_call(
        body, name="fox_gate_fwd", grid=(t // tb,),
        in_specs=[pl.BlockSpec((tb, LANES), lambda i: (i, 0)), pl.BlockSpec((1, LANES), lambda i: (0, 0))],
        out_specs=pl.BlockSpec((tb, LANES), lambda i: (i, 0)),
        out_shape=jax.ShapeDtypeStruct((t, LANES), F32),
        scratch_shapes=[pltpu.VMEM((1, LANES), F32)],
        compiler_params=_params(("arbitrary",)),
    )(small, b_forget_p)


def _fox_gate_bwd(dc, small, b_forget_p, dga, t):
    tb = _pick(t, 384, LANES)
    nb = t // tb

    def body(dc_ref, s_ref, b_ref, dga_ref, ds_ref, db_ref, carry_ref):
        i = pl.program_id(0)

        @pl.when(i == 0)
        def _():
            carry_ref[...] = jnp.zeros_like(carry_ref)
            db_ref[...] = jnp.zeros_like(db_ref)

        r = lax.broadcasted_iota(jnp.int32, (tb, tb), 0)
        c = lax.broadcasted_iota(jnp.int32, (tb, tb), 1)
        tri = (c >= r).astype(BF16)
        dlogf = _tri_dot(tri, dc_ref[...]) + carry_ref[...]
        carry_ref[...] = dlogf[0:1, :]
        z = s_ref[...] + b_ref[...]
        dff = dlogf * _sigmoid(-z)
        lane = lax.broadcasted_iota(jnp.int32, (tb, LANES), 1)
        keep = (_row_ids((nb - 1 - i) * tb, tb) >= PAD) & (lane < FOX_HEADS)
        dff = jnp.where(keep, dff, 0.0)
        ds_ref[...] = dff + dga_ref[...]
        db_ref[...] += _colsum(dff)

    rev = lambda i: (nb - 1 - i, 0)
    return pl.pallas_call(
        body, name="fox_gate_bwd", grid=(nb,),
        in_specs=[pl.BlockSpec((tb, LANES), rev), pl.BlockSpec((tb, LANES), rev),
                  pl.BlockSpec((1, LANES), lambda i: (0, 0)), pl.BlockSpec((tb, LANES), rev)],
        out_specs=[pl.BlockSpec((tb, LANES), rev), pl.BlockSpec((1, LANES), lambda i: (0, 0))],
        out_shape=[jax.ShapeDtypeStruct((t, LANES), F32), jax.ShapeDtypeStruct((1, LANES), F32)],
        scratch_shapes=[pltpu.VMEM((1, LANES), F32)],
        compiler_params=_params(("arbitrary",)),
    )(dc, small, b_forget_p, dga)


def _fox_pairs(nb, by_key):
    if by_key:
        pairs = [(qi, ki) for ki in range(nb) for qi in range(ki, nb)]
    else:
        pairs = [(qi, ki) for qi in range(nb) for ki in range(qi + 1)]
    return (jnp.asarray(np.array([p[0] for p in pairs], np.int32)),
            jnp.asarray(np.array([p[1] for p in pairs], np.int32)), len(pairs))


def _fox_specs(tb, fd, c_fq, c_fkv):
    gw = FOX_GROUP * fd
    q0, kv0 = c_fq // gw, c_fkv // (2 * gw)
    return dict(
        q=pl.BlockSpec((tb, gw), lambda g, p, qt, kt: (qt[p], q0 + g)),
        kv=pl.BlockSpec((tb, 2 * gw), lambda g, p, qt, kt: (kt[p], kv0 + g)),
        col=pl.BlockSpec((FOX_GROUP, tb, 1), lambda g, p, qt, kt: (g, qt[p], 0)),
        row=pl.BlockSpec((FOX_GROUP, 1, tb), lambda g, p, qt, kt: (g, 0, kt[p])),
        head=pl.BlockSpec((tb, gw), lambda g, p, qt, kt: (qt[p], g)),
        key_kv=pl.BlockSpec((tb, 2 * gw), lambda g, p, qt, kt: (kt[p], g)),
    )


def _fox_mask(qi, ki, tb):
    row = qi * tb + lax.broadcasted_iota(jnp.int32, (tb, tb), 0)
    col = ki * tb + lax.broadcasted_iota(jnp.int32, (tb, tb), 1)
    return (col <= row) & (col >= PAD)


def _fox_heads(q_ref, kv_ref, fd):
    return [(q_ref[:, hh * fd:(hh + 1) * fd], kv_ref[:, 2 * hh * fd:(2 * hh + 1) * fd],
             kv_ref[:, (2 * hh + 1) * fd:(2 * hh + 2) * fd]) for hh in range(FOX_GROUP)]


def _fox_fwd(proj, c_col, c_row, t, fd, c_fq, c_fkv):
    tb = _pick(t, 384, LANES)
    nb = t // tb
    scale = fd ** -0.5
    sp = _fox_specs(tb, fd, c_fq, c_fkv)
    qt, kt, npairs = _fox_pairs(nb, by_key=False)

    def body(qt_ref, kt_ref, q_ref, kv_ref, cq_ref, ck_ref, o_ref, lse_ref, m_ref, l_ref, acc_ref):
        p = pl.program_id(1)
        qi, ki = qt_ref[p], kt_ref[p]

        @pl.when(ki == 0)
        def _():
            m_ref[...] = jnp.full_like(m_ref, -jnp.inf)
            l_ref[...] = jnp.zeros_like(l_ref)
            acc_ref[...] = jnp.zeros_like(acc_ref)

        def update(masked):
            mask = _fox_mask(qi, ki, tb) if masked else None
            for hh, (q, k, v) in enumerate(_fox_heads(q_ref, kv_ref, fd)):
                s = _dot(q, k, NT) * scale + cq_ref[hh] - ck_ref[hh]
                if masked:
                    s = jnp.where(mask, s, MASK_VALUE)
                m_prev = m_ref[hh]
                m_new = jnp.maximum(m_prev, jnp.max(s, axis=-1, keepdims=True))
                alpha = jnp.exp(m_prev - m_new)
                pe = jnp.exp(s - m_new)
                l_ref[hh] = alpha * l_ref[hh] + jnp.sum(pe, axis=-1, keepdims=True)
                acc_ref[hh] = alpha * acc_ref[hh] + _dot(pe.astype(BF16), v, NN)
                m_ref[hh] = m_new

        edge = (ki == 0) | (ki == qi)
        pl.when(edge)(functools.partial(update, True))
        pl.when(jnp.logical_not(edge))(functools.partial(update, False))

        @pl.when(ki == qi)
        def _():
            real = _row_ids(qi * tb, tb) >= PAD
            for hh in range(FOX_GROUP):
                o_ref[:, hh * fd:(hh + 1) * fd] = jnp.where(real, acc_ref[hh] / l_ref[hh], 0.0)
                lse_ref[hh] = m_ref[hh] + jnp.log(l_ref[hh])

    return pl.pallas_call(
        body, name="fox_fwd",
        grid_spec=pltpu.PrefetchScalarGridSpec(
            num_scalar_prefetch=2, grid=(FOX_HEADS // FOX_GROUP, npairs),
            in_specs=[sp["q"], sp["kv"], sp["col"], sp["row"]],
            out_specs=[sp["head"], sp["col"]],
            scratch_shapes=[pltpu.VMEM((FOX_GROUP, tb, 1), F32), pltpu.VMEM((FOX_GROUP, tb, 1), F32),
                            pltpu.VMEM((FOX_GROUP, tb, fd), F32)]),
        out_shape=[jax.ShapeDtypeStruct((t, FOX_HEADS * fd), F32), jax.ShapeDtypeStruct((FOX_HEADS, t, 1), F32)],
        compiler_params=_params(("parallel", "arbitrary")),
    )(qt, kt, proj, proj, c_col, c_row)


def _fox_delta(o_fox, do_fox, t, fd):
    tb = _pick(t, 384, LANES)

    def body(o_ref, do_ref, out_ref):
        for h in range(FOX_HEADS):
            sl = slice(h * fd, (h + 1) * fd)
            out_ref[h] = jnp.sum(o_ref[:, sl] * do_ref[:, sl].astype(BF16).astype(F32), axis=-1, keepdims=True)

    w = FOX_HEADS * fd
    return pl.pallas_call(
        body, name="fox_delta", grid=(t // tb,),
        in_specs=[pl.BlockSpec((tb, w), lambda i: (i, 0)), pl.BlockSpec((tb, w), lambda i: (i, 0))],
        out_specs=pl.BlockSpec((FOX_HEADS, tb, 1), lambda i: (0, i, 0)),
        out_shape=jax.ShapeDtypeStruct((FOX_HEADS, t, 1), F32),
        compiler_params=_params(("parallel",)),
    )(o_fox, do_fox)


def _fox_bwd(proj, c_col, c_row, lse, delta, do_fox, t, fd, c_fq, c_fkv):
    tb = _pick(t, 384, LANES)
    nb = t // tb
    scale = fd ** -0.5
    sp = _fox_specs(tb, fd, c_fq, c_fkv)
    qt, kt, npairs = _fox_pairs(nb, by_key=True)
    gw = FOX_GROUP * fd

    def body(qt_ref, kt_ref, q_ref, kv_ref, cq_ref, ck_ref, lse_ref, dl_ref, do_ref, dq_ref, dkv_ref, dc_ref, dr_ref,
             dq_acc, dk_acc, dv_acc, dc_acc, dr_acc):
        p = pl.program_id(1)
        qi, ki = qt_ref[p], kt_ref[p]

        @pl.when(p == 0)
        def _():
            dq_acc[...] = jnp.zeros_like(dq_acc)
            dr_acc[...] = jnp.zeros_like(dr_acc)

        @pl.when(qi == ki)
        def _():
            dk_acc[...] = jnp.zeros_like(dk_acc)
            dv_acc[...] = jnp.zeros_like(dv_acc)
            dc_acc[...] = jnp.zeros_like(dc_acc)

        rows = pl.ds(pl.multiple_of(qi * tb, LANES), tb)

        def update(masked):
            mask = _fox_mask(qi, ki, tb) if masked else None
            for hh, (q, k, v) in enumerate(_fox_heads(q_ref, kv_ref, fd)):
                do = _bf(do_ref[:, hh * fd:(hh + 1) * fd])
                s = _dot(q, k, NT) * scale + cq_ref[hh] - ck_ref[hh]
                if masked:
                    s = jnp.where(mask, s, MASK_VALUE)
                pr = jnp.exp(s - lse_ref[hh])
                dp = _dot(do, v, NT)
                ds = pr * (dp - dl_ref[hh])
                ds16 = ds.astype(BF16)
                dv_acc[hh] += _dot(pr.astype(BF16), do, TN)
                dk_acc[hh] += _dot(ds16, q, TN)
                dc_acc[hh] += _colsum(ds)
                dr_acc[hh, rows, :] += jnp.sum(ds, axis=-1, keepdims=True)
                dq_acc[hh, rows, :] += _dot(ds16, k, NN)

        edge = (ki == 0) | (ki == qi)
        pl.when(edge)(functools.partial(update, True))
        pl.when(jnp.logical_not(edge))(functools.partial(update, False))

        @pl.when(qi == nb - 1)
        def _():
            for hh in range(FOX_GROUP):
                dkv_ref[:, 2 * hh * fd:(2 * hh + 1) * fd] = (dk_acc[hh] * scale).astype(dkv_ref.dtype)
                dkv_ref[:, (2 * hh + 1) * fd:(2 * hh + 2) * fd] = dv_acc[hh].astype(dkv_ref.dtype)
                dc_ref[hh] = -dc_acc[hh]

        @pl.when(p == npairs - 1)
        def _():
            for hh in range(FOX_GROUP):
                dq_ref[:, hh * fd:(hh + 1) * fd] = (dq_acc[hh] * scale).astype(dq_ref.dtype)
            dr_ref[...] = dr_acc[...]

    return pl.pallas_call(
        body, name="fox_bwd",
        grid_spec=pltpu.PrefetchScalarGridSpec(
            num_scalar_prefetch=2, grid=(FOX_HEADS // FOX_GROUP, npairs),
            in_specs=[sp["q"], sp["kv"], sp["col"], sp["row"], sp["col"], sp["col"], sp["head"]],
            out_specs=[pl.BlockSpec((t, gw), lambda g, p, qt, kt: (0, g)), sp["key_kv"], sp["row"],
                       pl.BlockSpec((FOX_GROUP, t, 1), lambda g, p, qt, kt: (g, 0, 0))],
            scratch_shapes=[pltpu.VMEM((FOX_GROUP, t, fd), F32), pltpu.VMEM((FOX_GROUP, tb, fd), F32),
                            pltpu.VMEM((FOX_GROUP, tb, fd), F32), pltpu.VMEM((FOX_GROUP, 1, tb), F32),
                            pltpu.VMEM((FOX_GROUP, t, 1), F32)]),
        out_shape=[jax.ShapeDtypeStruct((t, FOX_HEADS * fd), BF16), jax.ShapeDtypeStruct((t, 2 * FOX_HEADS * fd), BF16),
                   jax.ShapeDtypeStruct((FOX_HEADS, 1, t), F32), jax.ShapeDtypeStruct((FOX_HEADS, t, 1), F32)],
        compiler_params=_params(("parallel", "arbitrary")),
    )(qt, kt, proj, proj, c_col, c_row, lse, delta, do_fox)


def _gla_gate_fwd(small, w_alpha_p, b_alpha, t, gk):
    def fn(row0, s, w, b):
        z = _dot(s.astype(BF16), w, NN) + b
        return (jnp.where(_row_ids(row0, s.shape[0]) >= PAD, _log_sigmoid(z) * (1.0 / GLA_TAU), 0.0),)

    return _ew("gla_gate_fwd", fn, [("tile", small, LANES, 0), ("full", w_alpha_p), ("full", b_alpha)],
               [("tile", F32, gk)], t, 264)


def _gla_gate_bwd(dglog, small, w_alpha_p, b_alpha, t, gk):
    def fn(row0, dg, s, w, b):
        z = _dot(s.astype(BF16), w, NN) + b
        dz = jnp.where(_row_ids(row0, s.shape[0]) >= PAD, dg * (1.0 / GLA_TAU) * _sigmoid(-z), 0.0)
        return dz, _colsum(dz)

    return _ew("gla_gate_bwd", fn,
               [("tile", dglog, gk, 0), ("tile", small, LANES, 0), ("full", w_alpha_p), ("full", b_alpha)],
               [("tile", BF16, gk), ("acc", F32, gk)], t, 264)


def _gla_chunk(q, k, g, scale, cs):
    r = lax.broadcasted_iota(jnp.int32, (cs, cs), 0)
    c = lax.broadcasted_iota(jnp.int32, (cs, cs), 1)
    causal = c <= r
    b = _tri_dot(causal.astype(BF16), g)
    bl = b[cs - 1:cs, :]
    eb, einv, eend = jnp.exp(b), jnp.exp(-b), jnp.exp(bl - b)
    qd = q.astype(F32) * scale * eb
    kf = k.astype(F32)
    return causal, (eb, einv, eend), bl, qd, kf * einv, kf * eend


def _gla_fwd(proj, glog, t, dk, dv, c_q, c_k, c_v):
    cs = GLA_CHUNK
    nc = t // cs
    wk, wv = GLA_HEADS * dk, GLA_HEADS * dv
    scale = dk ** -0.5

    def body(q_ref, k_ref, v_ref, g_ref, o_ref, sp_ref, st_ref):
        @pl.when(pl.program_id(0) == 0)
        def _():
            st_ref[...] = jnp.zeros_like(st_ref)

        for h in range(GLA_HEADS):
            ks, vs = slice(h * dk, (h + 1) * dk), slice(h * dv, (h + 1) * dv)
            v = v_ref[:, vs]
            causal, _, bl, qd, ki, ke = _gla_chunk(q_ref[:, ks], k_ref[:, ks], g_ref[:, ks], scale, cs)
            st = st_ref[h]
            sp_ref[h] = st
            a = jnp.where(causal, _dot(qd.astype(BF16), ki.astype(BF16), NT), 0.0)
            o_ref[:, vs] = _dot(a.astype(BF16), v, NN) + _dot(qd.astype(BF16), st.astype(BF16), NT)
            st_ref[h] = st * jnp.exp(bl) + _dot(v, ke.astype(BF16), TN)

    return pl.pallas_call(
        body, name="gla_fwd", grid=(nc,),
        in_specs=[pl.BlockSpec((cs, wk), lambda n: (n, c_q // wk)), pl.BlockSpec((cs, wk), lambda n: (n, c_k // wk)),
                  pl.BlockSpec((cs, wv), lambda n: (n, c_v // wv)), pl.BlockSpec((cs, wk), lambda n: (n, 0))],
        out_specs=[pl.BlockSpec((cs, wv), lambda n: (n, 0)),
                   pl.BlockSpec((None, GLA_HEADS, dv, dk), lambda n: (n, 0, 0, 0))],
        out_shape=[jax.ShapeDtypeStruct((t, wv), F32), jax.ShapeDtypeStruct((nc, GLA_HEADS, dv, dk), F32)],
        scratch_shapes=[pltpu.VMEM((GLA_HEADS, dv, dk), F32)],
        compiler_params=_params(("arbitrary",)),
    )(proj, proj, proj, glog)


def _gla_bwd(proj, glog, s_prev, do_raw, t, dk, dv, c_q, c_k, c_v):
    cs = GLA_CHUNK
    nc = t // cs
    wk, wv = GLA_HEADS * dk, GLA_HEADS * dv
    scale = dk ** -0.5

    def body(q_ref, k_ref, v_ref, g_ref, sp_ref, do_ref, dq_ref, dk_ref, dv_ref, dg_ref, dst_ref):
        @pl.when(pl.program_id(0) == 0)
        def _():
            dst_ref[...] = jnp.zeros_like(dst_ref)

        for h in range(GLA_HEADS):
            ks, vs = slice(h * dk, (h + 1) * dk), slice(h * dv, (h + 1) * dv)
            v = v_ref[:, vs]
            do = do_ref[:, vs].astype(BF16)
            causal, (eb, einv, eend), bl, qd, ki, ke = _gla_chunk(q_ref[:, ks], k_ref[:, ks], g_ref[:, ks], scale, cs)
            qd16, ki16, ke16 = qd.astype(BF16), ki.astype(BF16), ke.astype(BF16)
            st = sp_ref[h]
            dst = dst_ref[h]
            dst16 = dst.astype(BF16)
            a = jnp.where(causal, _dot(qd16, ki16, NT), 0.0).astype(BF16)
            da = jnp.where(causal, _dot(do, v, NT), 0.0).astype(BF16)
            dvv = _dot(a, do, TN) + _dot(ke16, dst16, NT)
            dqd = _dot(da, ki16, NN) + _dot(do, st.astype(BF16), NN)
            dki = _dot(da, qd16, TN)
            dke = _dot(v, dst16, NN)
            dl = jnp.exp(bl)
            ddl = _colsum(dst * st)
            dst_ref[h] = dst * dl + _dot(do, qd16, TN)
            dq_ref[:, ks] = (dqd * eb * scale).astype(dq_ref.dtype)
            dk_ref[:, ks] = (dki * einv + dke * eend).astype(dk_ref.dtype)
            dv_ref[:, vs] = dvv.astype(dv_ref.dtype)
            db = dqd * qd - dki * ki - dke * ke
            db_last = _colsum(dke * ke) + ddl * dl
            r = lax.broadcasted_iota(jnp.int32, (cs, cs), 0)
            c = lax.broadcasted_iota(jnp.int32, (cs, cs), 1)
            dg_ref[:, ks] = _tri_dot((c >= r).astype(BF16), db) + db_last

    rev = lambda f: (lambda n: f(nc - 1 - n))
    return pl.pallas_call(
        body, name="gla_bwd", grid=(nc,),
        in_specs=[pl.BlockSpec((cs, wk), rev(lambda n: (n, c_q // wk))), pl.BlockSpec((cs, wk), rev(lambda n: (n, c_k // wk))),
                  pl.BlockSpec((cs, wv), rev(lambda n: (n, c_v // wv))), pl.BlockSpec((cs, wk), rev(lambda n: (n, 0))),
                  pl.BlockSpec((None, GLA_HEADS, dv, dk), rev(lambda n: (n, 0, 0, 0))),
                  pl.BlockSpec((cs, wv), rev(lambda n: (n, 0)))],
        out_specs=[pl.BlockSpec((cs, wk), rev(lambda n: (n, 0))), pl.BlockSpec((cs, wk), rev(lambda n: (n, 0))),
                   pl.BlockSpec((cs, wv), rev(lambda n: (n, 0))), pl.BlockSpec((cs, wk), rev(lambda n: (n, 0)))],
        out_shape=[jax.ShapeDtypeStruct((t, wk), BF16), jax.ShapeDtypeStruct((t, wk), BF16),
                   jax.ShapeDtypeStruct((t, wv), BF16), jax.ShapeDtypeStruct((t, wk), F32)],
        scratch_shapes=[pltpu.VMEM((GLA_HEADS, dv, dk), F32)],
        compiler_params=_params(("arbitrary",)),
    )(proj, proj, proj, glog, s_prev, do_raw)


def _gla_post_fwd(o_raw, proj, gn, t, dv, c_gr):
    w = GLA_HEADS * dv

    def fn(row0, o, gr, g):
        gr = gr.astype(F32)
        outs = []
        for h in range(GLA_HEADS):
            oh = o[:, h * dv:(h + 1) * dv]
            outs.append(oh * lax.rsqrt(jnp.mean(oh * oh, axis=-1, keepdims=True) + EPS))
        on = jnp.concatenate(outs, axis=1) * g
        return (on * (gr * _sigmoid(gr)),)

    return _ew("gla_post_fwd", fn, [("tile", o_raw, w, 0), ("tile", proj, w, c_gr), ("full", gn)],
               [("tile", BF16, w)], t, 264)


def _gla_post_bwd(o_raw, proj, gn, do_gla, t, dv, c_gr):
    w = GLA_HEADS * dv

    def fn(row0, o, gr, g, do):
        gr = gr.astype(F32)
        sg = _sigmoid(gr)
        don = do * (gr * sg)
        ohs, dos = [], []
        for h in range(GLA_HEADS):
            sl = slice(h * dv, (h + 1) * dv)
            oh = o[:, sl]
            r = lax.rsqrt(jnp.mean(oh * oh, axis=-1, keepdims=True) + EPS)
            xh = oh * r
            dxh = don[:, sl] * g[:, sl]
            ohs.append(xh)
            dos.append(r * (dxh - xh * jnp.mean(dxh * xh, axis=-1, keepdims=True)))
        xh = jnp.concatenate(ohs, axis=1)
        dgr = do * (xh * g) * (sg * (1.0 + gr * (1.0 - sg)))
        return jnp.concatenate(dos, axis=1), dgr, _colsum(don * xh)

    return _ew("gla_post_bwd", fn,
               [("tile", o_raw, w, 0), ("tile", proj, w, c_gr), ("full", gn), ("tile", do_gla, w, 0)],
               [("tile", F32, w), ("tile", BF16, w), ("acc", F32, w)], t, 264)


def _adamw(name, w, g, m, v, layer=None, into=None):
    nl, rows, cols = w.shape
    tm = _pick(rows, max(8, (512 * 1024) // max(cols, 1) // 8 * 8), 8)

    def body(w_ref, g_ref, m_ref, v_ref, *rest):
        go_ref, d_ref, nm_ref, nv_ref = rest[-4:]
        gg = g_ref[...]
        nm = ADAM_B1 * m_ref[...] + (1.0 - ADAM_B1) * gg
        nv = ADAM_B2 * v_ref[...] + (1.0 - ADAM_B2) * (gg * gg)
        m_hat = nm / (1.0 - ADAM_B1 ** ADAM_STEP)
        v_hat = nv / (1.0 - ADAM_B2 ** ADAM_STEP)
        go_ref[...] = gg
        d_ref[...] = -ADAM_LR * (m_hat / (jnp.sqrt(v_hat) + ADAM_EPS) + ADAM_WD * w_ref[...])
        nm_ref[...] = nm
        nv_ref[...] = nv

    out_shape = [jax.ShapeDtypeStruct((nl, rows, cols), F32)] * 4
    if layer is None:
        spec = pl.BlockSpec((None, tm, cols), lambda l, i: (l, i, 0))
        return pl.pallas_call(
            body, name=name, grid=(nl, rows // tm), in_specs=[spec] * 4, out_specs=[spec] * 4, out_shape=out_shape,
            compiler_params=_params(("parallel", "parallel")),
        )(w, g, m, v)
    spec = pl.BlockSpec((None, tm, cols), lambda i: (layer, i, 0))
    in_specs = [spec, pl.BlockSpec((tm, cols), lambda i: (i, 0)), spec, spec]
    args, aliases = [w, g, m, v], {}
    if into is not None:
        in_specs += [pl.BlockSpec(memory_space=pl.ANY)] * 4
        args += list(into)
        aliases = {4 + k: k for k in range(4)}
    return pl.pallas_call(
        body, name=name, grid=(rows // tm,), in_specs=in_specs, out_specs=[spec] * 4, out_shape=out_shape,
        input_output_aliases=aliases, compiler_params=_params(("parallel",)),
    )(*args)


def _me():
    return lax.axis_index("x"), lax.axis_index("y"), lax.axis_index("c")


def _hbm_specs(n):
    return [pl.BlockSpec(memory_space=pl.ANY)] * n


_HBM = pl.BlockSpec(memory_space=pltpu.HBM)
_SEM = pl.BlockSpec(memory_space=pltpu.SEMAPHORE)
_EFFECT = pltpu.SideEffectType.DATAFLOW_SIDE_EFFECTING
N_PEERS = N_CHIPS - 1


def _other_chips(x, y):
    return [(1 - x, y), (x, 1 - y), (1 - x, 1 - y)]


def _split_copies(plan, src, land, send_sems, recv_sems):
    me = _me()
    copies = []
    for i in range(len(src)):
        for j, (s, d, peer) in enumerate(plan(src[i], land[i], me)):
            k = plan.copies * i + j
            copies.append(pltpu.make_async_remote_copy(
                src_ref=s, dst_ref=d, send_sem=send_sems.at[k], recv_sem=recv_sems.at[k], device_id=peer,
                device_id_type=MESH))
    return copies


def _split_start(name, srcs, lands, plan, after=None):
    n = len(srcs)
    extra = [] if after is None else [after]

    def body(*refs):
        src, land = refs[:n], refs[n:2 * n]
        send_sems, recv_sems = refs[2 * n + len(extra)], refs[2 * n + len(extra) + 1]
        token = refs[-1]
        for cp in _split_copies(plan, src, land, send_sems, recv_sems):
            cp.start()
        token[...] = jnp.zeros_like(token)

    out_shape = ([pltpu.SemaphoreType.DMA((plan.copies * n,)), pltpu.SemaphoreType.DMA((plan.copies * n,))]
                 + [pltpu.HBM(a.shape, a.dtype) for a in list(srcs) + list(lands)]
                 + [jax.ShapeDtypeStruct((8, LANES), F32)])
    res = pl.pallas_call(
        body, name=name, out_shape=out_shape,
        in_specs=[_HBM] * (2 * n) + [pl.BlockSpec(memory_space=pl.ANY)] * len(extra),
        out_specs=[_SEM, _SEM] + [_HBM] * (2 * n) + [pl.BlockSpec(memory_space=pltpu.VMEM)],
        input_output_aliases={i: 2 + i for i in range(2 * n)},
        compiler_params=pltpu.CompilerParams(has_side_effects=_EFFECT),
    )(*[pltpu.with_memory_space_constraint(a, pltpu.HBM) for a in list(srcs) + list(lands)], *extra)
    return res[0], res[1], res[2:2 + n], res[2 + n:2 + 2 * n], res[-1]


def _split_wait(name, send_sems, recv_sems, srcs, lands, plan, after):
    n = len(srcs)

    def body(*refs):
        src, land = refs[:n], refs[n:2 * n]
        s_sems, r_sems = refs[2 * n], refs[2 * n + 1]
        for cp in _split_copies(plan, src, land, s_sems, r_sems):
            cp.wait_send()
            cp.wait_recv()

    res = pl.pallas_call(
        body, name=name, out_shape=[pltpu.HBM(a.shape, a.dtype) for a in list(srcs) + list(lands)],
        in_specs=[_HBM] * (2 * n) + [_SEM, _SEM, pl.BlockSpec(memory_space=pl.ANY)], out_specs=[_HBM] * (2 * n),
        input_output_aliases={i: i for i in range(2 * n)},
        compiler_params=pltpu.CompilerParams(has_side_effects=_EFFECT),
    )(*srcs, *lands, send_sems, recv_sems, after)
    return res[:n], res[n:]


def _exchange_start(name, bufs, plan, after):
    n = len(bufs)

    def body(*refs):
        buf = refs[:n]
        for cp in _split_copies(plan, buf, buf, refs[n + 1], refs[n + 2]):
            cp.start()
        refs[-1][...] = jnp.zeros_like(refs[-1])

    out_shape = ([pltpu.SemaphoreType.DMA((plan.copies * n,)), pltpu.SemaphoreType.DMA((plan.copies * n,))]
                 + [pltpu.HBM(a.shape, a.dtype) for a in bufs] + [jax.ShapeDtypeStruct((8, LANES), F32)])
    res = pl.pallas_call(
        body, name=name, out_shape=out_shape, in_specs=[_HBM] * n + [pl.BlockSpec(memory_space=pl.ANY)],
        out_specs=[_SEM, _SEM] + [_HBM] * n + [pl.BlockSpec(memory_space=pltpu.VMEM)],
        input_output_aliases={i: 2 + i for i in range(n)},
        compiler_params=pltpu.CompilerParams(has_side_effects=_EFFECT),
    )(*[pltpu.with_memory_space_constraint(a, pltpu.HBM) for a in bufs], after)
    return res[0], res[1], res[2:2 + n], res[-1]


def _exchange_wait(name, send_sems, recv_sems, bufs, plan, after):
    n = len(bufs)

    def body(*refs):
        buf = refs[:n]
        for cp in _split_copies(plan, buf, buf, refs[n], refs[n + 1]):
            cp.wait_send()
            cp.wait_recv()

    return pl.pallas_call(
        body, name=name, out_shape=[pltpu.HBM(a.shape, a.dtype) for a in bufs],
        in_specs=[_HBM] * n + [_SEM, _SEM, pl.BlockSpec(memory_space=pl.ANY)], out_specs=[_HBM] * n,
        input_output_aliases={i: i for i in range(n)},
        compiler_params=pltpu.CompilerParams(has_side_effects=_EFFECT),
    )(*bufs, send_sems, recv_sems, after)


def _forward_plan(src, land, me):
    x, y, c = me
    rows = _half(src.shape[1], c)
    return [(src.at[2 * px + py, rows], land.at[2 * px + py, rows], (x, y, 1 - c)) for px, py in _other_chips(x, y)]


def _share_plan(src, land, me):
    x, y, c = me
    rows = _half(src.shape[0], c)
    return [(src.at[rows], land.at[rows], (x, y, 1 - c))]


_forward_plan.copies = N_CHIPS - 1
_share_plan.copies = 1


def _half(ref_rows, c):
    half = ref_rows // 2
    return pl.ds(c * half, half)


def _gather_plan(src, land, me):
    x, y, c = me
    rows = _half(src.shape[0], c)
    return [(src.at[rows], land.at[2 * x + y, rows], (px, py, c)) for px, py in _other_chips(x, y)]


def _scatter_plan(src, land, me):
    x, y, c = me
    return [(src.at[2 * px + py], land.at[2 * x + y], (px, py, c)) for px, py in _other_chips(x, y)]


def _swap_plan(src, land, me):
    x, y, c = me
    return [(src.at[:, _half(src.shape[1], 1 - c)], land, (x, y, 1 - c))]


_gather_plan.copies = N_PEERS
_scatter_plan.copies = N_PEERS
_swap_plan.copies = 1


def _all_gather_small(v):
    def body(v_ref, out_ref, send_sems, recv_sems, local_sem):
        x, y, c = _me()
        mine = pltpu.make_async_copy(v_ref, out_ref.at[4 * x + 2 * y + c], local_sem)
        mine.start()
        copies = []
        for k in range(1, N_DEV):
            peer = (x ^ ((k >> 2) & 1), y ^ ((k >> 1) & 1), c ^ (k & 1))
            cp = pltpu.make_async_remote_copy(
                src_ref=v_ref, dst_ref=out_ref.at[4 * x + 2 * y + c], send_sem=send_sems.at[k - 1],
                recv_sem=recv_sems.at[k - 1], device_id=peer, device_id_type=MESH)
            cp.start()
            copies.append(cp)
        for cp in copies:
            cp.wait_recv()
        for cp in copies:
            cp.wait_send()
        mine.wait()

    return pl.pallas_call(
        body, name="all_gather_small", in_specs=_hbm_specs(1), out_specs=pl.BlockSpec(memory_space=pl.ANY),
        out_shape=jax.ShapeDtypeStruct((N_DEV,) + v.shape, v.dtype),
        scratch_shapes=[pltpu.SemaphoreType.DMA((N_DEV - 1,)), pltpu.SemaphoreType.DMA((N_DEV - 1,)),
                        pltpu.SemaphoreType.DMA],
    )(v)


def _pair_sum(name, own, recv, chip, core):
    nch, half, cdim = recv.shape
    tm = _pick(half, max(8, (512 * 1024) // cdim // 16 * 16), 16)
    nt = half // tm

    def body(c_ref, k_ref, a_ref, b_ref, s16_ref):
        s16_ref[...] = (a_ref[...] + b_ref[...]).astype(BF16)

    other = lambda j, c: (c[0] + 1 + j) % nch
    spec = pl.BlockSpec((None, tm, cdim), lambda j, i, c, k: (other(j, c), i, 0))
    return pl.pallas_call(
        body, name=name,
        grid_spec=pltpu.PrefetchScalarGridSpec(
            num_scalar_prefetch=2, grid=(nch - 1, nt),
            in_specs=[pl.BlockSpec((None, tm, cdim), lambda j, i, c, k: (other(j, c), k[0] * nt + i, 0)), spec],
            out_specs=spec),
        out_shape=jax.ShapeDtypeStruct((nch, half, cdim), BF16),
        compiler_params=_params(("parallel", "parallel")),
    )(chip, core, own, recv)


def _chip_sum(name, own, recv, landed, chip, core):
    nch, half, cdim = recv.shape
    tm = _pick(half, max(8, (512 * 1024) // cdim // 16 * 16), 16)
    nt = half // tm

    def body(c_ref, k_ref, own_ref, recv_ref, *rest):
        landed_refs, out_ref = rest[:nch], rest[-1]
        me = c_ref[0]
        mine = own_ref[...] + recv_ref[...]
        acc = None
        for j in range(nch):
            term = jnp.where(me == j, mine, landed_refs[j][...].astype(F32))
            acc = term if acc is None else acc + term
        out_ref[...] = acc

    landed_specs = [pl.BlockSpec((None, tm, cdim), functools.partial(lambda i, c, k, j: (j, i, 0), j=j))
                    for j in range(nch)]
    return pl.pallas_call(
        body, name=name,
        grid_spec=pltpu.PrefetchScalarGridSpec(
            num_scalar_prefetch=2, grid=(nt,),
            in_specs=[pl.BlockSpec((None, tm, cdim), lambda i, c, k: (c[0], k[0] * nt + i, 0)),
                      pl.BlockSpec((None, tm, cdim), lambda i, c, k: (c[0], i, 0))] + landed_specs,
            out_specs=pl.BlockSpec((tm, cdim), lambda i, c, k: (k[0] * nt + i, 0))),
        out_shape=jax.ShapeDtypeStruct((2 * half, cdim), F32),
        compiler_params=_params(("parallel",)),
    )(chip, core, own, recv, *([landed] * nch))


def _sum_devices(gathered):
    _, r, cdim = gathered.shape

    def body(g_ref, o_ref):
        acc = g_ref[0]
        for k in range(1, N_DEV):
            acc = acc + g_ref[k]
        o_ref[...] = acc

    return pl.pallas_call(
        body, name="sum_devices", out_shape=jax.ShapeDtypeStruct((r, cdim), F32),
        compiler_params=_params(),
    )(gathered)


class _Layout:
    def __init__(self, d):
        self.d = d
        self.fw = d // 2
        self.fd = self.fw // FOX_HEADS
        self.gk = d // 2
        self.gv = d
        self.dk = self.gk // GLA_HEADS
        self.dv = self.gv // GLA_HEADS
        self.c_fq = 0
        self.c_gq = self.fw
        self.c_gv = self.c_gq + self.gk
        self.c_gr = self.c_gv + self.gv
        self.c_fkv = self.c_gr + self.gv
        self.c_gates = self.c_fkv + 2 * self.fw
        self.c_gk = self.c_gates + 2 * d
        self.c_small = self.c_gk + self.gk
        self.n_main = self.c_small
        self.n_p = self.c_small + LANES
        self.o_fk = self.fw
        self.o_fv = 2 * self.fw
        self.o_ff = 3 * self.fw
        self.o_gq = self.o_ff + FOX_HEADS
        self.o_gk = self.o_gq + self.gk
        self.o_gv = self.o_gk + self.gk
        self.o_gr = self.o_gv + self.gv
        self.o_ga = self.o_gr + self.gv
        self.o_gf = self.o_ga + GLA_RANK
        self.o_gg = self.o_gf + d
        self.n_orig = self.o_gg + d

    def to_p(self, shards):
        per = self.n_orig // N_CHIPS
        ranges = [(0, self.fw), (self.o_gq, self.gk), (self.o_gv, self.gv), (self.o_gr, self.gv)]
        for h in range(FOX_HEADS):
            ranges += [(self.o_fk + h * self.fd, self.fd), (self.o_fv + h * self.fd, self.fd)]
        ranges += [(self.o_gf, 2 * self.d), (self.o_gk, self.gk), (self.o_ff, FOX_HEADS), (self.o_ga, GLA_RANK)]
        pieces = []
        for a, width in ranges:
            for j in range(a // per, (a + width - 1) // per + 1):
                lo, hi = max(a, j * per), min(a + width, (j + 1) * per)
                pieces.append(shards[j][:, lo - j * per:hi - j * per])
        pieces.append(jnp.zeros((shards.shape[1], LANES - FOX_HEADS - GLA_RANK), shards.dtype))
        return jnp.concatenate(pieces, axis=1)

    def from_segments(self, seg):
        per = self.n_orig // N_CHIPS
        fd = self.fd
        atoms = [("fq", 0, self.fw)]
        atoms += [("fkv", 2 * h * fd, fd) for h in range(FOX_HEADS)]
        atoms += [("fkv", (2 * h + 1) * fd, fd) for h in range(FOX_HEADS)]
        atoms += [("small", 0, FOX_HEADS), ("gq", 0, self.gk), ("gk", 0, self.gk), ("gv", 0, self.gv),
                  ("gr", 0, self.gv), ("small", FOX_HEADS, GLA_RANK), ("gates", 0, 2 * self.d)]
        shards = [[] for _ in range(N_CHIPS)]
        pos = 0
        for name, c0, width in atoms:
            for j in range(pos // per, (pos + width - 1) // per + 1):
                lo, hi = max(pos, j * per), min(pos + width, (j + 1) * per)
                shards[j].append(seg[name][:, c0 + lo - pos:c0 + hi - pos])
            pos += width
        assert pos == self.n_orig
        return jnp.stack([jnp.concatenate(s, axis=1) for s in shards])


def _layer_fwd(lay, h, p, t, hooks=None):
    hooks = hooks or {}
    d = lay.d
    xn, xn_t = _rms_fwd_t("rms_mix_fwd", h, p["norm_mix_g"], t, d)
    proj = _mm("mm_proj", xn, p["w_in"], mode="nn", m=t, n=lay.n_main, k=d, out_dtypes=(BF16,))
    small = _mm("mm_small", xn, p["w_in"], mode="nn", m=t, n=LANES, k=d, b_c0=lay.c_small)
    cs = _fox_gate_fwd(small, p["b_forget_p"], t)
    ct = cs[:, :FOX_HEADS].T
    c_col, c_row = ct[:, :, None], ct[:, None, :]
    o_fox, lse = _fox_fwd(proj, c_col, c_row, t, lay.fd, lay.c_fq, lay.c_fkv)
    b_alpha = p["b_alpha"] + hooks["mixers"](o_fox) if "mixers" in hooks else p["b_alpha"]
    glog = _gla_gate_fwd(small, p["w_alpha_p"], b_alpha, t, lay.gk)
    o_raw, s_prev = _gla_fwd(proj, glog, t, lay.dk, lay.dv, lay.c_gq, lay.c_gk, lay.c_gv)
    o_gla = _gla_post_fwd(o_raw, proj, p["gla_norm_g"], t, lay.dv, lay.c_gr)
    if "late" in hooks:
        p.update(hooks["late"](o_gla))
    a_fox = _mm("mm_o_fox", o_fox, p["w_o_fox"], mode="nn", m=t, n=d, k=lay.fw, b_shards=N_CHIPS)
    a_gla = _mm("mm_o_gla", o_gla, p["w_o_gla"], mode="nn", m=t, n=d, k=lay.gv)
    y = _merge_fwd(a_fox, a_gla, proj, lay.c_gates, t, d)
    h1 = _mm("mm_out", y, p["w_out"], mode="nn", m=t, n=d, k=d, extras=[h], epilogue=lambda acc, res: (res + acc,))
    xn2, xn2_t = _rms_fwd_t("rms_mlp_fwd", h1, p["norm_mlp_g"], t, d)
    u, act = _mm("mm_ff1", xn2, p["w_ff1"], mode="nn", m=t, n=4 * d, k=d, out_dtypes=(BF16, BF16), b_shards=N_CHIPS,
                 epilogue=lambda acc: (acc, jnp.square(jnp.maximum(acc, 0.0))))
    if "mlp" in hooks:
        hooks["mlp"](act)
    h2 = _mm("mm_ff2", act, p["w_ff2"], mode="nn", m=t, n=d, k=4 * d, extras=[h1],
             epilogue=lambda acc, res: (res + acc,))
    saved = dict(h=h, xn_t=xn_t, proj=proj, small=small, c_col=c_col, c_row=c_row, o_fox=o_fox, lse=lse, glog=glog,
                 o_raw=o_raw, s_prev=s_prev, o_gla=o_gla, a_fox=a_fox, a_gla=a_gla, y=y, h1=h1, xn2_t=xn2_t, u=u, act=act)
    return h2, saved


def _layer_bwd(lay, dh2, p, s, t, gates=None, first=False):
    d = lay.d
    g = {}

    def gated(gain, point):
        return gain + gates[point](g) if gates and point in gates else gain
    du = _mm("mm_dact", dh2, p["w_ff2"], mode="nt", m=t, n=4 * d, k=d, extras=[s["u"]], out_dtypes=(BF16,),
             epilogue=lambda acc, u: (acc * (2.0 * jnp.maximum(u.astype(F32), 0.0)),))
    g["w_ff2"] = _mm("mm_dw_ff2", s["act"], dh2, mode="tn", m=4 * d, n=d, k=t)
    g["w_ff1"] = _mm("mm_dw_ff1", s["xn2_t"], du, mode="nn", m=d, n=4 * d, k=t, tk=t, out_shards=N_CHIPS)
    dxn2 = _mm("mm_dxn2", du, p["w_ff1"], mode="nt", m=t, n=d, k=4 * d, b_shards=N_CHIPS)
    dh1, g["norm_mlp_g"] = _rms_bwd("rms_mlp_bwd", s["h1"], gated(p["norm_mlp_g"], "mlp"), dxn2, dh2, t, d)
    dy = _mm("mm_dy", dh1, p["w_out"], mode="nt", m=t, n=d, k=d)
    g["w_out"] = _mm("mm_dw_out", s["y"], dh1, mode="tn", m=d, n=d, k=t)
    da_fox, da_gla, dgates = _merge_bwd(dy, s["a_fox"], s["a_gla"], s["proj"], lay.c_gates, t, d)
    g["w_o_fox"] = _mm("mm_dw_o_fox", s["o_fox"], da_fox, mode="tn", m=lay.fw, n=d, k=t, out_shards=N_CHIPS)
    do_fox = _mm("mm_do_fox", da_fox, p["w_o_fox"], mode="nt", m=t, n=lay.fw, k=d, b_shards=N_CHIPS)
    g["w_o_gla"] = _mm("mm_dw_o_gla", s["o_gla"], da_gla, mode="tn", m=lay.gv, n=d, k=t)
    do_gla = _mm("mm_do_gla", da_gla, p["w_o_gla"], mode="nt", m=t, n=lay.gv, k=d)
    do_raw, dgr, g["gla_norm_g"] = _gla_post_bwd(s["o_raw"], s["proj"], gated(p["gla_norm_g"], "out"), do_gla, t,
                                                 lay.dv, lay.c_gr)
    dgq, dgk, dgv, dglog = _gla_bwd(s["proj"], s["glog"], s["s_prev"], do_raw, t, lay.dk, lay.dv,
                                    lay.c_gq, lay.c_gk, lay.c_gv)
    dz, g["b_alpha"] = _gla_gate_bwd(dglog, s["small"], p["w_alpha_p"], p["b_alpha"], t, lay.gk)
    g["w_alpha_p"] = _mm("mm_dw_alpha", s["small"], dz, mode="tn", m=LANES, n=lay.gk, k=t)
    dga = _mm("mm_dga", dz, p["w_alpha_p"], mode="nt", m=t, n=LANES, k=lay.gk)
    delta = _fox_delta(s["o_fox"], do_fox, t, lay.fd)
    dfq, dfkv, dc, dr = _fox_bwd(s["proj"], s["c_col"], s["c_row"], s["lse"], delta, do_fox, t, lay.fd,
                                 lay.c_fq, lay.c_fkv)
    dc_p = jnp.pad((dc[:, 0, :] + dr[:, :, 0]).T, ((0, 0), (0, LANES - FOX_HEADS)))
    dsmall, g["b_forget_p"] = _fox_gate_bwd(dc_p, s["small"], p["b_forget_p"], dga, t)
    segs = [("fq", dfq, lay.c_fq), ("gq", dgq, lay.c_gq), ("gv", dgv, lay.c_gv), ("gr", dgr, lay.c_gr),
            ("fkv", dfkv, lay.c_fkv), ("gates", dgates, lay.c_gates), ("gk", dgk, lay.c_gk),
            ("small", dsmall, lay.c_small)]
    dw_in = {nm: _mm("mm_dw_in_" + nm, s["xn_t"], dseg, mode="nn", m=d, n=dseg.shape[1], k=t, tk=t)
             for nm, dseg, _ in segs}
    g["w_in"] = lay.from_segments(dw_in)
    dxn = _mm("mm_dxn_gates", dgates, p["w_in"], mode="nt", m=t, n=d, k=2 * d, b_c0=lay.c_gates)
    dxn = _mm("mm_dxn_small", dsmall, p["w_in"], mode="nt", m=t, n=d, k=LANES, b_c0=lay.c_small, extras=[dxn],
              epilogue=lambda acc, prev: (prev + acc,))
    for nm, (a1, c1), (a2, c2) in (("q", (dfq, lay.c_fq), (dgq, lay.c_gq)), ("v", (dgv, lay.c_gv), (dgr, lay.c_gr)),
                                   ("k", (dfkv, lay.c_fkv), (dgk, lay.c_gk))):
        dxn = _mm_nt2("mm_dxn_" + nm, a1, c1, a2, c2, p["w_in"], dxn, m=t, n=d)
    if first:
        dx, head, g["norm_mix_g"] = _rms_bwd_x("rms_mix_bwd_x", s["h"], gated(p["norm_mix_g"], "in"), dxn, dh1, t, d)
        return (dx, head), g
    dh, g["norm_mix_g"] = _rms_bwd("rms_mix_bwd", s["h"], gated(p["norm_mix_g"], "in"), dxn, dh1, t, d)
    return dh, g


def _sequence_step(x, target, meta, layers, final_g):
    seq, d = x.shape
    t = seq + ROW0
    lay = _Layout(d)
    h = jnp.pad(x, ((ROW0, 0), (0, 0))).at[PAD:ROW0].set(meta)
    target_p = jnp.pad(target, ((ROW0, 0), (0, 0)))
    saved = []
    for p in layers:
        h, s = _layer_fwd(lay, h, p, t)
        saved.append(s)
    dh, dg_final, loss_part = _loss_head(h, final_g, target_p, t, d)
    grads = [None] * len(layers)
    for l in reversed(range(len(layers))):
        dh, grads[l] = _layer_bwd(lay, dh, layers[l], saved[l], t)
    return loss_part, dh[ROW0:], dh[PAD:ROW0], grads, dg_final


_SMALL_ROWS = 48


def _pack_small(d, meta, mix, gla, mlp, final, b_alpha, b_forget, w_alpha2):
    rows = [meta.reshape(N_META, d), mix.reshape(DEPTH, d), gla.reshape(DEPTH, d), mlp.reshape(DEPTH, d),
            final.reshape(1, d), b_alpha.reshape(1, d),
            jnp.pad(b_forget.reshape(1, DEPTH * FOX_HEADS), ((0, 0), (0, d - DEPTH * FOX_HEADS))),
            jnp.zeros((7, d), F32), w_alpha2.reshape(GLA_RANK, d)]
    return jnp.concatenate(rows, axis=0)


def _unpack_small(d, packed):
    return dict(meta=packed[:N_META], norm_mix_g=packed[16:18], gla_norm_g=packed[18:20], norm_mlp_g=packed[20:22],
                final_norm_g=packed[22], b_alpha=packed[23].reshape(DEPTH, d // 2),
                b_forget=packed[24, :DEPTH * FOX_HEADS].reshape(DEPTH, FOX_HEADS),
                w_alpha2=packed[32:48].reshape(DEPTH, GLA_RANK, d // 2))


_BIG = ("w_in", "w_o_fox", "w_o_gla", "w_out", "w_ff1", "w_ff2")
_COL_SHARDED = ("w_in", "w_o_fox", "w_ff1")


def _full_matrix(name, gathered_l):
    nch, r, c = gathered_l.shape
    if name in _COL_SHARDED:
        return gathered_l.transpose(1, 0, 2).reshape(r, nch * c)
    return gathered_l.reshape(nch * r, c)


def _shard_major(name, full):
    r, c = full.shape
    if name in _COL_SHARDED:
        return full.reshape(r, N_CHIPS, c // N_CHIPS).transpose(1, 0, 2)
    return full.reshape(N_CHIPS, r // N_CHIPS, c)


def kernel(x, meta_tokens, norm_mix_g, w_in, b_forget, w_alpha2, b_alpha, gla_norm_g, w_o_fox, w_o_gla, w_out, norm_mlp_g, w_ff1, w_ff2, final_norm_g, loss_target, m_meta_tokens, m_norm_mix_g, m_w_in, m_b_forget, m_w_alpha2, m_b_alpha, m_gla_norm_g, m_w_o_fox, m_w_o_gla, m_w_out, m_norm_mlp_g, m_w_ff1, m_w_ff2, m_final_norm_g, v_meta_tokens, v_norm_mix_g, v_w_in, v_b_forget, v_w_alpha2, v_b_alpha, v_gla_norm_g, v_w_o_fox, v_w_o_gla, v_w_out, v_norm_mlp_g, v_w_ff1, v_w_ff2, v_final_norm_g):
    d = x.shape[2]
    lay = _Layout(d)
    xi, yi, ci = lax.axis_index("x"), lax.axis_index("y"), lax.axis_index("c")
    chip = (2 * xi + yi).astype(jnp.int32)
    w = dict(w_in=w_in, w_alpha2=w_alpha2, w_o_fox=w_o_fox, w_o_gla=w_o_gla, w_out=w_out, w_ff1=w_ff1, w_ff2=w_ff2)
    m = dict(w_in=m_w_in, w_alpha2=m_w_alpha2, w_o_fox=m_w_o_fox, w_o_gla=m_w_o_gla, w_out=m_w_out, w_ff1=m_w_ff1,
             w_ff2=m_w_ff2)
    v = dict(w_in=v_w_in, w_alpha2=v_w_alpha2, w_o_fox=v_w_o_fox, w_o_gla=v_w_o_gla, w_out=v_w_out, w_ff1=v_w_ff1,
             w_ff2=v_w_ff2)

    seq = x.shape[1]
    t = seq + ROW0
    core_idx = ci.astype(jnp.int32)[None]
    chip_idx = chip[None]

    cols = d // N_CHIPS
    small_w = jnp.concatenate([meta_tokens, w_alpha2.reshape(-1, cols)], axis=0)
    small_raw = _all_gather_small(small_w)
    small_all = small_raw[0::2]
    alpha_full = small_all[:, N_META:].reshape(N_CHIPS, DEPTH, GLA_RANK, lay.gk // N_CHIPS)
    alpha_full = alpha_full.transpose(1, 2, 0, 3).reshape(DEPTH, GLA_RANK, lay.gk)
    groups = [(0, ("w_in",)), (0, _BIG[1:]), (1, _BIG)]
    started, after = [], small_raw
    for gi, (l, names) in enumerate(groups):
        zero = 0.0 if gi == 0 else after[0, 0]
        own16 = [(w[n][l] + zero).astype(BF16) for n in names]
        lands = [lax.empty((N_CHIPS,) + o.shape, BF16) for o in own16]
        started.append(_split_start("gather_start_%d" % gi, own16, lands, _gather_plan, after=after))
        after = started[gi][4]
    meta_full = small_all[:, :N_META].transpose(1, 0, 2).reshape(N_META, d) + after[0, 0]

    passing = {}

    def arrive(gi, after):
        send_sems, recv_sems, srcs, lands, _ = started[gi]
        srcs, lands = _split_wait("gather_wait_%d" % gi, send_sems, recv_sems, srcs, lands, _gather_plan, after)
        passing[gi] = (srcs, _exchange_start("gather_pass_%d" % gi, lands, _forward_plan, after=srcs[0]))
        return passing[gi][1][3][0, 0]

    def gathered(gi, after):
        if gi not in passing:
            arrive(gi, after)
        srcs, (send_sems, recv_sems, lands, _) = passing[gi]
        lands = _exchange_wait("gather_pass_wait_%d" % gi, send_sems, recv_sems, lands, _forward_plan, after)
        return {n: lax.dynamic_update_slice(g, o[None], (chip, 0, 0)) for n, g, o in zip(groups[gi][1], lands, srcs)}

    def early_weights(l, gl):
        w_alpha_p = jnp.zeros((LANES, lay.gk), BF16).at[FOX_HEADS:FOX_HEADS + GLA_RANK].set(
            alpha_full[l].astype(BF16))
        return dict(
            w_in=lay.to_p(gl["w_in"]), w_alpha_p=w_alpha_p,
            norm_mix_g=norm_mix_g[l][None], norm_mlp_g=norm_mlp_g[l][None], gla_norm_g=gla_norm_g[l][None],
            b_alpha=b_alpha[l][None],
            b_forget_p=jnp.pad(b_forget[l][None], ((0, 0), (0, LANES - FOX_HEADS))))

    def late_weights(gl):
        return dict(w_o_fox=gl["w_o_fox"], w_o_gla=_full_matrix("w_o_gla", gl["w_o_gla"]),
                    w_out=_full_matrix("w_out", gl["w_out"]), w_ff1=gl["w_ff1"],
                    w_ff2=_full_matrix("w_ff2", gl["w_ff2"]))

    h = jnp.pad(x[0], ((ROW0, 0), (0, 0))).at[PAD:ROW0].set(meta_full)
    layers, saved = [], []
    layers.append(early_weights(0, gathered(0, after=h)))
    h, s = _layer_fwd(lay, h, layers[0], t, hooks=dict(
        mixers=lambda after: arrive(1, after), late=lambda after: late_weights(gathered(1, after)),
        mlp=lambda after: arrive(2, after)))
    saved.append(s)
    gl = gathered(2, after=h)
    layers.append({**early_weights(1, gl), **late_weights(gl)})
    h, s = _layer_fwd(lay, h, layers[1], t)
    saved.append(s)
    dh, dg_final, loss_part = _loss_head(h, final_norm_g[None], jnp.pad(loss_target[0], ((ROW0, 0), (0, 0))), t, d)
    loss = lax.psum(loss_part[0, 0], ("x", "y", "c"))

    def partial_of(g, n):
        return g[n] if n in ("w_ff1", "w_o_fox", "w_in") else _shard_major(n, g[n])

    scatter_groups = dict(mlp=("w_ff1", "w_ff2"), out=("w_o_fox", "w_o_gla", "w_out"))
    scatter_groups["in"] = ("w_in",)
    swapping, scattered = [], {}

    def start_swap(l, grp, g, after=None):
        parts = [partial_of(g, n) for n in scatter_groups[grp]]
        lands = [lax.empty((p_.shape[0], p_.shape[1] // 2, p_.shape[2]), F32) for p_ in parts]
        started_swap = _split_start("swap_start_%d_%s" % (l, grp), parts, lands, _swap_plan, after=after)
        swapping.append((l, grp, started_swap))
        return started_swap[4]

    def start_scatter(after):
        l, grp, (send_sems, recv_sems, srcs, lands, _) = swapping.pop(0)
        names = scatter_groups[grp]
        tag = "%d_%s" % (l, grp)
        parts, from_sibling = _split_wait("swap_wait_" + tag, send_sems, recv_sems, srcs, lands, _swap_plan, after)
        sums = [_pair_sum("pair_sum_%d_%s" % (l, n), p_, r_, chip_idx, core_idx)
                for n, p_, r_ in zip(names, parts, from_sibling)]
        lands = [lax.empty(s16.shape, BF16) for s16 in sums]
        send_sems, recv_sems, srcs, lands, token = _split_start("scatter_start_" + tag, sums, lands, _scatter_plan)
        scattered[l, grp] = (send_sems, recv_sems, srcs, lands, parts, from_sibling)
        return token

    def gate(l, grp, g):
        token = start_swap(l, grp, g)
        if len(swapping) > 1:
            token = start_scatter(after=token)
        return token[0, 0]

    def gates_for(l, points):
        return {grp: functools.partial(gate, l, grp) for grp in points}

    grads = [None] * DEPTH
    dh, grads[1] = _layer_bwd(lay, dh, layers[1], saved[1], t, gates=gates_for(1, ("mlp", "out", "in")))
    (grad_x, head), grads[0] = _layer_bwd(lay, dh, layers[0], saved[0], t, gates=gates_for(0, ("mlp", "out")),
                                          first=True)
    d_meta = head[PAD:ROW0]

    stack = lambda key: jnp.concatenate([grads[l][key] for l in range(DEPTH)], axis=0)
    b_forget_g = jnp.concatenate([grads[l]["b_forget_p"][:, :FOX_HEADS] for l in range(DEPTH)], axis=0)
    alpha_g = jnp.stack([grads[l]["w_alpha_p"][FOX_HEADS:FOX_HEADS + GLA_RANK] for l in range(DEPTH)])
    packed = _pack_small(d, d_meta, stack("norm_mix_g"), stack("gla_norm_g"), stack("norm_mlp_g"), dg_final,
                         stack("b_alpha"), b_forget_g, alpha_g)
    small_g = _unpack_small(d, _sum_devices(_all_gather_small(packed)))
    small_g["meta"] = lax.dynamic_slice_in_dim(small_g["meta"], chip * (d // N_CHIPS), d // N_CHIPS, axis=1)
    alpha_shard = lax.dynamic_slice_in_dim(small_g["w_alpha2"], chip * (lay.gk // N_CHIPS), lay.gk // N_CHIPS, axis=2)

    after = start_swap(0, "in", grads[0], after=small_g["final_norm_g"])
    while swapping:
        after = start_scatter(after)
    outs = {n: None for n in _BIG}

    def reduce_and_share(l, grps, after):
        tag = "%d_%s" % (l, grps[0])
        names, sums = [], []
        for grp in grps:
            send_sems, recv_sems, srcs, lands, parts, from_sibling = scattered[l, grp]
            _, lands = _split_wait("scatter_wait_%d_%s" % (l, grp), send_sems, recv_sems, srcs, lands, _scatter_plan,
                                 after)
            names += scatter_groups[grp]
            sums += [_chip_sum("chip_sum_%d_%s" % (l, n), p_, r_, landed, chip_idx, core_idx)
                     for n, p_, r_, landed in zip(scatter_groups[grp], parts, from_sibling, lands)]
        return l, tag, names, _exchange_start("share_start_" + tag, sums, _share_plan, after=sums[-1])

    def update(shared, after):
        l, tag, names, (send_sems, recv_sems, sums, _) = shared
        sums = _exchange_wait("share_wait_" + tag, send_sems, recv_sems, sums, _share_plan, after)
        for n, g in zip(names, sums):
            outs[n] = _adamw("adamw_%d_%s" % (l, n), w[n], g, m[n], v[n], layer=l, into=outs[n])
        return outs[names[-1]][0]

    shared_1 = reduce_and_share(1, ("mlp", "out", "in"), after)
    shared_0 = reduce_and_share(0, ("mlp", "out"), shared_1[3][3])
    after = update(shared_1, shared_0[3][3])
    shared_in = reduce_and_share(0, ("in",), after)
    after = update(shared_0, shared_in[3][3])
    update(shared_in, after)
    out_g, out_d, out_m, out_v = {}, {}, {}, {}
    for n in _BIG:
        out_g[n], out_d[n], out_m[n], out_v[n] = outs[n]
    out_g["w_alpha2"], out_d["w_alpha2"], out_m["w_alpha2"], out_v["w_alpha2"] = _adamw(
        "adamw_w_alpha2", w["w_alpha2"], alpha_shard, m["w_alpha2"], v["w_alpha2"])
    sm_w = dict(meta_tokens=meta_tokens, norm_mix_g=norm_mix_g, b_forget=b_forget, b_alpha=b_alpha,
                gla_norm_g=gla_norm_g, norm_mlp_g=norm_mlp_g, final_norm_g=final_norm_g)
    sm_m = dict(meta_tokens=m_meta_tokens, norm_mix_g=m_norm_mix_g, b_forget=m_b_forget, b_alpha=m_b_alpha,
                gla_norm_g=m_gla_norm_g, norm_mlp_g=m_norm_mlp_g, final_norm_g=m_final_norm_g)
    sm_v = dict(meta_tokens=v_meta_tokens, norm_mix_g=v_norm_mix_g, b_forget=v_b_forget, b_alpha=v_b_alpha,
                gla_norm_g=v_gla_norm_g, norm_mlp_g=v_norm_mlp_g, final_norm_g=v_final_norm_g)
    sm_g = dict(meta_tokens=small_g["meta"], norm_mix_g=small_g["norm_mix_g"], b_forget=small_g["b_forget"],
                b_alpha=small_g["b_alpha"], gla_norm_g=small_g["gla_norm_g"], norm_mlp_g=small_g["norm_mlp_g"],
                final_norm_g=small_g["final_norm_g"])
    names_small = list(sm_w)
    sizes = [sm_w[n].size for n in names_small]
    width = 512
    total = -(-sum(sizes) // (8 * width)) * (8 * width)

    def pack_flat(dct, fill):
        flat = jnp.concatenate([dct[n].reshape(-1) for n in names_small])
        return jnp.pad(flat, (0, total - flat.shape[0]), constant_values=fill).reshape(1, -1, width)

    res = _adamw("adamw_small", pack_flat(sm_w, 0.0), pack_flat(sm_g, 0.0), pack_flat(sm_m, 0.0), pack_flat(sm_v, 1.0))
    offs = [0]
    for sz in sizes:
        offs.append(offs[-1] + sz)
    for i, n in enumerate(names_small):
        out_g[n] = sm_g[n].reshape(sm_w[n].shape)
        out_d[n], out_m[n], out_v[n] = [r.reshape(-1)[offs[i]:offs[i + 1]].reshape(sm_w[n].shape) for r in res[1:]]

    order = ["meta_tokens", "norm_mix_g", "w_in", "b_forget", "w_alpha2", "b_alpha", "gla_norm_g", "w_o_fox",
             "w_o_gla", "w_out", "norm_mlp_g", "w_ff1", "w_ff2", "final_norm_g"]
    return (loss, grad_x[None], *[out_g[n] for n in order], *[out_d[n] for n in order],
            *[out_m[n] for n in order], *[out_v[n] for n in order])
```

```python
import functools

import numpy as np

import jax
import jax.numpy as jnp
from jax import lax
from jax.experimental import pallas as pl
from jax.experimental.pallas import tpu as pltpu

F32 = jnp.float32
BF16 = jnp.bfloat16

N_META = 16
PAD = 112
ROW0 = PAD + N_META
EPS = 1e-6
MASK_VALUE = -1e30
FOX_HEADS = 8
FOX_GROUP = 2
GLA_HEADS = 4
GLA_RANK = 16
GLA_TAU = 16.0
GLA_CHUNK = 64
DEPTH = 2
N_CHIPS = 4
N_DEV = 8

ADAM_LR = 0.001
ADAM_B1 = 0.9
ADAM_B2 = 0.999
ADAM_EPS = 1e-08
ADAM_WD = 0.01
ADAM_STEP = 10

LANES = 128
VMEM_LIMIT = 56 * 1024 * 1024
MESH = pl.DeviceIdType.MESH


def _pick(n, target, mult):
    best = None
    for d in range(mult, min(n, target) + 1, mult):
        if n % d == 0:
            best = d
    return n if best is None else best


def _params(sem=None):
    return pltpu.CompilerParams(dimension_semantics=sem, vmem_limit_bytes=VMEM_LIMIT)


def _bf(v):
    return v if v.dtype == BF16 else v.astype(BF16)


def _sigmoid(z):
    return 1.0 / (1.0 + jnp.exp(-z))


def _log_sigmoid(z):
    return jnp.minimum(z, 0.0) - jnp.log(1.0 + jnp.exp(-jnp.abs(z)))


def _split3(v):
    a = v.astype(BF16)
    r = v - a.astype(F32)
    b = r.astype(BF16)
    c = (r - b.astype(F32)).astype(BF16)
    return a, b, c


def _dot(a, b, dims):
    return lax.dot_general(a, b, (dims, ((), ())), preferred_element_type=F32)


NN = ((1,), (0,))
NT = ((1,), (1,))
TN = ((0,), (0,))


def _tri_dot(tri, v, dims=NN):
    a, b, c = _split3(v)
    return _dot(tri, a, dims) + _dot(tri, b, dims) + _dot(tri, c, dims)


def _mm(name, a, b, *, mode, m, n, k, b_c0=0, extras=(), epilogue=None, out_dtypes=(F32,),
        b_shards=1, out_shards=1, tm=1056, tn=1024, tk=2048):
    tm = _pick(m, tm, LANES if mode == "tn" else 16)
    tn = _pick(n // max(b_shards if mode == "nn" else 1, out_shards), tn, LANES)
    if mode == "tn":
        tk = _pick(k, 2112, 16)
    else:
        tk = _pick(k // (b_shards if mode == "nt" else 1), tk, LANES)
    assert b_c0 % (tk if mode == "nt" else tn) == 0 and (b_shards == 1 or b_c0 == 0)
    nk = k // tk
    if mode == "tn":
        a_spec = pl.BlockSpec((tk, tm), lambda i, j, kk: (kk, i))
    else:
        a_spec = pl.BlockSpec((tm, tk), lambda i, j, kk: (i, kk))
    if mode == "nt":
        dims = NT
        if b_shards > 1:
            per = (k // b_shards) // tk
            b_spec = pl.BlockSpec((None, tn, tk), lambda i, j, kk: (kk // per, j, kk % per))
        else:
            b_spec = pl.BlockSpec((tn, tk), lambda i, j, kk: (j, kk + b_c0 // tk))
    else:
        dims = NN if mode == "nn" else TN
        if b_shards > 1:
            per = (n // b_shards) // tn
            b_spec = pl.BlockSpec((None, tk, tn), lambda i, j, kk: (j // per, kk, j % per))
        else:
            b_spec = pl.BlockSpec((tk, tn), lambda i, j, kk: (kk, j + b_c0 // tn))
    ex_specs = [pl.BlockSpec((tm, tn), lambda i, j, kk: (i, j)) for _ in extras]
    if out_shards > 1:
        oper = (n // out_shards) // tn
        out_specs = [pl.BlockSpec((None, tm, tn), lambda i, j, kk: (j // oper, i, j % oper)) for _ in out_dtypes]
        out_shape = [jax.ShapeDtypeStruct((out_shards, m, n // out_shards), dt) for dt in out_dtypes]
    else:
        out_specs = [pl.BlockSpec((tm, tn), lambda i, j, kk: (i, j)) for _ in out_dtypes]
        out_shape = [jax.ShapeDtypeStruct((m, n), dt) for dt in out_dtypes]
    n_ex = len(extras)
    n_out = len(out_dtypes)

    def finish(acc, ex_refs, out_refs):
        vals = (acc,) if epilogue is None else epilogue(acc, *[r[...] for r in ex_refs])
        for r, v in zip(out_refs, vals):
            r[...] = v.astype(r.dtype)

    def body(a_ref, b_ref, *rest):
        ex_refs = rest[:n_ex]
        out_refs = rest[n_ex:n_ex + n_out]
        prod = _dot(_bf(a_ref[...]), _bf(b_ref[...]), dims)
        if nk == 1:
            finish(prod, ex_refs, out_refs)
            return
        acc_ref = rest[n_ex + n_out]
        kk = pl.program_id(2)

        @pl.when(kk == 0)
        def _():
            acc_ref[...] = prod

        @pl.when((kk > 0) & (kk < nk - 1))
        def _():
            acc_ref[...] += prod

        @pl.when(kk == nk - 1)
        def _():
            finish(acc_ref[...] + prod, ex_refs, out_refs)

    outs = pl.pallas_call(
        body,
        name=name,
        grid=(m // tm, n // tn, nk),
        in_specs=[a_spec, b_spec] + ex_specs,
        out_specs=out_specs,
        out_shape=out_shape,
        scratch_shapes=[pltpu.VMEM((tm, tn), F32)] if nk > 1 else [],
        compiler_params=_params(("parallel", "parallel", "arbitrary")),
    )(a, b, *extras)
    return outs[0] if n_out == 1 else outs


def _mm_nt2(name, a1, c1, a2, c2, b, prev, *, m, n, tm=528, tn=1024):
    k1, k2 = a1.shape[1], a2.shape[1]
    assert c1 % k1 == 0 and c2 % k2 == 0
    tm = _pick(m, tm, 16)
    tn = _pick(n, tn, LANES)

    def body(a1_ref, a2_ref, b1_ref, b2_ref, *rest):
        acc = _dot(a1_ref[...], b1_ref[...], NT) + _dot(a2_ref[...], b2_ref[...], NT)
        if prev is not None:
            acc = rest[0][...] + acc
        rest[-1][...] = acc

    tile = pl.BlockSpec((tm, tn), lambda i, j: (i, j))
    in_specs = [pl.BlockSpec((tm, k1), lambda i, j: (i, 0)), pl.BlockSpec((tm, k2), lambda i, j: (i, 0)),
                pl.BlockSpec((tn, k1), lambda i, j: (j, c1 // k1)), pl.BlockSpec((tn, k2), lambda i, j: (j, c2 // k2))]
    args = [a1, a2, b, b]
    if prev is not None:
        in_specs.append(tile)
        args.append(prev)
    return pl.pallas_call(
        body, name=name, grid=(m // tm, n // tn), in_specs=in_specs, out_specs=tile,
        out_shape=jax.ShapeDtypeStruct((m, n), F32),
        compiler_params=_params(("parallel", "parallel")),
    )(*args)


def _ew(name, fn, ins, outs, rows, tm):
    tm = _pick(rows, tm, 16)
    in_specs, args = [], []
    for spec in ins:
        if spec[0] == "tile":
            _, arr, width, c0 = spec
            assert c0 % width == 0
            in_specs.append(pl.BlockSpec((tm, width), functools.partial(lambda i, o: (i, o), o=c0 // width)))
        else:
            arr = spec[1]
            in_specs.append(pl.BlockSpec(arr.shape, lambda i: (0, 0)))
        args.append(arr)
    out_specs, out_shape = [], []
    for kind, dt, width in outs:
        if kind == "tile":
            out_specs.append(pl.BlockSpec((tm, width), lambda i: (i, 0)))
            out_shape.append(jax.ShapeDtypeStruct((rows, width), dt))
        else:
            out_specs.append(pl.BlockSpec((1, width), lambda i: (0, 0)))
            out_shape.append(jax.ShapeDtypeStruct((1, width), dt))
    n_in = len(ins)
    has_acc = any(o[0] == "acc" for o in outs)

    def body(*refs):
        i = pl.program_id(0)
        vals = fn(i * tm, *[r[...] for r in refs[:n_in]])
        for (kind, _, _), r, v in zip(outs, refs[n_in:], vals):
            if kind == "tile":
                r[...] = v.astype(r.dtype)
            else:
                @pl.when(i == 0)
                def _():
                    r[...] = jnp.zeros_like(r)

                r[...] += v.astype(r.dtype)

    res = pl.pallas_call(
        body,
        name=name,
        grid=(rows // tm,),
        in_specs=in_specs,
        out_specs=out_specs,
        out_shape=out_shape,
        compiler_params=_params(("arbitrary",) if has_acc else ("parallel",)),
    )(*args)
    return res[0] if len(outs) == 1 else res


def _row_ids(row0, tm):
    return row0 + lax.broadcasted_iota(jnp.int32, (tm, 1), 0)


def _colsum(v):
    return jnp.sum(v, axis=0, keepdims=True)


def _rms_fwd_t(name, h, g, t, d):
    tm = _pick(t, 384, LANES)

    def body(x_ref, g_ref, y_ref, yt_ref):
        x = x_ref[...]
        y = x * lax.rsqrt(jnp.mean(x * x, axis=-1, keepdims=True) + EPS) * g_ref[...]
        y_ref[...] = y.astype(BF16)
        yt_ref[...] = y.T.astype(BF16)

    return pl.pallas_call(
        body, name=name, grid=(t // tm,),
        in_specs=[pl.BlockSpec((tm, d), lambda i: (i, 0)), pl.BlockSpec((1, d), lambda i: (0, 0))],
        out_specs=[pl.BlockSpec((tm, d), lambda i: (i, 0)), pl.BlockSpec((d, tm), lambda i: (0, i))],
        out_shape=[jax.ShapeDtypeStruct((t, d), BF16), jax.ShapeDtypeStruct((d, t), BF16)],
        compiler_params=_params(("parallel",)),
    )(h, g)


def _rms_bwd(name, h, g, dy, dres, t, d):
    def fn(row0, x, gg, dyv, dr):
        r = lax.rsqrt(jnp.mean(x * x, axis=-1, keepdims=True) + EPS)
        xh = x * r
        dxh = dyv * gg
        dx = r * (dxh - xh * jnp.mean(dxh * xh, axis=-1, keepdims=True))
        out = jnp.where(_row_ids(row0, x.shape[0]) >= PAD, dr + dx, 0.0)
        return out, _colsum(dyv * xh)

    return _ew(name, fn, [("tile", h, d, 0), ("full", g), ("tile", dy, d, 0), ("tile", dres, d, 0)],
               [("tile", F32, d), ("acc", F32, d)], t, 264)


def _rms_bwd_x(name, h, g, dy, dres, t, d):
    tm = ROW0

    def body(x_ref, g_ref, dy_ref, dr_ref, dx_ref, head_ref, dg_ref):
        i = pl.program_id(0)
        x = x_ref[...]
        r = lax.rsqrt(jnp.mean(x * x, axis=-1, keepdims=True) + EPS)
        xh = x * r
        dxh = dy_ref[...] * g_ref[...]
        dx = r * (dxh - xh * jnp.mean(dxh * xh, axis=-1, keepdims=True))
        out = jnp.where(_row_ids(i * tm, tm) >= PAD, dr_ref[...] + dx, 0.0)

        @pl.when(i == 0)
        def _():
            head_ref[...] = out
            dg_ref[...] = jnp.zeros_like(dg_ref)

        dx_ref[...] = out
        dg_ref[...] += _colsum(dy_ref[...] * xh)

    tile = pl.BlockSpec((tm, d), lambda i: (i, 0))
    fixed = lambda shape: pl.BlockSpec(shape, lambda i: (0, 0))
    return pl.pallas_call(
        body, name=name, grid=(t // tm,),
        in_specs=[tile, fixed((1, d)), tile, tile],
        out_specs=[pl.BlockSpec((tm, d), lambda i: (jnp.maximum(i - 1, 0), 0)), fixed((tm, d)), fixed((1, d))],
        out_shape=[jax.ShapeDtypeStruct((t - ROW0, d), F32), jax.ShapeDtypeStruct((ROW0, d), F32),
                   jax.ShapeDtypeStruct((1, d), F32)],
        compiler_params=_params(("arbitrary",)),
    )(h, g, dy, dres)


def _loss_head(h, g, target_p, t, d):
    def fn(row0, x, gg, tgt):
        real = _row_ids(row0, x.shape[0]) >= ROW0
        r = lax.rsqrt(jnp.mean(x * x, axis=-1, keepdims=True) + EPS)
        xh = x * r
        err = jnp.where(real, xh * gg - tgt, 0.0)
        loss_rows = 0.5 * jnp.mean(err * err, axis=-1, keepdims=True)
        dyv = err * (1.0 / d)
        dxh = dyv * gg
        dx = r * (dxh - xh * jnp.mean(dxh * xh, axis=-1, keepdims=True))
        loss_part = jnp.sum(loss_rows, axis=0, keepdims=True) * jnp.ones((1, LANES), F32)
        return jnp.where(real, dx, 0.0), _colsum(dyv * xh), loss_part

    return _ew("loss_head", fn, [("tile", h, d, 0), ("full", g), ("tile", target_p, d, 0)],
               [("tile", F32, d), ("acc", F32, d), ("acc", F32, LANES)], t, 264)


def _merge_fwd(a_fox, a_gla, proj, c_gates, t, d):
    def fn(row0, af, ag, gates):
        gates = gates.astype(F32)
        return (_sigmoid(gates[:, :d]) * af + _sigmoid(gates[:, d:]) * ag,)

    return _ew("merge_fwd", fn, [("tile", a_fox, d, 0), ("tile", a_gla, d, 0), ("tile", proj, 2 * d, c_gates)],
               [("tile", BF16, d)], t, 264)


def _merge_bwd(dy, a_fox, a_gla, proj, c_gates, t, d):
    def fn(row0, dyv, af, ag, gates):
        gates = gates.astype(F32)
        sf = _sigmoid(gates[:, :d])
        sg = _sigmoid(gates[:, d:])
        dgates = jnp.concatenate([dyv * af * sf * (1.0 - sf), dyv * ag * sg * (1.0 - sg)], axis=1)
        return dyv * sf, dyv * sg, dgates

    return _ew("merge_bwd", fn,
               [("tile", dy, d, 0), ("tile", a_fox, d, 0), ("tile", a_gla, d, 0), ("tile", proj, 2 * d, c_gates)],
               [("tile", BF16, d), ("tile", BF16, d), ("tile", BF16, 2 * d)], t, 264)


def _fox_gate_fwd(small, b_forget_p, t):
    tb = _pick(t, 384, LANES)

    def body(s_ref, b_ref, c_ref, carry_ref):
        i = pl.program_id(0)

        @pl.when(i == 0)
        def _():
            carry_ref[...] = jnp.zeros_like(carry_ref)

        logf = _log_sigmoid(s_ref[...] + b_ref[...])
        logf = jnp.where(_row_ids(i * tb, tb) >= PAD, logf, 0.0)
        r = lax.broadcasted_iota(jnp.int32, (tb, tb), 0)
        c = lax.broadcasted_iota(jnp.int32, (tb, tb), 1)
        tri = (c <= r).astype(BF16)
        cs = _tri_dot(tri, logf) + carry_ref[...]
        c_ref[...] = cs
        carry_ref[...] = cs[tb - 1:tb, :]

    return pl.pallas_call(
        body, name="fox_gate_fwd", grid=(t // tb,),
        in_specs=[pl.BlockSpec((tb, LANES), lambda i: (i, 0)), pl.BlockSpec((1, LANES), lambda i: (0, 0))],
        out_specs=pl.BlockSpec((tb, LANES), lambda i: (i, 0)),
        out_shape=jax.ShapeDtypeStruct((t, LANES), F32),
        scratch_shapes=[pltpu.VMEM((1, LANES), F32)],
        compiler_params=_params(("arbitrary",)),
    )(small, b_forget_p)


def _fox_gate_bwd(dc, small, b_forget_p, dga, t):
    tb = _pick(t, 384, LANES)
    nb = t // tb

    def body(dc_ref, s_ref, b_ref, dga_ref, ds_ref, db_ref, carry_ref):
        i = pl.program_id(0)

        @pl.when(i == 0)
        def _():
            carry_ref[...] = jnp.zeros_like(carry_ref)
            db_ref[...] = jnp.zeros_like(db_ref)

        r = lax.broadcasted_iota(jnp.int32, (tb, tb), 0)
        c = lax.broadcasted_iota(jnp.int32, (tb, tb), 1)
        tri = (c >= r).astype(BF16)
        dlogf = _tri_dot(tri, dc_ref[...]) + carry_ref[...]
        carry_ref[...] = dlogf[0:1, :]
        z = s_ref[...] + b_ref[...]
        dff = dlogf * _sigmoid(-z)
        lane = lax.broadcasted_iota(jnp.int32, (tb, LANES), 1)
        keep = (_row_ids((nb - 1 - i) * tb, tb) >= PAD) & (lane < FOX_HEADS)
        dff = jnp.where(keep, dff, 0.0)
        ds_ref[...] = dff + dga_ref[...]
        db_ref[...] += _colsum(dff)

    rev = lambda i: (nb - 1 - i, 0)
    return pl.pallas_call(
        body, name="fox_gate_bwd", grid=(nb,),
        in_specs=[pl.BlockSpec((tb, LANES), rev), pl.BlockSpec((tb, LANES), rev),
                  pl.BlockSpec((1, LANES), lambda i: (0, 0)), pl.BlockSpec((tb, LANES), rev)],
        out_specs=[pl.BlockSpec((tb, LANES), rev), pl.BlockSpec((1, LANES), lambda i: (0, 0))],
        out_shape=[jax.ShapeDtypeStruct((t, LANES), F32), jax.ShapeDtypeStruct((1, LANES), F32)],
        scratch_shapes=[pltpu.VMEM((1, LANES), F32)],
        compiler_params=_params(("arbitrary",)),
    )(dc, small, b_forget_p, dga)


def _fox_pairs(nb, by_key):
    if by_key:
        pairs = [(qi, ki) for ki in range(nb) for qi in range(ki, nb)]
    else:
        pairs = [(qi, ki) for qi in range(nb) for ki in range(qi + 1)]
    return (jnp.asarray(np.array([p[0] for p in pairs], np.int32)),
            jnp.asarray(np.array([p[1] for p in pairs], np.int32)), len(pairs))


def _fox_specs(tb, fd, c_fq, c_fkv):
    gw = FOX_GROUP * fd
    q0, kv0 = c_fq // gw, c_fkv // (2 * gw)
    return dict(
        q=pl.BlockSpec((tb, gw), lambda g, p, qt, kt: (qt[p], q0 + g)),
        kv=pl.BlockSpec((tb, 2 * gw), lambda g, p, qt, kt: (kt[p], kv0 + g)),
        col=pl.BlockSpec((FOX_GROUP, tb, 1), lambda g, p, qt, kt: (g, qt[p], 0)),
        row=pl.BlockSpec((FOX_GROUP, 1, tb), lambda g, p, qt, kt: (g, 0, kt[p])),
        head=pl.BlockSpec((tb, gw), lambda g, p, qt, kt: (qt[p], g)),
        key_kv=pl.BlockSpec((tb, 2 * gw), lambda g, p, qt, kt: (kt[p], g)),
    )


def _fox_mask(qi, ki, tb):
    row = qi * tb + lax.broadcasted_iota(jnp.int32, (tb, tb), 0)
    col = ki * tb + lax.broadcasted_iota(jnp.int32, (tb, tb), 1)
    return (col <= row) & (col >= PAD)


def _fox_heads(q_ref, kv_ref, fd):
    return [(q_ref[:, hh * fd:(hh + 1) * fd], kv_ref[:, 2 * hh * fd:(2 * hh + 1) * fd],
             kv_ref[:, (2 * hh + 1) * fd:(2 * hh + 2) * fd]) for hh in range(FOX_GROUP)]


def _fox_fwd(proj, c_col, c_row, t, fd, c_fq, c_fkv):
    tb = _pick(t, 384, LANES)
    nb = t // tb
    scale = fd ** -0.5
    sp = _fox_specs(tb, fd, c_fq, c_fkv)
    qt, kt, npairs = _fox_pairs(nb, by_key=False)

    def body(qt_ref, kt_ref, q_ref, kv_ref, cq_ref, ck_ref, o_ref, lse_ref, m_ref, l_ref, acc_ref):
        p = pl.program_id(1)
        qi, ki = qt_ref[p], kt_ref[p]

        @pl.when(ki == 0)
        def _():
            m_ref[...] = jnp.full_like(m_ref, -jnp.inf)
            l_ref[...] = jnp.zeros_like(l_ref)
            acc_ref[...] = jnp.zeros_like(acc_ref)

        def update(masked):
            mask = _fox_mask(qi, ki, tb) if masked else None
            for hh, (q, k, v) in enumerate(_fox_heads(q_ref, kv_ref, fd)):
                s = _dot(q, k, NT) * scale + cq_ref[hh] - ck_ref[hh]
                if masked:
                    s = jnp.where(mask, s, MASK_VALUE)
                m_prev = m_ref[hh]
                m_new = jnp.maximum(m_prev, jnp.max(s, axis=-1, keepdims=True))
                alpha = jnp.exp(m_prev - m_new)
                pe = jnp.exp(s - m_new)
                l_ref[hh] = alpha * l_ref[hh] + jnp.sum(pe, axis=-1, keepdims=True)
                acc_ref[hh] = alpha * acc_ref[hh] + _dot(pe.astype(BF16), v, NN)
                m_ref[hh] = m_new

        edge = (ki == 0) | (ki == qi)
        pl.when(edge)(functools.partial(update, True))
        pl.when(jnp.logical_not(edge))(functools.partial(update, False))

        @pl.when(ki == qi)
        def _():
            real = _row_ids(qi * tb, tb) >= PAD
            for hh in range(FOX_GROUP):
                o_ref[:, hh * fd:(hh + 1) * fd] = jnp.where(real, acc_ref[hh] / l_ref[hh], 0.0)
                lse_ref[hh] = m_ref[hh] + jnp.log(l_ref[hh])

    return pl.pallas_call(
        body, name="fox_fwd",
        grid_spec=pltpu.PrefetchScalarGridSpec(
            num_scalar_prefetch=2, grid=(FOX_HEADS // FOX_GROUP, npairs),
            in_specs=[sp["q"], sp["kv"], sp["col"], sp["row"]],
            out_specs=[sp["head"], sp["col"]],
            scratch_shapes=[pltpu.VMEM((FOX_GROUP, tb, 1), F32), pltpu.VMEM((FOX_GROUP, tb, 1), F32),
                            pltpu.VMEM((FOX_GROUP, tb, fd), F32)]),
        out_shape=[jax.ShapeDtypeStruct((t, FOX_HEADS * fd), F32), jax.ShapeDtypeStruct((FOX_HEADS, t, 1), F32)],
        compiler_params=_params(("parallel", "arbitrary")),
    )(qt, kt, proj, proj, c_col, c_row)


def _fox_delta(o_fox, do_fox, t, fd):
    tb = _pick(t, 384, LANES)

    def body(o_ref, do_ref, out_ref):
        for h in range(FOX_HEADS):
            sl = slice(h * fd, (h + 1) * fd)
            out_ref[h] = jnp.sum(o_ref[:, sl] * do_ref[:, sl].astype(BF16).astype(F32), axis=-1, keepdims=True)

    w = FOX_HEADS * fd
    return pl.pallas_call(
        body, name="fox_delta", grid=(t // tb,),
        in_specs=[pl.BlockSpec((tb, w), lambda i: (i, 0)), pl.BlockSpec((tb, w), lambda i: (i, 0))],
        out_specs=pl.BlockSpec((FOX_HEADS, tb, 1), lambda i: (0, i, 0)),
        out_shape=jax.ShapeDtypeStruct((FOX_HEADS, t, 1), F32),
        compiler_params=_params(("parallel",)),
    )(o_fox, do_fox)


def _fox_bwd(proj, c_col, c_row, lse, delta, do_fox, t, fd, c_fq, c_fkv):
    tb = _pick(t, 384, LANES)
    nb = t // tb
    scale = fd ** -0.5
    sp = _fox_specs(tb, fd, c_fq, c_fkv)
    qt, kt, npairs = _fox_pairs(nb, by_key=True)
    gw = FOX_GROUP * fd

    def body(qt_ref, kt_ref, q_ref, kv_ref, cq_ref, ck_ref, lse_ref, dl_ref, do_ref, dq_ref, dkv_ref, dc_ref, dr_ref,
             dq_acc, dk_acc, dv_acc, dc_acc, dr_acc):
        p = pl.program_id(1)
        qi, ki = qt_ref[p], kt_ref[p]

        @pl.when(p == 0)
        def _():
            dq_acc[...] = jnp.zeros_like(dq_acc)
            dr_acc[...] = jnp.zeros_like(dr_acc)

        @pl.when(qi == ki)
        def _():
            dk_acc[...] = jnp.zeros_like(dk_acc)
            dv_acc[...] = jnp.zeros_like(dv_acc)
            dc_acc[...] = jnp.zeros_like(dc_acc)

        rows = pl.ds(pl.multiple_of(qi * tb, LANES), tb)

        def update(masked):
            mask = _fox_mask(qi, ki, tb) if masked else None
            for hh, (q, k, v) in enumerate(_fox_heads(q_ref, kv_ref, fd)):
                do = _bf(do_ref[:, hh * fd:(hh + 1) * fd])
                s = _dot(q, k, NT) * scale + cq_ref[hh] - ck_ref[hh]
                if masked:
                    s = jnp.where(mask, s, MASK_VALUE)
                pr = jnp.exp(s - lse_ref[hh])
                dp = _dot(do, v, NT)
                ds = pr * (dp - dl_ref[hh])
                ds16 = ds.astype(BF16)
                dv_acc[hh] += _dot(pr.astype(BF16), do, TN)
                dk_acc[hh] += _dot(ds16, q, TN)
                dc_acc[hh] += _colsum(ds)
                dr_acc[hh, rows, :] += jnp.sum(ds, axis=-1, keepdims=True)
                dq_acc[hh, rows, :] += _dot(ds16, k, NN)

        edge = (ki == 0) | (ki == qi)
        pl.when(edge)(functools.partial(update, True))
        pl.when(jnp.logical_not(edge))(functools.partial(update, False))

        @pl.when(qi == nb - 1)
        def _():
            for hh in range(FOX_GROUP):
                dkv_ref[:, 2 * hh * fd:(2 * hh + 1) * fd] = (dk_acc[hh] * scale).astype(dkv_ref.dtype)
                dkv_ref[:, (2 * hh + 1) * fd:(2 * hh + 2) * fd] = dv_acc[hh].astype(dkv_ref.dtype)
                dc_ref[hh] = -dc_acc[hh]

        @pl.when(p == npairs - 1)
        def _():
            for hh in range(FOX_GROUP):
                dq_ref[:, hh * fd:(hh + 1) * fd] = (dq_acc[hh] * scale).astype(dq_ref.dtype)
            dr_ref[...] = dr_acc[...]

    return pl.pallas_call(
        body, name="fox_bwd",
        grid_spec=pltpu.PrefetchScalarGridSpec(
            num_scalar_prefetch=2, grid=(FOX_HEADS // FOX_GROUP, npairs),
            in_specs=[sp["q"], sp["kv"], sp["col"], sp["row"], sp["col"], sp["col"], sp["head"]],
            out_specs=[pl.BlockSpec((t, gw), lambda g, p, qt, kt: (0, g)), sp["key_kv"], sp["row"],
                       pl.BlockSpec((FOX_GROUP, t, 1), lambda g, p, qt, kt: (g, 0, 0))],
            scratch_shapes=[pltpu.VMEM((FOX_GROUP, t, fd), F32), pltpu.VMEM((FOX_GROUP, tb, fd), F32),
                            pltpu.VMEM((FOX_GROUP, tb, fd), F32), pltpu.VMEM((FOX_GROUP, 1, tb), F32),
                            pltpu.VMEM((FOX_GROUP, t, 1), F32)]),
        out_shape=[jax.ShapeDtypeStruct((t, FOX_HEADS * fd), BF16), jax.ShapeDtypeStruct((t, 2 * FOX_HEADS * fd), BF16),
                   jax.ShapeDtypeStruct((FOX_HEADS, 1, t), F32), jax.ShapeDtypeStruct((FOX_HEADS, t, 1), F32)],
        compiler_params=_params(("parallel", "arbitrary")),
    )(qt, kt, proj, proj, c_col, c_row, lse, delta, do_fox)


def _gla_gate_fwd(small, w_alpha_p, b_alpha, t, gk):
    def fn(row0, s, w, b):
        z = _dot(s.astype(BF16), w, NN) + b
        return (jnp.where(_row_ids(row0, s.shape[0]) >= PAD, _log_sigmoid(z) * (1.0 / GLA_TAU), 0.0),)

    return _ew("gla_gate_fwd", fn, [("tile", small, LANES, 0), ("full", w_alpha_p), ("full", b_alpha)],
               [("tile", F32, gk)], t, 264)


def _gla_gate_bwd(dglog, small, w_alpha_p, b_alpha, t, gk):
    def fn(row0, dg, s, w, b):
        z = _dot(s.astype(BF16), w, NN) + b
        dz = jnp.where(_row_ids(row0, s.shape[0]) >= PAD, dg * (1.0 / GLA_TAU) * _sigmoid(-z), 0.0)
        return dz, _colsum(dz)

    return _ew("gla_gate_bwd", fn,
               [("tile", dglog, gk, 0), ("tile", small, LANES, 0), ("full", w_alpha_p), ("full", b_alpha)],
               [("tile", BF16, gk), ("acc", F32, gk)], t, 264)


def _gla_chunk(q, k, g, scale, cs):
    r = lax.broadcasted_iota(jnp.int32, (cs, cs), 0)
    c = lax.broadcasted_iota(jnp.int32, (cs, cs), 1)
    causal = c <= r
    b = _tri_dot(causal.astype(BF16), g)
    bl = b[cs - 1:cs, :]
    eb, einv, eend = jnp.exp(b), jnp.exp(-b), jnp.exp(bl - b)
    qd = q.astype(F32) * scale * eb
    kf = k.astype(F32)
    return causal, (eb, einv, eend), bl, qd, kf * einv, kf * eend


def _gla_fwd(proj, glog, t, dk, dv, c_q, c_k, c_v):
    cs = GLA_CHUNK
    nc = t // cs
    wk, wv = GLA_HEADS * dk, GLA_HEADS * dv
    scale = dk ** -0.5

    def body(q_ref, k_ref, v_ref, g_ref, o_ref, sp_ref, st_ref):
        @pl.when(pl.program_id(0) == 0)
        def _():
            st_ref[...] = jnp.zeros_like(st_ref)

        for h in range(GLA_HEADS):
            ks, vs = slice(h * dk, (h + 1) * dk), slice(h * dv, (h + 1) * dv)
            v = v_ref[:, vs]
            causal, _, bl, qd, ki, ke = _gla_chunk(q_ref[:, ks], k_ref[:, ks], g_ref[:, ks], scale, cs)
            st = st_ref[h]
            sp_ref[h] = st
            a = jnp.where(causal, _dot(qd.astype(BF16), ki.astype(BF16), NT), 0.0)
            o_ref[:, vs] = _dot(a.astype(BF16), v, NN) + _dot(qd.astype(BF16), st.astype(BF16), NT)
            st_ref[h] = st * jnp.exp(bl) + _dot(v, ke.astype(BF16), TN)

    return pl.pallas_call(
        body, name="gla_fwd", grid=(nc,),
        in_specs=[pl.BlockSpec((cs, wk), lambda n: (n, c_q // wk)), pl.BlockSpec((cs, wk), lambda n: (n, c_k // wk)),
                  pl.BlockSpec((cs, wv), lambda n: (n, c_v // wv)), pl.BlockSpec((cs, wk), lambda n: (n, 0))],
        out_specs=[pl.BlockSpec((cs, wv), lambda n: (n, 0)),
                   pl.BlockSpec((None, GLA_HEADS, dv, dk), lambda n: (n, 0, 0, 0))],
        out_shape=[jax.ShapeDtypeStruct((t, wv), F32), jax.ShapeDtypeStruct((nc, GLA_HEADS, dv, dk), F32)],
        scratch_shapes=[pltpu.VMEM((GLA_HEADS, dv, dk), F32)],
        compiler_params=_params(("arbitrary",)),
    )(proj, proj, proj, glog)


def _gla_bwd(proj, glog, s_prev, do_raw, t, dk, dv, c_q, c_k, c_v):
    cs = GLA_CHUNK
    nc = t // cs
    wk, wv = GLA_HEADS * dk, GLA_HEADS * dv
    scale = dk ** -0.5

    def body(q_ref, k_ref, v_ref, g_ref, sp_ref, do_ref, dq_ref, dk_ref, dv_ref, dg_ref, dst_ref):
        @pl.when(pl.program_id(0) == 0)
        def _():
            dst_ref[...] = jnp.zeros_like(dst_ref)

        for h in range(GLA_HEADS):
            ks, vs = slice(h * dk, (h + 1) * dk), slice(h * dv, (h + 1) * dv)
            v = v_ref[:, vs]
            do = do_ref[:, vs].astype(BF16)
            causal, (eb, einv, eend), bl, qd, ki, ke = _gla_chunk(q_ref[:, ks], k_ref[:, ks], g_ref[:, ks], scale, cs)
            qd16, ki16, ke16 = qd.astype(BF16), ki.astype(BF16), ke.astype(BF16)
            st = sp_ref[h]
            dst = dst_ref[h]
            dst16 = dst.astype(BF16)
            a = jnp.where(causal, _dot(qd16, ki16, NT), 0.0).astype(BF16)
            da = jnp.where(causal, _dot(do, v, NT), 0.0).astype(BF16)
            dvv = _dot(a, do, TN) + _dot(ke16, dst16, NT)
            dqd = _dot(da, ki16, NN) + _dot(do, st.astype(BF16), NN)
            dki = _dot(da, qd16, TN)
            dke = _dot(v, dst16, NN)
            dl = jnp.exp(bl)
            ddl = _colsum(dst * st)
            dst_ref[h] = dst * dl + _dot(do, qd16, TN)
            dq_ref[:, ks] = (dqd * eb * scale).astype(dq_ref.dtype)
            dk_ref[:, ks] = (dki * einv + dke * eend).astype(dk_ref.dtype)
            dv_ref[:, vs] = dvv.astype(dv_ref.dtype)
            db = dqd * qd - dki * ki - dke * ke
            db_last = _colsum(dke * ke) + ddl * dl
            r = lax.broadcasted_iota(jnp.int32, (cs, cs), 0)
            c = lax.broadcasted_iota(jnp.int32, (cs, cs), 1)
            dg_ref[:, ks] = _tri_dot((c >= r).astype(BF16), db) + db_last

    rev = lambda f: (lambda n: f(nc - 1 - n))
    return pl.pallas_call(
        body, name="gla_bwd", grid=(nc,),
        in_specs=[pl.BlockSpec((cs, wk), rev(lambda n: (n, c_q // wk))), pl.BlockSpec((cs, wk), rev(lambda n: (n, c_k // wk))),
                  pl.BlockSpec((cs, wv), rev(lambda n: (n, c_v // wv))), pl.BlockSpec((cs, wk), rev(lambda n: (n, 0))),
                  pl.BlockSpec((None, GLA_HEADS, dv, dk), rev(lambda n: (n, 0, 0, 0))),
                  pl.BlockSpec((cs, wv), rev(lambda n: (n, 0)))],
        out_specs=[pl.BlockSpec((cs, wk), rev(lambda n: (n, 0))), pl.BlockSpec((cs, wk), rev(lambda n: (n, 0))),
                   pl.BlockSpec((cs, wv), rev(lambda n: (n, 0))), pl.BlockSpec((cs, wk), rev(lambda n: (n, 0)))],
        out_shape=[jax.ShapeDtypeStruct((t, wk), BF16), jax.ShapeDtypeStruct((t, wk), BF16),
                   jax.ShapeDtypeStruct((t, wv), BF16), jax.ShapeDtypeStruct((t, wk), F32)],
        scratch_shapes=[pltpu.VMEM((GLA_HEADS, dv, dk), F32)],
        compiler_params=_params(("arbitrary",)),
    )(proj, proj, proj, glog, s_prev, do_raw)


def _gla_post_fwd(o_raw, proj, gn, t, dv, c_gr):
    w = GLA_HEADS * dv

    def fn(row0, o, gr, g):
        gr = gr.astype(F32)
        outs = []
        for h in range(GLA_HEADS):
            oh = o[:, h * dv:(h + 1) * dv]
            outs.append(oh * lax.rsqrt(jnp.mean(oh * oh, axis=-1, keepdims=True) + EPS))
        on = jnp.concatenate(outs, axis=1) * g
        return (on * (gr * _sigmoid(gr)),)

    return _ew("gla_post_fwd", fn, [("tile", o_raw, w, 0), ("tile", proj, w, c_gr), ("full", gn)],
               [("tile", BF16, w)], t, 264)


def _gla_post_bwd(o_raw, proj, gn, do_gla, t, dv, c_gr):
    w = GLA_HEADS * dv

    def fn(row0, o, gr, g, do):
        gr = gr.astype(F32)
        sg = _sigmoid(gr)
        don = do * (gr * sg)
        ohs, dos = [], []
        for h in range(GLA_HEADS):
            sl = slice(h * dv, (h + 1) * dv)
            oh = o[:, sl]
            r = lax.rsqrt(jnp.mean(oh * oh, axis=-1, keepdims=True) + EPS)
            xh = oh * r
            dxh = don[:, sl] * g[:, sl]
            ohs.append(xh)
            dos.append(r * (dxh - xh * jnp.mean(dxh * xh, axis=-1, keepdims=True)))
        xh = jnp.concatenate(ohs, axis=1)
        dgr = do * (xh * g) * (sg * (1.0 + gr * (1.0 - sg)))
        return jnp.concatenate(dos, axis=1), dgr, _colsum(don * xh)

    return _ew("gla_post_bwd", fn,
               [("tile", o_raw, w, 0), ("tile", proj, w, c_gr), ("full", gn), ("tile", do_gla, w, 0)],
               [("tile", F32, w), ("tile", BF16, w), ("acc", F32, w)], t, 264)


def _adamw(name, w, g, m, v, layer=None, into=None):
    nl, rows, cols = w.shape
    tm = _pick(rows, max(8, (512 * 1024) // max(cols, 1) // 8 * 8), 8)

    def body(w_ref, g_ref, m_ref, v_ref, *rest):
        go_ref, d_ref, nm_ref, nv_ref = rest[-4:]
        gg = g_ref[...]
        nm = ADAM_B1 * m_ref[...] + (1.0 - ADAM_B1) * gg
        nv = ADAM_B2 * v_ref[...] + (1.0 - ADAM_B2) * (gg * gg)
        m_hat = nm / (1.0 - ADAM_B1 ** ADAM_STEP)
        v_hat = nv / (1.0 - ADAM_B2 ** ADAM_STEP)
        go_ref[...] = gg
        d_ref[...] = -ADAM_LR * (m_hat / (jnp.sqrt(v_hat) + ADAM_EPS) + ADAM_WD * w_ref[...])
        nm_ref[...] = nm
        nv_ref[...] = nv

    out_shape = [jax.ShapeDtypeStruct((nl, rows, cols), F32)] * 4
    if layer is None:
        spec = pl.BlockSpec((None, tm, cols), lambda l, i: (l, i, 0))
        return pl.pallas_call(
            body, name=name, grid=(nl, rows // tm), in_specs=[spec] * 4, out_specs=[spec] * 4, out_shape=out_shape,
            compiler_params=_params(("parallel", "parallel")),
        )(w, g, m, v)
    spec = pl.BlockSpec((None, tm, cols), lambda i: (layer, i, 0))
    in_specs = [spec, pl.BlockSpec((tm, cols), lambda i: (i, 0)), spec, spec]
    args, aliases = [w, g, m, v], {}
    if into is not None:
        in_specs += [pl.BlockSpec(memory_space=pl.ANY)] * 4
        args += list(into)
        aliases = {4 + k: k for k in range(4)}
    return pl.pallas_call(
        body, name=name, grid=(rows // tm,), in_specs=in_specs, out_specs=[spec] * 4, out_shape=out_shape,
        input_output_aliases=aliases, compiler_params=_params(("parallel",)),
    )(*args)


def _me():
    return lax.axis_index("x"), lax.axis_index("y"), lax.axis_index("c")


def _hbm_specs(n):
    return [pl.BlockSpec(memory_space=pl.ANY)] * n


_HBM = pl.BlockSpec(memory_space=pltpu.HBM)
_SEM = pl.BlockSpec(memory_space=pltpu.SEMAPHORE)
_EFFECT = pltpu.SideEffectType.DATAFLOW_SIDE_EFFECTING
N_PEERS = N_CHIPS - 1


def _other_chips(x, y):
    return [(1 - x, y), (x, 1 - y), (1 - x, 1 - y)]


def _split_copies(plan, src, land, send_sems, recv_sems):
    me = _me()
    copies = []
    for i in range(len(src)):
        for j, (s, d, peer) in enumerate(plan(src[i], land[i], me)):
            k = plan.copies * i + j
            copies.append(pltpu.make_async_remote_copy(
                src_ref=s, dst_ref=d, send_sem=send_sems.at[k], recv_sem=recv_sems.at[k], device_id=peer,
                device_id_type=MESH))
    return copies


def _split_start(name, srcs, lands, plan, after=None):
    n = len(srcs)
    extra = [] if after is None else [after]

    def body(*refs):
        src, land = refs[:n], refs[n:2 * n]
        send_sems, recv_sems = refs[2 * n + len(extra)], refs[2 * n + len(extra) + 1]
        token = refs[-1]
        for cp in _split_copies(plan, src, land, send_sems, recv_sems):
            cp.start()
        token[...] = jnp.zeros_like(token)

    out_shape = ([pltpu.SemaphoreType.DMA((plan.copies * n,)), pltpu.SemaphoreType.DMA((plan.copies * n,))]
                 + [pltpu.HBM(a.shape, a.dtype) for a in list(srcs) + list(lands)]
                 + [jax.ShapeDtypeStruct((8, LANES), F32)])
    res = pl.pallas_call(
        body, name=name, out_shape=out_shape,
        in_specs=[_HBM] * (2 * n) + [pl.BlockSpec(memory_space=pl.ANY)] * len(extra),
        out_specs=[_SEM, _SEM] + [_HBM] * (2 * n) + [pl.BlockSpec(memory_space=pltpu.VMEM)],
        input_output_aliases={i: 2 + i for i in range(2 * n)},
        compiler_params=pltpu.CompilerParams(has_side_effects=_EFFECT),
    )(*[pltpu.with_memory_space_constraint(a, pltpu.HBM) for a in list(srcs) + list(lands)], *extra)
    return res[0], res[1], res[2:2 + n], res[2 + n:2 + 2 * n], res[-1]


def _split_wait(name, send_sems, recv_sems, srcs, lands, plan, after):
    n = len(srcs)

    def body(*refs):
        src, land = refs[:n], refs[n:2 * n]
        s_sems, r_sems = refs[2 * n], refs[2 * n + 1]
        for cp in _split_copies(plan, src, land, s_sems, r_sems):
            cp.wait_send()
            cp.wait_recv()

    res = pl.pallas_call(
        body, name=name, out_shape=[pltpu.HBM(a.shape, a.dtype) for a in list(srcs) + list(lands)],
        in_specs=[_HBM] * (2 * n) + [_SEM, _SEM, pl.BlockSpec(memory_space=pl.ANY)], out_specs=[_HBM] * (2 * n),
        input_output_aliases={i: i for i in range(2 * n)},
        compiler_params=pltpu.CompilerParams(has_side_effects=_EFFECT),
    )(*srcs, *lands, send_sems, recv_sems, after)
    return res[:n], res[n:]


def _exchange_start(name, bufs, plan, after):
    n = len(bufs)

    def body(*refs):
        buf = refs[:n]
        for cp in _split_copies(plan, buf, buf, refs[n + 1], refs[n + 2]):
            cp.start()
        refs[-1][...] = jnp.zeros_like(refs[-1])

    out_shape = ([pltpu.SemaphoreType.DMA((plan.copies * n,)), pltpu.SemaphoreType.DMA((plan.copies * n,))]
                 + [pltpu.HBM(a.shape, a.dtype) for a in bufs] + [jax.ShapeDtypeStruct((8, LANES), F32)])
    res = pl.pallas_call(
        body, name=name, out_shape=out_shape, in_specs=[_HBM] * n + [pl.BlockSpec(memory_space=pl.ANY)],
        out_specs=[_SEM, _SEM] + [_HBM] * n + [pl.BlockSpec(memory_space=pltpu.VMEM)],
        input_output_aliases={i: 2 + i for i in range(n)},
        compiler_params=pltpu.CompilerParams(has_side_effects=_EFFECT),
    )(*[pltpu.with_memory_space_constraint(a, pltpu.HBM) for a in bufs], after)
    return res[0], res[1], res[2:2 + n], res[-1]


def _exchange_wait(name, send_sems, recv_sems, bufs, plan, after):
    n = len(bufs)

    def body(*refs):
        buf = refs[:n]
        for cp in _split_copies(plan, buf, buf, refs[n], refs[n + 1]):
            cp.wait_send()
            cp.wait_recv()

    return pl.pallas_call(
        body, name=name, out_shape=[pltpu.HBM(a.shape, a.dtype) for a in bufs],
        in_specs=[_HBM] * n + [_SEM, _SEM, pl.BlockSpec(memory_space=pl.ANY)], out_specs=[_HBM] * n,
        input_output_aliases={i: i for i in range(n)},
        compiler_params=pltpu.CompilerParams(has_side_effects=_EFFECT),
    )(*bufs, send_sems, recv_sems, after)


def _forward_plan(src, land, me):
    x, y, c = me
    rows = _half(src.shape[1], c)
    return [(src.at[2 * px + py, rows], land.at[2 * px + py, rows], (x, y, 1 - c)) for px, py in _other_chips(x, y)]


def _share_plan(src, land, me):
    x, y, c = me
    rows = _half(src.shape[0], c)
    return [(src.at[rows], land.at[rows], (x, y, 1 - c))]


_forward_plan.copies = N_CHIPS - 1
_share_plan.copies = 1


def _half(ref_rows, c):
    half = ref_rows // 2
    return pl.ds(c * half, half)


def _gather_plan(src, land, me):
    x, y, c = me
    rows = _half(src.shape[0], c)
    return [(src.at[rows], land.at[2 * x + y, rows], (px, py, c)) for px, py in _other_chips(x, y)]


def _scatter_plan(src, land, me):
    x, y, c = me
    return [(src.at[2 * px + py], land.at[2 * x + y], (px, py, c)) for px, py in _other_chips(x, y)]


def _swap_plan(src, land, me):
    x, y, c = me
    return [(src.at[:, _half(src.shape[1], 1 - c)], land, (x, y, 1 - c))]


_gather_plan.copies = N_PEERS
_scatter_plan.copies = N_PEERS
_swap_plan.copies = 1


def _all_gather_small(v):
    def body(v_ref, out_ref, send_sems, recv_sems, local_sem):
        x, y, c = _me()
        mine = pltpu.make_async_copy(v_ref, out_ref.at[4 * x + 2 * y + c], local_sem)
        mine.start()
        copies = []
        for k in range(1, N_DEV):
            peer = (x ^ ((k >> 2) & 1), y ^ ((k >> 1) & 1), c ^ (k & 1))
            cp = pltpu.make_async_remote_copy(
                src_ref=v_ref, dst_ref=out_ref.at[4 * x + 2 * y + c], send_sem=send_sems.at[k - 1],
                recv_sem=recv_sems.at[k - 1], device_id=peer, device_id_type=MESH)
            cp.start()
            copies.append(cp)
        for cp in copies:
            cp.wait_recv()
        for cp in copies:
            cp.wait_send()
        mine.wait()

    return pl.pallas_call(
        body, name="all_gather_small", in_specs=_hbm_specs(1), out_specs=pl.BlockSpec(memory_space=pl.ANY),
        out_shape=jax.ShapeDtypeStruct((N_DEV,) + v.shape, v.dtype),
        scratch_shapes=[pltpu.SemaphoreType.DMA((N_DEV - 1,)), pltpu.SemaphoreType.DMA((N_DEV - 1,)),
                        pltpu.SemaphoreType.DMA],
    )(v)


def _pair_sum(name, own, recv, chip, core):
    nch, half, cdim = recv.shape
    tm = _pick(half, max(8, (512 * 1024) // cdim // 16 * 16), 16)
    nt = half // tm

    def body(c_ref, k_ref, a_ref, b_ref, s16_ref):
        s16_ref[...] = (a_ref[...] + b_ref[...]).astype(BF16)

    other = lambda j, c: (c[0] + 1 + j) % nch
    spec = pl.BlockSpec((None, tm, cdim), lambda j, i, c, k: (other(j, c), i, 0))
    return pl.pallas_call(
        body, name=name,
        grid_spec=pltpu.PrefetchScalarGridSpec(
            num_scalar_prefetch=2, grid=(nch - 1, nt),
            in_specs=[pl.BlockSpec((None, tm, cdim), lambda j, i, c, k: (other(j, c), k[0] * nt + i, 0)), spec],
            out_specs=spec),
        out_shape=jax.ShapeDtypeStruct((nch, half, cdim), BF16),
        compiler_params=_params(("parallel", "parallel")),
    )(chip, core, own, recv)


def _chip_sum(name, own, recv, landed, chip, core):
    nch, half, cdim = recv.shape
    tm = _pick(half, max(8, (512 * 1024) // cdim // 16 * 16), 16)
    nt = half // tm

    def body(c_ref, k_ref, own_ref, recv_ref, *rest):
        landed_refs, out_ref = rest[:nch], rest[-1]
        me = c_ref[0]
        mine = own_ref[...] + recv_ref[...]
        acc = None
        for j in range(nch):
            term = jnp.where(me == j, mine, landed_refs[j][...].astype(F32))
            acc = term if acc is None else acc + term
        out_ref[...] = acc

    landed_specs = [pl.BlockSpec((None, tm, cdim), functools.partial(lambda i, c, k, j: (j, i, 0), j=j))
                    for j in range(nch)]
    return pl.pallas_call(
        body, name=name,
        grid_spec=pltpu.PrefetchScalarGridSpec(
            num_scalar_prefetch=2, grid=(nt,),
            in_specs=[pl.BlockSpec((None, tm, cdim), lambda i, c, k: (c[0], k[0] * nt + i, 0)),
                      pl.BlockSpec((None, tm, cdim), lambda i, c, k: (c[0], i, 0))] + landed_specs,
            out_specs=pl.BlockSpec((tm, cdim), lambda i, c, k: (k[0] * nt + i, 0))),
        out_shape=jax.ShapeDtypeStruct((2 * half, cdim), F32),
        compiler_params=_params(("parallel",)),
    )(chip, core, own, recv, *([landed] * nch))


def _sum_devices(gathered):
    _, r, cdim = gathered.shape

    def body(g_ref, o_ref):
        acc = g_ref[0]
        for k in range(1, N_DEV):
            acc = acc + g_ref[k]
        o_ref[...] = acc

    return pl.pallas_call(
        body, name="sum_devices", out_shape=jax.ShapeDtypeStruct((r, cdim), F32),
        compiler_params=_params(),
    )(gathered)


class _Layout:
    def __init__(self, d):
        self.d = d
        self.fw = d // 2
        self.fd = self.fw // FOX_HEADS
        self.gk = d // 2
        self.gv = d
        self.dk = self.gk // GLA_HEADS
        self.dv = self.gv // GLA_HEADS
        self.c_fq = 0
        self.c_gq = self.fw
        self.c_gv = self.c_gq + self.gk
        self.c_gr = self.c_gv + self.gv
        self.c_fkv = self.c_gr + self.gv
        self.c_gates = self.c_fkv + 2 * self.fw
        self.c_gk = self.c_gates + 2 * d
        self.c_small = self.c_gk + self.gk
        self.n_main = self.c_small
        self.n_p = self.c_small + LANES
        self.o_fk = self.fw
        self.o_fv = 2 * self.fw
        self.o_ff = 3 * self.fw
        self.o_gq = self.o_ff + FOX_HEADS
        self.o_gk = self.o_gq + self.gk
        self.o_gv = self.o_gk + self.gk
        self.o_gr = self.o_gv + self.gv
        self.o_ga = self.o_gr + self.gv
        self.o_gf = self.o_ga + GLA_RANK
        self.o_gg = self.o_gf + d
        self.n_orig = self.o_gg + d

    def to_p(self, shards):
        per = self.n_orig // N_CHIPS
        ranges = [(0, self.fw), (self.o_gq, self.gk), (self.o_gv, self.gv), (self.o_gr, self.gv)]
        for h in range(FOX_HEADS):
            ranges += [(self.o_fk + h * self.fd, self.fd), (self.o_fv + h * self.fd, self.fd)]
        ranges += [(self.o_gf, 2 * self.d), (self.o_gk, self.gk), (self.o_ff, FOX_HEADS), (self.o_ga, GLA_RANK)]
        pieces = []
        for a, width in ranges:
            for j in range(a // per, (a + width - 1) // per + 1):
                lo, hi = max(a, j * per), min(a + width, (j + 1) * per)
                pieces.append(shards[j][:, lo - j * per:hi - j * per])
        pieces.append(jnp.zeros((shards.shape[1], LANES - FOX_HEADS - GLA_RANK), shards.dtype))
        return jnp.concatenate(pieces, axis=1)

    def from_segments(self, seg):
        per = self.n_orig // N_CHIPS
        fd = self.fd
        atoms = [("fq", 0, self.fw)]
        atoms += [("fkv", 2 * h * fd, fd) for h in range(FOX_HEADS)]
        atoms += [("fkv", (2 * h + 1) * fd, fd) for h in range(FOX_HEADS)]
        atoms += [("small", 0, FOX_HEADS), ("gq", 0, self.gk), ("gk", 0, self.gk), ("gv", 0, self.gv),
                  ("gr", 0, self.gv), ("small", FOX_HEADS, GLA_RANK), ("gates", 0, 2 * self.d)]
        shards = [[] for _ in range(N_CHIPS)]
        pos = 0
        for name, c0, width in atoms:
            for j in range(pos // per, (pos + width - 1) // per + 1):
                lo, hi = max(pos, j * per), min(pos + width, (j + 1) * per)
                shards[j].append(seg[name][:, c0 + lo - pos:c0 + hi - pos])
            pos += width
        assert pos == self.n_orig
        return jnp.stack([jnp.concatenate(s, axis=1) for s in shards])


def _layer_fwd(lay, h, p, t, hooks=None):
    hooks = hooks or {}
    d = lay.d
    xn, xn_t = _rms_fwd_t("rms_mix_fwd", h, p["norm_mix_g"], t, d)
    proj = _mm("mm_proj", xn, p["w_in"], mode="nn", m=t, n=lay.n_main, k=d, out_dtypes=(BF16,))
    small = _mm("mm_small", xn, p["w_in"], mode="nn", m=t, n=LANES, k=d, b_c0=lay.c_small)
    cs = _fox_gate_fwd(small, p["b_forget_p"], t)
    ct = cs[:, :FOX_HEADS].T
    c_col, c_row = ct[:, :, None], ct[:, None, :]
    o_fox, lse = _fox_fwd(proj, c_col, c_row, t, lay.fd, lay.c_fq, lay.c_fkv)
    b_alpha = p["b_alpha"] + hooks["mixers"](o_fox) if "mixers" in hooks else p["b_alpha"]
    glog = _gla_gate_fwd(small, p["w_alpha_p"], b_alpha, t, lay.gk)
    o_raw, s_prev = _gla_fwd(proj, glog, t, lay.dk, lay.dv, lay.c_gq, lay.c_gk, lay.c_gv)
    o_gla = _gla_post_fwd(o_raw, proj, p["gla_norm_g"], t, lay.dv, lay.c_gr)
    if "late" in hooks:
        p.update(hooks["late"](o_gla))
    a_fox = _mm("mm_o_fox", o_fox, p["w_o_fox"], mode="nn", m=t, n=d, k=lay.fw, b_shards=N_CHIPS)
    a_gla = _mm("mm_o_gla", o_gla, p["w_o_gla"], mode="nn", m=t, n=d, k=lay.gv)
    y = _merge_fwd(a_fox, a_gla, proj, lay.c_gates, t, d)
    h1 = _mm("mm_out", y, p["w_out"], mode="nn", m=t, n=d, k=d, extras=[h], epilogue=lambda acc, res: (res + acc,))
    xn2, xn2_t = _rms_fwd_t("rms_mlp_fwd", h1, p["norm_mlp_g"], t, d)
    u, act = _mm("mm_ff1", xn2, p["w_ff1"], mode="nn", m=t, n=4 * d, k=d, out_dtypes=(BF16, BF16), b_shards=N_CHIPS,
                 epilogue=lambda acc: (acc, jnp.square(jnp.maximum(acc, 0.0))))
    if "mlp" in hooks:
        hooks["mlp"](act)
    h2 = _mm("mm_ff2", act, p["w_ff2"], mode="nn", m=t, n=d, k=4 * d, extras=[h1],
             epilogue=lambda acc, res: (res + acc,))
    saved = dict(h=h, xn_t=xn_t, proj=proj, small=small, c_col=c_col, c_row=c_row, o_fox=o_fox, lse=lse, glog=glog,
                 o_raw=o_raw, s_prev=s_prev, o_gla=o_gla, a_fox=a_fox, a_gla=a_gla, y=y, h1=h1, xn2_t=xn2_t, u=u, act=act)
    return h2, saved


def _layer_bwd(lay, dh2, p, s, t, gates=None, first=False):
    d = lay.d
    g = {}

    def gated(gain, point):
        return gain + gates[point](g) if gates and point in gates else gain
    du = _mm("mm_dact", dh2, p["w_ff2"], mode="nt", m=t, n=4 * d, k=d, extras=[s["u"]], out_dtypes=(BF16,),
             epilogue=lambda acc, u: (acc * (2.0 * jnp.maximum(u.astype(F32), 0.0)),))
    g["w_ff2"] = _mm("mm_dw_ff2", s["act"], dh2, mode="tn", m=4 * d, n=d, k=t)
    g["w_ff1"] = _mm("mm_dw_ff1", s["xn2_t"], du, mode="nn", m=d, n=4 * d, k=t, tk=t, out_shards=N_CHIPS)
    dxn2 = _mm("mm_dxn2", du, p["w_ff1"], mode="nt", m=t, n=d, k=4 * d, b_shards=N_CHIPS)
    dh1, g["norm_mlp_g"] = _rms_bwd("rms_mlp_bwd", s["h1"], gated(p["norm_mlp_g"], "mlp"), dxn2, dh2, t, d)
    dy = _mm("mm_dy", dh1, p["w_out"], mode="nt", m=t, n=d, k=d)
    g["w_out"] = _mm("mm_dw_out", s["y"], dh1, mode="tn", m=d, n=d, k=t)
    da_fox, da_gla, dgates = _merge_bwd(dy, s["a_fox"], s["a_gla"], s["proj"], lay.c_gates, t, d)
    g["w_o_fox"] = _mm("mm_dw_o_fox", s["o_fox"], da_fox, mode="tn", m=lay.fw, n=d, k=t, out_shards=N_CHIPS)
    do_fox = _mm("mm_do_fox", da_fox, p["w_o_fox"], mode="nt", m=t, n=lay.fw, k=d, b_shards=N_CHIPS)
    g["w_o_gla"] = _mm("mm_dw_o_gla", s["o_gla"], da_gla, mode="tn", m=lay.gv, n=d, k=t)
    do_gla = _mm("mm_do_gla", da_gla, p["w_o_gla"], mode="nt", m=t, n=lay.gv, k=d)
    do_raw, dgr, g["gla_norm_g"] = _gla_post_bwd(s["o_raw"], s["proj"], gated(p["gla_norm_g"], "out"), do_gla, t,
                                                 lay.dv, lay.c_gr)
    dgq, dgk, dgv, dglog = _gla_bwd(s["proj"], s["glog"], s["s_prev"], do_raw, t, lay.dk, lay.dv,
                                    lay.c_gq, lay.c_gk, lay.c_gv)
    dz, g["b_alpha"] = _gla_gate_bwd(dglog, s["small"], p["w_alpha_p"], p["b_alpha"], t, lay.gk)
    g["w_alpha_p"] = _mm("mm_dw_alpha", s["small"], dz, mode="tn", m=LANES, n=lay.gk, k=t)
    dga = _mm("mm_dga", dz, p["w_alpha_p"], mode="nt", m=t, n=LANES, k=lay.gk)
    delta = _fox_delta(s["o_fox"], do_fox, t, lay.fd)
    dfq, dfkv, dc, dr = _fox_bwd(s["proj"], s["c_col"], s["c_row"], s["lse"], delta, do_fox, t, lay.fd,
                                 lay.c_fq, lay.c_fkv)
    dc_p = jnp.pad((dc[:, 0, :] + dr[:, :, 0]).T, ((0, 0), (0, LANES - FOX_HEADS)))
    dsmall, g["b_forget_p"] = _fox_gate_bwd(dc_p, s["small"], p["b_forget_p"], dga, t)
    segs = [("fq", dfq, lay.c_fq), ("gq", dgq, lay.c_gq), ("gv", dgv, lay.c_gv), ("gr", dgr, lay.c_gr),
            ("fkv", dfkv, lay.c_fkv), ("gates", dgates, lay.c_gates), ("gk", dgk, lay.c_gk),
            ("small", dsmall, lay.c_small)]
    dw_in = {nm: _mm("mm_dw_in_" + nm, s["xn_t"], dseg, mode="nn", m=d, n=dseg.shape[1], k=t, tk=t)
             for nm, dseg, _ in segs}
    g["w_in"] = lay.from_segments(dw_in)
    dxn = _mm("mm_dxn_gates", dgates, p["w_in"], mode="nt", m=t, n=d, k=2 * d, b_c0=lay.c_gates)
    dxn = _mm("mm_dxn_small", dsmall, p["w_in"], mode="nt", m=t, n=d, k=LANES, b_c0=lay.c_small, extras=[dxn],
              epilogue=lambda acc, prev: (prev + acc,))
    for nm, (a1, c1), (a2, c2) in (("q", (dfq, lay.c_fq), (dgq, lay.c_gq)), ("v", (dgv, lay.c_gv), (dgr, lay.c_gr)),
                                   ("k", (dfkv, lay.c_fkv), (dgk, lay.c_gk))):
        dxn = _mm_nt2("mm_dxn_" + nm, a1, c1, a2, c2, p["w_in"], dxn, m=t, n=d)
    if first:
        dx, head, g["norm_mix_g"] = _rms_bwd_x("rms_mix_bwd_x", s["h"], gated(p["norm_mix_g"], "in"), dxn, dh1, t, d)
        return (dx, head), g
    dh, g["norm_mix_g"] = _rms_bwd("rms_mix_bwd", s["h"], gated(p["norm_mix_g"], "in"), dxn, dh1, t, d)
    return dh, g


def _sequence_step(x, target, meta, layers, final_g):
    seq, d = x.shape
    t = seq + ROW0
    lay = _Layout(d)
    h = jnp.pad(x, ((ROW0, 0), (0, 0))).at[PAD:ROW0].set(meta)
    target_p = jnp.pad(target, ((ROW0, 0), (0, 0)))
    saved = []
    for p in layers:
        h, s = _layer_fwd(lay, h, p, t)
        saved.append(s)
    dh, dg_final, loss_part = _loss_head(h, final_g, target_p, t, d)
    grads = [None] * len(layers)
    for l in reversed(range(len(layers))):
        dh, grads[l] = _layer_bwd(lay, dh, layers[l], saved[l], t)
    return loss_part, dh[ROW0:], dh[PAD:ROW0], grads, dg_final


_SMALL_ROWS = 48


def _pack_small(d, meta, mix, gla, mlp, final, b_alpha, b_forget, w_alpha2):
    rows = [meta.reshape(N_META, d), mix.reshape(DEPTH, d), gla.reshape(DEPTH, d), mlp.reshape(DEPTH, d),
            final.reshape(1, d), b_alpha.reshape(1, d),
            jnp.pad(b_forget.reshape(1, DEPTH * FOX_HEADS), ((0, 0), (0, d - DEPTH * FOX_HEADS))),
            jnp.zeros((7, d), F32), w_alpha2.reshape(GLA_RANK, d)]
    return jnp.concatenate(rows, axis=0)


def _unpack_small(d, packed):
    return dict(meta=packed[:N_META], norm_mix_g=packed[16:18], gla_norm_g=packed[18:20], norm_mlp_g=packed[20:22],
                final_norm_g=packed[22], b_alpha=packed[23].reshape(DEPTH, d // 2),
                b_forget=packed[24, :DEPTH * FOX_HEADS].reshape(DEPTH, FOX_HEADS),
                w_alpha2=packed[32:48].reshape(DEPTH, GLA_RANK, d // 2))


_BIG = ("w_in", "w_o_fox", "w_o_gla", "w_out", "w_ff1", "w_ff2")
_COL_SHARDED = ("w_in", "w_o_fox", "w_ff1")


def _full_matrix(name, gathered_l):
    nch, r, c = gathered_l.shape
    if name in _COL_SHARDED:
        return gathered_l.transpose(1, 0, 2).reshape(r, nch * c)
    return gathered_l.reshape(nch * r, c)


def _shard_major(name, full):
    r, c = full.shape
    if name in _COL_SHARDED:
        return full.reshape(r, N_CHIPS, c // N_CHIPS).transpose(1, 0, 2)
    return full.reshape(N_CHIPS, r // N_CHIPS, c)


def kernel(x, meta_tokens, norm_mix_g, w_in, b_forget, w_alpha2, b_alpha, gla_norm_g, w_o_fox, w_o_gla, w_out, norm_mlp_g, w_ff1, w_ff2, final_norm_g, loss_target, m_meta_tokens, m_norm_mix_g, m_w_in, m_b_forget, m_w_alpha2, m_b_alpha, m_gla_norm_g, m_w_o_fox, m_w_o_gla, m_w_out, m_norm_mlp_g, m_w_ff1, m_w_ff2, m_final_norm_g, v_meta_tokens, v_norm_mix_g, v_w_in, v_b_forget, v_w_alpha2, v_b_alpha, v_gla_norm_g, v_w_o_fox, v_w_o_gla, v_w_out, v_norm_mlp_g, v_w_ff1, v_w_ff2, v_final_norm_g):
    d = x.shape[2]
    lay = _Layout(d)
    xi, yi, ci = lax.axis_index("x"), lax.axis_index("y"), lax.axis_index("c")
    chip = (2 * xi + yi).astype(jnp.int32)
    w = dict(w_in=w_in, w_alpha2=w_alpha2, w_o_fox=w_o_fox, w_o_gla=w_o_gla, w_out=w_out, w_ff1=w_ff1, w_ff2=w_ff2)
    m = dict(w_in=m_w_in, w_alpha2=m_w_alpha2, w_o_fox=m_w_o_fox, w_o_gla=m_w_o_gla, w_out=m_w_out, w_ff1=m_w_ff1,
             w_ff2=m_w_ff2)
    v = dict(w_in=v_w_in, w_alpha2=v_w_alpha2, w_o_fox=v_w_o_fox, w_o_gla=v_w_o_gla, w_out=v_w_out, w_ff1=v_w_ff1,
             w_ff2=v_w_ff2)

    seq = x.shape[1]
    t = seq + ROW0
    core_idx = ci.astype(jnp.int32)[None]
    chip_idx = chip[None]

    cols = d // N_CHIPS
    small_w = jnp.concatenate([meta_tokens, w_alpha2.reshape(-1, cols)], axis=0)
    small_raw = _all_gather_small(small_w)
    small_all = small_raw[0::2]
    alpha_full = small_all[:, N_META:].reshape(N_CHIPS, DEPTH, GLA_RANK, lay.gk // N_CHIPS)
    alpha_full = alpha_full.transpose(1, 2, 0, 3).reshape(DEPTH, GLA_RANK, lay.gk)
    groups = [(0, ("w_in",)), (0, _BIG[1:]), (1, _BIG)]
    started, after = [], small_raw
    for gi, (l, names) in enumerate(groups):
        own16 = [w[n][l].astype(BF16) for n in names]
        lands = [lax.empty((N_CHIPS,) + o.shape, BF16) for o in own16]
        started.append(_split_start("gather_start_%d" % gi, own16, lands, _gather_plan, after=after))
        after = started[gi][4]
    meta_full = small_all[:, :N_META].transpose(1, 0, 2).reshape(N_META, d) + after[0, 0]

    passing = {}

    def arrive(gi, after):
        send_sems, recv_sems, srcs, lands, _ = started[gi]
        srcs, lands = _split_wait("gather_wait_%d" % gi, send_sems, recv_sems, srcs, lands, _gather_plan, after)
        passing[gi] = (srcs, _exchange_start("gather_pass_%d" % gi, lands, _forward_plan, after=srcs[0]))
        return passing[gi][1][3][0, 0]

    def gathered(gi, after):
        if gi not in passing:
            arrive(gi, after)
        srcs, (send_sems, recv_sems, lands, _) = passing[gi]
        lands = _exchange_wait("gather_pass_wait_%d" % gi, send_sems, recv_sems, lands, _forward_plan, after)
        return {n: lax.dynamic_update_slice(g, o[None], (chip, 0, 0)) for n, g, o in zip(groups[gi][1], lands, srcs)}

    def early_weights(l, gl):
        w_alpha_p = jnp.zeros((LANES, lay.gk), BF16).at[FOX_HEADS:FOX_HEADS + GLA_RANK].set(
            alpha_full[l].astype(BF16))
        return dict(
            w_in=lay.to_p(gl["w_in"]), w_alpha_p=w_alpha_p,
            norm_mix_g=norm_mix_g[l][None], norm_mlp_g=norm_mlp_g[l][None], gla_norm_g=gla_norm_g[l][None],
            b_alpha=b_alpha[l][None],
            b_forget_p=jnp.pad(b_forget[l][None], ((0, 0), (0, LANES - FOX_HEADS))))

    def late_weights(gl):
        return dict(w_o_fox=gl["w_o_fox"], w_o_gla=_full_matrix("w_o_gla", gl["w_o_gla"]),
                    w_out=_full_matrix("w_out", gl["w_out"]), w_ff1=gl["w_ff1"],
                    w_ff2=_full_matrix("w_ff2", gl["w_ff2"]))

    h = jnp.pad(x[0], ((ROW0, 0), (0, 0))).at[PAD:ROW0].set(meta_full)
    layers, saved = [], []
    layers.append(early_weights(0, gathered(0, after=h)))
    h, s = _layer_fwd(lay, h, layers[0], t, hooks=dict(
        mixers=lambda after: arrive(1, after), late=lambda after: late_weights(gathered(1, after)),
        mlp=lambda after: arrive(2, after)))
    saved.append(s)
    gl = gathered(2, after=h)
    layers.append({**early_weights(1, gl), **late_weights(gl)})
    h, s = _layer_fwd(lay, h, layers[1], t)
    saved.append(s)
    dh, dg_final, loss_part = _loss_head(h, final_norm_g[None], jnp.pad(loss_target[0], ((ROW0, 0), (0, 0))), t, d)
    loss = lax.psum(loss_part[0, 0], ("x", "y", "c"))

    def partial_of(g, n):
        return g[n] if n in ("w_ff1", "w_o_fox", "w_in") else _shard_major(n, g[n])

    scatter_groups = dict(mlp=("w_ff1", "w_ff2"), out=("w_o_fox", "w_o_gla", "w_out"))
    scatter_groups["in"] = ("w_in",)
    swapping, scattered = [], {}

    def start_swap(l, grp, g, after=None):
        parts = [partial_of(g, n) for n in scatter_groups[grp]]
        lands = [lax.empty((p_.shape[0], p_.shape[1] // 2, p_.shape[2]), F32) for p_ in parts]
        started_swap = _split_start("swap_start_%d_%s" % (l, grp), parts, lands, _swap_plan, after=after)
        swapping.append((l, grp, started_swap))
        return started_swap[4]

    def start_scatter(after):
        l, grp, (send_sems, recv_sems, srcs, lands, _) = swapping.pop(0)
        names = scatter_groups[grp]
        tag = "%d_%s" % (l, grp)
        parts, from_sibling = _split_wait("swap_wait_" + tag, send_sems, recv_sems, srcs, lands, _swap_plan, after)
        sums = [_pair_sum("pair_sum_%d_%s" % (l, n), p_, r_, chip_idx, core_idx)
                for n, p_, r_ in zip(names, parts, from_sibling)]
        lands = [lax.empty(s16.shape, BF16) for s16 in sums]
        send_sems, recv_sems, srcs, lands, token = _split_start("scatter_start_" + tag, sums, lands, _scatter_plan)
        scattered[l, grp] = (send_sems, recv_sems, srcs, lands, parts, from_sibling)
        return token

    def gate(l, grp, g):
        token = start_swap(l, grp, g)
        if len(swapping) > 1:
            token = start_scatter(after=token)
        return token[0, 0]

    def gates_for(l, points):
        return {grp: functools.partial(gate, l, grp) for grp in points}

    grads = [None] * DEPTH
    dh, grads[1] = _layer_bwd(lay, dh, layers[1], saved[1], t, gates=gates_for(1, ("mlp", "out", "in")))
    (grad_x, head), grads[0] = _layer_bwd(lay, dh, layers[0], saved[0], t, gates=gates_for(0, ("mlp", "out")),
                                          first=True)
    d_meta = head[PAD:ROW0]

    stack = lambda key: jnp.concatenate([grads[l][key] for l in range(DEPTH)], axis=0)
    b_forget_g = jnp.concatenate([grads[l]["b_forget_p"][:, :FOX_HEADS] for l in range(DEPTH)], axis=0)
    alpha_g = jnp.stack([grads[l]["w_alpha_p"][FOX_HEADS:FOX_HEADS + GLA_RANK] for l in range(DEPTH)])
    packed = _pack_small(d, d_meta, stack("norm_mix_g"), stack("gla_norm_g"), stack("norm_mlp_g"), dg_final,
                         stack("b_alpha"), b_forget_g, alpha_g)
    small_g = _unpack_small(d, _sum_devices(_all_gather_small(packed)))
    small_g["meta"] = lax.dynamic_slice_in_dim(small_g["meta"], chip * (d // N_CHIPS), d // N_CHIPS, axis=1)
    alpha_shard = lax.dynamic_slice_in_dim(small_g["w_alpha2"], chip * (lay.gk // N_CHIPS), lay.gk // N_CHIPS, axis=2)

    after = start_swap(0, "in", grads[0], after=small_g["final_norm_g"])
    while swapping:
        after = start_scatter(after)
    outs = {n: None for n in _BIG}

    def reduce_and_share(l, grps, after):
        tag = "%d_%s" % (l, grps[0])
        names, sums = [], []
        for grp in grps:
            send_sems, recv_sems, srcs, lands, parts, from_sibling = scattered[l, grp]
            _, lands = _split_wait("scatter_wait_%d_%s" % (l, grp), send_sems, recv_sems, srcs, lands, _scatter_plan,
                                 after)
            names += scatter_groups[grp]
            sums += [_chip_sum("chip_sum_%d_%s" % (l, n), p_, r_, landed, chip_idx, core_idx)
                     for n, p_, r_, landed in zip(scatter_groups[grp], parts, from_sibling, lands)]
        return l, tag, names, _exchange_start("share_start_" + tag, sums, _share_plan, after=sums[-1])

    def update(shared, after):
        l, tag, names, (send_sems, recv_sems, sums, _) = shared
        sums = _exchange_wait("share_wait_" + tag, send_sems, recv_sems, sums, _share_plan, after)
        for n, g in zip(names, sums):
            outs[n] = _adamw("adamw_%d_%s" % (l, n), w[n], g, m[n], v[n], layer=l, into=outs[n])
        return outs[names[-1]][0]

    shared_1 = reduce_and_share(1, ("mlp", "out", "in"), after)
    shared_0 = reduce_and_share(0, ("mlp", "out"), shared_1[3][3])
    after = update(shared_1, shared_0[3][3])
    shared_in = reduce_and_share(0, ("in",), after)
    after = update(shared_0, shared_in[3][3])
    update(shared_in, after)
    out_g, out_d, out_m, out_v = {}, {}, {}, {}
    for n in _BIG:
        out_g[n], out_d[n], out_m[n], out_v[n] = outs[n]
    out_g["w_alpha2"], out_d["w_alpha2"], out_m["w_alpha2"], out_v["w_alpha2"] = _adamw(
        "adamw_w_alpha2", w["w_alpha2"], alpha_shard, m["w_alpha2"], v["w_alpha2"])
    sm_w = dict(meta_tokens=meta_tokens, norm_mix_g=norm_mix_g, b_forget=b_forget, b_alpha=b_alpha,
                gla_norm_g=gla_norm_g, norm_mlp_g=norm_mlp_g, final_norm_g=final_norm_g)
    sm_m = dict(meta_tokens=m_meta_tokens, norm_mix_g=m_norm_mix_g, b_forget=m_b_forget, b_alpha=m_b_alpha,
                gla_norm_g=m_gla_norm_g, norm_mlp_g=m_norm_mlp_g, final_norm_g=m_final_norm_g)
    sm_v = dict(meta_tokens=v_meta_tokens, norm_mix_g=v_norm_mix_g, b_forget=v_b_forget, b_alpha=v_b_alpha,
                gla_norm_g=v_gla_norm_g, norm_mlp_g=v_norm_mlp_g, final_norm_g=v_final_norm_g)
    sm_g = dict(meta_tokens=small_g["meta"], norm_mix_g=small_g["norm_mix_g"], b_forget=small_g["b_forget"],
                b_alpha=small_g["b_alpha"], gla_norm_g=small_g["gla_norm_g"], norm_mlp_g=small_g["norm_mlp_g"],
                final_norm_g=small_g["final_norm_g"])
    names_small = list(sm_w)
    sizes = [sm_w[n].size for n in names_small]
    width = 512
    total = -(-sum(sizes) // (8 * width)) * (8 * width)

    def pack_flat(dct, fill):
        flat = jnp.concatenate([dct[n].reshape(-1) for n in names_small])
        return jnp.pad(flat, (0, total - flat.shape[0]), constant_values=fill).reshape(1, -1, width)

    res = _adamw("adamw_small", pack_flat(sm_w, 0.0), pack_flat(sm_g, 0.0), pack_flat(sm_m, 0.0), pack_flat(sm_v, 1.0))
    offs = [0]
    for sz in sizes:
        offs.append(offs[-1] + sz)
    for i, n in enumerate(names_small):
        out_g[n] = sm_g[n].reshape(sm_w[n].shape)
        out_d[n], out_m[n], out_v[n] = [r.reshape(-1)[offs[i]:offs[i + 1]].reshape(sm_w[n].shape) for r in res[1:]]

    order = ["meta_tokens", "norm_mix_g", "w_in", "b_forget", "w_alpha2", "b_alpha", "gla_norm_g", "w_o_fox",
             "w_o_gla", "w_out", "norm_mlp_g", "w_ff1", "w_ff2", "final_norm_g"]
    return (loss, grad_x[None], *[out_g[n] for n in order], *[out_d[n] for n in order],
            *[out_m[n] for n in order], *[out_v[n] for n in order])
```

```python
import functools

import numpy as np

import jax
import jax.numpy as jnp
from jax import lax
from jax.experimental import pallas as pl
from jax.experimental.pallas import tpu as pltpu

F32 = jnp.float32
BF16 = jnp.bfloat16

N_META = 16
PAD = 112
ROW0 = PAD + N_META
EPS = 1e-6
MASK_VALUE = -1e30
FOX_HEADS = 8
FOX_GROUP = 2
GLA_HEADS = 4
GLA_RANK = 16
GLA_TAU = 16.0
GLA_CHUNK = 64
DEPTH = 2
N_CHIPS = 4
N_DEV = 8

ADAM_LR = 0.001
ADAM_B1 = 0.9
ADAM_B2 = 0.999
ADAM_EPS = 1e-08
ADAM_WD = 0.01
ADAM_STEP = 10

LANES = 128
VMEM_LIMIT = 56 * 1024 * 1024
MESH = pl.DeviceIdType.MESH


def _pick(n, target, mult):
    best = None
    for d in range(mult, min(n, target) + 1, mult):
        if n % d == 0:
            best = d
    return n if best is None else best


def _params(sem=None):
    return pltpu.CompilerParams(dimension_semantics=sem, vmem_limit_bytes=VMEM_LIMIT)


def _bf(v):
    return v if v.dtype == BF16 else v.astype(BF16)


def _sigmoid(z):
    return 1.0 / (1.0 + jnp.exp(-z))


def _log_sigmoid(z):
    return jnp.minimum(z, 0.0) - jnp.log(1.0 + jnp.exp(-jnp.abs(z)))


def _split3(v):
    a = v.astype(BF16)
    r = v - a.astype(F32)
    b = r.astype(BF16)
    c = (r - b.astype(F32)).astype(BF16)
    return a, b, c


def _dot(a, b, dims):
    return lax.dot_general(a, b, (dims, ((), ())), preferred_element_type=F32)


NN = ((1,), (0,))
NT = ((1,), (1,))
TN = ((0,), (0,))


def _tri_dot(tri, v, dims=NN):
    a, b, c = _split3(v)
    return _dot(tri, a, dims) + _dot(tri, b, dims) + _dot(tri, c, dims)


def _mm(name, a, b, *, mode, m, n, k, b_c0=0, extras=(), epilogue=None, out_dtypes=(F32,),
        b_shards=1, out_shards=1, tm=1056, tn=1024, tk=2048):
    tm = _pick(m, tm, LANES if mode == "tn" else 16)
    tn = _pick(n // max(b_shards if mode == "nn" else 1, out_shards), tn, LANES)
    if mode == "tn":
        tk = _pick(k, 2112, 16)
    else:
        tk = _pick(k // (b_shards if mode == "nt" else 1), tk, LANES)
    assert b_c0 % (tk if mode == "nt" else tn) == 0 and (b_shards == 1 or b_c0 == 0)
    nk = k // tk
    if mode == "tn":
        a_spec = pl.BlockSpec((tk, tm), lambda i, j, kk: (kk, i))
    else:
        a_spec = pl.BlockSpec((tm, tk), lambda i, j, kk: (i, kk))
    if mode == "nt":
        dims = NT
        if b_shards > 1:
            per = (k // b_shards) // tk
            b_spec = pl.BlockSpec((None, tn, tk), lambda i, j, kk: (kk // per, j, kk % per))
        else:
            b_spec = pl.BlockSpec((tn, tk), lambda i, j, kk: (j, kk + b_c0 // tk))
    else:
        dims = NN if mode == "nn" else TN
        if b_shards > 1:
            per = (n // b_shards) // tn
            b_spec = pl.BlockSpec((None, tk, tn), lambda i, j, kk: (j // per, kk, j % per))
        else:
            b_spec = pl.BlockSpec((tk, tn), lambda i, j, kk: (kk, j + b_c0 // tn))
    ex_specs = [pl.BlockSpec((tm, tn), lambda i, j, kk: (i, j)) for _ in extras]
    if out_shards > 1:
        oper = (n // out_shards) // tn
        out_specs = [pl.BlockSpec((None, tm, tn), lambda i, j, kk: (j // oper, i, j % oper)) for _ in out_dtypes]
        out_shape = [jax.ShapeDtypeStruct((out_shards, m, n // out_shards), dt) for dt in out_dtypes]
    else:
        out_specs = [pl.BlockSpec((tm, tn), lambda i, j, kk: (i, j)) for _ in out_dtypes]
        out_shape = [jax.ShapeDtypeStruct((m, n), dt) for dt in out_dtypes]
    n_ex = len(extras)
    n_out = len(out_dtypes)

    def finish(acc, ex_refs, out_refs):
        vals = (acc,) if epilogue is None else epilogue(acc, *[r[...] for r in ex_refs])
        for r, v in zip(out_refs, vals):
            r[...] = v.astype(r.dtype)

    def body(a_ref, b_ref, *rest):
        ex_refs = rest[:n_ex]
        out_refs = rest[n_ex:n_ex + n_out]
        prod = _dot(_bf(a_ref[...]), _bf(b_ref[...]), dims)
        if nk == 1:
            finish(prod, ex_refs, out_refs)
            return
        acc_ref = rest[n_ex + n_out]
        kk = pl.program_id(2)

        @pl.when(kk == 0)
        def _():
            acc_ref[...] = prod

        @pl.when((kk > 0) & (kk < nk - 1))
        def _():
            acc_ref[...] += prod

        @pl.when(kk == nk - 1)
        def _():
            finish(acc_ref[...] + prod, ex_refs, out_refs)

    outs = pl.pallas_call(
        body,
        name=name,
        grid=(m // tm, n // tn, nk),
        in_specs=[a_spec, b_spec] + ex_specs,
        out_specs=out_specs,
        out_shape=out_shape,
        scratch_shapes=[pltpu.VMEM((tm, tn), F32)] if nk > 1 else [],
        compiler_params=_params(("parallel", "parallel", "arbitrary")),
    )(a, b, *extras)
    return outs[0] if n_out == 1 else outs


def _mm_nt2(name, a1, c1, a2, c2, b, prev, *, m, n, tm=528, tn=1024):
    k1, k2 = a1.shape[1], a2.shape[1]
    assert c1 % k1 == 0 and c2 % k2 == 0
    tm = _pick(m, tm, 16)
    tn = _pick(n, tn, LANES)

    def body(a1_ref, a2_ref, b1_ref, b2_ref, *rest):
        acc = _dot(a1_ref[...], b1_ref[...], NT) + _dot(a2_ref[...], b2_ref[...], NT)
        if prev is not None:
            acc = rest[0][...] + acc
        rest[-1][...] = acc

    tile = pl.BlockSpec((tm, tn), lambda i, j: (i, j))
    in_specs = [pl.BlockSpec((tm, k1), lambda i, j: (i, 0)), pl.BlockSpec((tm, k2), lambda i, j: (i, 0)),
                pl.BlockSpec((tn, k1), lambda i, j: (j, c1 // k1)), pl.BlockSpec((tn, k2), lambda i, j: (j, c2 // k2))]
    args = [a1, a2, b, b]
    if prev is not None:
        in_specs.append(tile)
        args.append(prev)
    return pl.pallas_call(
        body, name=name, grid=(m // tm, n // tn), in_specs=in_specs, out_specs=tile,
        out_shape=jax.ShapeDtypeStruct((m, n), F32),
        compiler_params=_params(("parallel", "parallel")),
    )(*args)


def _ew(name, fn, ins, outs, rows, tm):
    tm = _pick(rows, tm, 16)
    in_specs, args = [], []
    for spec in ins:
        if spec[0] == "tile":
            _, arr, width, c0 = spec
            assert c0 % width == 0
            in_specs.append(pl.BlockSpec((tm, width), functools.partial(lambda i, o: (i, o), o=c0 // width)))
        else:
            arr = spec[1]
            in_specs.append(pl.BlockSpec(arr.shape, lambda i: (0, 0)))
        args.append(arr)
    out_specs, out_shape = [], []
    for kind, dt, width in outs:
        if kind == "tile":
            out_specs.append(pl.BlockSpec((tm, width), lambda i: (i, 0)))
            out_shape.append(jax.ShapeDtypeStruct((rows, width), dt))
        else:
            out_specs.append(pl.BlockSpec((1, width), lambda i: (0, 0)))
            out_shape.append(jax.ShapeDtypeStruct((1, width), dt))
    n_in = len(ins)
    has_acc = any(o[0] == "acc" for o in outs)

    def body(*refs):
        i = pl.program_id(0)
        vals = fn(i * tm, *[r[...] for r in refs[:n_in]])
        for (kind, _, _), r, v in zip(outs, refs[n_in:], vals):
            if kind == "tile":
                r[...] = v.astype(r.dtype)
            else:
                @pl.when(i == 0)
                def _():
                    r[...] = jnp.zeros_like(r)

                r[...] += v.astype(r.dtype)

    res = pl.pallas_call(
        body,
        name=name,
        grid=(rows // tm,),
        in_specs=in_specs,
        out_specs=out_specs,
        out_shape=out_shape,
        compiler_params=_params(("arbitrary",) if has_acc else ("parallel",)),
    )(*args)
    return res[0] if len(outs) == 1 else res


def _row_ids(row0, tm):
    return row0 + lax.broadcasted_iota(jnp.int32, (tm, 1), 0)


def _colsum(v):
    return jnp.sum(v, axis=0, keepdims=True)


def _rms_fwd_t(name, h, g, t, d):
    tm = _pick(t, 384, LANES)

    def body(x_ref, g_ref, y_ref, yt_ref):
        x = x_ref[...]
        y = x * lax.rsqrt(jnp.mean(x * x, axis=-1, keepdims=True) + EPS) * g_ref[...]
        y_ref[...] = y.astype(BF16)
        yt_ref[...] = y.T.astype(BF16)

    return pl.pallas_call(
        body, name=name, grid=(t // tm,),
        in_specs=[pl.BlockSpec((tm, d), lambda i: (i, 0)), pl.BlockSpec((1, d), lambda i: (0, 0))],
        out_specs=[pl.BlockSpec((tm, d), lambda i: (i, 0)), pl.BlockSpec((d, tm), lambda i: (0, i))],
        out_shape=[jax.ShapeDtypeStruct((t, d), BF16), jax.ShapeDtypeStruct((d, t), BF16)],
        compiler_params=_params(("parallel",)),
    )(h, g)


def _rms_bwd(name, h, g, dy, dres, t, d):
    def fn(row0, x, gg, dyv, dr):
        r = lax.rsqrt(jnp.mean(x * x, axis=-1, keepdims=True) + EPS)
        xh = x * r
        dxh = dyv * gg
        dx = r * (dxh - xh * jnp.mean(dxh * xh, axis=-1, keepdims=True))
        out = jnp.where(_row_ids(row0, x.shape[0]) >= PAD, dr + dx, 0.0)
        return out, _colsum(dyv * xh)

    return _ew(name, fn, [("tile", h, d, 0), ("full", g), ("tile", dy, d, 0), ("tile", dres, d, 0)],
               [("tile", F32, d), ("acc", F32, d)], t, 264)


def _rms_bwd_x(name, h, g, dy, dres, t, d):
    tm = ROW0

    def body(x_ref, g_ref, dy_ref, dr_ref, dx_ref, head_ref, dg_ref):
        i = pl.program_id(0)
        x = x_ref[...]
        r = lax.rsqrt(jnp.mean(x * x, axis=-1, keepdims=True) + EPS)
        xh = x * r
        dxh = dy_ref[...] * g_ref[...]
        dx = r * (dxh - xh * jnp.mean(dxh * xh, axis=-1, keepdims=True))
        out = jnp.where(_row_ids(i * tm, tm) >= PAD, dr_ref[...] + dx, 0.0)

        @pl.when(i == 0)
        def _():
            head_ref[...] = out
            dg_ref[...] = jnp.zeros_like(dg_ref)

        dx_ref[...] = out
        dg_ref[...] += _colsum(dy_ref[...] * xh)

    tile = pl.BlockSpec((tm, d), lambda i: (i, 0))
    fixed = lambda shape: pl.BlockSpec(shape, lambda i: (0, 0))
    return pl.pallas_call(
        body, name=name, grid=(t // tm,),
        in_specs=[tile, fixed((1, d)), tile, tile],
        out_specs=[pl.BlockSpec((tm, d), lambda i: (jnp.maximum(i - 1, 0), 0)), fixed((tm, d)), fixed((1, d))],
        out_shape=[jax.ShapeDtypeStruct((t - ROW0, d), F32), jax.ShapeDtypeStruct((ROW0, d), F32),
                   jax.ShapeDtypeStruct((1, d), F32)],
        compiler_params=_params(("arbitrary",)),
    )(h, g, dy, dres)


def _loss_head(h, g, target_p, t, d):
    def fn(row0, x, gg, tgt):
        real = _row_ids(row0, x.shape[0]) >= ROW0
        r = lax.rsqrt(jnp.mean(x * x, axis=-1, keepdims=True) + EPS)
        xh = x * r
        err = jnp.where(real, xh * gg - tgt, 0.0)
        loss_rows = 0.5 * jnp.mean(err * err, axis=-1, keepdims=True)
        dyv = err * (1.0 / d)
        dxh = dyv * gg
        dx = r * (dxh - xh * jnp.mean(dxh * xh, axis=-1, keepdims=True))
        loss_part = jnp.sum(loss_rows, axis=0, keepdims=True) * jnp.ones((1, LANES), F32)
        return jnp.where(real, dx, 0.0), _colsum(dyv * xh), loss_part

    return _ew("loss_head", fn, [("tile", h, d, 0), ("full", g), ("tile", target_p, d, 0)],
               [("tile", F32, d), ("acc", F32, d), ("acc", F32, LANES)], t, 264)


def _merge_fwd(a_fox, a_gla, proj, c_gates, t, d):
    def fn(row0, af, ag, gates):
        gates = gates.astype(F32)
        return (_sigmoid(gates[:, :d]) * af + _sigmoid(gates[:, d:]) * ag,)

    return _ew("merge_fwd", fn, [("tile", a_fox, d, 0), ("tile", a_gla, d, 0), ("tile", proj, 2 * d, c_gates)],
               [("tile", BF16, d)], t, 264)


def _merge_bwd(dy, a_fox, a_gla, proj, c_gates, t, d):
    def fn(row0, dyv, af, ag, gates):
        gates = gates.astype(F32)
        sf = _sigmoid(gates[:, :d])
        sg = _sigmoid(gates[:, d:])
        dgates = jnp.concatenate([dyv * af * sf * (1.0 - sf), dyv * ag * sg * (1.0 - sg)], axis=1)
        return dyv * sf, dyv * sg, dgates

    return _ew("merge_bwd", fn,
               [("tile", dy, d, 0), ("tile", a_fox, d, 0), ("tile", a_gla, d, 0), ("tile", proj, 2 * d, c_gates)],
               [("tile", BF16, d), ("tile", BF16, d), ("tile", BF16, 2 * d)], t, 264)


def _fox_gate_fwd(small, b_forget_p, t):
    tb = _pick(t, 384, LANES)

    def body(s_ref, b_ref, c_ref, carry_ref):
        i = pl.program_id(0)

        @pl.when(i == 0)
        def _():
            carry_ref[...] = jnp.zeros_like(carry_ref)

        logf = _log_sigmoid(s_ref[...] + b_ref[...])
        logf = jnp.where(_row_ids(i * tb, tb) >= PAD, logf, 0.0)
        r = lax.broadcasted_iota(jnp.int32, (tb, tb), 0)
        c = lax.broadcasted_iota(jnp.int32, (tb, tb), 1)
        tri = (c <= r).astype(BF16)
        cs = _tri_dot(tri, logf) + carry_ref[...]
        c_ref[...] = cs
        carry_ref[...] = cs[tb - 1:tb, :]

    return pl.pallas_call(
        body, name="fox_gate_fwd", grid=(t // tb,),
        in_specs=[pl.BlockSpec((tb, LANES), lambda i: (i, 0)), pl.BlockSpec((1, LANES), lambda i: (0, 0))],
        out_specs=pl.BlockSpec((tb, LANES), lambda i: (i, 0)),
        out_shape=jax.ShapeDtypeStruct((t, LANES), F32),
        scratch_shapes=[pltpu.VMEM((1, LANES), F32)],
        compiler_params=_params(("arbitrary",)),
    )(small, b_forget_p)


def _fox_gate_bwd(dc, small, b_forget_p, dga, t):
    tb = _pick(t, 384, LANES)
    nb = t // tb

    def body(dc_ref, s_ref, b_ref, dga_ref, ds_ref, db_ref, carry_ref):
        i = pl.program_id(0)

        @pl.when(i == 0)
        def _():
            carry_ref[...] = jnp.zeros_like(carry_ref)
            db_ref[...] = jnp.zeros_like(db_ref)

        r = lax.broadcasted_iota(jnp.int32, (tb, tb), 0)
        c = lax.broadcasted_iota(jnp.int32, (tb, tb), 1)
        tri = (c >= r).astype(BF16)
        dlogf = _tri_dot(tri, dc_ref[...]) + carry_ref[...]
        carry_ref[...] = dlogf[0:1, :]
        z = s_ref[...] + b_ref[...]
        dff = dlogf * _sigmoid(-z)
        lane = lax.broadcasted_iota(jnp.int32, (tb, LANES), 1)
        keep = (_row_ids((nb - 1 - i) * tb, tb) >= PAD) & (lane < FOX_HEADS)
        dff = jnp.where(keep, dff, 0.0)
        ds_ref[...] = dff + dga_ref[...]
        db_ref[...] += _colsum(dff)

    rev = lambda i: (nb - 1 - i, 0)
    return pl.pallas_call(
        body, name="fox_gate_bwd", grid=(nb,),
        in_specs=[pl.BlockSpec((tb, LANES), rev), pl.BlockSpec((tb, LANES), rev),
                  pl.BlockSpec((1, LANES), lambda i: (0, 0)), pl.BlockSpec((tb, LANES), rev)],
        out_specs=[pl.BlockSpec((tb, LANES), rev), pl.BlockSpec((1, LANES), lambda i: (0, 0))],
        out_shape=[jax.ShapeDtypeStruct((t, LANES), F32), jax.ShapeDtypeStruct((1, LANES), F32)],
        scratch_shapes=[pltpu.VMEM((1, LANES), F32)],
        compiler_params=_params(("arbitrary",)),
    )(dc, small, b_forget_p, dga)


def _fox_pairs(nb, by_key):
    if by_key:
        pairs = [(qi, ki) for ki in range(nb) for qi in range(ki, nb)]
    else:
        pairs = [(qi, ki) for qi in range(nb) for ki in range(qi + 1)]
    return (jnp.asarray(np.array([p[0] for p in pairs], np.int32)),
            jnp.asarray(np.array([p[1] for p in pairs], np.int32)), len(pairs))


def _fox_specs(tb, fd, c_fq, c_fkv):
    gw = FOX_GROUP * fd
    q0, kv0 = c_fq // gw, c_fkv // (2 * gw)
    return dict(
        q=pl.BlockSpec((tb, gw), lambda g, p, qt, kt: (qt[p], q0 + g)),
        kv=pl.BlockSpec((tb, 2 * gw), lambda g, p, qt, kt: (kt[p], kv0 + g)),
        col=pl.BlockSpec((FOX_GROUP, tb, 1), lambda g, p, qt, kt: (g, qt[p], 0)),
        row=pl.BlockSpec((FOX_GROUP, 1, tb), lambda g, p, qt, kt: (g, 0, kt[p])),
        head=pl.BlockSpec((tb, gw), lambda g, p, qt, kt: (qt[p], g)),
        key_kv=pl.BlockSpec((tb, 2 * gw), lambda g, p, qt, kt: (kt[p], g)),
    )


def _fox_mask(qi, ki, tb):
    row = qi * tb + lax.broadcasted_iota(jnp.int32, (tb, tb), 0)
    col = ki * tb + lax.broadcasted_iota(jnp.int32, (tb, tb), 1)
    return (col <= row) & (col >= PAD)


def _fox_heads(q_ref, kv_ref, fd):
    return [(q_ref[:, hh * fd:(hh + 1) * fd], kv_ref[:, 2 * hh * fd:(2 * hh + 1) * fd],
             kv_ref[:, (2 * hh + 1) * fd:(2 * hh + 2) * fd]) for hh in range(FOX_GROUP)]


def _fox_fwd(proj, c_col, c_row, t, fd, c_fq, c_fkv):
    tb = _pick(t, 384, LANES)
    nb = t // tb
    scale = fd ** -0.5
    sp = _fox_specs(tb, fd, c_fq, c_fkv)
    qt, kt, npairs = _fox_pairs(nb, by_key=False)

    def body(qt_ref, kt_ref, q_ref, kv_ref, cq_ref, ck_ref, o_ref, lse_ref, m_ref, l_ref, acc_ref):
        p = pl.program_id(1)
        qi, ki = qt_ref[p], kt_ref[p]

        @pl.when(ki == 0)
        def _():
            m_ref[...] = jnp.full_like(m_ref, -jnp.inf)
            l_ref[...] = jnp.zeros_like(l_ref)
            acc_ref[...] = jnp.zeros_like(acc_ref)

        def update(masked):
            mask = _fox_mask(qi, ki, tb) if masked else None
            heads = _fox_heads(q_ref, kv_ref, fd)
            scores = [_dot(q, k, NT) for q, k, _ in heads]
            for hh, (q, k, v) in enumerate(heads):
                s = scores[hh] * scale + cq_ref[hh] - ck_ref[hh]
                if masked:
                    s = jnp.where(mask, s, MASK_VALUE)
                m_prev = m_ref[hh]
                m_new = jnp.maximum(m_prev, jnp.max(s, axis=-1, keepdims=True))
                alpha = jnp.exp(m_prev - m_new)
                pe = jnp.exp(s - m_new)
                l_ref[hh] = alpha * l_ref[hh] + jnp.sum(pe, axis=-1, keepdims=True)
                acc_ref[hh] = alpha * acc_ref[hh] + _dot(pe.astype(BF16), v, NN)
                m_ref[hh] = m_new

        edge = (ki == 0) | (ki == qi)
        pl.when(edge)(functools.partial(update, True))
        pl.when(jnp.logical_not(edge))(functools.partial(update, False))

        @pl.when(ki == qi)
        def _():
            real = _row_ids(qi * tb, tb) >= PAD
            for hh in range(FOX_GROUP):
                o_ref[:, hh * fd:(hh + 1) * fd] = jnp.where(real, acc_ref[hh] / l_ref[hh], 0.0)
                lse_ref[hh] = m_ref[hh] + jnp.log(l_ref[hh])

    return pl.pallas_call(
        body, name="fox_fwd",
        grid_spec=pltpu.PrefetchScalarGridSpec(
            num_scalar_prefetch=2, grid=(FOX_HEADS // FOX_GROUP, npairs),
            in_specs=[sp["q"], sp["kv"], sp["col"], sp["row"]],
            out_specs=[sp["head"], sp["col"]],
            scratch_shapes=[pltpu.VMEM((FOX_GROUP, tb, 1), F32), pltpu.VMEM((FOX_GROUP, tb, 1), F32),
                            pltpu.VMEM((FOX_GROUP, tb, fd), F32)]),
        out_shape=[jax.ShapeDtypeStruct((t, FOX_HEADS * fd), F32), jax.ShapeDtypeStruct((FOX_HEADS, t, 1), F32)],
        compiler_params=_params(("parallel", "arbitrary")),
    )(qt, kt, proj, proj, c_col, c_row)


def _fox_delta(o_fox, do_fox, t, fd):
    tb = _pick(t, 384, LANES)

    def body(o_ref, do_ref, out_ref):
        for h in range(FOX_HEADS):
            sl = slice(h * fd, (h + 1) * fd)
            out_ref[h] = jnp.sum(o_ref[:, sl] * do_ref[:, sl].astype(BF16).astype(F32), axis=-1, keepdims=True)

    w = FOX_HEADS * fd
    return pl.pallas_call(
        body, name="fox_delta", grid=(t // tb,),
        in_specs=[pl.BlockSpec((tb, w), lambda i: (i, 0)), pl.BlockSpec((tb, w), lambda i: (i, 0))],
        out_specs=pl.BlockSpec((FOX_HEADS, tb, 1), lambda i: (0, i, 0)),
        out_shape=jax.ShapeDtypeStruct((FOX_HEADS, t, 1), F32),
        compiler_params=_params(("parallel",)),
    )(o_fox, do_fox)


def _fox_bwd(proj, c_col, c_row, lse, delta, do_fox, t, fd, c_fq, c_fkv):
    tb = _pick(t, 384, LANES)
    nb = t // tb
    scale = fd ** -0.5
    sp = _fox_specs(tb, fd, c_fq, c_fkv)
    qt, kt, npairs = _fox_pairs(nb, by_key=True)
    gw = FOX_GROUP * fd

    def body(qt_ref, kt_ref, q_ref, kv_ref, cq_ref, ck_ref, lse_ref, dl_ref, do_ref, dq_ref, dkv_ref, dc_ref, dr_ref,
             dq_acc, dk_acc, dv_acc, dc_acc, dr_acc):
        p = pl.program_id(1)
        qi, ki = qt_ref[p], kt_ref[p]

        @pl.when(p == 0)
        def _():
            dq_acc[...] = jnp.zeros_like(dq_acc)
            dr_acc[...] = jnp.zeros_like(dr_acc)

        @pl.when(qi == ki)
        def _():
            dk_acc[...] = jnp.zeros_like(dk_acc)
            dv_acc[...] = jnp.zeros_like(dv_acc)
            dc_acc[...] = jnp.zeros_like(dc_acc)

        rows = pl.ds(pl.multiple_of(qi * tb, LANES), tb)

        def update(masked):
            mask = _fox_mask(qi, ki, tb) if masked else None
            for hh, (q, k, v) in enumerate(_fox_heads(q_ref, kv_ref, fd)):
                do = _bf(do_ref[:, hh * fd:(hh + 1) * fd])
                s = _dot(q, k, NT) * scale + cq_ref[hh] - ck_ref[hh]
                if masked:
                    s = jnp.where(mask, s, MASK_VALUE)
                pr = jnp.exp(s - lse_ref[hh])
                dp = _dot(do, v, NT)
                ds = pr * (dp - dl_ref[hh])
                ds16 = ds.astype(BF16)
                dv_acc[hh] += _dot(pr.astype(BF16), do, TN)
                dk_acc[hh] += _dot(ds16, q, TN)
                dc_acc[hh] += _colsum(ds)
                dr_acc[hh, rows, :] += jnp.sum(ds, axis=-1, keepdims=True)
                dq_acc[hh, rows, :] += _dot(ds16, k, NN)

        edge = (ki == 0) | (ki == qi)
        pl.when(edge)(functools.partial(update, True))
        pl.when(jnp.logical_not(edge))(functools.partial(update, False))

        @pl.when(qi == nb - 1)
        def _():
            for hh in range(FOX_GROUP):
                dkv_ref[:, 2 * hh * fd:(2 * hh + 1) * fd] = (dk_acc[hh] * scale).astype(dkv_ref.dtype)
                dkv_ref[:, (2 * hh + 1) * fd:(2 * hh + 2) * fd] = dv_acc[hh].astype(dkv_ref.dtype)
                dc_ref[hh] = -dc_acc[hh]

        @pl.when(p == npairs - 1)
        def _():
            for hh in range(FOX_GROUP):
                dq_ref[:, hh * fd:(hh + 1) * fd] = (dq_acc[hh] * scale).astype(dq_ref.dtype)
            dr_ref[...] = dr_acc[...]

    return pl.pallas_call(
        body, name="fox_bwd",
        grid_spec=pltpu.PrefetchScalarGridSpec(
            num_scalar_prefetch=2, grid=(FOX_HEADS // FOX_GROUP, npairs),
            in_specs=[sp["q"], sp["kv"], sp["col"], sp["row"], sp["col"], sp["col"], sp["head"]],
            out_specs=[pl.BlockSpec((t, gw), lambda g, p, qt, kt: (0, g)), sp["key_kv"], sp["row"],
                       pl.BlockSpec((FOX_GROUP, t, 1), lambda g, p, qt, kt: (g, 0, 0))],
            scratch_shapes=[pltpu.VMEM((FOX_GROUP, t, fd), F32), pltpu.VMEM((FOX_GROUP, tb, fd), F32),
                            pltpu.VMEM((FOX_GROUP, tb, fd), F32), pltpu.VMEM((FOX_GROUP, 1, tb), F32),
                            pltpu.VMEM((FOX_GROUP, t, 1), F32)]),
        out_shape=[jax.ShapeDtypeStruct((t, FOX_HEADS * fd), BF16), jax.ShapeDtypeStruct((t, 2 * FOX_HEADS * fd), BF16),
                   jax.ShapeDtypeStruct((FOX_HEADS, 1, t), F32), jax.ShapeDtypeStruct((FOX_HEADS, t, 1), F32)],
        compiler_params=_params(("parallel", "arbitrary")),
    )(qt, kt, proj, proj, c_col, c_row, lse, delta, do_fox)


def _gla_gate_fwd(small, w_alpha_p, b_alpha, t, gk):
    def fn(row0, s, w, b):
        z = _dot(s.astype(BF16), w, NN) + b
        return (jnp.where(_row_ids(row0, s.shape[0]) >= PAD, _log_sigmoid(z) * (1.0 / GLA_TAU), 0.0),)

    return _ew("gla_gate_fwd", fn, [("tile", small, LANES, 0), ("full", w_alpha_p), ("full", b_alpha)],
               [("tile", F32, gk)], t, 264)


def _gla_gate_bwd(dglog, small, w_alpha_p, b_alpha, t, gk):
    def fn(row0, dg, s, w, b):
        z = _dot(s.astype(BF16), w, NN) + b
        dz = jnp.where(_row_ids(row0, s.shape[0]) >= PAD, dg * (1.0 / GLA_TAU) * _sigmoid(-z), 0.0)
        return dz, _colsum(dz)

    return _ew("gla_gate_bwd", fn,
               [("tile", dglog, gk, 0), ("tile", small, LANES, 0), ("full", w_alpha_p), ("full", b_alpha)],
               [("tile", BF16, gk), ("acc", F32, gk)], t, 264)


def _gla_chunk(q, k, g, scale, cs):
    r = lax.broadcasted_iota(jnp.int32, (cs, cs), 0)
    c = lax.broadcasted_iota(jnp.int32, (cs, cs), 1)
    causal = c <= r
    b = _tri_dot(causal.astype(BF16), g)
    bl = b[cs - 1:cs, :]
    eb, einv, eend = jnp.exp(b), jnp.exp(-b), jnp.exp(bl - b)
    qd = q.astype(F32) * scale * eb
    kf = k.astype(F32)
    return causal, (eb, einv, eend), bl, qd, kf * einv, kf * eend


def _gla_fwd(proj, glog, t, dk, dv, c_q, c_k, c_v):
    cs = GLA_CHUNK
    nc = t // cs
    wk, wv = GLA_HEADS * dk, GLA_HEADS * dv
    scale = dk ** -0.5

    def body(q_ref, k_ref, v_ref, g_ref, o_ref, sp_ref, st_ref):
        @pl.when(pl.program_id(0) == 0)
        def _():
            st_ref[...] = jnp.zeros_like(st_ref)

        for h in range(GLA_HEADS):
            ks, vs = slice(h * dk, (h + 1) * dk), slice(h * dv, (h + 1) * dv)
            v = v_ref[:, vs]
            causal, _, bl, qd, ki, ke = _gla_chunk(q_ref[:, ks], k_ref[:, ks], g_ref[:, ks], scale, cs)
            st = st_ref[h]
            sp_ref[h] = st
            a = jnp.where(causal, _dot(qd.astype(BF16), ki.astype(BF16), NT), 0.0)
            o_ref[:, vs] = _dot(a.astype(BF16), v, NN) + _dot(qd.astype(BF16), st.astype(BF16), NT)
            st_ref[h] = st * jnp.exp(bl) + _dot(v, ke.astype(BF16), TN)

    return pl.pallas_call(
        body, name="gla_fwd", grid=(nc,),
        in_specs=[pl.BlockSpec((cs, wk), lambda n: (n, c_q // wk)), pl.BlockSpec((cs, wk), lambda n: (n, c_k // wk)),
                  pl.BlockSpec((cs, wv), lambda n: (n, c_v // wv)), pl.BlockSpec((cs, wk), lambda n: (n, 0))],
        out_specs=[pl.BlockSpec((cs, wv), lambda n: (n, 0)),
                   pl.BlockSpec((None, GLA_HEADS, dv, dk), lambda n: (n, 0, 0, 0))],
        out_shape=[jax.ShapeDtypeStruct((t, wv), F32), jax.ShapeDtypeStruct((nc, GLA_HEADS, dv, dk), F32)],
        scratch_shapes=[pltpu.VMEM((GLA_HEADS, dv, dk), F32)],
        compiler_params=_params(("arbitrary",)),
    )(proj, proj, proj, glog)


def _gla_bwd(proj, glog, s_prev, do_raw, t, dk, dv, c_q, c_k, c_v):
    cs = GLA_CHUNK
    nc = t // cs
    wk, wv = GLA_HEADS * dk, GLA_HEADS * dv
    scale = dk ** -0.5

    def body(q_ref, k_ref, v_ref, g_ref, sp_ref, do_ref, dq_ref, dk_ref, dv_ref, dg_ref, dst_ref):
        @pl.when(pl.program_id(0) == 0)
        def _():
            dst_ref[...] = jnp.zeros_like(dst_ref)

        for h in range(GLA_HEADS):
            ks, vs = slice(h * dk, (h + 1) * dk), slice(h * dv, (h + 1) * dv)
            v = v_ref[:, vs]
            do = do_ref[:, vs].astype(BF16)
            causal, (eb, einv, eend), bl, qd, ki, ke = _gla_chunk(q_ref[:, ks], k_ref[:, ks], g_ref[:, ks], scale, cs)
            qd16, ki16, ke16 = qd.astype(BF16), ki.astype(BF16), ke.astype(BF16)
            st = sp_ref[h]
            dst = dst_ref[h]
            dst16 = dst.astype(BF16)
            a = jnp.where(causal, _dot(qd16, ki16, NT), 0.0).astype(BF16)
            da = jnp.where(causal, _dot(do, v, NT), 0.0).astype(BF16)
            dvv = _dot(a, do, TN) + _dot(ke16, dst16, NT)
            dqd = _dot(da, ki16, NN) + _dot(do, st.astype(BF16), NN)
            dki = _dot(da, qd16, TN)
            dke = _dot(v, dst16, NN)
            dl = jnp.exp(bl)
            ddl = _colsum(dst * st)
            dst_ref[h] = dst * dl + _dot(do, qd16, TN)
            dq_ref[:, ks] = (dqd * eb * scale).astype(dq_ref.dtype)
            dk_ref[:, ks] = (dki * einv + dke * eend).astype(dk_ref.dtype)
            dv_ref[:, vs] = dvv.astype(dv_ref.dtype)
            db = dqd * qd - dki * ki - dke * ke
            db_last = _colsum(dke * ke) + ddl * dl
            r = lax.broadcasted_iota(jnp.int32, (cs, cs), 0)
            c = lax.broadcasted_iota(jnp.int32, (cs, cs), 1)
            dg_ref[:, ks] = _tri_dot((c >= r).astype(BF16), db) + db_last

    rev = lambda f: (lambda n: f(nc - 1 - n))
    return pl.pallas_call(
        body, name="gla_bwd", grid=(nc,),
        in_specs=[pl.BlockSpec((cs, wk), rev(lambda n: (n, c_q // wk))), pl.BlockSpec((cs, wk), rev(lambda n: (n, c_k // wk))),
                  pl.BlockSpec((cs, wv), rev(lambda n: (n, c_v // wv))), pl.BlockSpec((cs, wk), rev(lambda n: (n, 0))),
                  pl.BlockSpec((None, GLA_HEADS, dv, dk), rev(lambda n: (n, 0, 0, 0))),
                  pl.BlockSpec((cs, wv), rev(lambda n: (n, 0)))],
        out_specs=[pl.BlockSpec((cs, wk), rev(lambda n: (n, 0))), pl.BlockSpec((cs, wk), rev(lambda n: (n, 0))),
                   pl.BlockSpec((cs, wv), rev(lambda n: (n, 0))), pl.BlockSpec((cs, wk), rev(lambda n: (n, 0)))],
        out_shape=[jax.ShapeDtypeStruct((t, wk), BF16), jax.ShapeDtypeStruct((t, wk), BF16),
                   jax.ShapeDtypeStruct((t, wv), BF16), jax.ShapeDtypeStruct((t, wk), F32)],
        scratch_shapes=[pltpu.VMEM((GLA_HEADS, dv, dk), F32)],
        compiler_params=_params(("arbitrary",)),
    )(proj, proj, proj, glog, s_prev, do_raw)


def _gla_post_fwd(o_raw, proj, gn, t, dv, c_gr):
    w = GLA_HEADS * dv

    def fn(row0, o, gr, g):
        gr = gr.astype(F32)
        outs = []
        for h in range(GLA_HEADS):
            oh = o[:, h * dv:(h + 1) * dv]
            outs.append(oh * lax.rsqrt(jnp.mean(oh * oh, axis=-1, keepdims=True) + EPS))
        on = jnp.concatenate(outs, axis=1) * g
        return (on * (gr * _sigmoid(gr)),)

    return _ew("gla_post_fwd", fn, [("tile", o_raw, w, 0), ("tile", proj, w, c_gr), ("full", gn)],
               [("tile", BF16, w)], t, 264)


def _gla_post_bwd(o_raw, proj, gn, do_gla, t, dv, c_gr):
    w = GLA_HEADS * dv

    def fn(row0, o, gr, g, do):
        gr = gr.astype(F32)
        sg = _sigmoid(gr)
        don = do * (gr * sg)
        ohs, dos = [], []
        for h in range(GLA_HEADS):
            sl = slice(h * dv, (h + 1) * dv)
            oh = o[:, sl]
            r = lax.rsqrt(jnp.mean(oh * oh, axis=-1, keepdims=True) + EPS)
            xh = oh * r
            dxh = don[:, sl] * g[:, sl]
            ohs.append(xh)
            dos.append(r * (dxh - xh * jnp.mean(dxh * xh, axis=-1, keepdims=True)))
        xh = jnp.concatenate(ohs, axis=1)
        dgr = do * (xh * g) * (sg * (1.0 + gr * (1.0 - sg)))
        return jnp.concatenate(dos, axis=1), dgr, _colsum(don * xh)

    return _ew("gla_post_bwd", fn,
               [("tile", o_raw, w, 0), ("tile", proj, w, c_gr), ("full", gn), ("tile", do_gla, w, 0)],
               [("tile", F32, w), ("tile", BF16, w), ("acc", F32, w)], t, 264)


def _adamw(name, w, g, m, v, layer=None, into=None):
    nl, rows, cols = w.shape
    tm = _pick(rows, max(8, (512 * 1024) // max(cols, 1) // 8 * 8), 8)

    def body(w_ref, g_ref, m_ref, v_ref, *rest):
        go_ref, d_ref, nm_ref, nv_ref = rest[-4:]
        gg = g_ref[...]
        nm = ADAM_B1 * m_ref[...] + (1.0 - ADAM_B1) * gg
        nv = ADAM_B2 * v_ref[...] + (1.0 - ADAM_B2) * (gg * gg)
        m_hat = nm / (1.0 - ADAM_B1 ** ADAM_STEP)
        v_hat = nv / (1.0 - ADAM_B2 ** ADAM_STEP)
        go_ref[...] = gg
        d_ref[...] = -ADAM_LR * (m_hat / (jnp.sqrt(v_hat) + ADAM_EPS) + ADAM_WD * w_ref[...])
        nm_ref[...] = nm
        nv_ref[...] = nv

    out_shape = [jax.ShapeDtypeStruct((nl, rows, cols), F32)] * 4
    if layer is None:
        spec = pl.BlockSpec((None, tm, cols), lambda l, i: (l, i, 0))
        return pl.pallas_call(
            body, name=name, grid=(nl, rows // tm), in_specs=[spec] * 4, out_specs=[spec] * 4, out_shape=out_shape,
            compiler_params=_params(("parallel", "parallel")),
        )(w, g, m, v)
    spec = pl.BlockSpec((None, tm, cols), lambda i: (layer, i, 0))
    in_specs = [spec, pl.BlockSpec((tm, cols), lambda i: (i, 0)), spec, spec]
    args, aliases = [w, g, m, v], {}
    if into is not None:
        in_specs += [pl.BlockSpec(memory_space=pl.ANY)] * 4
        args += list(into)
        aliases = {4 + k: k for k in range(4)}
    return pl.pallas_call(
        body, name=name, grid=(rows // tm,), in_specs=in_specs, out_specs=[spec] * 4, out_shape=out_shape,
        input_output_aliases=aliases, compiler_params=_params(("parallel",)),
    )(*args)


def _me():
    return lax.axis_index("x"), lax.axis_index("y"), lax.axis_index("c")


def _hbm_specs(n):
    return [pl.BlockSpec(memory_space=pl.ANY)] * n


_HBM = pl.BlockSpec(memory_space=pltpu.HBM)
_SEM = pl.BlockSpec(memory_space=pltpu.SEMAPHORE)
_EFFECT = pltpu.SideEffectType.DATAFLOW_SIDE_EFFECTING
N_PEERS = N_CHIPS - 1


def _other_chips(x, y):
    return [(1 - x, y), (x, 1 - y), (1 - x, 1 - y)]


def _split_copies(plan, src, land, send_sems, recv_sems):
    me = _me()
    copies = []
    for i in range(len(src)):
        for j, (s, d, peer) in enumerate(plan(src[i], land[i], me)):
            k = plan.copies * i + j
            copies.append(pltpu.make_async_remote_copy(
                src_ref=s, dst_ref=d, send_sem=send_sems.at[k], recv_sem=recv_sems.at[k], device_id=peer,
                device_id_type=MESH))
    return copies


def _split_start(name, srcs, lands, plan, after=None):
    n = len(srcs)
    extra = [] if after is None else [after]

    def body(*refs):
        src, land = refs[:n], refs[n:2 * n]
        send_sems, recv_sems = refs[2 * n + len(extra)], refs[2 * n + len(extra) + 1]
        token = refs[-1]
        for cp in _split_copies(plan, src, land, send_sems, recv_sems):
            cp.start()
        token[...] = jnp.zeros_like(token)

    out_shape = ([pltpu.SemaphoreType.DMA((plan.copies * n,)), pltpu.SemaphoreType.DMA((plan.copies * n,))]
                 + [pltpu.HBM(a.shape, a.dtype) for a in list(srcs) + list(lands)]
                 + [jax.ShapeDtypeStruct((8, LANES), F32)])
    res = pl.pallas_call(
        body, name=name, out_shape=out_shape,
        in_specs=[_HBM] * (2 * n) + [pl.BlockSpec(memory_space=pl.ANY)] * len(extra),
        out_specs=[_SEM, _SEM] + [_HBM] * (2 * n) + [pl.BlockSpec(memory_space=pltpu.VMEM)],
        input_output_aliases={i: 2 + i for i in range(2 * n)},
        compiler_params=pltpu.CompilerParams(has_side_effects=_EFFECT),
    )(*[pltpu.with_memory_space_constraint(a, pltpu.HBM) for a in list(srcs) + list(lands)], *extra)
    return res[0], res[1], res[2:2 + n], res[2 + n:2 + 2 * n], res[-1]


def _split_wait(name, send_sems, recv_sems, srcs, lands, plan, after):
    n = len(srcs)

    def body(*refs):
        src, land = refs[:n], refs[n:2 * n]
        s_sems, r_sems = refs[2 * n], refs[2 * n + 1]
        for cp in _split_copies(plan, src, land, s_sems, r_sems):
            cp.wait_send()
            cp.wait_recv()

    res = pl.pallas_call(
        body, name=name, out_shape=[pltpu.HBM(a.shape, a.dtype) for a in list(srcs) + list(lands)],
        in_specs=[_HBM] * (2 * n) + [_SEM, _SEM, pl.BlockSpec(memory_space=pl.ANY)], out_specs=[_HBM] * (2 * n),
        input_output_aliases={i: i for i in range(2 * n)},
        compiler_params=pltpu.CompilerParams(has_side_effects=_EFFECT),
    )(*srcs, *lands, send_sems, recv_sems, after)
    return res[:n], res[n:]


def _exchange_start(name, bufs, plan, after):
    n = len(bufs)

    def body(*refs):
        buf = refs[:n]
        for cp in _split_copies(plan, buf, buf, refs[n + 1], refs[n + 2]):
            cp.start()
        refs[-1][...] = jnp.zeros_like(refs[-1])

    out_shape = ([pltpu.SemaphoreType.DMA((plan.copies * n,)), pltpu.SemaphoreType.DMA((plan.copies * n,))]
                 + [pltpu.HBM(a.shape, a.dtype) for a in bufs] + [jax.ShapeDtypeStruct((8, LANES), F32)])
    res = pl.pallas_call(
        body, name=name, out_shape=out_shape, in_specs=[_HBM] * n + [pl.BlockSpec(memory_space=pl.ANY)],
        out_specs=[_SEM, _SEM] + [_HBM] * n + [pl.BlockSpec(memory_space=pltpu.VMEM)],
        input_output_aliases={i: 2 + i for i in range(n)},
        compiler_params=pltpu.CompilerParams(has_side_effects=_EFFECT),
    )(*[pltpu.with_memory_space_constraint(a, pltpu.HBM) for a in bufs], after)
    return res[0], res[1], res[2:2 + n], res[-1]


def _exchange_wait(name, send_sems, recv_sems, bufs, plan, after):
    n = len(bufs)

    def body(*refs):
        buf = refs[:n]
        for cp in _split_copies(plan, buf, buf, refs[n], refs[n + 1]):
            cp.wait_send()
            cp.wait_recv()

    return pl.pallas_call(
        body, name=name, out_shape=[pltpu.HBM(a.shape, a.dtype) for a in bufs],
        in_specs=[_HBM] * n + [_SEM, _SEM, pl.BlockSpec(memory_space=pl.ANY)], out_specs=[_HBM] * n,
        input_output_aliases={i: i for i in range(n)},
        compiler_params=pltpu.CompilerParams(has_side_effects=_EFFECT),
    )(*bufs, send_sems, recv_sems, after)


def _forward_plan(src, land, me):
    x, y, c = me
    rows = _half(src.shape[1], c)
    return [(src.at[2 * px + py, rows], land.at[2 * px + py, rows], (x, y, 1 - c)) for px, py in _other_chips(x, y)]


def _share_plan(src, land, me):
    x, y, c = me
    rows = _half(src.shape[0], c)
    return [(src.at[rows], land.at[rows], (x, y, 1 - c))]


_forward_plan.copies = N_CHIPS - 1
_share_plan.copies = 1


def _half(ref_rows, c):
    half = ref_rows // 2
    return pl.ds(c * half, half)


def _gather_plan(src, land, me):
    x, y, c = me
    rows = _half(src.shape[0], c)
    return [(src.at[rows], land.at[2 * x + y, rows], (px, py, c)) for px, py in _other_chips(x, y)]


def _scatter_plan(src, land, me):
    x, y, c = me
    return [(src.at[2 * px + py], land.at[2 * x + y], (px, py, c)) for px, py in _other_chips(x, y)]


def _swap_plan(src, land, me):
    x, y, c = me
    return [(src.at[:, _half(src.shape[1], 1 - c)], land, (x, y, 1 - c))]


_gather_plan.copies = N_PEERS
_scatter_plan.copies = N_PEERS
_swap_plan.copies = 1


def _all_gather_small(v):
    def body(v_ref, out_ref, send_sems, recv_sems, local_sem):
        x, y, c = _me()
        mine = pltpu.make_async_copy(v_ref, out_ref.at[4 * x + 2 * y + c], local_sem)
        mine.start()
        copies = []
        for k in range(1, N_DEV):
            peer = (x ^ ((k >> 2) & 1), y ^ ((k >> 1) & 1), c ^ (k & 1))
            cp = pltpu.make_async_remote_copy(
                src_ref=v_ref, dst_ref=out_ref.at[4 * x + 2 * y + c], send_sem=send_sems.at[k - 1],
                recv_sem=recv_sems.at[k - 1], device_id=peer, device_id_type=MESH)
            cp.start()
            copies.append(cp)
        for cp in copies:
            cp.wait_recv()
        for cp in copies:
            cp.wait_send()
        mine.wait()

    return pl.pallas_call(
        body, name="all_gather_small", in_specs=_hbm_specs(1), out_specs=pl.BlockSpec(memory_space=pl.ANY),
        out_shape=jax.ShapeDtypeStruct((N_DEV,) + v.shape, v.dtype),
        scratch_shapes=[pltpu.SemaphoreType.DMA((N_DEV - 1,)), pltpu.SemaphoreType.DMA((N_DEV - 1,)),
                        pltpu.SemaphoreType.DMA],
    )(v)


def _pair_sum(name, own, recv, chip, core):
    nch, half, cdim = recv.shape
    tm = _pick(half, max(8, (512 * 1024) // cdim // 16 * 16), 16)
    nt = half // tm

    def body(c_ref, k_ref, a_ref, b_ref, s16_ref):
        s16_ref[...] = (a_ref[...] + b_ref[...]).astype(BF16)

    other = lambda j, c: (c[0] + 1 + j) % nch
    spec = pl.BlockSpec((None, tm, cdim), lambda j, i, c, k: (other(j, c), i, 0))
    return pl.pallas_call(
        body, name=name,
        grid_spec=pltpu.PrefetchScalarGridSpec(
            num_scalar_prefetch=2, grid=(nch - 1, nt),
            in_specs=[pl.BlockSpec((None, tm, cdim), lambda j, i, c, k: (other(j, c), k[0] * nt + i, 0)), spec],
            out_specs=spec),
        out_shape=jax.ShapeDtypeStruct((nch, half, cdim), BF16),
        compiler_params=_params(("parallel", "parallel")),
    )(chip, core, own, recv)


def _chip_sum(name, own, recv, landed, chip, core):
    nch, half, cdim = recv.shape
    tm = _pick(half, max(8, (512 * 1024) // cdim // 16 * 16), 16)
    nt = half // tm

    def body(c_ref, k_ref, own_ref, recv_ref, *rest):
        landed_refs, out_ref = rest[:nch], rest[-1]
        me = c_ref[0]
        mine = own_ref[...] + recv_ref[...]
        acc = None
        for j in range(nch):
            term = jnp.where(me == j, mine, landed_refs[j][...].astype(F32))
            acc = term if acc is None else acc + term
        out_ref[...] = acc

    landed_specs = [pl.BlockSpec((None, tm, cdim), functools.partial(lambda i, c, k, j: (j, i, 0), j=j))
                    for j in range(nch)]
    return pl.pallas_call(
        body, name=name,
        grid_spec=pltpu.PrefetchScalarGridSpec(
            num_scalar_prefetch=2, grid=(nt,),
            in_specs=[pl.BlockSpec((None, tm, cdim), lambda i, c, k: (c[0], k[0] * nt + i, 0)),
                      pl.BlockSpec((None, tm, cdim), lambda i, c, k: (c[0], i, 0))] + landed_specs,
            out_specs=pl.BlockSpec((tm, cdim), lambda i, c, k: (k[0] * nt + i, 0))),
        out_shape=jax.ShapeDtypeStruct((2 * half, cdim), F32),
        compiler_params=_params(("parallel",)),
    )(chip, core, own, recv, *([landed] * nch))


def _sum_devices(gathered):
    _, r, cdim = gathered.shape

    def body(g_ref, o_ref):
        acc = g_ref[0]
        for k in range(1, N_DEV):
            acc = acc + g_ref[k]
        o_ref[...] = acc

    return pl.pallas_call(
        body, name="sum_devices", out_shape=jax.ShapeDtypeStruct((r, cdim), F32),
        compiler_params=_params(),
    )(gathered)


class _Layout:
    def __init__(self, d):
        self.d = d
        self.fw = d // 2
        self.fd = self.fw // FOX_HEADS
        self.gk = d // 2
        self.gv = d
        self.dk = self.gk // GLA_HEADS
        self.dv = self.gv // GLA_HEADS
        self.c_fq = 0
        self.c_gq = self.fw
        self.c_gv = self.c_gq + self.gk
        self.c_gr = self.c_gv + self.gv
        self.c_fkv = self.c_gr + self.gv
        self.c_gates = self.c_fkv + 2 * self.fw
        self.c_gk = self.c_gates + 2 * d
        self.c_small = self.c_gk + self.gk
        self.n_main = self.c_small
        self.n_p = self.c_small + LANES
        self.o_fk = self.fw
        self.o_fv = 2 * self.fw
        self.o_ff = 3 * self.fw
        self.o_gq = self.o_ff + FOX_HEADS
        self.o_gk = self.o_gq + self.gk
        self.o_gv = self.o_gk + self.gk
        self.o_gr = self.o_gv + self.gv
        self.o_ga = self.o_gr + self.gv
        self.o_gf = self.o_ga + GLA_RANK
        self.o_gg = self.o_gf + d
        self.n_orig = self.o_gg + d

    def to_p(self, shards):
        per = self.n_orig // N_CHIPS
        ranges = [(0, self.fw), (self.o_gq, self.gk), (self.o_gv, self.gv), (self.o_gr, self.gv)]
        for h in range(FOX_HEADS):
            ranges += [(self.o_fk + h * self.fd, self.fd), (self.o_fv + h * self.fd, self.fd)]
        ranges += [(self.o_gf, 2 * self.d), (self.o_gk, self.gk), (self.o_ff, FOX_HEADS), (self.o_ga, GLA_RANK)]
        pieces = []
        for a, width in ranges:
            for j in range(a // per, (a + width - 1) // per + 1):
                lo, hi = max(a, j * per), min(a + width, (j + 1) * per)
                pieces.append(shards[j][:, lo - j * per:hi - j * per])
        pieces.append(jnp.zeros((shards.shape[1], LANES - FOX_HEADS - GLA_RANK), shards.dtype))
        return jnp.concatenate(pieces, axis=1)

    def from_segments(self, seg):
        per = self.n_orig // N_CHIPS
        fd = self.fd
        atoms = [("fq", 0, self.fw)]
        atoms += [("fkv", 2 * h * fd, fd) for h in range(FOX_HEADS)]
        atoms += [("fkv", (2 * h + 1) * fd, fd) for h in range(FOX_HEADS)]
        atoms += [("small", 0, FOX_HEADS), ("gq", 0, self.gk), ("gk", 0, self.gk), ("gv", 0, self.gv),
                  ("gr", 0, self.gv), ("small", FOX_HEADS, GLA_RANK), ("gates", 0, 2 * self.d)]
        shards = [[] for _ in range(N_CHIPS)]
        pos = 0
        for name, c0, width in atoms:
            for j in range(pos // per, (pos + width - 1) // per + 1):
                lo, hi = max(pos, j * per), min(pos + width, (j + 1) * per)
                shards[j].append(seg[name][:, c0 + lo - pos:c0 + hi - pos])
            pos += width
        assert pos == self.n_orig
        return jnp.stack([jnp.concatenate(s, axis=1) for s in shards])


def _layer_fwd(lay, h, p, t, hooks=None):
    hooks = hooks or {}
    d = lay.d
    xn, xn_t = _rms_fwd_t("rms_mix_fwd", h, p["norm_mix_g"], t, d)
    proj = _mm("mm_proj", xn, p["w_in"], mode="nn", m=t, n=lay.n_main, k=d, out_dtypes=(BF16,))
    small = _mm("mm_small", xn, p["w_in"], mode="nn", m=t, n=LANES, k=d, b_c0=lay.c_small)
    cs = _fox_gate_fwd(small, p["b_forget_p"], t)
    ct = cs[:, :FOX_HEADS].T
    c_col, c_row = ct[:, :, None], ct[:, None, :]
    o_fox, lse = _fox_fwd(proj, c_col, c_row, t, lay.fd, lay.c_fq, lay.c_fkv)
    b_alpha = p["b_alpha"] + hooks["mixers"](o_fox) if "mixers" in hooks else p["b_alpha"]
    glog = _gla_gate_fwd(small, p["w_alpha_p"], b_alpha, t, lay.gk)
    o_raw, s_prev = _gla_fwd(proj, glog, t, lay.dk, lay.dv, lay.c_gq, lay.c_gk, lay.c_gv)
    o_gla = _gla_post_fwd(o_raw, proj, p["gla_norm_g"], t, lay.dv, lay.c_gr)
    if "late" in hooks:
        p.update(hooks["late"](o_gla))
    a_fox = _mm("mm_o_fox", o_fox, p["w_o_fox"], mode="nn", m=t, n=d, k=lay.fw, b_shards=N_CHIPS)
    a_gla = _mm("mm_o_gla", o_gla, p["w_o_gla"], mode="nn", m=t, n=d, k=lay.gv)
    y = _merge_fwd(a_fox, a_gla, proj, lay.c_gates, t, d)
    h1 = _mm("mm_out", y, p["w_out"], mode="nn", m=t, n=d, k=d, extras=[h], epilogue=lambda acc, res: (res + acc,))
    xn2, xn2_t = _rms_fwd_t("rms_mlp_fwd", h1, p["norm_mlp_g"], t, d)
    u, act = _mm("mm_ff1", xn2, p["w_ff1"], mode="nn", m=t, n=4 * d, k=d, out_dtypes=(BF16, BF16), b_shards=N_CHIPS,
                 epilogue=lambda acc: (acc, jnp.square(jnp.maximum(acc, 0.0))))
    if "mlp" in hooks:
        hooks["mlp"](act)
    h2 = _mm("mm_ff2", act, p["w_ff2"], mode="nn", m=t, n=d, k=4 * d, extras=[h1],
             epilogue=lambda acc, res: (res + acc,))
    saved = dict(h=h, xn_t=xn_t, proj=proj, small=small, c_col=c_col, c_row=c_row, o_fox=o_fox, lse=lse, glog=glog,
                 o_raw=o_raw, s_prev=s_prev, o_gla=o_gla, a_fox=a_fox, a_gla=a_gla, y=y, h1=h1, xn2_t=xn2_t, u=u, act=act)
    return h2, saved


def _layer_bwd(lay, dh2, p, s, t, gates=None, first=False):
    d = lay.d
    g = {}

    def gated(gain, point):
        return gain + gates[point](g) if gates and point in gates else gain
    du = _mm("mm_dact", dh2, p["w_ff2"], mode="nt", m=t, n=4 * d, k=d, extras=[s["u"]], out_dtypes=(BF16,),
             epilogue=lambda acc, u: (acc * (2.0 * jnp.maximum(u.astype(F32), 0.0)),))
    g["w_ff2"] = _mm("mm_dw_ff2", s["act"], dh2, mode="tn", m=4 * d, n=d, k=t)
    g["w_ff1"] = _mm("mm_dw_ff1", s["xn2_t"], du, mode="nn", m=d, n=4 * d, k=t, tk=t, out_shards=N_CHIPS)
    dxn2 = _mm("mm_dxn2", du, p["w_ff1"], mode="nt", m=t, n=d, k=4 * d, b_shards=N_CHIPS)
    dh1, g["norm_mlp_g"] = _rms_bwd("rms_mlp_bwd", s["h1"], gated(p["norm_mlp_g"], "mlp"), dxn2, dh2, t, d)
    dy = _mm("mm_dy", dh1, p["w_out"], mode="nt", m=t, n=d, k=d)
    g["w_out"] = _mm("mm_dw_out", s["y"], dh1, mode="tn", m=d, n=d, k=t)
    da_fox, da_gla, dgates = _merge_bwd(dy, s["a_fox"], s["a_gla"], s["proj"], lay.c_gates, t, d)
    g["w_o_fox"] = _mm("mm_dw_o_fox", s["o_fox"], da_fox, mode="tn", m=lay.fw, n=d, k=t, out_shards=N_CHIPS)
    do_fox = _mm("mm_do_fox", da_fox, p["w_o_fox"], mode="nt", m=t, n=lay.fw, k=d, b_shards=N_CHIPS)
    g["w_o_gla"] = _mm("mm_dw_o_gla", s["o_gla"], da_gla, mode="tn", m=lay.gv, n=d, k=t)
    do_gla = _mm("mm_do_gla", da_gla, p["w_o_gla"], mode="nt", m=t, n=lay.gv, k=d)
    do_raw, dgr, g["gla_norm_g"] = _gla_post_bwd(s["o_raw"], s["proj"], gated(p["gla_norm_g"], "out"), do_gla, t,
                                                 lay.dv, lay.c_gr)
    dgq, dgk, dgv, dglog = _gla_bwd(s["proj"], s["glog"], s["s_prev"], do_raw, t, lay.dk, lay.dv,
                                    lay.c_gq, lay.c_gk, lay.c_gv)
    dz, g["b_alpha"] = _gla_gate_bwd(dglog, s["small"], p["w_alpha_p"], p["b_alpha"], t, lay.gk)
    g["w_alpha_p"] = _mm("mm_dw_alpha", s["small"], dz, mode="tn", m=LANES, n=lay.gk, k=t)
    dga = _mm("mm_dga", dz, p["w_alpha_p"], mode="nt", m=t, n=LANES, k=lay.gk)
    delta = _fox_delta(s["o_fox"], do_fox, t, lay.fd)
    dfq, dfkv, dc, dr = _fox_bwd(s["proj"], s["c_col"], s["c_row"], s["lse"], delta, do_fox, t, lay.fd,
                                 lay.c_fq, lay.c_fkv)
    dc_p = jnp.pad((dc[:, 0, :] + dr[:, :, 0]).T, ((0, 0), (0, LANES - FOX_HEADS)))
    dsmall, g["b_forget_p"] = _fox_gate_bwd(dc_p, s["small"], p["b_forget_p"], dga, t)
    segs = [("fq", dfq, lay.c_fq), ("gq", dgq, lay.c_gq), ("gv", dgv, lay.c_gv), ("gr", dgr, lay.c_gr),
            ("fkv", dfkv, lay.c_fkv), ("gates", dgates, lay.c_gates), ("gk", dgk, lay.c_gk),
            ("small", dsmall, lay.c_small)]
    dw_in = {nm: _mm("mm_dw_in_" + nm, s["xn_t"], dseg, mode="nn", m=d, n=dseg.shape[1], k=t, tk=t)
             for nm, dseg, _ in segs}
    g["w_in"] = lay.from_segments(dw_in)
    dxn = _mm("mm_dxn_gates", dgates, p["w_in"], mode="nt", m=t, n=d, k=2 * d, b_c0=lay.c_gates)
    dxn = _mm("mm_dxn_small", dsmall, p["w_in"], mode="nt", m=t, n=d, k=LANES, b_c0=lay.c_small, extras=[dxn],
              epilogue=lambda acc, prev: (prev + acc,))
    for nm, (a1, c1), (a2, c2) in (("q", (dfq, lay.c_fq), (dgq, lay.c_gq)), ("v", (dgv, lay.c_gv), (dgr, lay.c_gr)),
                                   ("k", (dfkv, lay.c_fkv), (dgk, lay.c_gk))):
        dxn = _mm_nt2("mm_dxn_" + nm, a1, c1, a2, c2, p["w_in"], dxn, m=t, n=d)
    if first:
        dx, head, g["norm_mix_g"] = _rms_bwd_x("rms_mix_bwd_x", s["h"], gated(p["norm_mix_g"], "in"), dxn, dh1, t, d)
        return (dx, head), g
    dh, g["norm_mix_g"] = _rms_bwd("rms_mix_bwd", s["h"], gated(p["norm_mix_g"], "in"), dxn, dh1, t, d)
    return dh, g


def _pack_small(d, meta, mix, gla, mlp, final, b_alpha, b_forget, w_alpha2):
    rows = [meta.reshape(N_META, d), mix.reshape(DEPTH, d), gla.reshape(DEPTH, d), mlp.reshape(DEPTH, d),
            final.reshape(1, d), b_alpha.reshape(1, d),
            jnp.pad(b_forget.reshape(1, DEPTH * FOX_HEADS), ((0, 0), (0, d - DEPTH * FOX_HEADS))),
            jnp.zeros((7, d), F32), w_alpha2.reshape(GLA_RANK, d)]
    return jnp.concatenate(rows, axis=0)


def _unpack_small(d, packed):
    return dict(meta=packed[:N_META], norm_mix_g=packed[16:18], gla_norm_g=packed[18:20], norm_mlp_g=packed[20:22],
                final_norm_g=packed[22], b_alpha=packed[23].reshape(DEPTH, d // 2),
                b_forget=packed[24, :DEPTH * FOX_HEADS].reshape(DEPTH, FOX_HEADS),
                w_alpha2=packed[32:48].reshape(DEPTH, GLA_RANK, d // 2))


_BIG = ("w_in", "w_o_fox", "w_o_gla", "w_out", "w_ff1", "w_ff2")
_COL_SHARDED = ("w_in", "w_o_fox", "w_ff1")


def _full_matrix(name, gathered_l):
    nch, r, c = gathered_l.shape
    if name in _COL_SHARDED:
        return gathered_l.transpose(1, 0, 2).reshape(r, nch * c)
    return gathered_l.reshape(nch * r, c)


def _shard_major(name, full):
    r, c = full.shape
    if name in _COL_SHARDED:
        return full.reshape(r, N_CHIPS, c // N_CHIPS).transpose(1, 0, 2)
    return full.reshape(N_CHIPS, r // N_CHIPS, c)


def kernel(x, meta_tokens, norm_mix_g, w_in, b_forget, w_alpha2, b_alpha, gla_norm_g, w_o_fox, w_o_gla, w_out, norm_mlp_g, w_ff1, w_ff2, final_norm_g, loss_target, m_meta_tokens, m_norm_mix_g, m_w_in, m_b_forget, m_w_alpha2, m_b_alpha, m_gla_norm_g, m_w_o_fox, m_w_o_gla, m_w_out, m_norm_mlp_g, m_w_ff1, m_w_ff2, m_final_norm_g, v_meta_tokens, v_norm_mix_g, v_w_in, v_b_forget, v_w_alpha2, v_b_alpha, v_gla_norm_g, v_w_o_fox, v_w_o_gla, v_w_out, v_norm_mlp_g, v_w_ff1, v_w_ff2, v_final_norm_g):
    d = x.shape[2]
    lay = _Layout(d)
    xi, yi, ci = lax.axis_index("x"), lax.axis_index("y"), lax.axis_index("c")
    chip = (2 * xi + yi).astype(jnp.int32)
    w = dict(w_in=w_in, w_alpha2=w_alpha2, w_o_fox=w_o_fox, w_o_gla=w_o_gla, w_out=w_out, w_ff1=w_ff1, w_ff2=w_ff2)
    m = dict(w_in=m_w_in, w_alpha2=m_w_alpha2, w_o_fox=m_w_o_fox, w_o_gla=m_w_o_gla, w_out=m_w_out, w_ff1=m_w_ff1,
             w_ff2=m_w_ff2)
    v = dict(w_in=v_w_in, w_alpha2=v_w_alpha2, w_o_fox=v_w_o_fox, w_o_gla=v_w_o_gla, w_out=v_w_out, w_ff1=v_w_ff1,
             w_ff2=v_w_ff2)

    seq = x.shape[1]
    t = seq + ROW0
    core_idx = ci.astype(jnp.int32)[None]
    chip_idx = chip[None]

    cols = d // N_CHIPS
    small_w = jnp.concatenate([meta_tokens, w_alpha2.reshape(-1, cols)], axis=0)
    small_raw = _all_gather_small(small_w)
    small_all = small_raw[0::2]
    alpha_full = small_all[:, N_META:].reshape(N_CHIPS, DEPTH, GLA_RANK, lay.gk // N_CHIPS)
    alpha_full = alpha_full.transpose(1, 2, 0, 3).reshape(DEPTH, GLA_RANK, lay.gk)
    groups = [(0, ("w_in",)), (0, _BIG[1:]), (1, _BIG)]
    started, after = [], small_raw
    for gi, (l, names) in enumerate(groups):
        own16 = [w[n][l].astype(BF16) for n in names]
        lands = [lax.empty((N_CHIPS,) + o.shape, BF16) for o in own16]
        started.append(_split_start("gather_start_%d" % gi, own16, lands, _gather_plan, after=after))
        after = started[gi][4]
    meta_full = small_all[:, :N_META].transpose(1, 0, 2).reshape(N_META, d) + after[0, 0]

    passing = {}

    def arrive(gi, after):
        send_sems, recv_sems, srcs, lands, _ = started[gi]
        srcs, lands = _split_wait("gather_wait_%d" % gi, send_sems, recv_sems, srcs, lands, _gather_plan, after)
        passing[gi] = (srcs, _exchange_start("gather_pass_%d" % gi, lands, _forward_plan, after=srcs[0]))
        return passing[gi][1][3][0, 0]

    def gathered(gi, after):
        if gi not in passing:
            arrive(gi, after)
        srcs, (send_sems, recv_sems, lands, _) = passing[gi]
        lands = _exchange_wait("gather_pass_wait_%d" % gi, send_sems, recv_sems, lands, _forward_plan, after)
        return {n: lax.dynamic_update_slice(g, o[None], (chip, 0, 0)) for n, g, o in zip(groups[gi][1], lands, srcs)}

    def early_weights(l, gl):
        w_alpha_p = jnp.zeros((LANES, lay.gk), BF16).at[FOX_HEADS:FOX_HEADS + GLA_RANK].set(
            alpha_full[l].astype(BF16))
        return dict(
            w_in=lay.to_p(gl["w_in"]), w_alpha_p=w_alpha_p,
            norm_mix_g=norm_mix_g[l][None], norm_mlp_g=norm_mlp_g[l][None], gla_norm_g=gla_norm_g[l][None],
            b_alpha=b_alpha[l][None],
            b_forget_p=jnp.pad(b_forget[l][None], ((0, 0), (0, LANES - FOX_HEADS))))

    def late_weights(gl):
        return dict(w_o_fox=gl["w_o_fox"], w_o_gla=_full_matrix("w_o_gla", gl["w_o_gla"]),
                    w_out=_full_matrix("w_out", gl["w_out"]), w_ff1=gl["w_ff1"],
                    w_ff2=_full_matrix("w_ff2", gl["w_ff2"]))

    h = jnp.pad(x[0], ((ROW0, 0), (0, 0))).at[PAD:ROW0].set(meta_full)
    layers, saved = [], []
    layers.append(early_weights(0, gathered(0, after=h)))
    h, s = _layer_fwd(lay, h, layers[0], t, hooks=dict(
        mixers=lambda after: arrive(1, after), late=lambda after: late_weights(gathered(1, after)),
        mlp=lambda after: arrive(2, after)))
    saved.append(s)
    gl = gathered(2, after=h)
    layers.append({**early_weights(1, gl), **late_weights(gl)})
    h, s = _layer_fwd(lay, h, layers[1], t)
    saved.append(s)
    dh, dg_final, loss_part = _loss_head(h, final_norm_g[None], jnp.pad(loss_target[0], ((ROW0, 0), (0, 0))), t, d)
    loss = lax.psum(loss_part[0, 0], ("x", "y", "c"))

    def partial_of(g, n):
        return g[n] if n in ("w_ff1", "w_o_fox", "w_in") else _shard_major(n, g[n])

    scatter_groups = dict(mlp=("w_ff1", "w_ff2"), out=("w_o_fox", "w_o_gla", "w_out"))
    scatter_groups["in"] = ("w_in",)
    swapping, scattered = [], {}

    def start_swap(l, grp, g, after=None):
        parts = [partial_of(g, n) for n in scatter_groups[grp]]
        lands = [lax.empty((p_.shape[0], p_.shape[1] // 2, p_.shape[2]), F32) for p_ in parts]
        started_swap = _split_start("swap_start_%d_%s" % (l, grp), parts, lands, _swap_plan, after=after)
        swapping.append((l, grp, started_swap))
        return started_swap[4]

    def start_scatter(after):
        l, grp, (send_sems, recv_sems, srcs, lands, _) = swapping.pop(0)
        names = scatter_groups[grp]
        tag = "%d_%s" % (l, grp)
        parts, from_sibling = _split_wait("swap_wait_" + tag, send_sems, recv_sems, srcs, lands, _swap_plan, after)
        sums = [_pair_sum("pair_sum_%d_%s" % (l, n), p_, r_, chip_idx, core_idx)
                for n, p_, r_ in zip(names, parts, from_sibling)]
        lands = [lax.empty(s16.shape, BF16) for s16 in sums]
        send_sems, recv_sems, srcs, lands, token = _split_start("scatter_start_" + tag, sums, lands, _scatter_plan)
        scattered[l, grp] = (send_sems, recv_sems, srcs, lands, parts, from_sibling)
        return token

    def gate(l, grp, g):
        token = start_swap(l, grp, g)
        if len(swapping) > 1:
            token = start_scatter(after=token)
        return token[0, 0]

    def gates_for(l, points):
        return {grp: functools.partial(gate, l, grp) for grp in points}

    grads = [None] * DEPTH
    dh, grads[1] = _layer_bwd(lay, dh, layers[1], saved[1], t, gates=gates_for(1, ("mlp", "out", "in")))
    (grad_x, head), grads[0] = _layer_bwd(lay, dh, layers[0], saved[0], t, gates=gates_for(0, ("mlp", "out")),
                                          first=True)
    d_meta = head[PAD:ROW0]

    stack = lambda key: jnp.concatenate([grads[l][key] for l in range(DEPTH)], axis=0)
    b_forget_g = jnp.concatenate([grads[l]["b_forget_p"][:, :FOX_HEADS] for l in range(DEPTH)], axis=0)
    alpha_g = jnp.stack([grads[l]["w_alpha_p"][FOX_HEADS:FOX_HEADS + GLA_RANK] for l in range(DEPTH)])
    packed = _pack_small(d, d_meta, stack("norm_mix_g"), stack("gla_norm_g"), stack("norm_mlp_g"), dg_final,
                         stack("b_alpha"), b_forget_g, alpha_g)
    small_g = _unpack_small(d, _sum_devices(_all_gather_small(packed)))
    small_g["meta"] = lax.dynamic_slice_in_dim(small_g["meta"], chip * (d // N_CHIPS), d // N_CHIPS, axis=1)
    alpha_shard = lax.dynamic_slice_in_dim(small_g["w_alpha2"], chip * (lay.gk // N_CHIPS), lay.gk // N_CHIPS, axis=2)

    after = start_swap(0, "in", grads[0], after=small_g["final_norm_g"])
    while swapping:
        after = start_scatter(after)
    outs = {n: None for n in _BIG}

    def reduce_and_share(l, grps, after):
        tag = "%d_%s" % (l, grps[0])
        names, sums = [], []
        for grp in grps:
            send_sems, recv_sems, srcs, lands, parts, from_sibling = scattered[l, grp]
            _, lands = _split_wait("scatter_wait_%d_%s" % (l, grp), send_sems, recv_sems, srcs, lands, _scatter_plan,
                                 after)
            names += scatter_groups[grp]
            sums += [_chip_sum("chip_sum_%d_%s" % (l, n), p_, r_, landed, chip_idx, core_idx)
                     for n, p_, r_, landed in zip(scatter_groups[grp], parts, from_sibling, lands)]
        return l, tag, names, _exchange_start("share_start_" + tag, sums, _share_plan, after=sums[-1])

    def update(shared, after):
        l, tag, names, (send_sems, recv_sems, sums, _) = shared
        sums = _exchange_wait("share_wait_" + tag, send_sems, recv_sems, sums, _share_plan, after)
        for n, g in zip(names, sums):
            outs[n] = _adamw("adamw_%d_%s" % (l, n), w[n], g, m[n], v[n], layer=l, into=outs[n])
        return outs[names[-1]][0]

    shared_1 = reduce_and_share(1, ("mlp", "out", "in"), after)
    shared_0 = reduce_and_share(0, ("mlp", "out"), shared_1[3][3])
    after = update(shared_1, shared_0[3][3])
    shared_in = reduce_and_share(0, ("in",), after)
    after = update(shared_0, shared_in[3][3])
    update(shared_in, after)
    out_g, out_d, out_m, out_v = {}, {}, {}, {}
    for n in _BIG:
        out_g[n], out_d[n], out_m[n], out_v[n] = outs[n]
    out_g["w_alpha2"], out_d["w_alpha2"], out_m["w_alpha2"], out_v["w_alpha2"] = _adamw(
        "adamw_w_alpha2", w["w_alpha2"], alpha_shard, m["w_alpha2"], v["w_alpha2"])
    sm_w = dict(meta_tokens=meta_tokens, norm_mix_g=norm_mix_g, b_forget=b_forget, b_alpha=b_alpha,
                gla_norm_g=gla_norm_g, norm_mlp_g=norm_mlp_g, final_norm_g=final_norm_g)
    sm_m = dict(meta_tokens=m_meta_tokens, norm_mix_g=m_norm_mix_g, b_forget=m_b_forget, b_alpha=m_b_alpha,
                gla_norm_g=m_gla_norm_g, norm_mlp_g=m_norm_mlp_g, final_norm_g=m_final_norm_g)
    sm_v = dict(meta_tokens=v_meta_tokens, norm_mix_g=v_norm_mix_g, b_forget=v_b_forget, b_alpha=v_b_alpha,
                gla_norm_g=v_gla_norm_g, norm_mlp_g=v_norm_mlp_g, final_norm_g=v_final_norm_g)
    sm_g = dict(meta_tokens=small_g["meta"], norm_mix_g=small_g["norm_mix_g"], b_forget=small_g["b_forget"],
                b_alpha=small_g["b_alpha"], gla_norm_g=small_g["gla_norm_g"], norm_mlp_g=small_g["norm_mlp_g"],
                final_norm_g=small_g["final_norm_g"])
    names_small = list(sm_w)
    sizes = [sm_w[n].size for n in names_small]
    width = 512
    total = -(-sum(sizes) // (8 * width)) * (8 * width)

    def pack_flat(dct, fill):
        flat = jnp.concatenate([dct[n].reshape(-1) for n in names_small])
        return jnp.pad(flat, (0, total - flat.shape[0]), constant_values=fill).reshape(1, -1, width)

    res = _adamw("adamw_small", pack_flat(sm_w, 0.0), pack_flat(sm_g, 0.0), pack_flat(sm_m, 0.0), pack_flat(sm_v, 1.0))
    offs = [0]
    for sz in sizes:
        offs.append(offs[-1] + sz)
    for i, n in enumerate(names_small):
        out_g[n] = sm_g[n].reshape(sm_w[n].shape)
        out_d[n], out_m[n], out_v[n] = [r.reshape(-1)[offs[i]:offs[i + 1]].reshape(sm_w[n].shape) for r in res[1:]]

    order = ["meta_tokens", "norm_mix_g", "w_in", "b_forget", "w_alpha2", "b_alpha", "gla_norm_g", "w_o_fox",
             "w_o_gla", "w_out", "norm_mlp_g", "w_ff1", "w_ff2", "final_norm_g"]
    return (loss, grad_x[None], *[out_g[n] for n in order], *[out_d[n] for n in order],
            *[out_m[n] for n in order], *[out_v[n] for n in order])
```

```python
import functools

import numpy as np

import jax
import jax.numpy as jnp
from jax import lax
from jax.experimental import pallas as pl
from jax.experimental.pallas import tpu as pltpu

F32 = jnp.float32
BF16 = jnp.bfloat16

N_META = 16
PAD = 112
ROW0 = PAD + N_META
EPS = 1e-6
MASK_VALUE = -1e30
FOX_HEADS = 8
FOX_GROUP = 2
GLA_HEADS = 4
GLA_RANK = 16
GLA_TAU = 16.0
GLA_CHUNK = 64
DEPTH = 2
N_CHIPS = 4
N_DEV = 8

ADAM_LR = 0.001
ADAM_B1 = 0.9
ADAM_B2 = 0.999
ADAM_EPS = 1e-08
ADAM_WD = 0.01
ADAM_STEP = 10

LANES = 128
VMEM_LIMIT = 56 * 1024 * 1024
MESH = pl.DeviceIdType.MESH


def _pick(n, target, mult):
    best = None
    for d in range(mult, min(n, target) + 1, mult):
        if n % d == 0:
            best = d
    return n if best is None else best


def _params(sem=None):
    return pltpu.CompilerParams(dimension_semantics=sem, vmem_limit_bytes=VMEM_LIMIT)


def _bf(v):
    return v if v.dtype == BF16 else v.astype(BF16)


def _sigmoid(z):
    return 1.0 / (1.0 + jnp.exp(-z))


def _log_sigmoid(z):
    return jnp.minimum(z, 0.0) - jnp.log(1.0 + jnp.exp(-jnp.abs(z)))


def _split3(v):
    a = v.astype(BF16)
    r = v - a.astype(F32)
    b = r.astype(BF16)
    c = (r - b.astype(F32)).astype(BF16)
    return a, b, c


def _dot(a, b, dims):
    return lax.dot_general(a, b, (dims, ((), ())), preferred_element_type=F32)


NN = ((1,), (0,))
NT = ((1,), (1,))
TN = ((0,), (0,))


def _tri_dot(tri, v, dims=NN):
    a, b, c = _split3(v)
    return _dot(tri, a, dims) + _dot(tri, b, dims) + _dot(tri, c, dims)


def _mm(name, a, b, *, mode, m, n, k, b_c0=0, extras=(), epilogue=None, out_dtypes=(F32,),
        b_shards=1, out_shards=1, tm=1056, tn=1024, tk=2048):
    tm = _pick(m, tm, LANES if mode == "tn" else 16)
    tn = _pick(n // max(b_shards if mode == "nn" else 1, out_shards), tn, LANES)
    if mode == "tn":
        tk = _pick(k, 2112, 16)
    else:
        tk = _pick(k // (b_shards if mode == "nt" else 1), tk, LANES)
    assert b_c0 % (tk if mode == "nt" else tn) == 0 and (b_shards == 1 or b_c0 == 0)
    nk = k // tk
    if mode == "tn":
        a_spec = pl.BlockSpec((tk, tm), lambda i, j, kk: (kk, i))
    else:
        a_spec = pl.BlockSpec((tm, tk), lambda i, j, kk: (i, kk))
    if mode == "nt":
        dims = NT
        if b_shards > 1:
            per = (k // b_shards) // tk
            b_spec = pl.BlockSpec((None, tn, tk), lambda i, j, kk: (kk // per, j, kk % per))
        else:
            b_spec = pl.BlockSpec((tn, tk), lambda i, j, kk: (j, kk + b_c0 // tk))
    else:
        dims = NN if mode == "nn" else TN
        if b_shards > 1:
            per = (n // b_shards) // tn
            b_spec = pl.BlockSpec((None, tk, tn), lambda i, j, kk: (j // per, kk, j % per))
        else:
            b_spec = pl.BlockSpec((tk, tn), lambda i, j, kk: (kk, j + b_c0 // tn))
    ex_specs = [pl.BlockSpec((tm, tn), lambda i, j, kk: (i, j)) for _ in extras]
    if out_shards > 1:
        oper = (n // out_shards) // tn
        out_specs = [pl.BlockSpec((None, tm, tn), lambda i, j, kk: (j // oper, i, j % oper)) for _ in out_dtypes]
        out_shape = [jax.ShapeDtypeStruct((out_shards, m, n // out_shards), dt) for dt in out_dtypes]
    else:
        out_specs = [pl.BlockSpec((tm, tn), lambda i, j, kk: (i, j)) for _ in out_dtypes]
        out_shape = [jax.ShapeDtypeStruct((m, n), dt) for dt in out_dtypes]
    n_ex = len(extras)
    n_out = len(out_dtypes)

    def finish(acc, ex_refs, out_refs):
        vals = (acc,) if epilogue is None else epilogue(acc, *[r[...] for r in ex_refs])
        for r, v in zip(out_refs, vals):
            r[...] = v.astype(r.dtype)

    def body(a_ref, b_ref, *rest):
        ex_refs = rest[:n_ex]
        out_refs = rest[n_ex:n_ex + n_out]
        prod = _dot(_bf(a_ref[...]), _bf(b_ref[...]), dims)
        if nk == 1:
            finish(prod, ex_refs, out_refs)
            return
        acc_ref = rest[n_ex + n_out]
        kk = pl.program_id(2)

        @pl.when(kk == 0)
        def _():
            acc_ref[...] = prod

        @pl.when((kk > 0) & (kk < nk - 1))
        def _():
            acc_ref[...] += prod

        @pl.when(kk == nk - 1)
        def _():
            finish(acc_ref[...] + prod, ex_refs, out_refs)

    outs = pl.pallas_call(
        body,
        name=name,
        grid=(m // tm, n // tn, nk),
        in_specs=[a_spec, b_spec] + ex_specs,
        out_specs=out_specs,
        out_shape=out_shape,
        scratch_shapes=[pltpu.VMEM((tm, tn), F32)] if nk > 1 else [],
        compiler_params=_params(("parallel", "parallel", "arbitrary")),
    )(a, b, *extras)
    return outs[0] if n_out == 1 else outs


def _mm_nt2(name, a1, c1, a2, c2, b, prev, *, m, n, tm=528, tn=1024):
    k1, k2 = a1.shape[1], a2.shape[1]
    assert c1 % k1 == 0 and c2 % k2 == 0
    tm = _pick(m, tm, 16)
    tn = _pick(n, tn, LANES)

    def body(a1_ref, a2_ref, b1_ref, b2_ref, *rest):
        acc = _dot(a1_ref[...], b1_ref[...], NT) + _dot(a2_ref[...], b2_ref[...], NT)
        if prev is not None:
            acc = rest[0][...] + acc
        rest[-1][...] = acc

    tile = pl.BlockSpec((tm, tn), lambda i, j: (i, j))
    in_specs = [pl.BlockSpec((tm, k1), lambda i, j: (i, 0)), pl.BlockSpec((tm, k2), lambda i, j: (i, 0)),
                pl.BlockSpec((tn, k1), lambda i, j: (j, c1 // k1)), pl.BlockSpec((tn, k2), lambda i, j: (j, c2 // k2))]
    args = [a1, a2, b, b]
    if prev is not None:
        in_specs.append(tile)
        args.append(prev)
    return pl.pallas_call(
        body, name=name, grid=(m // tm, n // tn), in_specs=in_specs, out_specs=tile,
        out_shape=jax.ShapeDtypeStruct((m, n), F32),
        compiler_params=_params(("parallel", "parallel")),
    )(*args)


def _ew(name, fn, ins, outs, rows, tm):
    tm = _pick(rows, tm, 16)
    in_specs, args = [], []
    for spec in ins:
        if spec[0] == "tile":
            _, arr, width, c0 = spec
            assert c0 % width == 0
            in_specs.append(pl.BlockSpec((tm, width), functools.partial(lambda i, o: (i, o), o=c0 // width)))
        else:
            arr = spec[1]
            in_specs.append(pl.BlockSpec(arr.shape, lambda i: (0, 0)))
        args.append(arr)
    out_specs, out_shape = [], []
    for kind, dt, width in outs:
        if kind == "tile":
            out_specs.append(pl.BlockSpec((tm, width), lambda i: (i, 0)))
            out_shape.append(jax.ShapeDtypeStruct((rows, width), dt))
        else:
            out_specs.append(pl.BlockSpec((1, width), lambda i: (0, 0)))
            out_shape.append(jax.ShapeDtypeStruct((1, width), dt))
    n_in = len(ins)
    has_acc = any(o[0] == "acc" for o in outs)

    def body(*refs):
        i = pl.program_id(0)
        vals = fn(i * tm, *[r[...] for r in refs[:n_in]])
        for (kind, _, _), r, v in zip(outs, refs[n_in:], vals):
            if kind == "tile":
                r[...] = v.astype(r.dtype)
            else:
                @pl.when(i == 0)
                def _():
                    r[...] = jnp.zeros_like(r)

                r[...] += v.astype(r.dtype)

    res = pl.pallas_call(
        body,
        name=name,
        grid=(rows // tm,),
        in_specs=in_specs,
        out_specs=out_specs,
        out_shape=out_shape,
        compiler_params=_params(("arbitrary",) if has_acc else ("parallel",)),
    )(*args)
    return res[0] if len(outs) == 1 else res


def _row_ids(row0, tm):
    return row0 + lax.broadcasted_iota(jnp.int32, (tm, 1), 0)


def _colsum(v):
    return jnp.sum(v, axis=0, keepdims=True)


def _rms_fwd_t(name, h, g, t, d):
    tm = _pick(t, 384, LANES)

    def body(x_ref, g_ref, y_ref, yt_ref):
        x = x_ref[...]
        y = x * lax.rsqrt(jnp.mean(x * x, axis=-1, keepdims=True) + EPS) * g_ref[...]
        y_ref[...] = y.astype(BF16)
        yt_ref[...] = y.T.astype(BF16)

    return pl.pallas_call(
        body, name=name, grid=(t // tm,),
        in_specs=[pl.BlockSpec((tm, d), lambda i: (i, 0)), pl.BlockSpec((1, d), lambda i: (0, 0))],
        out_specs=[pl.BlockSpec((tm, d), lambda i: (i, 0)), pl.BlockSpec((d, tm), lambda i: (0, i))],
        out_shape=[jax.ShapeDtypeStruct((t, d), BF16), jax.ShapeDtypeStruct((d, t), BF16)],
        compiler_params=_params(("parallel",)),
    )(h, g)


def _rms_bwd(name, h, g, dy, dres, t, d):
    def fn(row0, x, gg, dyv, dr):
        r = lax.rsqrt(jnp.mean(x * x, axis=-1, keepdims=True) + EPS)
        xh = x * r
        dxh = dyv * gg
        dx = r * (dxh - xh * jnp.mean(dxh * xh, axis=-1, keepdims=True))
        out = jnp.where(_row_ids(row0, x.shape[0]) >= PAD, dr + dx, 0.0)
        return out, _colsum(dyv * xh)

    return _ew(name, fn, [("tile", h, d, 0), ("full", g), ("tile", dy, d, 0), ("tile", dres, d, 0)],
               [("tile", F32, d), ("acc", F32, d)], t, 264)


def _rms_bwd_x(name, h, g, dy, dres, t, d):
    tm = ROW0

    def body(x_ref, g_ref, dy_ref, dr_ref, dx_ref, head_ref, dg_ref):
        i = pl.program_id(0)
        x = x_ref[...]
        r = lax.rsqrt(jnp.mean(x * x, axis=-1, keepdims=True) + EPS)
        xh = x * r
        dxh = dy_ref[...] * g_ref[...]
        dx = r * (dxh - xh * jnp.mean(dxh * xh, axis=-1, keepdims=True))
        out = jnp.where(_row_ids(i * tm, tm) >= PAD, dr_ref[...] + dx, 0.0)

        @pl.when(i == 0)
        def _():
            head_ref[...] = out
            dg_ref[...] = jnp.zeros_like(dg_ref)

        dx_ref[...] = out
        dg_ref[...] += _colsum(dy_ref[...] * xh)

    tile = pl.BlockSpec((tm, d), lambda i: (i, 0))
    fixed = lambda shape: pl.BlockSpec(shape, lambda i: (0, 0))
    return pl.pallas_call(
        body, name=name, grid=(t // tm,),
        in_specs=[tile, fixed((1, d)), tile, tile],
        out_specs=[pl.BlockSpec((tm, d), lambda i: (jnp.maximum(i - 1, 0), 0)), fixed((tm, d)), fixed((1, d))],
        out_shape=[jax.ShapeDtypeStruct((t - ROW0, d), F32), jax.ShapeDtypeStruct((ROW0, d), F32),
                   jax.ShapeDtypeStruct((1, d), F32)],
        compiler_params=_params(("arbitrary",)),
    )(h, g, dy, dres)


def _loss_head(h, g, target_p, t, d):
    def fn(row0, x, gg, tgt):
        real = _row_ids(row0, x.shape[0]) >= ROW0
        r = lax.rsqrt(jnp.mean(x * x, axis=-1, keepdims=True) + EPS)
        xh = x * r
        err = jnp.where(real, xh * gg - tgt, 0.0)
        loss_rows = 0.5 * jnp.mean(err * err, axis=-1, keepdims=True)
        dyv = err * (1.0 / d)
        dxh = dyv * gg
        dx = r * (dxh - xh * jnp.mean(dxh * xh, axis=-1, keepdims=True))
        loss_part = jnp.sum(loss_rows, axis=0, keepdims=True) * jnp.ones((1, LANES), F32)
        return jnp.where(real, dx, 0.0), _colsum(dyv * xh), loss_part

    return _ew("loss_head", fn, [("tile", h, d, 0), ("full", g), ("tile", target_p, d, 0)],
               [("tile", F32, d), ("acc", F32, d), ("acc", F32, LANES)], t, 264)


def _merge_fwd(a_fox, a_gla, proj, c_gates, t, d):
    def fn(row0, af, ag, gates):
        gates = gates.astype(F32)
        return (_sigmoid(gates[:, :d]) * af + _sigmoid(gates[:, d:]) * ag,)

    return _ew("merge_fwd", fn, [("tile", a_fox, d, 0), ("tile", a_gla, d, 0), ("tile", proj, 2 * d, c_gates)],
               [("tile", BF16, d)], t, 264)


def _merge_bwd(dy, a_fox, a_gla, proj, c_gates, t, d):
    def fn(row0, dyv, af, ag, gates):
        gates = gates.astype(F32)
        sf = _sigmoid(gates[:, :d])
        sg = _sigmoid(gates[:, d:])
        dgates = jnp.concatenate([dyv * af * sf * (1.0 - sf), dyv * ag * sg * (1.0 - sg)], axis=1)
        return dyv * sf, dyv * sg, dgates

    return _ew("merge_bwd", fn,
               [("tile", dy, d, 0), ("tile", a_fox, d, 0), ("tile", a_gla, d, 0), ("tile", proj, 2 * d, c_gates)],
               [("tile", BF16, d), ("tile", BF16, d), ("tile", BF16, 2 * d)], t, 264)


def _fox_gate_fwd(small, b_forget_p, t):
    tb = _pick(t, 384, LANES)

    def body(s_ref, b_ref, c_ref, carry_ref):
        i = pl.program_id(0)

        @pl.when(i == 0)
        def _():
            carry_ref[...] = jnp.zeros_like(carry_ref)

        logf = _log_sigmoid(s_ref[...] + b_ref[...])
        logf = jnp.where(_row_ids(i * tb, tb) >= PAD, logf, 0.0)
        r = lax.broadcasted_iota(jnp.int32, (tb, tb), 0)
        c = lax.broadcasted_iota(jnp.int32, (tb, tb), 1)
        tri = (c <= r).astype(BF16)
        cs = _tri_dot(tri, logf) + carry_ref[...]
        c_ref[...] = cs
        carry_ref[...] = cs[tb - 1:tb, :]

    return pl.pallas_call(
        body, name="fox_gate_fwd", grid=(t // tb,),
        in_specs=[pl.BlockSpec((tb, LANES), lambda i: (i, 0)), pl.BlockSpec((1, LANES), lambda i: (0, 0))],
        out_specs=pl.BlockSpec((tb, LANES), lambda i: (i, 0)),
        out_shape=jax.ShapeDtypeStruct((t, LANES), F32),
        scratch_shapes=[pltpu.VMEM((1, LANES), F32)],
        compiler_params=_params(("arbitrary",)),
    )(small, b_forget_p)


def _fox_gate_bwd(dc, small, b_forget_p, dga, t):
    tb = _pick(t, 384, LANES)
    nb = t // tb

    def body(dc_ref, s_ref, b_ref, dga_ref, ds_ref, db_ref, carry_ref):
        i = pl.program_id(0)

        @pl.when(i == 0)
        def _():
            carry_ref[...] = jnp.zeros_like(carry_ref)
            db_ref[...] = jnp.zeros_like(db_ref)

        r = lax.broadcasted_iota(jnp.int32, (tb, tb), 0)
        c = lax.broadcasted_iota(jnp.int32, (tb, tb), 1)
        tri = (c >= r).astype(BF16)
        dlogf = _tri_dot(tri, dc_ref[...]) + carry_ref[...]
        carry_ref[...] = dlogf[0:1, :]
        z = s_ref[...] + b_ref[...]
        dff = dlogf * _sigmoid(-z)
        lane = lax.broadcasted_iota(jnp.int32, (tb, LANES), 1)
        keep = (_row_ids((nb - 1 - i) * tb, tb) >= PAD) & (lane < FOX_HEADS)
        dff = jnp.where(keep, dff, 0.0)
        ds_ref[...] = dff + dga_ref[...]
        db_ref[...] += _colsum(dff)

    rev = lambda i: (nb - 1 - i, 0)
    return pl.pallas_call(
        body, name="fox_gate_bwd", grid=(nb,),
        in_specs=[pl.BlockSpec((tb, LANES), rev), pl.BlockSpec((tb, LANES), rev),
                  pl.BlockSpec((1, LANES), lambda i: (0, 0)), pl.BlockSpec((tb, LANES), rev)],
        out_specs=[pl.BlockSpec((tb, LANES), rev), pl.BlockSpec((1, LANES), lambda i: (0, 0))],
        out_shape=[jax.ShapeDtypeStruct((t, LANES), F32), jax.ShapeDtypeStruct((1, LANES), F32)],
        scratch_shapes=[pltpu.VMEM((1, LANES), F32)],
        compiler_params=_params(("arbitrary",)),
    )(dc, small, b_forget_p, dga)


def _fox_pairs(nb, by_key):
    if by_key:
        pairs = [(qi, ki) for ki in range(nb) for qi in range(ki, nb)]
    else:
        pairs = [(qi, ki) for qi in range(nb) for ki in range(qi + 1)]
    return (jnp.asarray(np.array([p[0] for p in pairs], np.int32)),
            jnp.asarray(np.array([p[1] for p in pairs], np.int32)), len(pairs))


def _fox_specs(tb, fd, c_fq, c_fkv):
    gw = FOX_GROUP * fd
    q0, kv0 = c_fq // gw, c_fkv // (2 * gw)
    return dict(
        q=pl.BlockSpec((tb, gw), lambda g, p, qt, kt: (qt[p], q0 + g)),
        kv=pl.BlockSpec((tb, 2 * gw), lambda g, p, qt, kt: (kt[p], kv0 + g)),
        col=pl.BlockSpec((FOX_GROUP, tb, 1), lambda g, p, qt, kt: (g, qt[p], 0)),
        row=pl.BlockSpec((FOX_GROUP, 1, tb), lambda g, p, qt, kt: (g, 0, kt[p])),
        head=pl.BlockSpec((tb, gw), lambda g, p, qt, kt: (qt[p], g)),
        key_kv=pl.BlockSpec((tb, 2 * gw), lambda g, p, qt, kt: (kt[p], g)),
    )


def _fox_mask(qi, ki, tb):
    row = qi * tb + lax.broadcasted_iota(jnp.int32, (tb, tb), 0)
    col = ki * tb + lax.broadcasted_iota(jnp.int32, (tb, tb), 1)
    return (col <= row) & (col >= PAD)


def _fox_heads(q_ref, kv_ref, fd):
    return [(q_ref[:, hh * fd:(hh + 1) * fd], kv_ref[:, 2 * hh * fd:(2 * hh + 1) * fd],
             kv_ref[:, (2 * hh + 1) * fd:(2 * hh + 2) * fd]) for hh in range(FOX_GROUP)]


def _fox_fwd(proj, c_col, c_row, t, fd, c_fq, c_fkv):
    tb = _pick(t, 384, LANES)
    nb = t // tb
    scale = fd ** -0.5
    sp = _fox_specs(tb, fd, c_fq, c_fkv)
    qt, kt, npairs = _fox_pairs(nb, by_key=False)

    def body(qt_ref, kt_ref, q_ref, kv_ref, cq_ref, ck_ref, o_ref, lse_ref, m_ref, l_ref, acc_ref):
        p = pl.program_id(1)
        qi, ki = qt_ref[p], kt_ref[p]

        @pl.when(ki == 0)
        def _():
            m_ref[...] = jnp.full_like(m_ref, -jnp.inf)
            l_ref[...] = jnp.zeros_like(l_ref)
            acc_ref[...] = jnp.zeros_like(acc_ref)

        def update(masked):
            mask = _fox_mask(qi, ki, tb) if masked else None
            heads = _fox_heads(q_ref, kv_ref, fd)
            scores = [_dot(q, k, NT) for q, k, _ in heads]
            for hh, (q, k, v) in enumerate(heads):
                s = scores[hh] * scale + cq_ref[hh] - ck_ref[hh]
                if masked:
                    s = jnp.where(mask, s, MASK_VALUE)
                m_prev = m_ref[hh]
                m_new = jnp.maximum(m_prev, jnp.max(s, axis=-1, keepdims=True))
                alpha = jnp.exp(m_prev - m_new)
                pe = jnp.exp(s - m_new)
                l_ref[hh] = alpha * l_ref[hh] + jnp.sum(pe, axis=-1, keepdims=True)
                acc_ref[hh] = alpha * acc_ref[hh] + _dot(pe.astype(BF16), v, NN)
                m_ref[hh] = m_new

        edge = (ki == 0) | (ki == qi)
        pl.when(edge)(functools.partial(update, True))
        pl.when(jnp.logical_not(edge))(functools.partial(update, False))

        @pl.when(ki == qi)
        def _():
            real = _row_ids(qi * tb, tb) >= PAD
            for hh in range(FOX_GROUP):
                o_ref[:, hh * fd:(hh + 1) * fd] = jnp.where(real, acc_ref[hh] / l_ref[hh], 0.0)
                lse_ref[hh] = m_ref[hh] + jnp.log(l_ref[hh])

    return pl.pallas_call(
        body, name="fox_fwd",
        grid_spec=pltpu.PrefetchScalarGridSpec(
            num_scalar_prefetch=2, grid=(FOX_HEADS // FOX_GROUP, npairs),
            in_specs=[sp["q"], sp["kv"], sp["col"], sp["row"]],
            out_specs=[sp["head"], sp["col"]],
            scratch_shapes=[pltpu.VMEM((FOX_GROUP, tb, 1), F32), pltpu.VMEM((FOX_GROUP, tb, 1), F32),
                            pltpu.VMEM((FOX_GROUP, tb, fd), F32)]),
        out_shape=[jax.ShapeDtypeStruct((t, FOX_HEADS * fd), F32), jax.ShapeDtypeStruct((FOX_HEADS, t, 1), F32)],
        compiler_params=_params(("parallel", "arbitrary")),
    )(qt, kt, proj, proj, c_col, c_row)


def _fox_delta(o_fox, do_fox, t, fd):
    tb = _pick(t, 384, LANES)

    def body(o_ref, do_ref, out_ref):
        for h in range(FOX_HEADS):
            sl = slice(h * fd, (h + 1) * fd)
            out_ref[h] = jnp.sum(o_ref[:, sl] * do_ref[:, sl].astype(BF16).astype(F32), axis=-1, keepdims=True)

    w = FOX_HEADS * fd
    return pl.pallas_call(
        body, name="fox_delta", grid=(t // tb,),
        in_specs=[pl.BlockSpec((tb, w), lambda i: (i, 0)), pl.BlockSpec((tb, w), lambda i: (i, 0))],
        out_specs=pl.BlockSpec((FOX_HEADS, tb, 1), lambda i: (0, i, 0)),
        out_shape=jax.ShapeDtypeStruct((FOX_HEADS, t, 1), F32),
        compiler_params=_params(("parallel",)),
    )(o_fox, do_fox)


def _fox_bwd(proj, c_col, c_row, lse, delta, do_fox, t, fd, c_fq, c_fkv):
    tb = _pick(t, 384, LANES)
    nb = t // tb
    scale = fd ** -0.5
    sp = _fox_specs(tb, fd, c_fq, c_fkv)
    qt, kt, npairs = _fox_pairs(nb, by_key=True)
    gw = FOX_GROUP * fd

    def body(qt_ref, kt_ref, q_ref, kv_ref, cq_ref, ck_ref, lse_ref, dl_ref, do_ref, dq_ref, dkv_ref, dc_ref, dr_ref,
             dq_acc, dk_acc, dv_acc, dc_acc, dr_acc):
        p = pl.program_id(1)
        qi, ki = qt_ref[p], kt_ref[p]

        @pl.when(p == 0)
        def _():
            dq_acc[...] = jnp.zeros_like(dq_acc)
            dr_acc[...] = jnp.zeros_like(dr_acc)

        @pl.when(qi == ki)
        def _():
            dk_acc[...] = jnp.zeros_like(dk_acc)
            dv_acc[...] = jnp.zeros_like(dv_acc)
            dc_acc[...] = jnp.zeros_like(dc_acc)

        rows = pl.ds(pl.multiple_of(qi * tb, LANES), tb)

        def update(masked):
            mask = _fox_mask(qi, ki, tb) if masked else None
            for hh, (q, k, v) in enumerate(_fox_heads(q_ref, kv_ref, fd)):
                do = _bf(do_ref[:, hh * fd:(hh + 1) * fd])
                s = _dot(q, k, NT) * scale + cq_ref[hh] - ck_ref[hh]
                if masked:
                    s = jnp.where(mask, s, MASK_VALUE)
                pr = jnp.exp(s - lse_ref[hh])
                dp = _dot(do, v, NT)
                ds = pr * (dp - dl_ref[hh])
                ds16 = ds.astype(BF16)
                dv_acc[hh] += _dot(pr.astype(BF16), do, TN)
                dk_acc[hh] += _dot(ds16, q, TN)
                dc_acc[hh] += _colsum(ds)
                dr_acc[hh, rows, :] += jnp.sum(ds, axis=-1, keepdims=True)
                dq_acc[hh, rows, :] += _dot(ds16, k, NN)

        edge = (ki == 0) | (ki == qi)
        pl.when(edge)(functools.partial(update, True))
        pl.when(jnp.logical_not(edge))(functools.partial(update, False))

        @pl.when(qi == nb - 1)
        def _():
            for hh in range(FOX_GROUP):
                dkv_ref[:, 2 * hh * fd:(2 * hh + 1) * fd] = (dk_acc[hh] * scale).astype(dkv_ref.dtype)
                dkv_ref[:, (2 * hh + 1) * fd:(2 * hh + 2) * fd] = dv_acc[hh].astype(dkv_ref.dtype)
                dc_ref[hh] = -dc_acc[hh]

        @pl.when(p == npairs - 1)
        def _():
            for hh in range(FOX_GROUP):
                dq_ref[:, hh * fd:(hh + 1) * fd] = (dq_acc[hh] * scale).astype(dq_ref.dtype)
            dr_ref[...] = dr_acc[...]

    return pl.pallas_call(
        body, name="fox_bwd",
        grid_spec=pltpu.PrefetchScalarGridSpec(
            num_scalar_prefetch=2, grid=(FOX_HEADS // FOX_GROUP, npairs),
            in_specs=[sp["q"], sp["kv"], sp["col"], sp["row"], sp["col"], sp["col"], sp["head"]],
            out_specs=[pl.BlockSpec((t, gw), lambda g, p, qt, kt: (0, g)), sp["key_kv"], sp["row"],
                       pl.BlockSpec((FOX_GROUP, t, 1), lambda g, p, qt, kt: (g, 0, 0))],
            scratch_shapes=[pltpu.VMEM((FOX_GROUP, t, fd), F32), pltpu.VMEM((FOX_GROUP, tb, fd), F32),
                            pltpu.VMEM((FOX_GROUP, tb, fd), F32), pltpu.VMEM((FOX_GROUP, 1, tb), F32),
                            pltpu.VMEM((FOX_GROUP, t, 1), F32)]),
        out_shape=[jax.ShapeDtypeStruct((t, FOX_HEADS * fd), BF16), jax.ShapeDtypeStruct((t, 2 * FOX_HEADS * fd), BF16),
                   jax.ShapeDtypeStruct((FOX_HEADS, 1, t), F32), jax.ShapeDtypeStruct((FOX_HEADS, t, 1), F32)],
        compiler_params=_params(("parallel", "arbitrary")),
    )(qt, kt, proj, proj, c_col, c_row, lse, delta, do_fox)


def _gla_gate_fwd(small, w_alpha_p, b_alpha, t, gk):
    def fn(row0, s, w, b):
        z = _dot(s.astype(BF16), w, NN) + b
        return (jnp.where(_row_ids(row0, s.shape[0]) >= PAD, _log_sigmoid(z) * (1.0 / GLA_TAU), 0.0),)

    return _ew("gla_gate_fwd", fn, [("tile", small, LANES, 0), ("full", w_alpha_p), ("full", b_alpha)],
               [("tile", F32, gk)], t, 264)


def _gla_gate_bwd(dglog, small, w_alpha_p, b_alpha, t, gk):
    def fn(row0, dg, s, w, b):
        z = _dot(s.astype(BF16), w, NN) + b
        dz = jnp.where(_row_ids(row0, s.shape[0]) >= PAD, dg * (1.0 / GLA_TAU) * _sigmoid(-z), 0.0)
        return dz, _colsum(dz)

    return _ew("gla_gate_bwd", fn,
               [("tile", dglog, gk, 0), ("tile", small, LANES, 0), ("full", w_alpha_p), ("full", b_alpha)],
               [("tile", BF16, gk), ("acc", F32, gk)], t, 264)


def _gla_chunk(q, k, g, scale, cs):
    r = lax.broadcasted_iota(jnp.int32, (cs, cs), 0)
    c = lax.broadcasted_iota(jnp.int32, (cs, cs), 1)
    causal = c <= r
    b = _tri_dot(causal.astype(BF16), g)
    bl = b[cs - 1:cs, :]
    eb, einv, eend = jnp.exp(b), jnp.exp(-b), jnp.exp(bl - b)
    qd = q.astype(F32) * scale * eb
    kf = k.astype(F32)
    return causal, (eb, einv, eend), bl, qd, kf * einv, kf * eend


def _gla_fwd(proj, glog, t, dk, dv, c_q, c_k, c_v):
    cs = GLA_CHUNK
    nc = t // cs
    wk, wv = GLA_HEADS * dk, GLA_HEADS * dv
    scale = dk ** -0.5

    def body(q_ref, k_ref, v_ref, g_ref, o_ref, sp_ref, st_ref):
        @pl.when(pl.program_id(0) == 0)
        def _():
            st_ref[...] = jnp.zeros_like(st_ref)

        key = lambda h: slice(h * dk, (h + 1) * dk)
        chunks = [_gla_chunk(q_ref[:, key(h)], k_ref[:, key(h)], g_ref[:, key(h)], scale, cs)
                  for h in range(GLA_HEADS)]
        for h in range(GLA_HEADS):
            vs = slice(h * dv, (h + 1) * dv)
            v = v_ref[:, vs]
            causal, _, bl, qd, ki, ke = chunks[h]
            st = st_ref[h]
            sp_ref[h] = st
            a = jnp.where(causal, _dot(qd.astype(BF16), ki.astype(BF16), NT), 0.0)
            o_ref[:, vs] = _dot(a.astype(BF16), v, NN) + _dot(qd.astype(BF16), st.astype(BF16), NT)
            st_ref[h] = st * jnp.exp(bl) + _dot(v, ke.astype(BF16), TN)

    return pl.pallas_call(
        body, name="gla_fwd", grid=(nc,),
        in_specs=[pl.BlockSpec((cs, wk), lambda n: (n, c_q // wk)), pl.BlockSpec((cs, wk), lambda n: (n, c_k // wk)),
                  pl.BlockSpec((cs, wv), lambda n: (n, c_v // wv)), pl.BlockSpec((cs, wk), lambda n: (n, 0))],
        out_specs=[pl.BlockSpec((cs, wv), lambda n: (n, 0)),
                   pl.BlockSpec((None, GLA_HEADS, dv, dk), lambda n: (n, 0, 0, 0))],
        out_shape=[jax.ShapeDtypeStruct((t, wv), F32), jax.ShapeDtypeStruct((nc, GLA_HEADS, dv, dk), F32)],
        scratch_shapes=[pltpu.VMEM((GLA_HEADS, dv, dk), F32)],
        compiler_params=_params(("arbitrary",)),
    )(proj, proj, proj, glog)


def _gla_bwd(proj, glog, s_prev, do_raw, t, dk, dv, c_q, c_k, c_v):
    cs = GLA_CHUNK
    nc = t // cs
    wk, wv = GLA_HEADS * dk, GLA_HEADS * dv
    scale = dk ** -0.5

    def body(q_ref, k_ref, v_ref, g_ref, sp_ref, do_ref, dq_ref, dk_ref, dv_ref, dg_ref, dst_ref):
        @pl.when(pl.program_id(0) == 0)
        def _():
            dst_ref[...] = jnp.zeros_like(dst_ref)

        key = lambda h: slice(h * dk, (h + 1) * dk)
        chunks = [_gla_chunk(q_ref[:, key(h)], k_ref[:, key(h)], g_ref[:, key(h)], scale, cs)
                  for h in range(GLA_HEADS)]
        for h in range(GLA_HEADS):
            ks, vs = key(h), slice(h * dv, (h + 1) * dv)
            v = v_ref[:, vs]
            do = do_ref[:, vs].astype(BF16)
            causal, (eb, einv, eend), bl, qd, ki, ke = chunks[h]
            qd16, ki16, ke16 = qd.astype(BF16), ki.astype(BF16), ke.astype(BF16)
            st = sp_ref[h]
            dst = dst_ref[h]
            dst16 = dst.astype(BF16)
            a = jnp.where(causal, _dot(qd16, ki16, NT), 0.0).astype(BF16)
            da = jnp.where(causal, _dot(do, v, NT), 0.0).astype(BF16)
            dvv = _dot(a, do, TN) + _dot(ke16, dst16, NT)
            dqd = _dot(da, ki16, NN) + _dot(do, st.astype(BF16), NN)
            dki = _dot(da, qd16, TN)
            dke = _dot(v, dst16, NN)
            dl = jnp.exp(bl)
            ddl = _colsum(dst * st)
            dst_ref[h] = dst * dl + _dot(do, qd16, TN)
            dq_ref[:, ks] = (dqd * eb * scale).astype(dq_ref.dtype)
            dk_ref[:, ks] = (dki * einv + dke * eend).astype(dk_ref.dtype)
            dv_ref[:, vs] = dvv.astype(dv_ref.dtype)
            db = dqd * qd - dki * ki - dke * ke
            db_last = _colsum(dke * ke) + ddl * dl
            r = lax.broadcasted_iota(jnp.int32, (cs, cs), 0)
            c = lax.broadcasted_iota(jnp.int32, (cs, cs), 1)
            dg_ref[:, ks] = _tri_dot((c >= r).astype(BF16), db) + db_last

    rev = lambda f: (lambda n: f(nc - 1 - n))
    return pl.pallas_call(
        body, name="gla_bwd", grid=(nc,),
        in_specs=[pl.BlockSpec((cs, wk), rev(lambda n: (n, c_q // wk))), pl.BlockSpec((cs, wk), rev(lambda n: (n, c_k // wk))),
                  pl.BlockSpec((cs, wv), rev(lambda n: (n, c_v // wv))), pl.BlockSpec((cs, wk), rev(lambda n: (n, 0))),
                  pl.BlockSpec((None, GLA_HEADS, dv, dk), rev(lambda n: (n, 0, 0, 0))),
                  pl.BlockSpec((cs, wv), rev(lambda n: (n, 0)))],
        out_specs=[pl.BlockSpec((cs, wk), rev(lambda n: (n, 0))), pl.BlockSpec((cs, wk), rev(lambda n: (n, 0))),
                   pl.BlockSpec((cs, wv), rev(lambda n: (n, 0))), pl.BlockSpec((cs, wk), rev(lambda n: (n, 0)))],
        out_shape=[jax.ShapeDtypeStruct((t, wk), BF16), jax.ShapeDtypeStruct((t, wk), BF16),
                   jax.ShapeDtypeStruct((t, wv), BF16), jax.ShapeDtypeStruct((t, wk), F32)],
        scratch_shapes=[pltpu.VMEM((GLA_HEADS, dv, dk), F32)],
        compiler_params=_params(("arbitrary",)),
    )(proj, proj, proj, glog, s_prev, do_raw)


def _gla_post_fwd(o_raw, proj, gn, t, dv, c_gr):
    w = GLA_HEADS * dv

    def fn(row0, o, gr, g):
        gr = gr.astype(F32)
        outs = []
        for h in range(GLA_HEADS):
            oh = o[:, h * dv:(h + 1) * dv]
            outs.append(oh * lax.rsqrt(jnp.mean(oh * oh, axis=-1, keepdims=True) + EPS))
        on = jnp.concatenate(outs, axis=1) * g
        return (on * (gr * _sigmoid(gr)),)

    return _ew("gla_post_fwd", fn, [("tile", o_raw, w, 0), ("tile", proj, w, c_gr), ("full", gn)],
               [("tile", BF16, w)], t, 264)


def _gla_post_bwd(o_raw, proj, gn, do_gla, t, dv, c_gr):
    w = GLA_HEADS * dv

    def fn(row0, o, gr, g, do):
        gr = gr.astype(F32)
        sg = _sigmoid(gr)
        don = do * (gr * sg)
        ohs, dos = [], []
        for h in range(GLA_HEADS):
            sl = slice(h * dv, (h + 1) * dv)
            oh = o[:, sl]
            r = lax.rsqrt(jnp.mean(oh * oh, axis=-1, keepdims=True) + EPS)
            xh = oh * r
            dxh = don[:, sl] * g[:, sl]
            ohs.append(xh)
            dos.append(r * (dxh - xh * jnp.mean(dxh * xh, axis=-1, keepdims=True)))
        xh = jnp.concatenate(ohs, axis=1)
        dgr = do * (xh * g) * (sg * (1.0 + gr * (1.0 - sg)))
        return jnp.concatenate(dos, axis=1), dgr, _colsum(don * xh)

    return _ew("gla_post_bwd", fn,
               [("tile", o_raw, w, 0), ("tile", proj, w, c_gr), ("full", gn), ("tile", do_gla, w, 0)],
               [("tile", F32, w), ("tile", BF16, w), ("acc", F32, w)], t, 264)


def _adamw(name, w, g, m, v, layer=None, into=None):
    nl, rows, cols = w.shape
    tm = _pick(rows, max(8, (512 * 1024) // max(cols, 1) // 8 * 8), 8)

    def body(w_ref, g_ref, m_ref, v_ref, *rest):
        go_ref, d_ref, nm_ref, nv_ref = rest[-4:]
        gg = g_ref[...]
        nm = ADAM_B1 * m_ref[...] + (1.0 - ADAM_B1) * gg
        nv = ADAM_B2 * v_ref[...] + (1.0 - ADAM_B2) * (gg * gg)
        m_hat = nm / (1.0 - ADAM_B1 ** ADAM_STEP)
        v_hat = nv / (1.0 - ADAM_B2 ** ADAM_STEP)
        go_ref[...] = gg
        d_ref[...] = -ADAM_LR * (m_hat / (jnp.sqrt(v_hat) + ADAM_EPS) + ADAM_WD * w_ref[...])
        nm_ref[...] = nm
        nv_ref[...] = nv

    out_shape = [jax.ShapeDtypeStruct((nl, rows, cols), F32)] * 4
    if layer is None:
        spec = pl.BlockSpec((None, tm, cols), lambda l, i: (l, i, 0))
        return pl.pallas_call(
            body, name=name, grid=(nl, rows // tm), in_specs=[spec] * 4, out_specs=[spec] * 4, out_shape=out_shape,
            compiler_params=_params(("parallel", "parallel")),
        )(w, g, m, v)
    spec = pl.BlockSpec((None, tm, cols), lambda i: (layer, i, 0))
    in_specs = [spec, pl.BlockSpec((tm, cols), lambda i: (i, 0)), spec, spec]
    args, aliases = [w, g, m, v], {}
    if into is not None:
        in_specs += [pl.BlockSpec(memory_space=pl.ANY)] * 4
        args += list(into)
        aliases = {4 + k: k for k in range(4)}
    return pl.pallas_call(
        body, name=name, grid=(rows // tm,), in_specs=in_specs, out_specs=[spec] * 4, out_shape=out_shape,
        input_output_aliases=aliases, compiler_params=_params(("parallel",)),
    )(*args)


def _me():
    return lax.axis_index("x"), lax.axis_index("y"), lax.axis_index("c")


def _hbm_specs(n):
    return [pl.BlockSpec(memory_space=pl.ANY)] * n


_HBM = pl.BlockSpec(memory_space=pltpu.HBM)
_SEM = pl.BlockSpec(memory_space=pltpu.SEMAPHORE)
_EFFECT = pltpu.SideEffectType.DATAFLOW_SIDE_EFFECTING
N_PEERS = N_CHIPS - 1


def _other_chips(x, y):
    return [(1 - x, y), (x, 1 - y), (1 - x, 1 - y)]


def _split_copies(plan, src, land, send_sems, recv_sems):
    me = _me()
    copies = []
    for i in range(len(src)):
        for j, (s, d, peer) in enumerate(plan(src[i], land[i], me)):
            k = plan.copies * i + j
            copies.append(pltpu.make_async_remote_copy(
                src_ref=s, dst_ref=d, send_sem=send_sems.at[k], recv_sem=recv_sems.at[k], device_id=peer,
                device_id_type=MESH))
    return copies


def _split_start(name, srcs, lands, plan, after=None):
    n = len(srcs)
    extra = [] if after is None else [after]

    def body(*refs):
        src, land = refs[:n], refs[n:2 * n]
        send_sems, recv_sems = refs[2 * n + len(extra)], refs[2 * n + len(extra) + 1]
        token = refs[-1]
        for cp in _split_copies(plan, src, land, send_sems, recv_sems):
            cp.start()
        token[...] = jnp.zeros_like(token)

    out_shape = ([pltpu.SemaphoreType.DMA((plan.copies * n,)), pltpu.SemaphoreType.DMA((plan.copies * n,))]
                 + [pltpu.HBM(a.shape, a.dtype) for a in list(srcs) + list(lands)]
                 + [jax.ShapeDtypeStruct((8, LANES), F32)])
    res = pl.pallas_call(
        body, name=name, out_shape=out_shape,
        in_specs=[_HBM] * (2 * n) + [pl.BlockSpec(memory_space=pl.ANY)] * len(extra),
        out_specs=[_SEM, _SEM] + [_HBM] * (2 * n) + [pl.BlockSpec(memory_space=pltpu.VMEM)],
        input_output_aliases={i: 2 + i for i in range(2 * n)},
        compiler_params=pltpu.CompilerParams(has_side_effects=_EFFECT),
    )(*[pltpu.with_memory_space_constraint(a, pltpu.HBM) for a in list(srcs) + list(lands)], *extra)
    return res[0], res[1], res[2:2 + n], res[2 + n:2 + 2 * n], res[-1]


def _split_wait(name, send_sems, recv_sems, srcs, lands, plan, after):
    n = len(srcs)

    def body(*refs):
        src, land = refs[:n], refs[n:2 * n]
        s_sems, r_sems = refs[2 * n], refs[2 * n + 1]
        for cp in _split_copies(plan, src, land, s_sems, r_sems):
            cp.wait_send()
            cp.wait_recv()

    res = pl.pallas_call(
        body, name=name, out_shape=[pltpu.HBM(a.shape, a.dtype) for a in list(srcs) + list(lands)],
        in_specs=[_HBM] * (2 * n) + [_SEM, _SEM, pl.BlockSpec(memory_space=pl.ANY)], out_specs=[_HBM] * (2 * n),
        input_output_aliases={i: i for i in range(2 * n)},
        compiler_params=pltpu.CompilerParams(has_side_effects=_EFFECT),
    )(*srcs, *lands, send_sems, recv_sems, after)
    return res[:n], res[n:]


def _exchange_start(name, bufs, plan, after):
    n = len(bufs)

    def body(*refs):
        buf = refs[:n]
        for cp in _split_copies(plan, buf, buf, refs[n + 1], refs[n + 2]):
            cp.start()
        refs[-1][...] = jnp.zeros_like(refs[-1])

    out_shape = ([pltpu.SemaphoreType.DMA((plan.copies * n,)), pltpu.SemaphoreType.DMA((plan.copies * n,))]
                 + [pltpu.HBM(a.shape, a.dtype) for a in bufs] + [jax.ShapeDtypeStruct((8, LANES), F32)])
    res = pl.pallas_call(
        body, name=name, out_shape=out_shape, in_specs=[_HBM] * n + [pl.BlockSpec(memory_space=pl.ANY)],
        out_specs=[_SEM, _SEM] + [_HBM] * n + [pl.BlockSpec(memory_space=pltpu.VMEM)],
        input_output_aliases={i: 2 + i for i in range(n)},
        compiler_params=pltpu.CompilerParams(has_side_effects=_EFFECT),
    )(*[pltpu.with_memory_space_constraint(a, pltpu.HBM) for a in bufs], after)
    return res[0], res[1], res[2:2 + n], res[-1]


def _exchange_wait(name, send_sems, recv_sems, bufs, plan, after):
    n = len(bufs)

    def body(*refs):
        buf = refs[:n]
        for cp in _split_copies(plan, buf, buf, refs[n], refs[n + 1]):
            cp.wait_send()
            cp.wait_recv()

    return pl.pallas_call(
        body, name=name, out_shape=[pltpu.HBM(a.shape, a.dtype) for a in bufs],
        in_specs=[_HBM] * n + [_SEM, _SEM, pl.BlockSpec(memory_space=pl.ANY)], out_specs=[_HBM] * n,
        input_output_aliases={i: i for i in range(n)},
        compiler_params=pltpu.CompilerParams(has_side_effects=_EFFECT),
    )(*bufs, send_sems, recv_sems, after)


def _forward_plan(src, land, me):
    x, y, c = me
    rows = _half(src.shape[1], c)
    return [(src.at[2 * px + py, rows], land.at[2 * px + py, rows], (x, y, 1 - c)) for px, py in _other_chips(x, y)]


def _share_plan(src, land, me):
    x, y, c = me
    rows = _half(src.shape[0], c)
    return [(src.at[rows], land.at[rows], (x, y, 1 - c))]


_forward_plan.copies = N_CHIPS - 1
_share_plan.copies = 1


def _half(ref_rows, c):
    half = ref_rows // 2
    return pl.ds(c * half, half)


def _gather_plan(src, land, me):
    x, y, c = me
    rows = _half(src.shape[0], c)
    return [(src.at[rows], land.at[2 * x + y, rows], (px, py, c)) for px, py in _other_chips(x, y)]


def _scatter_plan(src, land, me):
    x, y, c = me
    return [(src.at[2 * px + py], land.at[2 * x + y], (px, py, c)) for px, py in _other_chips(x, y)]


def _swap_plan(src, land, me):
    x, y, c = me
    return [(src.at[:, _half(src.shape[1], 1 - c)], land, (x, y, 1 - c))]


_gather_plan.copies = N_PEERS
_scatter_plan.copies = N_PEERS
_swap_plan.copies = 1


def _all_gather_small(v):
    def body(v_ref, out_ref, send_sems, recv_sems, local_sem):
        x, y, c = _me()
        mine = pltpu.make_async_copy(v_ref, out_ref.at[4 * x + 2 * y + c], local_sem)
        mine.start()
        copies = []
        for k in range(1, N_DEV):
            peer = (x ^ ((k >> 2) & 1), y ^ ((k >> 1) & 1), c ^ (k & 1))
            cp = pltpu.make_async_remote_copy(
                src_ref=v_ref, dst_ref=out_ref.at[4 * x + 2 * y + c], send_sem=send_sems.at[k - 1],
                recv_sem=recv_sems.at[k - 1], device_id=peer, device_id_type=MESH)
            cp.start()
            copies.append(cp)
        for cp in copies:
            cp.wait_recv()
        for cp in copies:
            cp.wait_send()
        mine.wait()

    return pl.pallas_call(
        body, name="all_gather_small", in_specs=_hbm_specs(1), out_specs=pl.BlockSpec(memory_space=pl.ANY),
        out_shape=jax.ShapeDtypeStruct((N_DEV,) + v.shape, v.dtype),
        scratch_shapes=[pltpu.SemaphoreType.DMA((N_DEV - 1,)), pltpu.SemaphoreType.DMA((N_DEV - 1,)),
                        pltpu.SemaphoreType.DMA],
    )(v)


def _pair_sum(name, own, recv, chip, core):
    nch, half, cdim = recv.shape
    tm = _pick(half, max(8, (512 * 1024) // cdim // 16 * 16), 16)
    nt = half // tm

    def body(c_ref, k_ref, a_ref, b_ref, s16_ref):
        s16_ref[...] = (a_ref[...] + b_ref[...]).astype(BF16)

    other = lambda j, c: (c[0] + 1 + j) % nch
    spec = pl.BlockSpec((None, tm, cdim), lambda j, i, c, k: (other(j, c), i, 0))
    return pl.pallas_call(
        body, name=name,
        grid_spec=pltpu.PrefetchScalarGridSpec(
            num_scalar_prefetch=2, grid=(nch - 1, nt),
            in_specs=[pl.BlockSpec((None, tm, cdim), lambda j, i, c, k: (other(j, c), k[0] * nt + i, 0)), spec],
            out_specs=spec),
        out_shape=jax.ShapeDtypeStruct((nch, half, cdim), BF16),
        compiler_params=_params(("parallel", "parallel")),
    )(chip, core, own, recv)


def _chip_sum(name, own, recv, landed, chip, core):
    nch, half, cdim = recv.shape
    tm = _pick(half, max(8, (512 * 1024) // cdim // 16 * 16), 16)
    nt = half // tm

    def body(c_ref, k_ref, own_ref, recv_ref, *rest):
        landed_refs, out_ref = rest[:nch], rest[-1]
        me = c_ref[0]
        mine = own_ref[...] + recv_ref[...]
        acc = None
        for j in range(nch):
            term = jnp.where(me == j, mine, landed_refs[j][...].astype(F32))
            acc = term if acc is None else acc + term
        out_ref[...] = acc

    landed_specs = [pl.BlockSpec((None, tm, cdim), functools.partial(lambda i, c, k, j: (j, i, 0), j=j))
                    for j in range(nch)]
    return pl.pallas_call(
        body, name=name,
        grid_spec=pltpu.PrefetchScalarGridSpec(
            num_scalar_prefetch=2, grid=(nt,),
            in_specs=[pl.BlockSpec((None, tm, cdim), lambda i, c, k: (c[0], k[0] * nt + i, 0)),
                      pl.BlockSpec((None, tm, cdim), lambda i, c, k: (c[0], i, 0))] + landed_specs,
            out_specs=pl.BlockSpec((tm, cdim), lambda i, c, k: (k[0] * nt + i, 0))),
        out_shape=jax.ShapeDtypeStruct((2 * half, cdim), F32),
        compiler_params=_params(("parallel",)),
    )(chip, core, own, recv, *([landed] * nch))


def _sum_devices(gathered):
    _, r, cdim = gathered.shape

    def body(g_ref, o_ref):
        acc = g_ref[0]
        for k in range(1, N_DEV):
            acc = acc + g_ref[k]
        o_ref[...] = acc

    return pl.pallas_call(
        body, name="sum_devices", out_shape=jax.ShapeDtypeStruct((r, cdim), F32),
        compiler_params=_params(),
    )(gathered)


class _Layout:
    def __init__(self, d):
        self.d = d
        self.fw = d // 2
        self.fd = self.fw // FOX_HEADS
        self.gk = d // 2
        self.gv = d
        self.dk = self.gk // GLA_HEADS
        self.dv = self.gv // GLA_HEADS
        self.c_fq = 0
        self.c_gq = self.fw
        self.c_gv = self.c_gq + self.gk
        self.c_gr = self.c_gv + self.gv
        self.c_fkv = self.c_gr + self.gv
        self.c_gates = self.c_fkv + 2 * self.fw
        self.c_gk = self.c_gates + 2 * d
        self.c_small = self.c_gk + self.gk
        self.n_main = self.c_small
        self.n_p = self.c_small + LANES
        self.o_fk = self.fw
        self.o_fv = 2 * self.fw
        self.o_ff = 3 * self.fw
        self.o_gq = self.o_ff + FOX_HEADS
        self.o_gk = self.o_gq + self.gk
        self.o_gv = self.o_gk + self.gk
        self.o_gr = self.o_gv + self.gv
        self.o_ga = self.o_gr + self.gv
        self.o_gf = self.o_ga + GLA_RANK
        self.o_gg = self.o_gf + d
        self.n_orig = self.o_gg + d

    def to_p(self, shards):
        per = self.n_orig // N_CHIPS
        ranges = [(0, self.fw), (self.o_gq, self.gk), (self.o_gv, self.gv), (self.o_gr, self.gv)]
        for h in range(FOX_HEADS):
            ranges += [(self.o_fk + h * self.fd, self.fd), (self.o_fv + h * self.fd, self.fd)]
        ranges += [(self.o_gf, 2 * self.d), (self.o_gk, self.gk), (self.o_ff, FOX_HEADS), (self.o_ga, GLA_RANK)]
        pieces = []
        for a, width in ranges:
            for j in range(a // per, (a + width - 1) // per + 1):
                lo, hi = max(a, j * per), min(a + width, (j + 1) * per)
                pieces.append(shards[j][:, lo - j * per:hi - j * per])
        pieces.append(jnp.zeros((shards.shape[1], LANES - FOX_HEADS - GLA_RANK), shards.dtype))
        return jnp.concatenate(pieces, axis=1)

    def from_segments(self, seg):
        per = self.n_orig // N_CHIPS
        fd = self.fd
        atoms = [("fq", 0, self.fw)]
        atoms += [("fkv", 2 * h * fd, fd) for h in range(FOX_HEADS)]
        atoms += [("fkv", (2 * h + 1) * fd, fd) for h in range(FOX_HEADS)]
        atoms += [("small", 0, FOX_HEADS), ("gq", 0, self.gk), ("gk", 0, self.gk), ("gv", 0, self.gv),
                  ("gr", 0, self.gv), ("small", FOX_HEADS, GLA_RANK), ("gates", 0, 2 * self.d)]
        shards = [[] for _ in range(N_CHIPS)]
        pos = 0
        for name, c0, width in atoms:
            for j in range(pos // per, (pos + width - 1) // per + 1):
                lo, hi = max(pos, j * per), min(pos + width, (j + 1) * per)
                shards[j].append(seg[name][:, c0 + lo - pos:c0 + hi - pos])
            pos += width
        assert pos == self.n_orig
        return jnp.stack([jnp.concatenate(s, axis=1) for s in shards])


def _layer_fwd(lay, h, p, t, hooks=None):
    hooks = hooks or {}
    d = lay.d
    xn, xn_t = _rms_fwd_t("rms_mix_fwd", h, p["norm_mix_g"], t, d)
    proj = _mm("mm_proj", xn, p["w_in"], mode="nn", m=t, n=lay.n_main, k=d, out_dtypes=(BF16,))
    small = _mm("mm_small", xn, p["w_in"], mode="nn", m=t, n=LANES, k=d, b_c0=lay.c_small)
    cs = _fox_gate_fwd(small, p["b_forget_p"], t)
    ct = cs[:, :FOX_HEADS].T
    c_col, c_row = ct[:, :, None], ct[:, None, :]
    o_fox, lse = _fox_fwd(proj, c_col, c_row, t, lay.fd, lay.c_fq, lay.c_fkv)
    b_alpha = p["b_alpha"] + hooks["mixers"](o_fox) if "mixers" in hooks else p["b_alpha"]
    glog = _gla_gate_fwd(small, p["w_alpha_p"], b_alpha, t, lay.gk)
    o_raw, s_prev = _gla_fwd(proj, glog, t, lay.dk, lay.dv, lay.c_gq, lay.c_gk, lay.c_gv)
    o_gla = _gla_post_fwd(o_raw, proj, p["gla_norm_g"], t, lay.dv, lay.c_gr)
    if "late" in hooks:
        p.update(hooks["late"](o_gla))
    a_fox = _mm("mm_o_fox", o_fox, p["w_o_fox"], mode="nn", m=t, n=d, k=lay.fw, b_shards=N_CHIPS)
    a_gla = _mm("mm_o_gla", o_gla, p["w_o_gla"], mode="nn", m=t, n=d, k=lay.gv)
    y = _merge_fwd(a_fox, a_gla, proj, lay.c_gates, t, d)
    h1 = _mm("mm_out", y, p["w_out"], mode="nn", m=t, n=d, k=d, extras=[h], epilogue=lambda acc, res: (res + acc,))
    xn2, xn2_t = _rms_fwd_t("rms_mlp_fwd", h1, p["norm_mlp_g"], t, d)
    u, act = _mm("mm_ff1", xn2, p["w_ff1"], mode="nn", m=t, n=4 * d, k=d, out_dtypes=(BF16, BF16), b_shards=N_CHIPS,
                 epilogue=lambda acc: (acc, jnp.square(jnp.maximum(acc, 0.0))))
    if "mlp" in hooks:
        hooks["mlp"](act)
    h2 = _mm("mm_ff2", act, p["w_ff2"], mode="nn", m=t, n=d, k=4 * d, extras=[h1],
             epilogue=lambda acc, res: (res + acc,))
    saved = dict(h=h, xn_t=xn_t, proj=proj, small=small, c_col=c_col, c_row=c_row, o_fox=o_fox, lse=lse, glog=glog,
                 o_raw=o_raw, s_prev=s_prev, o_gla=o_gla, a_fox=a_fox, a_gla=a_gla, y=y, h1=h1, xn2_t=xn2_t, u=u, act=act)
    return h2, saved


def _layer_bwd(lay, dh2, p, s, t, gates=None, first=False):
    d = lay.d
    g = {}

    def gated(gain, point):
        return gain + gates[point](g) if gates and point in gates else gain
    du = _mm("mm_dact", dh2, p["w_ff2"], mode="nt", m=t, n=4 * d, k=d, extras=[s["u"]], out_dtypes=(BF16,),
             epilogue=lambda acc, u: (acc * (2.0 * jnp.maximum(u.astype(F32), 0.0)),))
    g["w_ff2"] = _mm("mm_dw_ff2", s["act"], dh2, mode="tn", m=4 * d, n=d, k=t)
    g["w_ff1"] = _mm("mm_dw_ff1", s["xn2_t"], du, mode="nn", m=d, n=4 * d, k=t, tk=t, out_shards=N_CHIPS)
    dxn2 = _mm("mm_dxn2", du, p["w_ff1"], mode="nt", m=t, n=d, k=4 * d, b_shards=N_CHIPS)
    dh1, g["norm_mlp_g"] = _rms_bwd("rms_mlp_bwd", s["h1"], gated(p["norm_mlp_g"], "mlp"), dxn2, dh2, t, d)
    dy = _mm("mm_dy", dh1, p["w_out"], mode="nt", m=t, n=d, k=d)
    g["w_out"] = _mm("mm_dw_out", s["y"], dh1, mode="tn", m=d, n=d, k=t)
    da_fox, da_gla, dgates = _merge_bwd(dy, s["a_fox"], s["a_gla"], s["proj"], lay.c_gates, t, d)
    g["w_o_fox"] = _mm("mm_dw_o_fox", s["o_fox"], da_fox, mode="tn", m=lay.fw, n=d, k=t, out_shards=N_CHIPS)
    do_fox = _mm("mm_do_fox", da_fox, p["w_o_fox"], mode="nt", m=t, n=lay.fw, k=d, b_shards=N_CHIPS)
    g["w_o_gla"] = _mm("mm_dw_o_gla", s["o_gla"], da_gla, mode="tn", m=lay.gv, n=d, k=t)
    do_gla = _mm("mm_do_gla", da_gla, p["w_o_gla"], mode="nt", m=t, n=lay.gv, k=d)
    do_raw, dgr, g["gla_norm_g"] = _gla_post_bwd(s["o_raw"], s["proj"], gated(p["gla_norm_g"], "out"), do_gla, t,
                                                 lay.dv, lay.c_gr)
    dgq, dgk, dgv, dglog = _gla_bwd(s["proj"], s["glog"], s["s_prev"], do_raw, t, lay.dk, lay.dv,
                                    lay.c_gq, lay.c_gk, lay.c_gv)
    dz, g["b_alpha"] = _gla_gate_bwd(dglog, s["small"], p["w_alpha_p"], p["b_alpha"], t, lay.gk)
    g["w_alpha_p"] = _mm("mm_dw_alpha", s["small"], dz, mode="tn", m=LANES, n=lay.gk, k=t)
    dga = _mm("mm_dga", dz, p["w_alpha_p"], mode="nt", m=t, n=LANES, k=lay.gk)
    delta = _fox_delta(s["o_fox"], do_fox, t, lay.fd)
    dfq, dfkv, dc, dr = _fox_bwd(s["proj"], s["c_col"], s["c_row"], s["lse"], delta, do_fox, t, lay.fd,
                                 lay.c_fq, lay.c_fkv)
    dc_p = jnp.pad((dc[:, 0, :] + dr[:, :, 0]).T, ((0, 0), (0, LANES - FOX_HEADS)))
    dsmall, g["b_forget_p"] = _fox_gate_bwd(dc_p, s["small"], p["b_forget_p"], dga, t)
    segs = [("fq", dfq, lay.c_fq), ("gq", dgq, lay.c_gq), ("gv", dgv, lay.c_gv), ("gr", dgr, lay.c_gr),
            ("fkv", dfkv, lay.c_fkv), ("gates", dgates, lay.c_gates), ("gk", dgk, lay.c_gk),
            ("small", dsmall, lay.c_small)]
    dw_in = {nm: _mm("mm_dw_in_" + nm, s["xn_t"], dseg, mode="nn", m=d, n=dseg.shape[1], k=t, tk=t)
             for nm, dseg, _ in segs}
    g["w_in"] = lay.from_segments(dw_in)
    dxn = _mm("mm_dxn_gates", dgates, p["w_in"], mode="nt", m=t, n=d, k=2 * d, b_c0=lay.c_gates)
    dxn = _mm("mm_dxn_small", dsmall, p["w_in"], mode="nt", m=t, n=d, k=LANES, b_c0=lay.c_small, extras=[dxn],
              epilogue=lambda acc, prev: (prev + acc,))
    for nm, (a1, c1), (a2, c2) in (("q", (dfq, lay.c_fq), (dgq, lay.c_gq)), ("v", (dgv, lay.c_gv), (dgr, lay.c_gr)),
                                   ("k", (dfkv, lay.c_fkv), (dgk, lay.c_gk))):
        dxn = _mm_nt2("mm_dxn_" + nm, a1, c1, a2, c2, p["w_in"], dxn, m=t, n=d)
    if first:
        dx, head, g["norm_mix_g"] = _rms_bwd_x("rms_mix_bwd_x", s["h"], gated(p["norm_mix_g"], "in"), dxn, dh1, t, d)
        return (dx, head), g
    dh, g["norm_mix_g"] = _rms_bwd("rms_mix_bwd", s["h"], gated(p["norm_mix_g"], "in"), dxn, dh1, t, d)
    return dh, g


def _pack_small(d, meta, mix, gla, mlp, final, b_alpha, b_forget, w_alpha2):
    rows = [meta.reshape(N_META, d), mix.reshape(DEPTH, d), gla.reshape(DEPTH, d), mlp.reshape(DEPTH, d),
            final.reshape(1, d), b_alpha.reshape(1, d),
            jnp.pad(b_forget.reshape(1, DEPTH * FOX_HEADS), ((0, 0), (0, d - DEPTH * FOX_HEADS))),
            jnp.zeros((7, d), F32), w_alpha2.reshape(GLA_RANK, d)]
    return jnp.concatenate(rows, axis=0)


def _unpack_small(d, packed):
    return dict(meta=packed[:N_META], norm_mix_g=packed[16:18], gla_norm_g=packed[18:20], norm_mlp_g=packed[20:22],
                final_norm_g=packed[22], b_alpha=packed[23].reshape(DEPTH, d // 2),
                b_forget=packed[24, :DEPTH * FOX_HEADS].reshape(DEPTH, FOX_HEADS),
                w_alpha2=packed[32:48].reshape(DEPTH, GLA_RANK, d // 2))


_BIG = ("w_in", "w_o_fox", "w_o_gla", "w_out", "w_ff1", "w_ff2")
_COL_SHARDED = ("w_in", "w_o_fox", "w_ff1")


def _full_matrix(name, gathered_l):
    nch, r, c = gathered_l.shape
    if name in _COL_SHARDED:
        return gathered_l.transpose(1, 0, 2).reshape(r, nch * c)
    return gathered_l.reshape(nch * r, c)


def _shard_major(name, full):
    r, c = full.shape
    if name in _COL_SHARDED:
        return full.reshape(r, N_CHIPS, c // N_CHIPS).transpose(1, 0, 2)
    return full.reshape(N_CHIPS, r // N_CHIPS, c)


def kernel(x, meta_tokens, norm_mix_g, w_in, b_forget, w_alpha2, b_alpha, gla_norm_g, w_o_fox, w_o_gla, w_out, norm_mlp_g, w_ff1, w_ff2, final_norm_g, loss_target, m_meta_tokens, m_norm_mix_g, m_w_in, m_b_forget, m_w_alpha2, m_b_alpha, m_gla_norm_g, m_w_o_fox, m_w_o_gla, m_w_out, m_norm_mlp_g, m_w_ff1, m_w_ff2, m_final_norm_g, v_meta_tokens, v_norm_mix_g, v_w_in, v_b_forget, v_w_alpha2, v_b_alpha, v_gla_norm_g, v_w_o_fox, v_w_o_gla, v_w_out, v_norm_mlp_g, v_w_ff1, v_w_ff2, v_final_norm_g):
    d = x.shape[2]
    lay = _Layout(d)
    xi, yi, ci = lax.axis_index("x"), lax.axis_index("y"), lax.axis_index("c")
    chip = (2 * xi + yi).astype(jnp.int32)
    w = dict(w_in=w_in, w_alpha2=w_alpha2, w_o_fox=w_o_fox, w_o_gla=w_o_gla, w_out=w_out, w_ff1=w_ff1, w_ff2=w_ff2)
    m = dict(w_in=m_w_in, w_alpha2=m_w_alpha2, w_o_fox=m_w_o_fox, w_o_gla=m_w_o_gla, w_out=m_w_out, w_ff1=m_w_ff1,
             w_ff2=m_w_ff2)
    v = dict(w_in=v_w_in, w_alpha2=v_w_alpha2, w_o_fox=v_w_o_fox, w_o_gla=v_w_o_gla, w_out=v_w_out, w_ff1=v_w_ff1,
             w_ff2=v_w_ff2)

    seq = x.shape[1]
    t = seq + ROW0
    core_idx = ci.astype(jnp.int32)[None]
    chip_idx = chip[None]

    cols = d // N_CHIPS
    small_w = jnp.concatenate([meta_tokens, w_alpha2.reshape(-1, cols)], axis=0)
    small_raw = _all_gather_small(small_w)
    small_all = small_raw[0::2]
    alpha_full = small_all[:, N_META:].reshape(N_CHIPS, DEPTH, GLA_RANK, lay.gk // N_CHIPS)
    alpha_full = alpha_full.transpose(1, 2, 0, 3).reshape(DEPTH, GLA_RANK, lay.gk)
    groups = [(0, ("w_in",)), (0, _BIG[1:]), (1, _BIG)]
    started, after = [], small_raw
    for gi, (l, names) in enumerate(groups):
        own16 = [w[n][l].astype(BF16) for n in names]
        lands = [lax.empty((N_CHIPS,) + o.shape, BF16) for o in own16]
        started.append(_split_start("gather_start_%d" % gi, own16, lands, _gather_plan, after=after))
        after = started[gi][4]
    meta_full = small_all[:, :N_META].transpose(1, 0, 2).reshape(N_META, d) + after[0, 0]

    passing = {}

    def arrive(gi, after):
        send_sems, recv_sems, srcs, lands, _ = started[gi]
        srcs, lands = _split_wait("gather_wait_%d" % gi, send_sems, recv_sems, srcs, lands, _gather_plan, after)
        passing[gi] = (srcs, _exchange_start("gather_pass_%d" % gi, lands, _forward_plan, after=srcs[0]))
        return passing[gi][1][3][0, 0]

    def gathered(gi, after):
        if gi not in passing:
            arrive(gi, after)
        srcs, (send_sems, recv_sems, lands, _) = passing[gi]
        lands = _exchange_wait("gather_pass_wait_%d" % gi, send_sems, recv_sems, lands, _forward_plan, after)
        return {n: lax.dynamic_update_slice(g, o[None], (chip, 0, 0)) for n, g, o in zip(groups[gi][1], lands, srcs)}

    def early_weights(l, gl):
        w_alpha_p = jnp.zeros((LANES, lay.gk), BF16).at[FOX_HEADS:FOX_HEADS + GLA_RANK].set(
            alpha_full[l].astype(BF16))
        return dict(
            w_in=lay.to_p(gl["w_in"]), w_alpha_p=w_alpha_p,
            norm_mix_g=norm_mix_g[l][None], norm_mlp_g=norm_mlp_g[l][None], gla_norm_g=gla_norm_g[l][None],
            b_alpha=b_alpha[l][None],
            b_forget_p=jnp.pad(b_forget[l][None], ((0, 0), (0, LANES - FOX_HEADS))))

    def late_weights(gl):
        return dict(w_o_fox=gl["w_o_fox"], w_o_gla=_full_matrix("w_o_gla", gl["w_o_gla"]),
                    w_out=_full_matrix("w_out", gl["w_out"]), w_ff1=gl["w_ff1"],
                    w_ff2=_full_matrix("w_ff2", gl["w_ff2"]))

    h = jnp.pad(x[0], ((ROW0, 0), (0, 0))).at[PAD:ROW0].set(meta_full)
    layers, saved = [], []
    layers.append(early_weights(0, gathered(0, after=h)))
    h, s = _layer_fwd(lay, h, layers[0], t, hooks=dict(
        mixers=lambda after: arrive(1, after), late=lambda after: late_weights(gathered(1, after)),
        mlp=lambda after: arrive(2, after)))
    saved.append(s)
    gl = gathered(2, after=h)
    layers.append({**early_weights(1, gl), **late_weights(gl)})
    h, s = _layer_fwd(lay, h, layers[1], t)
    saved.append(s)
    dh, dg_final, loss_part = _loss_head(h, final_norm_g[None], jnp.pad(loss_target[0], ((ROW0, 0), (0, 0))), t, d)
    loss = lax.psum(loss_part[0, 0], ("x", "y", "c"))

    def partial_of(g, n):
        return g[n] if n in ("w_ff1", "w_o_fox", "w_in") else _shard_major(n, g[n])

    scatter_groups = dict(mlp=("w_ff1", "w_ff2"), out=("w_o_fox", "w_o_gla", "w_out"))
    scatter_groups["in"] = ("w_in",)
    swapping, scattered = [], {}

    def start_swap(l, grp, g, after=None):
        parts = [partial_of(g, n) for n in scatter_groups[grp]]
        lands = [lax.empty((p_.shape[0], p_.shape[1] // 2, p_.shape[2]), F32) for p_ in parts]
        started_swap = _split_start("swap_start_%d_%s" % (l, grp), parts, lands, _swap_plan, after=after)
        swapping.append((l, grp, started_swap))
        return started_swap[4]

    def start_scatter(after):
        l, grp, (send_sems, recv_sems, srcs, lands, _) = swapping.pop(0)
        names = scatter_groups[grp]
        tag = "%d_%s" % (l, grp)
        parts, from_sibling = _split_wait("swap_wait_" + tag, send_sems, recv_sems, srcs, lands, _swap_plan, after)
        sums = [_pair_sum("pair_sum_%d_%s" % (l, n), p_, r_, chip_idx, core_idx)
                for n, p_, r_ in zip(names, parts, from_sibling)]
        lands = [lax.empty(s16.shape, BF16) for s16 in sums]
        send_sems, recv_sems, srcs, lands, token = _split_start("scatter_start_" + tag, sums, lands, _scatter_plan)
        scattered[l, grp] = (send_sems, recv_sems, srcs, lands, parts, from_sibling)
        return token

    def gate(l, grp, g):
        token = start_swap(l, grp, g)
        if len(swapping) > 1:
            token = start_scatter(after=token)
        return token[0, 0]

    def gates_for(l, points):
        return {grp: functools.partial(gate, l, grp) for grp in points}

    grads = [None] * DEPTH
    dh, grads[1] = _layer_bwd(lay, dh, layers[1], saved[1], t, gates=gates_for(1, ("mlp", "out", "in")))
    (grad_x, head), grads[0] = _layer_bwd(lay, dh, layers[0], saved[0], t, gates=gates_for(0, ("mlp", "out")),
                                          first=True)
    d_meta = head[PAD:ROW0]

    stack = lambda key: jnp.concatenate([grads[l][key] for l in range(DEPTH)], axis=0)
    b_forget_g = jnp.concatenate([grads[l]["b_forget_p"][:, :FOX_HEADS] for l in range(DEPTH)], axis=0)
    alpha_g = jnp.stack([grads[l]["w_alpha_p"][FOX_HEADS:FOX_HEADS + GLA_RANK] for l in range(DEPTH)])
    packed = _pack_small(d, d_meta, stack("norm_mix_g"), stack("gla_norm_g"), stack("norm_mlp_g"), dg_final,
                         stack("b_alpha"), b_forget_g, alpha_g)
    small_g = _unpack_small(d, _sum_devices(_all_gather_small(packed)))
    small_g["meta"] = lax.dynamic_slice_in_dim(small_g["meta"], chip * (d // N_CHIPS), d // N_CHIPS, axis=1)
    alpha_shard = lax.dynamic_slice_in_dim(small_g["w_alpha2"], chip * (lay.gk // N_CHIPS), lay.gk // N_CHIPS, axis=2)

    after = start_swap(0, "in", grads[0], after=small_g["final_norm_g"])
    while swapping:
        after = start_scatter(after)
    outs = {n: None for n in _BIG}

    def reduce_and_share(l, grps, after):
        tag = "%d_%s" % (l, grps[0])
        names, sums = [], []
        for grp in grps:
            send_sems, recv_sems, srcs, lands, parts, from_sibling = scattered[l, grp]
            _, lands = _split_wait("scatter_wait_%d_%s" % (l, grp), send_sems, recv_sems, srcs, lands, _scatter_plan,
                                 after)
            names += scatter_groups[grp]
            sums += [_chip_sum("chip_sum_%d_%s" % (l, n), p_, r_, landed, chip_idx, core_idx)
                     for n, p_, r_, landed in zip(scatter_groups[grp], parts, from_sibling, lands)]
        return l, tag, names, _exchange_start("share_start_" + tag, sums, _share_plan, after=sums[-1])

    def update(shared, after):
        l, tag, names, (send_sems, recv_sems, sums, _) = shared
        sums = _exchange_wait("share_wait_" + tag, send_sems, recv_sems, sums, _share_plan, after)
        for n, g in zip(names, sums):
            outs[n] = _adamw("adamw_%d_%s" % (l, n), w[n], g, m[n], v[n], layer=l, into=outs[n])
        return outs[names[-1]][0]

    shared_1 = reduce_and_share(1, ("mlp", "out", "in"), after)
    shared_0 = reduce_and_share(0, ("mlp", "out"), shared_1[3][3])
    after = update(shared_1, shared_0[3][3])
    shared_in = reduce_and_share(0, ("in",), after)
    after = update(shared_0, shared_in[3][3])
    update(shared_in, after)
    out_g, out_d, out_m, out_v = {}, {}, {}, {}
    for n in _BIG:
        out_g[n], out_d[n], out_m[n], out_v[n] = outs[n]
    out_g["w_alpha2"], out_d["w_alpha2"], out_m["w_alpha2"], out_v["w_alpha2"] = _adamw(
        "adamw_w_alpha2", w["w_alpha2"], alpha_shard, m["w_alpha2"], v["w_alpha2"])
    sm_w = dict(meta_tokens=meta_tokens, norm_mix_g=norm_mix_g, b_forget=b_forget, b_alpha=b_alpha,
                gla_norm_g=gla_norm_g, norm_mlp_g=norm_mlp_g, final_norm_g=final_norm_g)
    sm_m = dict(meta_tokens=m_meta_tokens, norm_mix_g=m_norm_mix_g, b_forget=m_b_forget, b_alpha=m_b_alpha,
                gla_norm_g=m_gla_norm_g, norm_mlp_g=m_norm_mlp_g, final_norm_g=m_final_norm_g)
    sm_v = dict(meta_tokens=v_meta_tokens, norm_mix_g=v_norm_mix_g, b_forget=v_b_forget, b_alpha=v_b_alpha,
                gla_norm_g=v_gla_norm_g, norm_mlp_g=v_norm_mlp_g, final_norm_g=v_final_norm_g)
    sm_g = dict(meta_tokens=small_g["meta"], norm_mix_g=small_g["norm_mix_g"], b_forget=small_g["b_forget"],
                b_alpha=small_g["b_alpha"], gla_norm_g=small_g["gla_norm_g"], norm_mlp_g=small_g["norm_mlp_g"],
                final_norm_g=small_g["final_norm_g"])
    names_small = list(sm_w)
    sizes = [sm_w[n].size for n in names_small]
    width = 512
    total = -(-sum(sizes) // (8 * width)) * (8 * width)

    def pack_flat(dct, fill):
        flat = jnp.concatenate([dct[n].reshape(-1) for n in names_small])
        return jnp.pad(flat, (0, total - flat.shape[0]), constant_values=fill).reshape(1, -1, width)

    res = _adamw("adamw_small", pack_flat(sm_w, 0.0), pack_flat(sm_g, 0.0), pack_flat(sm_m, 0.0), pack_flat(sm_v, 1.0))
    offs = [0]
    for sz in sizes:
        offs.append(offs[-1] + sz)
    for i, n in enumerate(names_small):
        out_g[n] = sm_g[n].reshape(sm_w[n].shape)
        out_d[n], out_m[n], out_v[n] = [r.reshape(-1)[offs[i]:offs[i + 1]].reshape(sm_w[n].shape) for r in res[1:]]

    order = ["meta_tokens", "norm_mix_g", "w_in", "b_forget", "w_alpha2", "b_alpha", "gla_norm_g", "w_o_fox",
             "w_o_gla", "w_out", "norm_mlp_g", "w_ff1", "w_ff2", "final_norm_g"]
    return (loss, grad_x[None], *[out_g[n] for n in order], *[out_d[n] for n in order],
            *[out_m[n] for n in order], *[out_v[n] for n in order])
```

```python
import functools

import numpy as np

import jax
import jax.numpy as jnp
from jax import lax
from jax.experimental import pallas as pl
from jax.experimental.pallas import tpu as pltpu

F32 = jnp.float32
BF16 = jnp.bfloat16

N_META = 16
PAD = 112
ROW0 = PAD + N_META
EPS = 1e-6
MASK_VALUE = -1e30
FOX_HEADS = 8
FOX_GROUP = 2
GLA_HEADS = 4
GLA_RANK = 16
GLA_TAU = 16.0
GLA_CHUNK = 64
DEPTH = 2
N_CHIPS = 4
N_DEV = 8

ADAM_LR = 0.001
ADAM_B1 = 0.9
ADAM_B2 = 0.999
ADAM_EPS = 1e-08
ADAM_WD = 0.01
ADAM_STEP = 10

LANES = 128
VMEM_LIMIT = 56 * 1024 * 1024
MESH = pl.DeviceIdType.MESH


def _pick(n, target, mult):
    best = None
    for d in range(mult, min(n, target) + 1, mult):
        if n % d == 0:
            best = d
    return n if best is None else best


def _params(sem=None):
    return pltpu.CompilerParams(dimension_semantics=sem, vmem_limit_bytes=VMEM_LIMIT)


def _bf(v):
    return v if v.dtype == BF16 else v.astype(BF16)


def _sigmoid(z):
    return 1.0 / (1.0 + jnp.exp(-z))


def _log_sigmoid(z):
    return jnp.minimum(z, 0.0) - jnp.log(1.0 + jnp.exp(-jnp.abs(z)))


def _split3(v):
    a = v.astype(BF16)
    r = v - a.astype(F32)
    b = r.astype(BF16)
    c = (r - b.astype(F32)).astype(BF16)
    return a, b, c


def _dot(a, b, dims):
    return lax.dot_general(a, b, (dims, ((), ())), preferred_element_type=F32)


NN = ((1,), (0,))
NT = ((1,), (1,))
TN = ((0,), (0,))


def _tri_dot(tri, v, dims=NN):
    a, b, c = _split3(v)
    return _dot(tri, a, dims) + _dot(tri, b, dims) + _dot(tri, c, dims)


def _mm(name, a, b, *, mode, m, n, k, b_c0=0, extras=(), epilogue=None, out_dtypes=(F32,),
        b_shards=1, out_shards=1, tm=1056, tn=1024, tk=2048):
    tm = _pick(m, tm, LANES if mode == "tn" else 16)
    tn = _pick(n // max(b_shards if mode == "nn" else 1, out_shards), tn, LANES)
    if mode == "tn":
        tk = _pick(k, 2112, 16)
    else:
        tk = _pick(k // (b_shards if mode == "nt" else 1), tk, LANES)
    assert b_c0 % (tk if mode == "nt" else tn) == 0 and (b_shards == 1 or b_c0 == 0)
    nk = k // tk
    if mode == "tn":
        a_spec = pl.BlockSpec((tk, tm), lambda i, j, kk: (kk, i))
    else:
        a_spec = pl.BlockSpec((tm, tk), lambda i, j, kk: (i, kk))
    if mode == "nt":
        dims = NT
        if b_shards > 1:
            per = (k // b_shards) // tk
            b_spec = pl.BlockSpec((None, tn, tk), lambda i, j, kk: (kk // per, j, kk % per))
        else:
            b_spec = pl.BlockSpec((tn, tk), lambda i, j, kk: (j, kk + b_c0 // tk))
    else:
        dims = NN if mode == "nn" else TN
        if b_shards > 1:
            per = (n // b_shards) // tn
            b_spec = pl.BlockSpec((None, tk, tn), lambda i, j, kk: (j // per, kk, j % per))
        else:
            b_spec = pl.BlockSpec((tk, tn), lambda i, j, kk: (kk, j + b_c0 // tn))
    ex_specs = [pl.BlockSpec((tm, tn), lambda i, j, kk: (i, j)) for _ in extras]
    if out_shards > 1:
        oper = (n // out_shards) // tn
        out_specs = [pl.BlockSpec((None, tm, tn), lambda i, j, kk: (j // oper, i, j % oper)) for _ in out_dtypes]
        out_shape = [jax.ShapeDtypeStruct((out_shards, m, n // out_shards), dt) for dt in out_dtypes]
    else:
        out_specs = [pl.BlockSpec((tm, tn), lambda i, j, kk: (i, j)) for _ in out_dtypes]
        out_shape = [jax.ShapeDtypeStruct((m, n), dt) for dt in out_dtypes]
    n_ex = len(extras)
    n_out = len(out_dtypes)

    def finish(acc, ex_refs, out_refs):
        vals = (acc,) if epilogue is None else epilogue(acc, *[r[...] for r in ex_refs])
        for r, v in zip(out_refs, vals):
            r[...] = v.astype(r.dtype)

    def body(a_ref, b_ref, *rest):
        ex_refs = rest[:n_ex]
        out_refs = rest[n_ex:n_ex + n_out]
        prod = _dot(_bf(a_ref[...]), _bf(b_ref[...]), dims)
        if nk == 1:
            finish(prod, ex_refs, out_refs)
            return
        acc_ref = rest[n_ex + n_out]
        kk = pl.program_id(2)

        @pl.when(kk == 0)
        def _():
            acc_ref[...] = prod

        @pl.when((kk > 0) & (kk < nk - 1))
        def _():
            acc_ref[...] += prod

        @pl.when(kk == nk - 1)
        def _():
            finish(acc_ref[...] + prod, ex_refs, out_refs)

    outs = pl.pallas_call(
        body,
        name=name,
        grid=(m // tm, n // tn, nk),
        in_specs=[a_spec, b_spec] + ex_specs,
        out_specs=out_specs,
        out_shape=out_shape,
        scratch_shapes=[pltpu.VMEM((tm, tn), F32)] if nk > 1 else [],
        compiler_params=_params(("parallel", "parallel", "arbitrary")),
    )(a, b, *extras)
    return outs[0] if n_out == 1 else outs


def _mm_nt2(name, a1, c1, a2, c2, b, prev, *, m, n, tm=528, tn=1024):
    k1, k2 = a1.shape[1], a2.shape[1]
    assert c1 % k1 == 0 and c2 % k2 == 0
    tm = _pick(m, tm, 16)
    tn = _pick(n, tn, LANES)

    def body(a1_ref, a2_ref, b1_ref, b2_ref, *rest):
        acc = _dot(a1_ref[...], b1_ref[...], NT) + _dot(a2_ref[...], b2_ref[...], NT)
        if prev is not None:
            acc = rest[0][...] + acc
        rest[-1][...] = acc

    tile = pl.BlockSpec((tm, tn), lambda i, j: (i, j))
    in_specs = [pl.BlockSpec((tm, k1), lambda i, j: (i, 0)), pl.BlockSpec((tm, k2), lambda i, j: (i, 0)),
                pl.BlockSpec((tn, k1), lambda i, j: (j, c1 // k1)), pl.BlockSpec((tn, k2), lambda i, j: (j, c2 // k2))]
    args = [a1, a2, b, b]
    if prev is not None:
        in_specs.append(tile)
        args.append(prev)
    return pl.pallas_call(
        body, name=name, grid=(m // tm, n // tn), in_specs=in_specs, out_specs=tile,
        out_shape=jax.ShapeDtypeStruct((m, n), F32),
        compiler_params=_params(("parallel", "parallel")),
    )(*args)


def _ew(name, fn, ins, outs, rows, tm):
    tm = _pick(rows, tm, 16)
    in_specs, args = [], []
    for spec in ins:
        if spec[0] == "tile":
            _, arr, width, c0 = spec
            assert c0 % width == 0
            in_specs.append(pl.BlockSpec((tm, width), functools.partial(lambda i, o: (i, o), o=c0 // width)))
        else:
            arr = spec[1]
            in_specs.append(pl.BlockSpec(arr.shape, lambda i: (0, 0)))
        args.append(arr)
    out_specs, out_shape = [], []
    for kind, dt, width in outs:
        if kind == "tile":
            out_specs.append(pl.BlockSpec((tm, width), lambda i: (i, 0)))
            out_shape.append(jax.ShapeDtypeStruct((rows, width), dt))
        else:
            out_specs.append(pl.BlockSpec((1, width), lambda i: (0, 0)))
            out_shape.append(jax.ShapeDtypeStruct((1, width), dt))
    n_in = len(ins)
    has_acc = any(o[0] == "acc" for o in outs)

    def body(*refs):
        i = pl.program_id(0)
        vals = fn(i * tm, *[r[...] for r in refs[:n_in]])
        for (kind, _, _), r, v in zip(outs, refs[n_in:], vals):
            if kind == "tile":
                r[...] = v.astype(r.dtype)
            else:
                @pl.when(i == 0)
                def _():
                    r[...] = jnp.zeros_like(r)

                r[...] += v.astype(r.dtype)

    res = pl.pallas_call(
        body,
        name=name,
        grid=(rows // tm,),
        in_specs=in_specs,
        out_specs=out_specs,
        out_shape=out_shape,
        compiler_params=_params(("arbitrary",) if has_acc else ("parallel",)),
    )(*args)
    return res[0] if len(outs) == 1 else res


def _row_ids(row0, tm):
    return row0 + lax.broadcasted_iota(jnp.int32, (tm, 1), 0)


def _colsum(v):
    return jnp.sum(v, axis=0, keepdims=True)


def _rms_fwd_t(name, h, g, t, d):
    tm = _pick(t, 384, LANES)

    def body(x_ref, g_ref, y_ref, yt_ref):
        x = x_ref[...]
        y = x * lax.rsqrt(jnp.mean(x * x, axis=-1, keepdims=True) + EPS) * g_ref[...]
        y_ref[...] = y.astype(BF16)
        yt_ref[...] = y.T.astype(BF16)

    return pl.pallas_call(
        body, name=name, grid=(t // tm,),
        in_specs=[pl.BlockSpec((tm, d), lambda i: (i, 0)), pl.BlockSpec((1, d), lambda i: (0, 0))],
        out_specs=[pl.BlockSpec((tm, d), lambda i: (i, 0)), pl.BlockSpec((d, tm), lambda i: (0, i))],
        out_shape=[jax.ShapeDtypeStruct((t, d), BF16), jax.ShapeDtypeStruct((d, t), BF16)],
        compiler_params=_params(("parallel",)),
    )(h, g)


def _rms_bwd(name, h, g, dy, dres, t, d):
    def fn(row0, x, gg, dyv, dr):
        r = lax.rsqrt(jnp.mean(x * x, axis=-1, keepdims=True) + EPS)
        xh = x * r
        dxh = dyv * gg
        dx = r * (dxh - xh * jnp.mean(dxh * xh, axis=-1, keepdims=True))
        out = jnp.where(_row_ids(row0, x.shape[0]) >= PAD, dr + dx, 0.0)
        return out, out, _colsum(dyv * xh)

    return _ew(name, fn, [("tile", h, d, 0), ("full", g), ("tile", dy, d, 0), ("tile", dres, d, 0)],
               [("tile", F32, d), ("tile", BF16, d), ("acc", F32, d)], t, 264)


def _rms_bwd_x(name, h, g, dy, dres, t, d):
    tm = ROW0

    def body(x_ref, g_ref, dy_ref, dr_ref, dx_ref, head_ref, dg_ref):
        i = pl.program_id(0)
        x = x_ref[...]
        r = lax.rsqrt(jnp.mean(x * x, axis=-1, keepdims=True) + EPS)
        xh = x * r
        dxh = dy_ref[...] * g_ref[...]
        dx = r * (dxh - xh * jnp.mean(dxh * xh, axis=-1, keepdims=True))
        out = jnp.where(_row_ids(i * tm, tm) >= PAD, dr_ref[...] + dx, 0.0)

        @pl.when(i == 0)
        def _():
            head_ref[...] = out
            dg_ref[...] = jnp.zeros_like(dg_ref)

        dx_ref[...] = out
        dg_ref[...] += _colsum(dy_ref[...] * xh)

    tile = pl.BlockSpec((tm, d), lambda i: (i, 0))
    fixed = lambda shape: pl.BlockSpec(shape, lambda i: (0, 0))
    return pl.pallas_call(
        body, name=name, grid=(t // tm,),
        in_specs=[tile, fixed((1, d)), tile, tile],
        out_specs=[pl.BlockSpec((tm, d), lambda i: (jnp.maximum(i - 1, 0), 0)), fixed((tm, d)), fixed((1, d))],
        out_shape=[jax.ShapeDtypeStruct((t - ROW0, d), F32), jax.ShapeDtypeStruct((ROW0, d), F32),
                   jax.ShapeDtypeStruct((1, d), F32)],
        compiler_params=_params(("arbitrary",)),
    )(h, g, dy, dres)


def _loss_head(h, g, target_p, t, d):
    def fn(row0, x, gg, tgt):
        real = _row_ids(row0, x.shape[0]) >= ROW0
        r = lax.rsqrt(jnp.mean(x * x, axis=-1, keepdims=True) + EPS)
        xh = x * r
        err = jnp.where(real, xh * gg - tgt, 0.0)
        loss_rows = 0.5 * jnp.mean(err * err, axis=-1, keepdims=True)
        dyv = err * (1.0 / d)
        dxh = dyv * gg
        dx = r * (dxh - xh * jnp.mean(dxh * xh, axis=-1, keepdims=True))
        loss_part = jnp.sum(loss_rows, axis=0, keepdims=True) * jnp.ones((1, LANES), F32)
        dxm = jnp.where(real, dx, 0.0)
        return dxm, dxm, _colsum(dyv * xh), loss_part

    return _ew("loss_head", fn, [("tile", h, d, 0), ("full", g), ("tile", target_p, d, 0)],
               [("tile", F32, d), ("tile", BF16, d), ("acc", F32, d), ("acc", F32, LANES)], t, 264)


def _merge_fwd(a_fox, a_gla, proj, c_gates, t, d):
    def fn(row0, af, ag, gates):
        gates = gates.astype(F32)
        return (_sigmoid(gates[:, :d]) * af + _sigmoid(gates[:, d:]) * ag,)

    return _ew("merge_fwd", fn, [("tile", a_fox, d, 0), ("tile", a_gla, d, 0), ("tile", proj, 2 * d, c_gates)],
               [("tile", BF16, d)], t, 264)


def _merge_bwd(dy, a_fox, a_gla, proj, c_gates, t, d):
    def fn(row0, dyv, af, ag, gates):
        gates = gates.astype(F32)
        sf = _sigmoid(gates[:, :d])
        sg = _sigmoid(gates[:, d:])
        dgates = jnp.concatenate([dyv * af * sf * (1.0 - sf), dyv * ag * sg * (1.0 - sg)], axis=1)
        return dyv * sf, dyv * sg, dgates

    return _ew("merge_bwd", fn,
               [("tile", dy, d, 0), ("tile", a_fox, d, 0), ("tile", a_gla, d, 0), ("tile", proj, 2 * d, c_gates)],
               [("tile", BF16, d), ("tile", BF16, d), ("tile", BF16, 2 * d)], t, 264)


def _fox_gate_fwd(small, b_forget_p, t):
    tb = _pick(t, 384, LANES)

    def body(s_ref, b_ref, c_ref, carry_ref):
        i = pl.program_id(0)

        @pl.when(i == 0)
        def _():
            carry_ref[...] = jnp.zeros_like(carry_ref)

        logf = _log_sigmoid(s_ref[...] + b_ref[...])
        logf = jnp.where(_row_ids(i * tb, tb) >= PAD, logf, 0.0)
        r = lax.broadcasted_iota(jnp.int32, (tb, tb), 0)
        c = lax.broadcasted_iota(jnp.int32, (tb, tb), 1)
        tri = (c <= r).astype(BF16)
        cs = _tri_dot(tri, logf) + carry_ref[...]
        c_ref[...] = cs
        carry_ref[...] = cs[tb - 1:tb, :]

    return pl.pallas_call(
        body, name="fox_gate_fwd", grid=(t // tb,),
        in_specs=[pl.BlockSpec((tb, LANES), lambda i: (i, 0)), pl.BlockSpec((1, LANES), lambda i: (0, 0))],
        out_specs=pl.BlockSpec((tb, LANES), lambda i: (i, 0)),
        out_shape=jax.ShapeDtypeStruct((t, LANES), F32),
        scratch_shapes=[pltpu.VMEM((1, LANES), F32)],
        compiler_params=_params(("arbitrary",)),
    )(small, b_forget_p)


def _fox_gate_bwd(dc, small, b_forget_p, dga, t):
    tb = _pick(t, 384, LANES)
    nb = t // tb

    def body(dc_ref, s_ref, b_ref, dga_ref, ds_ref, db_ref, carry_ref):
        i = pl.program_id(0)

        @pl.when(i == 0)
        def _():
            carry_ref[...] = jnp.zeros_like(carry_ref)
            db_ref[...] = jnp.zeros_like(db_ref)

        r = lax.broadcasted_iota(jnp.int32, (tb, tb), 0)
        c = lax.broadcasted_iota(jnp.int32, (tb, tb), 1)
        tri = (c >= r).astype(BF16)
        dlogf = _tri_dot(tri, dc_ref[...]) + carry_ref[...]
        carry_ref[...] = dlogf[0:1, :]
        z = s_ref[...] + b_ref[...]
        dff = dlogf * _sigmoid(-z)
        lane = lax.broadcasted_iota(jnp.int32, (tb, LANES), 1)
        keep = (_row_ids((nb - 1 - i) * tb, tb) >= PAD) & (lane < FOX_HEADS)
        dff = jnp.where(keep, dff, 0.0)
        ds_ref[...] = dff + dga_ref[...]
        db_ref[...] += _colsum(dff)

    rev = lambda i: (nb - 1 - i, 0)
    return pl.pallas_call(
        body, name="fox_gate_bwd", grid=(nb,),
        in_specs=[pl.BlockSpec((tb, LANES), rev), pl.BlockSpec((tb, LANES), rev),
                  pl.BlockSpec((1, LANES), lambda i: (0, 0)), pl.BlockSpec((tb, LANES), rev)],
        out_specs=[pl.BlockSpec((tb, LANES), rev), pl.BlockSpec((1, LANES), lambda i: (0, 0))],
        out_shape=[jax.ShapeDtypeStruct((t, LANES), F32), jax.ShapeDtypeStruct((1, LANES), F32)],
        scratch_shapes=[pltpu.VMEM((1, LANES), F32)],
        compiler_params=_params(("arbitrary",)),
    )(dc, small, b_forget_p, dga)


def _fox_pairs(nb, by_key):
    if by_key:
        pairs = [(qi, ki) for ki in range(nb) for qi in range(ki, nb)]
    else:
        pairs = [(qi, ki) for qi in range(nb) for ki in range(qi + 1)]
    return (jnp.asarray(np.array([p[0] for p in pairs], np.int32)),
            jnp.asarray(np.array([p[1] for p in pairs], np.int32)), len(pairs))


def _fox_specs(tb, fd, c_fq, c_fkv):
    gw = FOX_GROUP * fd
    q0, kv0 = c_fq // gw, c_fkv // (2 * gw)
    return dict(
        q=pl.BlockSpec((tb, gw), lambda g, p, qt, kt: (qt[p], q0 + g)),
        kv=pl.BlockSpec((tb, 2 * gw), lambda g, p, qt, kt: (kt[p], kv0 + g)),
        col=pl.BlockSpec((FOX_GROUP, tb, 1), lambda g, p, qt, kt: (g, qt[p], 0)),
        row=pl.BlockSpec((FOX_GROUP, 1, tb), lambda g, p, qt, kt: (g, 0, kt[p])),
        head=pl.BlockSpec((tb, gw), lambda g, p, qt, kt: (qt[p], g)),
        key_kv=pl.BlockSpec((tb, 2 * gw), lambda g, p, qt, kt: (kt[p], g)),
    )


def _fox_mask(qi, ki, tb):
    row = qi * tb + lax.broadcasted_iota(jnp.int32, (tb, tb), 0)
    col = ki * tb + lax.broadcasted_iota(jnp.int32, (tb, tb), 1)
    return (col <= row) & (col >= PAD)


def _fox_heads(q_ref, kv_ref, fd):
    return [(q_ref[:, hh * fd:(hh + 1) * fd], kv_ref[:, 2 * hh * fd:(2 * hh + 1) * fd],
             kv_ref[:, (2 * hh + 1) * fd:(2 * hh + 2) * fd]) for hh in range(FOX_GROUP)]


def _fox_fwd(proj, c_col, c_row, t, fd, c_fq, c_fkv):
    tb = _pick(t, 384, LANES)
    nb = t // tb
    scale = fd ** -0.5
    sp = _fox_specs(tb, fd, c_fq, c_fkv)
    qt, kt, npairs = _fox_pairs(nb, by_key=False)

    def body(qt_ref, kt_ref, q_ref, kv_ref, cq_ref, ck_ref, o_ref, lse_ref, m_ref, l_ref, acc_ref):
        p = pl.program_id(1)
        qi, ki = qt_ref[p], kt_ref[p]

        @pl.when(ki == 0)
        def _():
            m_ref[...] = jnp.full_like(m_ref, -jnp.inf)
            l_ref[...] = jnp.zeros_like(l_ref)
            acc_ref[...] = jnp.zeros_like(acc_ref)

        def update(masked):
            mask = _fox_mask(qi, ki, tb) if masked else None
            heads = _fox_heads(q_ref, kv_ref, fd)
            scores = [_dot(q, k, NT) for q, k, _ in heads]
            for hh, (q, k, v) in enumerate(heads):
                s = scores[hh] * scale + cq_ref[hh] - ck_ref[hh]
                if masked:
                    s = jnp.where(mask, s, MASK_VALUE)
                m_prev = m_ref[hh]
                m_new = jnp.maximum(m_prev, jnp.max(s, axis=-1, keepdims=True))
                alpha = jnp.exp(m_prev - m_new)
                pe = jnp.exp(s - m_new)
                l_ref[hh] = alpha * l_ref[hh] + jnp.sum(pe, axis=-1, keepdims=True)
                acc_ref[hh] = alpha * acc_ref[hh] + _dot(pe.astype(BF16), v, NN)
                m_ref[hh] = m_new

        edge = (ki == 0) | (ki == qi)
        pl.when(edge)(functools.partial(update, True))
        pl.when(jnp.logical_not(edge))(functools.partial(update, False))

        @pl.when(ki == qi)
        def _():
            real = _row_ids(qi * tb, tb) >= PAD
            for hh in range(FOX_GROUP):
                o_ref[:, hh * fd:(hh + 1) * fd] = jnp.where(real, acc_ref[hh] / l_ref[hh], 0.0)
                lse_ref[hh] = m_ref[hh] + jnp.log(l_ref[hh])

    return pl.pallas_call(
        body, name="fox_fwd",
        grid_spec=pltpu.PrefetchScalarGridSpec(
            num_scalar_prefetch=2, grid=(FOX_HEADS // FOX_GROUP, npairs),
            in_specs=[sp["q"], sp["kv"], sp["col"], sp["row"]],
            out_specs=[sp["head"], sp["col"]],
            scratch_shapes=[pltpu.VMEM((FOX_GROUP, tb, 1), F32), pltpu.VMEM((FOX_GROUP, tb, 1), F32),
                            pltpu.VMEM((FOX_GROUP, tb, fd), F32)]),
        out_shape=[jax.ShapeDtypeStruct((t, FOX_HEADS * fd), F32), jax.ShapeDtypeStruct((FOX_HEADS, t, 1), F32)],
        compiler_params=_params(("parallel", "arbitrary")),
    )(qt, kt, proj, proj, c_col, c_row)


def _fox_delta(o_fox, do_fox, t, fd):
    tb = _pick(t, 384, LANES)

    def body(o_ref, do_ref, out_ref):
        for h in range(FOX_HEADS):
            sl = slice(h * fd, (h + 1) * fd)
            out_ref[h] = jnp.sum(o_ref[:, sl] * do_ref[:, sl].astype(BF16).astype(F32), axis=-1, keepdims=True)

    w = FOX_HEADS * fd
    return pl.pallas_call(
        body, name="fox_delta", grid=(t // tb,),
        in_specs=[pl.BlockSpec((tb, w), lambda i: (i, 0)), pl.BlockSpec((tb, w), lambda i: (i, 0))],
        out_specs=pl.BlockSpec((FOX_HEADS, tb, 1), lambda i: (0, i, 0)),
        out_shape=jax.ShapeDtypeStruct((FOX_HEADS, t, 1), F32),
        compiler_params=_params(("parallel",)),
    )(o_fox, do_fox)


def _fox_bwd(proj, c_col, c_row, lse, delta, do_fox, t, fd, c_fq, c_fkv):
    tb = _pick(t, 384, LANES)
    nb = t // tb
    scale = fd ** -0.5
    sp = _fox_specs(tb, fd, c_fq, c_fkv)
    qt, kt, npairs = _fox_pairs(nb, by_key=True)
    gw = FOX_GROUP * fd

    def body(qt_ref, kt_ref, q_ref, kv_ref, cq_ref, ck_ref, lse_ref, dl_ref, do_ref, dq_ref, dkv_ref, dc_ref, dr_ref,
             dq_acc, dk_acc, dv_acc, dc_acc, dr_acc):
        p = pl.program_id(1)
        qi, ki = qt_ref[p], kt_ref[p]

        @pl.when(p == 0)
        def _():
            dq_acc[...] = jnp.zeros_like(dq_acc)
            dr_acc[...] = jnp.zeros_like(dr_acc)

        @pl.when(qi == ki)
        def _():
            dk_acc[...] = jnp.zeros_like(dk_acc)
            dv_acc[...] = jnp.zeros_like(dv_acc)
            dc_acc[...] = jnp.zeros_like(dc_acc)

        rows = pl.ds(pl.multiple_of(qi * tb, LANES), tb)

        def update(masked):
            mask = _fox_mask(qi, ki, tb) if masked else None
            for hh, (q, k, v) in enumerate(_fox_heads(q_ref, kv_ref, fd)):
                do = _bf(do_ref[:, hh * fd:(hh + 1) * fd])
                s = _dot(q, k, NT) * scale + cq_ref[hh] - ck_ref[hh]
                if masked:
                    s = jnp.where(mask, s, MASK_VALUE)
                pr = jnp.exp(s - lse_ref[hh])
                dp = _dot(do, v, NT)
                ds = pr * (dp - dl_ref[hh])
                ds16 = ds.astype(BF16)
                dv_acc[hh] += _dot(pr.astype(BF16), do, TN)
                dk_acc[hh] += _dot(ds16, q, TN)
                dc_acc[hh] += _colsum(ds)
                dr_acc[hh, rows, :] += jnp.sum(ds, axis=-1, keepdims=True)
                dq_acc[hh, rows, :] += _dot(ds16, k, NN)

        edge = (ki == 0) | (ki == qi)
        pl.when(edge)(functools.partial(update, True))
        pl.when(jnp.logical_not(edge))(functools.partial(update, False))

        @pl.when(qi == nb - 1)
        def _():
            for hh in range(FOX_GROUP):
                dkv_ref[:, 2 * hh * fd:(2 * hh + 1) * fd] = (dk_acc[hh] * scale).astype(dkv_ref.dtype)
                dkv_ref[:, (2 * hh + 1) * fd:(2 * hh + 2) * fd] = dv_acc[hh].astype(dkv_ref.dtype)
                dc_ref[hh] = -dc_acc[hh]

        @pl.when(p == npairs - 1)
        def _():
            for hh in range(FOX_GROUP):
                dq_ref[:, hh * fd:(hh + 1) * fd] = (dq_acc[hh] * scale).astype(dq_ref.dtype)
            dr_ref[...] = dr_acc[...]

    return pl.pallas_call(
        body, name="fox_bwd",
        grid_spec=pltpu.PrefetchScalarGridSpec(
            num_scalar_prefetch=2, grid=(FOX_HEADS // FOX_GROUP, npairs),
            in_specs=[sp["q"], sp["kv"], sp["col"], sp["row"], sp["col"], sp["col"], sp["head"]],
            out_specs=[pl.BlockSpec((t, gw), lambda g, p, qt, kt: (0, g)), sp["key_kv"], sp["row"],
                       pl.BlockSpec((FOX_GROUP, t, 1), lambda g, p, qt, kt: (g, 0, 0))],
            scratch_shapes=[pltpu.VMEM((FOX_GROUP, t, fd), F32), pltpu.VMEM((FOX_GROUP, tb, fd), F32),
                            pltpu.VMEM((FOX_GROUP, tb, fd), F32), pltpu.VMEM((FOX_GROUP, 1, tb), F32),
                            pltpu.VMEM((FOX_GROUP, t, 1), F32)]),
        out_shape=[jax.ShapeDtypeStruct((t, FOX_HEADS * fd), BF16), jax.ShapeDtypeStruct((t, 2 * FOX_HEADS * fd), BF16),
                   jax.ShapeDtypeStruct((FOX_HEADS, 1, t), F32), jax.ShapeDtypeStruct((FOX_HEADS, t, 1), F32)],
        compiler_params=_params(("parallel", "arbitrary")),
    )(qt, kt, proj, proj, c_col, c_row, lse, delta, do_fox)


def _gla_gate_fwd(small, w_alpha_p, b_alpha, t, gk):
    def fn(row0, s, w, b):
        z = _dot(s.astype(BF16), w, NN) + b
        return (jnp.where(_row_ids(row0, s.shape[0]) >= PAD, _log_sigmoid(z) * (1.0 / GLA_TAU), 0.0),)

    return _ew("gla_gate_fwd", fn, [("tile", small, LANES, 0), ("full", w_alpha_p), ("full", b_alpha)],
               [("tile", F32, gk)], t, 264)


def _gla_gate_bwd(dglog, small, w_alpha_p, b_alpha, t, gk):
    def fn(row0, dg, s, w, b):
        z = _dot(s.astype(BF16), w, NN) + b
        dz = jnp.where(_row_ids(row0, s.shape[0]) >= PAD, dg * (1.0 / GLA_TAU) * _sigmoid(-z), 0.0)
        return dz, _colsum(dz)

    return _ew("gla_gate_bwd", fn,
               [("tile", dglog, gk, 0), ("tile", small, LANES, 0), ("full", w_alpha_p), ("full", b_alpha)],
               [("tile", BF16, gk), ("acc", F32, gk)], t, 264)


def _gla_chunk(q, k, g, scale, cs):
    r = lax.broadcasted_iota(jnp.int32, (cs, cs), 0)
    c = lax.broadcasted_iota(jnp.int32, (cs, cs), 1)
    causal = c <= r
    b = _tri_dot(causal.astype(BF16), g)
    bl = b[cs - 1:cs, :]
    eb, einv, eend = jnp.exp(b), jnp.exp(-b), jnp.exp(bl - b)
    qd = q.astype(F32) * scale * eb
    kf = k.astype(F32)
    return causal, (eb, einv, eend), bl, qd, kf * einv, kf * eend


def _gla_fwd(proj, glog, t, dk, dv, c_q, c_k, c_v):
    cs = GLA_CHUNK
    nc = t // cs
    wk, wv = GLA_HEADS * dk, GLA_HEADS * dv
    scale = dk ** -0.5

    def body(q_ref, k_ref, v_ref, g_ref, o_ref, sp_ref, st_ref):
        @pl.when(pl.program_id(0) == 0)
        def _():
            st_ref[...] = jnp.zeros_like(st_ref)

        key = lambda h: slice(h * dk, (h + 1) * dk)
        chunks = [_gla_chunk(q_ref[:, key(h)], k_ref[:, key(h)], g_ref[:, key(h)], scale, cs)
                  for h in range(GLA_HEADS)]
        for h in range(GLA_HEADS):
            vs = slice(h * dv, (h + 1) * dv)
            v = v_ref[:, vs]
            causal, _, bl, qd, ki, ke = chunks[h]
            st = st_ref[h]
            sp_ref[h] = st
            a = jnp.where(causal, _dot(qd.astype(BF16), ki.astype(BF16), NT), 0.0)
            o_ref[:, vs] = _dot(a.astype(BF16), v, NN) + _dot(qd.astype(BF16), st.astype(BF16), NT)
            st_ref[h] = st * jnp.exp(bl) + _dot(v, ke.astype(BF16), TN)

    return pl.pallas_call(
        body, name="gla_fwd", grid=(nc,),
        in_specs=[pl.BlockSpec((cs, wk), lambda n: (n, c_q // wk)), pl.BlockSpec((cs, wk), lambda n: (n, c_k // wk)),
                  pl.BlockSpec((cs, wv), lambda n: (n, c_v // wv)), pl.BlockSpec((cs, wk), lambda n: (n, 0))],
        out_specs=[pl.BlockSpec((cs, wv), lambda n: (n, 0)),
                   pl.BlockSpec((None, GLA_HEADS, dv, dk), lambda n: (n, 0, 0, 0))],
        out_shape=[jax.ShapeDtypeStruct((t, wv), F32), jax.ShapeDtypeStruct((nc, GLA_HEADS, dv, dk), F32)],
        scratch_shapes=[pltpu.VMEM((GLA_HEADS, dv, dk), F32)],
        compiler_params=_params(("arbitrary",)),
    )(proj, proj, proj, glog)


def _gla_bwd(proj, glog, s_prev, do_raw, t, dk, dv, c_q, c_k, c_v):
    cs = GLA_CHUNK
    nc = t // cs
    wk, wv = GLA_HEADS * dk, GLA_HEADS * dv
    scale = dk ** -0.5

    def body(q_ref, k_ref, v_ref, g_ref, sp_ref, do_ref, dq_ref, dk_ref, dv_ref, dg_ref, dst_ref):
        @pl.when(pl.program_id(0) == 0)
        def _():
            dst_ref[...] = jnp.zeros_like(dst_ref)

        key = lambda h: slice(h * dk, (h + 1) * dk)
        chunks = [_gla_chunk(q_ref[:, key(h)], k_ref[:, key(h)], g_ref[:, key(h)], scale, cs)
                  for h in range(GLA_HEADS)]
        for h in range(GLA_HEADS):
            ks, vs = key(h), slice(h * dv, (h + 1) * dv)
            v = v_ref[:, vs]
            do = do_ref[:, vs].astype(BF16)
            causal, (eb, einv, eend), bl, qd, ki, ke = chunks[h]
            qd16, ki16, ke16 = qd.astype(BF16), ki.astype(BF16), ke.astype(BF16)
            st = sp_ref[h]
            dst = dst_ref[h]
            dst16 = dst.astype(BF16)
            a = jnp.where(causal, _dot(qd16, ki16, NT), 0.0).astype(BF16)
            da = jnp.where(causal, _dot(do, v, NT), 0.0).astype(BF16)
            dvv = _dot(a, do, TN) + _dot(ke16, dst16, NT)
            dqd = _dot(da, ki16, NN) + _dot(do, st.astype(BF16), NN)
            dki = _dot(da, qd16, TN)
            dke = _dot(v, dst16, NN)
            dl = jnp.exp(bl)
            ddl = _colsum(dst * st)
            dst_ref[h] = dst * dl + _dot(do, qd16, TN)
            dq_ref[:, ks] = (dqd * eb * scale).astype(dq_ref.dtype)
            dk_ref[:, ks] = (dki * einv + dke * eend).astype(dk_ref.dtype)
            dv_ref[:, vs] = dvv.astype(dv_ref.dtype)
            db = dqd * qd - dki * ki - dke * ke
            db_last = _colsum(dke * ke) + ddl * dl
            r = lax.broadcasted_iota(jnp.int32, (cs, cs), 0)
            c = lax.broadcasted_iota(jnp.int32, (cs, cs), 1)
            dg_ref[:, ks] = _tri_dot((c >= r).astype(BF16), db) + db_last

    rev = lambda f: (lambda n: f(nc - 1 - n))
    return pl.pallas_call(
        body, name="gla_bwd", grid=(nc,),
        in_specs=[pl.BlockSpec((cs, wk), rev(lambda n: (n, c_q // wk))), pl.BlockSpec((cs, wk), rev(lambda n: (n, c_k // wk))),
                  pl.BlockSpec((cs, wv), rev(lambda n: (n, c_v // wv))), pl.BlockSpec((cs, wk), rev(lambda n: (n, 0))),
                  pl.BlockSpec((None, GLA_HEADS, dv, dk), rev(lambda n: (n, 0, 0, 0))),
                  pl.BlockSpec((cs, wv), rev(lambda n: (n, 0)))],
        out_specs=[pl.BlockSpec((cs, wk), rev(lambda n: (n, 0))), pl.BlockSpec((cs, wk), rev(lambda n: (n, 0))),
                   pl.BlockSpec((cs, wv), rev(lambda n: (n, 0))), pl.BlockSpec((cs, wk), rev(lambda n: (n, 0)))],
        out_shape=[jax.ShapeDtypeStruct((t, wk), BF16), jax.ShapeDtypeStruct((t, wk), BF16),
                   jax.ShapeDtypeStruct((t, wv), BF16), jax.ShapeDtypeStruct((t, wk), F32)],
        scratch_shapes=[pltpu.VMEM((GLA_HEADS, dv, dk), F32)],
        compiler_params=_params(("arbitrary",)),
    )(proj, proj, proj, glog, s_prev, do_raw)


def _gla_post_fwd(o_raw, proj, gn, t, dv, c_gr):
    w = GLA_HEADS * dv

    def fn(row0, o, gr, g):
        gr = gr.astype(F32)
        outs = []
        for h in range(GLA_HEADS):
            oh = o[:, h * dv:(h + 1) * dv]
            outs.append(oh * lax.rsqrt(jnp.mean(oh * oh, axis=-1, keepdims=True) + EPS))
        on = jnp.concatenate(outs, axis=1) * g
        return (on * (gr * _sigmoid(gr)),)

    return _ew("gla_post_fwd", fn, [("tile", o_raw, w, 0), ("tile", proj, w, c_gr), ("full", gn)],
               [("tile", BF16, w)], t, 264)


def _gla_post_bwd(o_raw, proj, gn, do_gla, t, dv, c_gr):
    w = GLA_HEADS * dv

    def fn(row0, o, gr, g, do):
        gr = gr.astype(F32)
        sg = _sigmoid(gr)
        don = do * (gr * sg)
        ohs, dos = [], []
        for h in range(GLA_HEADS):
            sl = slice(h * dv, (h + 1) * dv)
            oh = o[:, sl]
            r = lax.rsqrt(jnp.mean(oh * oh, axis=-1, keepdims=True) + EPS)
            xh = oh * r
            dxh = don[:, sl] * g[:, sl]
            ohs.append(xh)
            dos.append(r * (dxh - xh * jnp.mean(dxh * xh, axis=-1, keepdims=True)))
        xh = jnp.concatenate(ohs, axis=1)
        dgr = do * (xh * g) * (sg * (1.0 + gr * (1.0 - sg)))
        return jnp.concatenate(dos, axis=1), dgr, _colsum(don * xh)

    return _ew("gla_post_bwd", fn,
               [("tile", o_raw, w, 0), ("tile", proj, w, c_gr), ("full", gn), ("tile", do_gla, w, 0)],
               [("tile", F32, w), ("tile", BF16, w), ("acc", F32, w)], t, 264)


def _adamw(name, w, g, m, v, layer=None, into=None):
    nl, rows, cols = w.shape
    tm = _pick(rows, max(8, (512 * 1024) // max(cols, 1) // 8 * 8), 8)

    def body(w_ref, g_ref, m_ref, v_ref, *rest):
        go_ref, d_ref, nm_ref, nv_ref = rest[-4:]
        gg = g_ref[...]
        nm = ADAM_B1 * m_ref[...] + (1.0 - ADAM_B1) * gg
        nv = ADAM_B2 * v_ref[...] + (1.0 - ADAM_B2) * (gg * gg)
        m_hat = nm / (1.0 - ADAM_B1 ** ADAM_STEP)
        v_hat = nv / (1.0 - ADAM_B2 ** ADAM_STEP)
        go_ref[...] = gg
        d_ref[...] = -ADAM_LR * (m_hat / (jnp.sqrt(v_hat) + ADAM_EPS) + ADAM_WD * w_ref[...])
        nm_ref[...] = nm
        nv_ref[...] = nv

    out_shape = [jax.ShapeDtypeStruct((nl, rows, cols), F32)] * 4
    if layer is None:
        spec = pl.BlockSpec((None, tm, cols), lambda l, i: (l, i, 0))
        return pl.pallas_call(
            body, name=name, grid=(nl, rows // tm), in_specs=[spec] * 4, out_specs=[spec] * 4, out_shape=out_shape,
            compiler_params=_params(("parallel", "parallel")),
        )(w, g, m, v)
    spec = pl.BlockSpec((None, tm, cols), lambda i: (layer, i, 0))
    in_specs = [spec, pl.BlockSpec((tm, cols), lambda i: (i, 0)), spec, spec]
    args, aliases = [w, g, m, v], {}
    if into is not None:
        in_specs += [pl.BlockSpec(memory_space=pl.ANY)] * 4
        args += list(into)
        aliases = {4 + k: k for k in range(4)}
    return pl.pallas_call(
        body, name=name, grid=(rows // tm,), in_specs=in_specs, out_specs=[spec] * 4, out_shape=out_shape,
        input_output_aliases=aliases, compiler_params=_params(("parallel",)),
    )(*args)


def _me():
    return lax.axis_index("x"), lax.axis_index("y"), lax.axis_index("c")


def _hbm_specs(n):
    return [pl.BlockSpec(memory_space=pl.ANY)] * n


_HBM = pl.BlockSpec(memory_space=pltpu.HBM)
_SEM = pl.BlockSpec(memory_space=pltpu.SEMAPHORE)
_EFFECT = pltpu.SideEffectType.DATAFLOW_SIDE_EFFECTING
N_PEERS = N_CHIPS - 1


def _other_chips(x, y):
    return [(1 - x, y), (x, 1 - y), (1 - x, 1 - y)]


def _split_copies(plan, src, land, send_sems, recv_sems):
    me = _me()
    copies = []
    for i in range(len(src)):
        for j, (s, d, peer) in enumerate(plan(src[i], land[i], me)):
            k = plan.copies * i + j
            copies.append(pltpu.make_async_remote_copy(
                src_ref=s, dst_ref=d, send_sem=send_sems.at[k], recv_sem=recv_sems.at[k], device_id=peer,
                device_id_type=MESH))
    return copies


def _split_start(name, srcs, lands, plan, after=None):
    n = len(srcs)
    extra = [] if after is None else [after]

    def body(*refs):
        src, land = refs[:n], refs[n:2 * n]
        send_sems, recv_sems = refs[2 * n + len(extra)], refs[2 * n + len(extra) + 1]
        token = refs[-1]
        for cp in _split_copies(plan, src, land, send_sems, recv_sems):
            cp.start()
        token[...] = jnp.zeros_like(token)

    out_shape = ([pltpu.SemaphoreType.DMA((plan.copies * n,)), pltpu.SemaphoreType.DMA((plan.copies * n,))]
                 + [pltpu.HBM(a.shape, a.dtype) for a in list(srcs) + list(lands)]
                 + [jax.ShapeDtypeStruct((8, LANES), F32)])
    res = pl.pallas_call(
        body, name=name, out_shape=out_shape,
        in_specs=[_HBM] * (2 * n) + [pl.BlockSpec(memory_space=pl.ANY)] * len(extra),
        out_specs=[_SEM, _SEM] + [_HBM] * (2 * n) + [pl.BlockSpec(memory_space=pltpu.VMEM)],
        input_output_aliases={i: 2 + i for i in range(2 * n)},
        compiler_params=pltpu.CompilerParams(has_side_effects=_EFFECT),
    )(*[pltpu.with_memory_space_constraint(a, pltpu.HBM) for a in list(srcs) + list(lands)], *extra)
    return res[0], res[1], res[2:2 + n], res[2 + n:2 + 2 * n], res[-1]


def _split_wait(name, send_sems, recv_sems, srcs, lands, plan, after):
    n = len(srcs)

    def body(*refs):
        src, land = refs[:n], refs[n:2 * n]
        s_sems, r_sems = refs[2 * n], refs[2 * n + 1]
        for cp in _split_copies(plan, src, land, s_sems, r_sems):
            cp.wait_send()
            cp.wait_recv()

    res = pl.pallas_call(
        body, name=name, out_shape=[pltpu.HBM(a.shape, a.dtype) for a in list(srcs) + list(lands)],
        in_specs=[_HBM] * (2 * n) + [_SEM, _SEM, pl.BlockSpec(memory_space=pl.ANY)], out_specs=[_HBM] * (2 * n),
        input_output_aliases={i: i for i in range(2 * n)},
        compiler_params=pltpu.CompilerParams(has_side_effects=_EFFECT),
    )(*srcs, *lands, send_sems, recv_sems, after)
    return res[:n], res[n:]


def _exchange_start(name, bufs, plan, after):
    n = len(bufs)

    def body(*refs):
        buf = refs[:n]
        for cp in _split_copies(plan, buf, buf, refs[n + 1], refs[n + 2]):
            cp.start()
        refs[-1][...] = jnp.zeros_like(refs[-1])

    out_shape = ([pltpu.SemaphoreType.DMA((plan.copies * n,)), pltpu.SemaphoreType.DMA((plan.copies * n,))]
                 + [pltpu.HBM(a.shape, a.dtype) for a in bufs] + [jax.ShapeDtypeStruct((8, LANES), F32)])
    res = pl.pallas_call(
        body, name=name, out_shape=out_shape, in_specs=[_HBM] * n + [pl.BlockSpec(memory_space=pl.ANY)],
        out_specs=[_SEM, _SEM] + [_HBM] * n + [pl.BlockSpec(memory_space=pltpu.VMEM)],
        input_output_aliases={i: 2 + i for i in range(n)},
        compiler_params=pltpu.CompilerParams(has_side_effects=_EFFECT),
    )(*[pltpu.with_memory_space_constraint(a, pltpu.HBM) for a in bufs], after)
    return res[0], res[1], res[2:2 + n], res[-1]


def _exchange_wait(name, send_sems, recv_sems, bufs, plan, after):
    n = len(bufs)

    def body(*refs):
        buf = refs[:n]
        for cp in _split_copies(plan, buf, buf, refs[n], refs[n + 1]):
            cp.wait_send()
            cp.wait_recv()

    return pl.pallas_call(
        body, name=name, out_shape=[pltpu.HBM(a.shape, a.dtype) for a in bufs],
        in_specs=[_HBM] * n + [_SEM, _SEM, pl.BlockSpec(memory_space=pl.ANY)], out_specs=[_HBM] * n,
        input_output_aliases={i: i for i in range(n)},
        compiler_params=pltpu.CompilerParams(has_side_effects=_EFFECT),
    )(*bufs, send_sems, recv_sems, after)


def _forward_plan(src, land, me):
    x, y, c = me
    rows = _half(src.shape[1], c)
    return [(src.at[2 * px + py, rows], land.at[2 * px + py, rows], (x, y, 1 - c)) for px, py in _other_chips(x, y)]


def _share_plan(src, land, me):
    x, y, c = me
    rows = _half(src.shape[0], c)
    return [(src.at[rows], land.at[rows], (x, y, 1 - c))]


_forward_plan.copies = N_CHIPS - 1
_share_plan.copies = 1


def _half(ref_rows, c):
    half = ref_rows // 2
    return pl.ds(c * half, half)


def _gather_plan(src, land, me):
    x, y, c = me
    rows = _half(src.shape[0], c)
    return [(src.at[rows], land.at[2 * x + y, rows], (px, py, c)) for px, py in _other_chips(x, y)]


def _scatter_plan(src, land, me):
    x, y, c = me
    return [(src.at[2 * px + py], land.at[2 * x + y], (px, py, c)) for px, py in _other_chips(x, y)]


def _swap_plan(src, land, me):
    x, y, c = me
    return [(src.at[:, _half(src.shape[1], 1 - c)], land, (x, y, 1 - c))]


_gather_plan.copies = N_PEERS
_scatter_plan.copies = N_PEERS
_swap_plan.copies = 1


def _all_gather_small(v):
    def body(v_ref, out_ref, send_sems, recv_sems, local_sem):
        x, y, c = _me()
        mine = pltpu.make_async_copy(v_ref, out_ref.at[4 * x + 2 * y + c], local_sem)
        mine.start()
        copies = []
        for k in range(1, N_DEV):
            peer = (x ^ ((k >> 2) & 1), y ^ ((k >> 1) & 1), c ^ (k & 1))
            cp = pltpu.make_async_remote_copy(
                src_ref=v_ref, dst_ref=out_ref.at[4 * x + 2 * y + c], send_sem=send_sems.at[k - 1],
                recv_sem=recv_sems.at[k - 1], device_id=peer, device_id_type=MESH)
            cp.start()
            copies.append(cp)
        for cp in copies:
            cp.wait_recv()
        for cp in copies:
            cp.wait_send()
        mine.wait()

    return pl.pallas_call(
        body, name="all_gather_small", in_specs=_hbm_specs(1), out_specs=pl.BlockSpec(memory_space=pl.ANY),
        out_shape=jax.ShapeDtypeStruct((N_DEV,) + v.shape, v.dtype),
        scratch_shapes=[pltpu.SemaphoreType.DMA((N_DEV - 1,)), pltpu.SemaphoreType.DMA((N_DEV - 1,)),
                        pltpu.SemaphoreType.DMA],
    )(v)


def _pair_sum(name, own, recv, chip, core):
    nch, half, cdim = recv.shape
    tm = _pick(half, max(8, (512 * 1024) // cdim // 16 * 16), 16)
    nt = half // tm

    def body(c_ref, k_ref, a_ref, b_ref, s16_ref):
        s16_ref[...] = (a_ref[...] + b_ref[...]).astype(BF16)

    other = lambda j, c: (c[0] + 1 + j) % nch
    spec = pl.BlockSpec((None, tm, cdim), lambda j, i, c, k: (other(j, c), i, 0))
    return pl.pallas_call(
        body, name=name,
        grid_spec=pltpu.PrefetchScalarGridSpec(
            num_scalar_prefetch=2, grid=(nch - 1, nt),
            in_specs=[pl.BlockSpec((None, tm, cdim), lambda j, i, c, k: (other(j, c), k[0] * nt + i, 0)), spec],
            out_specs=spec),
        out_shape=jax.ShapeDtypeStruct((nch, half, cdim), BF16),
        compiler_params=_params(("parallel", "parallel")),
    )(chip, core, own, recv)


def _chip_sum(name, own, recv, landed, chip, core):
    nch, half, cdim = recv.shape
    tm = _pick(half, max(8, (512 * 1024) // cdim // 16 * 16), 16)
    nt = half // tm

    def body(c_ref, k_ref, own_ref, recv_ref, *rest):
        landed_refs, out_ref = rest[:nch], rest[-1]
        me = c_ref[0]
        mine = own_ref[...] + recv_ref[...]
        acc = None
        for j in range(nch):
            term = jnp.where(me == j, mine, landed_refs[j][...].astype(F32))
            acc = term if acc is None else acc + term
        out_ref[...] = acc

    landed_specs = [pl.BlockSpec((None, tm, cdim), functools.partial(lambda i, c, k, j: (j, i, 0), j=j))
                    for j in range(nch)]
    return pl.pallas_call(
        body, name=name,
        grid_spec=pltpu.PrefetchScalarGridSpec(
            num_scalar_prefetch=2, grid=(nt,),
            in_specs=[pl.BlockSpec((None, tm, cdim), lambda i, c, k: (c[0], k[0] * nt + i, 0)),
                      pl.BlockSpec((None, tm, cdim), lambda i, c, k: (c[0], i, 0))] + landed_specs,
            out_specs=pl.BlockSpec((tm, cdim), lambda i, c, k: (k[0] * nt + i, 0))),
        out_shape=jax.ShapeDtypeStruct((2 * half, cdim), F32),
        compiler_params=_params(("parallel",)),
    )(chip, core, own, recv, *([landed] * nch))


def _sum_devices(gathered):
    _, r, cdim = gathered.shape

    def body(g_ref, o_ref):
        acc = g_ref[0]
        for k in range(1, N_DEV):
            acc = acc + g_ref[k]
        o_ref[...] = acc

    return pl.pallas_call(
        body, name="sum_devices", out_shape=jax.ShapeDtypeStruct((r, cdim), F32),
        compiler_params=_params(),
    )(gathered)


class _Layout:
    def __init__(self, d):
        self.d = d
        self.fw = d // 2
        self.fd = self.fw // FOX_HEADS
        self.gk = d // 2
        self.gv = d
        self.dk = self.gk // GLA_HEADS
        self.dv = self.gv // GLA_HEADS
        self.c_fq = 0
        self.c_gq = self.fw
        self.c_gv = self.c_gq + self.gk
        self.c_gr = self.c_gv + self.gv
        self.c_fkv = self.c_gr + self.gv
        self.c_gates = self.c_fkv + 2 * self.fw
        self.c_gk = self.c_gates + 2 * d
        self.c_small = self.c_gk + self.gk
        self.n_main = self.c_small
        self.n_p = self.c_small + LANES
        self.o_fk = self.fw
        self.o_fv = 2 * self.fw
        self.o_ff = 3 * self.fw
        self.o_gq = self.o_ff + FOX_HEADS
        self.o_gk = self.o_gq + self.gk
        self.o_gv = self.o_gk + self.gk
        self.o_gr = self.o_gv + self.gv
        self.o_ga = self.o_gr + self.gv
        self.o_gf = self.o_ga + GLA_RANK
        self.o_gg = self.o_gf + d
        self.n_orig = self.o_gg + d

    def to_p(self, shards):
        per = self.n_orig // N_CHIPS
        ranges = [(0, self.fw), (self.o_gq, self.gk), (self.o_gv, self.gv), (self.o_gr, self.gv)]
        for h in range(FOX_HEADS):
            ranges += [(self.o_fk + h * self.fd, self.fd), (self.o_fv + h * self.fd, self.fd)]
        ranges += [(self.o_gf, 2 * self.d), (self.o_gk, self.gk), (self.o_ff, FOX_HEADS), (self.o_ga, GLA_RANK)]
        pieces = []
        for a, width in ranges:
            for j in range(a // per, (a + width - 1) // per + 1):
                lo, hi = max(a, j * per), min(a + width, (j + 1) * per)
                pieces.append(shards[j][:, lo - j * per:hi - j * per])
        pieces.append(jnp.zeros((shards.shape[1], LANES - FOX_HEADS - GLA_RANK), shards.dtype))
        return jnp.concatenate(pieces, axis=1)

    def from_segments(self, seg):
        per = self.n_orig // N_CHIPS
        fd = self.fd
        atoms = [("fq", 0, self.fw)]
        atoms += [("fkv", 2 * h * fd, fd) for h in range(FOX_HEADS)]
        atoms += [("fkv", (2 * h + 1) * fd, fd) for h in range(FOX_HEADS)]
        atoms += [("small", 0, FOX_HEADS), ("gq", 0, self.gk), ("gk", 0, self.gk), ("gv", 0, self.gv),
                  ("gr", 0, self.gv), ("small", FOX_HEADS, GLA_RANK), ("gates", 0, 2 * self.d)]
        shards = [[] for _ in range(N_CHIPS)]
        pos = 0
        for name, c0, width in atoms:
            for j in range(pos // per, (pos + width - 1) // per + 1):
                lo, hi = max(pos, j * per), min(pos + width, (j + 1) * per)
                shards[j].append(seg[name][:, c0 + lo - pos:c0 + hi - pos])
            pos += width
        assert pos == self.n_orig
        return jnp.stack([jnp.concatenate(s, axis=1) for s in shards])


def _layer_fwd(lay, h, p, t, hooks=None):
    hooks = hooks or {}
    d = lay.d
    xn, xn_t = _rms_fwd_t("rms_mix_fwd", h, p["norm_mix_g"], t, d)
    proj = _mm("mm_proj", xn, p["w_in"], mode="nn", m=t, n=lay.n_main, k=d, out_dtypes=(BF16,))
    small = _mm("mm_small", xn, p["w_in"], mode="nn", m=t, n=LANES, k=d, b_c0=lay.c_small)
    cs = _fox_gate_fwd(small, p["b_forget_p"], t)
    ct = cs[:, :FOX_HEADS].T
    c_col, c_row = ct[:, :, None], ct[:, None, :]
    o_fox, lse = _fox_fwd(proj, c_col, c_row, t, lay.fd, lay.c_fq, lay.c_fkv)
    b_alpha = p["b_alpha"] + hooks["mixers"](o_fox) if "mixers" in hooks else p["b_alpha"]
    glog = _gla_gate_fwd(small, p["w_alpha_p"], b_alpha, t, lay.gk)
    o_raw, s_prev = _gla_fwd(proj, glog, t, lay.dk, lay.dv, lay.c_gq, lay.c_gk, lay.c_gv)
    o_gla = _gla_post_fwd(o_raw, proj, p["gla_norm_g"], t, lay.dv, lay.c_gr)
    if "late" in hooks:
        p.update(hooks["late"](o_gla))
    a_fox = _mm("mm_o_fox", o_fox, p["w_o_fox"], mode="nn", m=t, n=d, k=lay.fw, b_shards=N_CHIPS)
    a_gla = _mm("mm_o_gla", o_gla, p["w_o_gla"], mode="nn", m=t, n=d, k=lay.gv)
    y = _merge_fwd(a_fox, a_gla, proj, lay.c_gates, t, d)
    h1 = _mm("mm_out", y, p["w_out"], mode="nn", m=t, n=d, k=d, extras=[h], epilogue=lambda acc, res: (res + acc,))
    xn2, xn2_t = _rms_fwd_t("rms_mlp_fwd", h1, p["norm_mlp_g"], t, d)
    u, act = _mm("mm_ff1", xn2, p["w_ff1"], mode="nn", m=t, n=4 * d, k=d, out_dtypes=(BF16, BF16), b_shards=N_CHIPS,
                 epilogue=lambda acc: (acc, jnp.square(jnp.maximum(acc, 0.0))))
    if "mlp" in hooks:
        hooks["mlp"](act)
    h2 = _mm("mm_ff2", act, p["w_ff2"], mode="nn", m=t, n=d, k=4 * d, extras=[h1],
             epilogue=lambda acc, res: (res + acc,))
    saved = dict(h=h, xn_t=xn_t, proj=proj, small=small, c_col=c_col, c_row=c_row, o_fox=o_fox, lse=lse, glog=glog,
                 o_raw=o_raw, s_prev=s_prev, o_gla=o_gla, a_fox=a_fox, a_gla=a_gla, y=y, h1=h1, xn2_t=xn2_t, u=u, act=act)
    return h2, saved


def _layer_bwd(lay, dh2, p, s, t, gates=None, first=False):
    d = lay.d
    g = {}

    def gated(gain, point):
        return gain + gates[point](g) if gates and point in gates else gain
    dh2, dh2_16 = dh2
    du = _mm("mm_dact", dh2_16, p["w_ff2"], mode="nt", m=t, n=4 * d, k=d, extras=[s["u"]], out_dtypes=(BF16,),
             epilogue=lambda acc, u: (acc * (2.0 * jnp.maximum(u.astype(F32), 0.0)),))
    g["w_ff2"] = _mm("mm_dw_ff2", s["act"], dh2_16, mode="tn", m=4 * d, n=d, k=t)
    g["w_ff1"] = _mm("mm_dw_ff1", s["xn2_t"], du, mode="nn", m=d, n=4 * d, k=t, tk=t, out_shards=N_CHIPS)
    dxn2 = _mm("mm_dxn2", du, p["w_ff1"], mode="nt", m=t, n=d, k=4 * d, b_shards=N_CHIPS)
    dh1, dh1_16, g["norm_mlp_g"] = _rms_bwd("rms_mlp_bwd", s["h1"], gated(p["norm_mlp_g"], "mlp"), dxn2, dh2, t, d)
    dy = _mm("mm_dy", dh1_16, p["w_out"], mode="nt", m=t, n=d, k=d)
    g["w_out"] = _mm("mm_dw_out", s["y"], dh1_16, mode="tn", m=d, n=d, k=t)
    da_fox, da_gla, dgates = _merge_bwd(dy, s["a_fox"], s["a_gla"], s["proj"], lay.c_gates, t, d)
    g["w_o_fox"] = _mm("mm_dw_o_fox", s["o_fox"], da_fox, mode="tn", m=lay.fw, n=d, k=t, out_shards=N_CHIPS)
    do_fox = _mm("mm_do_fox", da_fox, p["w_o_fox"], mode="nt", m=t, n=lay.fw, k=d, b_shards=N_CHIPS)
    g["w_o_gla"] = _mm("mm_dw_o_gla", s["o_gla"], da_gla, mode="tn", m=lay.gv, n=d, k=t)
    do_gla = _mm("mm_do_gla", da_gla, p["w_o_gla"], mode="nt", m=t, n=lay.gv, k=d)
    do_raw, dgr, g["gla_norm_g"] = _gla_post_bwd(s["o_raw"], s["proj"], gated(p["gla_norm_g"], "out"), do_gla, t,
                                                 lay.dv, lay.c_gr)
    dgq, dgk, dgv, dglog = _gla_bwd(s["proj"], s["glog"], s["s_prev"], do_raw, t, lay.dk, lay.dv,
                                    lay.c_gq, lay.c_gk, lay.c_gv)
    dz, g["b_alpha"] = _gla_gate_bwd(dglog, s["small"], p["w_alpha_p"], p["b_alpha"], t, lay.gk)
    g["w_alpha_p"] = _mm("mm_dw_alpha", s["small"], dz, mode="tn", m=LANES, n=lay.gk, k=t)
    dga = _mm("mm_dga", dz, p["w_alpha_p"], mode="nt", m=t, n=LANES, k=lay.gk)
    delta = _fox_delta(s["o_fox"], do_fox, t, lay.fd)
    dfq, dfkv, dc, dr = _fox_bwd(s["proj"], s["c_col"], s["c_row"], s["lse"], delta, do_fox, t, lay.fd,
                                 lay.c_fq, lay.c_fkv)
    dc_p = jnp.pad((dc[:, 0, :] + dr[:, :, 0]).T, ((0, 0), (0, LANES - FOX_HEADS)))
    dsmall, g["b_forget_p"] = _fox_gate_bwd(dc_p, s["small"], p["b_forget_p"], dga, t)
    segs = [("fq", dfq, lay.c_fq), ("gq", dgq, lay.c_gq), ("gv", dgv, lay.c_gv), ("gr", dgr, lay.c_gr),
            ("fkv", dfkv, lay.c_fkv), ("gates", dgates, lay.c_gates), ("gk", dgk, lay.c_gk),
            ("small", dsmall, lay.c_small)]
    dw_in = {nm: _mm("mm_dw_in_" + nm, s["xn_t"], dseg, mode="nn", m=d, n=dseg.shape[1], k=t, tk=t)
             for nm, dseg, _ in segs}
    g["w_in"] = lay.from_segments(dw_in)
    dxn = _mm("mm_dxn_gates", dgates, p["w_in"], mode="nt", m=t, n=d, k=2 * d, b_c0=lay.c_gates)
    dxn = _mm("mm_dxn_small", dsmall, p["w_in"], mode="nt", m=t, n=d, k=LANES, b_c0=lay.c_small, extras=[dxn],
              epilogue=lambda acc, prev: (prev + acc,))
    for nm, (a1, c1), (a2, c2) in (("q", (dfq, lay.c_fq), (dgq, lay.c_gq)), ("v", (dgv, lay.c_gv), (dgr, lay.c_gr)),
                                   ("k", (dfkv, lay.c_fkv), (dgk, lay.c_gk))):
        dxn = _mm_nt2("mm_dxn_" + nm, a1, c1, a2, c2, p["w_in"], dxn, m=t, n=d)
    if first:
        dx, head, g["norm_mix_g"] = _rms_bwd_x("rms_mix_bwd_x", s["h"], gated(p["norm_mix_g"], "in"), dxn, dh1, t, d)
        return (dx, head), g
    dh, dh_16, g["norm_mix_g"] = _rms_bwd("rms_mix_bwd", s["h"], gated(p["norm_mix_g"], "in"), dxn, dh1, t, d)
    return (dh, dh_16), g


def _pack_small(d, meta, mix, gla, mlp, final, b_alpha, b_forget, w_alpha2):
    rows = [meta.reshape(N_META, d), mix.reshape(DEPTH, d), gla.reshape(DEPTH, d), mlp.reshape(DEPTH, d),
            final.reshape(1, d), b_alpha.reshape(1, d),
            jnp.pad(b_forget.reshape(1, DEPTH * FOX_HEADS), ((0, 0), (0, d - DEPTH * FOX_HEADS))),
            jnp.zeros((7, d), F32), w_alpha2.reshape(GLA_RANK, d)]
    return jnp.concatenate(rows, axis=0)


def _unpack_small(d, packed):
    return dict(meta=packed[:N_META], norm_mix_g=packed[16:18], gla_norm_g=packed[18:20], norm_mlp_g=packed[20:22],
                final_norm_g=packed[22], b_alpha=packed[23].reshape(DEPTH, d // 2),
                b_forget=packed[24, :DEPTH * FOX_HEADS].reshape(DEPTH, FOX_HEADS),
                w_alpha2=packed[32:48].reshape(DEPTH, GLA_RANK, d // 2))


_BIG = ("w_in", "w_o_fox", "w_o_gla", "w_out", "w_ff1", "w_ff2")
_COL_SHARDED = ("w_in", "w_o_fox", "w_ff1")


def _full_matrix(name, gathered_l):
    nch, r, c = gathered_l.shape
    if name in _COL_SHARDED:
        return gathered_l.transpose(1, 0, 2).reshape(r, nch * c)
    return gathered_l.reshape(nch * r, c)


def _shard_major(name, full):
    r, c = full.shape
    if name in _COL_SHARDED:
        return full.reshape(r, N_CHIPS, c // N_CHIPS).transpose(1, 0, 2)
    return full.reshape(N_CHIPS, r // N_CHIPS, c)


def kernel(x, meta_tokens, norm_mix_g, w_in, b_forget, w_alpha2, b_alpha, gla_norm_g, w_o_fox, w_o_gla, w_out, norm_mlp_g, w_ff1, w_ff2, final_norm_g, loss_target, m_meta_tokens, m_norm_mix_g, m_w_in, m_b_forget, m_w_alpha2, m_b_alpha, m_gla_norm_g, m_w_o_fox, m_w_o_gla, m_w_out, m_norm_mlp_g, m_w_ff1, m_w_ff2, m_final_norm_g, v_meta_tokens, v_norm_mix_g, v_w_in, v_b_forget, v_w_alpha2, v_b_alpha, v_gla_norm_g, v_w_o_fox, v_w_o_gla, v_w_out, v_norm_mlp_g, v_w_ff1, v_w_ff2, v_final_norm_g):
    d = x.shape[2]
    lay = _Layout(d)
    xi, yi, ci = lax.axis_index("x"), lax.axis_index("y"), lax.axis_index("c")
    chip = (2 * xi + yi).astype(jnp.int32)
    w = dict(w_in=w_in, w_alpha2=w_alpha2, w_o_fox=w_o_fox, w_o_gla=w_o_gla, w_out=w_out, w_ff1=w_ff1, w_ff2=w_ff2)
    m = dict(w_in=m_w_in, w_alpha2=m_w_alpha2, w_o_fox=m_w_o_fox, w_o_gla=m_w_o_gla, w_out=m_w_out, w_ff1=m_w_ff1,
             w_ff2=m_w_ff2)
    v = dict(w_in=v_w_in, w_alpha2=v_w_alpha2, w_o_fox=v_w_o_fox, w_o_gla=v_w_o_gla, w_out=v_w_out, w_ff1=v_w_ff1,
             w_ff2=v_w_ff2)

    seq = x.shape[1]
    t = seq + ROW0
    core_idx = ci.astype(jnp.int32)[None]
    chip_idx = chip[None]

    cols = d // N_CHIPS
    small_w = jnp.concatenate([meta_tokens, w_alpha2.reshape(-1, cols)], axis=0)
    small_raw = _all_gather_small(small_w)
    small_all = small_raw[0::2]
    alpha_full = small_all[:, N_META:].reshape(N_CHIPS, DEPTH, GLA_RANK, lay.gk // N_CHIPS)
    alpha_full = alpha_full.transpose(1, 2, 0, 3).reshape(DEPTH, GLA_RANK, lay.gk)
    groups = [(0, ("w_in",)), (0, _BIG[1:]), (1, _BIG)]
    started, after = [], small_raw
    for gi, (l, names) in enumerate(groups):
        own16 = [w[n][l].astype(BF16) for n in names]
        lands = [lax.empty((N_CHIPS,) + o.shape, BF16) for o in own16]
        started.append(_split_start("gather_start_%d" % gi, own16, lands, _gather_plan, after=after))
        after = started[gi][4]
    meta_full = small_all[:, :N_META].transpose(1, 0, 2).reshape(N_META, d) + after[0, 0]

    passing = {}

    def arrive(gi, after):
        send_sems, recv_sems, srcs, lands, _ = started[gi]
        srcs, lands = _split_wait("gather_wait_%d" % gi, send_sems, recv_sems, srcs, lands, _gather_plan, after)
        passing[gi] = (srcs, _exchange_start("gather_pass_%d" % gi, lands, _forward_plan, after=srcs[0]))
        return passing[gi][1][3][0, 0]

    def gathered(gi, after):
        if gi not in passing:
            arrive(gi, after)
        srcs, (send_sems, recv_sems, lands, _) = passing[gi]
        lands = _exchange_wait("gather_pass_wait_%d" % gi, send_sems, recv_sems, lands, _forward_plan, after)
        return {n: lax.dynamic_update_slice(g, o[None], (chip, 0, 0)) for n, g, o in zip(groups[gi][1], lands, srcs)}

    def early_weights(l, gl):
        w_alpha_p = jnp.zeros((LANES, lay.gk), BF16).at[FOX_HEADS:FOX_HEADS + GLA_RANK].set(
            alpha_full[l].astype(BF16))
        return dict(
            w_in=lay.to_p(gl["w_in"]), w_alpha_p=w_alpha_p,
            norm_mix_g=norm_mix_g[l][None], norm_mlp_g=norm_mlp_g[l][None], gla_norm_g=gla_norm_g[l][None],
            b_alpha=b_alpha[l][None],
            b_forget_p=jnp.pad(b_forget[l][None], ((0, 0), (0, LANES - FOX_HEADS))))

    def late_weights(gl):
        return dict(w_o_fox=gl["w_o_fox"], w_o_gla=_full_matrix("w_o_gla", gl["w_o_gla"]),
                    w_out=_full_matrix("w_out", gl["w_out"]), w_ff1=gl["w_ff1"],
                    w_ff2=_full_matrix("w_ff2", gl["w_ff2"]))

    h = jnp.pad(x[0], ((ROW0, 0), (0, 0))).at[PAD:ROW0].set(meta_full)
    layers, saved = [], []
    layers.append(early_weights(0, gathered(0, after=h)))
    h, s = _layer_fwd(lay, h, layers[0], t, hooks=dict(
        mixers=lambda after: arrive(1, after), late=lambda after: late_weights(gathered(1, after)),
        mlp=lambda after: arrive(2, after)))
    saved.append(s)
    gl = gathered(2, after=h)
    layers.append({**early_weights(1, gl), **late_weights(gl)})
    h, s = _layer_fwd(lay, h, layers[1], t)
    saved.append(s)
    dh, dh_16, dg_final, loss_part = _loss_head(h, final_norm_g[None], jnp.pad(loss_target[0], ((ROW0, 0), (0, 0))),
                                                t, d)
    dh = (dh, dh_16)
    loss = lax.psum(loss_part[0, 0], ("x", "y", "c"))

    def partial_of(g, n):
        return g[n] if n in ("w_ff1", "w_o_fox", "w_in") else _shard_major(n, g[n])

    scatter_groups = dict(mlp=("w_ff1", "w_ff2"), out=("w_o_fox", "w_o_gla", "w_out"))
    scatter_groups["in"] = ("w_in",)
    swapping, scattered = [], {}

    def start_swap(l, grp, g, after=None):
        parts = [partial_of(g, n) for n in scatter_groups[grp]]
        lands = [lax.empty((p_.shape[0], p_.shape[1] // 2, p_.shape[2]), F32) for p_ in parts]
        started_swap = _split_start("swap_start_%d_%s" % (l, grp), parts, lands, _swap_plan, after=after)
        swapping.append((l, grp, started_swap))
        return started_swap[4]

    def start_scatter(after):
        l, grp, (send_sems, recv_sems, srcs, lands, _) = swapping.pop(0)
        names = scatter_groups[grp]
        tag = "%d_%s" % (l, grp)
        parts, from_sibling = _split_wait("swap_wait_" + tag, send_sems, recv_sems, srcs, lands, _swap_plan, after)
        sums = [_pair_sum("pair_sum_%d_%s" % (l, n), p_, r_, chip_idx, core_idx)
                for n, p_, r_ in zip(names, parts, from_sibling)]
        lands = [lax.empty(s16.shape, BF16) for s16 in sums]
        send_sems, recv_sems, srcs, lands, token = _split_start("scatter_start_" + tag, sums, lands, _scatter_plan)
        scattered[l, grp] = (send_sems, recv_sems, srcs, lands, parts, from_sibling)
        return token

    def gate(l, grp, g):
        token = start_swap(l, grp, g)
        if len(swapping) > 1:
            token = start_scatter(after=token)
        return token[0, 0]

    def gates_for(l, points):
        return {grp: functools.partial(gate, l, grp) for grp in points}

    grads = [None] * DEPTH
    dh, grads[1] = _layer_bwd(lay, dh, layers[1], saved[1], t, gates=gates_for(1, ("mlp", "out", "in")))
    (grad_x, head), grads[0] = _layer_bwd(lay, dh, layers[0], saved[0], t, gates=gates_for(0, ("mlp", "out")),
                                          first=True)
    d_meta = head[PAD:ROW0]

    stack = lambda key: jnp.concatenate([grads[l][key] for l in range(DEPTH)], axis=0)
    b_forget_g = jnp.concatenate([grads[l]["b_forget_p"][:, :FOX_HEADS] for l in range(DEPTH)], axis=0)
    alpha_g = jnp.stack([grads[l]["w_alpha_p"][FOX_HEADS:FOX_HEADS + GLA_RANK] for l in range(DEPTH)])
    packed = _pack_small(d, d_meta, stack("norm_mix_g"), stack("gla_norm_g"), stack("norm_mlp_g"), dg_final,
                         stack("b_alpha"), b_forget_g, alpha_g)
    small_g = _unpack_small(d, _sum_devices(_all_gather_small(packed)))
    small_g["meta"] = lax.dynamic_slice_in_dim(small_g["meta"], chip * (d // N_CHIPS), d // N_CHIPS, axis=1)
    alpha_shard = lax.dynamic_slice_in_dim(small_g["w_alpha2"], chip * (lay.gk // N_CHIPS), lay.gk // N_CHIPS, axis=2)

    after = start_swap(0, "in", grads[0], after=small_g["final_norm_g"])
    while swapping:
        after = start_scatter(after)
    outs = {n: None for n in _BIG}

    def reduce_and_share(l, grps, after):
        tag = "%d_%s" % (l, grps[0])
        names, sums = [], []
        for grp in grps:
            send_sems, recv_sems, srcs, lands, parts, from_sibling = scattered[l, grp]
            _, lands = _split_wait("scatter_wait_%d_%s" % (l, grp), send_sems, recv_sems, srcs, lands, _scatter_plan,
                                 after)
            names += scatter_groups[grp]
            sums += [_chip_sum("chip_sum_%d_%s" % (l, n), p_, r_, landed, chip_idx, core_idx)
                     for n, p_, r_, landed in zip(scatter_groups[grp], parts, from_sibling, lands)]
        return l, tag, names, _exchange_start("share_start_" + tag, sums, _share_plan, after=sums[-1])

    def update(shared, after):
        l, tag, names, (send_sems, recv_sems, sums, _) = shared
        sums = _exchange_wait("share_wait_" + tag, send_sems, recv_sems, sums, _share_plan, after)
        for n, g in zip(names, sums):
            outs[n] = _adamw("adamw_%d_%s" % (l, n), w[n], g, m[n], v[n], layer=l, into=outs[n])
        return outs[names[-1]][0]

    shared_1 = reduce_and_share(1, ("mlp", "out", "in"), after)
    shared_0 = reduce_and_share(0, ("mlp", "out"), shared_1[3][3])
    after = update(shared_1, shared_0[3][3])
    shared_in = reduce_and_share(0, ("in",), after)
    after = update(shared_0, shared_in[3][3])
    update(shared_in, after)
    out_g, out_d, out_m, out_v = {}, {}, {}, {}
    for n in _BIG:
        out_g[n], out_d[n], out_m[n], out_v[n] = outs[n]
    out_g["w_alpha2"], out_d["w_alpha2"], out_m["w_alpha2"], out_v["w_alpha2"] = _adamw(
        "adamw_w_alpha2", w["w_alpha2"], alpha_shard, m["w_alpha2"], v["w_alpha2"])
    sm_w = dict(meta_tokens=meta_tokens, norm_mix_g=norm_mix_g, b_forget=b_forget, b_alpha=b_alpha,
                gla_norm_g=gla_norm_g, norm_mlp_g=norm_mlp_g, final_norm_g=final_norm_g)
    sm_m = dict(meta_tokens=m_meta_tokens, norm_mix_g=m_norm_mix_g, b_forget=m_b_forget, b_alpha=m_b_alpha,
                gla_norm_g=m_gla_norm_g, norm_mlp_g=m_norm_mlp_g, final_norm_g=m_final_norm_g)
    sm_v = dict(meta_tokens=v_meta_tokens, norm_mix_g=v_norm_mix_g, b_forget=v_b_forget, b_alpha=v_b_alpha,
                gla_norm_g=v_gla_norm_g, norm_mlp_g=v_norm_mlp_g, final_norm_g=v_final_norm_g)
    sm_g = dict(meta_tokens=small_g["meta"], norm_mix_g=small_g["norm_mix_g"], b_forget=small_g["b_forget"],
                b_alpha=small_g["b_alpha"], gla_norm_g=small_g["gla_norm_g"], norm_mlp_g=small_g["norm_mlp_g"],
                final_norm_g=small_g["final_norm_g"])
    names_small = list(sm_w)
    sizes = [sm_w[n].size for n in names_small]
    width = 512
    total = -(-sum(sizes) // (8 * width)) * (8 * width)

    def pack_flat(dct, fill):
        flat = jnp.concatenate([dct[n].reshape(-1) for n in names_small])
        return jnp.pad(flat, (0, total - flat.shape[0]), constant_values=fill).reshape(1, -1, width)

    res = _adamw("adamw_small", pack_flat(sm_w, 0.0), pack_flat(sm_g, 0.0), pack_flat(sm_m, 0.0), pack_flat(sm_v, 1.0))
    offs = [0]
    for sz in sizes:
        offs.append(offs[-1] + sz)
    for i, n in enumerate(names_small):
        out_g[n] = sm_g[n].reshape(sm_w[n].shape)
        out_d[n], out_m[n], out_v[n] = [r.reshape(-1)[offs[i]:offs[i + 1]].reshape(sm_w[n].shape) for r in res[1:]]

    order = ["meta_tokens", "norm_mix_g", "w_in", "b_forget", "w_alpha2", "b_alpha", "gla_norm_g", "w_o_fox",
             "w_o_gla", "w_out", "norm_mlp_g", "w_ff1", "w_ff2", "final_norm_g"]
    return (loss, grad_x[None], *[out_g[n] for n in order], *[out_d[n] for n in order],
            *[out_m[n] for n in order], *[out_v[n] for n in order])
```
